```python
import jax, jax.numpy as jnp
from jax import lax
import numpy as np

D_MODEL = 1024
BATCH = 8
SEQ = 4096
DEPTH = 1

POOL_WINDOWS = (2, 4, 8, 16)
POOL_GROUPS = len(POOL_WINDOWS)
POOL_WIDTH = D_MODEL // 4
POOL_GROUP_DIM = POOL_WIDTH // POOL_GROUPS
ATT_CONFIGS = ((128, 1), (512, 4), (2048, 16))
ATT_GROUPS = len(ATT_CONFIGS)
ATT_HEAD_DIM = 64
ATT_WIDTH = D_MODEL - POOL_WIDTH
ATT_HEADS = ATT_WIDTH // ATT_HEAD_DIM
ATT_HEADS_PER_GROUP = ATT_HEADS // ATT_GROUPS
ATT_OUT_WIDTH = ATT_HEADS_PER_GROUP * ATT_HEAD_DIM
ATT_BLOCK = 128
IN_PROJ_WIDTH = POOL_WIDTH + 3 * ATT_WIDTH
MIX_OUT_WIDTH = POOL_WIDTH + ATT_OUT_WIDTH
N_EXPERTS = 256
TOP_K = 8
N_EXPERT_GROUPS = 8
TOPK_GROUPS = 4
EXPERT_FF = D_MODEL // 4
SHARED_FF = D_MODEL // 4
ROUTED_SCALE = 2.5
MOE_BLOCK = 128
NORM_EPS = 1e-6

kernel_name = "hybrid_pool_dilatedattn_moe_adaln"


def rmsnorm(x, g):
    xf = x.astype(jnp.float32)
    y = xf * lax.rsqrt(jnp.mean(xf * xf, axis=-1, keepdims=True) + NORM_EPS)
    return (y * g.astype(jnp.float32)).astype(x.dtype)


def modulate(x, g, shift, scale):
    return rmsnorm(x, g) * (1 + scale[:, None, :]) + shift[:, None, :]


def pool_mixer(u, pool_w, pool_scale):
    B, S, _ = u.shape
    uf = u.astype(jnp.float32).reshape(B, S, POOL_GROUPS, POOL_GROUP_DIM)
    cs = jnp.cumsum(uf, axis=1)
    t = jnp.arange(S)
    outs = []
    for g, w in enumerate(POOL_WINDOWS):
        csg = cs[:, :, g]
        lag = jnp.pad(csg, ((0, 0), (w, 0), (0, 0)))[:, :S]
        cnt = jnp.minimum(t + 1, w).astype(jnp.float32)[None, :, None]
        outs.append((csg - lag) / cnt - uf[:, :, g])
    pooled = jnp.stack(outs, axis=2).astype(u.dtype)
    mixed = jnp.einsum('bsgc,gcd->bsgd', pooled, pool_w)
    return mixed.reshape(B, S, POOL_WIDTH) * pool_scale


def dilated_window_attention(q, k, v, window, dilation):
    B, S, H, E = q.shape
    steps = window // dilation
    span = dilation * ATT_BLOCK
    Sp = -(-S // span) * span
    L = Sp // dilation
    nb = L // ATT_BLOCK

    def to_blocks(a):
        a = jnp.pad(a, ((0, 0), (0, Sp - S), (0, 0), (0, 0)))
        a = a.reshape(B, L, dilation, H, E).transpose(0, 2, 1, 3, 4)
        return a.reshape(B, dilation, nb, ATT_BLOCK, H, E)

    def with_prev(a):
        prev = jnp.pad(a, ((0, 0), (0, 0), (1, 0), (0, 0), (0, 0), (0, 0)))[:, :, :nb]
        return jnp.concatenate([prev, a], axis=3)

    qb = to_blocks(q)
    kk = with_prev(to_blocks(k))
    vv = with_prev(to_blocks(v))
    s = jnp.einsum('brnqhe,brnkhe->brnhqk', qb, kk,
                   preferred_element_type=jnp.float32) * (E ** -0.5)
    i = jnp.arange(ATT_BLOCK)[:, None]
    j = jnp.arange(2 * ATT_BLOCK)[None, :]
    n = jnp.arange(nb)[:, None, None]
    dist = ATT_BLOCK + i - j
    valid = (dist >= 0) & (dist <= steps) & (n * ATT_BLOCK + j - ATT_BLOCK >= 0)
    s = jnp.where(valid[:, None], s, -jnp.inf)
    m = jnp.max(s, axis=-1, keepdims=True)
    p = jnp.exp(s - m)
    l = jnp.sum(p, axis=-1)
    o = jnp.einsum('brnhqk,brnkhe->brnqhe', p.astype(v.dtype), vv,
                   preferred_element_type=jnp.float32)
    o = o / jnp.swapaxes(l, 3, 4)[..., None]
    lse = jnp.swapaxes(m[..., 0] + jnp.log(l), 3, 4)

    def from_blocks(a):
        a = a.reshape((B, dilation, L) + a.shape[4:])
        a = jnp.swapaxes(a, 1, 2)
        return a.reshape((B, Sp) + a.shape[3:])[:, :S]

    return from_blocks(o), from_blocks(lse)


def attention_mixer(qkv):
    B, S, _ = qkv.shape
    qkv = qkv.reshape(B, S, 3, ATT_GROUPS, ATT_HEADS_PER_GROUP, ATT_HEAD_DIM)
    outs, lses = [], []
    for g, (w, d) in enumerate(ATT_CONFIGS):
        o, lse = dilated_window_attention(qkv[:, :, 0, g], qkv[:, :, 1, g], qkv[:, :, 2, g], w, d)
        outs.append(o)
        lses.append(lse)
    o = jnp.stack(outs, axis=0)
    alpha = jax.nn.softmax(jnp.stack(lses, axis=0), axis=0)
    out = jnp.sum(alpha[..., None] * o, axis=0)
    return out.reshape(B, S, ATT_OUT_WIDTH).astype(qkv.dtype)


def swiglu(h, w_gate, w_up, w_down):
    return (jax.nn.silu(h @ w_gate) * (h @ w_up)) @ w_down


def moe_ffn(h, w_router, router_bias, w_gate, w_up, w_down, ws_gate, ws_up, ws_down):
    B, S, D = h.shape
    N = B * S
    hf = h.reshape(N, D)
    scores = jax.nn.sigmoid((hf @ w_router).astype(jnp.float32))
    biased = scores + router_bias.astype(jnp.float32)
    grp = biased.reshape(N, N_EXPERT_GROUPS, N_EXPERTS // N_EXPERT_GROUPS)
    grp_score = jnp.sum(lax.top_k(grp, 2)[0], axis=-1)
    _, gidx = lax.top_k(grp_score, TOPK_GROUPS)
    gmask = jnp.sum(jax.nn.one_hot(gidx, N_EXPERT_GROUPS, dtype=jnp.float32), axis=1)
    emask = jnp.repeat(gmask, N_EXPERTS // N_EXPERT_GROUPS, axis=1) > 0
    _, eidx = lax.top_k(jnp.where(emask, biased, -jnp.inf), TOP_K)
    gates = jnp.take_along_axis(scores, eidx, axis=1)
    gates = gates / jnp.sum(gates, axis=-1, keepdims=True) * ROUTED_SCALE

    NK = N * TOP_K
    flat_e = eidx.reshape(NK)
    flat_tok = jnp.repeat(jnp.arange(N, dtype=jnp.int32), TOP_K)
    flat_w = gates.reshape(NK)
    order = jnp.argsort(flat_e, stable=True)
    se, stok, sw = flat_e[order], flat_tok[order], flat_w[order]
    counts = jnp.bincount(flat_e, length=N_EXPERTS)
    offs = jnp.cumsum(counts) - counts
    pcounts = (counts + MOE_BLOCK - 1) // MOE_BLOCK * MOE_BLOCK
    pend = jnp.cumsum(pcounts)
    poffs = pend - pcounts
    dest = poffs[se] + (jnp.arange(NK) - offs[se])
    P = NK + N_EXPERTS * MOE_BLOCK
    nblk = P // MOE_BLOCK
    row_tok = jnp.full((P,), N, jnp.int32).at[dest].set(stok)
    row_w = jnp.zeros((P,), jnp.float32).at[dest].set(sw)
    blk_e = jnp.minimum(jnp.searchsorted(pend, jnp.arange(nblk) * MOE_BLOCK, side='right'),
                        N_EXPERTS - 1).astype(jnp.int32)
    xpad = jnp.concatenate([hf, jnp.zeros((1, D), hf.dtype)], axis=0)

    def expert_block(args):
        e, toks, ws = args
        xb = xpad[toks]
        yb = swiglu(xb, w_gate[e], w_up[e], w_down[e])
        return yb * ws[:, None].astype(yb.dtype)

    ys = lax.map(expert_block, (blk_e, row_tok.reshape(nblk, MOE_BLOCK),
                                row_w.reshape(nblk, MOE_BLOCK)))
    routed = jax.ops.segment_sum(ys.reshape(P, D), row_tok, num_segments=N + 1)[:N]
    shared = swiglu(hf, ws_gate, ws_up, ws_down)
    return (routed + shared).reshape(B, S, D)


def setup_inputs(seed: int = 0) -> dict:
    key = jax.random.key(seed)
    ks = jax.random.split(key, 20)
    L, D, E, F = DEPTH, D_MODEL, N_EXPERTS, EXPERT_FF
    nrm = lambda k, shape, s: jax.random.normal(k, shape, jnp.float32) * s
    return {
        "x": nrm(ks[0], (BATCH, SEQ, D), 1.0),
        "c": nrm(ks[1], (BATCH, D), 1.0),
        "w_ada": nrm(ks[2], (L, D, 6 * D), 0.5 * D ** -0.5),
        "b_ada": nrm(ks[3], (L, 6 * D), 0.02),
        "g_mix": 1.0 + nrm(ks[4], (L, D), 0.05),
        "w_in": nrm(ks[5], (L, D, IN_PROJ_WIDTH), D ** -0.5),
        "pool_w": nrm(ks[6], (L, POOL_GROUPS, POOL_GROUP_DIM, POOL_GROUP_DIM), POOL_GROUP_DIM ** -0.5),
        "pool_scale": 1.0 + nrm(ks[7], (L, POOL_WIDTH), 0.1),
        "w_out": nrm(ks[8], (L, MIX_OUT_WIDTH, D), MIX_OUT_WIDTH ** -0.5),
        "g_ffn": 1.0 + nrm(ks[9], (L, D), 0.05),
        "w_router": nrm(ks[10], (L, D, E), D ** -0.5),
        "router_bias": nrm(ks[11], (L, E), 0.01),
        "w_gate": nrm(ks[12], (L, E, D, F), D ** -0.5),
        "w_up": nrm(ks[13], (L, E, D, F), D ** -0.5),
        "w_down": nrm(ks[14], (L, E, F, D), F ** -0.5),
        "ws_gate": nrm(ks[15], (L, D, SHARED_FF), D ** -0.5),
        "ws_up": nrm(ks[16], (L, D, SHARED_FF), D ** -0.5),
        "ws_down": nrm(ks[17], (L, SHARED_FF, D), SHARED_FF ** -0.5),
        "g_final": 1.0 + nrm(ks[18], (D,), 0.05),
    }


def reference(x, c, w_ada, b_ada, g_mix, w_in, pool_w, pool_scale, w_out, g_ffn,
              w_router, router_bias, w_gate, w_up, w_down, ws_gate, ws_up, ws_down, g_final):
    cs = jax.nn.silu(c.astype(jnp.float32))
    for l in range(DEPTH):
        mod = (cs @ w_ada[l].astype(jnp.float32) + b_ada[l].astype(jnp.float32)).astype(x.dtype)
        shift1, scale1, gate1, shift2, scale2, gate2 = jnp.split(mod, 6, axis=-1)
        h = modulate(x, g_mix[l], shift1, scale1)
        proj = h @ w_in[l]
        pool_out = pool_mixer(proj[..., :POOL_WIDTH], pool_w[l], pool_scale[l])
        attn_out = attention_mixer(proj[..., POOL_WIDTH:])
        mixed = jnp.concatenate([pool_out, attn_out], axis=-1) @ w_out[l]
        x = x + gate1[:, None, :] * mixed
        h = modulate(x, g_ffn[l], shift2, scale2)
        y = moe_ffn(h, w_router[l], router_bias[l], w_gate[l], w_up[l], w_down[l],
                    ws_gate[l], ws_up[l], ws_down[l])
        x = x + gate2[:, None, :] * y
    return rmsnorm(x, g_final)
```

```python
import functools

import jax
import jax.numpy as jnp
from jax import lax
from jax.experimental import pallas as pl
from jax.experimental.pallas import tpu as pltpu

F32 = jnp.float32
BF16 = jnp.bfloat16
I32 = jnp.int32

NORM_EPS = 1e-6
POOL_WINDOWS = (2, 4, 8, 16)
POOL_HALO = 16
ATT_DILATIONS = (1, 4, 16)
ATT_BLOCK = 128
ATT_HEADS_PER_GROUP = 4
ATT_HEAD_DIM = 64
ATT_GROUP_WIDTH = ATT_HEADS_PER_GROUP * ATT_HEAD_DIM
N_EXPERT_GROUPS = 8
TOPK_GROUPS = 4
TOP_K = 8
ROUTED_SCALE = 2.5

IN_TILE = 512
ATT_BLOCKS_PER_STEP = 8
MID_TILE = 512
ROUTE_TILE = 256
GMM_BLOCK = 256

NEG_INF = float("-inf")


def _sigmoid(v):
    return 1.0 / (1.0 + jnp.exp(-v))


def _rms(v):
    return v * lax.rsqrt(jnp.mean(v * v, axis=-1, keepdims=True) + NORM_EPS)


def _ada_kernel(c_ref, w_ref, b_ref, o_ref):
    c = c_ref[...]
    cs = c * _sigmoid(c)
    o_ref[...] = jnp.dot(cs, w_ref[...], preferred_element_type=F32,
                         precision=lax.Precision.HIGHEST) + b_ref[...]


def _ada(c, w_ada, b_ada):
    B, D = c.shape
    W = w_ada.shape[1]
    tn = 1024
    return pl.pallas_call(
        _ada_kernel,
        grid=(W // tn,),
        in_specs=[pl.BlockSpec((B, D), lambda j: (0, 0)),
                  pl.BlockSpec((D, tn), lambda j: (0, j)),
                  pl.BlockSpec((1, tn), lambda j: (0, j))],
        out_specs=pl.BlockSpec((B, tn), lambda j: (0, j)),
        out_shape=jax.ShapeDtypeStruct((B, W), F32),
        name="ada",
    )(c, w_ada, b_ada.reshape(1, W))


def _in_kernel(x_ref, g_ref, sh_ref, sc_ref, w_ref, pool_ref, q0_ref, q1_ref, q2_ref):
    h = _rms(x_ref[...]) * g_ref[...]
    h = h * (1.0 + sc_ref[0]) + sh_ref[0]
    hb = h.astype(BF16)
    pw = pool_ref.shape[1]
    gw = ATT_GROUP_WIDTH
    pool_ref[...] = jnp.dot(hb, w_ref[:, 0:pw], preferred_element_type=F32)
    for g, out in enumerate((q0_ref, q1_ref, q2_ref)):
        for sec in range(3):
            c0 = pw + sec * 3 * gw + g * gw
            out[:, sec * gw:(sec + 1) * gw] = jnp.dot(
                hb, w_ref[:, c0:c0 + gw], preferred_element_type=F32).astype(BF16)


def _in_proj(xf, g_mix, shift1, scale1, w_in_b, S, pool_width):
    N, D = xf.shape
    tm = IN_TILE
    spt = S // tm
    vec = lambda i: (i // spt, 0, 0)
    row = lambda i: (i, 0)
    gw3 = 3 * ATT_GROUP_WIDTH
    return pl.pallas_call(
        _in_kernel,
        grid=(N // tm,),
        in_specs=[pl.BlockSpec((tm, D), row),
                  pl.BlockSpec((1, D), lambda i: (0, 0)),
                  pl.BlockSpec((1, 1, D), vec),
                  pl.BlockSpec((1, 1, D), vec),
                  pl.BlockSpec(w_in_b.shape, lambda i: (0, 0))],
        out_specs=[pl.BlockSpec((tm, pool_width), row)] + [pl.BlockSpec((tm, gw3), row)] * 3,
        out_shape=[jax.ShapeDtypeStruct((N, pool_width), F32)] + [jax.ShapeDtypeStruct((N, gw3), BF16)] * 3,
        name="in_proj",
    )(xf, g_mix, shift1, scale1, w_in_b)


def _attn_kernel(nbs, a_ref, halo_ref, o_ref, lse_ref, kv_ref):
    i = pl.program_id(0)
    R = a_ref.shape[0] // ATT_BLOCK
    gw = ATT_GROUP_WIDTH
    blk = ATT_BLOCK
    nh = ATT_HEADS_PER_GROUP
    kv_ref[0:blk, :] = halo_ref[:, gw:3 * gw]
    kv_ref[blk:, :] = a_ref[:, gw:3 * gw]
    row = lax.broadcasted_iota(I32, (nh * blk, blk), 0) % blk
    col = lax.broadcasted_iota(I32, (nh * blk, blk), 1)
    head_of_lane = lax.broadcasted_iota(I32, (blk, gw), 1) // ATT_HEAD_DIM
    own_valid = col <= row
    nt = (((1,), (1,)), ((), ()))

    def body(jj, carry):
        r0 = pl.multiple_of(jj * blk, blk)
        r1 = pl.multiple_of(jj * blk + blk, blk)
        qf = a_ref[pl.ds(r0, blk), 0:gw].astype(F32)
        q4 = jnp.concatenate([jnp.where(head_of_lane == h, qf, 0.0) for h in range(nh)], axis=0).astype(BF16)
        kp = kv_ref[pl.ds(r0, blk), 0:gw]
        vp = kv_ref[pl.ds(r0, blk), gw:2 * gw]
        ko = kv_ref[pl.ds(r1, blk), 0:gw]
        vo = kv_ref[pl.ds(r1, blk), gw:2 * gw]
        scale = ATT_HEAD_DIM ** -0.5
        sp = lax.dot_general(q4, kp, nt, preferred_element_type=F32) * scale
        so = lax.dot_general(q4, ko, nt, preferred_element_type=F32) * scale
        first = ((i * R + jj) % nbs) == 0
        lo = row + jnp.where(first, blk, 0)
        sp = jnp.where(col >= lo, sp, NEG_INF)
        so = jnp.where(own_valid, so, NEG_INF)
        m = jnp.maximum(jnp.max(sp, axis=1, keepdims=True), jnp.max(so, axis=1, keepdims=True))
        pp = jnp.exp(sp - m)
        po = jnp.exp(so - m)
        l = jnp.sum(pp, axis=1, keepdims=True) + jnp.sum(po, axis=1, keepdims=True)
        o4 = (jnp.dot(pp.astype(BF16), vp, preferred_element_type=F32)
              + jnp.dot(po.astype(BF16), vo, preferred_element_type=F32)) / l
        lse4 = m + jnp.log(l)
        o = jnp.zeros((blk, gw), F32)
        lse = jnp.zeros((blk, gw), F32)
        for h in range(nh):
            hm = head_of_lane == h
            o = jnp.where(hm, o4[h * blk:(h + 1) * blk, :], o)
            lse = jnp.where(hm, lse4[h * blk:(h + 1) * blk, :], lse)
        o_ref[pl.ds(r0, blk), :] = o
        lse_ref[pl.ds(r0, blk), :] = lse
        return carry

    lax.fori_loop(0, R, body, 0)


def _attention(a, nbs):
    N = a.shape[0]
    R = ATT_BLOCKS_PER_STEP
    gw = ATT_GROUP_WIDTH
    tm = R * ATT_BLOCK
    return pl.pallas_call(
        functools.partial(_attn_kernel, nbs),
        grid=(N // tm,),
        in_specs=[pl.BlockSpec((tm, 3 * gw), lambda i: (i, 0)),
                  pl.BlockSpec((ATT_BLOCK, 3 * gw), lambda i: (jnp.maximum(i * R - 1, 0), 0))],
        out_specs=[pl.BlockSpec((tm, gw), lambda i: (i, 0))] * 2,
        out_shape=[jax.ShapeDtypeStruct((N, gw), F32)] * 2,
        scratch_shapes=[pltpu.VMEM((tm + ATT_BLOCK, 2 * gw), BF16)],
        name="attn",
    )(a, a)


def _mid_kernel(spt, u_ref, uh_ref, o0_ref, l0_ref, o1_ref, l1_ref, o2_ref, l2_ref, x_ref,
                pbd_ref, psc_ref, wout_ref, gate1_ref, gffn_ref, sh2_ref, sc2_ref, gate2_ref,
                wsg_ref, wsu_ref, wsd_ref, xacc_ref, h2_ref, ext_ref):
    i = pl.program_id(0)
    tm, pw = u_ref.shape
    si = i % spt
    u = u_ref[...]
    keep = jnp.full((POOL_HALO, pw), si, I32) > 0
    ext_ref[0:POOL_HALO, :] = jnp.where(keep, uh_ref[...], 0.0)
    ext_ref[POOL_HALO:, :] = u
    lane_grp = lax.broadcasted_iota(I32, (tm, pw), 1) // (pw // len(POOL_WINDOWS))
    s = u
    pooled = jnp.zeros((tm, pw), F32)
    for j in range(1, POOL_HALO):
        s = s + ext_ref[pl.ds(POOL_HALO - j, tm), :]
        if (j + 1) in POOL_WINDOWS:
            pooled = jnp.where(lane_grp == POOL_WINDOWS.index(j + 1), s, pooled)
    win = jnp.zeros((tm, pw), I32)
    for g, w in enumerate(POOL_WINDOWS):
        win = jnp.where(lane_grp == g, w, win)
    pos = si * tm + lax.broadcasted_iota(I32, (tm, pw), 0)
    cnt = jnp.minimum(pos + 1, win).astype(F32)
    pooled = pooled / cnt - u
    pool_out = jnp.dot(pooled.astype(BF16), pbd_ref[...], preferred_element_type=F32) * psc_ref[...]
    l0 = l0_ref[...]
    l1 = l1_ref[...]
    l2 = l2_ref[...]
    m = jnp.maximum(jnp.maximum(l0, l1), l2)
    w0 = jnp.exp(l0 - m)
    w1 = jnp.exp(l1 - m)
    w2 = jnp.exp(l2 - m)
    attn = (w0 * o0_ref[...] + w1 * o1_ref[...] + w2 * o2_ref[...]) / (w0 + w1 + w2)
    mixed = (jnp.dot(pool_out.astype(BF16), wout_ref[0:pw, :], preferred_element_type=F32)
             + jnp.dot(attn.astype(BF16), wout_ref[pw:, :], preferred_element_type=F32))
    x1 = x_ref[...] + gate1_ref[0] * mixed
    h2 = _rms(x1) * gffn_ref[...]
    h2 = h2 * (1.0 + sc2_ref[0]) + sh2_ref[0]
    h2_ref[...] = h2
    hb = h2.astype(BF16)
    a = jnp.dot(hb, wsg_ref[...], preferred_element_type=F32)
    b = jnp.dot(hb, wsu_ref[...], preferred_element_type=F32)
    act = (a * _sigmoid(a)) * b
    shared = jnp.dot(act.astype(BF16), wsd_ref[...], preferred_element_type=F32)
    xacc_ref[...] = x1 + gate2_ref[0] * shared


def _mid(u, attn_outs, xf, pool_bd, pool_scale, w_out_b, gate1, g_ffn, shift2, scale2, gate2,
         wsg_b, wsu_b, wsd_b, S):
    N, D = xf.shape
    pw = u.shape[1]
    tm = MID_TILE
    spt = S // tm
    row = lambda i: (i, 0)
    vec = lambda i: (i // spt, 0, 0)
    full = lambda a: pl.BlockSpec(a.shape, lambda i: (0,) * a.ndim)
    hpt = tm // POOL_HALO
    in_specs = [pl.BlockSpec((tm, pw), row),
                pl.BlockSpec((POOL_HALO, pw), lambda i: (jnp.maximum(i * hpt - 1, 0), 0))]
    in_specs += [pl.BlockSpec((tm, ATT_GROUP_WIDTH), row)] * 6
    in_specs += [pl.BlockSpec((tm, D), row), full(pool_bd), full(pool_scale), full(w_out_b),
                 pl.BlockSpec((1, 1, D), vec), full(g_ffn), pl.BlockSpec((1, 1, D), vec),
                 pl.BlockSpec((1, 1, D), vec), pl.BlockSpec((1, 1, D), vec),
                 full(wsg_b), full(wsu_b), full(wsd_b)]
    return pl.pallas_call(
        functools.partial(_mid_kernel, spt),
        grid=(N // tm,),
        in_specs=in_specs,
        out_specs=[pl.BlockSpec((tm, D), row)] * 2,
        out_shape=[jax.ShapeDtypeStruct((N, D), F32)] * 2,
        scratch_shapes=[pltpu.VMEM((tm + POOL_HALO, pw), F32)],
        name="mid",
    )(u, u, *attn_outs, xf, pool_bd, pool_scale, w_out_b, gate1, g_ffn, shift2, scale2, gate2,
      wsg_b, wsu_b, wsd_b)


def _route_kernel(h_ref, wr_ref, bias_ref, e_ref, g_ref, r_ref, cnt_ref, base_ref):
    i = pl.program_id(0)
    T = h_ref.shape[0]
    E = wr_ref.shape[0]
    gsz = E // N_EXPERT_GROUPS

    @pl.when(i == 0)
    def _():
        base_ref[...] = jnp.zeros_like(base_ref)

    hb = h_ref[...].astype(BF16)
    logits = lax.dot_general(wr_ref[...], hb, (((1,), (1,)), ((), ())), preferred_element_type=F32)
    scores = _sigmoid(logits)
    biased = scores + bias_ref[...]
    giota = lax.broadcasted_iota(I32, (gsz, T), 0)
    gscore = []
    for g in range(N_EXPERT_GROUPS):
        blk = biased[g * gsz:(g + 1) * gsz, :]
        m1 = jnp.max(blk, axis=0, keepdims=True)
        i1 = jnp.min(jnp.where(blk == m1, giota, gsz), axis=0, keepdims=True)
        m2 = jnp.max(jnp.where(giota == i1, NEG_INF, blk), axis=0, keepdims=True)
        gscore.append(m1 + m2)
    parts = []
    for g in range(N_EXPERT_GROUPS):
        beaten = jnp.zeros((1, T), I32)
        for o in range(N_EXPERT_GROUPS):
            if o == g:
                continue
            wins = (gscore[o] >= gscore[g]) if o < g else (gscore[o] > gscore[g])
            beaten = beaten + wins.astype(I32)
        keep = jnp.broadcast_to(beaten, (gsz, T)) < TOPK_GROUPS
        parts.append(jnp.where(keep, biased[g * gsz:(g + 1) * gsz, :], NEG_INF))
    cur = jnp.concatenate(parts, axis=0)
    eiota = lax.broadcasted_iota(I32, (E, T), 0)
    selm = jnp.zeros((E, T), F32)
    idxs, gates = [], []
    for k in range(TOP_K):
        m = jnp.max(cur, axis=0, keepdims=True)
        idx = jnp.min(jnp.where(cur == m, eiota, E), axis=0, keepdims=True)
        oh = eiota == idx
        gates.append(jnp.sum(jnp.where(oh, scores, 0.0), axis=0, keepdims=True))
        idxs.append(idx)
        cur = jnp.where(oh, NEG_INF, cur)
        selm = jnp.where(oh, 1.0, selm)
    gsum = gates[0]
    for k in range(1, TOP_K):
        gsum = gsum + gates[k]
    before = (lax.broadcasted_iota(I32, (T, T), 0) < lax.broadcasted_iota(I32, (T, T), 1)).astype(BF16)
    tot = jnp.dot(selm.astype(BF16), before, preferred_element_type=F32) + base_ref[...]
    for k in range(TOP_K):
        e_ref[k:k + 1, :] = idxs[k]
        g_ref[k:k + 1, :] = gates[k] / gsum * ROUTED_SCALE
        r_ref[k:k + 1, :] = jnp.sum(jnp.where(eiota == idxs[k], tot, 0.0), axis=0, keepdims=True).astype(I32)
    base_ref[...] = base_ref[...] + jnp.sum(selm, axis=1, keepdims=True)
    cnt_ref[...] = base_ref[...].astype(I32)


def _route(h2, wr_t, bias_col):
    N, D = h2.shape
    E = wr_t.shape[0]
    T = ROUTE_TILE
    col = lambda i: (0, i)
    return pl.pallas_call(
        _route_kernel,
        grid=(N // T,),
        in_specs=[pl.BlockSpec((T, D), lambda i: (i, 0)),
                  pl.BlockSpec((E, D), lambda i: (0, 0)),
                  pl.BlockSpec((E, 1), lambda i: (0, 0))],
        out_specs=[pl.BlockSpec((TOP_K, T), col), pl.BlockSpec((TOP_K, T), col),
                   pl.BlockSpec((TOP_K, T), col), pl.BlockSpec((E, 1), lambda i: (0, 0))],
        out_shape=[jax.ShapeDtypeStruct((TOP_K, N), I32), jax.ShapeDtypeStruct((TOP_K, N), F32),
                   jax.ShapeDtypeStruct((TOP_K, N), I32), jax.ShapeDtypeStruct((E, 1), I32)],
        scratch_shapes=[pltpu.VMEM((E, 1), F32)],
        compiler_params=pltpu.CompilerParams(dimension_semantics=("arbitrary",)),
        name="route",
    )(h2, wr_t, bias_col)


def _dest_kernel(e_ref, r_ref, off_ref, d_ref):
    E = off_ref.shape[0]
    T = e_ref.shape[1]
    eiota = lax.broadcasted_iota(I32, (E, T), 0)
    off = off_ref[...]
    for k in range(TOP_K):
        start = jnp.sum(jnp.where(eiota == e_ref[k:k + 1, :], off, 0.0), axis=0, keepdims=True)
        d_ref[0, k:k + 1, :] = start.astype(I32) + r_ref[k:k + 1, :]


def _dest(eidx, rank, offs_col):
    N = eidx.shape[1]
    E = offs_col.shape[0]
    T = ROUTE_TILE
    col = lambda i: (0, i)
    return pl.pallas_call(
        _dest_kernel,
        grid=(N // T,),
        in_specs=[pl.BlockSpec((TOP_K, T), col), pl.BlockSpec((TOP_K, T), col),
                  pl.BlockSpec((E, 1), lambda i: (0, 0))],
        out_specs=pl.BlockSpec((1, TOP_K, T), lambda i: (i, 0, 0)),
        out_shape=jax.ShapeDtypeStruct((N // T, TOP_K, T), I32),
        name="dest",
    )(eidx, rank, offs_col)


def _dispatch_kernel(steps, cnt_ref, off_ref, nbe_ref, nu_ref, d_ref, h_ref, xs_ref, zero_ref, sem, zsem):
    i = pl.program_id(0)
    T = h_ref.shape[0]
    E = cnt_ref.shape[0]
    bm = zero_ref.shape[0]
    nblk = xs_ref.shape[0] // bm

    @pl.when(i == 0)
    def _():
        zero_ref[...] = jnp.zeros_like(zero_ref)

    def body(t, carry):
        for k in range(TOP_K):
            pltpu.make_async_copy(h_ref.at[pl.ds(t, 1), :], xs_ref.at[pl.ds(d_ref[0, k, t], 1), :], sem).start()
        return carry

    lax.fori_loop(0, T, body, 0)

    def pad_copy(row):
        return pltpu.make_async_copy(zero_ref.at[pl.ds(0, 1), :], xs_ref.at[pl.ds(row, 1), :], zsem)

    def tail_copy(b):
        return pltpu.make_async_copy(zero_ref, xs_ref.at[pl.ds(pl.multiple_of(b * bm, bm), bm), :], zsem)

    e_per = -(-E // steps)
    n_tail = (jnp.maximum(nblk - nu_ref[0] - i, 0) + steps - 1) // steps

    def per_expert(fn):
        def ebody(j, carry):
            e = i * e_per + j

            @pl.when(e < E)
            def _():
                lax.fori_loop(cnt_ref[e], nbe_ref[e] * bm, lambda r, c: (fn(off_ref[e] + r), c)[1], 0)
            return carry
        lax.fori_loop(0, e_per, ebody, 0)

    per_expert(lambda row: pad_copy(row).start())
    lax.fori_loop(0, n_tail, lambda j, c: (tail_copy(nu_ref[0] + i + j * steps).start(), c)[1], 0)
    per_expert(lambda row: pad_copy(row).wait())
    lax.fori_loop(0, n_tail, lambda j, c: (tail_copy(nu_ref[0] + i + j * steps).wait(), c)[1], 0)
    for k in range(TOP_K):
        pltpu.make_async_copy(h_ref, xs_ref.at[pl.ds(0, T), :], sem).wait()


def _dispatch(counts, offs, nb_e, nused, dest, h2, P):
    N, D = h2.shape
    T = ROUTE_TILE
    grid_spec = pltpu.PrefetchScalarGridSpec(
        num_scalar_prefetch=4,
        grid=(N // T,),
        in_specs=[pl.BlockSpec((1, TOP_K, T), lambda i, *_: (i, 0, 0), memory_space=pltpu.SMEM),
                  pl.BlockSpec((T, D), lambda i, *_: (i, 0))],
        out_specs=pl.BlockSpec(memory_space=pl.ANY),
        scratch_shapes=[pltpu.VMEM((GMM_BLOCK, D), F32), pltpu.SemaphoreType.DMA(()),
                        pltpu.SemaphoreType.DMA(())],
    )
    return pl.pallas_call(
        functools.partial(_dispatch_kernel, N // T),
        grid_spec=grid_spec,
        out_shape=jax.ShapeDtypeStruct((P, D), F32),
        compiler_params=pltpu.CompilerParams(dimension_semantics=("arbitrary",)),
        name="dispatch",
    )(counts, offs, nb_e, nused, dest, h2)


def _gmm_kernel(be_ref, nu_ref, xs_ref, wg_ref, wu_ref, wd_ref, ys_ref, wgb, wub, wdb):
    i = pl.program_id(0)
    e = be_ref[i]
    prev = be_ref[jnp.maximum(i - 1, 0)]

    @pl.when((i == 0) | (e != prev))
    def _():
        wgb[...] = wg_ref[0].astype(BF16)
        wub[...] = wu_ref[0].astype(BF16)
        wdb[...] = wd_ref[0].astype(BF16)

    @pl.when(i < nu_ref[0])
    def _():
        xb = xs_ref[...].astype(BF16)
        a = jnp.dot(xb, wgb[...], preferred_element_type=F32)
        b = jnp.dot(xb, wub[...], preferred_element_type=F32)
        act = (a * _sigmoid(a)) * b
        ys_ref[...] = jnp.dot(act.astype(BF16), wdb[...], preferred_element_type=F32)

    @pl.when(i >= nu_ref[0])
    def _():
        ys_ref[...] = jnp.zeros_like(ys_ref)


def _gmm(blk_e, nused, xs, w_gate, w_up, w_down):
    P, D = xs.shape
    E, _, F = w_gate.shape
    bm = GMM_BLOCK
    rows = lambda i, be, nu: (i, 0)
    wsel = lambda i, be, nu: (be[i], 0, 0)
    grid_spec = pltpu.PrefetchScalarGridSpec(
        num_scalar_prefetch=2,
        grid=(P // bm,),
        in_specs=[pl.BlockSpec((bm, D), rows),
                  pl.BlockSpec((1, D, F), wsel), pl.BlockSpec((1, D, F), wsel), pl.BlockSpec((1, F, D), wsel)],
        out_specs=pl.BlockSpec((bm, D), rows),
        scratch_shapes=[pltpu.VMEM((D, F), BF16), pltpu.VMEM((D, F), BF16), pltpu.VMEM((F, D), BF16)],
    )
    return pl.pallas_call(
        _gmm_kernel,
        grid_spec=grid_spec,
        out_shape=jax.ShapeDtypeStruct((P, D), F32),
        compiler_params=pltpu.CompilerParams(dimension_semantics=("arbitrary",)),
        name="gmm",
    )(blk_e, nused, xs, w_gate, w_up, w_down)


def _combine_kernel(d_ref, gt_ref, gate2_ref, xacc_ref, gfin_ref, ys_ref, o_ref, buf, sem):
    T = xacc_ref.shape[0]

    def body(t, carry):
        for k in range(TOP_K):
            pltpu.make_async_copy(ys_ref.at[pl.ds(d_ref[0, k, t], 1), :], buf.at[k, pl.ds(t, 1), :], sem).start()
        return carry

    lax.fori_loop(0, T, body, 0)
    for k in range(TOP_K):
        pltpu.make_async_copy(ys_ref.at[pl.ds(0, T), :], buf.at[k], sem).wait()
    gt = gt_ref[...]
    routed = buf[0] * gt[:, 0:1]
    for k in range(1, TOP_K):
        routed = routed + buf[k] * gt[:, k:k + 1]
    x2 = xacc_ref[...] + gate2_ref[0] * routed
    o_ref[...] = _rms(x2) * gfin_ref[...]


def _combine(dest, gates_t, gate2, xacc, g_final, ys, S):
    N, D = xacc.shape
    T = ROUTE_TILE
    spt = S // T
    return pl.pallas_call(
        _combine_kernel,
        grid=(N // T,),
        in_specs=[pl.BlockSpec((1, TOP_K, T), lambda i: (i, 0, 0), memory_space=pltpu.SMEM),
                  pl.BlockSpec((T, TOP_K), lambda i: (i, 0)),
                  pl.BlockSpec((1, 1, D), lambda i: (i // spt, 0, 0)),
                  pl.BlockSpec((T, D), lambda i: (i, 0)),
                  pl.BlockSpec((1, D), lambda i: (0, 0)),
                  pl.BlockSpec(memory_space=pl.ANY)],
        out_specs=pl.BlockSpec((T, D), lambda i: (i, 0)),
        out_shape=jax.ShapeDtypeStruct((N, D), F32),
        scratch_shapes=[pltpu.VMEM((TOP_K, T, D), F32), pltpu.SemaphoreType.DMA(())],
        compiler_params=pltpu.CompilerParams(dimension_semantics=("arbitrary",)),
        name="combine",
    )(dest, gates_t, gate2, xacc, g_final, ys)


def _to_residue_major(a, B, S, d):
    if d == 1:
        return a
    w = a.shape[-1]
    return a.reshape(B, S // d, d, w).transpose(0, 2, 1, 3).reshape(B * S, w)


def _from_residue_major(a, B, S, d):
    if d == 1:
        return a
    w = a.shape[-1]
    return a.reshape(B, d, S // d, w).transpose(0, 2, 1, 3).reshape(B * S, w)


def _layer(xf, B, S, mod, g_mix, w_in, pool_w, pool_scale, w_out, g_ffn, w_router, router_bias,
           w_gate, w_up, w_down, ws_gate, ws_up, ws_down):
    N, D = xf.shape
    E = w_router.shape[1]
    pw = pool_scale.shape[0]
    shift1, scale1, gate1, shift2, scale2, gate2 = [m.reshape(B, 1, D) for m in jnp.split(mod, 6, axis=-1)]
    u, q0, q1, q2 = _in_proj(xf, g_mix.reshape(1, D), shift1, scale1, w_in.astype(BF16), S, pw)
    attn_outs = []
    for a, d in zip((q0, q1, q2), ATT_DILATIONS):
        o, lse = _attention(_to_residue_major(a, B, S, d), S // d // ATT_BLOCK)
        attn_outs += [_from_residue_major(o, B, S, d), _from_residue_major(lse, B, S, d)]
    ng = pool_w.shape[0]
    pool_bd = jnp.einsum('gcd,gh->gchd', pool_w, jnp.eye(ng, dtype=pool_w.dtype)).reshape(pw, pw).astype(BF16)
    xacc, h2 = _mid(u, attn_outs, xf, pool_bd, pool_scale.reshape(1, pw), w_out.astype(BF16), gate1,
                    g_ffn.reshape(1, D), shift2, scale2, gate2,
                    ws_gate.astype(BF16), ws_up.astype(BF16), ws_down.astype(BF16), S)
    eidx, gates, rank, counts = _route(h2, w_router.T.astype(BF16), router_bias.reshape(E, 1).astype(F32))
    bm = GMM_BLOCK
    nblk = N * TOP_K // bm + E
    nb_e = (counts[:, 0] + bm - 1) // bm
    bend = jnp.cumsum(nb_e)
    offs = ((bend - nb_e) * bm).astype(I32)
    nused = bend[-1:].astype(I32)
    blk_e = jnp.minimum(jnp.searchsorted(bend, jnp.arange(nblk, dtype=I32), side='right'), E - 1).astype(I32)
    dest = _dest(eidx, rank, offs.astype(F32).reshape(E, 1))
    xs = _dispatch(counts[:, 0], offs, nb_e.astype(I32), nused, dest, h2, nblk * bm)
    ys = _gmm(blk_e, nused, xs, w_gate, w_up, w_down)
    return dest, gates.T, gate2, xacc, ys


def kernel(x, c, w_ada, b_ada, g_mix, w_in, pool_w, pool_scale, w_out, g_ffn, w_router, router_bias,
           w_gate, w_up, w_down, ws_gate, ws_up, ws_down, g_final):
    B, S, D = x.shape
    depth = w_ada.shape[0]
    assert depth == 1, "the final residual is fused with the final norm, so exactly one layer is supported"
    assert S % (ATT_DILATIONS[-1] * ATT_BLOCK) == 0 and S % max(IN_TILE, MID_TILE, ATT_BLOCKS_PER_STEP * ATT_BLOCK) == 0
    xf = x.reshape(B * S, D)
    mod = _ada(c, w_ada[0], b_ada[0])
    dest, gates_t, gate2, xacc, ys = _layer(
        xf, B, S, mod, g_mix[0], w_in[0], pool_w[0], pool_scale[0], w_out[0], g_ffn[0], w_router[0],
        router_bias[0], w_gate[0], w_up[0], w_down[0], ws_gate[0], ws_up[0], ws_down[0])
    out = _combine(dest, gates_t, gate2, xacc, g_final.reshape(1, D), ys, S)
    return out.reshape(B, S, D)
```

```python
import functools

import jax
import jax.numpy as jnp
from jax import lax
from jax.experimental import pallas as pl
from jax.experimental.pallas import tpu as pltpu

F32 = jnp.float32
BF16 = jnp.bfloat16
I32 = jnp.int32
U32 = jnp.uint32

NORM_EPS = 1e-6
POOL_WINDOWS = (2, 4, 8, 16)
POOL_HALO = 16
ATT_DILATIONS = (1, 4, 16)
ATT_BLOCK = 128
ATT_HEADS_PER_GROUP = 4
ATT_HEAD_DIM = 64
ATT_GROUP_WIDTH = ATT_HEADS_PER_GROUP * ATT_HEAD_DIM
N_EXPERT_GROUPS = 8
TOPK_GROUPS = 4
TOP_K = 8
ROUTED_SCALE = 2.5

IN_TILE = 512
ATT_BLOCKS_PER_STEP = 8
MID_TILE = 512
ROUTE_TILE = 256
GMM_BLOCK = 256

NEG_INF = float("-inf")


def _sigmoid(v):
    return 1.0 / (1.0 + jnp.exp(-v))


def _rms(v):
    return v * lax.rsqrt(jnp.mean(v * v, axis=-1, keepdims=True) + NORM_EPS)


def _pack_bf16_pairs(v):
    n = v.shape[1] // 2
    lo = lax.bitcast_convert_type(v[:, :n].astype(BF16).astype(F32), U32)
    hi = lax.bitcast_convert_type(v[:, n:].astype(BF16).astype(F32), U32)
    return (hi & jnp.uint32(0xFFFF0000)) | (lo >> 16)


def _unpack_bf16_pairs(p):
    lo = lax.bitcast_convert_type(p << 16, F32).astype(BF16)
    hi = lax.bitcast_convert_type(p & jnp.uint32(0xFFFF0000), F32).astype(BF16)
    return lo, hi


def _ada_kernel(c_ref, w_ref, b_ref, o_ref):
    c = c_ref[...]
    cs = c * _sigmoid(c)
    o_ref[...] = jnp.dot(cs, w_ref[...], preferred_element_type=F32,
                         precision=lax.Precision.HIGHEST) + b_ref[...]


def _ada(c, w_ada, b_ada):
    B, D = c.shape
    W = w_ada.shape[1]
    tn = 1024
    return pl.pallas_call(
        _ada_kernel,
        grid=(W // tn,),
        in_specs=[pl.BlockSpec((B, D), lambda j: (0, 0)),
                  pl.BlockSpec((D, tn), lambda j: (0, j)),
                  pl.BlockSpec((1, tn), lambda j: (0, j))],
        out_specs=pl.BlockSpec((B, tn), lambda j: (0, j)),
        out_shape=jax.ShapeDtypeStruct((B, W), F32),
        name="ada",
    )(c, w_ada, b_ada.reshape(1, W))


def _in_kernel(x_ref, g_ref, sh_ref, sc_ref, w_ref, pool_ref, q0_ref, q1_ref, q2_ref):
    h = _rms(x_ref[...]) * g_ref[...]
    h = h * (1.0 + sc_ref[0]) + sh_ref[0]
    hb = h.astype(BF16)
    pw = pool_ref.shape[1]
    gw = ATT_GROUP_WIDTH
    pool_ref[...] = jnp.dot(hb, w_ref[:, 0:pw], preferred_element_type=F32)
    for g, out in enumerate((q0_ref, q1_ref, q2_ref)):
        for sec in range(3):
            c0 = pw + sec * 3 * gw + g * gw
            out[:, sec * gw:(sec + 1) * gw] = jnp.dot(
                hb, w_ref[:, c0:c0 + gw], preferred_element_type=F32).astype(BF16)


def _in_proj(xf, g_mix, shift1, scale1, w_in_b, S, pool_width):
    N, D = xf.shape
    tm = IN_TILE
    spt = S // tm
    vec = lambda i: (i // spt, 0, 0)
    row = lambda i: (i, 0)
    gw3 = 3 * ATT_GROUP_WIDTH
    return pl.pallas_call(
        _in_kernel,
        grid=(N // tm,),
        in_specs=[pl.BlockSpec((tm, D), row),
                  pl.BlockSpec((1, D), lambda i: (0, 0)),
                  pl.BlockSpec((1, 1, D), vec),
                  pl.BlockSpec((1, 1, D), vec),
                  pl.BlockSpec(w_in_b.shape, lambda i: (0, 0))],
        out_specs=[pl.BlockSpec((tm, pool_width), row)] + [pl.BlockSpec((tm, gw3), row)] * 3,
        out_shape=[jax.ShapeDtypeStruct((N, pool_width), F32)] + [jax.ShapeDtypeStruct((N, gw3), BF16)] * 3,
        name="in_proj",
    )(xf, g_mix, shift1, scale1, w_in_b)


def _attn_kernel(nbs, a_ref, halo_ref, o_ref, lse_ref, kv_ref):
    i = pl.program_id(0)
    R = a_ref.shape[0] // ATT_BLOCK
    gw = ATT_GROUP_WIDTH
    blk = ATT_BLOCK
    nh = ATT_HEADS_PER_GROUP
    kv_ref[0:blk, :] = halo_ref[:, gw:3 * gw]
    kv_ref[blk:, :] = a_ref[:, gw:3 * gw]
    row = lax.broadcasted_iota(I32, (nh * blk, blk), 0) % blk
    col = lax.broadcasted_iota(I32, (nh * blk, blk), 1)
    head_of_lane = lax.broadcasted_iota(I32, (blk, gw), 1) // ATT_HEAD_DIM
    own_valid = col <= row
    nt = (((1,), (1,)), ((), ()))

    def body(jj, carry):
        r0 = pl.multiple_of(jj * blk, blk)
        r1 = pl.multiple_of(jj * blk + blk, blk)
        qf = a_ref[pl.ds(r0, blk), 0:gw].astype(F32)
        q4 = jnp.concatenate([jnp.where(head_of_lane == h, qf, 0.0) for h in range(nh)], axis=0).astype(BF16)
        kp = kv_ref[pl.ds(r0, blk), 0:gw]
        vp = kv_ref[pl.ds(r0, blk), gw:2 * gw]
        ko = kv_ref[pl.ds(r1, blk), 0:gw]
        vo = kv_ref[pl.ds(r1, blk), gw:2 * gw]
        scale = ATT_HEAD_DIM ** -0.5
        sp = lax.dot_general(q4, kp, nt, preferred_element_type=F32) * scale
        so = lax.dot_general(q4, ko, nt, preferred_element_type=F32) * scale
        first = ((i * R + jj) % nbs) == 0
        lo = row + jnp.where(first, blk, 0)
        sp = jnp.where(col >= lo, sp, NEG_INF)
        so = jnp.where(own_valid, so, NEG_INF)
        m = jnp.maximum(jnp.max(sp, axis=1, keepdims=True), jnp.max(so, axis=1, keepdims=True))
        pp = jnp.exp(sp - m)
        po = jnp.exp(so - m)
        l = jnp.sum(pp, axis=1, keepdims=True) + jnp.sum(po, axis=1, keepdims=True)
        o4 = (jnp.dot(pp.astype(BF16), vp, preferred_element_type=F32)
              + jnp.dot(po.astype(BF16), vo, preferred_element_type=F32)) / l
        lse4 = m + jnp.log(l)
        o = jnp.zeros((blk, gw), F32)
        lse = jnp.zeros((blk, gw), F32)
        for h in range(nh):
            hm = head_of_lane == h
            o = jnp.where(hm, o4[h * blk:(h + 1) * blk, :], o)
            lse = jnp.where(hm, lse4[h * blk:(h + 1) * blk, :], lse)
        o_ref[pl.ds(r0, blk), :] = o
        lse_ref[pl.ds(r0, blk), :] = lse
        return carry

    lax.fori_loop(0, R, body, 0)


def _attention(a, nbs):
    N = a.shape[0]
    R = ATT_BLOCKS_PER_STEP
    gw = ATT_GROUP_WIDTH
    tm = R * ATT_BLOCK
    return pl.pallas_call(
        functools.partial(_attn_kernel, nbs),
        grid=(N // tm,),
        in_specs=[pl.BlockSpec((tm, 3 * gw), lambda i: (i, 0)),
                  pl.BlockSpec((ATT_BLOCK, 3 * gw), lambda i: (jnp.maximum(i * R - 1, 0), 0))],
        out_specs=[pl.BlockSpec((tm, gw), lambda i: (i, 0))] * 2,
        out_shape=[jax.ShapeDtypeStruct((N, gw), F32)] * 2,
        scratch_shapes=[pltpu.VMEM((tm + ATT_BLOCK, 2 * gw), BF16)],
        name="attn",
    )(a, a)


def _mid_kernel(spt, u_ref, uh_ref, o0_ref, l0_ref, o1_ref, l1_ref, o2_ref, l2_ref, x_ref,
                pbd_ref, psc_ref, wout_ref, gate1_ref, gffn_ref, sh2_ref, sc2_ref, gate2_ref,
                wsg_ref, wsu_ref, wsd_ref, xacc_ref, h2_ref, ext_ref):
    i = pl.program_id(0)
    tm, pw = u_ref.shape
    si = i % spt
    u = u_ref[...]
    keep = jnp.full((POOL_HALO, pw), si, I32) > 0
    ext_ref[0:POOL_HALO, :] = jnp.where(keep, uh_ref[...], 0.0)
    ext_ref[POOL_HALO:, :] = u
    lane_grp = lax.broadcasted_iota(I32, (tm, pw), 1) // (pw // len(POOL_WINDOWS))
    s = u
    pooled = jnp.zeros((tm, pw), F32)
    for j in range(1, POOL_HALO):
        s = s + ext_ref[pl.ds(POOL_HALO - j, tm), :]
        if (j + 1) in POOL_WINDOWS:
            pooled = jnp.where(lane_grp == POOL_WINDOWS.index(j + 1), s, pooled)
    win = jnp.zeros((tm, pw), I32)
    for g, w in enumerate(POOL_WINDOWS):
        win = jnp.where(lane_grp == g, w, win)
    pos = si * tm + lax.broadcasted_iota(I32, (tm, pw), 0)
    cnt = jnp.minimum(pos + 1, win).astype(F32)
    pooled = pooled / cnt - u
    pool_out = jnp.dot(pooled.astype(BF16), pbd_ref[...], preferred_element_type=F32) * psc_ref[...]
    l0 = l0_ref[...]
    l1 = l1_ref[...]
    l2 = l2_ref[...]
    m = jnp.maximum(jnp.maximum(l0, l1), l2)
    w0 = jnp.exp(l0 - m)
    w1 = jnp.exp(l1 - m)
    w2 = jnp.exp(l2 - m)
    attn = (w0 * o0_ref[...] + w1 * o1_ref[...] + w2 * o2_ref[...]) / (w0 + w1 + w2)
    mixed = (jnp.dot(pool_out.astype(BF16), wout_ref[0:pw, :], preferred_element_type=F32)
             + jnp.dot(attn.astype(BF16), wout_ref[pw:, :], preferred_element_type=F32))
    x1 = x_ref[...] + gate1_ref[0] * mixed
    h2 = _rms(x1) * gffn_ref[...]
    h2 = h2 * (1.0 + sc2_ref[0]) + sh2_ref[0]
    h2_ref[...] = _pack_bf16_pairs(h2)
    hb = h2.astype(BF16)
    a = jnp.dot(hb, wsg_ref[...], preferred_element_type=F32)
    b = jnp.dot(hb, wsu_ref[...], preferred_element_type=F32)
    act = (a * _sigmoid(a)) * b
    shared = jnp.dot(act.astype(BF16), wsd_ref[...], preferred_element_type=F32)
    xacc_ref[...] = x1 + gate2_ref[0] * shared


def _mid(u, attn_outs, xf, pool_bd, pool_scale, w_out_b, gate1, g_ffn, shift2, scale2, gate2,
         wsg_b, wsu_b, wsd_b, S):
    N, D = xf.shape
    pw = u.shape[1]
    tm = MID_TILE
    spt = S // tm
    row = lambda i: (i, 0)
    vec = lambda i: (i // spt, 0, 0)
    full = lambda a: pl.BlockSpec(a.shape, lambda i: (0,) * a.ndim)
    hpt = tm // POOL_HALO
    in_specs = [pl.BlockSpec((tm, pw), row),
                pl.BlockSpec((POOL_HALO, pw), lambda i: (jnp.maximum(i * hpt - 1, 0), 0))]
    in_specs += [pl.BlockSpec((tm, ATT_GROUP_WIDTH), row)] * 6
    in_specs += [pl.BlockSpec((tm, D), row), full(pool_bd), full(pool_scale), full(w_out_b),
                 pl.BlockSpec((1, 1, D), vec), full(g_ffn), pl.BlockSpec((1, 1, D), vec),
                 pl.BlockSpec((1, 1, D), vec), pl.BlockSpec((1, 1, D), vec),
                 full(wsg_b), full(wsu_b), full(wsd_b)]
    return pl.pallas_call(
        functools.partial(_mid_kernel, spt),
        grid=(N // tm,),
        in_specs=in_specs,
        out_specs=[pl.BlockSpec((tm, D), row), pl.BlockSpec((tm, D // 2), row)],
        out_shape=[jax.ShapeDtypeStruct((N, D), F32), jax.ShapeDtypeStruct((N, D // 2), U32)],
        scratch_shapes=[pltpu.VMEM((tm + POOL_HALO, pw), F32)],
        name="mid",
    )(u, u, *attn_outs, xf, pool_bd, pool_scale, w_out_b, gate1, g_ffn, shift2, scale2, gate2,
      wsg_b, wsu_b, wsd_b)


def _route_kernel(h_ref, wr_ref, bias_ref, e_ref, g_ref, r_ref, cnt_ref, base_ref):
    i = pl.program_id(0)
    T = h_ref.shape[0]
    E = wr_ref.shape[0]
    gsz = E // N_EXPERT_GROUPS

    @pl.when(i == 0)
    def _():
        base_ref[...] = jnp.zeros_like(base_ref)

    h_lo, h_hi = _unpack_bf16_pairs(h_ref[...])
    half = h_lo.shape[1]
    nt = (((1,), (1,)), ((), ()))
    logits = (lax.dot_general(wr_ref[:, 0:half], h_lo, nt, preferred_element_type=F32)
              + lax.dot_general(wr_ref[:, half:], h_hi, nt, preferred_element_type=F32))
    scores = _sigmoid(logits)
    biased = scores + bias_ref[...]
    giota = lax.broadcasted_iota(I32, (gsz, T), 0)
    gscore = []
    for g in range(N_EXPERT_GROUPS):
        blk = biased[g * gsz:(g + 1) * gsz, :]
        m1 = jnp.max(blk, axis=0, keepdims=True)
        i1 = jnp.min(jnp.where(blk == m1, giota, gsz), axis=0, keepdims=True)
        m2 = jnp.max(jnp.where(giota == i1, NEG_INF, blk), axis=0, keepdims=True)
        gscore.append(m1 + m2)
    parts = []
    for g in range(N_EXPERT_GROUPS):
        beaten = jnp.zeros((1, T), I32)
        for o in range(N_EXPERT_GROUPS):
            if o == g:
                continue
            wins = (gscore[o] >= gscore[g]) if o < g else (gscore[o] > gscore[g])
            beaten = beaten + wins.astype(I32)
        keep = jnp.broadcast_to(beaten, (gsz, T)) < TOPK_GROUPS
        parts.append(jnp.where(keep, biased[g * gsz:(g + 1) * gsz, :], NEG_INF))
    cur = jnp.concatenate(parts, axis=0)
    eiota = lax.broadcasted_iota(I32, (E, T), 0)
    selm = jnp.zeros((E, T), F32)
    idxs, gates = [], []
    for k in range(TOP_K):
        m = jnp.max(cur, axis=0, keepdims=True)
        idx = jnp.min(jnp.where(cur == m, eiota, E), axis=0, keepdims=True)
        oh = eiota == idx
        gates.append(jnp.sum(jnp.where(oh, scores, 0.0), axis=0, keepdims=True))
        idxs.append(idx)
        cur = jnp.where(oh, NEG_INF, cur)
        selm = jnp.where(oh, 1.0, selm)
    gsum = gates[0]
    for k in range(1, TOP_K):
        gsum = gsum + gates[k]
    before = (lax.broadcasted_iota(I32, (T, T), 0) < lax.broadcasted_iota(I32, (T, T), 1)).astype(BF16)
    tot = jnp.dot(selm.astype(BF16), before, preferred_element_type=F32) + base_ref[...]
    for k in range(TOP_K):
        e_ref[k:k + 1, :] = idxs[k]
        g_ref[k:k + 1, :] = gates[k] / gsum * ROUTED_SCALE
        r_ref[k:k + 1, :] = jnp.sum(jnp.where(eiota == idxs[k], tot, 0.0), axis=0, keepdims=True).astype(I32)
    base_ref[...] = base_ref[...] + jnp.sum(selm, axis=1, keepdims=True)
    cnt_ref[...] = base_ref[...].astype(I32)


def _route(h2p, wr_t, bias_col):
    N = h2p.shape[0]
    E, D = wr_t.shape
    T = ROUTE_TILE
    col = lambda i: (0, i)
    return pl.pallas_call(
        _route_kernel,
        grid=(N // T,),
        in_specs=[pl.BlockSpec((T, D // 2), lambda i: (i, 0)),
                  pl.BlockSpec((E, D), lambda i: (0, 0)),
                  pl.BlockSpec((E, 1), lambda i: (0, 0))],
        out_specs=[pl.BlockSpec((TOP_K, T), col), pl.BlockSpec((TOP_K, T), col),
                   pl.BlockSpec((TOP_K, T), col), pl.BlockSpec((E, 1), lambda i: (0, 0))],
        out_shape=[jax.ShapeDtypeStruct((TOP_K, N), I32), jax.ShapeDtypeStruct((TOP_K, N), F32),
                   jax.ShapeDtypeStruct((TOP_K, N), I32), jax.ShapeDtypeStruct((E, 1), I32)],
        scratch_shapes=[pltpu.VMEM((E, 1), F32)],
        compiler_params=pltpu.CompilerParams(dimension_semantics=("arbitrary",)),
        name="route",
    )(h2p, wr_t, bias_col)


def _dest_kernel(e_ref, r_ref, off_ref, d_ref):
    E = off_ref.shape[0]
    T = e_ref.shape[1]
    eiota = lax.broadcasted_iota(I32, (E, T), 0)
    off = off_ref[...]
    for k in range(TOP_K):
        start = jnp.sum(jnp.where(eiota == e_ref[k:k + 1, :], off, 0.0), axis=0, keepdims=True)
        d_ref[0, k:k + 1, :] = start.astype(I32) + r_ref[k:k + 1, :]


def _dest(eidx, rank, offs_col):
    N = eidx.shape[1]
    E = offs_col.shape[0]
    T = ROUTE_TILE
    col = lambda i: (0, i)
    return pl.pallas_call(
        _dest_kernel,
        grid=(N // T,),
        in_specs=[pl.BlockSpec((TOP_K, T), col), pl.BlockSpec((TOP_K, T), col),
                  pl.BlockSpec((E, 1), lambda i: (0, 0))],
        out_specs=pl.BlockSpec((1, TOP_K, T), lambda i: (i, 0, 0)),
        out_shape=jax.ShapeDtypeStruct((N // T, TOP_K, T), I32),
        name="dest",
    )(eidx, rank, offs_col)


def _dispatch_kernel(steps, cnt_ref, off_ref, nbe_ref, nu_ref, d_ref, h_ref, xs_ref, zero_ref, sem, zsem):
    i = pl.program_id(0)
    T = h_ref.shape[0]
    E = cnt_ref.shape[0]
    bm = zero_ref.shape[0]
    nblk = xs_ref.shape[0] // bm

    @pl.when(i == 0)
    def _():
        zero_ref[...] = jnp.zeros_like(zero_ref)

    def body(t, carry):
        for k in range(TOP_K):
            pltpu.make_async_copy(h_ref.at[pl.ds(t, 1), :], xs_ref.at[pl.ds(d_ref[0, k, t], 1), :],
                                  sem).start(priority=k % 2)
        return carry

    lax.fori_loop(0, T, body, 0)

    def pad_copy(row):
        return pltpu.make_async_copy(zero_ref.at[pl.ds(0, 1), :], xs_ref.at[pl.ds(row, 1), :], zsem)

    def tail_copy(b):
        return pltpu.make_async_copy(zero_ref, xs_ref.at[pl.ds(pl.multiple_of(b * bm, bm), bm), :], zsem)

    e_per = -(-E // steps)
    n_tail = (jnp.maximum(nblk - nu_ref[0] - i, 0) + steps - 1) // steps

    def per_expert(fn):
        def ebody(j, carry):
            e = i * e_per + j

            @pl.when(e < E)
            def _():
                lax.fori_loop(cnt_ref[e], nbe_ref[e] * bm, lambda r, c: (fn(off_ref[e] + r), c)[1], 0)
            return carry
        lax.fori_loop(0, e_per, ebody, 0)

    per_expert(lambda row: pad_copy(row).start())
    lax.fori_loop(0, n_tail, lambda j, c: (tail_copy(nu_ref[0] + i + j * steps).start(), c)[1], 0)
    per_expert(lambda row: pad_copy(row).wait())
    lax.fori_loop(0, n_tail, lambda j, c: (tail_copy(nu_ref[0] + i + j * steps).wait(), c)[1], 0)
    for k in range(TOP_K):
        pltpu.make_async_copy(h_ref, xs_ref.at[pl.ds(0, T), :], sem).wait()


def _dispatch(counts, offs, nb_e, nused, dest, h2, P):
    N, D = h2.shape
    T = ROUTE_TILE
    grid_spec = pltpu.PrefetchScalarGridSpec(
        num_scalar_prefetch=4,
        grid=(N // T,),
        in_specs=[pl.BlockSpec((1, TOP_K, T), lambda i, *_: (i, 0, 0), memory_space=pltpu.SMEM),
                  pl.BlockSpec((T, D), lambda i, *_: (i, 0))],
        out_specs=pl.BlockSpec(memory_space=pl.ANY),
        scratch_shapes=[pltpu.VMEM((GMM_BLOCK, D), h2.dtype), pltpu.SemaphoreType.DMA(()),
                        pltpu.SemaphoreType.DMA(())],
    )
    return pl.pallas_call(
        functools.partial(_dispatch_kernel, N // T),
        grid_spec=grid_spec,
        out_shape=jax.ShapeDtypeStruct((P, D), h2.dtype),
        compiler_params=pltpu.CompilerParams(dimension_semantics=("arbitrary",)),
        name="dispatch",
    )(counts, offs, nb_e, nused, dest, h2)


def _gmm_kernel(be_ref, nu_ref, xs_ref, wg_ref, wu_ref, wd_ref, ys_ref, wgb, wub, wdb):
    i = pl.program_id(0)
    e = be_ref[i]
    prev = be_ref[jnp.maximum(i - 1, 0)]

    @pl.when((i == 0) | (e != prev))
    def _():
        wgb[...] = wg_ref[0].astype(BF16)
        wub[...] = wu_ref[0].astype(BF16)
        wdb[...] = wd_ref[0].astype(BF16)

    @pl.when(i < nu_ref[0])
    def _():
        x_lo, x_hi = _unpack_bf16_pairs(xs_ref[...])
        half = x_lo.shape[1]
        a = (jnp.dot(x_lo, wgb[0:half, :], preferred_element_type=F32)
             + jnp.dot(x_hi, wgb[half:, :], preferred_element_type=F32))
        b = (jnp.dot(x_lo, wub[0:half, :], preferred_element_type=F32)
             + jnp.dot(x_hi, wub[half:, :], preferred_element_type=F32))
        act = (a * _sigmoid(a)) * b
        ys_ref[...] = jnp.dot(act.astype(BF16), wdb[...], preferred_element_type=F32)

    @pl.when(i >= nu_ref[0])
    def _():
        ys_ref[...] = jnp.zeros_like(ys_ref)


def _gmm(blk_e, nused, xs, w_gate, w_up, w_down):
    P = xs.shape[0]
    E, D, F = w_gate.shape
    bm = GMM_BLOCK
    rows = lambda i, be, nu: (i, 0)
    wsel = lambda i, be, nu: (be[i], 0, 0)
    grid_spec = pltpu.PrefetchScalarGridSpec(
        num_scalar_prefetch=2,
        grid=(P // bm,),
        in_specs=[pl.BlockSpec((bm, D // 2), rows),
                  pl.BlockSpec((1, D, F), wsel), pl.BlockSpec((1, D, F), wsel), pl.BlockSpec((1, F, D), wsel)],
        out_specs=pl.BlockSpec((bm, D), rows),
        scratch_shapes=[pltpu.VMEM((D, F), BF16), pltpu.VMEM((D, F), BF16), pltpu.VMEM((F, D), BF16)],
    )
    return pl.pallas_call(
        _gmm_kernel,
        grid_spec=grid_spec,
        out_shape=jax.ShapeDtypeStruct((P, D), F32),
        compiler_params=pltpu.CompilerParams(dimension_semantics=("arbitrary",)),
        name="gmm",
    )(blk_e, nused, xs, w_gate, w_up, w_down)


def _combine_kernel(d_ref, gt_ref, gate2_ref, xacc_ref, gfin_ref, ys_ref, o_ref, buf, sem):
    T = xacc_ref.shape[0]

    def body(t, carry):
        for k in range(TOP_K):
            pltpu.make_async_copy(ys_ref.at[pl.ds(d_ref[0, k, t], 1), :], buf.at[k, pl.ds(t, 1), :],
                                  sem).start(priority=k % 2)
        return carry

    lax.fori_loop(0, T, body, 0)
    for k in range(TOP_K):
        pltpu.make_async_copy(ys_ref.at[pl.ds(0, T), :], buf.at[k], sem).wait()
    gt = gt_ref[...]
    routed = buf[0] * gt[:, 0:1]
    for k in range(1, TOP_K):
        routed = routed + buf[k] * gt[:, k:k + 1]
    x2 = xacc_ref[...] + gate2_ref[0] * routed
    o_ref[...] = _rms(x2) * gfin_ref[...]


def _combine(dest, gates_t, gate2, xacc, g_final, ys, S):
    N, D = xacc.shape
    T = ROUTE_TILE
    spt = S // T
    return pl.pallas_call(
        _combine_kernel,
        grid=(N // T,),
        in_specs=[pl.BlockSpec((1, TOP_K, T), lambda i: (i, 0, 0), memory_space=pltpu.SMEM),
                  pl.BlockSpec((T, TOP_K), lambda i: (i, 0)),
                  pl.BlockSpec((1, 1, D), lambda i: (i // spt, 0, 0)),
                  pl.BlockSpec((T, D), lambda i: (i, 0)),
                  pl.BlockSpec((1, D), lambda i: (0, 0)),
                  pl.BlockSpec(memory_space=pl.ANY)],
        out_specs=pl.BlockSpec((T, D), lambda i: (i, 0)),
        out_shape=jax.ShapeDtypeStruct((N, D), F32),
        scratch_shapes=[pltpu.VMEM((TOP_K, T, D), F32), pltpu.SemaphoreType.DMA(())],
        compiler_params=pltpu.CompilerParams(dimension_semantics=("arbitrary",)),
        name="combine",
    )(dest, gates_t, gate2, xacc, g_final, ys)


def _to_residue_major(a, B, S, d):
    if d == 1:
        return a
    w = a.shape[-1]
    return a.reshape(B, S // d, d, w).transpose(0, 2, 1, 3).reshape(B * S, w)


def _from_residue_major(a, B, S, d):
    if d == 1:
        return a
    w = a.shape[-1]
    return a.reshape(B, d, S // d, w).transpose(0, 2, 1, 3).reshape(B * S, w)


def _layer(xf, B, S, mod, g_mix, w_in, pool_w, pool_scale, w_out, g_ffn, w_router, router_bias,
           w_gate, w_up, w_down, ws_gate, ws_up, ws_down):
    N, D = xf.shape
    E = w_router.shape[1]
    pw = pool_scale.shape[0]
    shift1, scale1, gate1, shift2, scale2, gate2 = [m.reshape(B, 1, D) for m in jnp.split(mod, 6, axis=-1)]
    u, q0, q1, q2 = _in_proj(xf, g_mix.reshape(1, D), shift1, scale1, w_in.astype(BF16), S, pw)
    attn_outs = []
    for a, d in zip((q0, q1, q2), ATT_DILATIONS):
        o, lse = _attention(_to_residue_major(a, B, S, d), S // d // ATT_BLOCK)
        attn_outs += [_from_residue_major(o, B, S, d), _from_residue_major(lse, B, S, d)]
    ng = pool_w.shape[0]
    pool_bd = jnp.einsum('gcd,gh->gchd', pool_w, jnp.eye(ng, dtype=pool_w.dtype)).reshape(pw, pw).astype(BF16)
    xacc, h2 = _mid(u, attn_outs, xf, pool_bd, pool_scale.reshape(1, pw), w_out.astype(BF16), gate1,
                    g_ffn.reshape(1, D), shift2, scale2, gate2,
                    ws_gate.astype(BF16), ws_up.astype(BF16), ws_down.astype(BF16), S)
    eidx, gates, rank, counts = _route(h2, w_router.T.astype(BF16), router_bias.reshape(E, 1).astype(F32))
    bm = GMM_BLOCK
    nblk = N * TOP_K // bm + E
    nb_e = (counts[:, 0] + bm - 1) // bm
    bend = jnp.cumsum(nb_e)
    offs = ((bend - nb_e) * bm).astype(I32)
    nused = bend[-1:].astype(I32)
    blk_e = jnp.minimum(jnp.sum(bend[None, :] <= jnp.arange(nblk, dtype=I32)[:, None], axis=1), E - 1).astype(I32)
    dest = _dest(eidx, rank, offs.astype(F32).reshape(E, 1))
    xs = _dispatch(counts[:, 0], offs, nb_e.astype(I32), nused, dest, h2, nblk * bm)
    ys = _gmm(blk_e, nused, xs, w_gate, w_up, w_down)
    return dest, gates.T, gate2, xacc, ys


def kernel(x, c, w_ada, b_ada, g_mix, w_in, pool_w, pool_scale, w_out, g_ffn, w_router, router_bias,
           w_gate, w_up, w_down, ws_gate, ws_up, ws_down, g_final):
    B, S, D = x.shape
    depth = w_ada.shape[0]
    assert depth == 1, "the final residual is fused with the final norm, so exactly one layer is supported"
    assert S % (ATT_DILATIONS[-1] * ATT_BLOCK) == 0 and S % max(IN_TILE, MID_TILE, ATT_BLOCKS_PER_STEP * ATT_BLOCK) == 0
    xf = x.reshape(B * S, D)
    mod = _ada(c, w_ada[0], b_ada[0])
    dest, gates_t, gate2, xacc, ys = _layer(
        xf, B, S, mod, g_mix[0], w_in[0], pool_w[0], pool_scale[0], w_out[0], g_ffn[0], w_router[0],
        router_bias[0], w_gate[0], w_up[0], w_down[0], ws_gate[0], ws_up[0], ws_down[0])
    out = _combine(dest, gates_t, gate2, xacc, g_final.reshape(1, D), ys, S)
    return out.reshape(B, S, D)
```

```python
import functools

import jax
import jax.numpy as jnp
from jax import lax
from jax.experimental import pallas as pl
from jax.experimental.pallas import tpu as pltpu
from jax.experimental.pallas import tpu_sc as plsc

F32 = jnp.float32
BF16 = jnp.bfloat16
I32 = jnp.int32
U32 = jnp.uint32

NORM_EPS = 1e-6
POOL_WINDOWS = (2, 4, 8, 16)
POOL_HALO = 16
ATT_DILATIONS = (1, 4, 16)
ATT_BLOCK = 128
ATT_HEADS_PER_GROUP = 4
ATT_HEAD_DIM = 64
ATT_GROUP_WIDTH = ATT_HEADS_PER_GROUP * ATT_HEAD_DIM
N_EXPERT_GROUPS = 8
TOPK_GROUPS = 4
TOP_K = 8
ROUTED_SCALE = 2.5

IN_TILE = 512
ATT_BLOCKS_PER_STEP = 8
MID_TILE = 512
ROUTE_TILE = 256
GMM_BLOCK = 256
SC_CHUNK = 128

NEG_INF = float("-inf")


def _sigmoid(v):
    return 1.0 / (1.0 + jnp.exp(-v))


def _rms(v):
    return v * lax.rsqrt(jnp.mean(v * v, axis=-1, keepdims=True) + NORM_EPS)


def _pack_bf16_pairs(v):
    n = v.shape[1] // 2
    lo = lax.bitcast_convert_type(v[:, :n].astype(BF16).astype(F32), U32)
    hi = lax.bitcast_convert_type(v[:, n:].astype(BF16).astype(F32), U32)
    return (hi & jnp.uint32(0xFFFF0000)) | (lo >> 16)


def _unpack_bf16_pairs(p):
    lo = lax.bitcast_convert_type(p << 16, F32).astype(BF16)
    hi = lax.bitcast_convert_type(p & jnp.uint32(0xFFFF0000), F32).astype(BF16)
    return lo, hi


def _ada_kernel(c_ref, w_ref, b_ref, o_ref):
    c = c_ref[...]
    cs = c * _sigmoid(c)
    o_ref[...] = jnp.dot(cs, w_ref[...], preferred_element_type=F32,
                         precision=lax.Precision.HIGHEST) + b_ref[...]


def _ada(c, w_ada, b_ada):
    B, D = c.shape
    W = w_ada.shape[1]
    tn = 1024
    return pl.pallas_call(
        _ada_kernel,
        grid=(W // tn,),
        in_specs=[pl.BlockSpec((B, D), lambda j: (0, 0)),
                  pl.BlockSpec((D, tn), lambda j: (0, j)),
                  pl.BlockSpec((1, tn), lambda j: (0, j))],
        out_specs=pl.BlockSpec((B, tn), lambda j: (0, j)),
        out_shape=jax.ShapeDtypeStruct((B, W), F32),
        name="ada",
    )(c, w_ada, b_ada.reshape(1, W))


def _in_kernel(x_ref, g_ref, sh_ref, sc_ref, w_ref, pool_ref, q0_ref, q1_ref, q2_ref):
    h = _rms(x_ref[...]) * g_ref[...]
    h = h * (1.0 + sc_ref[0]) + sh_ref[0]
    hb = h.astype(BF16)
    pw = pool_ref.shape[1]
    gw = ATT_GROUP_WIDTH
    pool_ref[...] = jnp.dot(hb, w_ref[:, 0:pw], preferred_element_type=F32)
    for g, out in enumerate((q0_ref, q1_ref, q2_ref)):
        for sec in range(3):
            c0 = pw + sec * 3 * gw + g * gw
            out[:, sec * gw:(sec + 1) * gw] = jnp.dot(
                hb, w_ref[:, c0:c0 + gw], preferred_element_type=F32).astype(BF16)


def _in_proj(xf, g_mix, shift1, scale1, w_in_b, S, pool_width):
    N, D = xf.shape
    tm = IN_TILE
    spt = S // tm
    vec = lambda i: (i // spt, 0, 0)
    row = lambda i: (i, 0)
    gw3 = 3 * ATT_GROUP_WIDTH
    return pl.pallas_call(
        _in_kernel,
        grid=(N // tm,),
        in_specs=[pl.BlockSpec((tm, D), row),
                  pl.BlockSpec((1, D), lambda i: (0, 0)),
                  pl.BlockSpec((1, 1, D), vec),
                  pl.BlockSpec((1, 1, D), vec),
                  pl.BlockSpec(w_in_b.shape, lambda i: (0, 0))],
        out_specs=[pl.BlockSpec((tm, pool_width), row)] + [pl.BlockSpec((tm, gw3), row)] * 3,
        out_shape=[jax.ShapeDtypeStruct((N, pool_width), F32)] + [jax.ShapeDtypeStruct((N, gw3), BF16)] * 3,
        name="in_proj",
    )(xf, g_mix, shift1, scale1, w_in_b)


def _attn_kernel(nbs, a_ref, halo_ref, o_ref, lse_ref, kv_ref):
    i = pl.program_id(0)
    R = a_ref.shape[0] // ATT_BLOCK
    gw = ATT_GROUP_WIDTH
    blk = ATT_BLOCK
    nh = ATT_HEADS_PER_GROUP
    kv_ref[0:blk, :] = halo_ref[:, gw:3 * gw]
    kv_ref[blk:, :] = a_ref[:, gw:3 * gw]
    row = lax.broadcasted_iota(I32, (nh * blk, blk), 0) % blk
    col = lax.broadcasted_iota(I32, (nh * blk, blk), 1)
    head_of_lane = lax.broadcasted_iota(I32, (blk, gw), 1) // ATT_HEAD_DIM
    own_valid = col <= row
    nt = (((1,), (1,)), ((), ()))

    def body(jj, carry):
        r0 = pl.multiple_of(jj * blk, blk)
        r1 = pl.multiple_of(jj * blk + blk, blk)
        qf = a_ref[pl.ds(r0, blk), 0:gw].astype(F32)
        q4 = jnp.concatenate([jnp.where(head_of_lane == h, qf, 0.0) for h in range(nh)], axis=0).astype(BF16)
        kp = kv_ref[pl.ds(r0, blk), 0:gw]
        vp = kv_ref[pl.ds(r0, blk), gw:2 * gw]
        ko = kv_ref[pl.ds(r1, blk), 0:gw]
        vo = kv_ref[pl.ds(r1, blk), gw:2 * gw]
        scale = ATT_HEAD_DIM ** -0.5
        sp = lax.dot_general(q4, kp, nt, preferred_element_type=F32) * scale
        so = lax.dot_general(q4, ko, nt, preferred_element_type=F32) * scale
        first = ((i * R + jj) % nbs) == 0
        lo = row + jnp.where(first, blk, 0)
        sp = jnp.where(col >= lo, sp, NEG_INF)
        so = jnp.where(own_valid, so, NEG_INF)
        m = jnp.maximum(jnp.max(sp, axis=1, keepdims=True), jnp.max(so, axis=1, keepdims=True))
        pp = jnp.exp(sp - m)
        po = jnp.exp(so - m)
        l = jnp.sum(pp, axis=1, keepdims=True) + jnp.sum(po, axis=1, keepdims=True)
        o4 = (jnp.dot(pp.astype(BF16), vp, preferred_element_type=F32)
              + jnp.dot(po.astype(BF16), vo, preferred_element_type=F32)) / l
        lse4 = m + jnp.log(l)
        o = jnp.zeros((blk, gw), F32)
        lse = jnp.zeros((blk, gw), F32)
        for h in range(nh):
            hm = head_of_lane == h
            o = jnp.where(hm, o4[h * blk:(h + 1) * blk, :], o)
            lse = jnp.where(hm, lse4[h * blk:(h + 1) * blk, :], lse)
        o_ref[pl.ds(r0, blk), :] = o
        lse_ref[pl.ds(r0, blk), :] = lse
        return carry

    lax.fori_loop(0, R, body, 0)


def _attention(a, nbs):
    N = a.shape[0]
    R = ATT_BLOCKS_PER_STEP
    gw = ATT_GROUP_WIDTH
    tm = R * ATT_BLOCK
    return pl.pallas_call(
        functools.partial(_attn_kernel, nbs),
        grid=(N // tm,),
        in_specs=[pl.BlockSpec((tm, 3 * gw), lambda i: (i, 0)),
                  pl.BlockSpec((ATT_BLOCK, 3 * gw), lambda i: (jnp.maximum(i * R - 1, 0), 0))],
        out_specs=[pl.BlockSpec((tm, gw), lambda i: (i, 0))] * 2,
        out_shape=[jax.ShapeDtypeStruct((N, gw), F32)] * 2,
        scratch_shapes=[pltpu.VMEM((tm + ATT_BLOCK, 2 * gw), BF16)],
        name="attn",
    )(a, a)


def _mid_kernel(spt, u_ref, uh_ref, o0_ref, l0_ref, o1_ref, l1_ref, o2_ref, l2_ref, x_ref,
                pbd_ref, psc_ref, wout_ref, gate1_ref, gffn_ref, sh2_ref, sc2_ref, gate2_ref,
                wsg_ref, wsu_ref, wsd_ref, xacc_ref, h2_ref, ext_ref):
    i = pl.program_id(0)
    tm, pw = u_ref.shape
    si = i % spt
    u = u_ref[...]
    keep = jnp.full((POOL_HALO, pw), si, I32) > 0
    ext_ref[0:POOL_HALO, :] = jnp.where(keep, uh_ref[...], 0.0)
    ext_ref[POOL_HALO:, :] = u
    lane_grp = lax.broadcasted_iota(I32, (tm, pw), 1) // (pw // len(POOL_WINDOWS))
    s = u
    pooled = jnp.zeros((tm, pw), F32)
    for j in range(1, POOL_HALO):
        s = s + ext_ref[pl.ds(POOL_HALO - j, tm), :]
        if (j + 1) in POOL_WINDOWS:
            pooled = jnp.where(lane_grp == POOL_WINDOWS.index(j + 1), s, pooled)
    win = jnp.zeros((tm, pw), I32)
    for g, w in enumerate(POOL_WINDOWS):
        win = jnp.where(lane_grp == g, w, win)
    pos = si * tm + lax.broadcasted_iota(I32, (tm, pw), 0)
    cnt = jnp.minimum(pos + 1, win).astype(F32)
    pooled = pooled / cnt - u
    pool_out = jnp.dot(pooled.astype(BF16), pbd_ref[...], preferred_element_type=F32) * psc_ref[...]
    l0 = l0_ref[...]
    l1 = l1_ref[...]
    l2 = l2_ref[...]
    m = jnp.maximum(jnp.maximum(l0, l1), l2)
    w0 = jnp.exp(l0 - m)
    w1 = jnp.exp(l1 - m)
    w2 = jnp.exp(l2 - m)
    attn = (w0 * o0_ref[...] + w1 * o1_ref[...] + w2 * o2_ref[...]) / (w0 + w1 + w2)
    mixed = (jnp.dot(pool_out.astype(BF16), wout_ref[0:pw, :], preferred_element_type=F32)
             + jnp.dot(attn.astype(BF16), wout_ref[pw:, :], preferred_element_type=F32))
    x1 = x_ref[...] + gate1_ref[0] * mixed
    h2 = _rms(x1) * gffn_ref[...]
    h2 = h2 * (1.0 + sc2_ref[0]) + sh2_ref[0]
    h2_ref[...] = _pack_bf16_pairs(h2)
    hb = h2.astype(BF16)
    a = jnp.dot(hb, wsg_ref[...], preferred_element_type=F32)
    b = jnp.dot(hb, wsu_ref[...], preferred_element_type=F32)
    act = (a * _sigmoid(a)) * b
    shared = jnp.dot(act.astype(BF16), wsd_ref[...], preferred_element_type=F32)
    xacc_ref[...] = x1 + gate2_ref[0] * shared


def _mid(u, attn_outs, xf, pool_bd, pool_scale, w_out_b, gate1, g_ffn, shift2, scale2, gate2,
         wsg_b, wsu_b, wsd_b, S):
    N, D = xf.shape
    pw = u.shape[1]
    tm = MID_TILE
    spt = S // tm
    row = lambda i: (i, 0)
    vec = lambda i: (i // spt, 0, 0)
    full = lambda a: pl.BlockSpec(a.shape, lambda i: (0,) * a.ndim)
    hpt = tm // POOL_HALO
    in_specs = [pl.BlockSpec((tm, pw), row),
                pl.BlockSpec((POOL_HALO, pw), lambda i: (jnp.maximum(i * hpt - 1, 0), 0))]
    in_specs += [pl.BlockSpec((tm, ATT_GROUP_WIDTH), row)] * 6
    in_specs += [pl.BlockSpec((tm, D), row), full(pool_bd), full(pool_scale), full(w_out_b),
                 pl.BlockSpec((1, 1, D), vec), full(g_ffn), pl.BlockSpec((1, 1, D), vec),
                 pl.BlockSpec((1, 1, D), vec), pl.BlockSpec((1, 1, D), vec),
                 full(wsg_b), full(wsu_b), full(wsd_b)]
    return pl.pallas_call(
        functools.partial(_mid_kernel, spt),
        grid=(N // tm,),
        in_specs=in_specs,
        out_specs=[pl.BlockSpec((tm, D), row), pl.BlockSpec((tm, D // 2), row)],
        out_shape=[jax.ShapeDtypeStruct((N, D), F32), jax.ShapeDtypeStruct((N, D // 2), U32)],
        scratch_shapes=[pltpu.VMEM((tm + POOL_HALO, pw), F32)],
        name="mid",
    )(u, u, *attn_outs, xf, pool_bd, pool_scale, w_out_b, gate1, g_ffn, shift2, scale2, gate2,
      wsg_b, wsu_b, wsd_b)


def _route_kernel(h_ref, wr_ref, bias_ref, e_ref, g_ref, r_ref, cnt_ref, base_ref):
    i = pl.program_id(0)
    T = h_ref.shape[0]
    E = wr_ref.shape[0]
    gsz = E // N_EXPERT_GROUPS

    @pl.when(i == 0)
    def _():
        base_ref[...] = jnp.zeros_like(base_ref)

    h_lo, h_hi = _unpack_bf16_pairs(h_ref[...])
    half = h_lo.shape[1]
    nt = (((1,), (1,)), ((), ()))
    logits = (lax.dot_general(wr_ref[:, 0:half], h_lo, nt, preferred_element_type=F32)
              + lax.dot_general(wr_ref[:, half:], h_hi, nt, preferred_element_type=F32))
    scores = _sigmoid(logits)
    biased = scores + bias_ref[...]
    giota = lax.broadcasted_iota(I32, (gsz, T), 0)
    gscore = []
    for g in range(N_EXPERT_GROUPS):
        blk = biased[g * gsz:(g + 1) * gsz, :]
        m1 = jnp.max(blk, axis=0, keepdims=True)
        i1 = jnp.min(jnp.where(blk == m1, giota, gsz), axis=0, keepdims=True)
        m2 = jnp.max(jnp.where(giota == i1, NEG_INF, blk), axis=0, keepdims=True)
        gscore.append(m1 + m2)
    parts = []
    for g in range(N_EXPERT_GROUPS):
        beaten = jnp.zeros((1, T), I32)
        for o in range(N_EXPERT_GROUPS):
            if o == g:
                continue
            wins = (gscore[o] >= gscore[g]) if o < g else (gscore[o] > gscore[g])
            beaten = beaten + wins.astype(I32)
        keep = jnp.broadcast_to(beaten, (gsz, T)) < TOPK_GROUPS
        parts.append(jnp.where(keep, biased[g * gsz:(g + 1) * gsz, :], NEG_INF))
    cur = jnp.concatenate(parts, axis=0)
    eiota = lax.broadcasted_iota(I32, (E, T), 0)
    selm = jnp.zeros((E, T), F32)
    idxs, gates = [], []
    for k in range(TOP_K):
        m = jnp.max(cur, axis=0, keepdims=True)
        idx = jnp.min(jnp.where(cur == m, eiota, E), axis=0, keepdims=True)
        oh = eiota == idx
        gates.append(jnp.sum(jnp.where(oh, scores, 0.0), axis=0, keepdims=True))
        idxs.append(idx)
        cur = jnp.where(oh, NEG_INF, cur)
        selm = jnp.where(oh, 1.0, selm)
    gsum = gates[0]
    for k in range(1, TOP_K):
        gsum = gsum + gates[k]
    before = (lax.broadcasted_iota(I32, (T, T), 0) < lax.broadcasted_iota(I32, (T, T), 1)).astype(BF16)
    tot = jnp.dot(selm.astype(BF16), before, preferred_element_type=F32) + base_ref[...]
    for k in range(TOP_K):
        e_ref[k:k + 1, :] = idxs[k]
        g_ref[k:k + 1, :] = gates[k] / gsum * ROUTED_SCALE
        r_ref[k:k + 1, :] = jnp.sum(jnp.where(eiota == idxs[k], tot, 0.0), axis=0, keepdims=True).astype(I32)
    base_ref[...] = base_ref[...] + jnp.sum(selm, axis=1, keepdims=True)
    cnt_ref[...] = base_ref[...].astype(I32)


def _route(h2p, wr_t, bias_col):
    N = h2p.shape[0]
    E, D = wr_t.shape
    T = ROUTE_TILE
    col = lambda i: (0, i)
    return pl.pallas_call(
        _route_kernel,
        grid=(N // T,),
        in_specs=[pl.BlockSpec((T, D // 2), lambda i: (i, 0)),
                  pl.BlockSpec((E, D), lambda i: (0, 0)),
                  pl.BlockSpec((E, 1), lambda i: (0, 0))],
        out_specs=[pl.BlockSpec((TOP_K, T), col), pl.BlockSpec((TOP_K, T), col),
                   pl.BlockSpec((TOP_K, T), col), pl.BlockSpec((E, 1), lambda i: (0, 0))],
        out_shape=[jax.ShapeDtypeStruct((TOP_K, N), I32), jax.ShapeDtypeStruct((TOP_K, N), F32),
                   jax.ShapeDtypeStruct((TOP_K, N), I32), jax.ShapeDtypeStruct((E, 1), I32)],
        scratch_shapes=[pltpu.VMEM((E, 1), F32)],
        compiler_params=pltpu.CompilerParams(dimension_semantics=("arbitrary",)),
        name="route",
    )(h2p, wr_t, bias_col)


def _dest_kernel(e_ref, r_ref, off_ref, d_ref, dc_ref):
    E = off_ref.shape[0]
    T = e_ref.shape[1]
    C = dc_ref.shape[2]
    eiota = lax.broadcasted_iota(I32, (E, T), 0)
    off = off_ref[...]
    for k in range(TOP_K):
        start = jnp.sum(jnp.where(eiota == e_ref[k:k + 1, :], off, 0.0), axis=0, keepdims=True)
        d = start.astype(I32) + r_ref[k:k + 1, :]
        d_ref[0, k:k + 1, :] = d
        for c in range(T // C):
            dc_ref[c, k:k + 1, :] = d[:, c * C:(c + 1) * C]


def _dest(eidx, rank, offs_col):
    N = eidx.shape[1]
    E = offs_col.shape[0]
    T = ROUTE_TILE
    C = SC_CHUNK
    col = lambda i: (0, i)
    return pl.pallas_call(
        _dest_kernel,
        grid=(N // T,),
        in_specs=[pl.BlockSpec((TOP_K, T), col), pl.BlockSpec((TOP_K, T), col),
                  pl.BlockSpec((E, 1), lambda i: (0, 0))],
        out_specs=[pl.BlockSpec((1, TOP_K, T), lambda i: (i, 0, 0)),
                   pl.BlockSpec((T // C, TOP_K, C), lambda i: (i, 0, 0))],
        out_shape=[jax.ShapeDtypeStruct((N // T, TOP_K, T), I32),
                   jax.ShapeDtypeStruct((N // C, TOP_K, C), I32)],
        name="dest",
    )(eidx, rank, offs_col)


def _sc_dispatch(dest_c, h2p, P):
    N, W = h2p.shape
    C = dest_c.shape[2]
    info = plsc.get_sparse_core_info()
    nw = info.num_cores * info.num_subcores
    per_w = N // C // nw
    mesh = plsc.VectorSubcoreMesh(core_axis_name="c", subcore_axis_name="s")

    @functools.partial(
        pl.kernel, mesh=mesh,
        out_type=jax.ShapeDtypeStruct((P, W), h2p.dtype),
        scratch_types=[pltpu.VMEM((TOP_K, C), I32), pltpu.VMEM((C, W), h2p.dtype), pltpu.SemaphoreType.DMA],
        name="sc_dispatch",
    )
    def k(dest_hbm, h_hbm, xs_hbm, idx_v, rows_v, sem):
        wid = lax.axis_index("s") * info.num_cores + lax.axis_index("c")

        @pl.loop(0, per_w)
        def _(j):
            ch = wid * per_w + j
            pltpu.sync_copy(dest_hbm.at[ch], idx_v)
            pltpu.sync_copy(h_hbm.at[pl.ds(ch * C, C)], rows_v)
            copies = [pltpu.async_copy(rows_v, xs_hbm.at[idx_v.at[kk]], sem) for kk in range(TOP_K)]
            for cp in copies:
                cp.wait()

    return k(dest_c, h2p)


def _dispatch_kernel(steps, cnt_ref, off_ref, nbe_ref, nu_ref, d_ref, h_ref, xs_ref, zero_ref, sem, zsem):
    i = pl.program_id(0)
    T = h_ref.shape[0]
    E = cnt_ref.shape[0]
    bm = zero_ref.shape[0]
    nblk = xs_ref.shape[0] // bm

    @pl.when(i == 0)
    def _():
        zero_ref[...] = jnp.zeros_like(zero_ref)

    def body(t, carry):
        for k in range(TOP_K):
            pltpu.make_async_copy(h_ref.at[pl.ds(t, 1), :], xs_ref.at[pl.ds(d_ref[0, k, t], 1), :],
                                  sem).start(priority=k % 2)
        return carry

    lax.fori_loop(0, T, body, 0)

    def pad_copy(row):
        return pltpu.make_async_copy(zero_ref.at[pl.ds(0, 1), :], xs_ref.at[pl.ds(row, 1), :], zsem)

    def tail_copy(b):
        return pltpu.make_async_copy(zero_ref, xs_ref.at[pl.ds(pl.multiple_of(b * bm, bm), bm), :], zsem)

    e_per = -(-E // steps)
    n_tail = (jnp.maximum(nblk - nu_ref[0] - i, 0) + steps - 1) // steps

    def per_expert(fn):
        def ebody(j, carry):
            e = i * e_per + j

            @pl.when(e < E)
            def _():
                lax.fori_loop(cnt_ref[e], nbe_ref[e] * bm, lambda r, c: (fn(off_ref[e] + r), c)[1], 0)
            return carry
        lax.fori_loop(0, e_per, ebody, 0)

    per_expert(lambda row: pad_copy(row).start())
    lax.fori_loop(0, n_tail, lambda j, c: (tail_copy(nu_ref[0] + i + j * steps).start(), c)[1], 0)
    per_expert(lambda row: pad_copy(row).wait())
    lax.fori_loop(0, n_tail, lambda j, c: (tail_copy(nu_ref[0] + i + j * steps).wait(), c)[1], 0)
    for k in range(TOP_K):
        pltpu.make_async_copy(h_ref, xs_ref.at[pl.ds(0, T), :], sem).wait()


def _dispatch(counts, offs, nb_e, nused, dest, h2, P):
    N, D = h2.shape
    T = ROUTE_TILE
    grid_spec = pltpu.PrefetchScalarGridSpec(
        num_scalar_prefetch=4,
        grid=(N // T,),
        in_specs=[pl.BlockSpec((1, TOP_K, T), lambda i, *_: (i, 0, 0), memory_space=pltpu.SMEM),
                  pl.BlockSpec((T, D), lambda i, *_: (i, 0))],
        out_specs=pl.BlockSpec(memory_space=pl.ANY),
        scratch_shapes=[pltpu.VMEM((GMM_BLOCK, D), h2.dtype), pltpu.SemaphoreType.DMA(()),
                        pltpu.SemaphoreType.DMA(())],
    )
    return pl.pallas_call(
        functools.partial(_dispatch_kernel, N // T),
        grid_spec=grid_spec,
        out_shape=jax.ShapeDtypeStruct((P, D), h2.dtype),
        compiler_params=pltpu.CompilerParams(dimension_semantics=("arbitrary",)),
        name="dispatch",
    )(counts, offs, nb_e, nused, dest, h2)


def _gmm_kernel(be_ref, nu_ref, xs_ref, wg_ref, wu_ref, wd_ref, ys_ref, wgb, wub, wdb):
    i = pl.program_id(0)
    e = be_ref[i]
    prev = be_ref[jnp.maximum(i - 1, 0)]

    @pl.when((i == 0) | (e != prev))
    def _():
        wgb[...] = wg_ref[0].astype(BF16)
        wub[...] = wu_ref[0].astype(BF16)
        wdb[...] = wd_ref[0].astype(BF16)

    @pl.when(i < nu_ref[0])
    def _():
        x_lo, x_hi = _unpack_bf16_pairs(xs_ref[...])
        half = x_lo.shape[1]
        a = (jnp.dot(x_lo, wgb[0:half, :], preferred_element_type=F32)
             + jnp.dot(x_hi, wgb[half:, :], preferred_element_type=F32))
        b = (jnp.dot(x_lo, wub[0:half, :], preferred_element_type=F32)
             + jnp.dot(x_hi, wub[half:, :], preferred_element_type=F32))
        act = (a * _sigmoid(a)) * b
        ys_ref[...] = jnp.dot(act.astype(BF16), wdb[...], preferred_element_type=F32)

    @pl.when(i >= nu_ref[0])
    def _():
        ys_ref[...] = jnp.zeros_like(ys_ref)


def _gmm(blk_e, nused, xs, w_gate, w_up, w_down):
    P = xs.shape[0]
    E, D, F = w_gate.shape
    bm = GMM_BLOCK
    rows = lambda i, be, nu: (i, 0)
    wsel = lambda i, be, nu: (be[i], 0, 0)
    grid_spec = pltpu.PrefetchScalarGridSpec(
        num_scalar_prefetch=2,
        grid=(P // bm,),
        in_specs=[pl.BlockSpec((bm, D // 2), rows),
                  pl.BlockSpec((1, D, F), wsel), pl.BlockSpec((1, D, F), wsel), pl.BlockSpec((1, F, D), wsel)],
        out_specs=pl.BlockSpec((bm, D), rows),
        scratch_shapes=[pltpu.VMEM((D, F), BF16), pltpu.VMEM((D, F), BF16), pltpu.VMEM((F, D), BF16)],
    )
    return pl.pallas_call(
        _gmm_kernel,
        grid_spec=grid_spec,
        out_shape=jax.ShapeDtypeStruct((P, D), F32),
        compiler_params=pltpu.CompilerParams(dimension_semantics=("arbitrary",)),
        name="gmm",
    )(blk_e, nused, xs, w_gate, w_up, w_down)


def _combine_kernel(d_ref, gt_ref, gate2_ref, xacc_ref, gfin_ref, ys_ref, o_ref, buf, sem):
    T = xacc_ref.shape[0]

    def body(t, carry):
        for k in range(TOP_K):
            pltpu.make_async_copy(ys_ref.at[pl.ds(d_ref[0, k, t], 1), :], buf.at[k, pl.ds(t, 1), :],
                                  sem).start(priority=k % 2)
        return carry

    lax.fori_loop(0, T, body, 0)
    for k in range(TOP_K):
        pltpu.make_async_copy(ys_ref.at[pl.ds(0, T), :], buf.at[k], sem).wait()
    gt = gt_ref[...]
    routed = buf[0] * gt[:, 0:1]
    for k in range(1, TOP_K):
        routed = routed + buf[k] * gt[:, k:k + 1]
    x2 = xacc_ref[...] + gate2_ref[0] * routed
    o_ref[...] = _rms(x2) * gfin_ref[...]


def _combine(dest, gates_t, gate2, xacc, g_final, ys, S):
    N, D = xacc.shape
    T = ROUTE_TILE
    spt = S // T
    return pl.pallas_call(
        _combine_kernel,
        grid=(N // T,),
        in_specs=[pl.BlockSpec((1, TOP_K, T), lambda i: (i, 0, 0), memory_space=pltpu.SMEM),
                  pl.BlockSpec((T, TOP_K), lambda i: (i, 0)),
                  pl.BlockSpec((1, 1, D), lambda i: (i // spt, 0, 0)),
                  pl.BlockSpec((T, D), lambda i: (i, 0)),
                  pl.BlockSpec((1, D), lambda i: (0, 0)),
                  pl.BlockSpec(memory_space=pl.ANY)],
        out_specs=pl.BlockSpec((T, D), lambda i: (i, 0)),
        out_shape=jax.ShapeDtypeStruct((N, D), F32),
        scratch_shapes=[pltpu.VMEM((TOP_K, T, D), F32), pltpu.SemaphoreType.DMA(())],
        compiler_params=pltpu.CompilerParams(dimension_semantics=("arbitrary",)),
        name="combine",
    )(dest, gates_t, gate2, xacc, g_final, ys)


def _to_residue_major(a, B, S, d):
    if d == 1:
        return a
    w = a.shape[-1]
    return a.reshape(B, S // d, d, w).transpose(0, 2, 1, 3).reshape(B * S, w)


def _from_residue_major(a, B, S, d):
    if d == 1:
        return a
    w = a.shape[-1]
    return a.reshape(B, d, S // d, w).transpose(0, 2, 1, 3).reshape(B * S, w)


def _layer(xf, B, S, mod, g_mix, w_in, pool_w, pool_scale, w_out, g_ffn, w_router, router_bias,
           w_gate, w_up, w_down, ws_gate, ws_up, ws_down):
    N, D = xf.shape
    E = w_router.shape[1]
    pw = pool_scale.shape[0]
    shift1, scale1, gate1, shift2, scale2, gate2 = [m.reshape(B, 1, D) for m in jnp.split(mod, 6, axis=-1)]
    u, q0, q1, q2 = _in_proj(xf, g_mix.reshape(1, D), shift1, scale1, w_in.astype(BF16), S, pw)
    attn_outs = []
    for a, d in zip((q0, q1, q2), ATT_DILATIONS):
        o, lse = _attention(_to_residue_major(a, B, S, d), S // d // ATT_BLOCK)
        attn_outs += [_from_residue_major(o, B, S, d), _from_residue_major(lse, B, S, d)]
    ng = pool_w.shape[0]
    pool_bd = jnp.einsum('gcd,gh->gchd', pool_w, jnp.eye(ng, dtype=pool_w.dtype)).reshape(pw, pw).astype(BF16)
    xacc, h2 = _mid(u, attn_outs, xf, pool_bd, pool_scale.reshape(1, pw), w_out.astype(BF16), gate1,
                    g_ffn.reshape(1, D), shift2, scale2, gate2,
                    ws_gate.astype(BF16), ws_up.astype(BF16), ws_down.astype(BF16), S)
    eidx, gates, rank, counts = _route(h2, w_router.T.astype(BF16), router_bias.reshape(E, 1).astype(F32))
    bm = GMM_BLOCK
    nblk = N * TOP_K // bm + E
    nb_e = (counts[:, 0] + bm - 1) // bm
    bend = jnp.cumsum(nb_e)
    offs = ((bend - nb_e) * bm).astype(I32)
    nused = bend[-1:].astype(I32)
    blk_e = jnp.minimum(jnp.sum(bend[None, :] <= jnp.arange(nblk, dtype=I32)[:, None], axis=1), E - 1).astype(I32)
    dest, dest_c = _dest(eidx, rank, offs.astype(F32).reshape(E, 1))
    xs = _sc_dispatch(dest_c, h2, nblk * bm)
    ys = _gmm(blk_e, nused, xs, w_gate, w_up, w_down)
    return dest, gates.T, gate2, xacc, ys


def kernel(x, c, w_ada, b_ada, g_mix, w_in, pool_w, pool_scale, w_out, g_ffn, w_router, router_bias,
           w_gate, w_up, w_down, ws_gate, ws_up, ws_down, g_final):
    B, S, D = x.shape
    depth = w_ada.shape[0]
    assert depth == 1, "the final residual is fused with the final norm, so exactly one layer is supported"
    assert S % (ATT_DILATIONS[-1] * ATT_BLOCK) == 0 and S % max(IN_TILE, MID_TILE, ATT_BLOCKS_PER_STEP * ATT_BLOCK) == 0
    xf = x.reshape(B * S, D)
    mod = _ada(c, w_ada[0], b_ada[0])
    dest, gates_t, gate2, xacc, ys = _layer(
        xf, B, S, mod, g_mix[0], w_in[0], pool_w[0], pool_scale[0], w_out[0], g_ffn[0], w_router[0],
        router_bias[0], w_gate[0], w_up[0], w_down[0], ws_gate[0], ws_up[0], ws_down[0])
    out = _combine(dest, gates_t, gate2, xacc, g_final.reshape(1, D), ys, S)
    return out.reshape(B, S, D)
```

```python
import functools

import jax
import jax.numpy as jnp
from jax import lax
from jax.experimental import pallas as pl
from jax.experimental.pallas import tpu as pltpu
from jax.experimental.pallas import tpu_sc as plsc

F32 = jnp.float32
BF16 = jnp.bfloat16
I32 = jnp.int32
U32 = jnp.uint32

NORM_EPS = 1e-6
POOL_WINDOWS = (2, 4, 8, 16)
POOL_HALO = 16
ATT_DILATIONS = (1, 4, 16)
ATT_BLOCK = 128
ATT_HEADS_PER_GROUP = 4
ATT_HEAD_DIM = 64
ATT_GROUP_WIDTH = ATT_HEADS_PER_GROUP * ATT_HEAD_DIM
N_EXPERT_GROUPS = 8
TOPK_GROUPS = 4
TOP_K = 8
ROUTED_SCALE = 2.5

IN_TILE = 512
ATT_BLOCKS_PER_STEP = 8
MID_TILE = 512
ROUTE_TILE = 256
GMM_BLOCK = 256
SC_CHUNK = 128

NEG_INF = float("-inf")


def _sigmoid(v):
    return 1.0 / (1.0 + jnp.exp(-v))


def _rms(v):
    return v * lax.rsqrt(jnp.mean(v * v, axis=-1, keepdims=True) + NORM_EPS)


def _pack_bf16_pairs(v):
    n = v.shape[1] // 2
    lo = lax.bitcast_convert_type(v[:, :n].astype(BF16).astype(F32), U32)
    hi = lax.bitcast_convert_type(v[:, n:].astype(BF16).astype(F32), U32)
    return (hi & jnp.uint32(0xFFFF0000)) | (lo >> 16)


def _unpack_bf16_pairs(p):
    lo = lax.bitcast_convert_type(p << 16, F32).astype(BF16)
    hi = lax.bitcast_convert_type(p & jnp.uint32(0xFFFF0000), F32).astype(BF16)
    return lo, hi


def _ada_kernel(c_ref, w_ref, b_ref, o_ref):
    c = c_ref[...]
    cs = c * _sigmoid(c)
    o_ref[...] = jnp.dot(cs, w_ref[...], preferred_element_type=F32,
                         precision=lax.Precision.HIGHEST) + b_ref[...]


def _ada(c, w_ada, b_ada):
    B, D = c.shape
    W = w_ada.shape[1]
    tn = 1024
    return pl.pallas_call(
        _ada_kernel,
        grid=(W // tn,),
        in_specs=[pl.BlockSpec((B, D), lambda j: (0, 0)),
                  pl.BlockSpec((D, tn), lambda j: (0, j)),
                  pl.BlockSpec((1, tn), lambda j: (0, j))],
        out_specs=pl.BlockSpec((B, tn), lambda j: (0, j)),
        out_shape=jax.ShapeDtypeStruct((B, W), F32),
        name="ada",
    )(c, w_ada, b_ada.reshape(1, W))


def _in_kernel(x_ref, g_ref, sh_ref, sc_ref, w_ref, pool_ref, q0_ref, q1_ref, q2_ref):
    h = _rms(x_ref[...]) * g_ref[...]
    h = h * (1.0 + sc_ref[0]) + sh_ref[0]
    hb = h.astype(BF16)
    pw = pool_ref.shape[1]
    gw = ATT_GROUP_WIDTH
    pool_ref[...] = jnp.dot(hb, w_ref[:, 0:pw], preferred_element_type=F32)
    for g, out in enumerate((q0_ref, q1_ref, q2_ref)):
        for sec in range(3):
            c0 = pw + sec * 3 * gw + g * gw
            out[:, sec * gw:(sec + 1) * gw] = jnp.dot(
                hb, w_ref[:, c0:c0 + gw], preferred_element_type=F32).astype(BF16)


def _in_proj(xf, g_mix, shift1, scale1, w_in_b, S, pool_width):
    N, D = xf.shape
    tm = IN_TILE
    spt = S // tm
    vec = lambda i: (i // spt, 0, 0)
    row = lambda i: (i, 0)
    gw3 = 3 * ATT_GROUP_WIDTH
    return pl.pallas_call(
        _in_kernel,
        grid=(N // tm,),
        in_specs=[pl.BlockSpec((tm, D), row),
                  pl.BlockSpec((1, D), lambda i: (0, 0)),
                  pl.BlockSpec((1, 1, D), vec),
                  pl.BlockSpec((1, 1, D), vec),
                  pl.BlockSpec(w_in_b.shape, lambda i: (0, 0))],
        out_specs=[pl.BlockSpec((tm, pool_width), row)] + [pl.BlockSpec((tm, gw3), row)] * 3,
        out_shape=[jax.ShapeDtypeStruct((N, pool_width), F32)] + [jax.ShapeDtypeStruct((N, gw3), BF16)] * 3,
        name="in_proj",
    )(xf, g_mix, shift1, scale1, w_in_b)


def _attn_kernel(nbs, a_ref, halo_ref, o_ref, lse_ref, kv_ref):
    i = pl.program_id(0)
    R = a_ref.shape[0] // ATT_BLOCK
    gw = ATT_GROUP_WIDTH
    blk = ATT_BLOCK
    nh = ATT_HEADS_PER_GROUP
    kv_ref[0:blk, :] = halo_ref[:, gw:3 * gw]
    kv_ref[blk:, :] = a_ref[:, gw:3 * gw]
    row = lax.broadcasted_iota(I32, (nh * blk, blk), 0) % blk
    col = lax.broadcasted_iota(I32, (nh * blk, blk), 1)
    head_of_lane = lax.broadcasted_iota(I32, (blk, gw), 1) // ATT_HEAD_DIM
    own_valid = col <= row
    nt = (((1,), (1,)), ((), ()))

    def body(jj, carry):
        r0 = pl.multiple_of(jj * blk, blk)
        r1 = pl.multiple_of(jj * blk + blk, blk)
        qf = a_ref[pl.ds(r0, blk), 0:gw].astype(F32)
        q4 = jnp.concatenate([jnp.where(head_of_lane == h, qf, 0.0) for h in range(nh)], axis=0).astype(BF16)
        kp = kv_ref[pl.ds(r0, blk), 0:gw]
        vp = kv_ref[pl.ds(r0, blk), gw:2 * gw]
        ko = kv_ref[pl.ds(r1, blk), 0:gw]
        vo = kv_ref[pl.ds(r1, blk), gw:2 * gw]
        scale = ATT_HEAD_DIM ** -0.5
        sp = lax.dot_general(q4, kp, nt, preferred_element_type=F32) * scale
        so = lax.dot_general(q4, ko, nt, preferred_element_type=F32) * scale
        first = ((i * R + jj) % nbs) == 0
        lo = row + jnp.where(first, blk, 0)
        sp = jnp.where(col >= lo, sp, NEG_INF)
        so = jnp.where(own_valid, so, NEG_INF)
        m = jnp.maximum(jnp.max(sp, axis=1, keepdims=True), jnp.max(so, axis=1, keepdims=True))
        pp = jnp.exp(sp - m)
        po = jnp.exp(so - m)
        l = jnp.sum(pp, axis=1, keepdims=True) + jnp.sum(po, axis=1, keepdims=True)
        o4 = (jnp.dot(pp.astype(BF16), vp, preferred_element_type=F32)
              + jnp.dot(po.astype(BF16), vo, preferred_element_type=F32)) / l
        lse4 = m + jnp.log(l)
        o = jnp.zeros((blk, gw), F32)
        lse = jnp.zeros((blk, gw), F32)
        for h in range(nh):
            hm = head_of_lane == h
            o = jnp.where(hm, o4[h * blk:(h + 1) * blk, :], o)
            lse = jnp.where(hm, lse4[h * blk:(h + 1) * blk, :], lse)
        o_ref[pl.ds(r0, blk), :] = o
        lse_ref[pl.ds(r0, blk), :] = lse
        return carry

    lax.fori_loop(0, R, body, 0)


def _attention(a, nbs):
    N = a.shape[0]
    R = ATT_BLOCKS_PER_STEP
    gw = ATT_GROUP_WIDTH
    tm = R * ATT_BLOCK
    return pl.pallas_call(
        functools.partial(_attn_kernel, nbs),
        grid=(N // tm,),
        in_specs=[pl.BlockSpec((tm, 3 * gw), lambda i: (i, 0)),
                  pl.BlockSpec((ATT_BLOCK, 3 * gw), lambda i: (jnp.maximum(i * R - 1, 0), 0))],
        out_specs=[pl.BlockSpec((tm, gw), lambda i: (i, 0))] * 2,
        out_shape=[jax.ShapeDtypeStruct((N, gw), F32)] * 2,
        scratch_shapes=[pltpu.VMEM((tm + ATT_BLOCK, 2 * gw), BF16)],
        name="attn",
    )(a, a)


def _mid_kernel(spt, u_ref, uh_ref, o0_ref, l0_ref, o1_ref, l1_ref, o2_ref, l2_ref, x_ref,
                pbd_ref, psc_ref, wout_ref, gate1_ref, gffn_ref, sh2_ref, sc2_ref, gate2_ref,
                wsg_ref, wsu_ref, wsd_ref, xacc_ref, h2_ref, ext_ref):
    i = pl.program_id(0)
    tm, pw = u_ref.shape
    si = i % spt
    u = u_ref[...]
    keep = jnp.full((POOL_HALO, pw), si, I32) > 0
    ext_ref[0:POOL_HALO, :] = jnp.where(keep, uh_ref[...], 0.0)
    ext_ref[POOL_HALO:, :] = u
    lane_grp = lax.broadcasted_iota(I32, (tm, pw), 1) // (pw // len(POOL_WINDOWS))
    s = u
    pooled = jnp.zeros((tm, pw), F32)
    for j in range(1, POOL_HALO):
        s = s + ext_ref[pl.ds(POOL_HALO - j, tm), :]
        if (j + 1) in POOL_WINDOWS:
            pooled = jnp.where(lane_grp == POOL_WINDOWS.index(j + 1), s, pooled)
    win = jnp.zeros((tm, pw), I32)
    for g, w in enumerate(POOL_WINDOWS):
        win = jnp.where(lane_grp == g, w, win)
    pos = si * tm + lax.broadcasted_iota(I32, (tm, pw), 0)
    cnt = jnp.minimum(pos + 1, win).astype(F32)
    pooled = pooled / cnt - u
    pool_out = jnp.dot(pooled.astype(BF16), pbd_ref[...], preferred_element_type=F32) * psc_ref[...]
    l0 = l0_ref[...]
    l1 = l1_ref[...]
    l2 = l2_ref[...]
    m = jnp.maximum(jnp.maximum(l0, l1), l2)
    w0 = jnp.exp(l0 - m)
    w1 = jnp.exp(l1 - m)
    w2 = jnp.exp(l2 - m)
    attn = (w0 * o0_ref[...] + w1 * o1_ref[...] + w2 * o2_ref[...]) / (w0 + w1 + w2)
    mixed = (jnp.dot(pool_out.astype(BF16), wout_ref[0:pw, :], preferred_element_type=F32)
             + jnp.dot(attn.astype(BF16), wout_ref[pw:, :], preferred_element_type=F32))
    x1 = x_ref[...] + gate1_ref[0] * mixed
    h2 = _rms(x1) * gffn_ref[...]
    h2 = h2 * (1.0 + sc2_ref[0]) + sh2_ref[0]
    h2_ref[...] = _pack_bf16_pairs(h2)
    hb = h2.astype(BF16)
    a = jnp.dot(hb, wsg_ref[...], preferred_element_type=F32)
    b = jnp.dot(hb, wsu_ref[...], preferred_element_type=F32)
    act = (a * _sigmoid(a)) * b
    shared = jnp.dot(act.astype(BF16), wsd_ref[...], preferred_element_type=F32)
    xacc_ref[...] = x1 + gate2_ref[0] * shared


def _mid(u, attn_outs, xf, pool_bd, pool_scale, w_out_b, gate1, g_ffn, shift2, scale2, gate2,
         wsg_b, wsu_b, wsd_b, S):
    N, D = xf.shape
    pw = u.shape[1]
    tm = MID_TILE
    spt = S // tm
    row = lambda i: (i, 0)
    vec = lambda i: (i // spt, 0, 0)
    full = lambda a: pl.BlockSpec(a.shape, lambda i: (0,) * a.ndim)
    hpt = tm // POOL_HALO
    in_specs = [pl.BlockSpec((tm, pw), row),
                pl.BlockSpec((POOL_HALO, pw), lambda i: (jnp.maximum(i * hpt - 1, 0), 0))]
    in_specs += [pl.BlockSpec((tm, ATT_GROUP_WIDTH), row)] * 6
    in_specs += [pl.BlockSpec((tm, D), row), full(pool_bd), full(pool_scale), full(w_out_b),
                 pl.BlockSpec((1, 1, D), vec), full(g_ffn), pl.BlockSpec((1, 1, D), vec),
                 pl.BlockSpec((1, 1, D), vec), pl.BlockSpec((1, 1, D), vec),
                 full(wsg_b), full(wsu_b), full(wsd_b)]
    return pl.pallas_call(
        functools.partial(_mid_kernel, spt),
        grid=(N // tm,),
        in_specs=in_specs,
        out_specs=[pl.BlockSpec((tm, D), row), pl.BlockSpec((tm, D // 2), row)],
        out_shape=[jax.ShapeDtypeStruct((N, D), F32), jax.ShapeDtypeStruct((N, D // 2), U32)],
        scratch_shapes=[pltpu.VMEM((tm + POOL_HALO, pw), F32)],
        name="mid",
    )(u, u, *attn_outs, xf, pool_bd, pool_scale, w_out_b, gate1, g_ffn, shift2, scale2, gate2,
      wsg_b, wsu_b, wsd_b)


def _route_kernel(h_ref, wr_ref, bias_ref, e_ref, g_ref, r_ref, cnt_ref, base_ref):
    i = pl.program_id(0)
    T = h_ref.shape[0]
    E = wr_ref.shape[0]
    gsz = E // N_EXPERT_GROUPS

    @pl.when(i == 0)
    def _():
        base_ref[...] = jnp.zeros_like(base_ref)

    h_lo, h_hi = _unpack_bf16_pairs(h_ref[...])
    half = h_lo.shape[1]
    nt = (((1,), (1,)), ((), ()))
    logits = (lax.dot_general(wr_ref[:, 0:half], h_lo, nt, preferred_element_type=F32)
              + lax.dot_general(wr_ref[:, half:], h_hi, nt, preferred_element_type=F32))
    scores = _sigmoid(logits)
    biased = scores + bias_ref[...]
    giota = lax.broadcasted_iota(I32, (gsz, T), 0)
    gscore = []
    for g in range(N_EXPERT_GROUPS):
        blk = biased[g * gsz:(g + 1) * gsz, :]
        m1 = jnp.max(blk, axis=0, keepdims=True)
        i1 = jnp.min(jnp.where(blk == m1, giota, gsz), axis=0, keepdims=True)
        m2 = jnp.max(jnp.where(giota == i1, NEG_INF, blk), axis=0, keepdims=True)
        gscore.append(m1 + m2)
    parts = []
    for g in range(N_EXPERT_GROUPS):
        beaten = jnp.zeros((1, T), I32)
        for o in range(N_EXPERT_GROUPS):
            if o == g:
                continue
            wins = (gscore[o] >= gscore[g]) if o < g else (gscore[o] > gscore[g])
            beaten = beaten + wins.astype(I32)
        keep = jnp.broadcast_to(beaten, (gsz, T)) < TOPK_GROUPS
        parts.append(jnp.where(keep, biased[g * gsz:(g + 1) * gsz, :], NEG_INF))
    cur = jnp.concatenate(parts, axis=0)
    eiota = lax.broadcasted_iota(I32, (E, T), 0)
    selm = jnp.zeros((E, T), F32)
    idxs, gates = [], []
    for k in range(TOP_K):
        m = jnp.max(cur, axis=0, keepdims=True)
        idx = jnp.min(jnp.where(cur == m, eiota, E), axis=0, keepdims=True)
        oh = eiota == idx
        gates.append(jnp.sum(jnp.where(oh, scores, 0.0), axis=0, keepdims=True))
        idxs.append(idx)
        cur = jnp.where(oh, NEG_INF, cur)
        selm = jnp.where(oh, 1.0, selm)
    gsum = gates[0]
    for k in range(1, TOP_K):
        gsum = gsum + gates[k]
    before = (lax.broadcasted_iota(I32, (T, T), 0) < lax.broadcasted_iota(I32, (T, T), 1)).astype(BF16)
    tot = jnp.dot(selm.astype(BF16), before, preferred_element_type=F32) + base_ref[...]
    for k in range(TOP_K):
        e_ref[k:k + 1, :] = idxs[k]
        g_ref[k:k + 1, :] = gates[k] / gsum * ROUTED_SCALE
        r_ref[k:k + 1, :] = jnp.sum(jnp.where(eiota == idxs[k], tot, 0.0), axis=0, keepdims=True).astype(I32)
    base_ref[...] = base_ref[...] + jnp.sum(selm, axis=1, keepdims=True)
    cnt_ref[...] = base_ref[...].astype(I32)


def _route(h2p, wr_t, bias_col):
    N = h2p.shape[0]
    E, D = wr_t.shape
    T = ROUTE_TILE
    col = lambda i: (0, i)
    return pl.pallas_call(
        _route_kernel,
        grid=(N // T,),
        in_specs=[pl.BlockSpec((T, D // 2), lambda i: (i, 0)),
                  pl.BlockSpec((E, D), lambda i: (0, 0)),
                  pl.BlockSpec((E, 1), lambda i: (0, 0))],
        out_specs=[pl.BlockSpec((TOP_K, T), col), pl.BlockSpec((TOP_K, T), col),
                   pl.BlockSpec((TOP_K, T), col), pl.BlockSpec((E, 1), lambda i: (0, 0))],
        out_shape=[jax.ShapeDtypeStruct((TOP_K, N), I32), jax.ShapeDtypeStruct((TOP_K, N), F32),
                   jax.ShapeDtypeStruct((TOP_K, N), I32), jax.ShapeDtypeStruct((E, 1), I32)],
        scratch_shapes=[pltpu.VMEM((E, 1), F32)],
        compiler_params=pltpu.CompilerParams(dimension_semantics=("arbitrary",)),
        name="route",
    )(h2p, wr_t, bias_col)


def _dest_kernel(e_ref, r_ref, off_ref, dc_ref):
    E = off_ref.shape[0]
    T = e_ref.shape[1]
    C = dc_ref.shape[2]
    eiota = lax.broadcasted_iota(I32, (E, T), 0)
    off = off_ref[...]
    for k in range(TOP_K):
        start = jnp.sum(jnp.where(eiota == e_ref[k:k + 1, :], off, 0.0), axis=0, keepdims=True)
        d = start.astype(I32) + r_ref[k:k + 1, :]
        for c in range(T // C):
            dc_ref[c, k:k + 1, :] = d[:, c * C:(c + 1) * C]


def _dest(eidx, rank, offs_col):
    N = eidx.shape[1]
    E = offs_col.shape[0]
    T = ROUTE_TILE
    C = SC_CHUNK
    col = lambda i: (0, i)
    return pl.pallas_call(
        _dest_kernel,
        grid=(N // T,),
        in_specs=[pl.BlockSpec((TOP_K, T), col), pl.BlockSpec((TOP_K, T), col),
                  pl.BlockSpec((E, 1), lambda i: (0, 0))],
        out_specs=pl.BlockSpec((T // C, TOP_K, C), lambda i: (i, 0, 0)),
        out_shape=jax.ShapeDtypeStruct((N // C, TOP_K, C), I32),
        name="dest",
    )(eidx, rank, offs_col)


def _sc_dispatch(dest_c, h2p, P):
    N, W = h2p.shape
    C = dest_c.shape[2]
    info = plsc.get_sparse_core_info()
    nw = info.num_cores * info.num_subcores
    per_w = N // C // nw
    mesh = plsc.VectorSubcoreMesh(core_axis_name="c", subcore_axis_name="s")

    @functools.partial(
        pl.kernel, mesh=mesh,
        out_type=jax.ShapeDtypeStruct((P, W), h2p.dtype),
        scratch_types=[pltpu.VMEM((TOP_K, C), I32), pltpu.VMEM((C, W), h2p.dtype), pltpu.SemaphoreType.DMA],
        name="sc_dispatch",
    )
    def k(dest_hbm, h_hbm, xs_hbm, idx_v, rows_v, sem):
        wid = lax.axis_index("s") * info.num_cores + lax.axis_index("c")

        @pl.loop(0, per_w)
        def _(j):
            ch = wid * per_w + j
            pltpu.sync_copy(dest_hbm.at[ch], idx_v)
            pltpu.sync_copy(h_hbm.at[pl.ds(ch * C, C)], rows_v)
            copies = [pltpu.async_copy(rows_v, xs_hbm.at[idx_v.at[kk]], sem) for kk in range(TOP_K)]
            for cp in copies:
                cp.wait()

    return k(dest_c, h2p)


def _gmm_kernel(bstart_ref, nbe_ref, cnt_ref, nu_ref, wg_ref, wu_ref, wd_ref, xs_hbm, ys_hbm,
                wgb, wub, wdb, xbuf, ybuf, xsem, ysem):
    e = pl.program_id(0)
    last = pl.num_programs(0) - 1
    bm = xbuf.shape[1]
    nblk = ys_hbm.shape[0] // bm
    b0 = bstart_ref[e]
    nb = nbe_ref[e]
    nused = nu_ref[0]

    def x_copy(b, slot):
        return pltpu.make_async_copy(xs_hbm.at[pl.ds(pl.multiple_of(b * bm, bm), bm), :], xbuf.at[slot],
                                     xsem.at[slot])

    def y_copy(b, slot):
        return pltpu.make_async_copy(ybuf.at[slot], ys_hbm.at[pl.ds(pl.multiple_of(b * bm, bm), bm), :],
                                     ysem.at[slot])

    @pl.when((e == 0) & (nused > 0))
    def _():
        x_copy(0, 0).start()

    @pl.when(nb > 0)
    def _():
        wgb[...] = wg_ref[0].astype(BF16)
        wub[...] = wu_ref[0].astype(BF16)
        wdb[...] = wd_ref[0].astype(BF16)
        rows = lax.broadcasted_iota(I32, (bm, 1), 0)

        def body(b, carry):
            slot = jnp.bitwise_and(b, 1)
            x_copy(b, slot).wait()

            @pl.when(b + 1 < nused)
            def _():
                x_copy(b + 1, 1 - slot).start()

            @pl.when(b >= 2)
            def _():
                y_copy(b - 2, slot).wait()

            valid = cnt_ref[e] - (b - b0) * bm
            x_lo, x_hi = _unpack_bf16_pairs(jnp.where(rows < valid, xbuf[slot], jnp.uint32(0)))
            half = x_lo.shape[1]
            a = (jnp.dot(x_lo, wgb[0:half, :], preferred_element_type=F32)
                 + jnp.dot(x_hi, wgb[half:, :], preferred_element_type=F32))
            g = (jnp.dot(x_lo, wub[0:half, :], preferred_element_type=F32)
                 + jnp.dot(x_hi, wub[half:, :], preferred_element_type=F32))
            act = (a * _sigmoid(a)) * g
            ybuf[slot] = _pack_bf16_pairs(jnp.dot(act.astype(BF16), wdb[...], preferred_element_type=F32))
            y_copy(b, slot).start()
            return carry

        lax.fori_loop(b0, b0 + nb, body, 0)

    @pl.when(e == last)
    def _():
        for back in (2, 1):
            @pl.when(nused >= back)
            def _():
                y_copy(nused - back, jnp.bitwise_and(nused - back, 1)).wait()
        ybuf[0] = jnp.zeros(ybuf.shape[1:], ybuf.dtype)
        lax.fori_loop(nused, nblk, lambda b, c: (y_copy(b, 0).start(), c)[1], 0)
        lax.fori_loop(nused, nblk, lambda b, c: (y_copy(b, 0).wait(), c)[1], 0)


def _gmm(bstart, nb_e, counts, nused, xs, w_gate, w_up, w_down):
    P, W = xs.shape
    E, D, F = w_gate.shape
    bm = GMM_BLOCK
    wsel = lambda e, *_: (e, 0, 0)
    grid_spec = pltpu.PrefetchScalarGridSpec(
        num_scalar_prefetch=4,
        grid=(E,),
        in_specs=[pl.BlockSpec((1, D, F), wsel), pl.BlockSpec((1, D, F), wsel), pl.BlockSpec((1, F, D), wsel),
                  pl.BlockSpec(memory_space=pl.ANY)],
        out_specs=pl.BlockSpec(memory_space=pl.ANY),
        scratch_shapes=[pltpu.VMEM((D, F), BF16), pltpu.VMEM((D, F), BF16), pltpu.VMEM((F, D), BF16),
                        pltpu.VMEM((2, bm, W), xs.dtype), pltpu.VMEM((2, bm, W), xs.dtype),
                        pltpu.SemaphoreType.DMA((2,)), pltpu.SemaphoreType.DMA((2,))],
    )
    return pl.pallas_call(
        _gmm_kernel,
        grid_spec=grid_spec,
        out_shape=jax.ShapeDtypeStruct((P, W), xs.dtype),
        compiler_params=pltpu.CompilerParams(dimension_semantics=("arbitrary",)),
        name="gmm",
    )(bstart, nb_e, counts, nused, w_gate, w_up, w_down, xs)


def _sc_gather(dest_c, ys):
    nch, K, C = dest_c.shape
    W = ys.shape[1]
    H = C // 2
    info = plsc.get_sparse_core_info()
    nw = info.num_cores * info.num_subcores
    per_w = nch // nw
    nbuf = 3
    mesh = plsc.VectorSubcoreMesh(core_axis_name="c", subcore_axis_name="s")
    items = [(kk, hh) for kk in range(K) for hh in range(2)]

    @functools.partial(
        pl.kernel, mesh=mesh,
        out_type=jax.ShapeDtypeStruct((K, nch * C, W), ys.dtype),
        scratch_types=([pltpu.VMEM((K, C), I32)] + [pltpu.VMEM((H, W), ys.dtype)] * nbuf
                       + [pltpu.SemaphoreType.DMA] * (2 * nbuf)),
        name="sc_gather",
    )
    def k(dest_hbm, ys_hbm, yk_hbm, idx_v, *rest):
        bufs, gsem, wsem = rest[:nbuf], rest[nbuf:2 * nbuf], rest[2 * nbuf:]
        wid = lax.axis_index("s") * info.num_cores + lax.axis_index("c")

        @pl.loop(0, per_w)
        def _(j):
            ch = wid * per_w + j
            pltpu.sync_copy(dest_hbm.at[ch], idx_v)

            def gather(i):
                kk, hh = items[i]
                return pltpu.async_copy(ys_hbm.at[idx_v.at[kk, pl.ds(hh * H, H)]], bufs[i % nbuf], gsem[i % nbuf])

            def write(i):
                kk, hh = items[i]
                return pltpu.async_copy(bufs[i % nbuf], yk_hbm.at[kk, pl.ds(ch * C + hh * H, H)], wsem[i % nbuf])

            n = len(items)
            g = {0: gather(0), 1: gather(1)}
            w = {}
            for i in range(n):
                g[i].wait()
                w[i] = write(i)
                if i + 2 < n:
                    if i >= 1:
                        w.pop(i - 1).wait()
                    g[i + 2] = gather(i + 2)
            for i in sorted(w):
                w[i].wait()

    return k(dest_c, ys)


def _combine_kernel(yk_ref, gt_ref, gate2_ref, xacc_ref, gfin_ref, o_ref):
    gt = gt_ref[...]
    hi_mask = jnp.uint32(0xFFFF0000)
    acc_lo = acc_hi = None
    for k in range(TOP_K):
        p = yk_ref[k]
        g = gt[:, k:k + 1]
        lo = lax.bitcast_convert_type(p << 16, F32) * g
        hi = lax.bitcast_convert_type(p & hi_mask, F32) * g
        acc_lo = lo if k == 0 else acc_lo + lo
        acc_hi = hi if k == 0 else acc_hi + hi
    routed = jnp.concatenate([acc_lo, acc_hi], axis=1)
    x2 = xacc_ref[...] + gate2_ref[0] * routed
    o_ref[...] = _rms(x2) * gfin_ref[...]


def _combine(yk, gates_t, gate2, xacc, g_final, S):
    N, D = xacc.shape
    W = yk.shape[2]
    T = ROUTE_TILE
    spt = S // T
    return pl.pallas_call(
        _combine_kernel,
        grid=(N // T,),
        in_specs=[pl.BlockSpec((TOP_K, T, W), lambda i: (0, i, 0)),
                  pl.BlockSpec((T, TOP_K), lambda i: (i, 0)),
                  pl.BlockSpec((1, 1, D), lambda i: (i // spt, 0, 0)),
                  pl.BlockSpec((T, D), lambda i: (i, 0)),
                  pl.BlockSpec((1, D), lambda i: (0, 0))],
        out_specs=pl.BlockSpec((T, D), lambda i: (i, 0)),
        out_shape=jax.ShapeDtypeStruct((N, D), F32),
        name="combine",
    )(yk, gates_t, gate2, xacc, g_final)


def _to_residue_major(a, B, S, d):
    if d == 1:
        return a
    w = a.shape[-1]
    return a.reshape(B, S // d, d, w).transpose(0, 2, 1, 3).reshape(B * S, w)


def _from_residue_major(a, B, S, d):
    if d == 1:
        return a
    w = a.shape[-1]
    return a.reshape(B, d, S // d, w).transpose(0, 2, 1, 3).reshape(B * S, w)


def _layer(xf, B, S, mod, g_mix, w_in, pool_w, pool_scale, w_out, g_ffn, w_router, router_bias,
           w_gate, w_up, w_down, ws_gate, ws_up, ws_down):
    N, D = xf.shape
    E = w_router.shape[1]
    pw = pool_scale.shape[0]
    shift1, scale1, gate1, shift2, scale2, gate2 = [m.reshape(B, 1, D) for m in jnp.split(mod, 6, axis=-1)]
    u, q0, q1, q2 = _in_proj(xf, g_mix.reshape(1, D), shift1, scale1, w_in.astype(BF16), S, pw)
    attn_outs = []
    for a, d in zip((q0, q1, q2), ATT_DILATIONS):
        o, lse = _attention(_to_residue_major(a, B, S, d), S // d // ATT_BLOCK)
        attn_outs += [_from_residue_major(o, B, S, d), _from_residue_major(lse, B, S, d)]
    ng = pool_w.shape[0]
    pool_bd = jnp.einsum('gcd,gh->gchd', pool_w, jnp.eye(ng, dtype=pool_w.dtype)).reshape(pw, pw).astype(BF16)
    xacc, h2 = _mid(u, attn_outs, xf, pool_bd, pool_scale.reshape(1, pw), w_out.astype(BF16), gate1,
                    g_ffn.reshape(1, D), shift2, scale2, gate2,
                    ws_gate.astype(BF16), ws_up.astype(BF16), ws_down.astype(BF16), S)
    eidx, gates, rank, counts = _route(h2, w_router.T.astype(BF16), router_bias.reshape(E, 1).astype(F32))
    bm = GMM_BLOCK
    nblk = N * TOP_K // bm + E
    nb_e = (counts[:, 0] + bm - 1) // bm
    bend = jnp.cumsum(nb_e)
    bstart = (bend - nb_e).astype(I32)
    nused = bend[-1:].astype(I32)
    dest_c = _dest(eidx, rank, (bstart * bm).astype(F32).reshape(E, 1))
    xs = _sc_dispatch(dest_c, h2, nblk * bm)
    ys = _gmm(bstart, nb_e.astype(I32), counts[:, 0], nused, xs, w_gate, w_up, w_down)
    return _sc_gather(dest_c, ys), gates.T, gate2, xacc


def kernel(x, c, w_ada, b_ada, g_mix, w_in, pool_w, pool_scale, w_out, g_ffn, w_router, router_bias,
           w_gate, w_up, w_down, ws_gate, ws_up, ws_down, g_final):
    B, S, D = x.shape
    depth = w_ada.shape[0]
    assert depth == 1, "the final residual is fused with the final norm, so exactly one layer is supported"
    assert S % (ATT_DILATIONS[-1] * ATT_BLOCK) == 0 and S % max(IN_TILE, MID_TILE, ATT_BLOCKS_PER_STEP * ATT_BLOCK) == 0
    xf = x.reshape(B * S, D)
    mod = _ada(c, w_ada[0], b_ada[0])
    yk, gates_t, gate2, xacc = _layer(
        xf, B, S, mod, g_mix[0], w_in[0], pool_w[0], pool_scale[0], w_out[0], g_ffn[0], w_router[0],
        router_bias[0], w_gate[0], w_up[0], w_down[0], ws_gate[0], ws_up[0], ws_down[0])
    out = _combine(yk, gates_t, gate2, xacc, g_final.reshape(1, D), S)
    return out.reshape(B, S, D)
```

```python
import functools

import jax
import jax.numpy as jnp
from jax import lax
from jax.experimental import pallas as pl
from jax.experimental.pallas import tpu as pltpu
from jax.experimental.pallas import tpu_sc as plsc

F32 = jnp.float32
BF16 = jnp.bfloat16
I32 = jnp.int32
U32 = jnp.uint32

NORM_EPS = 1e-6
POOL_WINDOWS = (2, 4, 8, 16)
POOL_HALO = 16
ATT_DILATIONS = (1, 4, 16)
ATT_BLOCK = 128
ATT_HEADS_PER_GROUP = 4
ATT_HEAD_DIM = 64
ATT_GROUP_WIDTH = ATT_HEADS_PER_GROUP * ATT_HEAD_DIM
N_EXPERT_GROUPS = 8
TOPK_GROUPS = 4
TOP_K = 8
ROUTED_SCALE = 2.5

IN_TILE = 512
ATT_BLOCKS_PER_STEP = 8
MID_TILE = 512
ROUTE_TILE = 256
GMM_BLOCK = 256
GMM_RING = 4
SC_CHUNK = 128

NEG_INF = float("-inf")


def _sigmoid(v):
    return 1.0 / (1.0 + jnp.exp(-v))


def _rms(v):
    return v * lax.rsqrt(jnp.mean(v * v, axis=-1, keepdims=True) + NORM_EPS)


def _pack_bf16_pairs(v):
    n = v.shape[1] // 2
    lo = lax.bitcast_convert_type(v[:, :n].astype(BF16).astype(F32), U32)
    hi = lax.bitcast_convert_type(v[:, n:].astype(BF16).astype(F32), U32)
    return (hi & jnp.uint32(0xFFFF0000)) | (lo >> 16)


def _unpack_bf16_pairs(p):
    lo = lax.bitcast_convert_type(p << 16, F32).astype(BF16)
    hi = lax.bitcast_convert_type(p & jnp.uint32(0xFFFF0000), F32).astype(BF16)
    return lo, hi


def _ada_kernel(c_ref, w_ref, b_ref, o_ref):
    c = c_ref[...]
    cs = c * _sigmoid(c)
    o_ref[...] = jnp.dot(cs, w_ref[...], preferred_element_type=F32,
                         precision=lax.Precision.HIGHEST) + b_ref[...]


def _ada(c, w_ada, b_ada):
    B, D = c.shape
    W = w_ada.shape[1]
    tn = 1024
    return pl.pallas_call(
        _ada_kernel,
        grid=(W // tn,),
        in_specs=[pl.BlockSpec((B, D), lambda j: (0, 0)),
                  pl.BlockSpec((D, tn), lambda j: (0, j)),
                  pl.BlockSpec((1, tn), lambda j: (0, j))],
        out_specs=pl.BlockSpec((B, tn), lambda j: (0, j)),
        out_shape=jax.ShapeDtypeStruct((B, W), F32),
        name="ada",
    )(c, w_ada, b_ada.reshape(1, W))


def _in_kernel(x_ref, g_ref, sh_ref, sc_ref, w_ref, pool_ref, q0_ref, q1_ref, q2_ref):
    h = _rms(x_ref[...]) * g_ref[...]
    h = h * (1.0 + sc_ref[0]) + sh_ref[0]
    hb = h.astype(BF16)
    pw = pool_ref.shape[1]
    gw = ATT_GROUP_WIDTH
    pool_ref[...] = jnp.dot(hb, w_ref[:, 0:pw], preferred_element_type=F32)
    for g, out in enumerate((q0_ref, q1_ref, q2_ref)):
        for sec in range(3):
            c0 = pw + sec * 3 * gw + g * gw
            out[:, sec * gw:(sec + 1) * gw] = jnp.dot(
                hb, w_ref[:, c0:c0 + gw], preferred_element_type=F32).astype(BF16)


def _in_proj(xf, g_mix, shift1, scale1, w_in_b, S, pool_width):
    N, D = xf.shape
    tm = IN_TILE
    spt = S // tm
    vec = lambda i: (i // spt, 0, 0)
    row = lambda i: (i, 0)
    gw3 = 3 * ATT_GROUP_WIDTH
    return pl.pallas_call(
        _in_kernel,
        grid=(N // tm,),
        in_specs=[pl.BlockSpec((tm, D), row),
                  pl.BlockSpec((1, D), lambda i: (0, 0)),
                  pl.BlockSpec((1, 1, D), vec),
                  pl.BlockSpec((1, 1, D), vec),
                  pl.BlockSpec(w_in_b.shape, lambda i: (0, 0))],
        out_specs=[pl.BlockSpec((tm, pool_width), row)] + [pl.BlockSpec((tm, gw3), row)] * 3,
        out_shape=[jax.ShapeDtypeStruct((N, pool_width), F32)] + [jax.ShapeDtypeStruct((N, gw3), BF16)] * 3,
        name="in_proj",
    )(xf, g_mix, shift1, scale1, w_in_b)


def _attn_kernel(nbs, a_ref, halo_ref, o_ref, lse_ref, kv_ref):
    i = pl.program_id(0)
    R = a_ref.shape[0] // ATT_BLOCK
    gw = ATT_GROUP_WIDTH
    blk = ATT_BLOCK
    nh = ATT_HEADS_PER_GROUP
    kv_ref[0:blk, :] = halo_ref[:, gw:3 * gw]
    kv_ref[blk:, :] = a_ref[:, gw:3 * gw]
    row = lax.broadcasted_iota(I32, (nh * blk, blk), 0) % blk
    col = lax.broadcasted_iota(I32, (nh * blk, blk), 1)
    head_of_lane = lax.broadcasted_iota(I32, (blk, gw), 1) // ATT_HEAD_DIM
    own_valid = col <= row
    nt = (((1,), (1,)), ((), ()))

    def body(jj, carry):
        r0 = pl.multiple_of(jj * blk, blk)
        r1 = pl.multiple_of(jj * blk + blk, blk)
        qf = a_ref[pl.ds(r0, blk), 0:gw].astype(F32)
        q4 = jnp.concatenate([jnp.where(head_of_lane == h, qf, 0.0) for h in range(nh)], axis=0).astype(BF16)
        kp = kv_ref[pl.ds(r0, blk), 0:gw]
        vp = kv_ref[pl.ds(r0, blk), gw:2 * gw]
        ko = kv_ref[pl.ds(r1, blk), 0:gw]
        vo = kv_ref[pl.ds(r1, blk), gw:2 * gw]
        scale = ATT_HEAD_DIM ** -0.5
        sp = lax.dot_general(q4, kp, nt, preferred_element_type=F32) * scale
        so = lax.dot_general(q4, ko, nt, preferred_element_type=F32) * scale
        first = ((i * R + jj) % nbs) == 0
        lo = row + jnp.where(first, blk, 0)
        sp = jnp.where(col >= lo, sp, NEG_INF)
        so = jnp.where(own_valid, so, NEG_INF)
        m = jnp.maximum(jnp.max(sp, axis=1, keepdims=True), jnp.max(so, axis=1, keepdims=True))
        pp = jnp.exp(sp - m)
        po = jnp.exp(so - m)
        l = jnp.sum(pp, axis=1, keepdims=True) + jnp.sum(po, axis=1, keepdims=True)
        o4 = (jnp.dot(pp.astype(BF16), vp, preferred_element_type=F32)
              + jnp.dot(po.astype(BF16), vo, preferred_element_type=F32)) / l
        lse4 = m + jnp.log(l)
        o = jnp.zeros((blk, gw), F32)
        lse = jnp.zeros((blk, gw), F32)
        for h in range(nh):
            hm = head_of_lane == h
            o = jnp.where(hm, o4[h * blk:(h + 1) * blk, :], o)
            lse = jnp.where(hm, lse4[h * blk:(h + 1) * blk, :], lse)
        o_ref[pl.ds(r0, blk), :] = o
        lse_ref[pl.ds(r0, blk), :] = lse
        return carry

    lax.fori_loop(0, R, body, 0)


def _attention(a, nbs):
    N = a.shape[0]
    R = ATT_BLOCKS_PER_STEP
    gw = ATT_GROUP_WIDTH
    tm = R * ATT_BLOCK
    return pl.pallas_call(
        functools.partial(_attn_kernel, nbs),
        grid=(N // tm,),
        in_specs=[pl.BlockSpec((tm, 3 * gw), lambda i: (i, 0)),
                  pl.BlockSpec((ATT_BLOCK, 3 * gw), lambda i: (jnp.maximum(i * R - 1, 0), 0))],
        out_specs=[pl.BlockSpec((tm, gw), lambda i: (i, 0))] * 2,
        out_shape=[jax.ShapeDtypeStruct((N, gw), F32)] * 2,
        scratch_shapes=[pltpu.VMEM((tm + ATT_BLOCK, 2 * gw), BF16)],
        name="attn",
    )(a, a)


def _mid_kernel(spt, u_ref, uh_ref, o0_ref, l0_ref, o1_ref, l1_ref, o2_ref, l2_ref, x_ref,
                pbd_ref, psc_ref, wout_ref, gate1_ref, gffn_ref, sh2_ref, sc2_ref, gate2_ref,
                wsg_ref, wsu_ref, wsd_ref, xacc_ref, h2_ref, ext_ref):
    i = pl.program_id(0)
    tm, pw = u_ref.shape
    si = i % spt
    u = u_ref[...]
    keep = jnp.full((POOL_HALO, pw), si, I32) > 0
    ext_ref[0:POOL_HALO, :] = jnp.where(keep, uh_ref[...], 0.0)
    ext_ref[POOL_HALO:, :] = u
    lane_grp = lax.broadcasted_iota(I32, (tm, pw), 1) // (pw // len(POOL_WINDOWS))
    s = u
    pooled = jnp.zeros((tm, pw), F32)
    for j in range(1, POOL_HALO):
        s = s + ext_ref[pl.ds(POOL_HALO - j, tm), :]
        if (j + 1) in POOL_WINDOWS:
            pooled = jnp.where(lane_grp == POOL_WINDOWS.index(j + 1), s, pooled)
    win = jnp.zeros((tm, pw), I32)
    for g, w in enumerate(POOL_WINDOWS):
        win = jnp.where(lane_grp == g, w, win)
    pos = si * tm + lax.broadcasted_iota(I32, (tm, pw), 0)
    cnt = jnp.minimum(pos + 1, win).astype(F32)
    pooled = pooled / cnt - u
    pool_out = jnp.dot(pooled.astype(BF16), pbd_ref[...], preferred_element_type=F32) * psc_ref[...]
    l0 = l0_ref[...]
    l1 = l1_ref[...]
    l2 = l2_ref[...]
    m = jnp.maximum(jnp.maximum(l0, l1), l2)
    w0 = jnp.exp(l0 - m)
    w1 = jnp.exp(l1 - m)
    w2 = jnp.exp(l2 - m)
    attn = (w0 * o0_ref[...] + w1 * o1_ref[...] + w2 * o2_ref[...]) / (w0 + w1 + w2)
    mixed = (jnp.dot(pool_out.astype(BF16), wout_ref[0:pw, :], preferred_element_type=F32)
             + jnp.dot(attn.astype(BF16), wout_ref[pw:, :], preferred_element_type=F32))
    x1 = x_ref[...] + gate1_ref[0] * mixed
    h2 = _rms(x1) * gffn_ref[...]
    h2 = h2 * (1.0 + sc2_ref[0]) + sh2_ref[0]
    h2_ref[...] = _pack_bf16_pairs(h2)
    hb = h2.astype(BF16)
    a = jnp.dot(hb, wsg_ref[...], preferred_element_type=F32)
    b = jnp.dot(hb, wsu_ref[...], preferred_element_type=F32)
    act = (a * _sigmoid(a)) * b
    shared = jnp.dot(act.astype(BF16), wsd_ref[...], preferred_element_type=F32)
    xacc_ref[...] = x1 + gate2_ref[0] * shared


def _mid(u, attn_outs, xf, pool_bd, pool_scale, w_out_b, gate1, g_ffn, shift2, scale2, gate2,
         wsg_b, wsu_b, wsd_b, S):
    N, D = xf.shape
    pw = u.shape[1]
    tm = MID_TILE
    spt = S // tm
    row = lambda i: (i, 0)
    vec = lambda i: (i // spt, 0, 0)
    full = lambda a: pl.BlockSpec(a.shape, lambda i: (0,) * a.ndim)
    hpt = tm // POOL_HALO
    in_specs = [pl.BlockSpec((tm, pw), row),
                pl.BlockSpec((POOL_HALO, pw), lambda i: (jnp.maximum(i * hpt - 1, 0), 0))]
    in_specs += [pl.BlockSpec((tm, ATT_GROUP_WIDTH), row)] * 6
    in_specs += [pl.BlockSpec((tm, D), row), full(pool_bd), full(pool_scale), full(w_out_b),
                 pl.BlockSpec((1, 1, D), vec), full(g_ffn), pl.BlockSpec((1, 1, D), vec),
                 pl.BlockSpec((1, 1, D), vec), pl.BlockSpec((1, 1, D), vec),
                 full(wsg_b), full(wsu_b), full(wsd_b)]
    return pl.pallas_call(
        functools.partial(_mid_kernel, spt),
        grid=(N // tm,),
        in_specs=in_specs,
        out_specs=[pl.BlockSpec((tm, D), row), pl.BlockSpec((tm, D // 2), row)],
        out_shape=[jax.ShapeDtypeStruct((N, D), F32), jax.ShapeDtypeStruct((N, D // 2), U32)],
        scratch_shapes=[pltpu.VMEM((tm + POOL_HALO, pw), F32)],
        name="mid",
    )(u, u, *attn_outs, xf, pool_bd, pool_scale, w_out_b, gate1, g_ffn, shift2, scale2, gate2,
      wsg_b, wsu_b, wsd_b)


def _route_kernel(h_ref, wr_ref, bias_ref, e_ref, g_ref, r_ref, cnt_ref, base_ref):
    i = pl.program_id(0)
    T = h_ref.shape[0]
    E = wr_ref.shape[0]
    gsz = E // N_EXPERT_GROUPS

    @pl.when(i == 0)
    def _():
        base_ref[...] = jnp.zeros_like(base_ref)

    h_lo, h_hi = _unpack_bf16_pairs(h_ref[...])
    half = h_lo.shape[1]
    nt = (((1,), (1,)), ((), ()))
    logits = (lax.dot_general(wr_ref[:, 0:half], h_lo, nt, preferred_element_type=F32)
              + lax.dot_general(wr_ref[:, half:], h_hi, nt, preferred_element_type=F32))
    scores = _sigmoid(logits)
    biased = scores + bias_ref[...]
    giota = lax.broadcasted_iota(I32, (gsz, T), 0)
    gscore = []
    for g in range(N_EXPERT_GROUPS):
        blk = biased[g * gsz:(g + 1) * gsz, :]
        m1 = jnp.max(blk, axis=0, keepdims=True)
        i1 = jnp.min(jnp.where(blk == m1, giota, gsz), axis=0, keepdims=True)
        m2 = jnp.max(jnp.where(giota == i1, NEG_INF, blk), axis=0, keepdims=True)
        gscore.append(m1 + m2)
    parts = []
    for g in range(N_EXPERT_GROUPS):
        beaten = jnp.zeros((1, T), I32)
        for o in range(N_EXPERT_GROUPS):
            if o == g:
                continue
            wins = (gscore[o] >= gscore[g]) if o < g else (gscore[o] > gscore[g])
            beaten = beaten + wins.astype(I32)
        keep = jnp.broadcast_to(beaten, (gsz, T)) < TOPK_GROUPS
        parts.append(jnp.where(keep, biased[g * gsz:(g + 1) * gsz, :], NEG_INF))
    cur = jnp.concatenate(parts, axis=0)
    eiota = lax.broadcasted_iota(I32, (E, T), 0)
    selm = jnp.zeros((E, T), F32)
    idxs, gates = [], []
    for k in range(TOP_K):
        m = jnp.max(cur, axis=0, keepdims=True)
        idx = jnp.min(jnp.where(cur == m, eiota, E), axis=0, keepdims=True)
        oh = eiota == idx
        gates.append(jnp.sum(jnp.where(oh, scores, 0.0), axis=0, keepdims=True))
        idxs.append(idx)
        cur = jnp.where(oh, NEG_INF, cur)
        selm = jnp.where(oh, 1.0, selm)
    gsum = gates[0]
    for k in range(1, TOP_K):
        gsum = gsum + gates[k]
    before = (lax.broadcasted_iota(I32, (T, T), 0) < lax.broadcasted_iota(I32, (T, T), 1)).astype(BF16)
    tot = jnp.dot(selm.astype(BF16), before, preferred_element_type=F32) + base_ref[...]
    for k in range(TOP_K):
        e_ref[k:k + 1, :] = idxs[k]
        g_ref[k:k + 1, :] = gates[k] / gsum * ROUTED_SCALE
        r_ref[k:k + 1, :] = jnp.sum(jnp.where(eiota == idxs[k], tot, 0.0), axis=0, keepdims=True).astype(I32)
    base_ref[...] = base_ref[...] + jnp.sum(selm, axis=1, keepdims=True)
    cnt_ref[...] = base_ref[...].astype(I32)


def _route(h2p, wr_t, bias_col):
    N = h2p.shape[0]
    E, D = wr_t.shape
    T = ROUTE_TILE
    col = lambda i: (0, i)
    return pl.pallas_call(
        _route_kernel,
        grid=(N // T,),
        in_specs=[pl.BlockSpec((T, D // 2), lambda i: (i, 0)),
                  pl.BlockSpec((E, D), lambda i: (0, 0)),
                  pl.BlockSpec((E, 1), lambda i: (0, 0))],
        out_specs=[pl.BlockSpec((TOP_K, T), col), pl.BlockSpec((TOP_K, T), col),
                   pl.BlockSpec((TOP_K, T), col), pl.BlockSpec((E, 1), lambda i: (0, 0))],
        out_shape=[jax.ShapeDtypeStruct((TOP_K, N), I32), jax.ShapeDtypeStruct((TOP_K, N), F32),
                   jax.ShapeDtypeStruct((TOP_K, N), I32), jax.ShapeDtypeStruct((E, 1), I32)],
        scratch_shapes=[pltpu.VMEM((E, 1), F32)],
        compiler_params=pltpu.CompilerParams(dimension_semantics=("arbitrary",)),
        name="route",
    )(h2p, wr_t, bias_col)


def _dest_kernel(e_ref, r_ref, off_ref, dc_ref):
    E = off_ref.shape[0]
    T = e_ref.shape[1]
    C = dc_ref.shape[2]
    eiota = lax.broadcasted_iota(I32, (E, T), 0)
    off = off_ref[...]
    for k in range(TOP_K):
        start = jnp.sum(jnp.where(eiota == e_ref[k:k + 1, :], off, 0.0), axis=0, keepdims=True)
        d = start.astype(I32) + r_ref[k:k + 1, :]
        for c in range(T // C):
            dc_ref[c, k:k + 1, :] = d[:, c * C:(c + 1) * C]


def _dest(eidx, rank, offs_col):
    N = eidx.shape[1]
    E = offs_col.shape[0]
    T = ROUTE_TILE
    C = SC_CHUNK
    col = lambda i: (0, i)
    return pl.pallas_call(
        _dest_kernel,
        grid=(N // T,),
        in_specs=[pl.BlockSpec((TOP_K, T), col), pl.BlockSpec((TOP_K, T), col),
                  pl.BlockSpec((E, 1), lambda i: (0, 0))],
        out_specs=pl.BlockSpec((T // C, TOP_K, C), lambda i: (i, 0, 0)),
        out_shape=jax.ShapeDtypeStruct((N // C, TOP_K, C), I32),
        name="dest",
    )(eidx, rank, offs_col)


def _sc_dispatch(dest_c, h2p, P):
    N, W = h2p.shape
    C = dest_c.shape[2]
    info = plsc.get_sparse_core_info()
    nw = info.num_cores * info.num_subcores
    per_w = N // C // nw
    mesh = plsc.VectorSubcoreMesh(core_axis_name="c", subcore_axis_name="s")

    @functools.partial(
        pl.kernel, mesh=mesh,
        out_type=jax.ShapeDtypeStruct((P, W), h2p.dtype),
        scratch_types=[pltpu.VMEM((TOP_K, C), I32), pltpu.VMEM((C, W), h2p.dtype), pltpu.SemaphoreType.DMA],
        name="sc_dispatch",
    )
    def k(dest_hbm, h_hbm, xs_hbm, idx_v, rows_v, sem):
        wid = lax.axis_index("s") * info.num_cores + lax.axis_index("c")

        @pl.loop(0, per_w)
        def _(j):
            ch = wid * per_w + j
            pltpu.sync_copy(dest_hbm.at[ch], idx_v)
            pltpu.sync_copy(h_hbm.at[pl.ds(ch * C, C)], rows_v)
            copies = [pltpu.async_copy(rows_v, xs_hbm.at[idx_v.at[kk]], sem) for kk in range(TOP_K)]
            for cp in copies:
                cp.wait()

    return k(dest_c, h2p)


def _gmm_kernel(bstart_ref, nbe_ref, cnt_ref, nu_ref, wg_ref, wu_ref, wd_ref, xs_hbm, ys_hbm,
                wgb, wub, wdb, xbuf, ybuf, xsem, ysem):
    e = pl.program_id(0)
    last = pl.num_programs(0) - 1
    ring, bm = xbuf.shape[0], xbuf.shape[1]
    nblk = ys_hbm.shape[0] // bm
    b0 = bstart_ref[e]
    nb = nbe_ref[e]
    nused = nu_ref[0]

    def x_copy(b, slot):
        return pltpu.make_async_copy(xs_hbm.at[pl.ds(pl.multiple_of(b * bm, bm), bm), :], xbuf.at[slot],
                                     xsem.at[slot])

    def y_copy(b, slot):
        return pltpu.make_async_copy(ybuf.at[slot], ys_hbm.at[pl.ds(pl.multiple_of(b * bm, bm), bm), :],
                                     ysem.at[slot])

    @pl.when(e == 0)
    def _():
        for j in range(ring - 1):
            @pl.when(j < nused)
            def _():
                x_copy(j, j).start()

    @pl.when(nb > 0)
    def _():
        wgb[...] = wg_ref[0].astype(BF16)
        wub[...] = wu_ref[0].astype(BF16)
        wdb[...] = wd_ref[0].astype(BF16)
        rows = lax.broadcasted_iota(I32, (bm, 1), 0)

        def body(b, carry):
            slot = jnp.bitwise_and(b, ring - 1)
            x_copy(b, slot).wait()
            ahead = b + (ring - 1)

            @pl.when(ahead < nused)
            def _():
                x_copy(ahead, jnp.bitwise_and(ahead, ring - 1)).start()

            @pl.when(b >= ring)
            def _():
                y_copy(b - ring, slot).wait()

            valid = cnt_ref[e] - (b - b0) * bm
            x_lo, x_hi = _unpack_bf16_pairs(jnp.where(rows < valid, xbuf[slot], jnp.uint32(0)))
            half = x_lo.shape[1]
            a = (jnp.dot(x_lo, wgb[0:half, :], preferred_element_type=F32)
                 + jnp.dot(x_hi, wgb[half:, :], preferred_element_type=F32))
            g = (jnp.dot(x_lo, wub[0:half, :], preferred_element_type=F32)
                 + jnp.dot(x_hi, wub[half:, :], preferred_element_type=F32))
            act = (a * _sigmoid(a)) * g
            ybuf[slot] = _pack_bf16_pairs(jnp.dot(act.astype(BF16), wdb[...], preferred_element_type=F32))
            y_copy(b, slot).start()
            return carry

        lax.fori_loop(b0, b0 + nb, body, 0)

    @pl.when(e == last)
    def _():
        for back in range(ring, 0, -1):
            @pl.when(nused >= back)
            def _():
                y_copy(nused - back, jnp.bitwise_and(nused - back, ring - 1)).wait()
        ybuf[0] = jnp.zeros(ybuf.shape[1:], ybuf.dtype)
        lax.fori_loop(nused, nblk, lambda b, c: (y_copy(b, 0).start(), c)[1], 0)
        lax.fori_loop(nused, nblk, lambda b, c: (y_copy(b, 0).wait(), c)[1], 0)


def _gmm(bstart, nb_e, counts, nused, xs, w_gate, w_up, w_down):
    P, W = xs.shape
    E, D, F = w_gate.shape
    bm = GMM_BLOCK
    wsel = lambda e, *_: (e, 0, 0)
    grid_spec = pltpu.PrefetchScalarGridSpec(
        num_scalar_prefetch=4,
        grid=(E,),
        in_specs=[pl.BlockSpec((1, D, F), wsel), pl.BlockSpec((1, D, F), wsel), pl.BlockSpec((1, F, D), wsel),
                  pl.BlockSpec(memory_space=pl.ANY)],
        out_specs=pl.BlockSpec(memory_space=pl.ANY),
        scratch_shapes=[pltpu.VMEM((D, F), BF16), pltpu.VMEM((D, F), BF16), pltpu.VMEM((F, D), BF16),
                        pltpu.VMEM((GMM_RING, bm, W), xs.dtype), pltpu.VMEM((GMM_RING, bm, W), xs.dtype),
                        pltpu.SemaphoreType.DMA((GMM_RING,)), pltpu.SemaphoreType.DMA((GMM_RING,))],
    )
    return pl.pallas_call(
        _gmm_kernel,
        grid_spec=grid_spec,
        out_shape=jax.ShapeDtypeStruct((P, W), xs.dtype),
        compiler_params=pltpu.CompilerParams(dimension_semantics=("arbitrary",)),
        name="gmm",
    )(bstart, nb_e, counts, nused, w_gate, w_up, w_down, xs)


def _sc_gather(dest_c, ys):
    nch, K, C = dest_c.shape
    W = ys.shape[1]
    H = C // 2
    info = plsc.get_sparse_core_info()
    nw = info.num_cores * info.num_subcores
    per_w = nch // nw
    nbuf = 3
    mesh = plsc.VectorSubcoreMesh(core_axis_name="c", subcore_axis_name="s")
    items = [(kk, hh) for kk in range(K) for hh in range(2)]

    @functools.partial(
        pl.kernel, mesh=mesh,
        out_type=jax.ShapeDtypeStruct((K, nch * C, W), ys.dtype),
        scratch_types=([pltpu.VMEM((K, C), I32)] + [pltpu.VMEM((H, W), ys.dtype)] * nbuf
                       + [pltpu.SemaphoreType.DMA] * (2 * nbuf)),
        name="sc_gather",
    )
    def k(dest_hbm, ys_hbm, yk_hbm, idx_v, *rest):
        bufs, gsem, wsem = rest[:nbuf], rest[nbuf:2 * nbuf], rest[2 * nbuf:]
        wid = lax.axis_index("s") * info.num_cores + lax.axis_index("c")

        @pl.loop(0, per_w)
        def _(j):
            ch = wid * per_w + j
            pltpu.sync_copy(dest_hbm.at[ch], idx_v)

            def gather(i):
                kk, hh = items[i]
                return pltpu.async_copy(ys_hbm.at[idx_v.at[kk, pl.ds(hh * H, H)]], bufs[i % nbuf], gsem[i % nbuf])

            def write(i):
                kk, hh = items[i]
                return pltpu.async_copy(bufs[i % nbuf], yk_hbm.at[kk, pl.ds(ch * C + hh * H, H)], wsem[i % nbuf])

            n = len(items)
            g = {0: gather(0), 1: gather(1)}
            w = {}
            for i in range(n):
                g[i].wait()
                w[i] = write(i)
                if i + 2 < n:
                    if i >= 1:
                        w.pop(i - 1).wait()
                    g[i + 2] = gather(i + 2)
            for i in sorted(w):
                w[i].wait()

    return k(dest_c, ys)


def _combine_kernel(yk_ref, gt_ref, gate2_ref, xacc_ref, gfin_ref, o_ref):
    gt = gt_ref[...]
    hi_mask = jnp.uint32(0xFFFF0000)
    acc_lo = acc_hi = None
    for k in range(TOP_K):
        p = yk_ref[k]
        g = gt[:, k:k + 1]
        lo = lax.bitcast_convert_type(p << 16, F32) * g
        hi = lax.bitcast_convert_type(p & hi_mask, F32) * g
        acc_lo = lo if k == 0 else acc_lo + lo
        acc_hi = hi if k == 0 else acc_hi + hi
    routed = jnp.concatenate([acc_lo, acc_hi], axis=1)
    x2 = xacc_ref[...] + gate2_ref[0] * routed
    o_ref[...] = _rms(x2) * gfin_ref[...]


def _combine(yk, gates_t, gate2, xacc, g_final, S):
    N, D = xacc.shape
    W = yk.shape[2]
    T = ROUTE_TILE
    spt = S // T
    return pl.pallas_call(
        _combine_kernel,
        grid=(N // T,),
        in_specs=[pl.BlockSpec((TOP_K, T, W), lambda i: (0, i, 0)),
                  pl.BlockSpec((T, TOP_K), lambda i: (i, 0)),
                  pl.BlockSpec((1, 1, D), lambda i: (i // spt, 0, 0)),
                  pl.BlockSpec((T, D), lambda i: (i, 0)),
                  pl.BlockSpec((1, D), lambda i: (0, 0))],
        out_specs=pl.BlockSpec((T, D), lambda i: (i, 0)),
        out_shape=jax.ShapeDtypeStruct((N, D), F32),
        name="combine",
    )(yk, gates_t, gate2, xacc, g_final)


def _to_residue_major(a, B, S, d):
    if d == 1:
        return a
    w = a.shape[-1]
    return a.reshape(B, S // d, d, w).transpose(0, 2, 1, 3).reshape(B * S, w)


def _from_residue_major(a, B, S, d):
    if d == 1:
        return a
    w = a.shape[-1]
    return a.reshape(B, d, S // d, w).transpose(0, 2, 1, 3).reshape(B * S, w)


def _layer(xf, B, S, mod, g_mix, w_in, pool_w, pool_scale, w_out, g_ffn, w_router, router_bias,
           w_gate, w_up, w_down, ws_gate, ws_up, ws_down):
    N, D = xf.shape
    E = w_router.shape[1]
    pw = pool_scale.shape[0]
    shift1, scale1, gate1, shift2, scale2, gate2 = [m.reshape(B, 1, D) for m in jnp.split(mod, 6, axis=-1)]
    u, q0, q1, q2 = _in_proj(xf, g_mix.reshape(1, D), shift1, scale1, w_in.astype(BF16), S, pw)
    attn_outs = []
    for a, d in zip((q0, q1, q2), ATT_DILATIONS):
        o, lse = _attention(_to_residue_major(a, B, S, d), S // d // ATT_BLOCK)
        attn_outs += [_from_residue_major(o, B, S, d), _from_residue_major(lse, B, S, d)]
    ng = pool_w.shape[0]
    pool_bd = jnp.einsum('gcd,gh->gchd', pool_w, jnp.eye(ng, dtype=pool_w.dtype)).reshape(pw, pw).astype(BF16)
    xacc, h2 = _mid(u, attn_outs, xf, pool_bd, pool_scale.reshape(1, pw), w_out.astype(BF16), gate1,
                    g_ffn.reshape(1, D), shift2, scale2, gate2,
                    ws_gate.astype(BF16), ws_up.astype(BF16), ws_down.astype(BF16), S)
    eidx, gates, rank, counts = _route(h2, w_router.T.astype(BF16), router_bias.reshape(E, 1).astype(F32))
    bm = GMM_BLOCK
    nblk = N * TOP_K // bm + E
    nb_e = (counts[:, 0] + bm - 1) // bm
    bend = jnp.cumsum(nb_e)
    bstart = (bend - nb_e).astype(I32)
    nused = bend[-1:].astype(I32)
    dest_c = _dest(eidx, rank, (bstart * bm).astype(F32).reshape(E, 1))
    xs = _sc_dispatch(dest_c, h2, nblk * bm)
    ys = _gmm(bstart, nb_e.astype(I32), counts[:, 0], nused, xs, w_gate, w_up, w_down)
    return _sc_gather(dest_c, ys), gates.T, gate2, xacc


def kernel(x, c, w_ada, b_ada, g_mix, w_in, pool_w, pool_scale, w_out, g_ffn, w_router, router_bias,
           w_gate, w_up, w_down, ws_gate, ws_up, ws_down, g_final):
    B, S, D = x.shape
    depth = w_ada.shape[0]
    assert depth == 1, "the final residual is fused with the final norm, so exactly one layer is supported"
    assert S % (ATT_DILATIONS[-1] * ATT_BLOCK) == 0 and S % max(IN_TILE, MID_TILE, ATT_BLOCKS_PER_STEP * ATT_BLOCK) == 0
    xf = x.reshape(B * S, D)
    mod = _ada(c, w_ada[0], b_ada[0])
    yk, gates_t, gate2, xacc = _layer(
        xf, B, S, mod, g_mix[0], w_in[0], pool_w[0], pool_scale[0], w_out[0], g_ffn[0], w_router[0],
        router_bias[0], w_gate[0], w_up[0], w_down[0], ws_gate[0], ws_up[0], ws_down[0])
    out = _combine(yk, gates_t, gate2, xacc, g_final.reshape(1, D), S)
    return out.reshape(B, S, D)
```

```python
import functools

import jax
import jax.numpy as jnp
from jax import lax
from jax.experimental import pallas as pl
from jax.experimental.pallas import tpu as pltpu
from jax.experimental.pallas import tpu_sc as plsc

F32 = jnp.float32
BF16 = jnp.bfloat16
I32 = jnp.int32
U32 = jnp.uint32

LANES = 128
NORM_EPS = 1e-6
POOL_WINDOWS = (2, 4, 8, 16)
POOL_HALO = 16
ATT_DILATIONS = (1, 4, 16)
ATT_BLOCK = 128
ATT_HEADS_PER_GROUP = 4
ATT_HEAD_DIM = 64
ATT_GROUP_WIDTH = ATT_HEADS_PER_GROUP * ATT_HEAD_DIM
N_EXPERT_GROUPS = 8
TOPK_GROUPS = 4
TOP_K = 8
ROUTED_SCALE = 2.5

IN_TILE = 512
ATT_BLOCKS_PER_STEP = 8
MID_TILE = 512
ROUTE_TILE = 256
DEST_TILE = 1024
GMM_BLOCK = 256
GMM_RING = 4
SC_CHUNK = 128

NEG_INF = float("-inf")


def _sigmoid(v):
    return 1.0 / (1.0 + jnp.exp(-v))


def _rms(v):
    return v * lax.rsqrt(jnp.mean(v * v, axis=-1, keepdims=True) + NORM_EPS)


def _pack_bf16_pairs(v):
    n = v.shape[1] // 2
    lo = lax.bitcast_convert_type(v[:, :n].astype(BF16).astype(F32), U32)
    hi = lax.bitcast_convert_type(v[:, n:].astype(BF16).astype(F32), U32)
    return (hi & jnp.uint32(0xFFFF0000)) | (lo >> 16)


def _unpack_bf16_pairs(p):
    lo = lax.bitcast_convert_type(p << 16, F32).astype(BF16)
    hi = lax.bitcast_convert_type(p & jnp.uint32(0xFFFF0000), F32).astype(BF16)
    return lo, hi


def _ada_kernel(c_ref, w_ref, b_ref, o_ref):
    c = c_ref[...]
    cs = c * _sigmoid(c)
    o_ref[...] = jnp.dot(cs, w_ref[...], preferred_element_type=F32,
                         precision=lax.Precision.HIGHEST) + b_ref[...]


def _ada(c, w_ada, b_ada):
    B, D = c.shape
    W = w_ada.shape[1]
    tn = 1024
    return pl.pallas_call(
        _ada_kernel,
        grid=(W // tn,),
        in_specs=[pl.BlockSpec((B, D), lambda j: (0, 0)),
                  pl.BlockSpec((D, tn), lambda j: (0, j)),
                  pl.BlockSpec((1, tn), lambda j: (0, j))],
        out_specs=pl.BlockSpec((B, tn), lambda j: (0, j)),
        out_shape=jax.ShapeDtypeStruct((B, W), F32),
        name="ada",
    )(c, w_ada, b_ada.reshape(1, W))


def _in_kernel(x_ref, g_ref, sh_ref, sc_ref, w_ref, pool_ref, q0_ref, q1_ref, q2_ref, scr_ref):
    h = _rms(x_ref[...]) * g_ref[...]
    h = h * (1.0 + sc_ref[0]) + sh_ref[0]
    hb = h.astype(BF16)
    tm = x_ref.shape[0]
    pw = pool_ref.shape[1]
    gw = ATT_GROUP_WIDTH
    pool_ref[...] = jnp.dot(hb, w_ref[:, 0:pw], preferred_element_type=F32)
    for g, (out, d) in enumerate(zip((q0_ref, q1_ref, q2_ref), ATT_DILATIONS)):
        for sec in range(3):
            c0 = pw + sec * 3 * gw + g * gw
            res = jnp.dot(hb, w_ref[:, c0:c0 + gw], preferred_element_type=F32)
            if d == 1:
                out[:, sec * gw:(sec + 1) * gw] = res.astype(BF16)
            else:
                for c in range(gw // LANES):
                    scr_ref[c] = res[:, c * LANES:(c + 1) * LANES]
                for r in range(d):
                    for c in range(gw // LANES):
                        c1 = sec * gw + c * LANES
                        out[r, :, c1:c1 + LANES] = scr_ref[c, pl.ds(r, tm // d, stride=d), :].astype(BF16)


def _in_proj(xf, g_mix, shift1, scale1, w_in_b, S, pool_width):
    N, D = xf.shape
    tm = IN_TILE
    spt = S // tm
    vec = lambda i: (i // spt, 0, 0)
    row = lambda i: (i, 0)
    gw3 = 3 * ATT_GROUP_WIDTH
    B = N // S
    res_spec = lambda d: pl.BlockSpec((d, tm // d, gw3), lambda i: (i // spt, i % spt, 0))
    res_shape = lambda d: jax.ShapeDtypeStruct((B * d, S // d, gw3), BF16)
    outs = pl.pallas_call(
        _in_kernel,
        grid=(N // tm,),
        in_specs=[pl.BlockSpec((tm, D), row),
                  pl.BlockSpec((1, D), lambda i: (0, 0)),
                  pl.BlockSpec((1, 1, D), vec),
                  pl.BlockSpec((1, 1, D), vec),
                  pl.BlockSpec(w_in_b.shape, lambda i: (0, 0))],
        out_specs=[pl.BlockSpec((tm, pool_width), row), pl.BlockSpec((tm, gw3), row)]
                  + [res_spec(d) for d in ATT_DILATIONS[1:]],
        out_shape=[jax.ShapeDtypeStruct((N, pool_width), F32), jax.ShapeDtypeStruct((N, gw3), BF16)]
                  + [res_shape(d) for d in ATT_DILATIONS[1:]],
        scratch_shapes=[pltpu.VMEM((ATT_GROUP_WIDTH // LANES, tm, LANES), F32)],
        name="in_proj",
    )(xf, g_mix, shift1, scale1, w_in_b)
    return outs[0], [o.reshape(N, gw3) for o in outs[1:]]


def _attn_kernel(nbs, a_ref, halo_ref, o_ref, lse_ref, kv_ref):
    i = pl.program_id(0)
    R = a_ref.shape[0] // ATT_BLOCK
    gw = ATT_GROUP_WIDTH
    blk = ATT_BLOCK
    nh = ATT_HEADS_PER_GROUP
    kv_ref[0:blk, :] = halo_ref[:, gw:3 * gw]
    kv_ref[blk:, :] = a_ref[:, gw:3 * gw]
    row = lax.broadcasted_iota(I32, (nh * blk, 2 * blk), 0) % blk
    col = lax.broadcasted_iota(I32, (nh * blk, 2 * blk), 1)
    head_of_lane = lax.broadcasted_iota(I32, (blk, gw), 1) // ATT_HEAD_DIM
    hi = row + blk
    nt = (((1,), (1,)), ((), ()))

    def body(jj, carry):
        r0 = pl.multiple_of(jj * blk, blk)
        qf = a_ref[pl.ds(r0, blk), 0:gw].astype(F32)
        q4 = jnp.concatenate([jnp.where(head_of_lane == h, qf, 0.0) for h in range(nh)], axis=0).astype(BF16)
        kc = kv_ref[pl.ds(r0, 2 * blk), 0:gw]
        vc = kv_ref[pl.ds(r0, 2 * blk), gw:2 * gw]
        s = lax.dot_general(q4, kc, nt, preferred_element_type=F32) * (ATT_HEAD_DIM ** -0.5)
        first = ((i * R + jj) % nbs) == 0
        lo = jnp.maximum(row, jnp.where(first, blk, 0))
        s = jnp.where((col >= lo) & (col <= hi), s, NEG_INF)
        m = jnp.max(s, axis=1, keepdims=True)
        p = jnp.exp(s - m)
        l = jnp.sum(p, axis=1, keepdims=True)
        o4 = jnp.dot(p.astype(BF16), vc, preferred_element_type=F32) / l
        lse4 = m + jnp.log(l)
        o = jnp.zeros((blk, gw), F32)
        lse = jnp.zeros((blk, gw), F32)
        for h in range(nh):
            hm = head_of_lane == h
            o = jnp.where(hm, o4[h * blk:(h + 1) * blk, :], o)
            lse = jnp.where(hm, lse4[h * blk:(h + 1) * blk, :], lse)
        o_ref[pl.ds(r0, blk), :] = o
        lse_ref[pl.ds(r0, blk), :] = lse
        return carry

    lax.fori_loop(0, R, body, 0, unroll=2)


def _attention(a, nbs):
    N = a.shape[0]
    R = ATT_BLOCKS_PER_STEP
    gw = ATT_GROUP_WIDTH
    tm = R * ATT_BLOCK
    return pl.pallas_call(
        functools.partial(_attn_kernel, nbs),
        grid=(N // tm,),
        in_specs=[pl.BlockSpec((tm, 3 * gw), lambda i: (i, 0)),
                  pl.BlockSpec((ATT_BLOCK, 3 * gw), lambda i: (jnp.maximum(i * R - 1, 0), 0))],
        out_specs=[pl.BlockSpec((tm, gw), lambda i: (i, 0))] * 2,
        out_shape=[jax.ShapeDtypeStruct((N, gw), F32)] * 2,
        scratch_shapes=[pltpu.VMEM((tm + ATT_BLOCK, 2 * gw), BF16)],
        name="attn",
    )(a, a)


def _mid_kernel(spt, u_ref, uh_ref, o0_ref, l0_ref, o1_ref, l1_ref, o2_ref, l2_ref, x_ref,
                pbd_ref, psc_ref, wout_ref, gate1_ref, gffn_ref, sh2_ref, sc2_ref, gate2_ref,
                wsg_ref, wsu_ref, wsd_ref, xacc_ref, h2_ref, ext_ref, til_ref):
    i = pl.program_id(0)
    tm, pw = u_ref.shape
    si = i % spt
    u = u_ref[...]
    keep = jnp.full((POOL_HALO, pw), si, I32) > 0
    ext_ref[0:POOL_HALO, :] = jnp.where(keep, uh_ref[...], 0.0)
    ext_ref[POOL_HALO:, :] = u
    lane_grp = lax.broadcasted_iota(I32, (tm, pw), 1) // (pw // len(POOL_WINDOWS))
    s = u
    pooled = jnp.zeros((tm, pw), F32)
    for j in range(1, POOL_HALO):
        s = s + ext_ref[pl.ds(POOL_HALO - j, tm), :]
        if (j + 1) in POOL_WINDOWS:
            pooled = jnp.where(lane_grp == POOL_WINDOWS.index(j + 1), s, pooled)
    win = jnp.zeros((tm, pw), I32)
    for g, w in enumerate(POOL_WINDOWS):
        win = jnp.where(lane_grp == g, w, win)
    pos = si * tm + lax.broadcasted_iota(I32, (tm, pw), 0)
    cnt = jnp.minimum(pos + 1, win).astype(F32)
    pooled = pooled / cnt - u
    pool_out = jnp.dot(pooled.astype(BF16), pbd_ref[...], preferred_element_type=F32) * psc_ref[...]
    def token_order(slot, ref):
        d, n, w = ref.shape
        for r in range(d):
            for c in range(w // LANES):
                til_ref[slot, c, pl.ds(r, n, stride=d), :] = ref[r, :, c * LANES:(c + 1) * LANES]
        return jnp.concatenate([til_ref[slot, c] for c in range(w // LANES)], axis=1)

    l0 = l0_ref[...]
    l1 = token_order(0, l1_ref)
    l2 = token_order(1, l2_ref)
    m = jnp.maximum(jnp.maximum(l0, l1), l2)
    w0 = jnp.exp(l0 - m)
    w1 = jnp.exp(l1 - m)
    w2 = jnp.exp(l2 - m)
    attn = (w0 * o0_ref[...] + w1 * token_order(2, o1_ref) + w2 * token_order(3, o2_ref)) / (w0 + w1 + w2)
    mixed = (jnp.dot(pool_out.astype(BF16), wout_ref[0:pw, :], preferred_element_type=F32)
             + jnp.dot(attn.astype(BF16), wout_ref[pw:, :], preferred_element_type=F32))
    x1 = x_ref[...] + gate1_ref[0] * mixed
    h2 = _rms(x1) * gffn_ref[...]
    h2 = h2 * (1.0 + sc2_ref[0]) + sh2_ref[0]
    h2_ref[...] = _pack_bf16_pairs(h2)
    hb = h2.astype(BF16)
    a = jnp.dot(hb, wsg_ref[...], preferred_element_type=F32)
    b = jnp.dot(hb, wsu_ref[...], preferred_element_type=F32)
    act = (a * _sigmoid(a)) * b
    shared = jnp.dot(act.astype(BF16), wsd_ref[...], preferred_element_type=F32)
    xacc_ref[...] = x1 + gate2_ref[0] * shared


def _mid(u, attn_outs, xf, pool_bd, pool_scale, w_out_b, gate1, g_ffn, shift2, scale2, gate2,
         wsg_b, wsu_b, wsd_b, S):
    N, D = xf.shape
    pw = u.shape[1]
    tm = MID_TILE
    spt = S // tm
    row = lambda i: (i, 0)
    vec = lambda i: (i // spt, 0, 0)
    full = lambda a: pl.BlockSpec(a.shape, lambda i: (0,) * a.ndim)
    hpt = tm // POOL_HALO
    in_specs = [pl.BlockSpec((tm, pw), row),
                pl.BlockSpec((POOL_HALO, pw), lambda i: (jnp.maximum(i * hpt - 1, 0), 0))]
    gw = ATT_GROUP_WIDTH
    in_specs += [pl.BlockSpec((tm, gw), row)] * 2
    for d in ATT_DILATIONS[1:]:
        in_specs += [pl.BlockSpec((d, tm // d, gw), lambda i: (i // spt, i % spt, 0))] * 2
    in_specs += [pl.BlockSpec((tm, D), row), full(pool_bd), full(pool_scale), full(w_out_b),
                 pl.BlockSpec((1, 1, D), vec), full(g_ffn), pl.BlockSpec((1, 1, D), vec),
                 pl.BlockSpec((1, 1, D), vec), pl.BlockSpec((1, 1, D), vec),
                 full(wsg_b), full(wsu_b), full(wsd_b)]
    return pl.pallas_call(
        functools.partial(_mid_kernel, spt),
        grid=(N // tm,),
        in_specs=in_specs,
        out_specs=[pl.BlockSpec((tm, D), row), pl.BlockSpec((tm, D // 2), row)],
        out_shape=[jax.ShapeDtypeStruct((N, D), F32), jax.ShapeDtypeStruct((N, D // 2), U32)],
        scratch_shapes=[pltpu.VMEM((tm + POOL_HALO, pw), F32), pltpu.VMEM((4, gw // LANES, tm, LANES), F32)],
        name="mid",
    )(u, u, *attn_outs, xf, pool_bd, pool_scale, w_out_b, gate1, g_ffn, shift2, scale2, gate2,
      wsg_b, wsu_b, wsd_b)


def _route_kernel(h_ref, wr_ref, bias_ref, e_ref, g_ref, r_ref, cnt_ref, base_ref):
    i = pl.program_id(0)
    T = h_ref.shape[0]
    E = wr_ref.shape[0]
    gsz = E // N_EXPERT_GROUPS

    @pl.when(i == 0)
    def _():
        base_ref[...] = jnp.zeros_like(base_ref)

    h_lo, h_hi = _unpack_bf16_pairs(h_ref[...])
    half = h_lo.shape[1]
    nt = (((1,), (1,)), ((), ()))
    logits = (lax.dot_general(wr_ref[:, 0:half], h_lo, nt, preferred_element_type=F32)
              + lax.dot_general(wr_ref[:, half:], h_hi, nt, preferred_element_type=F32))
    scores = _sigmoid(logits)
    biased = scores + bias_ref[...]
    giota = lax.broadcasted_iota(I32, (gsz, T), 0)
    gscore = []
    for g in range(N_EXPERT_GROUPS):
        blk = biased[g * gsz:(g + 1) * gsz, :]
        m1 = jnp.max(blk, axis=0, keepdims=True)
        i1 = jnp.min(jnp.where(blk == m1, giota, gsz), axis=0, keepdims=True)
        m2 = jnp.max(jnp.where(giota == i1, NEG_INF, blk), axis=0, keepdims=True)
        gscore.append(m1 + m2)
    parts = []
    for g in range(N_EXPERT_GROUPS):
        beaten = jnp.zeros((1, T), I32)
        for o in range(N_EXPERT_GROUPS):
            if o == g:
                continue
            wins = (gscore[o] >= gscore[g]) if o < g else (gscore[o] > gscore[g])
            beaten = beaten + wins.astype(I32)
        keep = jnp.broadcast_to(beaten, (gsz, T)) < TOPK_GROUPS
        parts.append(jnp.where(keep, biased[g * gsz:(g + 1) * gsz, :], NEG_INF))
    cur = jnp.concatenate(parts, axis=0)
    eiota = lax.broadcasted_iota(I32, (E, T), 0)
    selm = jnp.zeros((E, T), F32)
    idxs, gates = [], []
    for k in range(TOP_K):
        m = jnp.max(cur, axis=0, keepdims=True)
        idx = jnp.min(jnp.where(cur == m, eiota, E), axis=0, keepdims=True)
        oh = eiota == idx
        gates.append(jnp.sum(jnp.where(oh, scores, 0.0), axis=0, keepdims=True))
        idxs.append(idx)
        cur = jnp.where(oh, NEG_INF, cur)
        selm = jnp.where(oh, 1.0, selm)
    gsum = gates[0]
    for k in range(1, TOP_K):
        gsum = gsum + gates[k]
    before = (lax.broadcasted_iota(I32, (T, T), 0) < lax.broadcasted_iota(I32, (T, T), 1)).astype(BF16)
    tot = jnp.dot(selm.astype(BF16), before, preferred_element_type=F32) + base_ref[...]
    for k in range(TOP_K):
        e_ref[k:k + 1, :] = idxs[k]
        g_ref[k:k + 1, :] = gates[k] / gsum * ROUTED_SCALE
        r_ref[k:k + 1, :] = jnp.sum(jnp.where(eiota == idxs[k], tot, 0.0), axis=0, keepdims=True).astype(I32)
    base_ref[...] = base_ref[...] + jnp.sum(selm, axis=1, keepdims=True)
    cnt_ref[...] = base_ref[...].astype(I32)


def _route(h2p, wr_t, bias_col):
    N = h2p.shape[0]
    E, D = wr_t.shape
    T = ROUTE_TILE
    col = lambda i: (0, i)
    return pl.pallas_call(
        _route_kernel,
        grid=(N // T,),
        in_specs=[pl.BlockSpec((T, D // 2), lambda i: (i, 0)),
                  pl.BlockSpec((E, D), lambda i: (0, 0)),
                  pl.BlockSpec((E, 1), lambda i: (0, 0))],
        out_specs=[pl.BlockSpec((TOP_K, T), col), pl.BlockSpec((TOP_K, T), col),
                   pl.BlockSpec((TOP_K, T), col), pl.BlockSpec((E, 1), lambda i: (0, 0))],
        out_shape=[jax.ShapeDtypeStruct((TOP_K, N), I32), jax.ShapeDtypeStruct((TOP_K, N), F32),
                   jax.ShapeDtypeStruct((TOP_K, N), I32), jax.ShapeDtypeStruct((E, 1), I32)],
        scratch_shapes=[pltpu.VMEM((E, 1), F32)],
        compiler_params=pltpu.CompilerParams(dimension_semantics=("arbitrary",)),
        name="route",
    )(h2p, wr_t, bias_col)


def _dest_kernel(e_ref, r_ref, off_ref, dc_ref):
    E = off_ref.shape[0]
    T = e_ref.shape[1]
    C = dc_ref.shape[2]
    eiota = lax.broadcasted_iota(I32, (E, T), 0)
    off = off_ref[...]
    for k in range(TOP_K):
        start = jnp.sum(jnp.where(eiota == e_ref[k:k + 1, :], off, 0.0), axis=0, keepdims=True)
        d = start.astype(I32) + r_ref[k:k + 1, :]
        for c in range(T // C):
            dc_ref[c, k:k + 1, :] = d[:, c * C:(c + 1) * C]


def _dest(eidx, rank, offs_col):
    N = eidx.shape[1]
    E = offs_col.shape[0]
    T = DEST_TILE
    C = SC_CHUNK
    col = lambda i: (0, i)
    return pl.pallas_call(
        _dest_kernel,
        grid=(N // T,),
        in_specs=[pl.BlockSpec((TOP_K, T), col), pl.BlockSpec((TOP_K, T), col),
                  pl.BlockSpec((E, 1), lambda i: (0, 0))],
        out_specs=pl.BlockSpec((T // C, TOP_K, C), lambda i: (i, 0, 0)),
        out_shape=jax.ShapeDtypeStruct((N // C, TOP_K, C), I32),
        name="dest",
    )(eidx, rank, offs_col)


def _sc_dispatch(dest_c, h2p, P):
    N, W = h2p.shape
    C = dest_c.shape[2]
    info = plsc.get_sparse_core_info()
    nw = info.num_cores * info.num_subcores
    per_w = N // C // nw
    mesh = plsc.VectorSubcoreMesh(core_axis_name="c", subcore_axis_name="s")

    @functools.partial(
        pl.kernel, mesh=mesh,
        out_type=jax.ShapeDtypeStruct((P, W), h2p.dtype),
        scratch_types=[pltpu.VMEM((TOP_K, C), I32), pltpu.VMEM((C, W), h2p.dtype), pltpu.SemaphoreType.DMA],
        name="sc_dispatch",
    )
    def k(dest_hbm, h_hbm, xs_hbm, idx_v, rows_v, sem):
        wid = lax.axis_index("s") * info.num_cores + lax.axis_index("c")

        @pl.loop(0, per_w)
        def _(j):
            ch = wid * per_w + j
            pltpu.sync_copy(dest_hbm.at[ch], idx_v)
            pltpu.sync_copy(h_hbm.at[pl.ds(ch * C, C)], rows_v)
            copies = [pltpu.async_copy(rows_v, xs_hbm.at[idx_v.at[kk]], sem) for kk in range(TOP_K)]
            for cp in copies:
                cp.wait()

    return k(dest_c, h2p)


def _gmm_kernel(bstart_ref, nbe_ref, cnt_ref, nu_ref, wg_ref, wu_ref, wd_ref, xs_hbm, ys_hbm,
                wgb, wub, wdb, xbuf, ybuf, xsem, ysem):
    e = pl.program_id(0)
    last = pl.num_programs(0) - 1
    ring, bm = xbuf.shape[0], xbuf.shape[1]
    nblk = ys_hbm.shape[0] // bm
    b0 = bstart_ref[e]
    nb = nbe_ref[e]
    nused = nu_ref[0]

    def x_copy(b, slot):
        return pltpu.make_async_copy(xs_hbm.at[pl.ds(pl.multiple_of(b * bm, bm), bm), :], xbuf.at[slot],
                                     xsem.at[slot])

    def y_copy(b, slot):
        return pltpu.make_async_copy(ybuf.at[slot], ys_hbm.at[pl.ds(pl.multiple_of(b * bm, bm), bm), :],
                                     ysem.at[slot])

    @pl.when(e == 0)
    def _():
        for j in range(ring - 1):
            @pl.when(j < nused)
            def _():
                x_copy(j, j).start()

    @pl.when(nb > 0)
    def _():
        wgb[...] = wg_ref[0].astype(BF16)
        wub[...] = wu_ref[0].astype(BF16)
        wdb[...] = wd_ref[0].astype(BF16)
        rows = lax.broadcasted_iota(I32, (bm, 1), 0)

        def body(b, carry):
            slot = jnp.bitwise_and(b, ring - 1)
            x_copy(b, slot).wait()
            ahead = b + (ring - 1)

            @pl.when(ahead < nused)
            def _():
                x_copy(ahead, jnp.bitwise_and(ahead, ring - 1)).start()

            @pl.when(b >= ring)
            def _():
                y_copy(b - ring, slot).wait()

            valid = cnt_ref[e] - (b - b0) * bm
            x_lo, x_hi = _unpack_bf16_pairs(jnp.where(rows < valid, xbuf[slot], jnp.uint32(0)))
            half = x_lo.shape[1]
            a = (jnp.dot(x_lo, wgb[0:half, :], preferred_element_type=F32)
                 + jnp.dot(x_hi, wgb[half:, :], preferred_element_type=F32))
            g = (jnp.dot(x_lo, wub[0:half, :], preferred_element_type=F32)
                 + jnp.dot(x_hi, wub[half:, :], preferred_element_type=F32))
            act = (a * _sigmoid(a)) * g
            ybuf[slot] = _pack_bf16_pairs(jnp.dot(act.astype(BF16), wdb[...], preferred_element_type=F32))
            y_copy(b, slot).start()
            return carry

        lax.fori_loop(b0, b0 + nb, body, 0)

    @pl.when(e == last)
    def _():
        for back in range(ring, 0, -1):
            @pl.when(nused >= back)
            def _():
                y_copy(nused - back, jnp.bitwise_and(nused - back, ring - 1)).wait()
        ybuf[0] = jnp.zeros(ybuf.shape[1:], ybuf.dtype)
        lax.fori_loop(nused, nblk, lambda b, c: (y_copy(b, 0).start(), c)[1], 0)
        lax.fori_loop(nused, nblk, lambda b, c: (y_copy(b, 0).wait(), c)[1], 0)


def _gmm(bstart, nb_e, counts, nused, xs, w_gate, w_up, w_down):
    P, W = xs.shape
    E, D, F = w_gate.shape
    bm = GMM_BLOCK
    wsel = lambda e, *_: (e, 0, 0)
    grid_spec = pltpu.PrefetchScalarGridSpec(
        num_scalar_prefetch=4,
        grid=(E,),
        in_specs=[pl.BlockSpec((1, D, F), wsel), pl.BlockSpec((1, D, F), wsel), pl.BlockSpec((1, F, D), wsel),
                  pl.BlockSpec(memory_space=pl.ANY)],
        out_specs=pl.BlockSpec(memory_space=pl.ANY),
        scratch_shapes=[pltpu.VMEM((D, F), BF16), pltpu.VMEM((D, F), BF16), pltpu.VMEM((F, D), BF16),
                        pltpu.VMEM((GMM_RING, bm, W), xs.dtype), pltpu.VMEM((GMM_RING, bm, W), xs.dtype),
                        pltpu.SemaphoreType.DMA((GMM_RING,)), pltpu.SemaphoreType.DMA((GMM_RING,))],
    )
    return pl.pallas_call(
        _gmm_kernel,
        grid_spec=grid_spec,
        out_shape=jax.ShapeDtypeStruct((P, W), xs.dtype),
        compiler_params=pltpu.CompilerParams(dimension_semantics=("arbitrary",)),
        name="gmm",
    )(bstart, nb_e, counts, nused, w_gate, w_up, w_down, xs)


def _sc_gather(dest_c, ys):
    nch, K, C = dest_c.shape
    W = ys.shape[1]
    H = C // 2
    info = plsc.get_sparse_core_info()
    nw = info.num_cores * info.num_subcores
    per_w = nch // nw
    nbuf = 3
    mesh = plsc.VectorSubcoreMesh(core_axis_name="c", subcore_axis_name="s")
    items = [(kk, hh) for kk in range(K) for hh in range(2)]

    @functools.partial(
        pl.kernel, mesh=mesh,
        out_type=jax.ShapeDtypeStruct((K, nch * C, W), ys.dtype),
        scratch_types=([pltpu.VMEM((K, C), I32)] + [pltpu.VMEM((H, W), ys.dtype)] * nbuf
                       + [pltpu.SemaphoreType.DMA] * (2 * nbuf)),
        name="sc_gather",
    )
    def k(dest_hbm, ys_hbm, yk_hbm, idx_v, *rest):
        bufs, gsem, wsem = rest[:nbuf], rest[nbuf:2 * nbuf], rest[2 * nbuf:]
        wid = lax.axis_index("s") * info.num_cores + lax.axis_index("c")

        @pl.loop(0, per_w)
        def _(j):
            ch = wid * per_w + j
            pltpu.sync_copy(dest_hbm.at[ch], idx_v)

            def gather(i):
                kk, hh = items[i]
                return pltpu.async_copy(ys_hbm.at[idx_v.at[kk, pl.ds(hh * H, H)]], bufs[i % nbuf], gsem[i % nbuf])

            def write(i):
                kk, hh = items[i]
                return pltpu.async_copy(bufs[i % nbuf], yk_hbm.at[kk, pl.ds(ch * C + hh * H, H)], wsem[i % nbuf])

            n = len(items)
            g = {0: gather(0), 1: gather(1)}
            w = {}
            for i in range(n):
                g[i].wait()
                w[i] = write(i)
                if i + 2 < n:
                    if i >= 1:
                        w.pop(i - 1).wait()
                    g[i + 2] = gather(i + 2)
            for i in sorted(w):
                w[i].wait()

    return k(dest_c, ys)


def _combine_kernel(yk_ref, gt_ref, gate2_ref, xacc_ref, gfin_ref, o_ref):
    gt = gt_ref[...]
    hi_mask = jnp.uint32(0xFFFF0000)
    acc_lo = acc_hi = None
    for k in range(TOP_K):
        p = yk_ref[k]
        g = gt[:, k:k + 1]
        lo = lax.bitcast_convert_type(p << 16, F32) * g
        hi = lax.bitcast_convert_type(p & hi_mask, F32) * g
        acc_lo = lo if k == 0 else acc_lo + lo
        acc_hi = hi if k == 0 else acc_hi + hi
    routed = jnp.concatenate([acc_lo, acc_hi], axis=1)
    x2 = xacc_ref[...] + gate2_ref[0] * routed
    o_ref[...] = _rms(x2) * gfin_ref[...]


def _combine(yk, gates_t, gate2, xacc, g_final, S):
    N, D = xacc.shape
    W = yk.shape[2]
    T = ROUTE_TILE
    spt = S // T
    return pl.pallas_call(
        _combine_kernel,
        grid=(N // T,),
        in_specs=[pl.BlockSpec((TOP_K, T, W), lambda i: (0, i, 0)),
                  pl.BlockSpec((T, TOP_K), lambda i: (i, 0)),
                  pl.BlockSpec((1, 1, D), lambda i: (i // spt, 0, 0)),
                  pl.BlockSpec((T, D), lambda i: (i, 0)),
                  pl.BlockSpec((1, D), lambda i: (0, 0))],
        out_specs=pl.BlockSpec((T, D), lambda i: (i, 0)),
        out_shape=jax.ShapeDtypeStruct((N, D), F32),
        name="combine",
    )(yk, gates_t, gate2, xacc, g_final)


def _layer(xf, B, S, mod, g_mix, w_in, pool_w, pool_scale, w_out, g_ffn, w_router, router_bias,
           w_gate, w_up, w_down, ws_gate, ws_up, ws_down):
    N, D = xf.shape
    E = w_router.shape[1]
    pw = pool_scale.shape[0]
    shift1, scale1, gate1, shift2, scale2, gate2 = [m.reshape(B, 1, D) for m in jnp.split(mod, 6, axis=-1)]
    u, qkv = _in_proj(xf, g_mix.reshape(1, D), shift1, scale1, w_in.astype(BF16), S, pw)
    attn_outs = []
    for a, d in zip(qkv, ATT_DILATIONS):
        o, lse = _attention(a, S // d // ATT_BLOCK)
        shape = (N, ATT_GROUP_WIDTH) if d == 1 else (B * d, S // d, ATT_GROUP_WIDTH)
        attn_outs += [o.reshape(shape), lse.reshape(shape)]
    ng = pool_w.shape[0]
    pool_bd = jnp.einsum('gcd,gh->gchd', pool_w, jnp.eye(ng, dtype=pool_w.dtype)).reshape(pw, pw).astype(BF16)
    xacc, h2 = _mid(u, attn_outs, xf, pool_bd, pool_scale.reshape(1, pw), w_out.astype(BF16), gate1,
                    g_ffn.reshape(1, D), shift2, scale2, gate2,
                    ws_gate.astype(BF16), ws_up.astype(BF16), ws_down.astype(BF16), S)
    eidx, gates, rank, counts = _route(h2, w_router.T.astype(BF16), router_bias.reshape(E, 1).astype(F32))
    bm = GMM_BLOCK
    nblk = N * TOP_K // bm + E
    nb_e = (counts[:, 0] + bm - 1) // bm
    bend = jnp.cumsum(nb_e)
    bstart = (bend - nb_e).astype(I32)
    nused = bend[-1:].astype(I32)
    dest_c = _dest(eidx, rank, (bstart * bm).astype(F32).reshape(E, 1))
    xs = _sc_dispatch(dest_c, h2, nblk * bm)
    ys = _gmm(bstart, nb_e.astype(I32), counts[:, 0], nused, xs, w_gate, w_up, w_down)
    return _sc_gather(dest_c, ys), gates.T, gate2, xacc


def kernel(x, c, w_ada, b_ada, g_mix, w_in, pool_w, pool_scale, w_out, g_ffn, w_router, router_bias,
           w_gate, w_up, w_down, ws_gate, ws_up, ws_down, g_final):
    B, S, D = x.shape
    depth = w_ada.shape[0]
    assert depth == 1, "the final residual is fused with the final norm, so exactly one layer is supported"
    assert S % (ATT_DILATIONS[-1] * ATT_BLOCK) == 0 and S % max(IN_TILE, MID_TILE, ATT_BLOCKS_PER_STEP * ATT_BLOCK) == 0
    xf = x.reshape(B * S, D)
    mod = _ada(c, w_ada[0], b_ada[0])
    yk, gates_t, gate2, xacc = _layer(
        xf, B, S, mod, g_mix[0], w_in[0], pool_w[0], pool_scale[0], w_out[0], g_ffn[0], w_router[0],
        router_bias[0], w_gate[0], w_up[0], w_down[0], ws_gate[0], ws_up[0], ws_down[0])
    out = _combine(yk, gates_t, gate2, xacc, g_final.reshape(1, D), S)
    return out.reshape(B, S, D)
```

```python
import functools

import jax
import jax.numpy as jnp
from jax import lax
from jax.experimental import pallas as pl
from jax.experimental.pallas import tpu as pltpu
from jax.experimental.pallas import tpu_sc as plsc

F32 = jnp.float32
BF16 = jnp.bfloat16
I32 = jnp.int32
U32 = jnp.uint32

LANES = 128
NORM_EPS = 1e-6
POOL_WINDOWS = (2, 4, 8, 16)
POOL_HALO = 16
ATT_DILATIONS = (1, 4, 16)
ATT_BLOCK = 128
ATT_HEADS_PER_GROUP = 4
ATT_HEAD_DIM = 64
ATT_GROUP_WIDTH = ATT_HEADS_PER_GROUP * ATT_HEAD_DIM
N_EXPERT_GROUPS = 8
TOPK_GROUPS = 4
TOP_K = 8
ROUTED_SCALE = 2.5

IN_TILE = 512
ATT_BLOCKS_PER_STEP = 8
MID_TILE = 512
ROUTE_TILE = 256
DEST_TILE = 1024
GMM_BLOCK = 256
GMM_RING = 4
SC_CHUNK = 128

NEG_INF = float("-inf")


def _sigmoid(v):
    return 1.0 / (1.0 + jnp.exp(-v))


def _rms(v):
    return v * lax.rsqrt(jnp.mean(v * v, axis=-1, keepdims=True) + NORM_EPS)


def _pack_bf16_pairs(v):
    n = v.shape[1] // 2
    lo = lax.bitcast_convert_type(v[:, :n].astype(BF16).astype(F32), U32)
    hi = lax.bitcast_convert_type(v[:, n:].astype(BF16).astype(F32), U32)
    return (hi & jnp.uint32(0xFFFF0000)) | (lo >> 16)


def _unpack_bf16_pairs(p):
    lo = lax.bitcast_convert_type(p << 16, F32).astype(BF16)
    hi = lax.bitcast_convert_type(p & jnp.uint32(0xFFFF0000), F32).astype(BF16)
    return lo, hi


def _ada_kernel(c_ref, w_ref, b_ref, o_ref):
    c = c_ref[...]
    cs = c * _sigmoid(c)
    o_ref[...] = jnp.dot(cs, w_ref[...], preferred_element_type=F32,
                         precision=lax.Precision.HIGHEST) + b_ref[...]


def _ada(c, w_ada, b_ada):
    B, D = c.shape
    W = w_ada.shape[1]
    tn = 1024
    return pl.pallas_call(
        _ada_kernel,
        grid=(W // tn,),
        in_specs=[pl.BlockSpec((B, D), lambda j: (0, 0)),
                  pl.BlockSpec((D, tn), lambda j: (0, j)),
                  pl.BlockSpec((1, tn), lambda j: (0, j))],
        out_specs=pl.BlockSpec((B, tn), lambda j: (0, j)),
        out_shape=jax.ShapeDtypeStruct((B, W), F32),
        name="ada",
    )(c, w_ada, b_ada.reshape(1, W))


def _in_kernel(x_ref, g_ref, sh_ref, sc_ref, w_ref, pool_ref, q0_ref, q1_ref, q2_ref, scr_ref):
    h = _rms(x_ref[...]) * g_ref[...]
    h = h * (1.0 + sc_ref[0]) + sh_ref[0]
    hb = h.astype(BF16)
    tm = x_ref.shape[0]
    pw = pool_ref.shape[1]
    gw = ATT_GROUP_WIDTH
    pool_ref[...] = jnp.dot(hb, w_ref[:, 0:pw], preferred_element_type=F32)
    for g, (out, d) in enumerate(zip((q0_ref, q1_ref, q2_ref), ATT_DILATIONS)):
        for sec in range(3):
            c0 = pw + sec * 3 * gw + g * gw
            res = jnp.dot(hb, w_ref[:, c0:c0 + gw], preferred_element_type=F32)
            if d == 1:
                out[:, sec * gw:(sec + 1) * gw] = res.astype(BF16)
            else:
                for c in range(gw // LANES):
                    scr_ref[c] = res[:, c * LANES:(c + 1) * LANES]
                for r in range(d):
                    for c in range(gw // LANES):
                        c1 = sec * gw + c * LANES
                        out[r, :, c1:c1 + LANES] = scr_ref[c, pl.ds(r, tm // d, stride=d), :].astype(BF16)


def _in_proj(xf, g_mix, shift1, scale1, w_in_b, S, pool_width):
    N, D = xf.shape
    tm = IN_TILE
    spt = S // tm
    vec = lambda i: (i // spt, 0, 0)
    row = lambda i: (i, 0)
    gw3 = 3 * ATT_GROUP_WIDTH
    B = N // S
    res_spec = lambda d: pl.BlockSpec((d, tm // d, gw3), lambda i: (i // spt, i % spt, 0))
    res_shape = lambda d: jax.ShapeDtypeStruct((B * d, S // d, gw3), BF16)
    outs = pl.pallas_call(
        _in_kernel,
        grid=(N // tm,),
        in_specs=[pl.BlockSpec((tm, D), row),
                  pl.BlockSpec((1, D), lambda i: (0, 0)),
                  pl.BlockSpec((1, 1, D), vec),
                  pl.BlockSpec((1, 1, D), vec),
                  pl.BlockSpec(w_in_b.shape, lambda i: (0, 0))],
        out_specs=[pl.BlockSpec((tm, pool_width), row), pl.BlockSpec((tm, gw3), row)]
                  + [res_spec(d) for d in ATT_DILATIONS[1:]],
        out_shape=[jax.ShapeDtypeStruct((N, pool_width), F32), jax.ShapeDtypeStruct((N, gw3), BF16)]
                  + [res_shape(d) for d in ATT_DILATIONS[1:]],
        scratch_shapes=[pltpu.VMEM((ATT_GROUP_WIDTH // LANES, tm, LANES), F32)],
        name="in_proj",
    )(xf, g_mix, shift1, scale1, w_in_b)
    return outs[0], [o.reshape(N, gw3) for o in outs[1:]]


def _attn_kernel(nbs, a_ref, halo_ref, o_ref, lse_ref, kv_ref):
    i = pl.program_id(0)
    R = a_ref.shape[0] // ATT_BLOCK
    gw = ATT_GROUP_WIDTH
    blk = ATT_BLOCK
    nh = ATT_HEADS_PER_GROUP
    kv_ref[0:blk, :] = halo_ref[:, gw:3 * gw]
    kv_ref[blk:, :] = a_ref[:, gw:3 * gw]
    row = lax.broadcasted_iota(I32, (nh * blk, 2 * blk), 0) % blk
    col = lax.broadcasted_iota(I32, (nh * blk, 2 * blk), 1)
    head_of_lane = lax.broadcasted_iota(I32, (blk, gw), 1) // ATT_HEAD_DIM
    hi = row + blk
    nt = (((1,), (1,)), ((), ()))

    def body(jj, carry):
        r0 = pl.multiple_of(jj * blk, blk)
        qf = a_ref[pl.ds(r0, blk), 0:gw].astype(F32)
        q4 = jnp.concatenate([jnp.where(head_of_lane == h, qf, 0.0) for h in range(nh)], axis=0).astype(BF16)
        kc = kv_ref[pl.ds(r0, 2 * blk), 0:gw]
        vc = kv_ref[pl.ds(r0, 2 * blk), gw:2 * gw]
        s = lax.dot_general(q4, kc, nt, preferred_element_type=F32) * (ATT_HEAD_DIM ** -0.5)
        first = ((i * R + jj) % nbs) == 0
        lo = jnp.maximum(row, jnp.where(first, blk, 0))
        s = jnp.where((col >= lo) & (col <= hi), s, NEG_INF)
        m = jnp.max(s, axis=1, keepdims=True)
        p = jnp.exp(s - m)
        l = jnp.sum(p, axis=1, keepdims=True)
        o4 = jnp.dot(p.astype(BF16), vc, preferred_element_type=F32) / l
        lse4 = m + jnp.log(l)
        o = jnp.zeros((blk, gw), F32)
        lse = jnp.zeros((blk, gw), F32)
        for h in range(nh):
            hm = head_of_lane == h
            o = jnp.where(hm, o4[h * blk:(h + 1) * blk, :], o)
            lse = jnp.where(hm, lse4[h * blk:(h + 1) * blk, :], lse)
        o_ref[pl.ds(r0, blk), :] = o
        lse_ref[pl.ds(r0, blk), :] = lse
        return carry

    lax.fori_loop(0, R, body, 0, unroll=2)


def _attention(a, nbs):
    N = a.shape[0]
    R = ATT_BLOCKS_PER_STEP
    gw = ATT_GROUP_WIDTH
    tm = R * ATT_BLOCK
    return pl.pallas_call(
        functools.partial(_attn_kernel, nbs),
        grid=(N // tm,),
        in_specs=[pl.BlockSpec((tm, 3 * gw), lambda i: (i, 0)),
                  pl.BlockSpec((ATT_BLOCK, 3 * gw), lambda i: (jnp.maximum(i * R - 1, 0), 0))],
        out_specs=[pl.BlockSpec((tm, gw), lambda i: (i, 0))] * 2,
        out_shape=[jax.ShapeDtypeStruct((N, gw), F32)] * 2,
        scratch_shapes=[pltpu.VMEM((tm + ATT_BLOCK, 2 * gw), BF16)],
        name="attn",
    )(a, a)


def _mid_kernel(spt, u_ref, uh_ref, o0_ref, l0_ref, o1_ref, l1_ref, o2_ref, l2_ref, x_ref,
                pbd_ref, psc_ref, wout_ref, gate1_ref, gffn_ref, sh2_ref, sc2_ref, gate2_ref,
                wsg_ref, wsu_ref, wsd_ref, xacc_ref, h2_ref, ext_ref, til_ref):
    i = pl.program_id(0)
    tm, pw = u_ref.shape
    si = i % spt
    u = u_ref[...]
    keep = jnp.full((POOL_HALO, pw), si, I32) > 0
    ext_ref[0:POOL_HALO, :] = jnp.where(keep, uh_ref[...], 0.0)
    ext_ref[POOL_HALO:, :] = u
    lane_grp = lax.broadcasted_iota(I32, (tm, pw), 1) // (pw // len(POOL_WINDOWS))
    s = u
    pooled = jnp.zeros((tm, pw), F32)
    for j in range(1, POOL_HALO):
        s = s + ext_ref[pl.ds(POOL_HALO - j, tm), :]
        if (j + 1) in POOL_WINDOWS:
            pooled = jnp.where(lane_grp == POOL_WINDOWS.index(j + 1), s, pooled)
    win = jnp.zeros((tm, pw), I32)
    for g, w in enumerate(POOL_WINDOWS):
        win = jnp.where(lane_grp == g, w, win)
    pos = si * tm + lax.broadcasted_iota(I32, (tm, pw), 0)
    cnt = jnp.minimum(pos + 1, win).astype(F32)
    pooled = pooled / cnt - u
    pool_out = jnp.dot(pooled.astype(BF16), pbd_ref[...], preferred_element_type=F32) * psc_ref[...]
    def token_order(slot, ref):
        d, n, w = ref.shape
        for r in range(d):
            for c in range(w // LANES):
                til_ref[slot, c, pl.ds(r, n, stride=d), :] = ref[r, :, c * LANES:(c + 1) * LANES]
        return jnp.concatenate([til_ref[slot, c] for c in range(w // LANES)], axis=1)

    l0 = l0_ref[...]
    l1 = token_order(0, l1_ref)
    l2 = token_order(1, l2_ref)
    m = jnp.maximum(jnp.maximum(l0, l1), l2)
    w0 = jnp.exp(l0 - m)
    w1 = jnp.exp(l1 - m)
    w2 = jnp.exp(l2 - m)
    attn = (w0 * o0_ref[...] + w1 * token_order(2, o1_ref) + w2 * token_order(3, o2_ref)) / (w0 + w1 + w2)
    mixed = (jnp.dot(pool_out.astype(BF16), wout_ref[0:pw, :], preferred_element_type=F32)
             + jnp.dot(attn.astype(BF16), wout_ref[pw:, :], preferred_element_type=F32))
    x1 = x_ref[...] + gate1_ref[0] * mixed
    h2 = _rms(x1) * gffn_ref[...]
    h2 = h2 * (1.0 + sc2_ref[0]) + sh2_ref[0]
    h2_ref[...] = _pack_bf16_pairs(h2)
    hb = h2.astype(BF16)
    a = jnp.dot(hb, wsg_ref[...], preferred_element_type=F32)
    b = jnp.dot(hb, wsu_ref[...], preferred_element_type=F32)
    act = (a * _sigmoid(a)) * b
    shared = jnp.dot(act.astype(BF16), wsd_ref[...], preferred_element_type=F32)
    xacc_ref[...] = x1 + gate2_ref[0] * shared


def _mid(u, attn_outs, xf, pool_bd, pool_scale, w_out_b, gate1, g_ffn, shift2, scale2, gate2,
         wsg_b, wsu_b, wsd_b, S):
    N, D = xf.shape
    pw = u.shape[1]
    tm = MID_TILE
    spt = S // tm
    row = lambda i: (i, 0)
    vec = lambda i: (i // spt, 0, 0)
    full = lambda a: pl.BlockSpec(a.shape, lambda i: (0,) * a.ndim)
    hpt = tm // POOL_HALO
    in_specs = [pl.BlockSpec((tm, pw), row),
                pl.BlockSpec((POOL_HALO, pw), lambda i: (jnp.maximum(i * hpt - 1, 0), 0))]
    gw = ATT_GROUP_WIDTH
    in_specs += [pl.BlockSpec((tm, gw), row)] * 2
    for d in ATT_DILATIONS[1:]:
        in_specs += [pl.BlockSpec((d, tm // d, gw), lambda i: (i // spt, i % spt, 0))] * 2
    in_specs += [pl.BlockSpec((tm, D), row), full(pool_bd), full(pool_scale), full(w_out_b),
                 pl.BlockSpec((1, 1, D), vec), full(g_ffn), pl.BlockSpec((1, 1, D), vec),
                 pl.BlockSpec((1, 1, D), vec), pl.BlockSpec((1, 1, D), vec),
                 full(wsg_b), full(wsu_b), full(wsd_b)]
    return pl.pallas_call(
        functools.partial(_mid_kernel, spt),
        grid=(N // tm,),
        in_specs=in_specs,
        out_specs=[pl.BlockSpec((tm, D), row), pl.BlockSpec((tm, D // 2), row)],
        out_shape=[jax.ShapeDtypeStruct((N, D), F32), jax.ShapeDtypeStruct((N, D // 2), U32)],
        scratch_shapes=[pltpu.VMEM((tm + POOL_HALO, pw), F32), pltpu.VMEM((4, gw // LANES, tm, LANES), F32)],
        name="mid",
    )(u, u, *attn_outs, xf, pool_bd, pool_scale, w_out_b, gate1, g_ffn, shift2, scale2, gate2,
      wsg_b, wsu_b, wsd_b)


def _route_kernel(h_ref, wr_ref, bias_ref, e_ref, g_ref, r_ref, cnt_ref, base_ref):
    i = pl.program_id(0)
    T = h_ref.shape[0]
    E = wr_ref.shape[0]
    gsz = E // N_EXPERT_GROUPS

    @pl.when(i == 0)
    def _():
        base_ref[...] = jnp.zeros_like(base_ref)

    h_lo, h_hi = _unpack_bf16_pairs(h_ref[...])
    half = h_lo.shape[1]
    nt = (((1,), (1,)), ((), ()))
    logits = (lax.dot_general(wr_ref[:, 0:half], h_lo, nt, preferred_element_type=F32)
              + lax.dot_general(wr_ref[:, half:], h_hi, nt, preferred_element_type=F32))
    scores = _sigmoid(logits)
    biased = scores + bias_ref[...]
    giota = lax.broadcasted_iota(I32, (gsz, T), 0)
    gscore = []
    for g in range(N_EXPERT_GROUPS):
        blk = biased[g * gsz:(g + 1) * gsz, :]
        m1 = jnp.max(blk, axis=0, keepdims=True)
        i1 = jnp.min(jnp.where(blk == m1, giota, gsz), axis=0, keepdims=True)
        m2 = jnp.max(jnp.where(giota == i1, NEG_INF, blk), axis=0, keepdims=True)
        gscore.append(m1 + m2)
    parts = []
    for g in range(N_EXPERT_GROUPS):
        beaten = jnp.zeros((1, T), I32)
        for o in range(N_EXPERT_GROUPS):
            if o == g:
                continue
            wins = (gscore[o] >= gscore[g]) if o < g else (gscore[o] > gscore[g])
            beaten = beaten + wins.astype(I32)
        keep = jnp.broadcast_to(beaten, (gsz, T)) < TOPK_GROUPS
        parts.append(jnp.where(keep, biased[g * gsz:(g + 1) * gsz, :], NEG_INF))
    cur = jnp.concatenate(parts, axis=0)
    eiota = lax.broadcasted_iota(I32, (E, T), 0)
    selm = jnp.zeros((E, T), F32)
    idxs, gates = [], []
    for k in range(TOP_K):
        m = jnp.max(cur, axis=0, keepdims=True)
        idx = jnp.min(jnp.where(cur == m, eiota, E), axis=0, keepdims=True)
        oh = eiota == idx
        gates.append(jnp.sum(jnp.where(oh, scores, 0.0), axis=0, keepdims=True))
        idxs.append(idx)
        cur = jnp.where(oh, NEG_INF, cur)
        selm = jnp.where(oh, 1.0, selm)
    gsum = gates[0]
    for k in range(1, TOP_K):
        gsum = gsum + gates[k]
    before = (lax.broadcasted_iota(I32, (T, T), 0) < lax.broadcasted_iota(I32, (T, T), 1)).astype(BF16)
    tot = jnp.dot(selm.astype(BF16), before, preferred_element_type=F32) + base_ref[...]
    for k in range(TOP_K):
        e_ref[k:k + 1, :] = idxs[k]
        g_ref[k:k + 1, :] = gates[k] / gsum * ROUTED_SCALE
        r_ref[k:k + 1, :] = jnp.sum(jnp.where(eiota == idxs[k], tot, 0.0), axis=0, keepdims=True).astype(I32)
    base_ref[...] = base_ref[...] + jnp.sum(selm, axis=1, keepdims=True)
    cnt_ref[...] = base_ref[...].astype(I32)


def _route(h2p, wr_t, bias_col):
    N = h2p.shape[0]
    E, D = wr_t.shape
    T = ROUTE_TILE
    col = lambda i: (0, i)
    return pl.pallas_call(
        _route_kernel,
        grid=(N // T,),
        in_specs=[pl.BlockSpec((T, D // 2), lambda i: (i, 0)),
                  pl.BlockSpec((E, D), lambda i: (0, 0)),
                  pl.BlockSpec((E, 1), lambda i: (0, 0))],
        out_specs=[pl.BlockSpec((TOP_K, T), col), pl.BlockSpec((TOP_K, T), col),
                   pl.BlockSpec((TOP_K, T), col), pl.BlockSpec((E, 1), lambda i: (0, 0))],
        out_shape=[jax.ShapeDtypeStruct((TOP_K, N), I32), jax.ShapeDtypeStruct((TOP_K, N), F32),
                   jax.ShapeDtypeStruct((TOP_K, N), I32), jax.ShapeDtypeStruct((E, 1), I32)],
        scratch_shapes=[pltpu.VMEM((E, 1), F32)],
        compiler_params=pltpu.CompilerParams(dimension_semantics=("arbitrary",)),
        name="route",
    )(h2p, wr_t, bias_col)


def _dest_kernel(e_ref, r_ref, off_ref, dc_ref):
    E = off_ref.shape[0]
    T = e_ref.shape[1]
    C = dc_ref.shape[2]
    eiota = lax.broadcasted_iota(I32, (E, T), 0)
    off = off_ref[...]
    for k in range(TOP_K):
        start = jnp.sum(jnp.where(eiota == e_ref[k:k + 1, :], off, 0.0), axis=0, keepdims=True)
        d = start.astype(I32) + r_ref[k:k + 1, :]
        for c in range(T // C):
            dc_ref[c, k:k + 1, :] = d[:, c * C:(c + 1) * C]


def _dest(eidx, rank, offs_col):
    N = eidx.shape[1]
    E = offs_col.shape[0]
    T = DEST_TILE
    C = SC_CHUNK
    col = lambda i: (0, i)
    return pl.pallas_call(
        _dest_kernel,
        grid=(N // T,),
        in_specs=[pl.BlockSpec((TOP_K, T), col), pl.BlockSpec((TOP_K, T), col),
                  pl.BlockSpec((E, 1), lambda i: (0, 0))],
        out_specs=pl.BlockSpec((T // C, TOP_K, C), lambda i: (i, 0, 0)),
        out_shape=jax.ShapeDtypeStruct((N // C, TOP_K, C), I32),
        name="dest",
    )(eidx, rank, offs_col)


def _sc_dispatch(dest_c, h2p, P):
    N, W = h2p.shape
    C = dest_c.shape[2]
    info = plsc.get_sparse_core_info()
    nw = info.num_cores * info.num_subcores
    per_w = N // C // nw
    mesh = plsc.VectorSubcoreMesh(core_axis_name="c", subcore_axis_name="s")

    @functools.partial(
        pl.kernel, mesh=mesh,
        out_type=jax.ShapeDtypeStruct((P, W), h2p.dtype),
        scratch_types=[pltpu.VMEM((TOP_K, C), I32), pltpu.VMEM((C, W), h2p.dtype), pltpu.SemaphoreType.DMA],
        name="sc_dispatch",
    )
    def k(dest_hbm, h_hbm, xs_hbm, idx_v, rows_v, sem):
        wid = lax.axis_index("s") * info.num_cores + lax.axis_index("c")

        @pl.loop(0, per_w)
        def _(j):
            ch = wid * per_w + j
            pltpu.sync_copy(dest_hbm.at[ch], idx_v)
            pltpu.sync_copy(h_hbm.at[pl.ds(ch * C, C)], rows_v)
            copies = [pltpu.async_copy(rows_v, xs_hbm.at[idx_v.at[kk]], sem) for kk in range(TOP_K)]
            for cp in copies:
                cp.wait()

    return k(dest_c, h2p)


def _gmm_kernel(bstart_ref, nbe_ref, cnt_ref, nu_ref, wg_ref, wu_ref, wd_ref, xs_hbm, ys_hbm,
                wgb, wub, wdb, xbuf, ybuf, xsem, ysem):
    e = pl.program_id(0)
    last = pl.num_programs(0) - 1
    ring, bm = xbuf.shape[0], xbuf.shape[1]
    nblk = ys_hbm.shape[0] // bm
    b0 = bstart_ref[e]
    nb = nbe_ref[e]
    nused = nu_ref[0]

    def x_copy(b, slot):
        return pltpu.make_async_copy(xs_hbm.at[pl.ds(pl.multiple_of(b * bm, bm), bm), :], xbuf.at[slot],
                                     xsem.at[slot])

    def y_copy(b, slot):
        return pltpu.make_async_copy(ybuf.at[slot], ys_hbm.at[pl.ds(pl.multiple_of(b * bm, bm), bm), :],
                                     ysem.at[slot])

    @pl.when(e == 0)
    def _():
        for j in range(ring - 1):
            @pl.when(j < nused)
            def _():
                x_copy(j, j).start()

    @pl.when(nb > 0)
    def _():
        wgb[...] = wg_ref[0].astype(BF16)
        wub[...] = wu_ref[0].astype(BF16)
        wdb[...] = wd_ref[0].astype(BF16)

        def prefetch(t):
            @pl.when(t < nused)
            def _():
                x_copy(t, jnp.bitwise_and(t, ring - 1)).start()

        def process(b, n):
            slots = [jnp.bitwise_and(b + j, ring - 1) for j in range(n)]
            for j in range(n):
                x_copy(b + j, slots[j]).wait()
            prefetch(b + ring - 1)
            for j in range(n):
                @pl.when(b + j >= ring)
                def _():
                    y_copy(b + j - ring, slots[j]).wait()
            rows = lax.broadcasted_iota(I32, (n * bm, 1), 0)
            valid = cnt_ref[e] - (b - b0) * bm
            xp = jnp.concatenate([xbuf[s] for s in slots], axis=0)
            xb = jnp.concatenate(_unpack_bf16_pairs(jnp.where(rows < valid, xp, jnp.uint32(0))), axis=1)
            a = jnp.dot(xb, wgb[...], preferred_element_type=F32)
            g = jnp.dot(xb, wub[...], preferred_element_type=F32)
            act = (a * _sigmoid(a)) * g
            yp = _pack_bf16_pairs(jnp.dot(act.astype(BF16), wdb[...], preferred_element_type=F32))
            for j in range(n):
                ybuf[slots[j]] = yp[j * bm:(j + 1) * bm, :]
                y_copy(b + j, slots[j]).start()
            for j in range(1, n):
                prefetch(b + ring - 1 + j)

        pairs = lax.shift_right_logical(nb, 1)
        lax.fori_loop(0, pairs, lambda j, c: (process(b0 + 2 * j, 2), c)[1], 0)

        @pl.when(jnp.bitwise_and(nb, 1) == 1)
        def _():
            process(b0 + nb - 1, 1)

    @pl.when(e == last)
    def _():
        for back in range(ring, 0, -1):
            @pl.when(nused >= back)
            def _():
                y_copy(nused - back, jnp.bitwise_and(nused - back, ring - 1)).wait()
        ybuf[0] = jnp.zeros(ybuf.shape[1:], ybuf.dtype)
        lax.fori_loop(nused, nblk, lambda b, c: (y_copy(b, 0).start(), c)[1], 0)
        lax.fori_loop(nused, nblk, lambda b, c: (y_copy(b, 0).wait(), c)[1], 0)


def _gmm(bstart, nb_e, counts, nused, xs, w_gate, w_up, w_down):
    P, W = xs.shape
    E, D, F = w_gate.shape
    bm = GMM_BLOCK
    wsel = lambda e, *_: (e, 0, 0)
    grid_spec = pltpu.PrefetchScalarGridSpec(
        num_scalar_prefetch=4,
        grid=(E,),
        in_specs=[pl.BlockSpec((1, D, F), wsel), pl.BlockSpec((1, D, F), wsel), pl.BlockSpec((1, F, D), wsel),
                  pl.BlockSpec(memory_space=pl.ANY)],
        out_specs=pl.BlockSpec(memory_space=pl.ANY),
        scratch_shapes=[pltpu.VMEM((D, F), BF16), pltpu.VMEM((D, F), BF16), pltpu.VMEM((F, D), BF16),
                        pltpu.VMEM((GMM_RING, bm, W), xs.dtype), pltpu.VMEM((GMM_RING, bm, W), xs.dtype),
                        pltpu.SemaphoreType.DMA((GMM_RING,)), pltpu.SemaphoreType.DMA((GMM_RING,))],
    )
    return pl.pallas_call(
        _gmm_kernel,
        grid_spec=grid_spec,
        out_shape=jax.ShapeDtypeStruct((P, W), xs.dtype),
        compiler_params=pltpu.CompilerParams(dimension_semantics=("arbitrary",)),
        name="gmm",
    )(bstart, nb_e, counts, nused, w_gate, w_up, w_down, xs)


def _sc_gather(dest_c, ys):
    nch, K, C = dest_c.shape
    W = ys.shape[1]
    H = C // 2
    info = plsc.get_sparse_core_info()
    nw = info.num_cores * info.num_subcores
    per_w = nch // nw
    nbuf = 3
    mesh = plsc.VectorSubcoreMesh(core_axis_name="c", subcore_axis_name="s")
    items = [(kk, hh) for kk in range(K) for hh in range(2)]

    @functools.partial(
        pl.kernel, mesh=mesh,
        out_type=jax.ShapeDtypeStruct((K, nch * C, W), ys.dtype),
        scratch_types=([pltpu.VMEM((K, C), I32)] + [pltpu.VMEM((H, W), ys.dtype)] * nbuf
                       + [pltpu.SemaphoreType.DMA] * (2 * nbuf)),
        name="sc_gather",
    )
    def k(dest_hbm, ys_hbm, yk_hbm, idx_v, *rest):
        bufs, gsem, wsem = rest[:nbuf], rest[nbuf:2 * nbuf], rest[2 * nbuf:]
        wid = lax.axis_index("s") * info.num_cores + lax.axis_index("c")

        @pl.loop(0, per_w)
        def _(j):
            ch = wid * per_w + j
            pltpu.sync_copy(dest_hbm.at[ch], idx_v)

            def gather(i):
                kk, hh = items[i]
                return pltpu.async_copy(ys_hbm.at[idx_v.at[kk, pl.ds(hh * H, H)]], bufs[i % nbuf], gsem[i % nbuf])

            def write(i):
                kk, hh = items[i]
                return pltpu.async_copy(bufs[i % nbuf], yk_hbm.at[kk, pl.ds(ch * C + hh * H, H)], wsem[i % nbuf])

            n = len(items)
            g = {0: gather(0), 1: gather(1)}
            w = {}
            for i in range(n):
                g[i].wait()
                w[i] = write(i)
                if i + 2 < n:
                    if i >= 1:
                        w.pop(i - 1).wait()
                    g[i + 2] = gather(i + 2)
            for i in sorted(w):
                w[i].wait()

    return k(dest_c, ys)


def _combine_kernel(yk_ref, gt_ref, gate2_ref, xacc_ref, gfin_ref, o_ref):
    gt = gt_ref[...]
    hi_mask = jnp.uint32(0xFFFF0000)
    acc_lo = acc_hi = None
    for k in range(TOP_K):
        p = yk_ref[k]
        g = gt[:, k:k + 1]
        lo = lax.bitcast_convert_type(p << 16, F32) * g
        hi = lax.bitcast_convert_type(p & hi_mask, F32) * g
        acc_lo = lo if k == 0 else acc_lo + lo
        acc_hi = hi if k == 0 else acc_hi + hi
    routed = jnp.concatenate([acc_lo, acc_hi], axis=1)
    x2 = xacc_ref[...] + gate2_ref[0] * routed
    o_ref[...] = _rms(x2) * gfin_ref[...]


def _combine(yk, gates_t, gate2, xacc, g_final, S):
    N, D = xacc.shape
    W = yk.shape[2]
    T = ROUTE_TILE
    spt = S // T
    return pl.pallas_call(
        _combine_kernel,
        grid=(N // T,),
        in_specs=[pl.BlockSpec((TOP_K, T, W), lambda i: (0, i, 0)),
                  pl.BlockSpec((T, TOP_K), lambda i: (i, 0)),
                  pl.BlockSpec((1, 1, D), lambda i: (i // spt, 0, 0)),
                  pl.BlockSpec((T, D), lambda i: (i, 0)),
                  pl.BlockSpec((1, D), lambda i: (0, 0))],
        out_specs=pl.BlockSpec((T, D), lambda i: (i, 0)),
        out_shape=jax.ShapeDtypeStruct((N, D), F32),
        name="combine",
    )(yk, gates_t, gate2, xacc, g_final)


def _layer(xf, B, S, mod, g_mix, w_in, pool_w, pool_scale, w_out, g_ffn, w_router, router_bias,
           w_gate, w_up, w_down, ws_gate, ws_up, ws_down):
    N, D = xf.shape
    E = w_router.shape[1]
    pw = pool_scale.shape[0]
    shift1, scale1, gate1, shift2, scale2, gate2 = [m.reshape(B, 1, D) for m in jnp.split(mod, 6, axis=-1)]
    u, qkv = _in_proj(xf, g_mix.reshape(1, D), shift1, scale1, w_in.astype(BF16), S, pw)
    attn_outs = []
    for a, d in zip(qkv, ATT_DILATIONS):
        o, lse = _attention(a, S // d // ATT_BLOCK)
        shape = (N, ATT_GROUP_WIDTH) if d == 1 else (B * d, S // d, ATT_GROUP_WIDTH)
        attn_outs += [o.reshape(shape), lse.reshape(shape)]
    ng = pool_w.shape[0]
    pool_bd = jnp.einsum('gcd,gh->gchd', pool_w, jnp.eye(ng, dtype=pool_w.dtype)).reshape(pw, pw).astype(BF16)
    xacc, h2 = _mid(u, attn_outs, xf, pool_bd, pool_scale.reshape(1, pw), w_out.astype(BF16), gate1,
                    g_ffn.reshape(1, D), shift2, scale2, gate2,
                    ws_gate.astype(BF16), ws_up.astype(BF16), ws_down.astype(BF16), S)
    eidx, gates, rank, counts = _route(h2, w_router.T.astype(BF16), router_bias.reshape(E, 1).astype(F32))
    bm = GMM_BLOCK
    nblk = N * TOP_K // bm + E
    nb_e = (counts[:, 0] + bm - 1) // bm
    bend = jnp.cumsum(nb_e)
    bstart = (bend - nb_e).astype(I32)
    nused = bend[-1:].astype(I32)
    dest_c = _dest(eidx, rank, (bstart * bm).astype(F32).reshape(E, 1))
    xs = _sc_dispatch(dest_c, h2, nblk * bm)
    ys = _gmm(bstart, nb_e.astype(I32), counts[:, 0], nused, xs, w_gate, w_up, w_down)
    return _sc_gather(dest_c, ys), gates.T, gate2, xacc


def kernel(x, c, w_ada, b_ada, g_mix, w_in, pool_w, pool_scale, w_out, g_ffn, w_router, router_bias,
           w_gate, w_up, w_down, ws_gate, ws_up, ws_down, g_final):
    B, S, D = x.shape
    depth = w_ada.shape[0]
    assert depth == 1, "the final residual is fused with the final norm, so exactly one layer is supported"
    assert S % (ATT_DILATIONS[-1] * ATT_BLOCK) == 0 and S % max(IN_TILE, MID_TILE, ATT_BLOCKS_PER_STEP * ATT_BLOCK) == 0
    xf = x.reshape(B * S, D)
    mod = _ada(c, w_ada[0], b_ada[0])
    yk, gates_t, gate2, xacc = _layer(
        xf, B, S, mod, g_mix[0], w_in[0], pool_w[0], pool_scale[0], w_out[0], g_ffn[0], w_router[0],
        router_bias[0], w_gate[0], w_up[0], w_down[0], ws_gate[0], ws_up[0], ws_down[0])
    out = _combine(yk, gates_t, gate2, xacc, g_final.reshape(1, D), S)
    return out.reshape(B, S, D)
```

```python
import functools

import jax
import jax.numpy as jnp
from jax import lax
from jax.experimental import pallas as pl
from jax.experimental.pallas import tpu as pltpu
from jax.experimental.pallas import tpu_sc as plsc

F32 = jnp.float32
BF16 = jnp.bfloat16
I32 = jnp.int32
U32 = jnp.uint32

LANES = 128
NORM_EPS = 1e-6
POOL_WINDOWS = (2, 4, 8, 16)
POOL_HALO = 16
ATT_DILATIONS = (1, 4, 16)
ATT_BLOCK = 128
ATT_HEADS_PER_GROUP = 4
ATT_HEAD_DIM = 64
ATT_GROUP_WIDTH = ATT_HEADS_PER_GROUP * ATT_HEAD_DIM
N_EXPERT_GROUPS = 8
TOPK_GROUPS = 4
TOP_K = 8
ROUTED_SCALE = 2.5

IN_TILE = 512
ATT_BLOCKS_PER_STEP = 8
MID_TILE = 512
ROUTE_TILE = 256
DEST_TILE = 1024
GMM_BLOCK = 256
GMM_RING = 8
SC_CHUNK = 128

NEG_INF = float("-inf")


def _sigmoid(v):
    return 1.0 / (1.0 + jnp.exp(-v))


def _rms(v):
    return v * lax.rsqrt(jnp.mean(v * v, axis=-1, keepdims=True) + NORM_EPS)


def _pack_bf16_pairs(v):
    n = v.shape[1] // 2
    lo = lax.bitcast_convert_type(v[:, :n].astype(BF16).astype(F32), U32)
    hi = lax.bitcast_convert_type(v[:, n:].astype(BF16).astype(F32), U32)
    return (hi & jnp.uint32(0xFFFF0000)) | (lo >> 16)


def _unpack_bf16_pairs(p):
    lo = lax.bitcast_convert_type(p << 16, F32).astype(BF16)
    hi = lax.bitcast_convert_type(p & jnp.uint32(0xFFFF0000), F32).astype(BF16)
    return lo, hi


def _ada_kernel(c_ref, w_ref, b_ref, o_ref):
    c = c_ref[...]
    cs = c * _sigmoid(c)
    o_ref[...] = jnp.dot(cs, w_ref[...], preferred_element_type=F32,
                         precision=lax.Precision.HIGHEST) + b_ref[...]


def _ada(c, w_ada, b_ada):
    B, D = c.shape
    W = w_ada.shape[1]
    tn = 1024
    return pl.pallas_call(
        _ada_kernel,
        grid=(W // tn,),
        in_specs=[pl.BlockSpec((B, D), lambda j: (0, 0)),
                  pl.BlockSpec((D, tn), lambda j: (0, j)),
                  pl.BlockSpec((1, tn), lambda j: (0, j))],
        out_specs=pl.BlockSpec((B, tn), lambda j: (0, j)),
        out_shape=jax.ShapeDtypeStruct((B, W), F32),
        name="ada",
    )(c, w_ada, b_ada.reshape(1, W))


def _in_kernel(x_ref, g_ref, sh_ref, sc_ref, w_ref, pool_ref, q0_ref, q1_ref, q2_ref, scr_ref):
    h = _rms(x_ref[...]) * g_ref[...]
    h = h * (1.0 + sc_ref[0]) + sh_ref[0]
    hb = h.astype(BF16)
    tm = x_ref.shape[0]
    pw = pool_ref.shape[1]
    gw = ATT_GROUP_WIDTH
    pool_ref[...] = jnp.dot(hb, w_ref[:, 0:pw], preferred_element_type=F32)
    for g, (out, d) in enumerate(zip((q0_ref, q1_ref, q2_ref), ATT_DILATIONS)):
        for sec in range(3):
            c0 = pw + sec * 3 * gw + g * gw
            res = jnp.dot(hb, w_ref[:, c0:c0 + gw], preferred_element_type=F32)
            if d == 1:
                out[:, sec * gw:(sec + 1) * gw] = res.astype(BF16)
            else:
                for c in range(gw // LANES):
                    scr_ref[c] = res[:, c * LANES:(c + 1) * LANES]
                for r in range(d):
                    for c in range(gw // LANES):
                        c1 = sec * gw + c * LANES
                        out[r, :, c1:c1 + LANES] = scr_ref[c, pl.ds(r, tm // d, stride=d), :].astype(BF16)


def _in_proj(xf, g_mix, shift1, scale1, w_in_b, S, pool_width):
    N, D = xf.shape
    tm = IN_TILE
    spt = S // tm
    vec = lambda i: (i // spt, 0, 0)
    row = lambda i: (i, 0)
    gw3 = 3 * ATT_GROUP_WIDTH
    B = N // S
    res_spec = lambda d: pl.BlockSpec((d, tm // d, gw3), lambda i: (i // spt, i % spt, 0))
    res_shape = lambda d: jax.ShapeDtypeStruct((B * d, S // d, gw3), BF16)
    outs = pl.pallas_call(
        _in_kernel,
        grid=(N // tm,),
        in_specs=[pl.BlockSpec((tm, D), row),
                  pl.BlockSpec((1, D), lambda i: (0, 0)),
                  pl.BlockSpec((1, 1, D), vec),
                  pl.BlockSpec((1, 1, D), vec),
                  pl.BlockSpec(w_in_b.shape, lambda i: (0, 0))],
        out_specs=[pl.BlockSpec((tm, pool_width), row), pl.BlockSpec((tm, gw3), row)]
                  + [res_spec(d) for d in ATT_DILATIONS[1:]],
        out_shape=[jax.ShapeDtypeStruct((N, pool_width), F32), jax.ShapeDtypeStruct((N, gw3), BF16)]
                  + [res_shape(d) for d in ATT_DILATIONS[1:]],
        scratch_shapes=[pltpu.VMEM((ATT_GROUP_WIDTH // LANES, tm, LANES), F32)],
        name="in_proj",
    )(xf, g_mix, shift1, scale1, w_in_b)
    return outs[0], [o.reshape(N, gw3) for o in outs[1:]]


def _attn_kernel(nbs, a_ref, halo_ref, o_ref, lse_ref, kv_ref):
    i = pl.program_id(0)
    R = a_ref.shape[0] // ATT_BLOCK
    gw = ATT_GROUP_WIDTH
    blk = ATT_BLOCK
    nh = ATT_HEADS_PER_GROUP
    kv_ref[0:blk, :] = halo_ref[:, gw:3 * gw]
    kv_ref[blk:, :] = a_ref[:, gw:3 * gw]
    row = lax.broadcasted_iota(I32, (nh * blk, 2 * blk), 0) % blk
    col = lax.broadcasted_iota(I32, (nh * blk, 2 * blk), 1)
    head_of_lane = lax.broadcasted_iota(I32, (blk, gw), 1) // ATT_HEAD_DIM
    hi = row + blk
    nt = (((1,), (1,)), ((), ()))

    def body(jj, carry):
        r0 = pl.multiple_of(jj * blk, blk)
        qf = a_ref[pl.ds(r0, blk), 0:gw].astype(F32)
        q4 = jnp.concatenate([jnp.where(head_of_lane == h, qf, 0.0) for h in range(nh)], axis=0).astype(BF16)
        kc = kv_ref[pl.ds(r0, 2 * blk), 0:gw]
        vc = kv_ref[pl.ds(r0, 2 * blk), gw:2 * gw]
        s = lax.dot_general(q4, kc, nt, preferred_element_type=F32) * (ATT_HEAD_DIM ** -0.5)
        first = ((i * R + jj) % nbs) == 0
        lo = jnp.maximum(row, jnp.where(first, blk, 0))
        s = jnp.where((col >= lo) & (col <= hi), s, NEG_INF)
        m = jnp.max(s, axis=1, keepdims=True)
        p = jnp.exp(s - m)
        l = jnp.sum(p, axis=1, keepdims=True)
        o4 = jnp.dot(p.astype(BF16), vc, preferred_element_type=F32) / l
        lse4 = m + jnp.log(l)
        o = jnp.zeros((blk, gw), F32)
        lse = jnp.zeros((blk, gw), F32)
        for h in range(nh):
            hm = head_of_lane == h
            o = jnp.where(hm, o4[h * blk:(h + 1) * blk, :], o)
            lse = jnp.where(hm, lse4[h * blk:(h + 1) * blk, :], lse)
        o_ref[pl.ds(r0, blk), :] = o
        lse_ref[pl.ds(r0, blk), :] = lse
        return carry

    lax.fori_loop(0, R, body, 0, unroll=2)


def _attention(a, nbs):
    N = a.shape[0]
    R = ATT_BLOCKS_PER_STEP
    gw = ATT_GROUP_WIDTH
    tm = R * ATT_BLOCK
    return pl.pallas_call(
        functools.partial(_attn_kernel, nbs),
        grid=(N // tm,),
        in_specs=[pl.BlockSpec((tm, 3 * gw), lambda i: (i, 0)),
                  pl.BlockSpec((ATT_BLOCK, 3 * gw), lambda i: (jnp.maximum(i * R - 1, 0), 0))],
        out_specs=[pl.BlockSpec((tm, gw), lambda i: (i, 0))] * 2,
        out_shape=[jax.ShapeDtypeStruct((N, gw), F32)] * 2,
        scratch_shapes=[pltpu.VMEM((tm + ATT_BLOCK, 2 * gw), BF16)],
        name="attn",
    )(a, a)


def _mid_kernel(spt, u_ref, uh_ref, o0_ref, l0_ref, o1_ref, l1_ref, o2_ref, l2_ref, x_ref,
                pbd_ref, psc_ref, wout_ref, gate1_ref, gffn_ref, sh2_ref, sc2_ref, gate2_ref,
                wsg_ref, wsu_ref, wsd_ref, xacc_ref, h2_ref, ext_ref, til_ref):
    i = pl.program_id(0)
    tm, pw = u_ref.shape
    si = i % spt
    u = u_ref[...]
    keep = jnp.full((POOL_HALO, pw), si, I32) > 0
    ext_ref[0:POOL_HALO, :] = jnp.where(keep, uh_ref[...], 0.0)
    ext_ref[POOL_HALO:, :] = u
    lane_grp = lax.broadcasted_iota(I32, (tm, pw), 1) // (pw // len(POOL_WINDOWS))
    s = u
    pooled = jnp.zeros((tm, pw), F32)
    for j in range(1, POOL_HALO):
        s = s + ext_ref[pl.ds(POOL_HALO - j, tm), :]
        if (j + 1) in POOL_WINDOWS:
            pooled = jnp.where(lane_grp == POOL_WINDOWS.index(j + 1), s, pooled)
    win = jnp.zeros((tm, pw), I32)
    for g, w in enumerate(POOL_WINDOWS):
        win = jnp.where(lane_grp == g, w, win)
    pos = si * tm + lax.broadcasted_iota(I32, (tm, pw), 0)
    cnt = jnp.minimum(pos + 1, win).astype(F32)
    pooled = pooled / cnt - u
    pool_out = jnp.dot(pooled.astype(BF16), pbd_ref[...], preferred_element_type=F32) * psc_ref[...]
    def token_order(slot, ref):
        d, n, w = ref.shape
        for r in range(d):
            for c in range(w // LANES):
                til_ref[slot, c, pl.ds(r, n, stride=d), :] = ref[r, :, c * LANES:(c + 1) * LANES]
        return jnp.concatenate([til_ref[slot, c] for c in range(w // LANES)], axis=1)

    l0 = l0_ref[...]
    l1 = token_order(0, l1_ref)
    l2 = token_order(1, l2_ref)
    m = jnp.maximum(jnp.maximum(l0, l1), l2)
    w0 = jnp.exp(l0 - m)
    w1 = jnp.exp(l1 - m)
    w2 = jnp.exp(l2 - m)
    attn = (w0 * o0_ref[...] + w1 * token_order(2, o1_ref) + w2 * token_order(3, o2_ref)) / (w0 + w1 + w2)
    mixed = (jnp.dot(pool_out.astype(BF16), wout_ref[0:pw, :], preferred_element_type=F32)
             + jnp.dot(attn.astype(BF16), wout_ref[pw:, :], preferred_element_type=F32))
    x1 = x_ref[...] + gate1_ref[0] * mixed
    h2 = _rms(x1) * gffn_ref[...]
    h2 = h2 * (1.0 + sc2_ref[0]) + sh2_ref[0]
    h2_ref[...] = _pack_bf16_pairs(h2)
    hb = h2.astype(BF16)
    a = jnp.dot(hb, wsg_ref[...], preferred_element_type=F32)
    b = jnp.dot(hb, wsu_ref[...], preferred_element_type=F32)
    act = (a * _sigmoid(a)) * b
    shared = jnp.dot(act.astype(BF16), wsd_ref[...], preferred_element_type=F32)
    xacc_ref[...] = x1 + gate2_ref[0] * shared


def _mid(u, attn_outs, xf, pool_bd, pool_scale, w_out_b, gate1, g_ffn, shift2, scale2, gate2,
         wsg_b, wsu_b, wsd_b, S):
    N, D = xf.shape
    pw = u.shape[1]
    tm = MID_TILE
    spt = S // tm
    row = lambda i: (i, 0)
    vec = lambda i: (i // spt, 0, 0)
    full = lambda a: pl.BlockSpec(a.shape, lambda i: (0,) * a.ndim)
    hpt = tm // POOL_HALO
    in_specs = [pl.BlockSpec((tm, pw), row),
                pl.BlockSpec((POOL_HALO, pw), lambda i: (jnp.maximum(i * hpt - 1, 0), 0))]
    gw = ATT_GROUP_WIDTH
    in_specs += [pl.BlockSpec((tm, gw), row)] * 2
    for d in ATT_DILATIONS[1:]:
        in_specs += [pl.BlockSpec((d, tm // d, gw), lambda i: (i // spt, i % spt, 0))] * 2
    in_specs += [pl.BlockSpec((tm, D), row), full(pool_bd), full(pool_scale), full(w_out_b),
                 pl.BlockSpec((1, 1, D), vec), full(g_ffn), pl.BlockSpec((1, 1, D), vec),
                 pl.BlockSpec((1, 1, D), vec), pl.BlockSpec((1, 1, D), vec),
                 full(wsg_b), full(wsu_b), full(wsd_b)]
    return pl.pallas_call(
        functools.partial(_mid_kernel, spt),
        grid=(N // tm,),
        in_specs=in_specs,
        out_specs=[pl.BlockSpec((tm, D), row), pl.BlockSpec((tm, D // 2), row)],
        out_shape=[jax.ShapeDtypeStruct((N, D), F32), jax.ShapeDtypeStruct((N, D // 2), U32)],
        scratch_shapes=[pltpu.VMEM((tm + POOL_HALO, pw), F32), pltpu.VMEM((4, gw // LANES, tm, LANES), F32)],
        name="mid",
    )(u, u, *attn_outs, xf, pool_bd, pool_scale, w_out_b, gate1, g_ffn, shift2, scale2, gate2,
      wsg_b, wsu_b, wsd_b)


def _route_kernel(h_ref, wr_ref, bias_ref, e_ref, g_ref, r_ref, cnt_ref, base_ref):
    i = pl.program_id(0)
    T = h_ref.shape[0]
    E = wr_ref.shape[0]
    gsz = E // N_EXPERT_GROUPS

    @pl.when(i == 0)
    def _():
        base_ref[...] = jnp.zeros_like(base_ref)

    h_lo, h_hi = _unpack_bf16_pairs(h_ref[...])
    half = h_lo.shape[1]
    nt = (((1,), (1,)), ((), ()))
    logits = (lax.dot_general(wr_ref[:, 0:half], h_lo, nt, preferred_element_type=F32)
              + lax.dot_general(wr_ref[:, half:], h_hi, nt, preferred_element_type=F32))
    scores = _sigmoid(logits)
    biased = scores + bias_ref[...]
    giota = lax.broadcasted_iota(I32, (gsz, T), 0)
    gscore = []
    for g in range(N_EXPERT_GROUPS):
        blk = biased[g * gsz:(g + 1) * gsz, :]
        m1 = jnp.max(blk, axis=0, keepdims=True)
        i1 = jnp.min(jnp.where(blk == m1, giota, gsz), axis=0, keepdims=True)
        m2 = jnp.max(jnp.where(giota == i1, NEG_INF, blk), axis=0, keepdims=True)
        gscore.append(m1 + m2)
    parts = []
    for g in range(N_EXPERT_GROUPS):
        beaten = jnp.zeros((1, T), I32)
        for o in range(N_EXPERT_GROUPS):
            if o == g:
                continue
            wins = (gscore[o] >= gscore[g]) if o < g else (gscore[o] > gscore[g])
            beaten = beaten + wins.astype(I32)
        keep = jnp.broadcast_to(beaten, (gsz, T)) < TOPK_GROUPS
        parts.append(jnp.where(keep, biased[g * gsz:(g + 1) * gsz, :], NEG_INF))
    cur = jnp.concatenate(parts, axis=0)
    eiota = lax.broadcasted_iota(I32, (E, T), 0)
    selm = jnp.zeros((E, T), F32)
    idxs, gates = [], []
    for k in range(TOP_K):
        m = jnp.max(cur, axis=0, keepdims=True)
        idx = jnp.min(jnp.where(cur == m, eiota, E), axis=0, keepdims=True)
        oh = eiota == idx
        gates.append(jnp.sum(jnp.where(oh, scores, 0.0), axis=0, keepdims=True))
        idxs.append(idx)
        cur = jnp.where(oh, NEG_INF, cur)
        selm = jnp.where(oh, 1.0, selm)
    gsum = gates[0]
    for k in range(1, TOP_K):
        gsum = gsum + gates[k]
    before = (lax.broadcasted_iota(I32, (T, T), 0) < lax.broadcasted_iota(I32, (T, T), 1)).astype(BF16)
    tot = jnp.dot(selm.astype(BF16), before, preferred_element_type=F32) + base_ref[...]
    for k in range(TOP_K):
        e_ref[k:k + 1, :] = idxs[k]
        g_ref[k:k + 1, :] = gates[k] / gsum * ROUTED_SCALE
        r_ref[k:k + 1, :] = jnp.sum(jnp.where(eiota == idxs[k], tot, 0.0), axis=0, keepdims=True).astype(I32)
    base_ref[...] = base_ref[...] + jnp.sum(selm, axis=1, keepdims=True)
    cnt_ref[...] = base_ref[...].astype(I32)


def _route(h2p, wr_t, bias_col):
    N = h2p.shape[0]
    E, D = wr_t.shape
    T = ROUTE_TILE
    col = lambda i: (0, i)
    return pl.pallas_call(
        _route_kernel,
        grid=(N // T,),
        in_specs=[pl.BlockSpec((T, D // 2), lambda i: (i, 0)),
                  pl.BlockSpec((E, D), lambda i: (0, 0)),
                  pl.BlockSpec((E, 1), lambda i: (0, 0))],
        out_specs=[pl.BlockSpec((TOP_K, T), col), pl.BlockSpec((TOP_K, T), col),
                   pl.BlockSpec((TOP_K, T), col), pl.BlockSpec((E, 1), lambda i: (0, 0))],
        out_shape=[jax.ShapeDtypeStruct((TOP_K, N), I32), jax.ShapeDtypeStruct((TOP_K, N), F32),
                   jax.ShapeDtypeStruct((TOP_K, N), I32), jax.ShapeDtypeStruct((E, 1), I32)],
        scratch_shapes=[pltpu.VMEM((E, 1), F32)],
        compiler_params=pltpu.CompilerParams(dimension_semantics=("arbitrary",)),
        name="route",
    )(h2p, wr_t, bias_col)


def _dest_kernel(e_ref, r_ref, off_ref, dc_ref):
    E = off_ref.shape[0]
    T = e_ref.shape[1]
    C = dc_ref.shape[2]
    eiota = lax.broadcasted_iota(I32, (E, T), 0)
    off = off_ref[...]
    for k in range(TOP_K):
        start = jnp.sum(jnp.where(eiota == e_ref[k:k + 1, :], off, 0.0), axis=0, keepdims=True)
        d = start.astype(I32) + r_ref[k:k + 1, :]
        for c in range(T // C):
            dc_ref[c, k:k + 1, :] = d[:, c * C:(c + 1) * C]


def _dest(eidx, rank, offs_col):
    N = eidx.shape[1]
    E = offs_col.shape[0]
    T = DEST_TILE
    C = SC_CHUNK
    col = lambda i: (0, i)
    return pl.pallas_call(
        _dest_kernel,
        grid=(N // T,),
        in_specs=[pl.BlockSpec((TOP_K, T), col), pl.BlockSpec((TOP_K, T), col),
                  pl.BlockSpec((E, 1), lambda i: (0, 0))],
        out_specs=pl.BlockSpec((T // C, TOP_K, C), lambda i: (i, 0, 0)),
        out_shape=jax.ShapeDtypeStruct((N // C, TOP_K, C), I32),
        name="dest",
    )(eidx, rank, offs_col)


def _sc_dispatch(dest_c, h2p, P):
    N, W = h2p.shape
    C = dest_c.shape[2]
    info = plsc.get_sparse_core_info()
    nw = info.num_cores * info.num_subcores
    per_w = N // C // nw
    mesh = plsc.VectorSubcoreMesh(core_axis_name="c", subcore_axis_name="s")

    @functools.partial(
        pl.kernel, mesh=mesh,
        out_type=jax.ShapeDtypeStruct((P, W), h2p.dtype),
        scratch_types=[pltpu.VMEM((TOP_K, C), I32), pltpu.VMEM((C, W), h2p.dtype), pltpu.SemaphoreType.DMA],
        name="sc_dispatch",
    )
    def k(dest_hbm, h_hbm, xs_hbm, idx_v, rows_v, sem):
        wid = lax.axis_index("s") * info.num_cores + lax.axis_index("c")

        @pl.loop(0, per_w)
        def _(j):
            ch = wid * per_w + j
            pltpu.sync_copy(dest_hbm.at[ch], idx_v)
            pltpu.sync_copy(h_hbm.at[pl.ds(ch * C, C)], rows_v)
            copies = [pltpu.async_copy(rows_v, xs_hbm.at[idx_v.at[kk]], sem) for kk in range(TOP_K)]
            for cp in copies:
                cp.wait()

    return k(dest_c, h2p)


def _gmm_kernel(bstart_ref, nbe_ref, cnt_ref, nu_ref, wg_ref, wu_ref, wd_ref, xs_hbm, ys_hbm,
                wgb, wub, wdb, xbuf, ybuf, xsem, ysem):
    e = pl.program_id(0)
    last = pl.num_programs(0) - 1
    ring, bm = xbuf.shape[0], xbuf.shape[1]
    nblk = ys_hbm.shape[0] // bm
    b0 = bstart_ref[e]
    nb = nbe_ref[e]
    nused = nu_ref[0]

    def x_copy(b, slot):
        return pltpu.make_async_copy(xs_hbm.at[pl.ds(pl.multiple_of(b * bm, bm), bm), :], xbuf.at[slot],
                                     xsem.at[slot])

    def y_copy(b, slot):
        return pltpu.make_async_copy(ybuf.at[slot], ys_hbm.at[pl.ds(pl.multiple_of(b * bm, bm), bm), :],
                                     ysem.at[slot])

    @pl.when(e == 0)
    def _():
        for j in range(ring - 1):
            @pl.when(j < nused)
            def _():
                x_copy(j, j).start()

    @pl.when(nb > 0)
    def _():
        wgb[...] = wg_ref[0].astype(BF16)
        wub[...] = wu_ref[0].astype(BF16)
        wdb[...] = wd_ref[0].astype(BF16)

        def prefetch(t):
            @pl.when(t < nused)
            def _():
                x_copy(t, jnp.bitwise_and(t, ring - 1)).start()

        def process(b, n):
            slots = [jnp.bitwise_and(b + j, ring - 1) for j in range(n)]
            for j in range(n):
                x_copy(b + j, slots[j]).wait()
            prefetch(b + ring - 1)
            for j in range(n):
                @pl.when(b + j >= ring)
                def _():
                    y_copy(b + j - ring, slots[j]).wait()
            rows = lax.broadcasted_iota(I32, (n * bm, 1), 0)
            valid = cnt_ref[e] - (b - b0) * bm
            xp = jnp.concatenate([xbuf[s] for s in slots], axis=0)
            xb = jnp.concatenate(_unpack_bf16_pairs(jnp.where(rows < valid, xp, jnp.uint32(0))), axis=1)
            a = jnp.dot(xb, wgb[...], preferred_element_type=F32)
            g = jnp.dot(xb, wub[...], preferred_element_type=F32)
            act = (a * _sigmoid(a)) * g
            yp = _pack_bf16_pairs(jnp.dot(act.astype(BF16), wdb[...], preferred_element_type=F32))
            for j in range(n):
                ybuf[slots[j]] = yp[j * bm:(j + 1) * bm, :]
                y_copy(b + j, slots[j]).start()
            for j in range(1, n):
                prefetch(b + ring - 1 + j)

        pairs = lax.shift_right_logical(nb, 1)
        lax.fori_loop(0, pairs, lambda j, c: (process(b0 + 2 * j, 2), c)[1], 0)

        @pl.when(jnp.bitwise_and(nb, 1) == 1)
        def _():
            process(b0 + nb - 1, 1)

    @pl.when(e == last)
    def _():
        for back in range(ring, 0, -1):
            @pl.when(nused >= back)
            def _():
                y_copy(nused - back, jnp.bitwise_and(nused - back, ring - 1)).wait()
        ybuf[0] = jnp.zeros(ybuf.shape[1:], ybuf.dtype)
        lax.fori_loop(nused, nblk, lambda b, c: (y_copy(b, 0).start(), c)[1], 0)
        lax.fori_loop(nused, nblk, lambda b, c: (y_copy(b, 0).wait(), c)[1], 0)


def _gmm(bstart, nb_e, counts, nused, xs, w_gate, w_up, w_down):
    P, W = xs.shape
    E, D, F = w_gate.shape
    bm = GMM_BLOCK
    wsel = lambda e, *_: (e, 0, 0)
    grid_spec = pltpu.PrefetchScalarGridSpec(
        num_scalar_prefetch=4,
        grid=(E,),
        in_specs=[pl.BlockSpec((1, D, F), wsel), pl.BlockSpec((1, D, F), wsel), pl.BlockSpec((1, F, D), wsel),
                  pl.BlockSpec(memory_space=pl.ANY)],
        out_specs=pl.BlockSpec(memory_space=pl.ANY),
        scratch_shapes=[pltpu.VMEM((D, F), BF16), pltpu.VMEM((D, F), BF16), pltpu.VMEM((F, D), BF16),
                        pltpu.VMEM((GMM_RING, bm, W), xs.dtype), pltpu.VMEM((GMM_RING, bm, W), xs.dtype),
                        pltpu.SemaphoreType.DMA((GMM_RING,)), pltpu.SemaphoreType.DMA((GMM_RING,))],
    )
    return pl.pallas_call(
        _gmm_kernel,
        grid_spec=grid_spec,
        out_shape=jax.ShapeDtypeStruct((P, W), xs.dtype),
        compiler_params=pltpu.CompilerParams(dimension_semantics=("arbitrary",)),
        name="gmm",
    )(bstart, nb_e, counts, nused, w_gate, w_up, w_down, xs)


def _sc_gather(dest_c, ys):
    nch, K, C = dest_c.shape
    W = ys.shape[1]
    H = C // 2
    info = plsc.get_sparse_core_info()
    nw = info.num_cores * info.num_subcores
    per_w = nch // nw
    nbuf = 3
    mesh = plsc.VectorSubcoreMesh(core_axis_name="c", subcore_axis_name="s")
    items = [(kk, hh) for kk in range(K) for hh in range(2)]

    @functools.partial(
        pl.kernel, mesh=mesh,
        out_type=jax.ShapeDtypeStruct((K, nch * C, W), ys.dtype),
        scratch_types=([pltpu.VMEM((K, C), I32)] + [pltpu.VMEM((H, W), ys.dtype)] * nbuf
                       + [pltpu.SemaphoreType.DMA] * (2 * nbuf)),
        name="sc_gather",
    )
    def k(dest_hbm, ys_hbm, yk_hbm, idx_v, *rest):
        bufs, gsem, wsem = rest[:nbuf], rest[nbuf:2 * nbuf], rest[2 * nbuf:]
        wid = lax.axis_index("s") * info.num_cores + lax.axis_index("c")

        @pl.loop(0, per_w)
        def _(j):
            ch = wid * per_w + j
            pltpu.sync_copy(dest_hbm.at[ch], idx_v)

            def gather(i):
                kk, hh = items[i]
                return pltpu.async_copy(ys_hbm.at[idx_v.at[kk, pl.ds(hh * H, H)]], bufs[i % nbuf], gsem[i % nbuf])

            def write(i):
                kk, hh = items[i]
                return pltpu.async_copy(bufs[i % nbuf], yk_hbm.at[kk, pl.ds(ch * C + hh * H, H)], wsem[i % nbuf])

            n = len(items)
            g = {0: gather(0), 1: gather(1)}
            w = {}
            for i in range(n):
                g[i].wait()
                w[i] = write(i)
                if i + 2 < n:
                    if i >= 1:
                        w.pop(i - 1).wait()
                    g[i + 2] = gather(i + 2)
            for i in sorted(w):
                w[i].wait()

    return k(dest_c, ys)


def _combine_kernel(yk_ref, gt_ref, gate2_ref, xacc_ref, gfin_ref, o_ref):
    gt = gt_ref[...]
    hi_mask = jnp.uint32(0xFFFF0000)
    acc_lo = acc_hi = None
    for k in range(TOP_K):
        p = yk_ref[k]
        g = gt[:, k:k + 1]
        lo = lax.bitcast_convert_type(p << 16, F32) * g
        hi = lax.bitcast_convert_type(p & hi_mask, F32) * g
        acc_lo = lo if k == 0 else acc_lo + lo
        acc_hi = hi if k == 0 else acc_hi + hi
    routed = jnp.concatenate([acc_lo, acc_hi], axis=1)
    x2 = xacc_ref[...] + gate2_ref[0] * routed
    o_ref[...] = _rms(x2) * gfin_ref[...]


def _combine(yk, gates_t, gate2, xacc, g_final, S):
    N, D = xacc.shape
    W = yk.shape[2]
    T = ROUTE_TILE
    spt = S // T
    return pl.pallas_call(
        _combine_kernel,
        grid=(N // T,),
        in_specs=[pl.BlockSpec((TOP_K, T, W), lambda i: (0, i, 0)),
                  pl.BlockSpec((T, TOP_K), lambda i: (i, 0)),
                  pl.BlockSpec((1, 1, D), lambda i: (i // spt, 0, 0)),
                  pl.BlockSpec((T, D), lambda i: (i, 0)),
                  pl.BlockSpec((1, D), lambda i: (0, 0))],
        out_specs=pl.BlockSpec((T, D), lambda i: (i, 0)),
        out_shape=jax.ShapeDtypeStruct((N, D), F32),
        name="combine",
    )(yk, gates_t, gate2, xacc, g_final)


def _layer(xf, B, S, mod, g_mix, w_in, pool_w, pool_scale, w_out, g_ffn, w_router, router_bias,
           w_gate, w_up, w_down, ws_gate, ws_up, ws_down):
    N, D = xf.shape
    E = w_router.shape[1]
    pw = pool_scale.shape[0]
    shift1, scale1, gate1, shift2, scale2, gate2 = [m.reshape(B, 1, D) for m in jnp.split(mod, 6, axis=-1)]
    u, qkv = _in_proj(xf, g_mix.reshape(1, D), shift1, scale1, w_in.astype(BF16), S, pw)
    attn_outs = []
    for a, d in zip(qkv, ATT_DILATIONS):
        o, lse = _attention(a, S // d // ATT_BLOCK)
        shape = (N, ATT_GROUP_WIDTH) if d == 1 else (B * d, S // d, ATT_GROUP_WIDTH)
        attn_outs += [o.reshape(shape), lse.reshape(shape)]
    ng = pool_w.shape[0]
    pool_bd = jnp.einsum('gcd,gh->gchd', pool_w, jnp.eye(ng, dtype=pool_w.dtype)).reshape(pw, pw).astype(BF16)
    xacc, h2 = _mid(u, attn_outs, xf, pool_bd, pool_scale.reshape(1, pw), w_out.astype(BF16), gate1,
                    g_ffn.reshape(1, D), shift2, scale2, gate2,
                    ws_gate.astype(BF16), ws_up.astype(BF16), ws_down.astype(BF16), S)
    eidx, gates, rank, counts = _route(h2, w_router.T.astype(BF16), router_bias.reshape(E, 1).astype(F32))
    bm = GMM_BLOCK
    nblk = N * TOP_K // bm + E
    nb_e = (counts[:, 0] + bm - 1) // bm
    bend = jnp.cumsum(nb_e)
    bstart = (bend - nb_e).astype(I32)
    nused = bend[-1:].astype(I32)
    dest_c = _dest(eidx, rank, (bstart * bm).astype(F32).reshape(E, 1))
    xs = _sc_dispatch(dest_c, h2, nblk * bm)
    ys = _gmm(bstart, nb_e.astype(I32), counts[:, 0], nused, xs, w_gate, w_up, w_down)
    return _sc_gather(dest_c, ys), gates.T, gate2, xacc


def kernel(x, c, w_ada, b_ada, g_mix, w_in, pool_w, pool_scale, w_out, g_ffn, w_router, router_bias,
           w_gate, w_up, w_down, ws_gate, ws_up, ws_down, g_final):
    B, S, D = x.shape
    depth = w_ada.shape[0]
    assert depth == 1, "the final residual is fused with the final norm, so exactly one layer is supported"
    assert S % (ATT_DILATIONS[-1] * ATT_BLOCK) == 0 and S % max(IN_TILE, MID_TILE, ATT_BLOCKS_PER_STEP * ATT_BLOCK) == 0
    xf = x.reshape(B * S, D)
    mod = _ada(c, w_ada[0], b_ada[0])
    yk, gates_t, gate2, xacc = _layer(
        xf, B, S, mod, g_mix[0], w_in[0], pool_w[0], pool_scale[0], w_out[0], g_ffn[0], w_router[0],
        router_bias[0], w_gate[0], w_up[0], w_down[0], ws_gate[0], ws_up[0], ws_down[0])
    out = _combine(yk, gates_t, gate2, xacc, g_final.reshape(1, D), S)
    return out.reshape(B, S, D)
```

```python
import functools

import jax
import jax.numpy as jnp
from jax import lax
from jax.experimental import pallas as pl
from jax.experimental.pallas import tpu as pltpu
from jax.experimental.pallas import tpu_sc as plsc

F32 = jnp.float32
BF16 = jnp.bfloat16
I32 = jnp.int32
U32 = jnp.uint32

LANES = 128
NORM_EPS = 1e-6
POOL_WINDOWS = (2, 4, 8, 16)
POOL_HALO = 16
ATT_DILATIONS = (1, 4, 16)
ATT_BLOCK = 128
ATT_HEADS_PER_GROUP = 4
ATT_HEAD_DIM = 64
ATT_GROUP_WIDTH = ATT_HEADS_PER_GROUP * ATT_HEAD_DIM
N_EXPERT_GROUPS = 8
TOPK_GROUPS = 4
TOP_K = 8
ROUTED_SCALE = 2.5

IN_TILE = 512
ATT_BLOCKS_PER_STEP = 8
MID_TILE = 512
ROUTE_TILE = 256
DEST_TILE = 1024
GMM_BLOCK = 256
GMM_RING = 8
SC_CHUNK = 128

NEG_INF = float("-inf")


def _sigmoid(v):
    return 1.0 / (1.0 + jnp.exp(-v))


def _rms(v):
    return v * lax.rsqrt(jnp.mean(v * v, axis=-1, keepdims=True) + NORM_EPS)


def _pack_bf16_pairs(v):
    n = v.shape[1] // 2
    lo = lax.bitcast_convert_type(v[:, :n].astype(BF16).astype(F32), U32)
    hi = lax.bitcast_convert_type(v[:, n:].astype(BF16).astype(F32), U32)
    return (hi & jnp.uint32(0xFFFF0000)) | (lo >> 16)


def _unpack_bf16_pairs(p):
    lo = lax.bitcast_convert_type(p << 16, F32).astype(BF16)
    hi = lax.bitcast_convert_type(p & jnp.uint32(0xFFFF0000), F32).astype(BF16)
    return lo, hi


def _ada_kernel(c_ref, w_ref, b_ref, o_ref):
    c = c_ref[...]
    cs = c * _sigmoid(c)
    o_ref[...] = jnp.dot(cs, w_ref[...], preferred_element_type=F32,
                         precision=lax.Precision.HIGHEST) + b_ref[...]


def _ada(c, w_ada, b_ada):
    B, D = c.shape
    W = w_ada.shape[1]
    tn = 1024
    return pl.pallas_call(
        _ada_kernel,
        grid=(W // tn,),
        in_specs=[pl.BlockSpec((B, D), lambda j: (0, 0)),
                  pl.BlockSpec((D, tn), lambda j: (0, j)),
                  pl.BlockSpec((1, tn), lambda j: (0, j))],
        out_specs=pl.BlockSpec((B, tn), lambda j: (0, j)),
        out_shape=jax.ShapeDtypeStruct((B, W), F32),
        name="ada",
    )(c, w_ada, b_ada.reshape(1, W))


def _in_kernel(x_ref, g_ref, sh_ref, sc_ref, w_ref, pool_ref, q0_ref, q1_ref, q2_ref, scr_ref):
    h = _rms(x_ref[...]) * g_ref[...]
    h = h * (1.0 + sc_ref[0]) + sh_ref[0]
    hb = h.astype(BF16)
    tm = x_ref.shape[0]
    pw = pool_ref.shape[1]
    gw = ATT_GROUP_WIDTH
    pool_ref[...] = jnp.dot(hb, w_ref[:, 0:pw], preferred_element_type=F32)
    for g, (out, d) in enumerate(zip((q0_ref, q1_ref, q2_ref), ATT_DILATIONS)):
        for sec in range(3):
            c0 = pw + sec * 3 * gw + g * gw
            res = jnp.dot(hb, w_ref[:, c0:c0 + gw], preferred_element_type=F32)
            if d == 1:
                out[:, sec * gw:(sec + 1) * gw] = res.astype(BF16)
            else:
                for c in range(gw // LANES):
                    scr_ref[c] = res[:, c * LANES:(c + 1) * LANES]
                for r in range(d):
                    for c in range(gw // LANES):
                        c1 = sec * gw + c * LANES
                        out[r, :, c1:c1 + LANES] = scr_ref[c, pl.ds(r, tm // d, stride=d), :].astype(BF16)


def _in_proj(xf, g_mix, shift1, scale1, w_in_b, S, pool_width):
    N, D = xf.shape
    tm = IN_TILE
    spt = S // tm
    vec = lambda i: (i // spt, 0, 0)
    row = lambda i: (i, 0)
    gw3 = 3 * ATT_GROUP_WIDTH
    B = N // S
    res_spec = lambda d: pl.BlockSpec((d, tm // d, gw3), lambda i: (i // spt, i % spt, 0))
    res_shape = lambda d: jax.ShapeDtypeStruct((B * d, S // d, gw3), BF16)
    outs = pl.pallas_call(
        _in_kernel,
        grid=(N // tm,),
        in_specs=[pl.BlockSpec((tm, D), row),
                  pl.BlockSpec((1, D), lambda i: (0, 0)),
                  pl.BlockSpec((1, 1, D), vec),
                  pl.BlockSpec((1, 1, D), vec),
                  pl.BlockSpec(w_in_b.shape, lambda i: (0, 0))],
        out_specs=[pl.BlockSpec((tm, pool_width), row), pl.BlockSpec((tm, gw3), row)]
                  + [res_spec(d) for d in ATT_DILATIONS[1:]],
        out_shape=[jax.ShapeDtypeStruct((N, pool_width), F32), jax.ShapeDtypeStruct((N, gw3), BF16)]
                  + [res_shape(d) for d in ATT_DILATIONS[1:]],
        scratch_shapes=[pltpu.VMEM((ATT_GROUP_WIDTH // LANES, tm, LANES), F32)],
        name="in_proj",
    )(xf, g_mix, shift1, scale1, w_in_b)
    return outs[0], [o.reshape(N, gw3) for o in outs[1:]]


def _attn_kernel(nbs, a_ref, halo_ref, o_ref, lse_ref, kv_ref, band_ref):
    i = pl.program_id(0)
    R = a_ref.shape[0] // ATT_BLOCK
    gw = ATT_GROUP_WIDTH
    blk = ATT_BLOCK
    nh = ATT_HEADS_PER_GROUP
    kv_ref[0:blk, :] = halo_ref[:, gw:3 * gw]
    kv_ref[blk:, :] = a_ref[:, gw:3 * gw]
    row = lax.broadcasted_iota(I32, (nh * blk, 2 * blk), 0) % blk
    col = lax.broadcasted_iota(I32, (nh * blk, 2 * blk), 1)
    band_ref[...] = jnp.where((col >= row) & (col <= row + blk), 0.0, NEG_INF)
    head_of_lane = lax.broadcasted_iota(I32, (blk, gw), 1) // ATT_HEAD_DIM
    nt = (((1,), (1,)), ((), ()))

    def body(jj, carry):
        r0 = pl.multiple_of(jj * blk, blk)
        qf = a_ref[pl.ds(r0, blk), 0:gw].astype(F32) * (ATT_HEAD_DIM ** -0.5)
        q4 = jnp.concatenate([jnp.where(head_of_lane == h, qf, 0.0) for h in range(nh)], axis=0).astype(BF16)
        kc = kv_ref[pl.ds(r0, 2 * blk), 0:gw]
        vc = kv_ref[pl.ds(r0, 2 * blk), gw:2 * gw]
        s = lax.dot_general(q4, kc, nt, preferred_element_type=F32) + band_ref[...]
        first = ((i * R + jj) % nbs) == 0
        s = jnp.where(col >= jnp.where(first, blk, 0), s, NEG_INF)
        m = jnp.max(s, axis=1, keepdims=True)
        p = jnp.exp(s - m)
        l = jnp.sum(p, axis=1, keepdims=True)
        o4 = jnp.dot(p.astype(BF16), vc, preferred_element_type=F32) / l
        lse4 = m + jnp.log(l)
        o = jnp.zeros((blk, gw), F32)
        lse = jnp.zeros((blk, gw), F32)
        for h in range(nh):
            hm = head_of_lane == h
            o = jnp.where(hm, o4[h * blk:(h + 1) * blk, :], o)
            lse = jnp.where(hm, lse4[h * blk:(h + 1) * blk, :], lse)
        o_ref[pl.ds(r0, blk), :] = o
        lse_ref[pl.ds(r0, blk), :] = lse
        return carry

    lax.fori_loop(0, R, body, 0, unroll=8)


def _attention(a, nbs):
    N = a.shape[0]
    R = ATT_BLOCKS_PER_STEP
    gw = ATT_GROUP_WIDTH
    tm = R * ATT_BLOCK
    return pl.pallas_call(
        functools.partial(_attn_kernel, nbs),
        grid=(N // tm,),
        in_specs=[pl.BlockSpec((tm, 3 * gw), lambda i: (i, 0)),
                  pl.BlockSpec((ATT_BLOCK, 3 * gw), lambda i: (jnp.maximum(i * R - 1, 0), 0))],
        out_specs=[pl.BlockSpec((tm, gw), lambda i: (i, 0))] * 2,
        out_shape=[jax.ShapeDtypeStruct((N, gw), F32)] * 2,
        scratch_shapes=[pltpu.VMEM((tm + ATT_BLOCK, 2 * gw), BF16),
                        pltpu.VMEM((ATT_HEADS_PER_GROUP * ATT_BLOCK, 2 * ATT_BLOCK), F32)],
        name="attn",
    )(a, a)


def _mid_kernel(spt, u_ref, uh_ref, o0_ref, l0_ref, o1_ref, l1_ref, o2_ref, l2_ref, x_ref,
                pbd_ref, psc_ref, wout_ref, gate1_ref, gffn_ref, sh2_ref, sc2_ref, gate2_ref,
                wsg_ref, wsu_ref, wsd_ref, xacc_ref, h2_ref, ext_ref, til_ref):
    i = pl.program_id(0)
    tm, pw = u_ref.shape
    si = i % spt
    u = u_ref[...]
    keep = jnp.full((POOL_HALO, pw), si, I32) > 0
    ext_ref[0:POOL_HALO, :] = jnp.where(keep, uh_ref[...], 0.0)
    ext_ref[POOL_HALO:, :] = u
    lane_grp = lax.broadcasted_iota(I32, (tm, pw), 1) // (pw // len(POOL_WINDOWS))
    s = u
    pooled = jnp.zeros((tm, pw), F32)
    for j in range(1, POOL_HALO):
        s = s + ext_ref[pl.ds(POOL_HALO - j, tm), :]
        if (j + 1) in POOL_WINDOWS:
            pooled = jnp.where(lane_grp == POOL_WINDOWS.index(j + 1), s, pooled)
    win = jnp.zeros((tm, pw), I32)
    for g, w in enumerate(POOL_WINDOWS):
        win = jnp.where(lane_grp == g, w, win)
    pos = si * tm + lax.broadcasted_iota(I32, (tm, pw), 0)
    cnt = jnp.minimum(pos + 1, win).astype(F32)
    pooled = pooled / cnt - u
    pool_out = jnp.dot(pooled.astype(BF16), pbd_ref[...], preferred_element_type=F32) * psc_ref[...]
    def token_order(slot, ref):
        d, n, w = ref.shape
        for r in range(d):
            for c in range(w // LANES):
                til_ref[slot, c, pl.ds(r, n, stride=d), :] = ref[r, :, c * LANES:(c + 1) * LANES]
        return jnp.concatenate([til_ref[slot, c] for c in range(w // LANES)], axis=1)

    l0 = l0_ref[...]
    l1 = token_order(0, l1_ref)
    l2 = token_order(1, l2_ref)
    m = jnp.maximum(jnp.maximum(l0, l1), l2)
    w0 = jnp.exp(l0 - m)
    w1 = jnp.exp(l1 - m)
    w2 = jnp.exp(l2 - m)
    attn = (w0 * o0_ref[...] + w1 * token_order(2, o1_ref) + w2 * token_order(3, o2_ref)) / (w0 + w1 + w2)
    mixed = (jnp.dot(pool_out.astype(BF16), wout_ref[0:pw, :], preferred_element_type=F32)
             + jnp.dot(attn.astype(BF16), wout_ref[pw:, :], preferred_element_type=F32))
    x1 = x_ref[...] + gate1_ref[0] * mixed
    h2 = _rms(x1) * gffn_ref[...]
    h2 = h2 * (1.0 + sc2_ref[0]) + sh2_ref[0]
    h2_ref[...] = _pack_bf16_pairs(h2)
    hb = h2.astype(BF16)
    a = jnp.dot(hb, wsg_ref[...], preferred_element_type=F32)
    b = jnp.dot(hb, wsu_ref[...], preferred_element_type=F32)
    act = (a * _sigmoid(a)) * b
    shared = jnp.dot(act.astype(BF16), wsd_ref[...], preferred_element_type=F32)
    xacc_ref[...] = x1 + gate2_ref[0] * shared


def _mid(u, attn_outs, xf, pool_bd, pool_scale, w_out_b, gate1, g_ffn, shift2, scale2, gate2,
         wsg_b, wsu_b, wsd_b, S):
    N, D = xf.shape
    pw = u.shape[1]
    tm = MID_TILE
    spt = S // tm
    row = lambda i: (i, 0)
    vec = lambda i: (i // spt, 0, 0)
    full = lambda a: pl.BlockSpec(a.shape, lambda i: (0,) * a.ndim)
    hpt = tm // POOL_HALO
    in_specs = [pl.BlockSpec((tm, pw), row),
                pl.BlockSpec((POOL_HALO, pw), lambda i: (jnp.maximum(i * hpt - 1, 0), 0))]
    gw = ATT_GROUP_WIDTH
    in_specs += [pl.BlockSpec((tm, gw), row)] * 2
    for d in ATT_DILATIONS[1:]:
        in_specs += [pl.BlockSpec((d, tm // d, gw), lambda i: (i // spt, i % spt, 0))] * 2
    in_specs += [pl.BlockSpec((tm, D), row), full(pool_bd), full(pool_scale), full(w_out_b),
                 pl.BlockSpec((1, 1, D), vec), full(g_ffn), pl.BlockSpec((1, 1, D), vec),
                 pl.BlockSpec((1, 1, D), vec), pl.BlockSpec((1, 1, D), vec),
                 full(wsg_b), full(wsu_b), full(wsd_b)]
    return pl.pallas_call(
        functools.partial(_mid_kernel, spt),
        grid=(N // tm,),
        in_specs=in_specs,
        out_specs=[pl.BlockSpec((tm, D), row), pl.BlockSpec((tm, D // 2), row)],
        out_shape=[jax.ShapeDtypeStruct((N, D), F32), jax.ShapeDtypeStruct((N, D // 2), U32)],
        scratch_shapes=[pltpu.VMEM((tm + POOL_HALO, pw), F32), pltpu.VMEM((4, gw // LANES, tm, LANES), F32)],
        name="mid",
    )(u, u, *attn_outs, xf, pool_bd, pool_scale, w_out_b, gate1, g_ffn, shift2, scale2, gate2,
      wsg_b, wsu_b, wsd_b)


def _route_kernel(h_ref, wr_ref, bias_ref, e_ref, g_ref, r_ref, cnt_ref, base_ref):
    i = pl.program_id(0)
    T = h_ref.shape[0]
    E = wr_ref.shape[0]
    gsz = E // N_EXPERT_GROUPS

    @pl.when(i == 0)
    def _():
        base_ref[...] = jnp.zeros_like(base_ref)

    h_lo, h_hi = _unpack_bf16_pairs(h_ref[...])
    half = h_lo.shape[1]
    nt = (((1,), (1,)), ((), ()))
    logits = (lax.dot_general(wr_ref[:, 0:half], h_lo, nt, preferred_element_type=F32)
              + lax.dot_general(wr_ref[:, half:], h_hi, nt, preferred_element_type=F32))
    scores = _sigmoid(logits)
    biased = scores + bias_ref[...]
    giota = lax.broadcasted_iota(I32, (gsz, T), 0)
    gscore = []
    for g in range(N_EXPERT_GROUPS):
        blk = biased[g * gsz:(g + 1) * gsz, :]
        m1 = jnp.max(blk, axis=0, keepdims=True)
        i1 = jnp.min(jnp.where(blk == m1, giota, gsz), axis=0, keepdims=True)
        m2 = jnp.max(jnp.where(giota == i1, NEG_INF, blk), axis=0, keepdims=True)
        gscore.append(m1 + m2)
    parts = []
    for g in range(N_EXPERT_GROUPS):
        beaten = jnp.zeros((1, T), I32)
        for o in range(N_EXPERT_GROUPS):
            if o == g:
                continue
            wins = (gscore[o] >= gscore[g]) if o < g else (gscore[o] > gscore[g])
            beaten = beaten + wins.astype(I32)
        keep = jnp.broadcast_to(beaten, (gsz, T)) < TOPK_GROUPS
        parts.append(jnp.where(keep, biased[g * gsz:(g + 1) * gsz, :], NEG_INF))
    cur = jnp.concatenate(parts, axis=0)
    eiota = lax.broadcasted_iota(I32, (E, T), 0)
    selm = jnp.zeros((E, T), F32)
    idxs, gates = [], []
    for k in range(TOP_K):
        m = jnp.max(cur, axis=0, keepdims=True)
        idx = jnp.min(jnp.where(cur == m, eiota, E), axis=0, keepdims=True)
        oh = eiota == idx
        gates.append(jnp.sum(jnp.where(oh, scores, 0.0), axis=0, keepdims=True))
        idxs.append(idx)
        cur = jnp.where(oh, NEG_INF, cur)
        selm = jnp.where(oh, 1.0, selm)
    gsum = gates[0]
    for k in range(1, TOP_K):
        gsum = gsum + gates[k]
    before = (lax.broadcasted_iota(I32, (T, T), 0) < lax.broadcasted_iota(I32, (T, T), 1)).astype(BF16)
    tot = jnp.dot(selm.astype(BF16), before, preferred_element_type=F32) + base_ref[...]
    for k in range(TOP_K):
        e_ref[k:k + 1, :] = idxs[k]
        g_ref[k:k + 1, :] = gates[k] / gsum * ROUTED_SCALE
        r_ref[k:k + 1, :] = jnp.sum(jnp.where(eiota == idxs[k], tot, 0.0), axis=0, keepdims=True).astype(I32)
    base_ref[...] = base_ref[...] + jnp.sum(selm, axis=1, keepdims=True)
    cnt_ref[...] = base_ref[...].astype(I32)


def _route(h2p, wr_t, bias_col):
    N = h2p.shape[0]
    E, D = wr_t.shape
    T = ROUTE_TILE
    col = lambda i: (0, i)
    return pl.pallas_call(
        _route_kernel,
        grid=(N // T,),
        in_specs=[pl.BlockSpec((T, D // 2), lambda i: (i, 0)),
                  pl.BlockSpec((E, D), lambda i: (0, 0)),
                  pl.BlockSpec((E, 1), lambda i: (0, 0))],
        out_specs=[pl.BlockSpec((TOP_K, T), col), pl.BlockSpec((TOP_K, T), col),
                   pl.BlockSpec((TOP_K, T), col), pl.BlockSpec((E, 1), lambda i: (0, 0))],
        out_shape=[jax.ShapeDtypeStruct((TOP_K, N), I32), jax.ShapeDtypeStruct((TOP_K, N), F32),
                   jax.ShapeDtypeStruct((TOP_K, N), I32), jax.ShapeDtypeStruct((E, 1), I32)],
        scratch_shapes=[pltpu.VMEM((E, 1), F32)],
        compiler_params=pltpu.CompilerParams(dimension_semantics=("arbitrary",)),
        name="route",
    )(h2p, wr_t, bias_col)


def _dest_kernel(e_ref, r_ref, off_ref, dc_ref):
    E = off_ref.shape[0]
    T = e_ref.shape[1]
    C = dc_ref.shape[2]
    eiota = lax.broadcasted_iota(I32, (E, T), 0)
    off = off_ref[...]
    for k in range(TOP_K):
        start = jnp.sum(jnp.where(eiota == e_ref[k:k + 1, :], off, 0.0), axis=0, keepdims=True)
        d = start.astype(I32) + r_ref[k:k + 1, :]
        for c in range(T // C):
            dc_ref[c, k:k + 1, :] = d[:, c * C:(c + 1) * C]


def _dest(eidx, rank, offs_col):
    N = eidx.shape[1]
    E = offs_col.shape[0]
    T = DEST_TILE
    C = SC_CHUNK
    col = lambda i: (0, i)
    return pl.pallas_call(
        _dest_kernel,
        grid=(N // T,),
        in_specs=[pl.BlockSpec((TOP_K, T), col), pl.BlockSpec((TOP_K, T), col),
                  pl.BlockSpec((E, 1), lambda i: (0, 0))],
        out_specs=pl.BlockSpec((T // C, TOP_K, C), lambda i: (i, 0, 0)),
        out_shape=jax.ShapeDtypeStruct((N // C, TOP_K, C), I32),
        name="dest",
    )(eidx, rank, offs_col)


def _sc_dispatch(dest_c, h2p, P):
    N, W = h2p.shape
    C = dest_c.shape[2]
    info = plsc.get_sparse_core_info()
    nw = info.num_cores * info.num_subcores
    per_w = N // C // nw
    mesh = plsc.VectorSubcoreMesh(core_axis_name="c", subcore_axis_name="s")

    @functools.partial(
        pl.kernel, mesh=mesh,
        out_type=jax.ShapeDtypeStruct((P, W), h2p.dtype),
        scratch_types=[pltpu.VMEM((TOP_K, C), I32), pltpu.VMEM((C, W), h2p.dtype), pltpu.SemaphoreType.DMA],
        name="sc_dispatch",
    )
    def k(dest_hbm, h_hbm, xs_hbm, idx_v, rows_v, sem):
        wid = lax.axis_index("s") * info.num_cores + lax.axis_index("c")

        @pl.loop(0, per_w)
        def _(j):
            ch = wid * per_w + j
            pltpu.sync_copy(dest_hbm.at[ch], idx_v)
            pltpu.sync_copy(h_hbm.at[pl.ds(ch * C, C)], rows_v)
            copies = [pltpu.async_copy(rows_v, xs_hbm.at[idx_v.at[kk]], sem) for kk in range(TOP_K)]
            for cp in copies:
                cp.wait()

    return k(dest_c, h2p)


def _gmm_kernel(bstart_ref, nbe_ref, cnt_ref, nu_ref, wg_ref, wu_ref, wd_ref, xs_hbm, ys_hbm,
                wgb, wub, wdb, xbuf, ybuf, xsem, ysem):
    e = pl.program_id(0)
    last = pl.num_programs(0) - 1
    ring, bm = xbuf.shape[0], xbuf.shape[1]
    nblk = ys_hbm.shape[0] // bm
    b0 = bstart_ref[e]
    nb = nbe_ref[e]
    nused = nu_ref[0]

    def x_copy(b, slot):
        return pltpu.make_async_copy(xs_hbm.at[pl.ds(pl.multiple_of(b * bm, bm), bm), :], xbuf.at[slot],
                                     xsem.at[slot])

    def y_copy(b, slot):
        return pltpu.make_async_copy(ybuf.at[slot], ys_hbm.at[pl.ds(pl.multiple_of(b * bm, bm), bm), :],
                                     ysem.at[slot])

    @pl.when(e == 0)
    def _():
        for j in range(ring - 1):
            @pl.when(j < nused)
            def _():
                x_copy(j, j).start()

    @pl.when(nb > 0)
    def _():
        wgb[...] = wg_ref[0].astype(BF16)
        wub[...] = wu_ref[0].astype(BF16)
        wdb[...] = wd_ref[0].astype(BF16)

        def prefetch(t):
            @pl.when(t < nused)
            def _():
                x_copy(t, jnp.bitwise_and(t, ring - 1)).start()

        def process(b, n):
            slots = [jnp.bitwise_and(b + j, ring - 1) for j in range(n)]
            for j in range(n):
                x_copy(b + j, slots[j]).wait()
            prefetch(b + ring - 1)
            for j in range(n):
                @pl.when(b + j >= ring)
                def _():
                    y_copy(b + j - ring, slots[j]).wait()
            rows = lax.broadcasted_iota(I32, (n * bm, 1), 0)
            valid = cnt_ref[e] - (b - b0) * bm
            xp = jnp.concatenate([xbuf[s] for s in slots], axis=0)
            xb = jnp.concatenate(_unpack_bf16_pairs(jnp.where(rows < valid, xp, jnp.uint32(0))), axis=1)
            a = jnp.dot(xb, wgb[...], preferred_element_type=F32)
            g = jnp.dot(xb, wub[...], preferred_element_type=F32)
            act = (a * _sigmoid(a)) * g
            yp = _pack_bf16_pairs(jnp.dot(act.astype(BF16), wdb[...], preferred_element_type=F32))
            for j in range(n):
                ybuf[slots[j]] = yp[j * bm:(j + 1) * bm, :]
                y_copy(b + j, slots[j]).start()
            for j in range(1, n):
                prefetch(b + ring - 1 + j)

        pairs = lax.shift_right_logical(nb, 1)
        lax.fori_loop(0, pairs, lambda j, c: (process(b0 + 2 * j, 2), c)[1], 0)

        @pl.when(jnp.bitwise_and(nb, 1) == 1)
        def _():
            process(b0 + nb - 1, 1)

    @pl.when(e == last)
    def _():
        for back in range(ring, 0, -1):
            @pl.when(nused >= back)
            def _():
                y_copy(nused - back, jnp.bitwise_and(nused - back, ring - 1)).wait()
        ybuf[0] = jnp.zeros(ybuf.shape[1:], ybuf.dtype)
        lax.fori_loop(nused, nblk, lambda b, c: (y_copy(b, 0).start(), c)[1], 0)
        lax.fori_loop(nused, nblk, lambda b, c: (y_copy(b, 0).wait(), c)[1], 0)


def _gmm(bstart, nb_e, counts, nused, xs, w_gate, w_up, w_down):
    P, W = xs.shape
    E, D, F = w_gate.shape
    bm = GMM_BLOCK
    wsel = lambda e, *_: (e, 0, 0)
    grid_spec = pltpu.PrefetchScalarGridSpec(
        num_scalar_prefetch=4,
        grid=(E,),
        in_specs=[pl.BlockSpec((1, D, F), wsel), pl.BlockSpec((1, D, F), wsel), pl.BlockSpec((1, F, D), wsel),
                  pl.BlockSpec(memory_space=pl.ANY)],
        out_specs=pl.BlockSpec(memory_space=pl.ANY),
        scratch_shapes=[pltpu.VMEM((D, F), BF16), pltpu.VMEM((D, F), BF16), pltpu.VMEM((F, D), BF16),
                        pltpu.VMEM((GMM_RING, bm, W), xs.dtype), pltpu.VMEM((GMM_RING, bm, W), xs.dtype),
                        pltpu.SemaphoreType.DMA((GMM_RING,)), pltpu.SemaphoreType.DMA((GMM_RING,))],
    )
    return pl.pallas_call(
        _gmm_kernel,
        grid_spec=grid_spec,
        out_shape=jax.ShapeDtypeStruct((P, W), xs.dtype),
        compiler_params=pltpu.CompilerParams(dimension_semantics=("arbitrary",)),
        name="gmm",
    )(bstart, nb_e, counts, nused, w_gate, w_up, w_down, xs)


def _sc_gather(dest_c, ys):
    nch, K, C = dest_c.shape
    W = ys.shape[1]
    H = C // 2
    info = plsc.get_sparse_core_info()
    nw = info.num_cores * info.num_subcores
    per_w = nch // nw
    nbuf = 3
    mesh = plsc.VectorSubcoreMesh(core_axis_name="c", subcore_axis_name="s")
    items = [(kk, hh) for kk in range(K) for hh in range(2)]

    @functools.partial(
        pl.kernel, mesh=mesh,
        out_type=jax.ShapeDtypeStruct((K, nch * C, W), ys.dtype),
        scratch_types=([pltpu.VMEM((K, C), I32)] + [pltpu.VMEM((H, W), ys.dtype)] * nbuf
                       + [pltpu.SemaphoreType.DMA] * (2 * nbuf)),
        name="sc_gather",
    )
    def k(dest_hbm, ys_hbm, yk_hbm, idx_v, *rest):
        bufs, gsem, wsem = rest[:nbuf], rest[nbuf:2 * nbuf], rest[2 * nbuf:]
        wid = lax.axis_index("s") * info.num_cores + lax.axis_index("c")

        @pl.loop(0, per_w)
        def _(j):
            ch = wid * per_w + j
            pltpu.sync_copy(dest_hbm.at[ch], idx_v)

            def gather(i):
                kk, hh = items[i]
                return pltpu.async_copy(ys_hbm.at[idx_v.at[kk, pl.ds(hh * H, H)]], bufs[i % nbuf], gsem[i % nbuf])

            def write(i):
                kk, hh = items[i]
                return pltpu.async_copy(bufs[i % nbuf], yk_hbm.at[kk, pl.ds(ch * C + hh * H, H)], wsem[i % nbuf])

            n = len(items)
            g = {0: gather(0), 1: gather(1)}
            w = {}
            for i in range(n):
                g[i].wait()
                w[i] = write(i)
                if i + 2 < n:
                    if i >= 1:
                        w.pop(i - 1).wait()
                    g[i + 2] = gather(i + 2)
            for i in sorted(w):
                w[i].wait()

    return k(dest_c, ys)


def _combine_kernel(yk_ref, gt_ref, gate2_ref, xacc_ref, gfin_ref, o_ref):
    gt = gt_ref[...]
    hi_mask = jnp.uint32(0xFFFF0000)
    acc_lo = acc_hi = None
    for k in range(TOP_K):
        p = yk_ref[k]
        g = gt[:, k:k + 1]
        lo = lax.bitcast_convert_type(p << 16, F32) * g
        hi = lax.bitcast_convert_type(p & hi_mask, F32) * g
        acc_lo = lo if k == 0 else acc_lo + lo
        acc_hi = hi if k == 0 else acc_hi + hi
    routed = jnp.concatenate([acc_lo, acc_hi], axis=1)
    x2 = xacc_ref[...] + gate2_ref[0] * routed
    o_ref[...] = _rms(x2) * gfin_ref[...]


def _combine(yk, gates_t, gate2, xacc, g_final, S):
    N, D = xacc.shape
    W = yk.shape[2]
    T = ROUTE_TILE
    spt = S // T
    return pl.pallas_call(
        _combine_kernel,
        grid=(N // T,),
        in_specs=[pl.BlockSpec((TOP_K, T, W), lambda i: (0, i, 0)),
                  pl.BlockSpec((T, TOP_K), lambda i: (i, 0)),
                  pl.BlockSpec((1, 1, D), lambda i: (i // spt, 0, 0)),
                  pl.BlockSpec((T, D), lambda i: (i, 0)),
                  pl.BlockSpec((1, D), lambda i: (0, 0))],
        out_specs=pl.BlockSpec((T, D), lambda i: (i, 0)),
        out_shape=jax.ShapeDtypeStruct((N, D), F32),
        name="combine",
    )(yk, gates_t, gate2, xacc, g_final)


def _layer(xf, B, S, mod, g_mix, w_in, pool_w, pool_scale, w_out, g_ffn, w_router, router_bias,
           w_gate, w_up, w_down, ws_gate, ws_up, ws_down):
    N, D = xf.shape
    E = w_router.shape[1]
    pw = pool_scale.shape[0]
    shift1, scale1, gate1, shift2, scale2, gate2 = [m.reshape(B, 1, D) for m in jnp.split(mod, 6, axis=-1)]
    u, qkv = _in_proj(xf, g_mix.reshape(1, D), shift1, scale1, w_in.astype(BF16), S, pw)
    attn_outs = []
    for a, d in zip(qkv, ATT_DILATIONS):
        o, lse = _attention(a, S // d // ATT_BLOCK)
        shape = (N, ATT_GROUP_WIDTH) if d == 1 else (B * d, S // d, ATT_GROUP_WIDTH)
        attn_outs += [o.reshape(shape), lse.reshape(shape)]
    ng = pool_w.shape[0]
    pool_bd = jnp.einsum('gcd,gh->gchd', pool_w, jnp.eye(ng, dtype=pool_w.dtype)).reshape(pw, pw).astype(BF16)
    xacc, h2 = _mid(u, attn_outs, xf, pool_bd, pool_scale.reshape(1, pw), w_out.astype(BF16), gate1,
                    g_ffn.reshape(1, D), shift2, scale2, gate2,
                    ws_gate.astype(BF16), ws_up.astype(BF16), ws_down.astype(BF16), S)
    eidx, gates, rank, counts = _route(h2, w_router.T.astype(BF16), router_bias.reshape(E, 1).astype(F32))
    bm = GMM_BLOCK
    nblk = N * TOP_K // bm + E
    nb_e = (counts[:, 0] + bm - 1) // bm
    bend = jnp.cumsum(nb_e)
    bstart = (bend - nb_e).astype(I32)
    nused = bend[-1:].astype(I32)
    dest_c = _dest(eidx, rank, (bstart * bm).astype(F32).reshape(E, 1))
    xs = _sc_dispatch(dest_c, h2, nblk * bm)
    ys = _gmm(bstart, nb_e.astype(I32), counts[:, 0], nused, xs, w_gate, w_up, w_down)
    return _sc_gather(dest_c, ys), gates.T, gate2, xacc


def kernel(x, c, w_ada, b_ada, g_mix, w_in, pool_w, pool_scale, w_out, g_ffn, w_router, router_bias,
           w_gate, w_up, w_down, ws_gate, ws_up, ws_down, g_final):
    B, S, D = x.shape
    depth = w_ada.shape[0]
    assert depth == 1, "the final residual is fused with the final norm, so exactly one layer is supported"
    assert S % (ATT_DILATIONS[-1] * ATT_BLOCK) == 0 and S % max(IN_TILE, MID_TILE, ATT_BLOCKS_PER_STEP * ATT_BLOCK) == 0
    xf = x.reshape(B * S, D)
    mod = _ada(c, w_ada[0], b_ada[0])
    yk, gates_t, gate2, xacc = _layer(
        xf, B, S, mod, g_mix[0], w_in[0], pool_w[0], pool_scale[0], w_out[0], g_ffn[0], w_router[0],
        router_bias[0], w_gate[0], w_up[0], w_down[0], ws_gate[0], ws_up[0], ws_down[0])
    out = _combine(yk, gates_t, gate2, xacc, g_final.reshape(1, D), S)
    return out.reshape(B, S, D)
```

```python
import functools

import jax
import jax.numpy as jnp
from jax import lax
from jax.experimental import pallas as pl
from jax.experimental.pallas import tpu as pltpu
from jax.experimental.pallas import tpu_sc as plsc

F32 = jnp.float32
BF16 = jnp.bfloat16
I32 = jnp.int32
U32 = jnp.uint32

LANES = 128
NORM_EPS = 1e-6
POOL_WINDOWS = (2, 4, 8, 16)
POOL_HALO = 16
ATT_DILATIONS = (1, 4, 16)
ATT_BLOCK = 128
ATT_HEADS_PER_GROUP = 4
ATT_HEAD_DIM = 64
ATT_GROUP_WIDTH = ATT_HEADS_PER_GROUP * ATT_HEAD_DIM
N_EXPERT_GROUPS = 8
TOPK_GROUPS = 4
TOP_K = 8
ROUTED_SCALE = 2.5

IN_TILE = 512
ATT_BLOCKS_PER_STEP = 8
MID_TILE = 512
ROUTE_TILE = 256
DEST_TILE = 1024
GMM_BLOCK = 256
GMM_RING = 8
GMM_EXPERTS_PER_STEP = 2
SC_CHUNK = 128

NEG_INF = float("-inf")


def _sigmoid(v):
    return 1.0 / (1.0 + jnp.exp(-v))


def _rms(v):
    return v * lax.rsqrt(jnp.mean(v * v, axis=-1, keepdims=True) + NORM_EPS)


def _pack_bf16_pairs(v):
    n = v.shape[1] // 2
    lo = lax.bitcast_convert_type(v[:, :n].astype(BF16).astype(F32), U32)
    hi = lax.bitcast_convert_type(v[:, n:].astype(BF16).astype(F32), U32)
    return (hi & jnp.uint32(0xFFFF0000)) | (lo >> 16)


def _unpack_bf16_pairs(p):
    lo = lax.bitcast_convert_type(p << 16, F32).astype(BF16)
    hi = lax.bitcast_convert_type(p & jnp.uint32(0xFFFF0000), F32).astype(BF16)
    return lo, hi


def _ada_kernel(c_ref, w_ref, b_ref, o_ref):
    c = c_ref[...]
    cs = c * _sigmoid(c)
    o_ref[...] = jnp.dot(cs, w_ref[...], preferred_element_type=F32,
                         precision=lax.Precision.HIGHEST) + b_ref[...]


def _ada(c, w_ada, b_ada):
    B, D = c.shape
    W = w_ada.shape[1]
    tn = 1024
    return pl.pallas_call(
        _ada_kernel,
        grid=(W // tn,),
        in_specs=[pl.BlockSpec((B, D), lambda j: (0, 0)),
                  pl.BlockSpec((D, tn), lambda j: (0, j)),
                  pl.BlockSpec((1, tn), lambda j: (0, j))],
        out_specs=pl.BlockSpec((B, tn), lambda j: (0, j)),
        out_shape=jax.ShapeDtypeStruct((B, W), F32),
        name="ada",
    )(c, w_ada, b_ada.reshape(1, W))


def _in_kernel(x_ref, g_ref, sh_ref, sc_ref, w_ref, pool_ref, q0_ref, q1_ref, q2_ref, scr_ref):
    h = _rms(x_ref[...]) * g_ref[...]
    h = h * (1.0 + sc_ref[0]) + sh_ref[0]
    hb = h.astype(BF16)
    tm = x_ref.shape[0]
    pw = pool_ref.shape[1]
    gw = ATT_GROUP_WIDTH
    pool_ref[...] = jnp.dot(hb, w_ref[:, 0:pw], preferred_element_type=F32)
    for g, (out, d) in enumerate(zip((q0_ref, q1_ref, q2_ref), ATT_DILATIONS)):
        for sec in range(3):
            c0 = pw + sec * 3 * gw + g * gw
            res = jnp.dot(hb, w_ref[:, c0:c0 + gw], preferred_element_type=F32)
            if d == 1:
                out[:, sec * gw:(sec + 1) * gw] = res.astype(BF16)
            else:
                for c in range(gw // LANES):
                    scr_ref[c] = res[:, c * LANES:(c + 1) * LANES]
                for r in range(d):
                    for c in range(gw // LANES):
                        c1 = sec * gw + c * LANES
                        out[r, :, c1:c1 + LANES] = scr_ref[c, pl.ds(r, tm // d, stride=d), :].astype(BF16)


def _in_proj(xf, g_mix, shift1, scale1, w_in_b, S, pool_width):
    N, D = xf.shape
    tm = IN_TILE
    spt = S // tm
    vec = lambda i: (i // spt, 0, 0)
    row = lambda i: (i, 0)
    gw3 = 3 * ATT_GROUP_WIDTH
    B = N // S
    res_spec = lambda d: pl.BlockSpec((d, tm // d, gw3), lambda i: (i // spt, i % spt, 0))
    res_shape = lambda d: jax.ShapeDtypeStruct((B * d, S // d, gw3), BF16)
    outs = pl.pallas_call(
        _in_kernel,
        grid=(N // tm,),
        in_specs=[pl.BlockSpec((tm, D), row),
                  pl.BlockSpec((1, D), lambda i: (0, 0)),
                  pl.BlockSpec((1, 1, D), vec),
                  pl.BlockSpec((1, 1, D), vec),
                  pl.BlockSpec(w_in_b.shape, lambda i: (0, 0))],
        out_specs=[pl.BlockSpec((tm, pool_width), row), pl.BlockSpec((tm, gw3), row)]
                  + [res_spec(d) for d in ATT_DILATIONS[1:]],
        out_shape=[jax.ShapeDtypeStruct((N, pool_width), F32), jax.ShapeDtypeStruct((N, gw3), BF16)]
                  + [res_shape(d) for d in ATT_DILATIONS[1:]],
        scratch_shapes=[pltpu.VMEM((ATT_GROUP_WIDTH // LANES, tm, LANES), F32)],
        name="in_proj",
    )(xf, g_mix, shift1, scale1, w_in_b)
    return outs[0], [o.reshape(N, gw3) for o in outs[1:]]


def _attn_kernel(nbs, a_ref, halo_ref, o_ref, lse_ref, kv_ref, band_ref):
    i = pl.program_id(0)
    R = a_ref.shape[0] // ATT_BLOCK
    gw = ATT_GROUP_WIDTH
    blk = ATT_BLOCK
    nh = ATT_HEADS_PER_GROUP
    kv_ref[0:blk, :] = halo_ref[:, gw:3 * gw]
    kv_ref[blk:, :] = a_ref[:, gw:3 * gw]
    row = lax.broadcasted_iota(I32, (nh * blk, 2 * blk), 0) % blk
    col = lax.broadcasted_iota(I32, (nh * blk, 2 * blk), 1)
    band_ref[...] = jnp.where((col >= row) & (col <= row + blk), 0.0, NEG_INF)
    head_of_lane = lax.broadcasted_iota(I32, (blk, gw), 1) // ATT_HEAD_DIM
    nt = (((1,), (1,)), ((), ()))

    def body(jj, carry):
        r0 = pl.multiple_of(jj * blk, blk)
        qf = a_ref[pl.ds(r0, blk), 0:gw].astype(F32) * (ATT_HEAD_DIM ** -0.5)
        q4 = jnp.concatenate([jnp.where(head_of_lane == h, qf, 0.0) for h in range(nh)], axis=0).astype(BF16)
        kc = kv_ref[pl.ds(r0, 2 * blk), 0:gw]
        vc = kv_ref[pl.ds(r0, 2 * blk), gw:2 * gw]
        s = lax.dot_general(q4, kc, nt, preferred_element_type=F32) + band_ref[...]
        first = ((i * R + jj) % nbs) == 0
        s = jnp.where(col >= jnp.where(first, blk, 0), s, NEG_INF)
        m = jnp.max(s, axis=1, keepdims=True)
        p = jnp.exp(s - m)
        l = jnp.sum(p, axis=1, keepdims=True)
        o4 = jnp.dot(p.astype(BF16), vc, preferred_element_type=F32) / l
        lse4 = m + jnp.log(l)
        o = jnp.zeros((blk, gw), F32)
        lse = jnp.zeros((blk, gw), F32)
        for h in range(nh):
            hm = head_of_lane == h
            o = jnp.where(hm, o4[h * blk:(h + 1) * blk, :], o)
            lse = jnp.where(hm, lse4[h * blk:(h + 1) * blk, :], lse)
        o_ref[pl.ds(r0, blk), :] = o
        lse_ref[pl.ds(r0, blk), :] = lse
        return carry

    lax.fori_loop(0, R, body, 0, unroll=8)


def _attention(a, nbs):
    N = a.shape[0]
    R = ATT_BLOCKS_PER_STEP
    gw = ATT_GROUP_WIDTH
    tm = R * ATT_BLOCK
    return pl.pallas_call(
        functools.partial(_attn_kernel, nbs),
        grid=(N // tm,),
        in_specs=[pl.BlockSpec((tm, 3 * gw), lambda i: (i, 0)),
                  pl.BlockSpec((ATT_BLOCK, 3 * gw), lambda i: (jnp.maximum(i * R - 1, 0), 0))],
        out_specs=[pl.BlockSpec((tm, gw), lambda i: (i, 0))] * 2,
        out_shape=[jax.ShapeDtypeStruct((N, gw), F32)] * 2,
        scratch_shapes=[pltpu.VMEM((tm + ATT_BLOCK, 2 * gw), BF16),
                        pltpu.VMEM((ATT_HEADS_PER_GROUP * ATT_BLOCK, 2 * ATT_BLOCK), F32)],
        name="attn",
    )(a, a)


def _mid_kernel(spt, u_ref, uh_ref, o0_ref, l0_ref, o1_ref, l1_ref, o2_ref, l2_ref, x_ref,
                pbd_ref, psc_ref, wout_ref, gate1_ref, gffn_ref, sh2_ref, sc2_ref, gate2_ref,
                wsg_ref, wsu_ref, wsd_ref, xacc_ref, h2_ref, ext_ref, til_ref):
    i = pl.program_id(0)
    tm, pw = u_ref.shape
    si = i % spt
    u = u_ref[...]
    keep = jnp.full((POOL_HALO, pw), si, I32) > 0
    ext_ref[0:POOL_HALO, :] = jnp.where(keep, uh_ref[...], 0.0)
    ext_ref[POOL_HALO:, :] = u
    lane_grp = lax.broadcasted_iota(I32, (tm, pw), 1) // (pw // len(POOL_WINDOWS))
    s = u
    pooled = jnp.zeros((tm, pw), F32)
    for j in range(1, POOL_HALO):
        s = s + ext_ref[pl.ds(POOL_HALO - j, tm), :]
        if (j + 1) in POOL_WINDOWS:
            pooled = jnp.where(lane_grp == POOL_WINDOWS.index(j + 1), s, pooled)
    win = jnp.zeros((tm, pw), I32)
    for g, w in enumerate(POOL_WINDOWS):
        win = jnp.where(lane_grp == g, w, win)
    pos = si * tm + lax.broadcasted_iota(I32, (tm, pw), 0)
    cnt = jnp.minimum(pos + 1, win).astype(F32)
    pooled = pooled / cnt - u
    pool_out = jnp.dot(pooled.astype(BF16), pbd_ref[...], preferred_element_type=F32) * psc_ref[...]
    def token_order(slot, ref):
        d, n, w = ref.shape
        for r in range(d):
            for c in range(w // LANES):
                til_ref[slot, c, pl.ds(r, n, stride=d), :] = ref[r, :, c * LANES:(c + 1) * LANES]
        return jnp.concatenate([til_ref[slot, c] for c in range(w // LANES)], axis=1)

    l0 = l0_ref[...]
    l1 = token_order(0, l1_ref)
    l2 = token_order(1, l2_ref)
    m = jnp.maximum(jnp.maximum(l0, l1), l2)
    w0 = jnp.exp(l0 - m)
    w1 = jnp.exp(l1 - m)
    w2 = jnp.exp(l2 - m)
    attn = (w0 * o0_ref[...] + w1 * token_order(2, o1_ref) + w2 * token_order(3, o2_ref)) / (w0 + w1 + w2)
    mixed = (jnp.dot(pool_out.astype(BF16), wout_ref[0:pw, :], preferred_element_type=F32)
             + jnp.dot(attn.astype(BF16), wout_ref[pw:, :], preferred_element_type=F32))
    x1 = x_ref[...] + gate1_ref[0] * mixed
    h2 = _rms(x1) * gffn_ref[...]
    h2 = h2 * (1.0 + sc2_ref[0]) + sh2_ref[0]
    h2_ref[...] = _pack_bf16_pairs(h2)
    hb = h2.astype(BF16)
    a = jnp.dot(hb, wsg_ref[...], preferred_element_type=F32)
    b = jnp.dot(hb, wsu_ref[...], preferred_element_type=F32)
    act = (a * _sigmoid(a)) * b
    shared = jnp.dot(act.astype(BF16), wsd_ref[...], preferred_element_type=F32)
    xacc_ref[...] = x1 + gate2_ref[0] * shared


def _mid(u, attn_outs, xf, pool_bd, pool_scale, w_out_b, gate1, g_ffn, shift2, scale2, gate2,
         wsg_b, wsu_b, wsd_b, S):
    N, D = xf.shape
    pw = u.shape[1]
    tm = MID_TILE
    spt = S // tm
    row = lambda i: (i, 0)
    vec = lambda i: (i // spt, 0, 0)
    full = lambda a: pl.BlockSpec(a.shape, lambda i: (0,) * a.ndim)
    hpt = tm // POOL_HALO
    in_specs = [pl.BlockSpec((tm, pw), row),
                pl.BlockSpec((POOL_HALO, pw), lambda i: (jnp.maximum(i * hpt - 1, 0), 0))]
    gw = ATT_GROUP_WIDTH
    in_specs += [pl.BlockSpec((tm, gw), row)] * 2
    for d in ATT_DILATIONS[1:]:
        in_specs += [pl.BlockSpec((d, tm // d, gw), lambda i: (i // spt, i % spt, 0))] * 2
    in_specs += [pl.BlockSpec((tm, D), row), full(pool_bd), full(pool_scale), full(w_out_b),
                 pl.BlockSpec((1, 1, D), vec), full(g_ffn), pl.BlockSpec((1, 1, D), vec),
                 pl.BlockSpec((1, 1, D), vec), pl.BlockSpec((1, 1, D), vec),
                 full(wsg_b), full(wsu_b), full(wsd_b)]
    return pl.pallas_call(
        functools.partial(_mid_kernel, spt),
        grid=(N // tm,),
        in_specs=in_specs,
        out_specs=[pl.BlockSpec((tm, D), row), pl.BlockSpec((tm, D // 2), row)],
        out_shape=[jax.ShapeDtypeStruct((N, D), F32), jax.ShapeDtypeStruct((N, D // 2), U32)],
        scratch_shapes=[pltpu.VMEM((tm + POOL_HALO, pw), F32), pltpu.VMEM((4, gw // LANES, tm, LANES), F32)],
        name="mid",
    )(u, u, *attn_outs, xf, pool_bd, pool_scale, w_out_b, gate1, g_ffn, shift2, scale2, gate2,
      wsg_b, wsu_b, wsd_b)


def _route_kernel(h_ref, wr_ref, bias_ref, e_ref, g_ref, r_ref, cnt_ref, base_ref):
    i = pl.program_id(0)
    T = h_ref.shape[0]
    E = wr_ref.shape[0]
    gsz = E // N_EXPERT_GROUPS

    @pl.when(i == 0)
    def _():
        base_ref[...] = jnp.zeros_like(base_ref)

    h_lo, h_hi = _unpack_bf16_pairs(h_ref[...])
    half = h_lo.shape[1]
    nt = (((1,), (1,)), ((), ()))
    logits = (lax.dot_general(wr_ref[:, 0:half], h_lo, nt, preferred_element_type=F32)
              + lax.dot_general(wr_ref[:, half:], h_hi, nt, preferred_element_type=F32))
    scores = _sigmoid(logits)
    biased = scores + bias_ref[...]
    giota = lax.broadcasted_iota(I32, (gsz, T), 0)
    gscore = []
    for g in range(N_EXPERT_GROUPS):
        blk = biased[g * gsz:(g + 1) * gsz, :]
        m1 = jnp.max(blk, axis=0, keepdims=True)
        i1 = jnp.min(jnp.where(blk == m1, giota, gsz), axis=0, keepdims=True)
        m2 = jnp.max(jnp.where(giota == i1, NEG_INF, blk), axis=0, keepdims=True)
        gscore.append(m1 + m2)
    parts = []
    for g in range(N_EXPERT_GROUPS):
        beaten = jnp.zeros((1, T), I32)
        for o in range(N_EXPERT_GROUPS):
            if o == g:
                continue
            wins = (gscore[o] >= gscore[g]) if o < g else (gscore[o] > gscore[g])
            beaten = beaten + wins.astype(I32)
        keep = jnp.broadcast_to(beaten, (gsz, T)) < TOPK_GROUPS
        parts.append(jnp.where(keep, biased[g * gsz:(g + 1) * gsz, :], NEG_INF))
    cur = jnp.concatenate(parts, axis=0)
    eiota = lax.broadcasted_iota(I32, (E, T), 0)
    selm = jnp.zeros((E, T), F32)
    idxs, gates = [], []
    for k in range(TOP_K):
        m = jnp.max(cur, axis=0, keepdims=True)
        idx = jnp.min(jnp.where(cur == m, eiota, E), axis=0, keepdims=True)
        oh = eiota == idx
        gates.append(jnp.sum(jnp.where(oh, scores, 0.0), axis=0, keepdims=True))
        idxs.append(idx)
        cur = jnp.where(oh, NEG_INF, cur)
        selm = jnp.where(oh, 1.0, selm)
    gsum = gates[0]
    for k in range(1, TOP_K):
        gsum = gsum + gates[k]
    before = (lax.broadcasted_iota(I32, (T, T), 0) < lax.broadcasted_iota(I32, (T, T), 1)).astype(BF16)
    tot = jnp.dot(selm.astype(BF16), before, preferred_element_type=F32) + base_ref[...]
    for k in range(TOP_K):
        e_ref[k:k + 1, :] = idxs[k]
        g_ref[k:k + 1, :] = gates[k] / gsum * ROUTED_SCALE
        r_ref[k:k + 1, :] = jnp.sum(jnp.where(eiota == idxs[k], tot, 0.0), axis=0, keepdims=True).astype(I32)
    base_ref[...] = base_ref[...] + jnp.sum(selm, axis=1, keepdims=True)
    cnt_ref[...] = base_ref[...].astype(I32)


def _route(h2p, wr_t, bias_col):
    N = h2p.shape[0]
    E, D = wr_t.shape
    T = ROUTE_TILE
    col = lambda i: (0, i)
    return pl.pallas_call(
        _route_kernel,
        grid=(N // T,),
        in_specs=[pl.BlockSpec((T, D // 2), lambda i: (i, 0)),
                  pl.BlockSpec((E, D), lambda i: (0, 0)),
                  pl.BlockSpec((E, 1), lambda i: (0, 0))],
        out_specs=[pl.BlockSpec((TOP_K, T), col), pl.BlockSpec((TOP_K, T), col),
                   pl.BlockSpec((TOP_K, T), col), pl.BlockSpec((E, 1), lambda i: (0, 0))],
        out_shape=[jax.ShapeDtypeStruct((TOP_K, N), I32), jax.ShapeDtypeStruct((TOP_K, N), F32),
                   jax.ShapeDtypeStruct((TOP_K, N), I32), jax.ShapeDtypeStruct((E, 1), I32)],
        scratch_shapes=[pltpu.VMEM((E, 1), F32)],
        compiler_params=pltpu.CompilerParams(dimension_semantics=("arbitrary",)),
        name="route",
    )(h2p, wr_t, bias_col)


def _dest_kernel(e_ref, r_ref, off_ref, dc_ref):
    E = off_ref.shape[0]
    T = e_ref.shape[1]
    C = dc_ref.shape[2]
    eiota = lax.broadcasted_iota(I32, (E, T), 0)
    off = off_ref[...]
    for k in range(TOP_K):
        start = jnp.sum(jnp.where(eiota == e_ref[k:k + 1, :], off, 0.0), axis=0, keepdims=True)
        d = start.astype(I32) + r_ref[k:k + 1, :]
        for c in range(T // C):
            dc_ref[c, k:k + 1, :] = d[:, c * C:(c + 1) * C]


def _dest(eidx, rank, offs_col):
    N = eidx.shape[1]
    E = offs_col.shape[0]
    T = DEST_TILE
    C = SC_CHUNK
    col = lambda i: (0, i)
    return pl.pallas_call(
        _dest_kernel,
        grid=(N // T,),
        in_specs=[pl.BlockSpec((TOP_K, T), col), pl.BlockSpec((TOP_K, T), col),
                  pl.BlockSpec((E, 1), lambda i: (0, 0))],
        out_specs=pl.BlockSpec((T // C, TOP_K, C), lambda i: (i, 0, 0)),
        out_shape=jax.ShapeDtypeStruct((N // C, TOP_K, C), I32),
        name="dest",
    )(eidx, rank, offs_col)


def _sc_dispatch(dest_c, h2p, P):
    N, W = h2p.shape
    C = dest_c.shape[2]
    info = plsc.get_sparse_core_info()
    nw = info.num_cores * info.num_subcores
    per_w = N // C // nw
    mesh = plsc.VectorSubcoreMesh(core_axis_name="c", subcore_axis_name="s")

    @functools.partial(
        pl.kernel, mesh=mesh,
        out_type=jax.ShapeDtypeStruct((P, W), h2p.dtype),
        scratch_types=[pltpu.VMEM((TOP_K, C), I32), pltpu.VMEM((C, W), h2p.dtype), pltpu.SemaphoreType.DMA],
        name="sc_dispatch",
    )
    def k(dest_hbm, h_hbm, xs_hbm, idx_v, rows_v, sem):
        wid = lax.axis_index("s") * info.num_cores + lax.axis_index("c")

        @pl.loop(0, per_w)
        def _(j):
            ch = wid * per_w + j
            pltpu.sync_copy(dest_hbm.at[ch], idx_v)
            pltpu.sync_copy(h_hbm.at[pl.ds(ch * C, C)], rows_v)
            copies = [pltpu.async_copy(rows_v, xs_hbm.at[idx_v.at[kk]], sem) for kk in range(TOP_K)]
            for cp in copies:
                cp.wait()

    return k(dest_c, h2p)


def _gmm_kernel(bstart_ref, nbe_ref, cnt_ref, nu_ref, wg_ref, wu_ref, wd_ref, xs_hbm, ys_hbm,
                wgb, wub, wdb, xbuf, ybuf, xsem, ysem):
    step = pl.program_id(0)
    last = pl.num_programs(0) - 1
    epg = wg_ref.shape[0]
    ring, bm = xbuf.shape[0], xbuf.shape[1]
    nblk = ys_hbm.shape[0] // bm
    nused = nu_ref[0]

    def x_copy(b, slot):
        return pltpu.make_async_copy(xs_hbm.at[pl.ds(pl.multiple_of(b * bm, bm), bm), :], xbuf.at[slot],
                                     xsem.at[slot])

    def y_copy(b, slot):
        return pltpu.make_async_copy(ybuf.at[slot], ys_hbm.at[pl.ds(pl.multiple_of(b * bm, bm), bm), :],
                                     ysem.at[slot])

    @pl.when(step == 0)
    def _():
        for j in range(ring - 1):
            @pl.when(j < nused)
            def _():
                x_copy(j, j).start()

    def run_expert(ee):
        e = step * epg + ee
        b0 = bstart_ref[e]
        nb = nbe_ref[e]

        @pl.when(nb > 0)
        def _():
            wgb[...] = wg_ref[ee].astype(BF16)
            wub[...] = wu_ref[ee].astype(BF16)
            wdb[...] = wd_ref[ee].astype(BF16)

            def prefetch(t):
                @pl.when(t < nused)
                def _():
                    x_copy(t, jnp.bitwise_and(t, ring - 1)).start()

            def process(b, n):
                slots = [jnp.bitwise_and(b + j, ring - 1) for j in range(n)]
                for j in range(n):
                    x_copy(b + j, slots[j]).wait()
                prefetch(b + ring - 1)
                for j in range(n):
                    @pl.when(b + j >= ring)
                    def _():
                        y_copy(b + j - ring, slots[j]).wait()
                rows = lax.broadcasted_iota(I32, (n * bm, 1), 0)
                valid = cnt_ref[e] - (b - b0) * bm
                xp = jnp.concatenate([xbuf[s] for s in slots], axis=0)
                xb = jnp.concatenate(_unpack_bf16_pairs(jnp.where(rows < valid, xp, jnp.uint32(0))), axis=1)
                a = jnp.dot(xb, wgb[...], preferred_element_type=F32)
                g = jnp.dot(xb, wub[...], preferred_element_type=F32)
                act = (a * _sigmoid(a)) * g
                yp = _pack_bf16_pairs(jnp.dot(act.astype(BF16), wdb[...], preferred_element_type=F32))
                for j in range(n):
                    ybuf[slots[j]] = yp[j * bm:(j + 1) * bm, :]
                    y_copy(b + j, slots[j]).start()
                for j in range(1, n):
                    prefetch(b + ring - 1 + j)

            quads = lax.shift_right_logical(nb, 2)
            lax.fori_loop(0, quads, lambda j, c: (process(b0 + 4 * j, 4), c)[1], 0)
            rest2 = jnp.bitwise_and(nb, 2)

            @pl.when(rest2 != 0)
            def _():
                process(b0 + 4 * quads, 2)

            @pl.when(jnp.bitwise_and(nb, 1) != 0)
            def _():
                process(b0 + 4 * quads + rest2, 1)

    for ee in range(epg):
        run_expert(ee)

    @pl.when(step == last)
    def _():
        for back in range(ring, 0, -1):
            @pl.when(nused >= back)
            def _():
                y_copy(nused - back, jnp.bitwise_and(nused - back, ring - 1)).wait()
        ybuf[0] = jnp.zeros(ybuf.shape[1:], ybuf.dtype)
        lax.fori_loop(nused, nblk, lambda b, c: (y_copy(b, 0).start(), c)[1], 0)
        lax.fori_loop(nused, nblk, lambda b, c: (y_copy(b, 0).wait(), c)[1], 0)


def _gmm(bstart, nb_e, counts, nused, xs, w_gate, w_up, w_down):
    P, W = xs.shape
    E, D, F = w_gate.shape
    bm = GMM_BLOCK
    epg = GMM_EXPERTS_PER_STEP
    wsel = lambda s, *_: (s, 0, 0)
    grid_spec = pltpu.PrefetchScalarGridSpec(
        num_scalar_prefetch=4,
        grid=(E // epg,),
        in_specs=[pl.BlockSpec((epg, D, F), wsel), pl.BlockSpec((epg, D, F), wsel), pl.BlockSpec((epg, F, D), wsel),
                  pl.BlockSpec(memory_space=pl.ANY)],
        out_specs=pl.BlockSpec(memory_space=pl.ANY),
        scratch_shapes=[pltpu.VMEM((D, F), BF16), pltpu.VMEM((D, F), BF16), pltpu.VMEM((F, D), BF16),
                        pltpu.VMEM((GMM_RING, bm, W), xs.dtype), pltpu.VMEM((GMM_RING, bm, W), xs.dtype),
                        pltpu.SemaphoreType.DMA((GMM_RING,)), pltpu.SemaphoreType.DMA((GMM_RING,))],
    )
    return pl.pallas_call(
        _gmm_kernel,
        grid_spec=grid_spec,
        out_shape=jax.ShapeDtypeStruct((P, W), xs.dtype),
        compiler_params=pltpu.CompilerParams(dimension_semantics=("arbitrary",)),
        name="gmm",
    )(bstart, nb_e, counts, nused, w_gate, w_up, w_down, xs)


def _sc_gather(dest_c, ys):
    nch, K, C = dest_c.shape
    W = ys.shape[1]
    H = C // 2
    info = plsc.get_sparse_core_info()
    nw = info.num_cores * info.num_subcores
    per_w = nch // nw
    nbuf = 3
    mesh = plsc.VectorSubcoreMesh(core_axis_name="c", subcore_axis_name="s")
    items = [(kk, hh) for kk in range(K) for hh in range(2)]

    @functools.partial(
        pl.kernel, mesh=mesh,
        out_type=jax.ShapeDtypeStruct((K, nch * C, W), ys.dtype),
        scratch_types=([pltpu.VMEM((K, C), I32)] + [pltpu.VMEM((H, W), ys.dtype)] * nbuf
                       + [pltpu.SemaphoreType.DMA] * (2 * nbuf)),
        name="sc_gather",
    )
    def k(dest_hbm, ys_hbm, yk_hbm, idx_v, *rest):
        bufs, gsem, wsem = rest[:nbuf], rest[nbuf:2 * nbuf], rest[2 * nbuf:]
        wid = lax.axis_index("s") * info.num_cores + lax.axis_index("c")

        @pl.loop(0, per_w)
        def _(j):
            ch = wid * per_w + j
            pltpu.sync_copy(dest_hbm.at[ch], idx_v)

            def gather(i):
                kk, hh = items[i]
                return pltpu.async_copy(ys_hbm.at[idx_v.at[kk, pl.ds(hh * H, H)]], bufs[i % nbuf], gsem[i % nbuf])

            def write(i):
                kk, hh = items[i]
                return pltpu.async_copy(bufs[i % nbuf], yk_hbm.at[kk, pl.ds(ch * C + hh * H, H)], wsem[i % nbuf])

            n = len(items)
            g = {0: gather(0), 1: gather(1)}
            w = {}
            for i in range(n):
                g[i].wait()
                w[i] = write(i)
                if i + 2 < n:
                    if i >= 1:
                        w.pop(i - 1).wait()
                    g[i + 2] = gather(i + 2)
            for i in sorted(w):
                w[i].wait()

    return k(dest_c, ys)


def _combine_kernel(yk_ref, gt_ref, gate2_ref, xacc_ref, gfin_ref, o_ref):
    gt = gt_ref[...]
    hi_mask = jnp.uint32(0xFFFF0000)
    acc_lo = acc_hi = None
    for k in range(TOP_K):
        p = yk_ref[k]
        g = gt[:, k:k + 1]
        lo = lax.bitcast_convert_type(p << 16, F32) * g
        hi = lax.bitcast_convert_type(p & hi_mask, F32) * g
        acc_lo = lo if k == 0 else acc_lo + lo
        acc_hi = hi if k == 0 else acc_hi + hi
    routed = jnp.concatenate([acc_lo, acc_hi], axis=1)
    x2 = xacc_ref[...] + gate2_ref[0] * routed
    o_ref[...] = _rms(x2) * gfin_ref[...]


def _combine(yk, gates_t, gate2, xacc, g_final, S):
    N, D = xacc.shape
    W = yk.shape[2]
    T = ROUTE_TILE
    spt = S // T
    return pl.pallas_call(
        _combine_kernel,
        grid=(N // T,),
        in_specs=[pl.BlockSpec((TOP_K, T, W), lambda i: (0, i, 0)),
                  pl.BlockSpec((T, TOP_K), lambda i: (i, 0)),
                  pl.BlockSpec((1, 1, D), lambda i: (i // spt, 0, 0)),
                  pl.BlockSpec((T, D), lambda i: (i, 0)),
                  pl.BlockSpec((1, D), lambda i: (0, 0))],
        out_specs=pl.BlockSpec((T, D), lambda i: (i, 0)),
        out_shape=jax.ShapeDtypeStruct((N, D), F32),
        name="combine",
    )(yk, gates_t, gate2, xacc, g_final)


def _layer(xf, B, S, mod, g_mix, w_in, pool_w, pool_scale, w_out, g_ffn, w_router, router_bias,
           w_gate, w_up, w_down, ws_gate, ws_up, ws_down):
    N, D = xf.shape
    E = w_router.shape[1]
    pw = pool_scale.shape[0]
    shift1, scale1, gate1, shift2, scale2, gate2 = [m.reshape(B, 1, D) for m in jnp.split(mod, 6, axis=-1)]
    u, qkv = _in_proj(xf, g_mix.reshape(1, D), shift1, scale1, w_in.astype(BF16), S, pw)
    attn_outs = []
    for a, d in zip(qkv, ATT_DILATIONS):
        o, lse = _attention(a, S // d // ATT_BLOCK)
        shape = (N, ATT_GROUP_WIDTH) if d == 1 else (B * d, S // d, ATT_GROUP_WIDTH)
        attn_outs += [o.reshape(shape), lse.reshape(shape)]
    ng = pool_w.shape[0]
    pool_bd = jnp.einsum('gcd,gh->gchd', pool_w, jnp.eye(ng, dtype=pool_w.dtype)).reshape(pw, pw).astype(BF16)
    xacc, h2 = _mid(u, attn_outs, xf, pool_bd, pool_scale.reshape(1, pw), w_out.astype(BF16), gate1,
                    g_ffn.reshape(1, D), shift2, scale2, gate2,
                    ws_gate.astype(BF16), ws_up.astype(BF16), ws_down.astype(BF16), S)
    eidx, gates, rank, counts = _route(h2, w_router.T.astype(BF16), router_bias.reshape(E, 1).astype(F32))
    bm = GMM_BLOCK
    nblk = N * TOP_K // bm + E
    nb_e = (counts[:, 0] + bm - 1) // bm
    bend = jnp.cumsum(nb_e)
    bstart = (bend - nb_e).astype(I32)
    nused = bend[-1:].astype(I32)
    dest_c = _dest(eidx, rank, (bstart * bm).astype(F32).reshape(E, 1))
    xs = _sc_dispatch(dest_c, h2, nblk * bm)
    ys = _gmm(bstart, nb_e.astype(I32), counts[:, 0], nused, xs, w_gate, w_up, w_down)
    return _sc_gather(dest_c, ys), gates.T, gate2, xacc


def kernel(x, c, w_ada, b_ada, g_mix, w_in, pool_w, pool_scale, w_out, g_ffn, w_router, router_bias,
           w_gate, w_up, w_down, ws_gate, ws_up, ws_down, g_final):
    B, S, D = x.shape
    depth = w_ada.shape[0]
    assert depth == 1, "the final residual is fused with the final norm, so exactly one layer is supported"
    assert S % (ATT_DILATIONS[-1] * ATT_BLOCK) == 0 and S % max(IN_TILE, MID_TILE, ATT_BLOCKS_PER_STEP * ATT_BLOCK) == 0
    xf = x.reshape(B * S, D)
    mod = _ada(c, w_ada[0], b_ada[0])
    yk, gates_t, gate2, xacc = _layer(
        xf, B, S, mod, g_mix[0], w_in[0], pool_w[0], pool_scale[0], w_out[0], g_ffn[0], w_router[0],
        router_bias[0], w_gate[0], w_up[0], w_down[0], ws_gate[0], ws_up[0], ws_down[0])
    out = _combine(yk, gates_t, gate2, xacc, g_final.reshape(1, D), S)
    return out.reshape(B, S, D)
```

```python
import functools

import jax
import jax.numpy as jnp
from jax import lax
from jax.experimental import pallas as pl
from jax.experimental.pallas import tpu as pltpu
from jax.experimental.pallas import tpu_sc as plsc

F32 = jnp.float32
BF16 = jnp.bfloat16
I32 = jnp.int32
U32 = jnp.uint32

LANES = 128
NORM_EPS = 1e-6
POOL_WINDOWS = (2, 4, 8, 16)
POOL_HALO = 16
ATT_DILATIONS = (1, 4, 16)
ATT_BLOCK = 128
ATT_HEADS_PER_GROUP = 4
ATT_HEAD_DIM = 64
ATT_GROUP_WIDTH = ATT_HEADS_PER_GROUP * ATT_HEAD_DIM
N_EXPERT_GROUPS = 8
TOPK_GROUPS = 4
TOP_K = 8
ROUTED_SCALE = 2.5

IN_TILE = 512
ATT_BLOCKS_PER_STEP = 8
MID_TILE = 512
ROUTE_TILE = 256
DEST_TILE = 1024
GMM_BLOCK = 256
GMM_RING = 8
GMM_EXPERTS_PER_STEP = 1
SC_CHUNK = 128

NEG_INF = float("-inf")


def _sigmoid(v):
    return 1.0 / (1.0 + jnp.exp(-v))


def _rms(v):
    return v * lax.rsqrt(jnp.mean(v * v, axis=-1, keepdims=True) + NORM_EPS)


def _pack_bf16_pairs(v):
    n = v.shape[1] // 2
    lo = lax.bitcast_convert_type(v[:, :n].astype(BF16).astype(F32), U32)
    hi = lax.bitcast_convert_type(v[:, n:].astype(BF16).astype(F32), U32)
    return (hi & jnp.uint32(0xFFFF0000)) | (lo >> 16)


def _unpack_bf16_pairs(p):
    lo = lax.bitcast_convert_type(p << 16, F32).astype(BF16)
    hi = lax.bitcast_convert_type(p & jnp.uint32(0xFFFF0000), F32).astype(BF16)
    return lo, hi


def _ada_kernel(c_ref, w_ref, b_ref, o_ref):
    c = c_ref[...]
    cs = c * _sigmoid(c)
    o_ref[...] = jnp.dot(cs, w_ref[...], preferred_element_type=F32,
                         precision=lax.Precision.HIGHEST) + b_ref[...]


def _ada(c, w_ada, b_ada):
    B, D = c.shape
    W = w_ada.shape[1]
    tn = 1024
    return pl.pallas_call(
        _ada_kernel,
        grid=(W // tn,),
        in_specs=[pl.BlockSpec((B, D), lambda j: (0, 0)),
                  pl.BlockSpec((D, tn), lambda j: (0, j)),
                  pl.BlockSpec((1, tn), lambda j: (0, j))],
        out_specs=pl.BlockSpec((B, tn), lambda j: (0, j)),
        out_shape=jax.ShapeDtypeStruct((B, W), F32),
        name="ada",
    )(c, w_ada, b_ada.reshape(1, W))


def _in_kernel(x_ref, g_ref, sh_ref, sc_ref, w_ref, pool_ref, q0_ref, q1_ref, q2_ref, scr_ref):
    h = _rms(x_ref[...]) * g_ref[...]
    h = h * (1.0 + sc_ref[0]) + sh_ref[0]
    hb = h.astype(BF16)
    tm = x_ref.shape[0]
    pw = pool_ref.shape[1]
    gw = ATT_GROUP_WIDTH
    pool_ref[...] = jnp.dot(hb, w_ref[:, 0:pw], preferred_element_type=F32)
    for g, (out, d) in enumerate(zip((q0_ref, q1_ref, q2_ref), ATT_DILATIONS)):
        for sec in range(3):
            c0 = pw + sec * 3 * gw + g * gw
            res = jnp.dot(hb, w_ref[:, c0:c0 + gw], preferred_element_type=F32)
            if d == 1:
                out[:, sec * gw:(sec + 1) * gw] = res.astype(BF16)
            else:
                for c in range(gw // LANES):
                    scr_ref[c] = res[:, c * LANES:(c + 1) * LANES]
                for r in range(d):
                    for c in range(gw // LANES):
                        c1 = sec * gw + c * LANES
                        out[r, :, c1:c1 + LANES] = scr_ref[c, pl.ds(r, tm // d, stride=d), :].astype(BF16)


def _in_proj(xf, g_mix, shift1, scale1, w_in_b, S, pool_width):
    N, D = xf.shape
    tm = IN_TILE
    spt = S // tm
    vec = lambda i: (i // spt, 0, 0)
    row = lambda i: (i, 0)
    gw3 = 3 * ATT_GROUP_WIDTH
    B = N // S
    res_spec = lambda d: pl.BlockSpec((d, tm // d, gw3), lambda i: (i // spt, i % spt, 0))
    res_shape = lambda d: jax.ShapeDtypeStruct((B * d, S // d, gw3), BF16)
    outs = pl.pallas_call(
        _in_kernel,
        grid=(N // tm,),
        in_specs=[pl.BlockSpec((tm, D), row),
                  pl.BlockSpec((1, D), lambda i: (0, 0)),
                  pl.BlockSpec((1, 1, D), vec),
                  pl.BlockSpec((1, 1, D), vec),
                  pl.BlockSpec(w_in_b.shape, lambda i: (0, 0))],
        out_specs=[pl.BlockSpec((tm, pool_width), row), pl.BlockSpec((tm, gw3), row)]
                  + [res_spec(d) for d in ATT_DILATIONS[1:]],
        out_shape=[jax.ShapeDtypeStruct((N, pool_width), F32), jax.ShapeDtypeStruct((N, gw3), BF16)]
                  + [res_shape(d) for d in ATT_DILATIONS[1:]],
        scratch_shapes=[pltpu.VMEM((ATT_GROUP_WIDTH // LANES, tm, LANES), F32)],
        name="in_proj",
    )(xf, g_mix, shift1, scale1, w_in_b)
    return outs[0], [o.reshape(N, gw3) for o in outs[1:]]


def _attn_kernel(nbs, a_ref, halo_ref, o_ref, lse_ref, kv_ref, band_ref):
    i = pl.program_id(0)
    R = a_ref.shape[0] // ATT_BLOCK
    gw = ATT_GROUP_WIDTH
    blk = ATT_BLOCK
    nh = ATT_HEADS_PER_GROUP
    kv_ref[0:blk, :] = halo_ref[:, gw:3 * gw]
    kv_ref[blk:, :] = a_ref[:, gw:3 * gw]
    row = lax.broadcasted_iota(I32, (nh * blk, 2 * blk), 0) % blk
    col = lax.broadcasted_iota(I32, (nh * blk, 2 * blk), 1)
    band_ref[...] = jnp.where((col >= row) & (col <= row + blk), 0.0, NEG_INF)
    head_of_lane = lax.broadcasted_iota(I32, (blk, gw), 1) // ATT_HEAD_DIM
    nt = (((1,), (1,)), ((), ()))

    def body(jj, carry):
        r0 = pl.multiple_of(jj * blk, blk)
        qf = a_ref[pl.ds(r0, blk), 0:gw].astype(F32) * (ATT_HEAD_DIM ** -0.5)
        q4 = jnp.concatenate([jnp.where(head_of_lane == h, qf, 0.0) for h in range(nh)], axis=0).astype(BF16)
        kc = kv_ref[pl.ds(r0, 2 * blk), 0:gw]
        vc = kv_ref[pl.ds(r0, 2 * blk), gw:2 * gw]
        s = lax.dot_general(q4, kc, nt, preferred_element_type=F32) + band_ref[...]
        first = ((i * R + jj) % nbs) == 0
        s = jnp.where(col >= jnp.where(first, blk, 0), s, NEG_INF)
        m = jnp.max(s, axis=1, keepdims=True)
        p = jnp.exp(s - m)
        l = jnp.sum(p, axis=1, keepdims=True)
        o4 = jnp.dot(p.astype(BF16), vc, preferred_element_type=F32) / l
        lse4 = m + jnp.log(l)
        o = jnp.zeros((blk, gw), F32)
        lse = jnp.zeros((blk, gw), F32)
        for h in range(nh):
            hm = head_of_lane == h
            o = jnp.where(hm, o4[h * blk:(h + 1) * blk, :], o)
            lse = jnp.where(hm, lse4[h * blk:(h + 1) * blk, :], lse)
        o_ref[pl.ds(r0, blk), :] = o
        lse_ref[pl.ds(r0, blk), :] = lse
        return carry

    lax.fori_loop(0, R, body, 0, unroll=8)


def _attention(a, nbs):
    N = a.shape[0]
    R = ATT_BLOCKS_PER_STEP
    gw = ATT_GROUP_WIDTH
    tm = R * ATT_BLOCK
    return pl.pallas_call(
        functools.partial(_attn_kernel, nbs),
        grid=(N // tm,),
        in_specs=[pl.BlockSpec((tm, 3 * gw), lambda i: (i, 0)),
                  pl.BlockSpec((ATT_BLOCK, 3 * gw), lambda i: (jnp.maximum(i * R - 1, 0), 0))],
        out_specs=[pl.BlockSpec((tm, gw), lambda i: (i, 0))] * 2,
        out_shape=[jax.ShapeDtypeStruct((N, gw), F32)] * 2,
        scratch_shapes=[pltpu.VMEM((tm + ATT_BLOCK, 2 * gw), BF16),
                        pltpu.VMEM((ATT_HEADS_PER_GROUP * ATT_BLOCK, 2 * ATT_BLOCK), F32)],
        name="attn",
    )(a, a)


def _mid_kernel(spt, u_ref, uh_ref, o0_ref, l0_ref, o1_ref, l1_ref, o2_ref, l2_ref, x_ref,
                pbd_ref, psc_ref, wout_ref, gate1_ref, gffn_ref, sh2_ref, sc2_ref, gate2_ref,
                wsg_ref, wsu_ref, wsd_ref, xacc_ref, h2_ref, ext_ref, til_ref):
    i = pl.program_id(0)
    tm, pw = u_ref.shape
    si = i % spt
    u = u_ref[...]
    keep = jnp.full((POOL_HALO, pw), si, I32) > 0
    ext_ref[0:POOL_HALO, :] = jnp.where(keep, uh_ref[...], 0.0)
    ext_ref[POOL_HALO:, :] = u
    lane_grp = lax.broadcasted_iota(I32, (tm, pw), 1) // (pw // len(POOL_WINDOWS))
    s = u
    pooled = jnp.zeros((tm, pw), F32)
    for j in range(1, POOL_HALO):
        s = s + ext_ref[pl.ds(POOL_HALO - j, tm), :]
        if (j + 1) in POOL_WINDOWS:
            pooled = jnp.where(lane_grp == POOL_WINDOWS.index(j + 1), s, pooled)
    win = jnp.zeros((tm, pw), I32)
    for g, w in enumerate(POOL_WINDOWS):
        win = jnp.where(lane_grp == g, w, win)
    pos = si * tm + lax.broadcasted_iota(I32, (tm, pw), 0)
    cnt = jnp.minimum(pos + 1, win).astype(F32)
    pooled = pooled / cnt - u
    pool_out = jnp.dot(pooled.astype(BF16), pbd_ref[...], preferred_element_type=F32) * psc_ref[...]
    def token_order(slot, ref):
        d, n, w = ref.shape
        for r in range(d):
            for c in range(w // LANES):
                til_ref[slot, c, pl.ds(r, n, stride=d), :] = ref[r, :, c * LANES:(c + 1) * LANES]
        return jnp.concatenate([til_ref[slot, c] for c in range(w // LANES)], axis=1)

    l0 = l0_ref[...]
    l1 = token_order(0, l1_ref)
    l2 = token_order(1, l2_ref)
    m = jnp.maximum(jnp.maximum(l0, l1), l2)
    w0 = jnp.exp(l0 - m)
    w1 = jnp.exp(l1 - m)
    w2 = jnp.exp(l2 - m)
    attn = (w0 * o0_ref[...] + w1 * token_order(2, o1_ref) + w2 * token_order(3, o2_ref)) / (w0 + w1 + w2)
    mixed = (jnp.dot(pool_out.astype(BF16), wout_ref[0:pw, :], preferred_element_type=F32)
             + jnp.dot(attn.astype(BF16), wout_ref[pw:, :], preferred_element_type=F32))
    x1 = x_ref[...] + gate1_ref[0] * mixed
    h2 = _rms(x1) * gffn_ref[...]
    h2 = h2 * (1.0 + sc2_ref[0]) + sh2_ref[0]
    h2_ref[...] = _pack_bf16_pairs(h2)
    hb = h2.astype(BF16)
    a = jnp.dot(hb, wsg_ref[...], preferred_element_type=F32)
    b = jnp.dot(hb, wsu_ref[...], preferred_element_type=F32)
    act = (a * _sigmoid(a)) * b
    shared = jnp.dot(act.astype(BF16), wsd_ref[...], preferred_element_type=F32)
    xacc_ref[...] = x1 + gate2_ref[0] * shared


def _mid(u, attn_outs, xf, pool_bd, pool_scale, w_out_b, gate1, g_ffn, shift2, scale2, gate2,
         wsg_b, wsu_b, wsd_b, S):
    N, D = xf.shape
    pw = u.shape[1]
    tm = MID_TILE
    spt = S // tm
    row = lambda i: (i, 0)
    vec = lambda i: (i // spt, 0, 0)
    full = lambda a: pl.BlockSpec(a.shape, lambda i: (0,) * a.ndim)
    hpt = tm // POOL_HALO
    in_specs = [pl.BlockSpec((tm, pw), row),
                pl.BlockSpec((POOL_HALO, pw), lambda i: (jnp.maximum(i * hpt - 1, 0), 0))]
    gw = ATT_GROUP_WIDTH
    in_specs += [pl.BlockSpec((tm, gw), row)] * 2
    for d in ATT_DILATIONS[1:]:
        in_specs += [pl.BlockSpec((d, tm // d, gw), lambda i: (i // spt, i % spt, 0))] * 2
    in_specs += [pl.BlockSpec((tm, D), row), full(pool_bd), full(pool_scale), full(w_out_b),
                 pl.BlockSpec((1, 1, D), vec), full(g_ffn), pl.BlockSpec((1, 1, D), vec),
                 pl.BlockSpec((1, 1, D), vec), pl.BlockSpec((1, 1, D), vec),
                 full(wsg_b), full(wsu_b), full(wsd_b)]
    return pl.pallas_call(
        functools.partial(_mid_kernel, spt),
        grid=(N // tm,),
        in_specs=in_specs,
        out_specs=[pl.BlockSpec((tm, D), row), pl.BlockSpec((tm, D // 2), row)],
        out_shape=[jax.ShapeDtypeStruct((N, D), F32), jax.ShapeDtypeStruct((N, D // 2), U32)],
        scratch_shapes=[pltpu.VMEM((tm + POOL_HALO, pw), F32), pltpu.VMEM((4, gw // LANES, tm, LANES), F32)],
        name="mid",
    )(u, u, *attn_outs, xf, pool_bd, pool_scale, w_out_b, gate1, g_ffn, shift2, scale2, gate2,
      wsg_b, wsu_b, wsd_b)


def _route_kernel(h_ref, wr_ref, bias_ref, e_ref, g_ref, r_ref, cnt_ref, base_ref):
    i = pl.program_id(0)
    T = h_ref.shape[0]
    E = wr_ref.shape[0]
    gsz = E // N_EXPERT_GROUPS

    @pl.when(i == 0)
    def _():
        base_ref[...] = jnp.zeros_like(base_ref)

    h_lo, h_hi = _unpack_bf16_pairs(h_ref[...])
    half = h_lo.shape[1]
    nt = (((1,), (1,)), ((), ()))
    logits = (lax.dot_general(wr_ref[:, 0:half], h_lo, nt, preferred_element_type=F32)
              + lax.dot_general(wr_ref[:, half:], h_hi, nt, preferred_element_type=F32))
    scores = _sigmoid(logits)
    biased = scores + bias_ref[...]
    giota = lax.broadcasted_iota(I32, (gsz, T), 0)
    gscore = []
    for g in range(N_EXPERT_GROUPS):
        blk = biased[g * gsz:(g + 1) * gsz, :]
        m1 = jnp.max(blk, axis=0, keepdims=True)
        i1 = jnp.min(jnp.where(blk == m1, giota, gsz), axis=0, keepdims=True)
        m2 = jnp.max(jnp.where(giota == i1, NEG_INF, blk), axis=0, keepdims=True)
        gscore.append(m1 + m2)
    parts = []
    for g in range(N_EXPERT_GROUPS):
        beaten = jnp.zeros((1, T), I32)
        for o in range(N_EXPERT_GROUPS):
            if o == g:
                continue
            wins = (gscore[o] >= gscore[g]) if o < g else (gscore[o] > gscore[g])
            beaten = beaten + wins.astype(I32)
        keep = jnp.broadcast_to(beaten, (gsz, T)) < TOPK_GROUPS
        parts.append(jnp.where(keep, biased[g * gsz:(g + 1) * gsz, :], NEG_INF))
    cur = jnp.concatenate(parts, axis=0)
    eiota = lax.broadcasted_iota(I32, (E, T), 0)
    selm = jnp.zeros((E, T), F32)
    idxs, gates = [], []
    for k in range(TOP_K):
        m = jnp.max(cur, axis=0, keepdims=True)
        idx = jnp.min(jnp.where(cur == m, eiota, E), axis=0, keepdims=True)
        oh = eiota == idx
        gates.append(jnp.sum(jnp.where(oh, scores, 0.0), axis=0, keepdims=True))
        idxs.append(idx)
        cur = jnp.where(oh, NEG_INF, cur)
        selm = jnp.where(oh, 1.0, selm)
    gsum = gates[0]
    for k in range(1, TOP_K):
        gsum = gsum + gates[k]
    before = (lax.broadcasted_iota(I32, (T, T), 0) < lax.broadcasted_iota(I32, (T, T), 1)).astype(BF16)
    tot = jnp.dot(selm.astype(BF16), before, preferred_element_type=F32) + base_ref[...]
    for k in range(TOP_K):
        e_ref[k:k + 1, :] = idxs[k]
        g_ref[k:k + 1, :] = gates[k] / gsum * ROUTED_SCALE
        r_ref[k:k + 1, :] = jnp.sum(jnp.where(eiota == idxs[k], tot, 0.0), axis=0, keepdims=True).astype(I32)
    base_ref[...] = base_ref[...] + jnp.sum(selm, axis=1, keepdims=True)
    cnt_ref[...] = base_ref[...].astype(I32)


def _route(h2p, wr_t, bias_col):
    N = h2p.shape[0]
    E, D = wr_t.shape
    T = ROUTE_TILE
    col = lambda i: (0, i)
    return pl.pallas_call(
        _route_kernel,
        grid=(N // T,),
        in_specs=[pl.BlockSpec((T, D // 2), lambda i: (i, 0)),
                  pl.BlockSpec((E, D), lambda i: (0, 0)),
                  pl.BlockSpec((E, 1), lambda i: (0, 0))],
        out_specs=[pl.BlockSpec((TOP_K, T), col), pl.BlockSpec((TOP_K, T), col),
                   pl.BlockSpec((TOP_K, T), col), pl.BlockSpec((E, 1), lambda i: (0, 0))],
        out_shape=[jax.ShapeDtypeStruct((TOP_K, N), I32), jax.ShapeDtypeStruct((TOP_K, N), F32),
                   jax.ShapeDtypeStruct((TOP_K, N), I32), jax.ShapeDtypeStruct((E, 1), I32)],
        scratch_shapes=[pltpu.VMEM((E, 1), F32)],
        compiler_params=pltpu.CompilerParams(dimension_semantics=("arbitrary",)),
        name="route",
    )(h2p, wr_t, bias_col)


def _dest_kernel(e_ref, r_ref, off_ref, dc_ref):
    E = off_ref.shape[0]
    T = e_ref.shape[1]
    C = dc_ref.shape[2]
    eiota = lax.broadcasted_iota(I32, (E, T), 0)
    off = off_ref[...]
    for k in range(TOP_K):
        start = jnp.sum(jnp.where(eiota == e_ref[k:k + 1, :], off, 0.0), axis=0, keepdims=True)
        d = start.astype(I32) + r_ref[k:k + 1, :]
        for c in range(T // C):
            dc_ref[c, k:k + 1, :] = d[:, c * C:(c + 1) * C]


def _dest(eidx, rank, offs_col):
    N = eidx.shape[1]
    E = offs_col.shape[0]
    T = DEST_TILE
    C = SC_CHUNK
    col = lambda i: (0, i)
    return pl.pallas_call(
        _dest_kernel,
        grid=(N // T,),
        in_specs=[pl.BlockSpec((TOP_K, T), col), pl.BlockSpec((TOP_K, T), col),
                  pl.BlockSpec((E, 1), lambda i: (0, 0))],
        out_specs=pl.BlockSpec((T // C, TOP_K, C), lambda i: (i, 0, 0)),
        out_shape=jax.ShapeDtypeStruct((N // C, TOP_K, C), I32),
        name="dest",
    )(eidx, rank, offs_col)


def _sc_dispatch(dest_c, h2p, P):
    N, W = h2p.shape
    C = dest_c.shape[2]
    info = plsc.get_sparse_core_info()
    nw = info.num_cores * info.num_subcores
    per_w = N // C // nw
    mesh = plsc.VectorSubcoreMesh(core_axis_name="c", subcore_axis_name="s")

    @functools.partial(
        pl.kernel, mesh=mesh,
        out_type=jax.ShapeDtypeStruct((P, W), h2p.dtype),
        scratch_types=[pltpu.VMEM((TOP_K, C), I32), pltpu.VMEM((C, W), h2p.dtype), pltpu.SemaphoreType.DMA],
        name="sc_dispatch",
    )
    def k(dest_hbm, h_hbm, xs_hbm, idx_v, rows_v, sem):
        wid = lax.axis_index("s") * info.num_cores + lax.axis_index("c")

        @pl.loop(0, per_w)
        def _(j):
            ch = wid * per_w + j
            pltpu.sync_copy(dest_hbm.at[ch], idx_v)
            pltpu.sync_copy(h_hbm.at[pl.ds(ch * C, C)], rows_v)
            copies = [pltpu.async_copy(rows_v, xs_hbm.at[idx_v.at[kk]], sem) for kk in range(TOP_K)]
            for cp in copies:
                cp.wait()

    return k(dest_c, h2p)


def _gmm_kernel(bstart_ref, nbe_ref, cnt_ref, nu_ref, wg_ref, wu_ref, wd_ref, xs_hbm, ys_hbm,
                wgb, wub, wdb, xbuf, ybuf, xsem, ysem):
    step = pl.program_id(0)
    last = pl.num_programs(0) - 1
    epg = wg_ref.shape[0]
    ring, bm = xbuf.shape[0], xbuf.shape[1]
    nblk = ys_hbm.shape[0] // bm
    nused = nu_ref[0]

    def x_copy(b, slot):
        return pltpu.make_async_copy(xs_hbm.at[pl.ds(pl.multiple_of(b * bm, bm), bm), :], xbuf.at[slot],
                                     xsem.at[slot])

    def y_copy(b, slot):
        return pltpu.make_async_copy(ybuf.at[slot], ys_hbm.at[pl.ds(pl.multiple_of(b * bm, bm), bm), :],
                                     ysem.at[slot])

    @pl.when(step == 0)
    def _():
        for j in range(ring - 1):
            @pl.when(j < nused)
            def _():
                x_copy(j, j).start()

    def run_expert(ee):
        e = step * epg + ee
        b0 = bstart_ref[e]
        nb = nbe_ref[e]

        @pl.when(nb > 0)
        def _():
            wgb[...] = wg_ref[ee].astype(BF16)
            wub[...] = wu_ref[ee].astype(BF16)
            wdb[...] = wd_ref[ee].astype(BF16)

            def prefetch(t):
                @pl.when(t < nused)
                def _():
                    x_copy(t, jnp.bitwise_and(t, ring - 1)).start()

            def process(b, n):
                slots = [jnp.bitwise_and(b + j, ring - 1) for j in range(n)]
                for j in range(n):
                    x_copy(b + j, slots[j]).wait()
                prefetch(b + ring - 1)
                for j in range(n):
                    @pl.when(b + j >= ring)
                    def _():
                        y_copy(b + j - ring, slots[j]).wait()
                rows = lax.broadcasted_iota(I32, (n * bm, 1), 0)
                valid = cnt_ref[e] - (b - b0) * bm
                xp = jnp.concatenate([xbuf[s] for s in slots], axis=0)
                xb = jnp.concatenate(_unpack_bf16_pairs(jnp.where(rows < valid, xp, jnp.uint32(0))), axis=1)
                a = jnp.dot(xb, wgb[...], preferred_element_type=F32)
                g = jnp.dot(xb, wub[...], preferred_element_type=F32)
                act = (a * _sigmoid(a)) * g
                yp = _pack_bf16_pairs(jnp.dot(act.astype(BF16), wdb[...], preferred_element_type=F32))
                for j in range(n):
                    ybuf[slots[j]] = yp[j * bm:(j + 1) * bm, :]
                    y_copy(b + j, slots[j]).start()
                for j in range(1, n):
                    prefetch(b + ring - 1 + j)

            quads = lax.shift_right_logical(nb, 2)
            lax.fori_loop(0, quads, lambda j, c: (process(b0 + 4 * j, 4), c)[1], 0)
            rest2 = jnp.bitwise_and(nb, 2)

            @pl.when(rest2 != 0)
            def _():
                process(b0 + 4 * quads, 2)

            @pl.when(jnp.bitwise_and(nb, 1) != 0)
            def _():
                process(b0 + 4 * quads + rest2, 1)

    for ee in range(epg):
        run_expert(ee)

    @pl.when(step == last)
    def _():
        for back in range(ring, 0, -1):
            @pl.when(nused >= back)
            def _():
                y_copy(nused - back, jnp.bitwise_and(nused - back, ring - 1)).wait()
        ybuf[0] = jnp.zeros(ybuf.shape[1:], ybuf.dtype)
        lax.fori_loop(nused, nblk, lambda b, c: (y_copy(b, 0).start(), c)[1], 0)
        lax.fori_loop(nused, nblk, lambda b, c: (y_copy(b, 0).wait(), c)[1], 0)


def _gmm(bstart, nb_e, counts, nused, xs, w_gate, w_up, w_down):
    P, W = xs.shape
    E, D, F = w_gate.shape
    bm = GMM_BLOCK
    epg = GMM_EXPERTS_PER_STEP
    wsel = lambda s, *_: (s, 0, 0)
    grid_spec = pltpu.PrefetchScalarGridSpec(
        num_scalar_prefetch=4,
        grid=(E // epg,),
        in_specs=[pl.BlockSpec((epg, D, F), wsel), pl.BlockSpec((epg, D, F), wsel), pl.BlockSpec((epg, F, D), wsel),
                  pl.BlockSpec(memory_space=pl.ANY)],
        out_specs=pl.BlockSpec(memory_space=pl.ANY),
        scratch_shapes=[pltpu.VMEM((D, F), BF16), pltpu.VMEM((D, F), BF16), pltpu.VMEM((F, D), BF16),
                        pltpu.VMEM((GMM_RING, bm, W), xs.dtype), pltpu.VMEM((GMM_RING, bm, W), xs.dtype),
                        pltpu.SemaphoreType.DMA((GMM_RING,)), pltpu.SemaphoreType.DMA((GMM_RING,))],
    )
    return pl.pallas_call(
        _gmm_kernel,
        grid_spec=grid_spec,
        out_shape=jax.ShapeDtypeStruct((P, W), xs.dtype),
        compiler_params=pltpu.CompilerParams(dimension_semantics=("arbitrary",)),
        name="gmm",
    )(bstart, nb_e, counts, nused, w_gate, w_up, w_down, xs)


def _sc_gather(dest_c, ys):
    nch, K, C = dest_c.shape
    W = ys.shape[1]
    H = C // 2
    info = plsc.get_sparse_core_info()
    nw = info.num_cores * info.num_subcores
    per_w = nch // nw
    nbuf = 3
    mesh = plsc.VectorSubcoreMesh(core_axis_name="c", subcore_axis_name="s")
    items = [(kk, hh) for kk in range(K) for hh in range(2)]

    @functools.partial(
        pl.kernel, mesh=mesh,
        out_type=jax.ShapeDtypeStruct((K, nch * C, W), ys.dtype),
        scratch_types=([pltpu.VMEM((K, C), I32)] + [pltpu.VMEM((H, W), ys.dtype)] * nbuf
                       + [pltpu.SemaphoreType.DMA] * (2 * nbuf)),
        name="sc_gather",
    )
    def k(dest_hbm, ys_hbm, yk_hbm, idx_v, *rest):
        bufs, gsem, wsem = rest[:nbuf], rest[nbuf:2 * nbuf], rest[2 * nbuf:]
        wid = lax.axis_index("s") * info.num_cores + lax.axis_index("c")

        @pl.loop(0, per_w)
        def _(j):
            ch = wid * per_w + j
            pltpu.sync_copy(dest_hbm.at[ch], idx_v)

            def gather(i):
                kk, hh = items[i]
                return pltpu.async_copy(ys_hbm.at[idx_v.at[kk, pl.ds(hh * H, H)]], bufs[i % nbuf], gsem[i % nbuf])

            def write(i):
                kk, hh = items[i]
                return pltpu.async_copy(bufs[i % nbuf], yk_hbm.at[kk, pl.ds(ch * C + hh * H, H)], wsem[i % nbuf])

            n = len(items)
            g = {0: gather(0), 1: gather(1)}
            w = {}
            for i in range(n):
                g[i].wait()
                w[i] = write(i)
                if i + 2 < n:
                    if i >= 1:
                        w.pop(i - 1).wait()
                    g[i + 2] = gather(i + 2)
            for i in sorted(w):
                w[i].wait()

    return k(dest_c, ys)


def _combine_kernel(yk_ref, gt_ref, gate2_ref, xacc_ref, gfin_ref, o_ref):
    gt = gt_ref[...]
    hi_mask = jnp.uint32(0xFFFF0000)
    acc_lo = acc_hi = None
    for k in range(TOP_K):
        p = yk_ref[k]
        g = gt[:, k:k + 1]
        lo = lax.bitcast_convert_type(p << 16, F32) * g
        hi = lax.bitcast_convert_type(p & hi_mask, F32) * g
        acc_lo = lo if k == 0 else acc_lo + lo
        acc_hi = hi if k == 0 else acc_hi + hi
    routed = jnp.concatenate([acc_lo, acc_hi], axis=1)
    x2 = xacc_ref[...] + gate2_ref[0] * routed
    o_ref[...] = _rms(x2) * gfin_ref[...]


def _combine(yk, gates_t, gate2, xacc, g_final, S):
    N, D = xacc.shape
    W = yk.shape[2]
    T = ROUTE_TILE
    spt = S // T
    return pl.pallas_call(
        _combine_kernel,
        grid=(N // T,),
        in_specs=[pl.BlockSpec((TOP_K, T, W), lambda i: (0, i, 0)),
                  pl.BlockSpec((T, TOP_K), lambda i: (i, 0)),
                  pl.BlockSpec((1, 1, D), lambda i: (i // spt, 0, 0)),
                  pl.BlockSpec((T, D), lambda i: (i, 0)),
                  pl.BlockSpec((1, D), lambda i: (0, 0))],
        out_specs=pl.BlockSpec((T, D), lambda i: (i, 0)),
        out_shape=jax.ShapeDtypeStruct((N, D), F32),
        name="combine",
    )(yk, gates_t, gate2, xacc, g_final)


def _layer(xf, B, S, mod, g_mix, w_in, pool_w, pool_scale, w_out, g_ffn, w_router, router_bias,
           w_gate, w_up, w_down, ws_gate, ws_up, ws_down):
    N, D = xf.shape
    E = w_router.shape[1]
    pw = pool_scale.shape[0]
    shift1, scale1, gate1, shift2, scale2, gate2 = [m.reshape(B, 1, D) for m in jnp.split(mod, 6, axis=-1)]
    u, qkv = _in_proj(xf, g_mix.reshape(1, D), shift1, scale1, w_in.astype(BF16), S, pw)
    attn_outs = []
    for a, d in zip(qkv, ATT_DILATIONS):
        o, lse = _attention(a, S // d // ATT_BLOCK)
        shape = (N, ATT_GROUP_WIDTH) if d == 1 else (B * d, S // d, ATT_GROUP_WIDTH)
        attn_outs += [o.reshape(shape), lse.reshape(shape)]
    ng = pool_w.shape[0]
    pool_bd = jnp.einsum('gcd,gh->gchd', pool_w, jnp.eye(ng, dtype=pool_w.dtype)).reshape(pw, pw).astype(BF16)
    xacc, h2 = _mid(u, attn_outs, xf, pool_bd, pool_scale.reshape(1, pw), w_out.astype(BF16), gate1,
                    g_ffn.reshape(1, D), shift2, scale2, gate2,
                    ws_gate.astype(BF16), ws_up.astype(BF16), ws_down.astype(BF16), S)
    eidx, gates, rank, counts = _route(h2, w_router.T.astype(BF16), router_bias.reshape(E, 1).astype(F32))
    bm = GMM_BLOCK
    nblk = N * TOP_K // bm + E
    nb_e = (counts[:, 0] + bm - 1) // bm
    bend = jnp.cumsum(nb_e)
    bstart = (bend - nb_e).astype(I32)
    nused = bend[-1:].astype(I32)
    dest_c = _dest(eidx, rank, (bstart * bm).astype(F32).reshape(E, 1))
    xs = _sc_dispatch(dest_c, h2, nblk * bm)
    ys = _gmm(bstart, nb_e.astype(I32), counts[:, 0], nused, xs, w_gate, w_up, w_down)
    return _sc_gather(dest_c, ys), gates.T, gate2, xacc


def kernel(x, c, w_ada, b_ada, g_mix, w_in, pool_w, pool_scale, w_out, g_ffn, w_router, router_bias,
           w_gate, w_up, w_down, ws_gate, ws_up, ws_down, g_final):
    B, S, D = x.shape
    depth = w_ada.shape[0]
    assert depth == 1, "the final residual is fused with the final norm, so exactly one layer is supported"
    assert S % (ATT_DILATIONS[-1] * ATT_BLOCK) == 0 and S % max(IN_TILE, MID_TILE, ATT_BLOCKS_PER_STEP * ATT_BLOCK) == 0
    xf = x.reshape(B * S, D)
    mod = _ada(c, w_ada[0], b_ada[0])
    yk, gates_t, gate2, xacc = _layer(
        xf, B, S, mod, g_mix[0], w_in[0], pool_w[0], pool_scale[0], w_out[0], g_ffn[0], w_router[0],
        router_bias[0], w_gate[0], w_up[0], w_down[0], ws_gate[0], ws_up[0], ws_down[0])
    out = _combine(yk, gates_t, gate2, xacc, g_final.reshape(1, D), S)
    return out.reshape(B, S, D)
```

```python
import functools

import jax
import jax.numpy as jnp
from jax import lax
from jax.experimental import pallas as pl
from jax.experimental.pallas import tpu as pltpu
from jax.experimental.pallas import tpu_sc as plsc

F32 = jnp.float32
BF16 = jnp.bfloat16
I32 = jnp.int32
U32 = jnp.uint32

LANES = 128
SINGLE_LOAD_STRIDE = 4
NORM_EPS = 1e-6
POOL_WINDOWS = (2, 4, 8, 16)
POOL_HALO = 16
ATT_DILATIONS = (1, 4, 16)
ATT_BLOCK = 128
ATT_HEADS_PER_GROUP = 4
ATT_HEAD_DIM = 64
ATT_GROUP_WIDTH = ATT_HEADS_PER_GROUP * ATT_HEAD_DIM
N_EXPERT_GROUPS = 8
TOPK_GROUPS = 4
TOP_K = 8
ROUTED_SCALE = 2.5

IN_TILE = 512
ATT_BLOCKS_PER_STEP = 16
MID_TILE = 512
ROUTE_TILE = 256
DEST_TILE = 1024
GMM_BLOCK = 256
GMM_RING = 8
GMM_EXPERTS_PER_STEP = 1
SC_CHUNK = 128

NEG_INF = float("-inf")


def _sigmoid(v):
    return 1.0 / (1.0 + jnp.exp(-v))


def _rms(v):
    return v * lax.rsqrt(jnp.mean(v * v, axis=-1, keepdims=True) + NORM_EPS)


def _pack_bf16_pairs(v):
    n = v.shape[1] // 2
    lo = lax.bitcast_convert_type(v[:, :n].astype(BF16).astype(F32), U32)
    hi = lax.bitcast_convert_type(v[:, n:].astype(BF16).astype(F32), U32)
    return (hi & jnp.uint32(0xFFFF0000)) | (lo >> 16)


def _unpack_bf16_pairs(p):
    lo = lax.bitcast_convert_type(p << 16, F32).astype(BF16)
    hi = lax.bitcast_convert_type(p & jnp.uint32(0xFFFF0000), F32).astype(BF16)
    return lo, hi


def _ada_kernel(c_ref, w_ref, b_ref, o_ref):
    c = c_ref[...]
    cs = c * _sigmoid(c)
    o_ref[...] = jnp.dot(cs, w_ref[...], preferred_element_type=F32,
                         precision=lax.Precision.HIGHEST) + b_ref[...]


def _ada(c, w_ada, b_ada):
    B, D = c.shape
    W = w_ada.shape[1]
    tn = 1024
    return pl.pallas_call(
        _ada_kernel,
        grid=(W // tn,),
        in_specs=[pl.BlockSpec((B, D), lambda j: (0, 0)),
                  pl.BlockSpec((D, tn), lambda j: (0, j)),
                  pl.BlockSpec((1, tn), lambda j: (0, j))],
        out_specs=pl.BlockSpec((B, tn), lambda j: (0, j)),
        out_shape=jax.ShapeDtypeStruct((B, W), F32),
        name="ada",
    )(c, w_ada, b_ada.reshape(1, W))


def _in_kernel(x_ref, g_ref, sh_ref, sc_ref, w_ref, pool_ref, q0_ref, q1_ref, q2_ref, scr_ref, tmp_ref):
    h = _rms(x_ref[...]) * g_ref[...]
    h = h * (1.0 + sc_ref[0]) + sh_ref[0]
    hb = h.astype(BF16)
    tm = x_ref.shape[0]
    pw = pool_ref.shape[1]
    gw = ATT_GROUP_WIDTH
    pool_ref[...] = jnp.dot(hb, w_ref[:, 0:pw], preferred_element_type=F32)
    for g, (out, d) in enumerate(zip((q0_ref, q1_ref, q2_ref), ATT_DILATIONS)):
        for sec in range(3):
            c0 = pw + sec * 3 * gw + g * gw
            res = jnp.dot(hb, w_ref[:, c0:c0 + gw], preferred_element_type=F32)
            if d == 1:
                out[:, sec * gw:(sec + 1) * gw] = res.astype(BF16)
            else:
                for c in range(gw // LANES):
                    scr_ref[c] = res[:, c * LANES:(c + 1) * LANES]
                    src, f1 = scr_ref.at[c], 1
                    if d > SINGLE_LOAD_STRIDE:
                        f1 = SINGLE_LOAD_STRIDE
                        for q in range(f1):
                            tmp_ref[c, q * (tm // f1):(q + 1) * (tm // f1), :] = scr_ref[c, pl.ds(q, tm // f1, stride=f1), :]
                        src = tmp_ref.at[c]
                    for r in range(d):
                        r_lo, r_hi = r % f1, r // f1
                        c1 = sec * gw + c * LANES
                        out[r, :, c1:c1 + LANES] = src[pl.ds(r_lo * (tm // f1) + r_hi, tm // d, stride=d // f1),
                                                       :].astype(BF16)


def _in_proj(xf, g_mix, shift1, scale1, w_in_b, S, pool_width):
    N, D = xf.shape
    tm = IN_TILE
    spt = S // tm
    vec = lambda i: (i // spt, 0, 0)
    row = lambda i: (i, 0)
    gw3 = 3 * ATT_GROUP_WIDTH
    B = N // S
    res_spec = lambda d: pl.BlockSpec((d, tm // d, gw3), lambda i: (i // spt, i % spt, 0))
    res_shape = lambda d: jax.ShapeDtypeStruct((B * d, S // d, gw3), BF16)
    outs = pl.pallas_call(
        _in_kernel,
        grid=(N // tm,),
        in_specs=[pl.BlockSpec((tm, D), row),
                  pl.BlockSpec((1, D), lambda i: (0, 0)),
                  pl.BlockSpec((1, 1, D), vec),
                  pl.BlockSpec((1, 1, D), vec),
                  pl.BlockSpec(w_in_b.shape, lambda i: (0, 0))],
        out_specs=[pl.BlockSpec((tm, pool_width), row), pl.BlockSpec((tm, gw3), row)]
                  + [res_spec(d) for d in ATT_DILATIONS[1:]],
        out_shape=[jax.ShapeDtypeStruct((N, pool_width), F32), jax.ShapeDtypeStruct((N, gw3), BF16)]
                  + [res_shape(d) for d in ATT_DILATIONS[1:]],
        scratch_shapes=[pltpu.VMEM((ATT_GROUP_WIDTH // LANES, tm, LANES), F32)] * 2,
        name="in_proj",
    )(xf, g_mix, shift1, scale1, w_in_b)
    return outs[0], [o.reshape(N, gw3) for o in outs[1:]]


def _attn_kernel(nbs, a_ref, halo_ref, o_ref, lse_ref, kv_ref, band_ref):
    i = pl.program_id(0)
    R = a_ref.shape[0] // ATT_BLOCK
    gw = ATT_GROUP_WIDTH
    blk = ATT_BLOCK
    nh = ATT_HEADS_PER_GROUP
    kv_ref[0:blk, :] = halo_ref[:, gw:3 * gw]
    kv_ref[blk:, :] = a_ref[:, gw:3 * gw]
    row = lax.broadcasted_iota(I32, (nh * blk, 2 * blk), 0) % blk
    col = lax.broadcasted_iota(I32, (nh * blk, 2 * blk), 1)
    band_ref[...] = jnp.where((col >= row) & (col <= row + blk), 0.0, NEG_INF)
    head_of_lane = lax.broadcasted_iota(I32, (blk, gw), 1) // ATT_HEAD_DIM
    nt = (((1,), (1,)), ((), ()))

    def body(jj, carry):
        r0 = pl.multiple_of(jj * blk, blk)
        qf = a_ref[pl.ds(r0, blk), 0:gw].astype(F32) * (ATT_HEAD_DIM ** -0.5)
        q4 = jnp.concatenate([jnp.where(head_of_lane == h, qf, 0.0) for h in range(nh)], axis=0).astype(BF16)
        kc = kv_ref[pl.ds(r0, 2 * blk), 0:gw]
        vc = kv_ref[pl.ds(r0, 2 * blk), gw:2 * gw]
        s = lax.dot_general(q4, kc, nt, preferred_element_type=F32) + band_ref[...]
        first = ((i * R + jj) % nbs) == 0
        s = jnp.where(col >= jnp.where(first, blk, 0), s, NEG_INF)
        m = jnp.max(s, axis=1, keepdims=True)
        p = jnp.exp(s - m)
        l = jnp.sum(p, axis=1, keepdims=True)
        o4 = jnp.dot(p.astype(BF16), vc, preferred_element_type=F32) / l
        lse4 = m + jnp.log(l)
        o = jnp.zeros((blk, gw), F32)
        lse = jnp.zeros((blk, gw), F32)
        for h in range(nh):
            hm = head_of_lane == h
            o = jnp.where(hm, o4[h * blk:(h + 1) * blk, :], o)
            lse = jnp.where(hm, lse4[h * blk:(h + 1) * blk, :], lse)
        o_ref[pl.ds(r0, blk), :] = o
        lse_ref[pl.ds(r0, blk), :] = lse
        return carry

    lax.fori_loop(0, R, body, 0, unroll=8)


def _attention(a, nbs):
    N = a.shape[0]
    R = ATT_BLOCKS_PER_STEP
    gw = ATT_GROUP_WIDTH
    tm = R * ATT_BLOCK
    return pl.pallas_call(
        functools.partial(_attn_kernel, nbs),
        grid=(N // tm,),
        in_specs=[pl.BlockSpec((tm, 3 * gw), lambda i: (i, 0)),
                  pl.BlockSpec((ATT_BLOCK, 3 * gw), lambda i: (jnp.maximum(i * R - 1, 0), 0))],
        out_specs=[pl.BlockSpec((tm, gw), lambda i: (i, 0))] * 2,
        out_shape=[jax.ShapeDtypeStruct((N, gw), F32)] * 2,
        scratch_shapes=[pltpu.VMEM((tm + ATT_BLOCK, 2 * gw), BF16),
                        pltpu.VMEM((ATT_HEADS_PER_GROUP * ATT_BLOCK, 2 * ATT_BLOCK), F32)],
        name="attn",
    )(a, a)


def _mid_kernel(spt, u_ref, uh_ref, o0_ref, l0_ref, o1_ref, l1_ref, o2_ref, l2_ref, x_ref,
                pbd_ref, psc_ref, wout_ref, gate1_ref, gffn_ref, sh2_ref, sc2_ref, gate2_ref,
                wsg_ref, wsu_ref, wsd_ref, xacc_ref, h2_ref, til_ref):
    i = pl.program_id(0)
    tm, pw = u_ref.shape
    si = i % spt
    u = u_ref[...]
    keep = jnp.full((POOL_HALO, pw), si, I32) > 0
    ext = jnp.concatenate([jnp.where(keep, uh_ref[...], 0.0), u], axis=0)
    lane_grp = lax.broadcasted_iota(I32, (tm, pw), 1) // (pw // len(POOL_WINDOWS))
    pooled = jnp.zeros((tm, pw), F32)
    s, w = ext, 1
    while w < POOL_HALO:
        s = s + pltpu.roll(s, w, axis=0)
        w *= 2
        if w in POOL_WINDOWS:
            pooled = jnp.where(lane_grp == POOL_WINDOWS.index(w), s[POOL_HALO:, :], pooled)
    win = jnp.zeros((tm, pw), I32)
    for g, w in enumerate(POOL_WINDOWS):
        win = jnp.where(lane_grp == g, w, win)
    pos = si * tm + lax.broadcasted_iota(I32, (tm, pw), 0)
    cnt = jnp.minimum(pos + 1, win).astype(F32)
    pooled = pooled / cnt - u
    pool_out = jnp.dot(pooled.astype(BF16), pbd_ref[...], preferred_element_type=F32) * psc_ref[...]
    def token_order(slot, ref):
        d, n, w = ref.shape
        for r in range(d):
            for c in range(w // LANES):
                til_ref[slot, c, pl.ds(r, n, stride=d), :] = ref[r, :, c * LANES:(c + 1) * LANES]
        return jnp.concatenate([til_ref[slot, c] for c in range(w // LANES)], axis=1)

    l0 = l0_ref[...]
    l1 = token_order(0, l1_ref)
    l2 = token_order(1, l2_ref)
    m = jnp.maximum(jnp.maximum(l0, l1), l2)
    w0 = jnp.exp(l0 - m)
    w1 = jnp.exp(l1 - m)
    w2 = jnp.exp(l2 - m)
    attn = (w0 * o0_ref[...] + w1 * token_order(2, o1_ref) + w2 * token_order(3, o2_ref)) / (w0 + w1 + w2)
    mixed = (jnp.dot(pool_out.astype(BF16), wout_ref[0:pw, :], preferred_element_type=F32)
             + jnp.dot(attn.astype(BF16), wout_ref[pw:, :], preferred_element_type=F32))
    x1 = x_ref[...] + gate1_ref[0] * mixed
    h2 = _rms(x1) * gffn_ref[...]
    h2 = h2 * (1.0 + sc2_ref[0]) + sh2_ref[0]
    h2_ref[...] = _pack_bf16_pairs(h2)
    hb = h2.astype(BF16)
    a = jnp.dot(hb, wsg_ref[...], preferred_element_type=F32)
    b = jnp.dot(hb, wsu_ref[...], preferred_element_type=F32)
    act = (a * _sigmoid(a)) * b
    shared = jnp.dot(act.astype(BF16), wsd_ref[...], preferred_element_type=F32)
    xacc_ref[...] = x1 + gate2_ref[0] * shared


def _mid(u, attn_outs, xf, pool_bd, pool_scale, w_out_b, gate1, g_ffn, shift2, scale2, gate2,
         wsg_b, wsu_b, wsd_b, S):
    N, D = xf.shape
    pw = u.shape[1]
    tm = MID_TILE
    spt = S // tm
    row = lambda i: (i, 0)
    vec = lambda i: (i // spt, 0, 0)
    full = lambda a: pl.BlockSpec(a.shape, lambda i: (0,) * a.ndim)
    hpt = tm // POOL_HALO
    in_specs = [pl.BlockSpec((tm, pw), row),
                pl.BlockSpec((POOL_HALO, pw), lambda i: (jnp.maximum(i * hpt - 1, 0), 0))]
    gw = ATT_GROUP_WIDTH
    in_specs += [pl.BlockSpec((tm, gw), row)] * 2
    for d in ATT_DILATIONS[1:]:
        in_specs += [pl.BlockSpec((d, tm // d, gw), lambda i: (i // spt, i % spt, 0))] * 2
    in_specs += [pl.BlockSpec((tm, D), row), full(pool_bd), full(pool_scale), full(w_out_b),
                 pl.BlockSpec((1, 1, D), vec), full(g_ffn), pl.BlockSpec((1, 1, D), vec),
                 pl.BlockSpec((1, 1, D), vec), pl.BlockSpec((1, 1, D), vec),
                 full(wsg_b), full(wsu_b), full(wsd_b)]
    return pl.pallas_call(
        functools.partial(_mid_kernel, spt),
        grid=(N // tm,),
        in_specs=in_specs,
        out_specs=[pl.BlockSpec((tm, D), row), pl.BlockSpec((tm, D // 2), row)],
        out_shape=[jax.ShapeDtypeStruct((N, D), F32), jax.ShapeDtypeStruct((N, D // 2), U32)],
        scratch_shapes=[pltpu.VMEM((4, gw // LANES, tm, LANES), F32)],
        name="mid",
    )(u, u, *attn_outs, xf, pool_bd, pool_scale, w_out_b, gate1, g_ffn, shift2, scale2, gate2,
      wsg_b, wsu_b, wsd_b)


def _route_kernel(h_ref, wr_ref, bias_ref, e_ref, g_ref, r_ref, cnt_ref, base_ref):
    i = pl.program_id(0)
    T = h_ref.shape[0]
    E = wr_ref.shape[0]
    gsz = E // N_EXPERT_GROUPS

    @pl.when(i == 0)
    def _():
        base_ref[...] = jnp.zeros_like(base_ref)

    h_lo, h_hi = _unpack_bf16_pairs(h_ref[...])
    half = h_lo.shape[1]
    nt = (((1,), (1,)), ((), ()))
    logits = (lax.dot_general(wr_ref[:, 0:half], h_lo, nt, preferred_element_type=F32)
              + lax.dot_general(wr_ref[:, half:], h_hi, nt, preferred_element_type=F32))
    scores = _sigmoid(logits)
    biased = scores + bias_ref[...]
    giota = lax.broadcasted_iota(I32, (gsz, T), 0)
    gscore = []
    for g in range(N_EXPERT_GROUPS):
        blk = biased[g * gsz:(g + 1) * gsz, :]
        m1 = jnp.max(blk, axis=0, keepdims=True)
        i1 = jnp.min(jnp.where(blk == m1, giota, gsz), axis=0, keepdims=True)
        m2 = jnp.max(jnp.where(giota == i1, NEG_INF, blk), axis=0, keepdims=True)
        gscore.append(m1 + m2)
    parts = []
    for g in range(N_EXPERT_GROUPS):
        beaten = jnp.zeros((1, T), I32)
        for o in range(N_EXPERT_GROUPS):
            if o == g:
                continue
            wins = (gscore[o] >= gscore[g]) if o < g else (gscore[o] > gscore[g])
            beaten = beaten + wins.astype(I32)
        keep = jnp.broadcast_to(beaten, (gsz, T)) < TOPK_GROUPS
        parts.append(jnp.where(keep, biased[g * gsz:(g + 1) * gsz, :], NEG_INF))
    cur = jnp.concatenate(parts, axis=0)
    eiota = lax.broadcasted_iota(I32, (E, T), 0)
    selm = jnp.zeros((E, T), F32)
    idxs, gates = [], []
    for k in range(TOP_K):
        m = jnp.max(cur, axis=0, keepdims=True)
        idx = jnp.min(jnp.where(cur == m, eiota, E), axis=0, keepdims=True)
        oh = eiota == idx
        gates.append(jnp.sum(jnp.where(oh, scores, 0.0), axis=0, keepdims=True))
        idxs.append(idx)
        cur = jnp.where(oh, NEG_INF, cur)
        selm = jnp.where(oh, 1.0, selm)
    gsum = gates[0]
    for k in range(1, TOP_K):
        gsum = gsum + gates[k]
    before = (lax.broadcasted_iota(I32, (T, T), 0) < lax.broadcasted_iota(I32, (T, T), 1)).astype(BF16)
    tot = jnp.dot(selm.astype(BF16), before, preferred_element_type=F32) + base_ref[...]
    for k in range(TOP_K):
        e_ref[k:k + 1, :] = idxs[k]
        g_ref[k:k + 1, :] = gates[k] / gsum * ROUTED_SCALE
        r_ref[k:k + 1, :] = jnp.sum(jnp.where(eiota == idxs[k], tot, 0.0), axis=0, keepdims=True).astype(I32)
    base_ref[...] = base_ref[...] + jnp.sum(selm, axis=1, keepdims=True)
    cnt_ref[...] = base_ref[...].astype(I32)


def _route(h2p, wr_t, bias_col):
    N = h2p.shape[0]
    E, D = wr_t.shape
    T = ROUTE_TILE
    col = lambda i: (0, i)
    return pl.pallas_call(
        _route_kernel,
        grid=(N // T,),
        in_specs=[pl.BlockSpec((T, D // 2), lambda i: (i, 0)),
                  pl.BlockSpec((E, D), lambda i: (0, 0)),
                  pl.BlockSpec((E, 1), lambda i: (0, 0))],
        out_specs=[pl.BlockSpec((TOP_K, T), col), pl.BlockSpec((TOP_K, T), col),
                   pl.BlockSpec((TOP_K, T), col), pl.BlockSpec((E, 1), lambda i: (0, 0))],
        out_shape=[jax.ShapeDtypeStruct((TOP_K, N), I32), jax.ShapeDtypeStruct((TOP_K, N), F32),
                   jax.ShapeDtypeStruct((TOP_K, N), I32), jax.ShapeDtypeStruct((E, 1), I32)],
        scratch_shapes=[pltpu.VMEM((E, 1), F32)],
        compiler_params=pltpu.CompilerParams(dimension_semantics=("arbitrary",)),
        name="route",
    )(h2p, wr_t, bias_col)


def _dest_kernel(e_ref, r_ref, off_ref, dc_ref):
    E = off_ref.shape[0]
    T = e_ref.shape[1]
    C = dc_ref.shape[2]
    eiota = lax.broadcasted_iota(I32, (E, T), 0)
    off = off_ref[...]
    for k in range(TOP_K):
        start = jnp.sum(jnp.where(eiota == e_ref[k:k + 1, :], off, 0.0), axis=0, keepdims=True)
        d = start.astype(I32) + r_ref[k:k + 1, :]
        for c in range(T // C):
            dc_ref[c, k:k + 1, :] = d[:, c * C:(c + 1) * C]


def _dest(eidx, rank, offs_col):
    N = eidx.shape[1]
    E = offs_col.shape[0]
    T = DEST_TILE
    C = SC_CHUNK
    col = lambda i: (0, i)
    return pl.pallas_call(
        _dest_kernel,
        grid=(N // T,),
        in_specs=[pl.BlockSpec((TOP_K, T), col), pl.BlockSpec((TOP_K, T), col),
                  pl.BlockSpec((E, 1), lambda i: (0, 0))],
        out_specs=pl.BlockSpec((T // C, TOP_K, C), lambda i: (i, 0, 0)),
        out_shape=jax.ShapeDtypeStruct((N // C, TOP_K, C), I32),
        name="dest",
    )(eidx, rank, offs_col)


def _sc_dispatch(dest_c, h2p, P):
    N, W = h2p.shape
    C = dest_c.shape[2]
    info = plsc.get_sparse_core_info()
    nw = info.num_cores * info.num_subcores
    per_w = N // C // nw
    mesh = plsc.VectorSubcoreMesh(core_axis_name="c", subcore_axis_name="s")

    @functools.partial(
        pl.kernel, mesh=mesh,
        out_type=jax.ShapeDtypeStruct((P, W), h2p.dtype),
        scratch_types=[pltpu.VMEM((TOP_K, C), I32), pltpu.VMEM((C, W), h2p.dtype), pltpu.SemaphoreType.DMA],
        name="sc_dispatch",
    )
    def k(dest_hbm, h_hbm, xs_hbm, idx_v, rows_v, sem):
        wid = lax.axis_index("s") * info.num_cores + lax.axis_index("c")

        @pl.loop(0, per_w)
        def _(j):
            ch = wid * per_w + j
            pltpu.sync_copy(dest_hbm.at[ch], idx_v)
            pltpu.sync_copy(h_hbm.at[pl.ds(ch * C, C)], rows_v)
            copies = [pltpu.async_copy(rows_v, xs_hbm.at[idx_v.at[kk]], sem) for kk in range(TOP_K)]
            for cp in copies:
                cp.wait()

    return k(dest_c, h2p)


def _gmm_kernel(bstart_ref, nbe_ref, cnt_ref, nu_ref, wg_ref, wu_ref, wd_ref, xs_hbm, ys_hbm,
                wgb, wub, wdb, xbuf, ybuf, xsem, ysem):
    step = pl.program_id(0)
    last = pl.num_programs(0) - 1
    epg = wg_ref.shape[0]
    ring, bm = xbuf.shape[0], xbuf.shape[1]
    nblk = ys_hbm.shape[0] // bm
    nused = nu_ref[0]

    def x_copy(b, slot):
        return pltpu.make_async_copy(xs_hbm.at[pl.ds(pl.multiple_of(b * bm, bm), bm), :], xbuf.at[slot],
                                     xsem.at[slot])

    def y_copy(b, slot):
        return pltpu.make_async_copy(ybuf.at[slot], ys_hbm.at[pl.ds(pl.multiple_of(b * bm, bm), bm), :],
                                     ysem.at[slot])

    @pl.when(step == 0)
    def _():
        for j in range(ring - 1):
            @pl.when(j < nused)
            def _():
                x_copy(j, j).start()

    def run_expert(ee):
        e = step * epg + ee
        b0 = bstart_ref[e]
        nb = nbe_ref[e]

        @pl.when(nb > 0)
        def _():
            wgb[...] = wg_ref[ee].astype(BF16)
            wub[...] = wu_ref[ee].astype(BF16)
            wdb[...] = wd_ref[ee].astype(BF16)

            def prefetch(t):
                @pl.when(t < nused)
                def _():
                    x_copy(t, jnp.bitwise_and(t, ring - 1)).start()

            def process(b, n):
                slots = [jnp.bitwise_and(b + j, ring - 1) for j in range(n)]
                for j in range(n):
                    x_copy(b + j, slots[j]).wait()
                prefetch(b + ring - 1)
                for j in range(n):
                    @pl.when(b + j >= ring)
                    def _():
                        y_copy(b + j - ring, slots[j]).wait()
                rows = lax.broadcasted_iota(I32, (n * bm, 1), 0)
                valid = cnt_ref[e] - (b - b0) * bm
                xp = jnp.concatenate([xbuf[s] for s in slots], axis=0)
                xb = jnp.concatenate(_unpack_bf16_pairs(jnp.where(rows < valid, xp, jnp.uint32(0))), axis=1)
                a = jnp.dot(xb, wgb[...], preferred_element_type=F32)
                g = jnp.dot(xb, wub[...], preferred_element_type=F32)
                act = (a * _sigmoid(a)) * g
                yp = _pack_bf16_pairs(jnp.dot(act.astype(BF16), wdb[...], preferred_element_type=F32))
                for j in range(n):
                    ybuf[slots[j]] = yp[j * bm:(j + 1) * bm, :]
                    y_copy(b + j, slots[j]).start()
                for j in range(1, n):
                    prefetch(b + ring - 1 + j)

            pairs = lax.shift_right_logical(nb, 1)
            lax.fori_loop(0, pairs, lambda j, c: (process(b0 + 2 * j, 2), c)[1], 0)

            @pl.when(jnp.bitwise_and(nb, 1) != 0)
            def _():
                process(b0 + nb - 1, 1)

    for ee in range(epg):
        run_expert(ee)

    @pl.when(step == last)
    def _():
        for back in range(ring, 0, -1):
            @pl.when(nused >= back)
            def _():
                y_copy(nused - back, jnp.bitwise_and(nused - back, ring - 1)).wait()
        ybuf[0] = jnp.zeros(ybuf.shape[1:], ybuf.dtype)
        lax.fori_loop(nused, nblk, lambda b, c: (y_copy(b, 0).start(), c)[1], 0)
        lax.fori_loop(nused, nblk, lambda b, c: (y_copy(b, 0).wait(), c)[1], 0)


def _gmm(bstart, nb_e, counts, nused, xs, w_gate, w_up, w_down):
    P, W = xs.shape
    E, D, F = w_gate.shape
    bm = GMM_BLOCK
    epg = GMM_EXPERTS_PER_STEP
    wsel = lambda s, *_: (s, 0, 0)
    grid_spec = pltpu.PrefetchScalarGridSpec(
        num_scalar_prefetch=4,
        grid=(E // epg,),
        in_specs=[pl.BlockSpec((epg, D, F), wsel), pl.BlockSpec((epg, D, F), wsel), pl.BlockSpec((epg, F, D), wsel),
                  pl.BlockSpec(memory_space=pl.ANY)],
        out_specs=pl.BlockSpec(memory_space=pl.ANY),
        scratch_shapes=[pltpu.VMEM((D, F), BF16), pltpu.VMEM((D, F), BF16), pltpu.VMEM((F, D), BF16),
                        pltpu.VMEM((GMM_RING, bm, W), xs.dtype), pltpu.VMEM((GMM_RING, bm, W), xs.dtype),
                        pltpu.SemaphoreType.DMA((GMM_RING,)), pltpu.SemaphoreType.DMA((GMM_RING,))],
    )
    return pl.pallas_call(
        _gmm_kernel,
        grid_spec=grid_spec,
        out_shape=jax.ShapeDtypeStruct((P, W), xs.dtype),
        compiler_params=pltpu.CompilerParams(dimension_semantics=("arbitrary",)),
        name="gmm",
    )(bstart, nb_e, counts, nused, w_gate, w_up, w_down, xs)


def _sc_gather(dest_c, ys):
    nch, K, C = dest_c.shape
    W = ys.shape[1]
    H = C // 2
    info = plsc.get_sparse_core_info()
    nw = info.num_cores * info.num_subcores
    per_w = nch // nw
    nbuf = 3
    mesh = plsc.VectorSubcoreMesh(core_axis_name="c", subcore_axis_name="s")
    items = [(kk, hh) for kk in range(K) for hh in range(2)]

    @functools.partial(
        pl.kernel, mesh=mesh,
        out_type=jax.ShapeDtypeStruct((K, nch * C, W), ys.dtype),
        scratch_types=([pltpu.VMEM((K, C), I32)] + [pltpu.VMEM((H, W), ys.dtype)] * nbuf
                       + [pltpu.SemaphoreType.DMA] * (2 * nbuf)),
        name="sc_gather",
    )
    def k(dest_hbm, ys_hbm, yk_hbm, idx_v, *rest):
        bufs, gsem, wsem = rest[:nbuf], rest[nbuf:2 * nbuf], rest[2 * nbuf:]
        wid = lax.axis_index("s") * info.num_cores + lax.axis_index("c")

        @pl.loop(0, per_w)
        def _(j):
            ch = wid * per_w + j
            pltpu.sync_copy(dest_hbm.at[ch], idx_v)

            def gather(i):
                kk, hh = items[i]
                return pltpu.async_copy(ys_hbm.at[idx_v.at[kk, pl.ds(hh * H, H)]], bufs[i % nbuf], gsem[i % nbuf])

            def write(i):
                kk, hh = items[i]
                return pltpu.async_copy(bufs[i % nbuf], yk_hbm.at[kk, pl.ds(ch * C + hh * H, H)], wsem[i % nbuf])

            n = len(items)
            g = {0: gather(0), 1: gather(1)}
            w = {}
            for i in range(n):
                g[i].wait()
                w[i] = write(i)
                if i + 2 < n:
                    if i >= 1:
                        w.pop(i - 1).wait()
                    g[i + 2] = gather(i + 2)
            for i in sorted(w):
                w[i].wait()

    return k(dest_c, ys)


def _combine_kernel(yk_ref, gt_ref, gate2_ref, xacc_ref, gfin_ref, o_ref):
    gt = gt_ref[...]
    hi_mask = jnp.uint32(0xFFFF0000)
    acc_lo = acc_hi = None
    for k in range(TOP_K):
        p = yk_ref[k]
        g = gt[:, k:k + 1]
        lo = lax.bitcast_convert_type(p << 16, F32) * g
        hi = lax.bitcast_convert_type(p & hi_mask, F32) * g
        acc_lo = lo if k == 0 else acc_lo + lo
        acc_hi = hi if k == 0 else acc_hi + hi
    routed = jnp.concatenate([acc_lo, acc_hi], axis=1)
    x2 = xacc_ref[...] + gate2_ref[0] * routed
    o_ref[...] = _rms(x2) * gfin_ref[...]


def _combine(yk, gates_t, gate2, xacc, g_final, S):
    N, D = xacc.shape
    W = yk.shape[2]
    T = ROUTE_TILE
    spt = S // T
    return pl.pallas_call(
        _combine_kernel,
        grid=(N // T,),
        in_specs=[pl.BlockSpec((TOP_K, T, W), lambda i: (0, i, 0)),
                  pl.BlockSpec((T, TOP_K), lambda i: (i, 0)),
                  pl.BlockSpec((1, 1, D), lambda i: (i // spt, 0, 0)),
                  pl.BlockSpec((T, D), lambda i: (i, 0)),
                  pl.BlockSpec((1, D), lambda i: (0, 0))],
        out_specs=pl.BlockSpec((T, D), lambda i: (i, 0)),
        out_shape=jax.ShapeDtypeStruct((N, D), F32),
        name="combine",
    )(yk, gates_t, gate2, xacc, g_final)


def _layer(xf, B, S, mod, g_mix, w_in, pool_w, pool_scale, w_out, g_ffn, w_router, router_bias,
           w_gate, w_up, w_down, ws_gate, ws_up, ws_down):
    N, D = xf.shape
    E = w_router.shape[1]
    pw = pool_scale.shape[0]
    shift1, scale1, gate1, shift2, scale2, gate2 = [m.reshape(B, 1, D) for m in jnp.split(mod, 6, axis=-1)]
    u, qkv = _in_proj(xf, g_mix.reshape(1, D), shift1, scale1, w_in.astype(BF16), S, pw)
    attn_outs = []
    for a, d in zip(qkv, ATT_DILATIONS):
        o, lse = _attention(a, S // d // ATT_BLOCK)
        shape = (N, ATT_GROUP_WIDTH) if d == 1 else (B * d, S // d, ATT_GROUP_WIDTH)
        attn_outs += [o.reshape(shape), lse.reshape(shape)]
    ng = pool_w.shape[0]
    pool_bd = jnp.einsum('gcd,gh->gchd', pool_w, jnp.eye(ng, dtype=pool_w.dtype)).reshape(pw, pw).astype(BF16)
    xacc, h2 = _mid(u, attn_outs, xf, pool_bd, pool_scale.reshape(1, pw), w_out.astype(BF16), gate1,
                    g_ffn.reshape(1, D), shift2, scale2, gate2,
                    ws_gate.astype(BF16), ws_up.astype(BF16), ws_down.astype(BF16), S)
    eidx, gates, rank, counts = _route(h2, w_router.T.astype(BF16), router_bias.reshape(E, 1).astype(F32))
    bm = GMM_BLOCK
    nblk = N * TOP_K // bm + E
    nb_e = (counts[:, 0] + bm - 1) // bm
    bend = jnp.cumsum(nb_e)
    bstart = (bend - nb_e).astype(I32)
    nused = bend[-1:].astype(I32)
    dest_c = _dest(eidx, rank, (bstart * bm).astype(F32).reshape(E, 1))
    xs = _sc_dispatch(dest_c, h2, nblk * bm)
    ys = _gmm(bstart, nb_e.astype(I32), counts[:, 0], nused, xs, w_gate, w_up, w_down)
    return _sc_gather(dest_c, ys), gates.T, gate2, xacc


def kernel(x, c, w_ada, b_ada, g_mix, w_in, pool_w, pool_scale, w_out, g_ffn, w_router, router_bias,
           w_gate, w_up, w_down, ws_gate, ws_up, ws_down, g_final):
    B, S, D = x.shape
    depth = w_ada.shape[0]
    assert depth == 1, "the final residual is fused with the final norm, so exactly one layer is supported"
    assert S % (ATT_DILATIONS[-1] * ATT_BLOCK) == 0 and S % max(IN_TILE, MID_TILE, ATT_BLOCKS_PER_STEP * ATT_BLOCK) == 0
    xf = x.reshape(B * S, D)
    mod = _ada(c, w_ada[0], b_ada[0])
    yk, gates_t, gate2, xacc = _layer(
        xf, B, S, mod, g_mix[0], w_in[0], pool_w[0], pool_scale[0], w_out[0], g_ffn[0], w_router[0],
        router_bias[0], w_gate[0], w_up[0], w_down[0], ws_gate[0], ws_up[0], ws_down[0])
    out = _combine(yk, gates_t, gate2, xacc, g_final.reshape(1, D), S)
    return out.reshape(B, S, D)
```

```python
import functools

import jax
import jax.numpy as jnp
from jax import lax
from jax.experimental import pallas as pl
from jax.experimental.pallas import tpu as pltpu
from jax.experimental.pallas import tpu_sc as plsc

F32 = jnp.float32
BF16 = jnp.bfloat16
I32 = jnp.int32
U32 = jnp.uint32

LANES = 128
SINGLE_LOAD_STRIDE = 4
NORM_EPS = 1e-6
POOL_WINDOWS = (2, 4, 8, 16)
POOL_HALO = 16
ATT_DILATIONS = (1, 4, 16)
ATT_BLOCK = 128
ATT_HEADS_PER_GROUP = 4
ATT_HEAD_DIM = 64
ATT_GROUP_WIDTH = ATT_HEADS_PER_GROUP * ATT_HEAD_DIM
N_EXPERT_GROUPS = 8
TOPK_GROUPS = 4
TOP_K = 8
ROUTED_SCALE = 2.5

IN_TILE = 512
ATT_BLOCKS_PER_STEP = 16
MID_TILE = 512
ROUTE_TILE = 256
DEST_TILE = 1024
GMM_BLOCK = 256
GMM_RING = 8
GMM_EXPERTS_PER_STEP = 1
SC_CHUNK = 128

NEG_INF = float("-inf")


def _sigmoid(v):
    return 1.0 / (1.0 + jnp.exp(-v))


def _rms(v):
    return v * lax.rsqrt(jnp.mean(v * v, axis=-1, keepdims=True) + NORM_EPS)


def _pack_bf16_pairs(v):
    n = v.shape[1] // 2
    lo = lax.bitcast_convert_type(v[:, :n].astype(BF16).astype(F32), U32)
    hi = lax.bitcast_convert_type(v[:, n:].astype(BF16).astype(F32), U32)
    return (hi & jnp.uint32(0xFFFF0000)) | (lo >> 16)


def _unpack_bf16_pairs(p):
    lo = lax.bitcast_convert_type(p << 16, F32).astype(BF16)
    hi = lax.bitcast_convert_type(p & jnp.uint32(0xFFFF0000), F32).astype(BF16)
    return lo, hi


def _ada_kernel(c_ref, w_ref, b_ref, o_ref):
    c = c_ref[...]
    cs = c * _sigmoid(c)
    o_ref[...] = jnp.dot(cs, w_ref[...], preferred_element_type=F32,
                         precision=lax.Precision.HIGHEST) + b_ref[...]


def _ada(c, w_ada, b_ada):
    B, D = c.shape
    W = w_ada.shape[1]
    tn = 1024
    return pl.pallas_call(
        _ada_kernel,
        grid=(W // tn,),
        in_specs=[pl.BlockSpec((B, D), lambda j: (0, 0)),
                  pl.BlockSpec((D, tn), lambda j: (0, j)),
                  pl.BlockSpec((1, tn), lambda j: (0, j))],
        out_specs=pl.BlockSpec((B, tn), lambda j: (0, j)),
        out_shape=jax.ShapeDtypeStruct((B, W), F32),
        name="ada",
    )(c, w_ada, b_ada.reshape(1, W))


def _in_kernel(x_ref, g_ref, sh_ref, sc_ref, w_ref, pool_ref, q0_ref, q1_ref, q2_ref, scr_ref, tmp_ref):
    h = _rms(x_ref[...]) * g_ref[...]
    h = h * (1.0 + sc_ref[0]) + sh_ref[0]
    hb = h.astype(BF16)
    tm = x_ref.shape[0]
    pw = pool_ref.shape[1]
    gw = ATT_GROUP_WIDTH
    pool_ref[...] = jnp.dot(hb, w_ref[:, 0:pw], preferred_element_type=F32)
    for g, (out, d) in enumerate(zip((q0_ref, q1_ref, q2_ref), ATT_DILATIONS)):
        for sec in range(3):
            c0 = pw + sec * 3 * gw + g * gw
            res = jnp.dot(hb, w_ref[:, c0:c0 + gw], preferred_element_type=F32)
            if d == 1:
                out[:, sec * gw:(sec + 1) * gw] = res.astype(BF16)
            else:
                for c in range(gw // LANES):
                    scr_ref[c] = res[:, c * LANES:(c + 1) * LANES]
                    src, f1 = scr_ref.at[c], 1
                    if d > SINGLE_LOAD_STRIDE:
                        f1 = SINGLE_LOAD_STRIDE
                        for q in range(f1):
                            tmp_ref[c, q * (tm // f1):(q + 1) * (tm // f1), :] = scr_ref[c, pl.ds(q, tm // f1, stride=f1), :]
                        src = tmp_ref.at[c]
                    for r in range(d):
                        r_lo, r_hi = r % f1, r // f1
                        c1 = sec * gw + c * LANES
                        out[r, :, c1:c1 + LANES] = src[pl.ds(r_lo * (tm // f1) + r_hi, tm // d, stride=d // f1),
                                                       :].astype(BF16)


def _in_proj(xf, g_mix, shift1, scale1, w_in_b, S, pool_width):
    N, D = xf.shape
    tm = IN_TILE
    spt = S // tm
    vec = lambda i: (i // spt, 0, 0)
    row = lambda i: (i, 0)
    gw3 = 3 * ATT_GROUP_WIDTH
    B = N // S
    res_spec = lambda d: pl.BlockSpec((d, tm // d, gw3), lambda i: (i // spt, i % spt, 0))
    res_shape = lambda d: jax.ShapeDtypeStruct((B * d, S // d, gw3), BF16)
    outs = pl.pallas_call(
        _in_kernel,
        grid=(N // tm,),
        in_specs=[pl.BlockSpec((tm, D), row),
                  pl.BlockSpec((1, D), lambda i: (0, 0)),
                  pl.BlockSpec((1, 1, D), vec),
                  pl.BlockSpec((1, 1, D), vec),
                  pl.BlockSpec(w_in_b.shape, lambda i: (0, 0))],
        out_specs=[pl.BlockSpec((tm, pool_width), row), pl.BlockSpec((tm, gw3), row)]
                  + [res_spec(d) for d in ATT_DILATIONS[1:]],
        out_shape=[jax.ShapeDtypeStruct((N, pool_width), F32), jax.ShapeDtypeStruct((N, gw3), BF16)]
                  + [res_shape(d) for d in ATT_DILATIONS[1:]],
        scratch_shapes=[pltpu.VMEM((ATT_GROUP_WIDTH // LANES, tm, LANES), F32)] * 2,
        name="in_proj",
    )(xf, g_mix, shift1, scale1, w_in_b)
    return outs[0], [o.reshape(N, gw3) for o in outs[1:]]


def _attn_kernel(nbs, a_ref, halo_ref, o_ref, lse_ref, kv_ref, band_ref):
    i = pl.program_id(0)
    R = a_ref.shape[0] // ATT_BLOCK
    gw = ATT_GROUP_WIDTH
    blk = ATT_BLOCK
    nh = ATT_HEADS_PER_GROUP
    kv_ref[0:blk, :] = halo_ref[:, gw:3 * gw]
    kv_ref[blk:, :] = a_ref[:, gw:3 * gw]
    row = lax.broadcasted_iota(I32, (nh * blk, 2 * blk), 0) % blk
    col = lax.broadcasted_iota(I32, (nh * blk, 2 * blk), 1)
    band_ref[...] = jnp.where((col >= row) & (col <= row + blk), 0.0, NEG_INF)
    head_of_lane = lax.broadcasted_iota(I32, (blk, gw), 1) // ATT_HEAD_DIM
    nt = (((1,), (1,)), ((), ()))

    def body(jj, carry):
        r0 = pl.multiple_of(jj * blk, blk)
        qf = a_ref[pl.ds(r0, blk), 0:gw].astype(F32) * (ATT_HEAD_DIM ** -0.5)
        q4 = jnp.concatenate([jnp.where(head_of_lane == h, qf, 0.0) for h in range(nh)], axis=0).astype(BF16)
        kc = kv_ref[pl.ds(r0, 2 * blk), 0:gw]
        vc = kv_ref[pl.ds(r0, 2 * blk), gw:2 * gw]
        s = lax.dot_general(q4, kc, nt, preferred_element_type=F32) + band_ref[...]
        first = ((i * R + jj) % nbs) == 0
        s = jnp.where(col >= jnp.where(first, blk, 0), s, NEG_INF)
        m = jnp.max(s, axis=1, keepdims=True)
        p = jnp.exp(s - m)
        l = jnp.sum(p, axis=1, keepdims=True)
        o4 = jnp.dot(p.astype(BF16), vc, preferred_element_type=F32) / l
        lse4 = m + jnp.log(l)
        o = jnp.zeros((blk, gw), F32)
        lse = jnp.zeros((blk, gw), F32)
        for h in range(nh):
            hm = head_of_lane == h
            o = jnp.where(hm, o4[h * blk:(h + 1) * blk, :], o)
            lse = jnp.where(hm, lse4[h * blk:(h + 1) * blk, :], lse)
        o_ref[pl.ds(r0, blk), :] = o
        lse_ref[pl.ds(r0, blk), :] = lse
        return carry

    lax.fori_loop(0, R, body, 0, unroll=8)


def _attention(a, nbs):
    N = a.shape[0]
    R = ATT_BLOCKS_PER_STEP
    gw = ATT_GROUP_WIDTH
    tm = R * ATT_BLOCK
    return pl.pallas_call(
        functools.partial(_attn_kernel, nbs),
        grid=(N // tm,),
        in_specs=[pl.BlockSpec((tm, 3 * gw), lambda i: (i, 0)),
                  pl.BlockSpec((ATT_BLOCK, 3 * gw), lambda i: (jnp.maximum(i * R - 1, 0), 0))],
        out_specs=[pl.BlockSpec((tm, gw), lambda i: (i, 0))] * 2,
        out_shape=[jax.ShapeDtypeStruct((N, gw), F32)] * 2,
        scratch_shapes=[pltpu.VMEM((tm + ATT_BLOCK, 2 * gw), BF16),
                        pltpu.VMEM((ATT_HEADS_PER_GROUP * ATT_BLOCK, 2 * ATT_BLOCK), F32)],
        name="attn",
    )(a, a)


def _mid_kernel(spt, u_ref, uh_ref, o0_ref, l0_ref, o1_ref, l1_ref, o2_ref, l2_ref, x_ref,
                pbd_ref, psc_ref, wout_ref, gate1_ref, gffn_ref, sh2_ref, sc2_ref, gate2_ref,
                wsg_ref, wsu_ref, wsd_ref, wr_ref, bias_ref,
                xacc_ref, h2_ref, e_ref, g_ref, r_ref, cnt_ref, til_ref, base_ref):
    i = pl.program_id(0)
    tm, pw = u_ref.shape
    si = i % spt
    u = u_ref[...]
    keep = jnp.full((POOL_HALO, pw), si, I32) > 0
    ext = jnp.concatenate([jnp.where(keep, uh_ref[...], 0.0), u], axis=0)
    lane_grp = lax.broadcasted_iota(I32, (tm, pw), 1) // (pw // len(POOL_WINDOWS))
    pooled = jnp.zeros((tm, pw), F32)
    s, w = ext, 1
    while w < POOL_HALO:
        s = s + pltpu.roll(s, w, axis=0)
        w *= 2
        if w in POOL_WINDOWS:
            pooled = jnp.where(lane_grp == POOL_WINDOWS.index(w), s[POOL_HALO:, :], pooled)
    win = jnp.zeros((tm, pw), I32)
    for g, w in enumerate(POOL_WINDOWS):
        win = jnp.where(lane_grp == g, w, win)
    pos = si * tm + lax.broadcasted_iota(I32, (tm, pw), 0)
    cnt = jnp.minimum(pos + 1, win).astype(F32)
    pooled = pooled / cnt - u
    pool_out = jnp.dot(pooled.astype(BF16), pbd_ref[...], preferred_element_type=F32) * psc_ref[...]
    def token_order(slot, ref):
        d, n, w = ref.shape
        for r in range(d):
            for c in range(w // LANES):
                til_ref[slot, c, pl.ds(r, n, stride=d), :] = ref[r, :, c * LANES:(c + 1) * LANES]
        return jnp.concatenate([til_ref[slot, c] for c in range(w // LANES)], axis=1)

    l0 = l0_ref[...]
    l1 = token_order(0, l1_ref)
    l2 = token_order(1, l2_ref)
    m = jnp.maximum(jnp.maximum(l0, l1), l2)
    w0 = jnp.exp(l0 - m)
    w1 = jnp.exp(l1 - m)
    w2 = jnp.exp(l2 - m)
    attn = (w0 * o0_ref[...] + w1 * token_order(2, o1_ref) + w2 * token_order(3, o2_ref)) / (w0 + w1 + w2)
    mixed = (jnp.dot(pool_out.astype(BF16), wout_ref[0:pw, :], preferred_element_type=F32)
             + jnp.dot(attn.astype(BF16), wout_ref[pw:, :], preferred_element_type=F32))
    x1 = x_ref[...] + gate1_ref[0] * mixed
    h2 = _rms(x1) * gffn_ref[...]
    h2 = h2 * (1.0 + sc2_ref[0]) + sh2_ref[0]
    h2_ref[...] = _pack_bf16_pairs(h2)
    hb = h2.astype(BF16)
    a = jnp.dot(hb, wsg_ref[...], preferred_element_type=F32)
    b = jnp.dot(hb, wsu_ref[...], preferred_element_type=F32)
    act = (a * _sigmoid(a)) * b
    shared = jnp.dot(act.astype(BF16), wsd_ref[...], preferred_element_type=F32)
    xacc_ref[...] = x1 + gate2_ref[0] * shared
    @pl.when(i == 0)
    def _():
        base_ref[...] = jnp.zeros_like(base_ref)

    for t0 in range(0, tm, ROUTE_TILE):
        idxs, gates, ranks = _route_tile(hb[t0:t0 + ROUTE_TILE, :], wr_ref, bias_ref, base_ref)
        for k in range(TOP_K):
            e_ref[k:k + 1, t0:t0 + ROUTE_TILE] = idxs[k]
            g_ref[k:k + 1, t0:t0 + ROUTE_TILE] = gates[k]
            r_ref[k:k + 1, t0:t0 + ROUTE_TILE] = ranks[k]
    cnt_ref[...] = base_ref[...].astype(I32)


def _mid(u, attn_outs, xf, pool_bd, pool_scale, w_out_b, gate1, g_ffn, shift2, scale2, gate2,
         wsg_b, wsu_b, wsd_b, wr_t, bias_col, S):
    N, D = xf.shape
    E = wr_t.shape[0]
    pw = u.shape[1]
    tm = MID_TILE
    spt = S // tm
    row = lambda i: (i, 0)
    vec = lambda i: (i // spt, 0, 0)
    full = lambda a: pl.BlockSpec(a.shape, lambda i: (0,) * a.ndim)
    hpt = tm // POOL_HALO
    in_specs = [pl.BlockSpec((tm, pw), row),
                pl.BlockSpec((POOL_HALO, pw), lambda i: (jnp.maximum(i * hpt - 1, 0), 0))]
    gw = ATT_GROUP_WIDTH
    in_specs += [pl.BlockSpec((tm, gw), row)] * 2
    for d in ATT_DILATIONS[1:]:
        in_specs += [pl.BlockSpec((d, tm // d, gw), lambda i: (i // spt, i % spt, 0))] * 2
    in_specs += [pl.BlockSpec((tm, D), row), full(pool_bd), full(pool_scale), full(w_out_b),
                 pl.BlockSpec((1, 1, D), vec), full(g_ffn), pl.BlockSpec((1, 1, D), vec),
                 pl.BlockSpec((1, 1, D), vec), pl.BlockSpec((1, 1, D), vec),
                 full(wsg_b), full(wsu_b), full(wsd_b), full(wr_t), full(bias_col)]
    col = lambda i: (0, i)
    return pl.pallas_call(
        functools.partial(_mid_kernel, spt),
        grid=(N // tm,),
        in_specs=in_specs,
        out_specs=[pl.BlockSpec((tm, D), row), pl.BlockSpec((tm, D // 2), row),
                   pl.BlockSpec((TOP_K, tm), col), pl.BlockSpec((TOP_K, tm), col), pl.BlockSpec((TOP_K, tm), col),
                   pl.BlockSpec((E, 1), lambda i: (0, 0))],
        out_shape=[jax.ShapeDtypeStruct((N, D), F32), jax.ShapeDtypeStruct((N, D // 2), U32),
                   jax.ShapeDtypeStruct((TOP_K, N), I32), jax.ShapeDtypeStruct((TOP_K, N), F32),
                   jax.ShapeDtypeStruct((TOP_K, N), I32), jax.ShapeDtypeStruct((E, 1), I32)],
        scratch_shapes=[pltpu.VMEM((4, gw // LANES, tm, LANES), F32), pltpu.VMEM((E, 1), F32)],
        compiler_params=pltpu.CompilerParams(dimension_semantics=("arbitrary",)),
        name="mid",
    )(u, u, *attn_outs, xf, pool_bd, pool_scale, w_out_b, gate1, g_ffn, shift2, scale2, gate2,
      wsg_b, wsu_b, wsd_b, wr_t, bias_col)


def _route_tile(hb, wr_ref, bias_ref, base_ref):
    T = hb.shape[0]
    E = wr_ref.shape[0]
    gsz = E // N_EXPERT_GROUPS
    logits = lax.dot_general(wr_ref[...], hb, (((1,), (1,)), ((), ())), preferred_element_type=F32)
    scores = _sigmoid(logits)
    biased = scores + bias_ref[...]
    giota = lax.broadcasted_iota(I32, (gsz, T), 0)
    gscore = []
    for g in range(N_EXPERT_GROUPS):
        blk = biased[g * gsz:(g + 1) * gsz, :]
        m1 = jnp.max(blk, axis=0, keepdims=True)
        i1 = jnp.min(jnp.where(blk == m1, giota, gsz), axis=0, keepdims=True)
        m2 = jnp.max(jnp.where(giota == i1, NEG_INF, blk), axis=0, keepdims=True)
        gscore.append(m1 + m2)
    parts = []
    for g in range(N_EXPERT_GROUPS):
        beaten = jnp.zeros((1, T), I32)
        for o in range(N_EXPERT_GROUPS):
            if o == g:
                continue
            wins = (gscore[o] >= gscore[g]) if o < g else (gscore[o] > gscore[g])
            beaten = beaten + wins.astype(I32)
        keep = jnp.broadcast_to(beaten, (gsz, T)) < TOPK_GROUPS
        parts.append(jnp.where(keep, biased[g * gsz:(g + 1) * gsz, :], NEG_INF))
    cur = jnp.concatenate(parts, axis=0)
    eiota = lax.broadcasted_iota(I32, (E, T), 0)
    selm = jnp.zeros((E, T), F32)
    idxs, gates = [], []
    for k in range(TOP_K):
        m = jnp.max(cur, axis=0, keepdims=True)
        idx = jnp.min(jnp.where(cur == m, eiota, E), axis=0, keepdims=True)
        oh = eiota == idx
        gates.append(jnp.sum(jnp.where(oh, scores, 0.0), axis=0, keepdims=True))
        idxs.append(idx)
        cur = jnp.where(oh, NEG_INF, cur)
        selm = jnp.where(oh, 1.0, selm)
    gsum = gates[0]
    for k in range(1, TOP_K):
        gsum = gsum + gates[k]
    gates = [gk / gsum * ROUTED_SCALE for gk in gates]
    before = (lax.broadcasted_iota(I32, (T, T), 0) < lax.broadcasted_iota(I32, (T, T), 1)).astype(BF16)
    tot = jnp.dot(selm.astype(BF16), before, preferred_element_type=F32) + base_ref[...]
    ranks = [jnp.sum(jnp.where(eiota == idxs[k], tot, 0.0), axis=0, keepdims=True).astype(I32)
             for k in range(TOP_K)]
    base_ref[...] = base_ref[...] + jnp.sum(selm, axis=1, keepdims=True)
    return idxs, gates, ranks


def _dest_kernel(e_ref, r_ref, off_ref, dc_ref):
    E = off_ref.shape[0]
    T = e_ref.shape[1]
    C = dc_ref.shape[2]
    eiota = lax.broadcasted_iota(I32, (E, T), 0)
    off = off_ref[...]
    for k in range(TOP_K):
        start = jnp.sum(jnp.where(eiota == e_ref[k:k + 1, :], off, 0.0), axis=0, keepdims=True)
        d = start.astype(I32) + r_ref[k:k + 1, :]
        for c in range(T // C):
            dc_ref[c, k:k + 1, :] = d[:, c * C:(c + 1) * C]


def _dest(eidx, rank, offs_col):
    N = eidx.shape[1]
    E = offs_col.shape[0]
    T = DEST_TILE
    C = SC_CHUNK
    col = lambda i: (0, i)
    return pl.pallas_call(
        _dest_kernel,
        grid=(N // T,),
        in_specs=[pl.BlockSpec((TOP_K, T), col), pl.BlockSpec((TOP_K, T), col),
                  pl.BlockSpec((E, 1), lambda i: (0, 0))],
        out_specs=pl.BlockSpec((T // C, TOP_K, C), lambda i: (i, 0, 0)),
        out_shape=jax.ShapeDtypeStruct((N // C, TOP_K, C), I32),
        name="dest",
    )(eidx, rank, offs_col)


def _sc_dispatch(dest_c, h2p, P):
    N, W = h2p.shape
    C = dest_c.shape[2]
    info = plsc.get_sparse_core_info()
    nw = info.num_cores * info.num_subcores
    per_w = N // C // nw
    mesh = plsc.VectorSubcoreMesh(core_axis_name="c", subcore_axis_name="s")

    @functools.partial(
        pl.kernel, mesh=mesh,
        out_type=jax.ShapeDtypeStruct((P, W), h2p.dtype),
        scratch_types=[pltpu.VMEM((TOP_K, C), I32), pltpu.VMEM((C, W), h2p.dtype), pltpu.SemaphoreType.DMA],
        name="sc_dispatch",
    )
    def k(dest_hbm, h_hbm, xs_hbm, idx_v, rows_v, sem):
        wid = lax.axis_index("s") * info.num_cores + lax.axis_index("c")

        @pl.loop(0, per_w)
        def _(j):
            ch = wid * per_w + j
            pltpu.sync_copy(dest_hbm.at[ch], idx_v)
            pltpu.sync_copy(h_hbm.at[pl.ds(ch * C, C)], rows_v)
            copies = [pltpu.async_copy(rows_v, xs_hbm.at[idx_v.at[kk]], sem) for kk in range(TOP_K)]
            for cp in copies:
                cp.wait()

    return k(dest_c, h2p)


def _gmm_kernel(bstart_ref, nbe_ref, cnt_ref, nu_ref, wg_ref, wu_ref, wd_ref, xs_hbm, ys_hbm,
                wgb, wub, wdb, xbuf, ybuf, xsem, ysem):
    step = pl.program_id(0)
    last = pl.num_programs(0) - 1
    epg = wg_ref.shape[0]
    ring, bm = xbuf.shape[0], xbuf.shape[1]
    nblk = ys_hbm.shape[0] // bm
    nused = nu_ref[0]

    def x_copy(b, slot):
        return pltpu.make_async_copy(xs_hbm.at[pl.ds(pl.multiple_of(b * bm, bm), bm), :], xbuf.at[slot],
                                     xsem.at[slot])

    def y_copy(b, slot):
        return pltpu.make_async_copy(ybuf.at[slot], ys_hbm.at[pl.ds(pl.multiple_of(b * bm, bm), bm), :],
                                     ysem.at[slot])

    @pl.when(step == 0)
    def _():
        for j in range(ring - 1):
            @pl.when(j < nused)
            def _():
                x_copy(j, j).start()

    def run_expert(ee):
        e = step * epg + ee
        b0 = bstart_ref[e]
        nb = nbe_ref[e]

        @pl.when(nb > 0)
        def _():
            wgb[...] = wg_ref[ee].astype(BF16)
            wub[...] = wu_ref[ee].astype(BF16)
            wdb[...] = wd_ref[ee].astype(BF16)

            def prefetch(t):
                @pl.when(t < nused)
                def _():
                    x_copy(t, jnp.bitwise_and(t, ring - 1)).start()

            def process(b, n):
                slots = [jnp.bitwise_and(b + j, ring - 1) for j in range(n)]
                for j in range(n):
                    x_copy(b + j, slots[j]).wait()
                prefetch(b + ring - 1)
                for j in range(n):
                    @pl.when(b + j >= ring)
                    def _():
                        y_copy(b + j - ring, slots[j]).wait()
                rows = lax.broadcasted_iota(I32, (n * bm, 1), 0)
                valid = cnt_ref[e] - (b - b0) * bm
                xp = jnp.concatenate([xbuf[s] for s in slots], axis=0)
                xb = jnp.concatenate(_unpack_bf16_pairs(jnp.where(rows < valid, xp, jnp.uint32(0))), axis=1)
                a = jnp.dot(xb, wgb[...], preferred_element_type=F32)
                g = jnp.dot(xb, wub[...], preferred_element_type=F32)
                act = (a * _sigmoid(a)) * g
                yp = _pack_bf16_pairs(jnp.dot(act.astype(BF16), wdb[...], preferred_element_type=F32))
                for j in range(n):
                    ybuf[slots[j]] = yp[j * bm:(j + 1) * bm, :]
                    y_copy(b + j, slots[j]).start()
                for j in range(1, n):
                    prefetch(b + ring - 1 + j)

            pairs = lax.shift_right_logical(nb, 1)
            lax.fori_loop(0, pairs, lambda j, c: (process(b0 + 2 * j, 2), c)[1], 0)

            @pl.when(jnp.bitwise_and(nb, 1) != 0)
            def _():
                process(b0 + nb - 1, 1)

    for ee in range(epg):
        run_expert(ee)

    @pl.when(step == last)
    def _():
        for back in range(ring, 0, -1):
            @pl.when(nused >= back)
            def _():
                y_copy(nused - back, jnp.bitwise_and(nused - back, ring - 1)).wait()
        ybuf[0] = jnp.zeros(ybuf.shape[1:], ybuf.dtype)
        lax.fori_loop(nused, nblk, lambda b, c: (y_copy(b, 0).start(), c)[1], 0)
        lax.fori_loop(nused, nblk, lambda b, c: (y_copy(b, 0).wait(), c)[1], 0)


def _gmm(bstart, nb_e, counts, nused, xs, w_gate, w_up, w_down):
    P, W = xs.shape
    E, D, F = w_gate.shape
    bm = GMM_BLOCK
    epg = GMM_EXPERTS_PER_STEP
    wsel = lambda s, *_: (s, 0, 0)
    grid_spec = pltpu.PrefetchScalarGridSpec(
        num_scalar_prefetch=4,
        grid=(E // epg,),
        in_specs=[pl.BlockSpec((epg, D, F), wsel), pl.BlockSpec((epg, D, F), wsel), pl.BlockSpec((epg, F, D), wsel),
                  pl.BlockSpec(memory_space=pl.ANY)],
        out_specs=pl.BlockSpec(memory_space=pl.ANY),
        scratch_shapes=[pltpu.VMEM((D, F), BF16), pltpu.VMEM((D, F), BF16), pltpu.VMEM((F, D), BF16),
                        pltpu.VMEM((GMM_RING, bm, W), xs.dtype), pltpu.VMEM((GMM_RING, bm, W), xs.dtype),
                        pltpu.SemaphoreType.DMA((GMM_RING,)), pltpu.SemaphoreType.DMA((GMM_RING,))],
    )
    return pl.pallas_call(
        _gmm_kernel,
        grid_spec=grid_spec,
        out_shape=jax.ShapeDtypeStruct((P, W), xs.dtype),
        compiler_params=pltpu.CompilerParams(dimension_semantics=("arbitrary",)),
        name="gmm",
    )(bstart, nb_e, counts, nused, w_gate, w_up, w_down, xs)


def _sc_gather(dest_c, ys):
    nch, K, C = dest_c.shape
    W = ys.shape[1]
    H = C // 2
    info = plsc.get_sparse_core_info()
    nw = info.num_cores * info.num_subcores
    per_w = nch // nw
    nbuf = 3
    mesh = plsc.VectorSubcoreMesh(core_axis_name="c", subcore_axis_name="s")
    items = [(kk, hh) for kk in range(K) for hh in range(2)]

    @functools.partial(
        pl.kernel, mesh=mesh,
        out_type=jax.ShapeDtypeStruct((K, nch * C, W), ys.dtype),
        scratch_types=([pltpu.VMEM((K, C), I32)] + [pltpu.VMEM((H, W), ys.dtype)] * nbuf
                       + [pltpu.SemaphoreType.DMA] * (2 * nbuf)),
        name="sc_gather",
    )
    def k(dest_hbm, ys_hbm, yk_hbm, idx_v, *rest):
        bufs, gsem, wsem = rest[:nbuf], rest[nbuf:2 * nbuf], rest[2 * nbuf:]
        wid = lax.axis_index("s") * info.num_cores + lax.axis_index("c")

        @pl.loop(0, per_w)
        def _(j):
            ch = wid * per_w + j
            pltpu.sync_copy(dest_hbm.at[ch], idx_v)

            def gather(i):
                kk, hh = items[i]
                return pltpu.async_copy(ys_hbm.at[idx_v.at[kk, pl.ds(hh * H, H)]], bufs[i % nbuf], gsem[i % nbuf])

            def write(i):
                kk, hh = items[i]
                return pltpu.async_copy(bufs[i % nbuf], yk_hbm.at[kk, pl.ds(ch * C + hh * H, H)], wsem[i % nbuf])

            n = len(items)
            g = {0: gather(0), 1: gather(1)}
            w = {}
            for i in range(n):
                g[i].wait()
                w[i] = write(i)
                if i + 2 < n:
                    if i >= 1:
                        w.pop(i - 1).wait()
                    g[i + 2] = gather(i + 2)
            for i in sorted(w):
                w[i].wait()

    return k(dest_c, ys)


def _combine_kernel(yk_ref, gt_ref, gate2_ref, xacc_ref, gfin_ref, o_ref):
    gt = gt_ref[...]
    hi_mask = jnp.uint32(0xFFFF0000)
    acc_lo = acc_hi = None
    for k in range(TOP_K):
        p = yk_ref[k]
        g = gt[:, k:k + 1]
        lo = lax.bitcast_convert_type(p << 16, F32) * g
        hi = lax.bitcast_convert_type(p & hi_mask, F32) * g
        acc_lo = lo if k == 0 else acc_lo + lo
        acc_hi = hi if k == 0 else acc_hi + hi
    routed = jnp.concatenate([acc_lo, acc_hi], axis=1)
    x2 = xacc_ref[...] + gate2_ref[0] * routed
    o_ref[...] = _rms(x2) * gfin_ref[...]


def _combine(yk, gates_t, gate2, xacc, g_final, S):
    N, D = xacc.shape
    W = yk.shape[2]
    T = ROUTE_TILE
    spt = S // T
    return pl.pallas_call(
        _combine_kernel,
        grid=(N // T,),
        in_specs=[pl.BlockSpec((TOP_K, T, W), lambda i: (0, i, 0)),
                  pl.BlockSpec((T, TOP_K), lambda i: (i, 0)),
                  pl.BlockSpec((1, 1, D), lambda i: (i // spt, 0, 0)),
                  pl.BlockSpec((T, D), lambda i: (i, 0)),
                  pl.BlockSpec((1, D), lambda i: (0, 0))],
        out_specs=pl.BlockSpec((T, D), lambda i: (i, 0)),
        out_shape=jax.ShapeDtypeStruct((N, D), F32),
        name="combine",
    )(yk, gates_t, gate2, xacc, g_final)


def _layer(xf, B, S, mod, g_mix, w_in, pool_w, pool_scale, w_out, g_ffn, w_router, router_bias,
           w_gate, w_up, w_down, ws_gate, ws_up, ws_down):
    N, D = xf.shape
    E = w_router.shape[1]
    pw = pool_scale.shape[0]
    shift1, scale1, gate1, shift2, scale2, gate2 = [m.reshape(B, 1, D) for m in jnp.split(mod, 6, axis=-1)]
    u, qkv = _in_proj(xf, g_mix.reshape(1, D), shift1, scale1, w_in.astype(BF16), S, pw)
    attn_outs = []
    for a, d in zip(qkv, ATT_DILATIONS):
        o, lse = _attention(a, S // d // ATT_BLOCK)
        shape = (N, ATT_GROUP_WIDTH) if d == 1 else (B * d, S // d, ATT_GROUP_WIDTH)
        attn_outs += [o.reshape(shape), lse.reshape(shape)]
    ng = pool_w.shape[0]
    pool_bd = jnp.einsum('gcd,gh->gchd', pool_w, jnp.eye(ng, dtype=pool_w.dtype)).reshape(pw, pw).astype(BF16)
    xacc, h2, eidx, gates, rank, counts = _mid(
        u, attn_outs, xf, pool_bd, pool_scale.reshape(1, pw), w_out.astype(BF16), gate1,
        g_ffn.reshape(1, D), shift2, scale2, gate2,
        ws_gate.astype(BF16), ws_up.astype(BF16), ws_down.astype(BF16),
        w_router.T.astype(BF16), router_bias.reshape(E, 1).astype(F32), S)
    bm = GMM_BLOCK
    nblk = N * TOP_K // bm + E
    nb_e = (counts[:, 0] + bm - 1) // bm
    bend = jnp.cumsum(nb_e)
    bstart = (bend - nb_e).astype(I32)
    nused = bend[-1:].astype(I32)
    dest_c = _dest(eidx, rank, (bstart * bm).astype(F32).reshape(E, 1))
    xs = _sc_dispatch(dest_c, h2, nblk * bm)
    ys = _gmm(bstart, nb_e.astype(I32), counts[:, 0], nused, xs, w_gate, w_up, w_down)
    return _sc_gather(dest_c, ys), gates.T, gate2, xacc


def kernel(x, c, w_ada, b_ada, g_mix, w_in, pool_w, pool_scale, w_out, g_ffn, w_router, router_bias,
           w_gate, w_up, w_down, ws_gate, ws_up, ws_down, g_final):
    B, S, D = x.shape
    depth = w_ada.shape[0]
    assert depth == 1, "the final residual is fused with the final norm, so exactly one layer is supported"
    assert S % (ATT_DILATIONS[-1] * ATT_BLOCK) == 0 and S % max(IN_TILE, MID_TILE, ATT_BLOCKS_PER_STEP * ATT_BLOCK) == 0
    xf = x.reshape(B * S, D)
    mod = _ada(c, w_ada[0], b_ada[0])
    yk, gates_t, gate2, xacc = _layer(
        xf, B, S, mod, g_mix[0], w_in[0], pool_w[0], pool_scale[0], w_out[0], g_ffn[0], w_router[0],
        router_bias[0], w_gate[0], w_up[0], w_down[0], ws_gate[0], ws_up[0], ws_down[0])
    out = _combine(yk, gates_t, gate2, xacc, g_final.reshape(1, D), S)
    return out.reshape(B, S, D)
```

```python
import functools

import jax
import jax.numpy as jnp
from jax import lax
from jax.experimental import pallas as pl
from jax.experimental.pallas import tpu as pltpu
from jax.experimental.pallas import tpu_sc as plsc

F32 = jnp.float32
BF16 = jnp.bfloat16
I32 = jnp.int32
U32 = jnp.uint32

LANES = 128
SINGLE_LOAD_STRIDE = 4
NORM_EPS = 1e-6
POOL_WINDOWS = (2, 4, 8, 16)
POOL_HALO = 16
ATT_DILATIONS = (1, 4, 16)
ATT_BLOCK = 128
ATT_HEADS_PER_GROUP = 4
ATT_HEAD_DIM = 64
ATT_GROUP_WIDTH = ATT_HEADS_PER_GROUP * ATT_HEAD_DIM
N_EXPERT_GROUPS = 8
TOPK_GROUPS = 4
TOP_K = 8
ROUTED_SCALE = 2.5

IN_TILE = 512
ATT_BLOCKS_PER_STEP = 16
MID_TILE = 512
ROUTE_TILE = 256
DEST_TILE = 1024
GMM_BLOCK = 128
GMM_RING = 16
GMM_EXPERTS_PER_STEP = 1
SC_CHUNK = 128

NEG_INF = float("-inf")


def _sigmoid(v):
    return 1.0 / (1.0 + jnp.exp(-v))


def _rms(v):
    return v * lax.rsqrt(jnp.mean(v * v, axis=-1, keepdims=True) + NORM_EPS)


def _pack_bf16_pairs(v):
    n = v.shape[1] // 2
    lo = lax.bitcast_convert_type(v[:, :n].astype(BF16).astype(F32), U32)
    hi = lax.bitcast_convert_type(v[:, n:].astype(BF16).astype(F32), U32)
    return (hi & jnp.uint32(0xFFFF0000)) | (lo >> 16)


def _unpack_bf16_pairs(p):
    lo = lax.bitcast_convert_type(p << 16, F32).astype(BF16)
    hi = lax.bitcast_convert_type(p & jnp.uint32(0xFFFF0000), F32).astype(BF16)
    return lo, hi


def _ada_kernel(c_ref, w_ref, b_ref, o_ref):
    c = c_ref[...]
    cs = c * _sigmoid(c)
    o_ref[...] = jnp.dot(cs, w_ref[...], preferred_element_type=F32,
                         precision=lax.Precision.HIGHEST) + b_ref[...]


def _ada(c, w_ada, b_ada):
    B, D = c.shape
    W = w_ada.shape[1]
    tn = 1024
    return pl.pallas_call(
        _ada_kernel,
        grid=(W // tn,),
        in_specs=[pl.BlockSpec((B, D), lambda j: (0, 0)),
                  pl.BlockSpec((D, tn), lambda j: (0, j)),
                  pl.BlockSpec((1, tn), lambda j: (0, j))],
        out_specs=pl.BlockSpec((B, tn), lambda j: (0, j)),
        out_shape=jax.ShapeDtypeStruct((B, W), F32),
        name="ada",
    )(c, w_ada, b_ada.reshape(1, W))


def _in_kernel(x_ref, g_ref, sh_ref, sc_ref, w_ref, pool_ref, q0_ref, q1_ref, q2_ref, scr_ref, tmp_ref):
    h = _rms(x_ref[...]) * g_ref[...]
    h = h * (1.0 + sc_ref[0]) + sh_ref[0]
    hb = h.astype(BF16)
    tm = x_ref.shape[0]
    pw = pool_ref.shape[1]
    gw = ATT_GROUP_WIDTH
    pool_ref[...] = jnp.dot(hb, w_ref[:, 0:pw], preferred_element_type=F32)
    for g, (out, d) in enumerate(zip((q0_ref, q1_ref, q2_ref), ATT_DILATIONS)):
        for sec in range(3):
            c0 = pw + sec * 3 * gw + g * gw
            res = jnp.dot(hb, w_ref[:, c0:c0 + gw], preferred_element_type=F32)
            if d == 1:
                out[:, sec * gw:(sec + 1) * gw] = res.astype(BF16)
            else:
                for c in range(gw // LANES):
                    scr_ref[c] = res[:, c * LANES:(c + 1) * LANES]
                    src, f1 = scr_ref.at[c], 1
                    if d > SINGLE_LOAD_STRIDE:
                        f1 = SINGLE_LOAD_STRIDE
                        for q in range(f1):
                            tmp_ref[c, q * (tm // f1):(q + 1) * (tm // f1), :] = scr_ref[c, pl.ds(q, tm // f1, stride=f1), :]
                        src = tmp_ref.at[c]
                    for r in range(d):
                        r_lo, r_hi = r % f1, r // f1
                        c1 = sec * gw + c * LANES
                        out[r, :, c1:c1 + LANES] = src[pl.ds(r_lo * (tm // f1) + r_hi, tm // d, stride=d // f1),
                                                       :].astype(BF16)


def _in_proj(xf, g_mix, shift1, scale1, w_in_b, S, pool_width):
    N, D = xf.shape
    tm = IN_TILE
    spt = S // tm
    vec = lambda i: (i // spt, 0, 0)
    row = lambda i: (i, 0)
    gw3 = 3 * ATT_GROUP_WIDTH
    B = N // S
    res_spec = lambda d: pl.BlockSpec((d, tm // d, gw3), lambda i: (i // spt, i % spt, 0))
    res_shape = lambda d: jax.ShapeDtypeStruct((B * d, S // d, gw3), BF16)
    outs = pl.pallas_call(
        _in_kernel,
        grid=(N // tm,),
        in_specs=[pl.BlockSpec((tm, D), row),
                  pl.BlockSpec((1, D), lambda i: (0, 0)),
                  pl.BlockSpec((1, 1, D), vec),
                  pl.BlockSpec((1, 1, D), vec),
                  pl.BlockSpec(w_in_b.shape, lambda i: (0, 0))],
        out_specs=[pl.BlockSpec((tm, pool_width), row), pl.BlockSpec((tm, gw3), row)]
                  + [res_spec(d) for d in ATT_DILATIONS[1:]],
        out_shape=[jax.ShapeDtypeStruct((N, pool_width), F32), jax.ShapeDtypeStruct((N, gw3), BF16)]
                  + [res_shape(d) for d in ATT_DILATIONS[1:]],
        scratch_shapes=[pltpu.VMEM((ATT_GROUP_WIDTH // LANES, tm, LANES), F32)] * 2,
        name="in_proj",
    )(xf, g_mix, shift1, scale1, w_in_b)
    return outs[0], [o.reshape(N, gw3) for o in outs[1:]]


def _attn_kernel(nbs, a_ref, halo_ref, o_ref, lse_ref, kv_ref, band_ref):
    i = pl.program_id(0)
    R = a_ref.shape[0] // ATT_BLOCK
    gw = ATT_GROUP_WIDTH
    blk = ATT_BLOCK
    nh = ATT_HEADS_PER_GROUP
    kv_ref[0:blk, :] = halo_ref[:, gw:3 * gw]
    kv_ref[blk:, :] = a_ref[:, gw:3 * gw]
    row = lax.broadcasted_iota(I32, (nh * blk, 2 * blk), 0) % blk
    col = lax.broadcasted_iota(I32, (nh * blk, 2 * blk), 1)
    band_ref[...] = jnp.where((col >= row) & (col <= row + blk), 0.0, NEG_INF)
    head_of_lane = lax.broadcasted_iota(I32, (blk, gw), 1) // ATT_HEAD_DIM
    nt = (((1,), (1,)), ((), ()))

    def body(jj, carry):
        r0 = pl.multiple_of(jj * blk, blk)
        qf = a_ref[pl.ds(r0, blk), 0:gw].astype(F32) * (ATT_HEAD_DIM ** -0.5)
        q4 = jnp.concatenate([jnp.where(head_of_lane == h, qf, 0.0) for h in range(nh)], axis=0).astype(BF16)
        kc = kv_ref[pl.ds(r0, 2 * blk), 0:gw]
        vc = kv_ref[pl.ds(r0, 2 * blk), gw:2 * gw]
        s = lax.dot_general(q4, kc, nt, preferred_element_type=F32) + band_ref[...]
        first = ((i * R + jj) % nbs) == 0
        s = jnp.where(col >= jnp.where(first, blk, 0), s, NEG_INF)
        m = jnp.max(s, axis=1, keepdims=True)
        p = jnp.exp(s - m)
        l = jnp.sum(p, axis=1, keepdims=True)
        o4 = jnp.dot(p.astype(BF16), vc, preferred_element_type=F32) / l
        lse4 = m + jnp.log(l)
        o = jnp.zeros((blk, gw), F32)
        lse = jnp.zeros((blk, gw), F32)
        for h in range(nh):
            hm = head_of_lane == h
            o = jnp.where(hm, o4[h * blk:(h + 1) * blk, :], o)
            lse = jnp.where(hm, lse4[h * blk:(h + 1) * blk, :], lse)
        o_ref[pl.ds(r0, blk), :] = o
        lse_ref[pl.ds(r0, blk), :] = lse
        return carry

    lax.fori_loop(0, R, body, 0, unroll=8)


def _attention(a, nbs):
    N = a.shape[0]
    R = ATT_BLOCKS_PER_STEP
    gw = ATT_GROUP_WIDTH
    tm = R * ATT_BLOCK
    return pl.pallas_call(
        functools.partial(_attn_kernel, nbs),
        grid=(N // tm,),
        in_specs=[pl.BlockSpec((tm, 3 * gw), lambda i: (i, 0)),
                  pl.BlockSpec((ATT_BLOCK, 3 * gw), lambda i: (jnp.maximum(i * R - 1, 0), 0))],
        out_specs=[pl.BlockSpec((tm, gw), lambda i: (i, 0))] * 2,
        out_shape=[jax.ShapeDtypeStruct((N, gw), F32)] * 2,
        scratch_shapes=[pltpu.VMEM((tm + ATT_BLOCK, 2 * gw), BF16),
                        pltpu.VMEM((ATT_HEADS_PER_GROUP * ATT_BLOCK, 2 * ATT_BLOCK), F32)],
        name="attn",
    )(a, a)


def _mid_kernel(spt, u_ref, uh_ref, o0_ref, l0_ref, o1_ref, l1_ref, o2_ref, l2_ref, x_ref,
                pbd_ref, psc_ref, wout_ref, gate1_ref, gffn_ref, sh2_ref, sc2_ref, gate2_ref,
                wsg_ref, wsu_ref, wsd_ref, wr_ref, bias_ref, before_ref,
                xacc_ref, h2_ref, e_ref, g_ref, r_ref, cnt_ref, til_ref, base_ref):
    i = pl.program_id(0)
    tm, pw = u_ref.shape
    si = i % spt
    u = u_ref[...]
    keep = jnp.full((POOL_HALO, pw), si, I32) > 0
    ext = jnp.concatenate([jnp.where(keep, uh_ref[...], 0.0), u], axis=0)
    lane_grp = lax.broadcasted_iota(I32, (tm, pw), 1) // (pw // len(POOL_WINDOWS))
    pooled = jnp.zeros((tm, pw), F32)
    s, w = ext, 1
    while w < POOL_HALO:
        s = s + pltpu.roll(s, w, axis=0)
        w *= 2
        if w in POOL_WINDOWS:
            pooled = jnp.where(lane_grp == POOL_WINDOWS.index(w), s[POOL_HALO:, :], pooled)
    win = jnp.zeros((tm, pw), I32)
    for g, w in enumerate(POOL_WINDOWS):
        win = jnp.where(lane_grp == g, w, win)
    pos = si * tm + lax.broadcasted_iota(I32, (tm, pw), 0)
    cnt = jnp.minimum(pos + 1, win).astype(F32)
    pooled = pooled / cnt - u
    pool_out = jnp.dot(pooled.astype(BF16), pbd_ref[...], preferred_element_type=F32) * psc_ref[...]
    def token_order(slot, ref):
        d, n, w = ref.shape
        for r in range(d):
            for c in range(w // LANES):
                til_ref[slot, c, pl.ds(r, n, stride=d), :] = ref[r, :, c * LANES:(c + 1) * LANES]
        return jnp.concatenate([til_ref[slot, c] for c in range(w // LANES)], axis=1)

    l0 = l0_ref[...]
    l1 = token_order(0, l1_ref)
    l2 = token_order(1, l2_ref)
    m = jnp.maximum(jnp.maximum(l0, l1), l2)
    w0 = jnp.exp(l0 - m)
    w1 = jnp.exp(l1 - m)
    w2 = jnp.exp(l2 - m)
    attn = (w0 * o0_ref[...] + w1 * token_order(2, o1_ref) + w2 * token_order(3, o2_ref)) / (w0 + w1 + w2)
    mixed = (jnp.dot(pool_out.astype(BF16), wout_ref[0:pw, :], preferred_element_type=F32)
             + jnp.dot(attn.astype(BF16), wout_ref[pw:, :], preferred_element_type=F32))
    x1 = x_ref[...] + gate1_ref[0] * mixed
    h2 = _rms(x1) * gffn_ref[...]
    h2 = h2 * (1.0 + sc2_ref[0]) + sh2_ref[0]
    h2_ref[...] = _pack_bf16_pairs(h2)
    hb = h2.astype(BF16)
    a = jnp.dot(hb, wsg_ref[...], preferred_element_type=F32)
    b = jnp.dot(hb, wsu_ref[...], preferred_element_type=F32)
    act = (a * _sigmoid(a)) * b
    shared = jnp.dot(act.astype(BF16), wsd_ref[...], preferred_element_type=F32)
    xacc_ref[...] = x1 + gate2_ref[0] * shared
    @pl.when(i == 0)
    def _():
        base_ref[...] = jnp.zeros_like(base_ref)

    for t0 in range(0, tm, ROUTE_TILE):
        idxs, gates, ranks = _route_tile(hb[t0:t0 + ROUTE_TILE, :], wr_ref, bias_ref, before_ref, base_ref)
        for k in range(TOP_K):
            e_ref[k:k + 1, t0:t0 + ROUTE_TILE] = idxs[k]
            g_ref[k:k + 1, t0:t0 + ROUTE_TILE] = gates[k]
            r_ref[k:k + 1, t0:t0 + ROUTE_TILE] = ranks[k]
    cnt_ref[...] = base_ref[...].astype(I32)


def _mid(u, attn_outs, xf, pool_bd, pool_scale, w_out_b, gate1, g_ffn, shift2, scale2, gate2,
         wsg_b, wsu_b, wsd_b, wr_t, bias_col, S):
    N, D = xf.shape
    E = wr_t.shape[0]
    tok = jnp.arange(ROUTE_TILE, dtype=I32)
    before = (tok[:, None] < tok[None, :]).astype(BF16)
    pw = u.shape[1]
    tm = MID_TILE
    spt = S // tm
    row = lambda i: (i, 0)
    vec = lambda i: (i // spt, 0, 0)
    full = lambda a: pl.BlockSpec(a.shape, lambda i: (0,) * a.ndim)
    hpt = tm // POOL_HALO
    in_specs = [pl.BlockSpec((tm, pw), row),
                pl.BlockSpec((POOL_HALO, pw), lambda i: (jnp.maximum(i * hpt - 1, 0), 0))]
    gw = ATT_GROUP_WIDTH
    in_specs += [pl.BlockSpec((tm, gw), row)] * 2
    for d in ATT_DILATIONS[1:]:
        in_specs += [pl.BlockSpec((d, tm // d, gw), lambda i: (i // spt, i % spt, 0))] * 2
    in_specs += [pl.BlockSpec((tm, D), row), full(pool_bd), full(pool_scale), full(w_out_b),
                 pl.BlockSpec((1, 1, D), vec), full(g_ffn), pl.BlockSpec((1, 1, D), vec),
                 pl.BlockSpec((1, 1, D), vec), pl.BlockSpec((1, 1, D), vec),
                 full(wsg_b), full(wsu_b), full(wsd_b), full(wr_t), full(bias_col), full(before)]
    col = lambda i: (0, i)
    return pl.pallas_call(
        functools.partial(_mid_kernel, spt),
        grid=(N // tm,),
        in_specs=in_specs,
        out_specs=[pl.BlockSpec((tm, D), row), pl.BlockSpec((tm, D // 2), row),
                   pl.BlockSpec((TOP_K, tm), col), pl.BlockSpec((TOP_K, tm), col), pl.BlockSpec((TOP_K, tm), col),
                   pl.BlockSpec((E, 1), lambda i: (0, 0))],
        out_shape=[jax.ShapeDtypeStruct((N, D), F32), jax.ShapeDtypeStruct((N, D // 2), U32),
                   jax.ShapeDtypeStruct((TOP_K, N), I32), jax.ShapeDtypeStruct((TOP_K, N), F32),
                   jax.ShapeDtypeStruct((TOP_K, N), I32), jax.ShapeDtypeStruct((E, 1), I32)],
        scratch_shapes=[pltpu.VMEM((4, gw // LANES, tm, LANES), F32), pltpu.VMEM((E, 1), F32)],
        compiler_params=pltpu.CompilerParams(dimension_semantics=("arbitrary",)),
        name="mid",
    )(u, u, *attn_outs, xf, pool_bd, pool_scale, w_out_b, gate1, g_ffn, shift2, scale2, gate2,
      wsg_b, wsu_b, wsd_b, wr_t, bias_col, before)


def _route_tile(hb, wr_ref, bias_ref, before_ref, base_ref):
    T = hb.shape[0]
    E = wr_ref.shape[0]
    gsz = E // N_EXPERT_GROUPS
    logits = lax.dot_general(wr_ref[...], hb, (((1,), (1,)), ((), ())), preferred_element_type=F32)
    scores = _sigmoid(logits)
    biased = scores + bias_ref[...]
    giota = lax.broadcasted_iota(I32, (gsz, T), 0)
    gscore = []
    for g in range(N_EXPERT_GROUPS):
        blk = biased[g * gsz:(g + 1) * gsz, :]
        m1 = jnp.max(blk, axis=0, keepdims=True)
        i1 = jnp.min(jnp.where(blk == m1, giota, gsz), axis=0, keepdims=True)
        m2 = jnp.max(jnp.where(giota == i1, NEG_INF, blk), axis=0, keepdims=True)
        gscore.append(m1 + m2)
    parts = []
    for g in range(N_EXPERT_GROUPS):
        beaten = jnp.zeros((1, T), I32)
        for o in range(N_EXPERT_GROUPS):
            if o == g:
                continue
            wins = (gscore[o] >= gscore[g]) if o < g else (gscore[o] > gscore[g])
            beaten = beaten + wins.astype(I32)
        keep = jnp.broadcast_to(beaten, (gsz, T)) < TOPK_GROUPS
        parts.append(jnp.where(keep, biased[g * gsz:(g + 1) * gsz, :], NEG_INF))
    cur = jnp.concatenate(parts, axis=0)
    eiota = lax.broadcasted_iota(I32, (E, T), 0)
    live = cur > NEG_INF
    idxs, gates = [], []
    for k in range(TOP_K):
        m = jnp.max(cur, axis=0, keepdims=True)
        idx = jnp.min(jnp.where(cur == m, eiota, E), axis=0, keepdims=True)
        oh = eiota == idx
        gates.append(jnp.sum(jnp.where(oh, scores, 0.0), axis=0, keepdims=True))
        idxs.append(idx)
        cur = jnp.where(oh, NEG_INF, cur)
    selm = jnp.where(live & (cur == NEG_INF), 1.0, 0.0)
    gsum = gates[0]
    for k in range(1, TOP_K):
        gsum = gsum + gates[k]
    gates = [gk / gsum * ROUTED_SCALE for gk in gates]
    tot = jnp.dot(selm.astype(BF16), before_ref[...], preferred_element_type=F32) + base_ref[...]
    ranks = [jnp.sum(jnp.where(eiota == idxs[k], tot, 0.0), axis=0, keepdims=True).astype(I32)
             for k in range(TOP_K)]
    base_ref[...] = base_ref[...] + jnp.sum(selm, axis=1, keepdims=True)
    return idxs, gates, ranks


def _dest_kernel(e_ref, r_ref, off_ref, dc_ref):
    E = off_ref.shape[0]
    T = e_ref.shape[1]
    C = dc_ref.shape[2]
    eiota = lax.broadcasted_iota(I32, (E, T), 0)
    off = off_ref[...]
    for k in range(TOP_K):
        start = jnp.sum(jnp.where(eiota == e_ref[k:k + 1, :], off, 0.0), axis=0, keepdims=True)
        d = start.astype(I32) + r_ref[k:k + 1, :]
        for c in range(T // C):
            dc_ref[c, k:k + 1, :] = d[:, c * C:(c + 1) * C]


def _dest(eidx, rank, offs_col):
    N = eidx.shape[1]
    E = offs_col.shape[0]
    T = DEST_TILE
    C = SC_CHUNK
    col = lambda i: (0, i)
    return pl.pallas_call(
        _dest_kernel,
        grid=(N // T,),
        in_specs=[pl.BlockSpec((TOP_K, T), col), pl.BlockSpec((TOP_K, T), col),
                  pl.BlockSpec((E, 1), lambda i: (0, 0))],
        out_specs=pl.BlockSpec((T // C, TOP_K, C), lambda i: (i, 0, 0)),
        out_shape=jax.ShapeDtypeStruct((N // C, TOP_K, C), I32),
        name="dest",
    )(eidx, rank, offs_col)


def _sc_dispatch(dest_c, h2p, P):
    N, W = h2p.shape
    C = dest_c.shape[2]
    info = plsc.get_sparse_core_info()
    nw = info.num_cores * info.num_subcores
    per_w = N // C // nw
    mesh = plsc.VectorSubcoreMesh(core_axis_name="c", subcore_axis_name="s")

    @functools.partial(
        pl.kernel, mesh=mesh,
        out_type=jax.ShapeDtypeStruct((P, W), h2p.dtype),
        scratch_types=[pltpu.VMEM((TOP_K, C), I32), pltpu.VMEM((C, W), h2p.dtype), pltpu.SemaphoreType.DMA],
        name="sc_dispatch",
    )
    def k(dest_hbm, h_hbm, xs_hbm, idx_v, rows_v, sem):
        wid = lax.axis_index("s") * info.num_cores + lax.axis_index("c")

        @pl.loop(0, per_w)
        def _(j):
            ch = wid * per_w + j
            pltpu.sync_copy(dest_hbm.at[ch], idx_v)
            pltpu.sync_copy(h_hbm.at[pl.ds(ch * C, C)], rows_v)
            copies = [pltpu.async_copy(rows_v, xs_hbm.at[idx_v.at[kk]], sem) for kk in range(TOP_K)]
            for cp in copies:
                cp.wait()

    return k(dest_c, h2p)


def _gmm_kernel(bstart_ref, nbe_ref, cnt_ref, nu_ref, wg_ref, wu_ref, wd_ref, xs_hbm, ys_hbm,
                wgb, wub, wdb, xbuf, ybuf, xsem, ysem):
    step = pl.program_id(0)
    last = pl.num_programs(0) - 1
    epg = wg_ref.shape[0]
    ring, bm = xbuf.shape[0], xbuf.shape[1]
    nblk = ys_hbm.shape[0] // bm
    nused = nu_ref[0]

    def x_copy(b, slot):
        return pltpu.make_async_copy(xs_hbm.at[pl.ds(pl.multiple_of(b * bm, bm), bm), :], xbuf.at[slot],
                                     xsem.at[slot])

    def y_copy(b, slot):
        return pltpu.make_async_copy(ybuf.at[slot], ys_hbm.at[pl.ds(pl.multiple_of(b * bm, bm), bm), :],
                                     ysem.at[slot])

    @pl.when(step == 0)
    def _():
        for j in range(ring - 1):
            @pl.when(j < nused)
            def _():
                x_copy(j, j).start()

    def run_expert(ee):
        e = step * epg + ee
        b0 = bstart_ref[e]
        nb = nbe_ref[e]

        @pl.when(nb > 0)
        def _():
            wgb[...] = wg_ref[ee].astype(BF16)
            wub[...] = wu_ref[ee].astype(BF16)
            wdb[...] = wd_ref[ee].astype(BF16)

            def prefetch(t):
                @pl.when(t < nused)
                def _():
                    x_copy(t, jnp.bitwise_and(t, ring - 1)).start()

            def process(b, n):
                slots = [jnp.bitwise_and(b + j, ring - 1) for j in range(n)]
                for j in range(n):
                    x_copy(b + j, slots[j]).wait()
                prefetch(b + ring - 1)
                for j in range(n):
                    @pl.when(b + j >= ring)
                    def _():
                        y_copy(b + j - ring, slots[j]).wait()
                rows = lax.broadcasted_iota(I32, (n * bm, 1), 0)
                valid = cnt_ref[e] - (b - b0) * bm
                xp = jnp.concatenate([xbuf[s] for s in slots], axis=0)
                xb = jnp.concatenate(_unpack_bf16_pairs(jnp.where(rows < valid, xp, jnp.uint32(0))), axis=1)
                a = jnp.dot(xb, wgb[...], preferred_element_type=F32)
                g = jnp.dot(xb, wub[...], preferred_element_type=F32)
                act = (a * _sigmoid(a)) * g
                yp = _pack_bf16_pairs(jnp.dot(act.astype(BF16), wdb[...], preferred_element_type=F32))
                for j in range(n):
                    ybuf[slots[j]] = yp[j * bm:(j + 1) * bm, :]
                    y_copy(b + j, slots[j]).start()
                for j in range(1, n):
                    prefetch(b + ring - 1 + j)

            quads = lax.shift_right_logical(nb, 2)
            lax.fori_loop(0, quads, lambda j, c: (process(b0 + 4 * j, 4), c)[1], 0)
            rest2 = jnp.bitwise_and(nb, 2)

            @pl.when(rest2 != 0)
            def _():
                process(b0 + 4 * quads, 2)

            @pl.when(jnp.bitwise_and(nb, 1) != 0)
            def _():
                process(b0 + 4 * quads + rest2, 1)

    for ee in range(epg):
        run_expert(ee)

    @pl.when(step == last)
    def _():
        for back in range(ring, 0, -1):
            @pl.when(nused >= back)
            def _():
                y_copy(nused - back, jnp.bitwise_and(nused - back, ring - 1)).wait()
        ybuf[0] = jnp.zeros(ybuf.shape[1:], ybuf.dtype)
        lax.fori_loop(nused, nblk, lambda b, c: (y_copy(b, 0).start(), c)[1], 0)
        lax.fori_loop(nused, nblk, lambda b, c: (y_copy(b, 0).wait(), c)[1], 0)


def _gmm(bstart, nb_e, counts, nused, xs, w_gate, w_up, w_down):
    P, W = xs.shape
    E, D, F = w_gate.shape
    bm = GMM_BLOCK
    epg = GMM_EXPERTS_PER_STEP
    wsel = lambda s, *_: (s, 0, 0)
    grid_spec = pltpu.PrefetchScalarGridSpec(
        num_scalar_prefetch=4,
        grid=(E // epg,),
        in_specs=[pl.BlockSpec((epg, D, F), wsel), pl.BlockSpec((epg, D, F), wsel), pl.BlockSpec((epg, F, D), wsel),
                  pl.BlockSpec(memory_space=pl.ANY)],
        out_specs=pl.BlockSpec(memory_space=pl.ANY),
        scratch_shapes=[pltpu.VMEM((D, F), BF16), pltpu.VMEM((D, F), BF16), pltpu.VMEM((F, D), BF16),
                        pltpu.VMEM((GMM_RING, bm, W), xs.dtype), pltpu.VMEM((GMM_RING, bm, W), xs.dtype),
                        pltpu.SemaphoreType.DMA((GMM_RING,)), pltpu.SemaphoreType.DMA((GMM_RING,))],
    )
    return pl.pallas_call(
        _gmm_kernel,
        grid_spec=grid_spec,
        out_shape=jax.ShapeDtypeStruct((P, W), xs.dtype),
        compiler_params=pltpu.CompilerParams(dimension_semantics=("arbitrary",)),
        name="gmm",
    )(bstart, nb_e, counts, nused, w_gate, w_up, w_down, xs)


def _sc_gather(dest_c, ys):
    nch, K, C = dest_c.shape
    W = ys.shape[1]
    H = C // 2
    info = plsc.get_sparse_core_info()
    nw = info.num_cores * info.num_subcores
    per_w = nch // nw
    nbuf = 3
    mesh = plsc.VectorSubcoreMesh(core_axis_name="c", subcore_axis_name="s")
    items = [(kk, hh) for kk in range(K) for hh in range(2)]

    @functools.partial(
        pl.kernel, mesh=mesh,
        out_type=jax.ShapeDtypeStruct((K, nch * C, W), ys.dtype),
        scratch_types=([pltpu.VMEM((K, C), I32)] + [pltpu.VMEM((H, W), ys.dtype)] * nbuf
                       + [pltpu.SemaphoreType.DMA] * (2 * nbuf)),
        name="sc_gather",
    )
    def k(dest_hbm, ys_hbm, yk_hbm, idx_v, *rest):
        bufs, gsem, wsem = rest[:nbuf], rest[nbuf:2 * nbuf], rest[2 * nbuf:]
        wid = lax.axis_index("s") * info.num_cores + lax.axis_index("c")

        @pl.loop(0, per_w)
        def _(j):
            ch = wid * per_w + j
            pltpu.sync_copy(dest_hbm.at[ch], idx_v)

            def gather(i):
                kk, hh = items[i]
                return pltpu.async_copy(ys_hbm.at[idx_v.at[kk, pl.ds(hh * H, H)]], bufs[i % nbuf], gsem[i % nbuf])

            def write(i):
                kk, hh = items[i]
                return pltpu.async_copy(bufs[i % nbuf], yk_hbm.at[kk, pl.ds(ch * C + hh * H, H)], wsem[i % nbuf])

            n = len(items)
            g = {0: gather(0), 1: gather(1)}
            w = {}
            for i in range(n):
                g[i].wait()
                w[i] = write(i)
                if i + 2 < n:
                    if i >= 1:
                        w.pop(i - 1).wait()
                    g[i + 2] = gather(i + 2)
            for i in sorted(w):
                w[i].wait()

    return k(dest_c, ys)


def _combine_kernel(yk_ref, gt_ref, gate2_ref, xacc_ref, gfin_ref, o_ref):
    gt = gt_ref[...]
    hi_mask = jnp.uint32(0xFFFF0000)
    acc_lo = acc_hi = None
    for k in range(TOP_K):
        p = yk_ref[k]
        g = gt[:, k:k + 1]
        lo = lax.bitcast_convert_type(p << 16, F32) * g
        hi = lax.bitcast_convert_type(p & hi_mask, F32) * g
        acc_lo = lo if k == 0 else acc_lo + lo
        acc_hi = hi if k == 0 else acc_hi + hi
    routed = jnp.concatenate([acc_lo, acc_hi], axis=1)
    x2 = xacc_ref[...] + gate2_ref[0] * routed
    o_ref[...] = _rms(x2) * gfin_ref[...]


def _combine(yk, gates_t, gate2, xacc, g_final, S):
    N, D = xacc.shape
    W = yk.shape[2]
    T = ROUTE_TILE
    spt = S // T
    return pl.pallas_call(
        _combine_kernel,
        grid=(N // T,),
        in_specs=[pl.BlockSpec((TOP_K, T, W), lambda i: (0, i, 0)),
                  pl.BlockSpec((T, TOP_K), lambda i: (i, 0)),
                  pl.BlockSpec((1, 1, D), lambda i: (i // spt, 0, 0)),
                  pl.BlockSpec((T, D), lambda i: (i, 0)),
                  pl.BlockSpec((1, D), lambda i: (0, 0))],
        out_specs=pl.BlockSpec((T, D), lambda i: (i, 0)),
        out_shape=jax.ShapeDtypeStruct((N, D), F32),
        name="combine",
    )(yk, gates_t, gate2, xacc, g_final)


def _layer(xf, B, S, mod, g_mix, w_in, pool_w, pool_scale, w_out, g_ffn, w_router, router_bias,
           w_gate, w_up, w_down, ws_gate, ws_up, ws_down):
    N, D = xf.shape
    E = w_router.shape[1]
    pw = pool_scale.shape[0]
    shift1, scale1, gate1, shift2, scale2, gate2 = [m.reshape(B, 1, D) for m in jnp.split(mod, 6, axis=-1)]
    u, qkv = _in_proj(xf, g_mix.reshape(1, D), shift1, scale1, w_in.astype(BF16), S, pw)
    attn_outs = []
    for a, d in zip(qkv, ATT_DILATIONS):
        o, lse = _attention(a, S // d // ATT_BLOCK)
        shape = (N, ATT_GROUP_WIDTH) if d == 1 else (B * d, S // d, ATT_GROUP_WIDTH)
        attn_outs += [o.reshape(shape), lse.reshape(shape)]
    ng = pool_w.shape[0]
    pool_bd = jnp.einsum('gcd,gh->gchd', pool_w, jnp.eye(ng, dtype=pool_w.dtype)).reshape(pw, pw).astype(BF16)
    xacc, h2, eidx, gates, rank, counts = _mid(
        u, attn_outs, xf, pool_bd, pool_scale.reshape(1, pw), w_out.astype(BF16), gate1,
        g_ffn.reshape(1, D), shift2, scale2, gate2,
        ws_gate.astype(BF16), ws_up.astype(BF16), ws_down.astype(BF16),
        w_router.T.astype(BF16), router_bias.reshape(E, 1).astype(F32), S)
    bm = GMM_BLOCK
    nblk = N * TOP_K // bm + E
    nb_e = (counts[:, 0] + bm - 1) // bm
    bend = jnp.cumsum(nb_e)
    bstart = (bend - nb_e).astype(I32)
    nused = bend[-1:].astype(I32)
    dest_c = _dest(eidx, rank, (bstart * bm).astype(F32).reshape(E, 1))
    xs = _sc_dispatch(dest_c, h2, nblk * bm)
    ys = _gmm(bstart, nb_e.astype(I32), counts[:, 0], nused, xs, w_gate, w_up, w_down)
    return _sc_gather(dest_c, ys), gates.T, gate2, xacc


def kernel(x, c, w_ada, b_ada, g_mix, w_in, pool_w, pool_scale, w_out, g_ffn, w_router, router_bias,
           w_gate, w_up, w_down, ws_gate, ws_up, ws_down, g_final):
    B, S, D = x.shape
    depth = w_ada.shape[0]
    assert depth == 1, "the final residual is fused with the final norm, so exactly one layer is supported"
    assert S % (ATT_DILATIONS[-1] * ATT_BLOCK) == 0 and S % max(IN_TILE, MID_TILE, ATT_BLOCKS_PER_STEP * ATT_BLOCK) == 0
    xf = x.reshape(B * S, D)
    mod = _ada(c, w_ada[0], b_ada[0])
    yk, gates_t, gate2, xacc = _layer(
        xf, B, S, mod, g_mix[0], w_in[0], pool_w[0], pool_scale[0], w_out[0], g_ffn[0], w_router[0],
        router_bias[0], w_gate[0], w_up[0], w_down[0], ws_gate[0], ws_up[0], ws_down[0])
    out = _combine(yk, gates_t, gate2, xacc, g_final.reshape(1, D), S)
    return out.reshape(B, S, D)
```

```python
import functools

import jax
import jax.numpy as jnp
from jax import lax
from jax.experimental import pallas as pl
from jax.experimental.pallas import tpu as pltpu
from jax.experimental.pallas import tpu_sc as plsc

F32 = jnp.float32
BF16 = jnp.bfloat16
I32 = jnp.int32
U32 = jnp.uint32

LANES = 128
SINGLE_LOAD_STRIDE = 4
NORM_EPS = 1e-6
POOL_WINDOWS = (2, 4, 8, 16)
POOL_HALO = 16
ATT_DILATIONS = (1, 4, 16)
ATT_BLOCK = 128
ATT_HEADS_PER_GROUP = 4
ATT_HEAD_DIM = 64
ATT_GROUP_WIDTH = ATT_HEADS_PER_GROUP * ATT_HEAD_DIM
N_EXPERT_GROUPS = 8
TOPK_GROUPS = 4
TOP_K = 8
ROUTED_SCALE = 2.5

IN_TILE = 1024
ATT_BLOCKS_PER_STEP = 16
MID_TILE = 512
ROUTE_TILE = 256
DEST_TILE = 1024
GMM_BLOCK = 128
GMM_RING = 16
GMM_EXPERTS_PER_STEP = 1
SC_CHUNK = 128

NEG_INF = float("-inf")


def _sigmoid(v):
    return 1.0 / (1.0 + jnp.exp(-v))


def _rms(v):
    return v * lax.rsqrt(jnp.mean(v * v, axis=-1, keepdims=True) + NORM_EPS)


def _pack_bf16_pairs(v):
    n = v.shape[1] // 2
    lo = lax.bitcast_convert_type(v[:, :n].astype(BF16).astype(F32), U32)
    hi = lax.bitcast_convert_type(v[:, n:].astype(BF16).astype(F32), U32)
    return (hi & jnp.uint32(0xFFFF0000)) | (lo >> 16)


def _unpack_bf16_pairs(p):
    lo = lax.bitcast_convert_type(p << 16, F32).astype(BF16)
    hi = lax.bitcast_convert_type(p & jnp.uint32(0xFFFF0000), F32).astype(BF16)
    return lo, hi


def _ada_kernel(c_ref, w_ref, b_ref, o_ref):
    c = c_ref[...]
    cs = c * _sigmoid(c)
    o_ref[...] = jnp.dot(cs, w_ref[...], preferred_element_type=F32,
                         precision=lax.Precision.HIGHEST) + b_ref[...]


def _ada(c, w_ada, b_ada):
    B, D = c.shape
    W = w_ada.shape[1]
    tn = 1024
    return pl.pallas_call(
        _ada_kernel,
        grid=(W // tn,),
        in_specs=[pl.BlockSpec((B, D), lambda j: (0, 0)),
                  pl.BlockSpec((D, tn), lambda j: (0, j)),
                  pl.BlockSpec((1, tn), lambda j: (0, j))],
        out_specs=pl.BlockSpec((B, tn), lambda j: (0, j)),
        out_shape=jax.ShapeDtypeStruct((B, W), F32),
        name="ada",
    )(c, w_ada, b_ada.reshape(1, W))


def _in_kernel(x_ref, g_ref, sh_ref, sc_ref, w_ref, pool_ref, q0_ref, q1_ref, q2_ref, scr_ref, tmp_ref):
    h = _rms(x_ref[...]) * g_ref[...]
    h = h * (1.0 + sc_ref[0]) + sh_ref[0]
    hb = h.astype(BF16)
    tm = x_ref.shape[0]
    pw = pool_ref.shape[1]
    gw = ATT_GROUP_WIDTH
    pool_ref[...] = jnp.dot(hb, w_ref[:, 0:pw], preferred_element_type=F32)
    for g, (out, d) in enumerate(zip((q0_ref, q1_ref, q2_ref), ATT_DILATIONS)):
        for sec in range(3):
            c0 = pw + sec * 3 * gw + g * gw
            res = jnp.dot(hb, w_ref[:, c0:c0 + gw], preferred_element_type=F32)
            if d == 1:
                out[:, sec * gw:(sec + 1) * gw] = res.astype(BF16)
            else:
                for c in range(gw // LANES):
                    scr_ref[c] = res[:, c * LANES:(c + 1) * LANES]
                    src, f1 = scr_ref.at[c], 1
                    if d > SINGLE_LOAD_STRIDE:
                        f1 = SINGLE_LOAD_STRIDE
                        for q in range(f1):
                            tmp_ref[c, q * (tm // f1):(q + 1) * (tm // f1), :] = scr_ref[c, pl.ds(q, tm // f1, stride=f1), :]
                        src = tmp_ref.at[c]
                    for r in range(d):
                        r_lo, r_hi = r % f1, r // f1
                        c1 = sec * gw + c * LANES
                        out[r, :, c1:c1 + LANES] = src[pl.ds(r_lo * (tm // f1) + r_hi, tm // d, stride=d // f1),
                                                       :].astype(BF16)


def _in_proj(xf, g_mix, shift1, scale1, w_in_b, S, pool_width):
    N, D = xf.shape
    tm = IN_TILE
    spt = S // tm
    vec = lambda i: (i // spt, 0, 0)
    row = lambda i: (i, 0)
    gw3 = 3 * ATT_GROUP_WIDTH
    B = N // S
    res_spec = lambda d: pl.BlockSpec((d, tm // d, gw3), lambda i: (i // spt, i % spt, 0))
    res_shape = lambda d: jax.ShapeDtypeStruct((B * d, S // d, gw3), BF16)
    outs = pl.pallas_call(
        _in_kernel,
        grid=(N // tm,),
        in_specs=[pl.BlockSpec((tm, D), row),
                  pl.BlockSpec((1, D), lambda i: (0, 0)),
                  pl.BlockSpec((1, 1, D), vec),
                  pl.BlockSpec((1, 1, D), vec),
                  pl.BlockSpec(w_in_b.shape, lambda i: (0, 0))],
        out_specs=[pl.BlockSpec((tm, pool_width), row), pl.BlockSpec((tm, gw3), row)]
                  + [res_spec(d) for d in ATT_DILATIONS[1:]],
        out_shape=[jax.ShapeDtypeStruct((N, pool_width), F32), jax.ShapeDtypeStruct((N, gw3), BF16)]
                  + [res_shape(d) for d in ATT_DILATIONS[1:]],
        scratch_shapes=[pltpu.VMEM((ATT_GROUP_WIDTH // LANES, tm, LANES), F32)] * 2,
        name="in_proj",
    )(xf, g_mix, shift1, scale1, w_in_b)
    return outs[0], [o.reshape(N, gw3) for o in outs[1:]]


def _attn_kernel(nbs, a_ref, halo_ref, o_ref, lse_ref, kv_ref, band_ref):
    i = pl.program_id(0)
    R = a_ref.shape[0] // ATT_BLOCK
    gw = ATT_GROUP_WIDTH
    blk = ATT_BLOCK
    nh = ATT_HEADS_PER_GROUP
    kv_ref[0:blk, :] = halo_ref[:, gw:3 * gw]
    kv_ref[blk:, :] = a_ref[:, gw:3 * gw]
    row = lax.broadcasted_iota(I32, (nh * blk, 2 * blk), 0) % blk
    col = lax.broadcasted_iota(I32, (nh * blk, 2 * blk), 1)
    band_ref[...] = jnp.where((col >= row) & (col <= row + blk), 0.0, NEG_INF)
    head_of_lane = lax.broadcasted_iota(I32, (blk, gw), 1) // ATT_HEAD_DIM
    nt = (((1,), (1,)), ((), ()))

    def body(jj, carry):
        r0 = pl.multiple_of(jj * blk, blk)
        qf = a_ref[pl.ds(r0, blk), 0:gw].astype(F32) * (ATT_HEAD_DIM ** -0.5)
        q4 = jnp.concatenate([jnp.where(head_of_lane == h, qf, 0.0) for h in range(nh)], axis=0).astype(BF16)
        kc = kv_ref[pl.ds(r0, 2 * blk), 0:gw]
        vc = kv_ref[pl.ds(r0, 2 * blk), gw:2 * gw]
        s = lax.dot_general(q4, kc, nt, preferred_element_type=F32) + band_ref[...]
        first = ((i * R + jj) % nbs) == 0
        s = jnp.where(col >= jnp.where(first, blk, 0), s, NEG_INF)
        m = jnp.max(s, axis=1, keepdims=True)
        p = jnp.exp(s - m)
        l = jnp.sum(p, axis=1, keepdims=True)
        o4 = jnp.dot(p.astype(BF16), vc, preferred_element_type=F32) / l
        lse4 = m + jnp.log(l)
        o = jnp.zeros((blk, gw), F32)
        lse = jnp.zeros((blk, gw), F32)
        for h in range(nh):
            hm = head_of_lane == h
            o = jnp.where(hm, o4[h * blk:(h + 1) * blk, :], o)
            lse = jnp.where(hm, lse4[h * blk:(h + 1) * blk, :], lse)
        o_ref[pl.ds(r0, blk), :] = o
        lse_ref[pl.ds(r0, blk), :] = lse
        return carry

    lax.fori_loop(0, R, body, 0, unroll=8)


def _attention(a, nbs):
    N = a.shape[0]
    R = ATT_BLOCKS_PER_STEP
    gw = ATT_GROUP_WIDTH
    tm = R * ATT_BLOCK
    return pl.pallas_call(
        functools.partial(_attn_kernel, nbs),
        grid=(N // tm,),
        in_specs=[pl.BlockSpec((tm, 3 * gw), lambda i: (i, 0)),
                  pl.BlockSpec((ATT_BLOCK, 3 * gw), lambda i: (jnp.maximum(i * R - 1, 0), 0))],
        out_specs=[pl.BlockSpec((tm, gw), lambda i: (i, 0))] * 2,
        out_shape=[jax.ShapeDtypeStruct((N, gw), F32)] * 2,
        scratch_shapes=[pltpu.VMEM((tm + ATT_BLOCK, 2 * gw), BF16),
                        pltpu.VMEM((ATT_HEADS_PER_GROUP * ATT_BLOCK, 2 * ATT_BLOCK), F32)],
        name="attn",
    )(a, a)


def _mid_kernel(spt, u_ref, uh_ref, o0_ref, l0_ref, o1_ref, l1_ref, o2_ref, l2_ref, x_ref,
                pbd_ref, psc_ref, wout_ref, gate1_ref, gffn_ref, sh2_ref, sc2_ref, gate2_ref,
                wsg_ref, wsu_ref, wsd_ref, wr_ref, bias_ref, before_ref,
                xacc_ref, h2_ref, e_ref, g_ref, r_ref, cnt_ref, til_ref, base_ref):
    i = pl.program_id(0)
    tm, pw = u_ref.shape
    si = i % spt
    u = u_ref[...]
    keep = jnp.full((POOL_HALO, pw), si, I32) > 0
    ext = jnp.concatenate([jnp.where(keep, uh_ref[...], 0.0), u], axis=0)
    lane_grp = lax.broadcasted_iota(I32, (tm, pw), 1) // (pw // len(POOL_WINDOWS))
    pooled = jnp.zeros((tm, pw), F32)
    s, w = ext, 1
    while w < POOL_HALO:
        s = s + pltpu.roll(s, w, axis=0)
        w *= 2
        if w in POOL_WINDOWS:
            pooled = jnp.where(lane_grp == POOL_WINDOWS.index(w), s[POOL_HALO:, :], pooled)
    win = jnp.zeros((tm, pw), I32)
    for g, w in enumerate(POOL_WINDOWS):
        win = jnp.where(lane_grp == g, w, win)
    pos = si * tm + lax.broadcasted_iota(I32, (tm, pw), 0)
    cnt = jnp.minimum(pos + 1, win).astype(F32)
    pooled = pooled / cnt - u
    pool_out = jnp.dot(pooled.astype(BF16), pbd_ref[...], preferred_element_type=F32) * psc_ref[...]
    def token_order(slot, ref):
        d, n, w = ref.shape
        for r in range(d):
            for c in range(w // LANES):
                til_ref[slot, c, pl.ds(r, n, stride=d), :] = ref[r, :, c * LANES:(c + 1) * LANES]
        return jnp.concatenate([til_ref[slot, c] for c in range(w // LANES)], axis=1)

    l0 = l0_ref[...]
    l1 = token_order(0, l1_ref)
    l2 = token_order(1, l2_ref)
    m = jnp.maximum(jnp.maximum(l0, l1), l2)
    w0 = jnp.exp(l0 - m)
    w1 = jnp.exp(l1 - m)
    w2 = jnp.exp(l2 - m)
    attn = (w0 * o0_ref[...] + w1 * token_order(2, o1_ref) + w2 * token_order(3, o2_ref)) / (w0 + w1 + w2)
    mixed = (jnp.dot(pool_out.astype(BF16), wout_ref[0:pw, :], preferred_element_type=F32)
             + jnp.dot(attn.astype(BF16), wout_ref[pw:, :], preferred_element_type=F32))
    x1 = x_ref[...] + gate1_ref[0] * mixed
    h2 = _rms(x1) * gffn_ref[...]
    h2 = h2 * (1.0 + sc2_ref[0]) + sh2_ref[0]
    h2_ref[...] = _pack_bf16_pairs(h2)
    hb = h2.astype(BF16)
    a = jnp.dot(hb, wsg_ref[...], preferred_element_type=F32)
    b = jnp.dot(hb, wsu_ref[...], preferred_element_type=F32)
    act = (a * _sigmoid(a)) * b
    shared = jnp.dot(act.astype(BF16), wsd_ref[...], preferred_element_type=F32)
    xacc_ref[...] = x1 + gate2_ref[0] * shared
    @pl.when(i == 0)
    def _():
        base_ref[...] = jnp.zeros_like(base_ref)

    for t0 in range(0, tm, ROUTE_TILE):
        idxs, gates, ranks = _route_tile(hb[t0:t0 + ROUTE_TILE, :], wr_ref, bias_ref, before_ref, base_ref)
        for k in range(TOP_K):
            e_ref[k:k + 1, t0:t0 + ROUTE_TILE] = idxs[k]
            g_ref[k:k + 1, t0:t0 + ROUTE_TILE] = gates[k]
            r_ref[k:k + 1, t0:t0 + ROUTE_TILE] = ranks[k]
    cnt_ref[...] = base_ref[...].astype(I32)


def _mid(u, attn_outs, xf, pool_bd, pool_scale, w_out_b, gate1, g_ffn, shift2, scale2, gate2,
         wsg_b, wsu_b, wsd_b, wr_t, bias_col, S):
    N, D = xf.shape
    E = wr_t.shape[0]
    tok = jnp.arange(ROUTE_TILE, dtype=I32)
    before = (tok[:, None] < tok[None, :]).astype(BF16)
    pw = u.shape[1]
    tm = MID_TILE
    spt = S // tm
    row = lambda i: (i, 0)
    vec = lambda i: (i // spt, 0, 0)
    full = lambda a: pl.BlockSpec(a.shape, lambda i: (0,) * a.ndim)
    hpt = tm // POOL_HALO
    in_specs = [pl.BlockSpec((tm, pw), row),
                pl.BlockSpec((POOL_HALO, pw), lambda i: (jnp.maximum(i * hpt - 1, 0), 0))]
    gw = ATT_GROUP_WIDTH
    in_specs += [pl.BlockSpec((tm, gw), row)] * 2
    for d in ATT_DILATIONS[1:]:
        in_specs += [pl.BlockSpec((d, tm // d, gw), lambda i: (i // spt, i % spt, 0))] * 2
    in_specs += [pl.BlockSpec((tm, D), row), full(pool_bd), full(pool_scale), full(w_out_b),
                 pl.BlockSpec((1, 1, D), vec), full(g_ffn), pl.BlockSpec((1, 1, D), vec),
                 pl.BlockSpec((1, 1, D), vec), pl.BlockSpec((1, 1, D), vec),
                 full(wsg_b), full(wsu_b), full(wsd_b), full(wr_t), full(bias_col), full(before)]
    col = lambda i: (0, i)
    return pl.pallas_call(
        functools.partial(_mid_kernel, spt),
        grid=(N // tm,),
        in_specs=in_specs,
        out_specs=[pl.BlockSpec((tm, D), row), pl.BlockSpec((tm, D // 2), row),
                   pl.BlockSpec((TOP_K, tm), col), pl.BlockSpec((TOP_K, tm), col), pl.BlockSpec((TOP_K, tm), col),
                   pl.BlockSpec((E, 1), lambda i: (0, 0))],
        out_shape=[jax.ShapeDtypeStruct((N, D), F32), jax.ShapeDtypeStruct((N, D // 2), U32),
                   jax.ShapeDtypeStruct((TOP_K, N), I32), jax.ShapeDtypeStruct((TOP_K, N), F32),
                   jax.ShapeDtypeStruct((TOP_K, N), I32), jax.ShapeDtypeStruct((E, 1), I32)],
        scratch_shapes=[pltpu.VMEM((4, gw // LANES, tm, LANES), F32), pltpu.VMEM((E, 1), F32)],
        compiler_params=pltpu.CompilerParams(dimension_semantics=("arbitrary",)),
        name="mid",
    )(u, u, *attn_outs, xf, pool_bd, pool_scale, w_out_b, gate1, g_ffn, shift2, scale2, gate2,
      wsg_b, wsu_b, wsd_b, wr_t, bias_col, before)


def _route_tile(hb, wr_ref, bias_ref, before_ref, base_ref):
    T = hb.shape[0]
    E = wr_ref.shape[0]
    gsz = E // N_EXPERT_GROUPS
    logits = lax.dot_general(wr_ref[...], hb, (((1,), (1,)), ((), ())), preferred_element_type=F32)
    scores = _sigmoid(logits)
    biased = scores + bias_ref[...]
    giota = lax.broadcasted_iota(I32, (gsz, T), 0)
    gscore = []
    for g in range(N_EXPERT_GROUPS):
        blk = biased[g * gsz:(g + 1) * gsz, :]
        m1 = jnp.max(blk, axis=0, keepdims=True)
        i1 = jnp.min(jnp.where(blk == m1, giota, gsz), axis=0, keepdims=True)
        m2 = jnp.max(jnp.where(giota == i1, NEG_INF, blk), axis=0, keepdims=True)
        gscore.append(m1 + m2)
    parts = []
    for g in range(N_EXPERT_GROUPS):
        beaten = jnp.zeros((1, T), I32)
        for o in range(N_EXPERT_GROUPS):
            if o == g:
                continue
            wins = (gscore[o] >= gscore[g]) if o < g else (gscore[o] > gscore[g])
            beaten = beaten + wins.astype(I32)
        keep = jnp.broadcast_to(beaten, (gsz, T)) < TOPK_GROUPS
        parts.append(jnp.where(keep, biased[g * gsz:(g + 1) * gsz, :], NEG_INF))
    cur = jnp.concatenate(parts, axis=0)
    eiota = lax.broadcasted_iota(I32, (E, T), 0)
    live = cur > NEG_INF
    idxs, gates = [], []
    for k in range(TOP_K):
        m = jnp.max(cur, axis=0, keepdims=True)
        idx = jnp.min(jnp.where(cur == m, eiota, E), axis=0, keepdims=True)
        oh = eiota == idx
        gates.append(jnp.sum(jnp.where(oh, scores, 0.0), axis=0, keepdims=True))
        idxs.append(idx)
        cur = jnp.where(oh, NEG_INF, cur)
    selm = jnp.where(live & (cur == NEG_INF), 1.0, 0.0)
    gsum = gates[0]
    for k in range(1, TOP_K):
        gsum = gsum + gates[k]
    gates = [gk / gsum * ROUTED_SCALE for gk in gates]
    tot = jnp.dot(selm.astype(BF16), before_ref[...], preferred_element_type=F32) + base_ref[...]
    ranks = [jnp.sum(jnp.where(eiota == idxs[k], tot, 0.0), axis=0, keepdims=True).astype(I32)
             for k in range(TOP_K)]
    base_ref[...] = base_ref[...] + jnp.sum(selm, axis=1, keepdims=True)
    return idxs, gates, ranks


def _dest_kernel(e_ref, r_ref, off_ref, dc_ref):
    E = off_ref.shape[0]
    T = e_ref.shape[1]
    C = dc_ref.shape[2]
    eiota = lax.broadcasted_iota(I32, (E, T), 0)
    off = off_ref[...]
    for k in range(TOP_K):
        start = jnp.sum(jnp.where(eiota == e_ref[k:k + 1, :], off, 0.0), axis=0, keepdims=True)
        d = start.astype(I32) + r_ref[k:k + 1, :]
        for c in range(T // C):
            dc_ref[c, k:k + 1, :] = d[:, c * C:(c + 1) * C]


def _dest(eidx, rank, offs_col):
    N = eidx.shape[1]
    E = offs_col.shape[0]
    T = DEST_TILE
    C = SC_CHUNK
    col = lambda i: (0, i)
    return pl.pallas_call(
        _dest_kernel,
        grid=(N // T,),
        in_specs=[pl.BlockSpec((TOP_K, T), col), pl.BlockSpec((TOP_K, T), col),
                  pl.BlockSpec((E, 1), lambda i: (0, 0))],
        out_specs=pl.BlockSpec((T // C, TOP_K, C), lambda i: (i, 0, 0)),
        out_shape=jax.ShapeDtypeStruct((N // C, TOP_K, C), I32),
        name="dest",
    )(eidx, rank, offs_col)


def _sc_dispatch(dest_c, h2p, P):
    N, W = h2p.shape
    C = dest_c.shape[2]
    info = plsc.get_sparse_core_info()
    nw = info.num_cores * info.num_subcores
    per_w = N // C // nw
    mesh = plsc.VectorSubcoreMesh(core_axis_name="c", subcore_axis_name="s")

    @functools.partial(
        pl.kernel, mesh=mesh,
        out_type=jax.ShapeDtypeStruct((P, W), h2p.dtype),
        scratch_types=[pltpu.VMEM((TOP_K, C), I32), pltpu.VMEM((C, W), h2p.dtype), pltpu.SemaphoreType.DMA],
        name="sc_dispatch",
    )
    def k(dest_hbm, h_hbm, xs_hbm, idx_v, rows_v, sem):
        wid = lax.axis_index("s") * info.num_cores + lax.axis_index("c")

        @pl.loop(0, per_w)
        def _(j):
            ch = wid * per_w + j
            pltpu.sync_copy(dest_hbm.at[ch], idx_v)
            pltpu.sync_copy(h_hbm.at[pl.ds(ch * C, C)], rows_v)
            copies = [pltpu.async_copy(rows_v, xs_hbm.at[idx_v.at[kk]], sem) for kk in range(TOP_K)]
            for cp in copies:
                cp.wait()

    return k(dest_c, h2p)


def _gmm_kernel(bstart_ref, nbe_ref, cnt_ref, nu_ref, wg_ref, wu_ref, wd_ref, xs_hbm, ys_hbm,
                xbuf, ybuf, xsem, ysem):
    step = pl.program_id(0)
    last = pl.num_programs(0) - 1
    epg = wg_ref.shape[0]
    ring, bm = xbuf.shape[0], xbuf.shape[1]
    nblk = ys_hbm.shape[0] // bm
    nused = nu_ref[0]

    def x_copy(b, slot):
        return pltpu.make_async_copy(xs_hbm.at[pl.ds(pl.multiple_of(b * bm, bm), bm), :], xbuf.at[slot],
                                     xsem.at[slot])

    def y_copy(b, slot):
        return pltpu.make_async_copy(ybuf.at[slot], ys_hbm.at[pl.ds(pl.multiple_of(b * bm, bm), bm), :],
                                     ysem.at[slot])

    @pl.when(step == 0)
    def _():
        for j in range(ring - 1):
            @pl.when(j < nused)
            def _():
                x_copy(j, j).start()

    def run_expert(ee):
        e = step * epg + ee
        b0 = bstart_ref[e]
        nb = nbe_ref[e]

        @pl.when(nb > 0)
        def _():
            def prefetch(t):
                @pl.when(t < nused)
                def _():
                    x_copy(t, jnp.bitwise_and(t, ring - 1)).start()

            def process(b, n):
                slots = [jnp.bitwise_and(b + j, ring - 1) for j in range(n)]
                for j in range(n):
                    x_copy(b + j, slots[j]).wait()
                prefetch(b + ring - 1)
                for j in range(n):
                    @pl.when(b + j >= ring)
                    def _():
                        y_copy(b + j - ring, slots[j]).wait()
                rows = lax.broadcasted_iota(I32, (n * bm, 1), 0)
                valid = cnt_ref[e] - (b - b0) * bm
                xp = jnp.concatenate([xbuf[s] for s in slots], axis=0)
                xb = jnp.concatenate(_unpack_bf16_pairs(jnp.where(rows < valid, xp, jnp.uint32(0))), axis=1)
                a = jnp.dot(xb, wg_ref[ee].astype(BF16), preferred_element_type=F32)
                g = jnp.dot(xb, wu_ref[ee].astype(BF16), preferred_element_type=F32)
                act = (a * _sigmoid(a)) * g
                yp = _pack_bf16_pairs(jnp.dot(act.astype(BF16), wd_ref[ee].astype(BF16),
                                              preferred_element_type=F32))
                for j in range(n):
                    ybuf[slots[j]] = yp[j * bm:(j + 1) * bm, :]
                    y_copy(b + j, slots[j]).start()
                for j in range(1, n):
                    prefetch(b + ring - 1 + j)

            quads = lax.shift_right_logical(nb, 2)
            lax.fori_loop(0, quads, lambda j, c: (process(b0 + 4 * j, 4), c)[1], 0)
            rest2 = jnp.bitwise_and(nb, 2)

            @pl.when(rest2 != 0)
            def _():
                process(b0 + 4 * quads, 2)

            @pl.when(jnp.bitwise_and(nb, 1) != 0)
            def _():
                process(b0 + 4 * quads + rest2, 1)

    for ee in range(epg):
        run_expert(ee)

    @pl.when(step == last)
    def _():
        for back in range(ring, 0, -1):
            @pl.when(nused >= back)
            def _():
                y_copy(nused - back, jnp.bitwise_and(nused - back, ring - 1)).wait()
        ybuf[0] = jnp.zeros(ybuf.shape[1:], ybuf.dtype)
        lax.fori_loop(nused, nblk, lambda b, c: (y_copy(b, 0).start(), c)[1], 0)
        lax.fori_loop(nused, nblk, lambda b, c: (y_copy(b, 0).wait(), c)[1], 0)


def _gmm(bstart, nb_e, counts, nused, xs, w_gate, w_up, w_down):
    P, W = xs.shape
    E, D, F = w_gate.shape
    bm = GMM_BLOCK
    epg = GMM_EXPERTS_PER_STEP
    wsel = lambda s, *_: (s, 0, 0)
    grid_spec = pltpu.PrefetchScalarGridSpec(
        num_scalar_prefetch=4,
        grid=(E // epg,),
        in_specs=[pl.BlockSpec((epg, D, F), wsel), pl.BlockSpec((epg, D, F), wsel), pl.BlockSpec((epg, F, D), wsel),
                  pl.BlockSpec(memory_space=pl.ANY)],
        out_specs=pl.BlockSpec(memory_space=pl.ANY),
        scratch_shapes=[pltpu.VMEM((GMM_RING, bm, W), xs.dtype), pltpu.VMEM((GMM_RING, bm, W), xs.dtype),
                        pltpu.SemaphoreType.DMA((GMM_RING,)), pltpu.SemaphoreType.DMA((GMM_RING,))],
    )
    return pl.pallas_call(
        _gmm_kernel,
        grid_spec=grid_spec,
        out_shape=jax.ShapeDtypeStruct((P, W), xs.dtype),
        compiler_params=pltpu.CompilerParams(dimension_semantics=("arbitrary",)),
        name="gmm",
    )(bstart, nb_e, counts, nused, w_gate, w_up, w_down, xs)


def _sc_gather(dest_c, ys):
    nch, K, C = dest_c.shape
    W = ys.shape[1]
    H = C // 2
    info = plsc.get_sparse_core_info()
    nw = info.num_cores * info.num_subcores
    per_w = nch // nw
    nbuf = 3
    mesh = plsc.VectorSubcoreMesh(core_axis_name="c", subcore_axis_name="s")
    items = [(kk, hh) for kk in range(K) for hh in range(2)]

    @functools.partial(
        pl.kernel, mesh=mesh,
        out_type=jax.ShapeDtypeStruct((K, nch * C, W), ys.dtype),
        scratch_types=([pltpu.VMEM((K, C), I32)] + [pltpu.VMEM((H, W), ys.dtype)] * nbuf
                       + [pltpu.SemaphoreType.DMA] * (2 * nbuf)),
        name="sc_gather",
    )
    def k(dest_hbm, ys_hbm, yk_hbm, idx_v, *rest):
        bufs, gsem, wsem = rest[:nbuf], rest[nbuf:2 * nbuf], rest[2 * nbuf:]
        wid = lax.axis_index("s") * info.num_cores + lax.axis_index("c")

        @pl.loop(0, per_w)
        def _(j):
            ch = wid * per_w + j
            pltpu.sync_copy(dest_hbm.at[ch], idx_v)

            def gather(i):
                kk, hh = items[i]
                return pltpu.async_copy(ys_hbm.at[idx_v.at[kk, pl.ds(hh * H, H)]], bufs[i % nbuf], gsem[i % nbuf])

            def write(i):
                kk, hh = items[i]
                return pltpu.async_copy(bufs[i % nbuf], yk_hbm.at[kk, pl.ds(ch * C + hh * H, H)], wsem[i % nbuf])

            n = len(items)
            g = {0: gather(0), 1: gather(1)}
            w = {}
            for i in range(n):
                g[i].wait()
                w[i] = write(i)
                if i + 2 < n:
                    if i >= 1:
                        w.pop(i - 1).wait()
                    g[i + 2] = gather(i + 2)
            for i in sorted(w):
                w[i].wait()

    return k(dest_c, ys)


def _combine_kernel(yk_ref, gt_ref, gate2_ref, xacc_ref, gfin_ref, o_ref):
    gt = gt_ref[...]
    hi_mask = jnp.uint32(0xFFFF0000)
    acc_lo = acc_hi = None
    for k in range(TOP_K):
        p = yk_ref[k]
        g = gt[:, k:k + 1]
        lo = lax.bitcast_convert_type(p << 16, F32) * g
        hi = lax.bitcast_convert_type(p & hi_mask, F32) * g
        acc_lo = lo if k == 0 else acc_lo + lo
        acc_hi = hi if k == 0 else acc_hi + hi
    routed = jnp.concatenate([acc_lo, acc_hi], axis=1)
    x2 = xacc_ref[...] + gate2_ref[0] * routed
    o_ref[...] = _rms(x2) * gfin_ref[...]


def _combine(yk, gates_t, gate2, xacc, g_final, S):
    N, D = xacc.shape
    W = yk.shape[2]
    T = ROUTE_TILE
    spt = S // T
    return pl.pallas_call(
        _combine_kernel,
        grid=(N // T,),
        in_specs=[pl.BlockSpec((TOP_K, T, W), lambda i: (0, i, 0)),
                  pl.BlockSpec((T, TOP_K), lambda i: (i, 0)),
                  pl.BlockSpec((1, 1, D), lambda i: (i // spt, 0, 0)),
                  pl.BlockSpec((T, D), lambda i: (i, 0)),
                  pl.BlockSpec((1, D), lambda i: (0, 0))],
        out_specs=pl.BlockSpec((T, D), lambda i: (i, 0)),
        out_shape=jax.ShapeDtypeStruct((N, D), F32),
        name="combine",
    )(yk, gates_t, gate2, xacc, g_final)


def _layer(xf, B, S, mod, g_mix, w_in, pool_w, pool_scale, w_out, g_ffn, w_router, router_bias,
           w_gate, w_up, w_down, ws_gate, ws_up, ws_down):
    N, D = xf.shape
    E = w_router.shape[1]
    pw = pool_scale.shape[0]
    shift1, scale1, gate1, shift2, scale2, gate2 = [m.reshape(B, 1, D) for m in jnp.split(mod, 6, axis=-1)]
    u, qkv = _in_proj(xf, g_mix.reshape(1, D), shift1, scale1, w_in.astype(BF16), S, pw)
    attn_outs = []
    for a, d in zip(qkv, ATT_DILATIONS):
        o, lse = _attention(a, S // d // ATT_BLOCK)
        shape = (N, ATT_GROUP_WIDTH) if d == 1 else (B * d, S // d, ATT_GROUP_WIDTH)
        attn_outs += [o.reshape(shape), lse.reshape(shape)]
    ng = pool_w.shape[0]
    pool_bd = jnp.einsum('gcd,gh->gchd', pool_w, jnp.eye(ng, dtype=pool_w.dtype)).reshape(pw, pw).astype(BF16)
    xacc, h2, eidx, gates, rank, counts = _mid(
        u, attn_outs, xf, pool_bd, pool_scale.reshape(1, pw), w_out.astype(BF16), gate1,
        g_ffn.reshape(1, D), shift2, scale2, gate2,
        ws_gate.astype(BF16), ws_up.astype(BF16), ws_down.astype(BF16),
        w_router.T.astype(BF16), router_bias.reshape(E, 1).astype(F32), S)
    bm = GMM_BLOCK
    nblk = N * TOP_K // bm + E
    nb_e = (counts[:, 0] + bm - 1) // bm
    bend = jnp.cumsum(nb_e)
    bstart = (bend - nb_e).astype(I32)
    nused = bend[-1:].astype(I32)
    dest_c = _dest(eidx, rank, (bstart * bm).astype(F32).reshape(E, 1))
    xs = _sc_dispatch(dest_c, h2, nblk * bm)
    ys = _gmm(bstart, nb_e.astype(I32), counts[:, 0], nused, xs, w_gate, w_up, w_down)
    return _sc_gather(dest_c, ys), gates.T, gate2, xacc


def kernel(x, c, w_ada, b_ada, g_mix, w_in, pool_w, pool_scale, w_out, g_ffn, w_router, router_bias,
           w_gate, w_up, w_down, ws_gate, ws_up, ws_down, g_final):
    B, S, D = x.shape
    depth = w_ada.shape[0]
    assert depth == 1, "the final residual is fused with the final norm, so exactly one layer is supported"
    assert S % (ATT_DILATIONS[-1] * ATT_BLOCK) == 0 and S % max(IN_TILE, MID_TILE, ATT_BLOCKS_PER_STEP * ATT_BLOCK) == 0
    xf = x.reshape(B * S, D)
    mod = _ada(c, w_ada[0], b_ada[0])
    yk, gates_t, gate2, xacc = _layer(
        xf, B, S, mod, g_mix[0], w_in[0], pool_w[0], pool_scale[0], w_out[0], g_ffn[0], w_router[0],
        router_bias[0], w_gate[0], w_up[0], w_down[0], ws_gate[0], ws_up[0], ws_down[0])
    out = _combine(yk, gates_t, gate2, xacc, g_final.reshape(1, D), S)
    return out.reshape(B, S, D)
```

```python
import functools

import jax
import jax.numpy as jnp
from jax import lax
from jax.experimental import pallas as pl
from jax.experimental.pallas import tpu as pltpu
from jax.experimental.pallas import tpu_sc as plsc

F32 = jnp.float32
BF16 = jnp.bfloat16
I32 = jnp.int32
U32 = jnp.uint32

LANES = 128
SINGLE_LOAD_STRIDE = 4
NORM_EPS = 1e-6
POOL_WINDOWS = (2, 4, 8, 16)
POOL_HALO = 16
ATT_DILATIONS = (1, 4, 16)
ATT_BLOCK = 128
ATT_HEADS_PER_GROUP = 4
ATT_HEAD_DIM = 64
ATT_GROUP_WIDTH = ATT_HEADS_PER_GROUP * ATT_HEAD_DIM
N_EXPERT_GROUPS = 8
TOPK_GROUPS = 4
TOP_K = 8
ROUTED_SCALE = 2.5

IN_TILE = 1024
ATT_BLOCKS_PER_STEP = 16
ATT_UNROLL = 8
MID_TILE = 512
ROUTE_TILE = 256
COMBINE_TILE = 512
DEST_TILE = 1024
GMM_BLOCK = 128
GMM_RING = 16
GMM_EXPERTS_PER_STEP = 1
SC_CHUNK = 128

NEG_INF = float("-inf")


def _sigmoid(v):
    return 1.0 / (1.0 + jnp.exp(-v))


def _rms(v):
    return v * lax.rsqrt(jnp.mean(v * v, axis=-1, keepdims=True) + NORM_EPS)


def _pack_bf16_pairs(v):
    n = v.shape[1] // 2
    lo = lax.bitcast_convert_type(v[:, :n].astype(BF16).astype(F32), U32)
    hi = lax.bitcast_convert_type(v[:, n:].astype(BF16).astype(F32), U32)
    return (hi & jnp.uint32(0xFFFF0000)) | (lo >> 16)


def _unpack_bf16_pairs(p):
    lo = lax.bitcast_convert_type(p << 16, F32).astype(BF16)
    hi = lax.bitcast_convert_type(p & jnp.uint32(0xFFFF0000), F32).astype(BF16)
    return lo, hi


def _ada_kernel(c_ref, w_ref, b_ref, o_ref):
    c = c_ref[...]
    cs = c * _sigmoid(c)
    o_ref[...] = jnp.dot(cs, w_ref[...], preferred_element_type=F32,
                         precision=lax.Precision.HIGHEST) + b_ref[...]


def _ada(c, w_ada, b_ada):
    B, D = c.shape
    W = w_ada.shape[1]
    tn = 1024
    return pl.pallas_call(
        _ada_kernel,
        grid=(W // tn,),
        in_specs=[pl.BlockSpec((B, D), lambda j: (0, 0)),
                  pl.BlockSpec((D, tn), lambda j: (0, j)),
                  pl.BlockSpec((1, tn), lambda j: (0, j))],
        out_specs=pl.BlockSpec((B, tn), lambda j: (0, j)),
        out_shape=jax.ShapeDtypeStruct((B, W), F32),
        name="ada",
    )(c, w_ada, b_ada.reshape(1, W))


def _in_kernel(x_ref, g_ref, sh_ref, sc_ref, w_ref, pool_ref, q0_ref, q1_ref, q2_ref, scr_ref, tmp_ref):
    h = _rms(x_ref[...]) * g_ref[...]
    h = h * (1.0 + sc_ref[0]) + sh_ref[0]
    hb = h.astype(BF16)
    tm = x_ref.shape[0]
    pw = pool_ref.shape[1]
    gw = ATT_GROUP_WIDTH
    pool_ref[...] = jnp.dot(hb, w_ref[:, 0:pw], preferred_element_type=F32)
    for g, (out, d) in enumerate(zip((q0_ref, q1_ref, q2_ref), ATT_DILATIONS)):
        for sec in range(3):
            c0 = pw + sec * 3 * gw + g * gw
            res = jnp.dot(hb, w_ref[:, c0:c0 + gw], preferred_element_type=F32)
            if d == 1:
                out[:, sec * gw:(sec + 1) * gw] = res.astype(BF16)
            else:
                for c in range(gw // LANES):
                    scr_ref[c] = res[:, c * LANES:(c + 1) * LANES]
                    src, f1 = scr_ref.at[c], 1
                    if d > SINGLE_LOAD_STRIDE:
                        f1 = SINGLE_LOAD_STRIDE
                        for q in range(f1):
                            tmp_ref[c, q * (tm // f1):(q + 1) * (tm // f1), :] = scr_ref[c, pl.ds(q, tm // f1, stride=f1), :]
                        src = tmp_ref.at[c]
                    for r in range(d):
                        r_lo, r_hi = r % f1, r // f1
                        c1 = sec * gw + c * LANES
                        out[r, :, c1:c1 + LANES] = src[pl.ds(r_lo * (tm // f1) + r_hi, tm // d, stride=d // f1),
                                                       :].astype(BF16)


def _in_proj(xf, g_mix, shift1, scale1, w_in_b, S, pool_width):
    N, D = xf.shape
    tm = IN_TILE
    spt = S // tm
    vec = lambda i: (i // spt, 0, 0)
    row = lambda i: (i, 0)
    gw3 = 3 * ATT_GROUP_WIDTH
    B = N // S
    res_spec = lambda d: pl.BlockSpec((d, tm // d, gw3), lambda i: (i // spt, i % spt, 0))
    res_shape = lambda d: jax.ShapeDtypeStruct((B * d, S // d, gw3), BF16)
    outs = pl.pallas_call(
        _in_kernel,
        grid=(N // tm,),
        in_specs=[pl.BlockSpec((tm, D), row),
                  pl.BlockSpec((1, D), lambda i: (0, 0)),
                  pl.BlockSpec((1, 1, D), vec),
                  pl.BlockSpec((1, 1, D), vec),
                  pl.BlockSpec(w_in_b.shape, lambda i: (0, 0))],
        out_specs=[pl.BlockSpec((tm, pool_width), row), pl.BlockSpec((tm, gw3), row)]
                  + [res_spec(d) for d in ATT_DILATIONS[1:]],
        out_shape=[jax.ShapeDtypeStruct((N, pool_width), F32), jax.ShapeDtypeStruct((N, gw3), BF16)]
                  + [res_shape(d) for d in ATT_DILATIONS[1:]],
        scratch_shapes=[pltpu.VMEM((ATT_GROUP_WIDTH // LANES, tm, LANES), F32)] * 2,
        name="in_proj",
    )(xf, g_mix, shift1, scale1, w_in_b)
    return outs[0], [o.reshape(N, gw3) for o in outs[1:]]


def _attn_kernel(nbs, a_ref, halo_ref, o_ref, lse_ref, kv_ref, band_ref):
    i = pl.program_id(0)
    R = a_ref.shape[0] // ATT_BLOCK
    gw = ATT_GROUP_WIDTH
    blk = ATT_BLOCK
    nh = ATT_HEADS_PER_GROUP
    kv_ref[0:blk, :] = halo_ref[:, gw:3 * gw]
    kv_ref[blk:, :] = a_ref[:, gw:3 * gw]
    row = lax.broadcasted_iota(I32, (nh * blk, 2 * blk), 0) % blk
    col = lax.broadcasted_iota(I32, (nh * blk, 2 * blk), 1)
    in_band = (col >= row) & (col <= row + blk)
    band_ref[0] = jnp.where(in_band, 0.0, NEG_INF)
    band_ref[1] = jnp.where(in_band & (col >= blk), 0.0, NEG_INF)
    head_of_lane = lax.broadcasted_iota(I32, (blk, gw), 1) // ATT_HEAD_DIM
    nt = (((1,), (1,)), ((), ()))

    def one_block(jj, start):
        r0 = pl.multiple_of(jj * blk, blk)
        qf = a_ref[pl.ds(r0, blk), 0:gw].astype(F32) * (ATT_HEAD_DIM ** -0.5)
        q4 = jnp.concatenate([jnp.where(head_of_lane == h, qf, 0.0) for h in range(nh)], axis=0).astype(BF16)
        kc = kv_ref[pl.ds(r0, 2 * blk), 0:gw]
        vc = kv_ref[pl.ds(r0, 2 * blk), gw:2 * gw]
        s = lax.dot_general(q4, kc, nt, preferred_element_type=F32) + band_ref[1 if start is True else 0]
        if start is not None and start is not True:
            s = jnp.where(col >= jnp.where(start, blk, 0), s, NEG_INF)
        m = jnp.max(s, axis=1, keepdims=True)
        p = jnp.exp(s - m)
        l = jnp.sum(p, axis=1, keepdims=True)
        o4 = jnp.dot(p.astype(BF16), vc, preferred_element_type=F32) / l
        lse4 = m + jnp.log(l)
        o = jnp.zeros((blk, gw), F32)
        lse = jnp.zeros((blk, gw), F32)
        for h in range(nh):
            hm = head_of_lane == h
            o = jnp.where(hm, o4[h * blk:(h + 1) * blk, :], o)
            lse = jnp.where(hm, lse4[h * blk:(h + 1) * blk, :], lse)
        o_ref[pl.ds(r0, blk), :] = o
        lse_ref[pl.ds(r0, blk), :] = lse

    U = ATT_UNROLL
    assert U % nbs == 0 or nbs % U == 0

    def body(it, carry):
        for j in range(U):
            if U % nbs == 0:
                start = True if j % nbs == 0 else None
            else:
                start = (((i * R + it * U) % nbs) == 0) if j == 0 else None
            one_block(it * U + j, start)
        return carry

    lax.fori_loop(0, R // U, body, 0)


def _attention(a, nbs):
    N = a.shape[0]
    R = ATT_BLOCKS_PER_STEP
    gw = ATT_GROUP_WIDTH
    tm = R * ATT_BLOCK
    return pl.pallas_call(
        functools.partial(_attn_kernel, nbs),
        grid=(N // tm,),
        in_specs=[pl.BlockSpec((tm, 3 * gw), lambda i: (i, 0)),
                  pl.BlockSpec((ATT_BLOCK, 3 * gw), lambda i: (jnp.maximum(i * R - 1, 0), 0))],
        out_specs=[pl.BlockSpec((tm, gw), lambda i: (i, 0))] * 2,
        out_shape=[jax.ShapeDtypeStruct((N, gw), F32)] * 2,
        scratch_shapes=[pltpu.VMEM((tm + ATT_BLOCK, 2 * gw), BF16),
                        pltpu.VMEM((2, ATT_HEADS_PER_GROUP * ATT_BLOCK, 2 * ATT_BLOCK), F32)],
        name="attn",
    )(a, a)


def _mid_kernel(spt, u_ref, uh_ref, o0_ref, l0_ref, o1_ref, l1_ref, o2_ref, l2_ref, x_ref,
                pbd_ref, psc_ref, wout_ref, gate1_ref, gffn_ref, sh2_ref, sc2_ref, gate2_ref,
                wsg_ref, wsu_ref, wsd_ref, wr_ref, bias_ref, before_ref,
                xacc_ref, h2_ref, e_ref, g_ref, r_ref, cnt_ref, til_ref, base_ref):
    i = pl.program_id(0)
    tm, pw = u_ref.shape
    si = i % spt
    u = u_ref[...]
    keep = jnp.full((POOL_HALO, pw), si, I32) > 0
    ext = jnp.concatenate([jnp.where(keep, uh_ref[...], 0.0), u], axis=0)
    lane_grp = lax.broadcasted_iota(I32, (tm, pw), 1) // (pw // len(POOL_WINDOWS))
    pooled = jnp.zeros((tm, pw), F32)
    s, w = ext, 1
    while w < POOL_HALO:
        s = s + pltpu.roll(s, w, axis=0)
        w *= 2
        if w in POOL_WINDOWS:
            pooled = jnp.where(lane_grp == POOL_WINDOWS.index(w), s[POOL_HALO:, :], pooled)
    win = jnp.zeros((tm, pw), I32)
    for g, w in enumerate(POOL_WINDOWS):
        win = jnp.where(lane_grp == g, w, win)
    pos = si * tm + lax.broadcasted_iota(I32, (tm, pw), 0)
    cnt = jnp.minimum(pos + 1, win).astype(F32)
    pooled = pooled / cnt - u
    pool_out = jnp.dot(pooled.astype(BF16), pbd_ref[...], preferred_element_type=F32) * psc_ref[...]
    def token_order(slot, ref):
        d, n, w = ref.shape
        for r in range(d):
            for c in range(w // LANES):
                til_ref[slot, c, pl.ds(r, n, stride=d), :] = ref[r, :, c * LANES:(c + 1) * LANES]
        return jnp.concatenate([til_ref[slot, c] for c in range(w // LANES)], axis=1)

    l0 = l0_ref[...]
    l1 = token_order(0, l1_ref)
    l2 = token_order(1, l2_ref)
    m = jnp.maximum(jnp.maximum(l0, l1), l2)
    w0 = jnp.exp(l0 - m)
    w1 = jnp.exp(l1 - m)
    w2 = jnp.exp(l2 - m)
    attn = (w0 * o0_ref[...] + w1 * token_order(2, o1_ref) + w2 * token_order(3, o2_ref)) / (w0 + w1 + w2)
    mixed = (jnp.dot(pool_out.astype(BF16), wout_ref[0:pw, :], preferred_element_type=F32)
             + jnp.dot(attn.astype(BF16), wout_ref[pw:, :], preferred_element_type=F32))
    x1 = x_ref[...] + gate1_ref[0] * mixed
    h2 = _rms(x1) * gffn_ref[...]
    h2 = h2 * (1.0 + sc2_ref[0]) + sh2_ref[0]
    h2_ref[...] = _pack_bf16_pairs(h2)
    hb = h2.astype(BF16)
    a = jnp.dot(hb, wsg_ref[...], preferred_element_type=F32)
    b = jnp.dot(hb, wsu_ref[...], preferred_element_type=F32)
    act = (a * _sigmoid(a)) * b
    shared = jnp.dot(act.astype(BF16), wsd_ref[...], preferred_element_type=F32)
    xacc_ref[...] = x1 + gate2_ref[0] * shared
    @pl.when(i == 0)
    def _():
        base_ref[...] = jnp.zeros_like(base_ref)

    for t0 in range(0, tm, ROUTE_TILE):
        idxs, gates, ranks = _route_tile(hb[t0:t0 + ROUTE_TILE, :], wr_ref, bias_ref, before_ref, base_ref)
        for k in range(TOP_K):
            e_ref[k:k + 1, t0:t0 + ROUTE_TILE] = idxs[k]
            g_ref[k:k + 1, t0:t0 + ROUTE_TILE] = gates[k]
            r_ref[k:k + 1, t0:t0 + ROUTE_TILE] = ranks[k]
    cnt_ref[...] = base_ref[...].astype(I32)


def _mid(u, attn_outs, xf, pool_bd, pool_scale, w_out_b, gate1, g_ffn, shift2, scale2, gate2,
         wsg_b, wsu_b, wsd_b, wr_t, bias_col, S):
    N, D = xf.shape
    E = wr_t.shape[0]
    tok = jnp.arange(ROUTE_TILE, dtype=I32)
    before = (tok[:, None] < tok[None, :]).astype(BF16)
    pw = u.shape[1]
    tm = MID_TILE
    spt = S // tm
    row = lambda i: (i, 0)
    vec = lambda i: (i // spt, 0, 0)
    full = lambda a: pl.BlockSpec(a.shape, lambda i: (0,) * a.ndim)
    hpt = tm // POOL_HALO
    in_specs = [pl.BlockSpec((tm, pw), row),
                pl.BlockSpec((POOL_HALO, pw), lambda i: (jnp.maximum(i * hpt - 1, 0), 0))]
    gw = ATT_GROUP_WIDTH
    in_specs += [pl.BlockSpec((tm, gw), row)] * 2
    for d in ATT_DILATIONS[1:]:
        in_specs += [pl.BlockSpec((d, tm // d, gw), lambda i: (i // spt, i % spt, 0))] * 2
    in_specs += [pl.BlockSpec((tm, D), row), full(pool_bd), full(pool_scale), full(w_out_b),
                 pl.BlockSpec((1, 1, D), vec), full(g_ffn), pl.BlockSpec((1, 1, D), vec),
                 pl.BlockSpec((1, 1, D), vec), pl.BlockSpec((1, 1, D), vec),
                 full(wsg_b), full(wsu_b), full(wsd_b), full(wr_t), full(bias_col), full(before)]
    col = lambda i: (0, i)
    return pl.pallas_call(
        functools.partial(_mid_kernel, spt),
        grid=(N // tm,),
        in_specs=in_specs,
        out_specs=[pl.BlockSpec((tm, D), row), pl.BlockSpec((tm, D // 2), row),
                   pl.BlockSpec((TOP_K, tm), col), pl.BlockSpec((TOP_K, tm), col), pl.BlockSpec((TOP_K, tm), col),
                   pl.BlockSpec((E, 1), lambda i: (0, 0))],
        out_shape=[jax.ShapeDtypeStruct((N, D), F32), jax.ShapeDtypeStruct((N, D // 2), U32),
                   jax.ShapeDtypeStruct((TOP_K, N), I32), jax.ShapeDtypeStruct((TOP_K, N), F32),
                   jax.ShapeDtypeStruct((TOP_K, N), I32), jax.ShapeDtypeStruct((E, 1), I32)],
        scratch_shapes=[pltpu.VMEM((4, gw // LANES, tm, LANES), F32), pltpu.VMEM((E, 1), F32)],
        compiler_params=pltpu.CompilerParams(dimension_semantics=("arbitrary",)),
        name="mid",
    )(u, u, *attn_outs, xf, pool_bd, pool_scale, w_out_b, gate1, g_ffn, shift2, scale2, gate2,
      wsg_b, wsu_b, wsd_b, wr_t, bias_col, before)


def _route_tile(hb, wr_ref, bias_ref, before_ref, base_ref):
    T = hb.shape[0]
    E = wr_ref.shape[0]
    gsz = E // N_EXPERT_GROUPS
    logits = lax.dot_general(wr_ref[...], hb, (((1,), (1,)), ((), ())), preferred_element_type=F32)
    scores = _sigmoid(logits)
    biased = scores + bias_ref[...]
    giota = lax.broadcasted_iota(I32, (gsz, T), 0)
    gscore = []
    for g in range(N_EXPERT_GROUPS):
        blk = biased[g * gsz:(g + 1) * gsz, :]
        m1 = jnp.max(blk, axis=0, keepdims=True)
        i1 = jnp.min(jnp.where(blk == m1, giota, gsz), axis=0, keepdims=True)
        m2 = jnp.max(jnp.where(giota == i1, NEG_INF, blk), axis=0, keepdims=True)
        gscore.append(m1 + m2)
    parts = []
    for g in range(N_EXPERT_GROUPS):
        beaten = jnp.zeros((1, T), I32)
        for o in range(N_EXPERT_GROUPS):
            if o == g:
                continue
            wins = (gscore[o] >= gscore[g]) if o < g else (gscore[o] > gscore[g])
            beaten = beaten + wins.astype(I32)
        keep = jnp.broadcast_to(beaten, (gsz, T)) < TOPK_GROUPS
        parts.append(jnp.where(keep, biased[g * gsz:(g + 1) * gsz, :], NEG_INF))
    cur = jnp.concatenate(parts, axis=0)
    eiota = lax.broadcasted_iota(I32, (E, T), 0)
    live = cur > NEG_INF
    idxs, gates = [], []
    for k in range(TOP_K):
        m = jnp.max(cur, axis=0, keepdims=True)
        idx = jnp.min(jnp.where(cur == m, eiota, E), axis=0, keepdims=True)
        oh = eiota == idx
        gates.append(jnp.sum(jnp.where(oh, scores, 0.0), axis=0, keepdims=True))
        idxs.append(idx)
        cur = jnp.where(oh, NEG_INF, cur)
    selm = jnp.where(live & (cur == NEG_INF), 1.0, 0.0)
    gsum = gates[0]
    for k in range(1, TOP_K):
        gsum = gsum + gates[k]
    gates = [gk / gsum * ROUTED_SCALE for gk in gates]
    tot = jnp.dot(selm.astype(BF16), before_ref[...], preferred_element_type=F32) + base_ref[...]
    ranks = [jnp.sum(jnp.where(eiota == idxs[k], tot, 0.0), axis=0, keepdims=True).astype(I32)
             for k in range(TOP_K)]
    base_ref[...] = base_ref[...] + jnp.sum(selm, axis=1, keepdims=True)
    return idxs, gates, ranks


def _dest_kernel(e_ref, r_ref, off_ref, dc_ref):
    E = off_ref.shape[0]
    T = e_ref.shape[1]
    C = dc_ref.shape[2]
    eiota = lax.broadcasted_iota(I32, (E, T), 0)
    off = off_ref[...]
    for k in range(TOP_K):
        start = jnp.sum(jnp.where(eiota == e_ref[k:k + 1, :], off, 0.0), axis=0, keepdims=True)
        d = start.astype(I32) + r_ref[k:k + 1, :]
        for c in range(T // C):
            dc_ref[c, k:k + 1, :] = d[:, c * C:(c + 1) * C]


def _dest(eidx, rank, offs_col):
    N = eidx.shape[1]
    E = offs_col.shape[0]
    T = DEST_TILE
    C = SC_CHUNK
    col = lambda i: (0, i)
    return pl.pallas_call(
        _dest_kernel,
        grid=(N // T,),
        in_specs=[pl.BlockSpec((TOP_K, T), col), pl.BlockSpec((TOP_K, T), col),
                  pl.BlockSpec((E, 1), lambda i: (0, 0))],
        out_specs=pl.BlockSpec((T // C, TOP_K, C), lambda i: (i, 0, 0)),
        out_shape=jax.ShapeDtypeStruct((N // C, TOP_K, C), I32),
        name="dest",
    )(eidx, rank, offs_col)


def _sc_dispatch(dest_c, h2p, P):
    N, W = h2p.shape
    C = dest_c.shape[2]
    info = plsc.get_sparse_core_info()
    nw = info.num_cores * info.num_subcores
    per_w = N // C // nw
    mesh = plsc.VectorSubcoreMesh(core_axis_name="c", subcore_axis_name="s")

    @functools.partial(
        pl.kernel, mesh=mesh,
        out_type=jax.ShapeDtypeStruct((P, W), h2p.dtype),
        scratch_types=[pltpu.VMEM((TOP_K, C), I32), pltpu.VMEM((C, W), h2p.dtype), pltpu.SemaphoreType.DMA],
        name="sc_dispatch",
    )
    def k(dest_hbm, h_hbm, xs_hbm, idx_v, rows_v, sem):
        wid = lax.axis_index("s") * info.num_cores + lax.axis_index("c")

        @pl.loop(0, per_w)
        def _(j):
            ch = wid * per_w + j
            pltpu.sync_copy(dest_hbm.at[ch], idx_v)
            pltpu.sync_copy(h_hbm.at[pl.ds(ch * C, C)], rows_v)
            copies = [pltpu.async_copy(rows_v, xs_hbm.at[idx_v.at[kk]], sem) for kk in range(TOP_K)]
            for cp in copies:
                cp.wait()

    return k(dest_c, h2p)


def _gmm_kernel(bstart_ref, nbe_ref, cnt_ref, nu_ref, wg_ref, wu_ref, wd_ref, xs_hbm, ys_hbm,
                xbuf, ybuf, xsem, ysem):
    step = pl.program_id(0)
    last = pl.num_programs(0) - 1
    epg = wg_ref.shape[0]
    ring, bm = xbuf.shape[0], xbuf.shape[1]
    nblk = ys_hbm.shape[0] // bm
    nused = nu_ref[0]

    def x_copy(b, slot):
        return pltpu.make_async_copy(xs_hbm.at[pl.ds(pl.multiple_of(b * bm, bm), bm), :], xbuf.at[slot],
                                     xsem.at[slot])

    def y_copy(b, slot):
        return pltpu.make_async_copy(ybuf.at[slot], ys_hbm.at[pl.ds(pl.multiple_of(b * bm, bm), bm), :],
                                     ysem.at[slot])

    @pl.when(step == 0)
    def _():
        for j in range(ring - 1):
            @pl.when(j < nused)
            def _():
                x_copy(j, j).start()

    def run_expert(ee):
        e = step * epg + ee
        b0 = bstart_ref[e]
        nb = nbe_ref[e]

        @pl.when(nb > 0)
        def _():
            def prefetch(t):
                @pl.when(t < nused)
                def _():
                    x_copy(t, jnp.bitwise_and(t, ring - 1)).start()

            def process(b, n):
                slots = [jnp.bitwise_and(b + j, ring - 1) for j in range(n)]
                for j in range(n):
                    x_copy(b + j, slots[j]).wait()
                prefetch(b + ring - 1)
                for j in range(n):
                    @pl.when(b + j >= ring)
                    def _():
                        y_copy(b + j - ring, slots[j]).wait()
                rows = lax.broadcasted_iota(I32, (n * bm, 1), 0)
                valid = cnt_ref[e] - (b - b0) * bm
                xp = jnp.concatenate([xbuf[s] for s in slots], axis=0)
                xb = jnp.concatenate(_unpack_bf16_pairs(jnp.where(rows < valid, xp, jnp.uint32(0))), axis=1)
                a = jnp.dot(xb, wg_ref[ee].astype(BF16), preferred_element_type=F32)
                g = jnp.dot(xb, wu_ref[ee].astype(BF16), preferred_element_type=F32)
                act = (a * _sigmoid(a)) * g
                yp = _pack_bf16_pairs(jnp.dot(act.astype(BF16), wd_ref[ee].astype(BF16),
                                              preferred_element_type=F32))
                for j in range(n):
                    ybuf[slots[j]] = yp[j * bm:(j + 1) * bm, :]
                    y_copy(b + j, slots[j]).start()
                for j in range(1, n):
                    prefetch(b + ring - 1 + j)

            quads = lax.shift_right_logical(nb, 2)
            lax.fori_loop(0, quads, lambda j, c: (process(b0 + 4 * j, 4), c)[1], 0)
            rest2 = jnp.bitwise_and(nb, 2)

            @pl.when(rest2 != 0)
            def _():
                process(b0 + 4 * quads, 2)

            @pl.when(jnp.bitwise_and(nb, 1) != 0)
            def _():
                process(b0 + 4 * quads + rest2, 1)

    for ee in range(epg):
        run_expert(ee)

    @pl.when(step == last)
    def _():
        for back in range(ring, 0, -1):
            @pl.when(nused >= back)
            def _():
                y_copy(nused - back, jnp.bitwise_and(nused - back, ring - 1)).wait()
        ybuf[0] = jnp.zeros(ybuf.shape[1:], ybuf.dtype)
        lax.fori_loop(nused, nblk, lambda b, c: (y_copy(b, 0).start(), c)[1], 0)
        lax.fori_loop(nused, nblk, lambda b, c: (y_copy(b, 0).wait(), c)[1], 0)


def _gmm(bstart, nb_e, counts, nused, xs, w_gate, w_up, w_down):
    P, W = xs.shape
    E, D, F = w_gate.shape
    bm = GMM_BLOCK
    epg = GMM_EXPERTS_PER_STEP
    wsel = lambda s, *_: (s, 0, 0)
    grid_spec = pltpu.PrefetchScalarGridSpec(
        num_scalar_prefetch=4,
        grid=(E // epg,),
        in_specs=[pl.BlockSpec((epg, D, F), wsel), pl.BlockSpec((epg, D, F), wsel), pl.BlockSpec((epg, F, D), wsel),
                  pl.BlockSpec(memory_space=pl.ANY)],
        out_specs=pl.BlockSpec(memory_space=pl.ANY),
        scratch_shapes=[pltpu.VMEM((GMM_RING, bm, W), xs.dtype), pltpu.VMEM((GMM_RING, bm, W), xs.dtype),
                        pltpu.SemaphoreType.DMA((GMM_RING,)), pltpu.SemaphoreType.DMA((GMM_RING,))],
    )
    return pl.pallas_call(
        _gmm_kernel,
        grid_spec=grid_spec,
        out_shape=jax.ShapeDtypeStruct((P, W), xs.dtype),
        compiler_params=pltpu.CompilerParams(dimension_semantics=("arbitrary",)),
        name="gmm",
    )(bstart, nb_e, counts, nused, w_gate, w_up, w_down, xs)


def _sc_gather(dest_c, ys):
    nch, K, C = dest_c.shape
    W = ys.shape[1]
    H = C // 2
    info = plsc.get_sparse_core_info()
    nw = info.num_cores * info.num_subcores
    per_w = nch // nw
    nbuf = 3
    mesh = plsc.VectorSubcoreMesh(core_axis_name="c", subcore_axis_name="s")
    items = [(kk, hh) for kk in range(K) for hh in range(2)]

    @functools.partial(
        pl.kernel, mesh=mesh,
        out_type=jax.ShapeDtypeStruct((K, nch * C, W), ys.dtype),
        scratch_types=([pltpu.VMEM((K, C), I32)] + [pltpu.VMEM((H, W), ys.dtype)] * nbuf
                       + [pltpu.SemaphoreType.DMA] * (2 * nbuf)),
        name="sc_gather",
    )
    def k(dest_hbm, ys_hbm, yk_hbm, idx_v, *rest):
        bufs, gsem, wsem = rest[:nbuf], rest[nbuf:2 * nbuf], rest[2 * nbuf:]
        wid = lax.axis_index("s") * info.num_cores + lax.axis_index("c")

        @pl.loop(0, per_w)
        def _(j):
            ch = wid * per_w + j
            pltpu.sync_copy(dest_hbm.at[ch], idx_v)

            def gather(i):
                kk, hh = items[i]
                return pltpu.async_copy(ys_hbm.at[idx_v.at[kk, pl.ds(hh * H, H)]], bufs[i % nbuf], gsem[i % nbuf])

            def write(i):
                kk, hh = items[i]
                return pltpu.async_copy(bufs[i % nbuf], yk_hbm.at[kk, pl.ds(ch * C + hh * H, H)], wsem[i % nbuf])

            n = len(items)
            g = {0: gather(0), 1: gather(1)}
            w = {}
            for i in range(n):
                g[i].wait()
                w[i] = write(i)
                if i + 2 < n:
                    if i >= 1:
                        w.pop(i - 1).wait()
                    g[i + 2] = gather(i + 2)
            for i in sorted(w):
                w[i].wait()

    return k(dest_c, ys)


def _combine_kernel(yk_ref, gt_ref, gate2_ref, xacc_ref, gfin_ref, o_ref):
    gt = gt_ref[...]
    hi_mask = jnp.uint32(0xFFFF0000)
    acc_lo = acc_hi = None
    for k in range(TOP_K):
        p = yk_ref[k]
        g = gt[:, k:k + 1]
        lo = lax.bitcast_convert_type(p << 16, F32) * g
        hi = lax.bitcast_convert_type(p & hi_mask, F32) * g
        acc_lo = lo if k == 0 else acc_lo + lo
        acc_hi = hi if k == 0 else acc_hi + hi
    routed = jnp.concatenate([acc_lo, acc_hi], axis=1)
    x2 = xacc_ref[...] + gate2_ref[0] * routed
    o_ref[...] = _rms(x2) * gfin_ref[...]


def _combine(yk, gates_t, gate2, xacc, g_final, S):
    N, D = xacc.shape
    W = yk.shape[2]
    T = COMBINE_TILE
    spt = S // T
    return pl.pallas_call(
        _combine_kernel,
        grid=(N // T,),
        in_specs=[pl.BlockSpec((TOP_K, T, W), lambda i: (0, i, 0)),
                  pl.BlockSpec((T, TOP_K), lambda i: (i, 0)),
                  pl.BlockSpec((1, 1, D), lambda i: (i // spt, 0, 0)),
                  pl.BlockSpec((T, D), lambda i: (i, 0)),
                  pl.BlockSpec((1, D), lambda i: (0, 0))],
        out_specs=pl.BlockSpec((T, D), lambda i: (i, 0)),
        out_shape=jax.ShapeDtypeStruct((N, D), F32),
        name="combine",
    )(yk, gates_t, gate2, xacc, g_final)


def _layer(xf, B, S, mod, g_mix, w_in, pool_w, pool_scale, w_out, g_ffn, w_router, router_bias,
           w_gate, w_up, w_down, ws_gate, ws_up, ws_down):
    N, D = xf.shape
    E = w_router.shape[1]
    pw = pool_scale.shape[0]
    shift1, scale1, gate1, shift2, scale2, gate2 = [m.reshape(B, 1, D) for m in jnp.split(mod, 6, axis=-1)]
    u, qkv = _in_proj(xf, g_mix.reshape(1, D), shift1, scale1, w_in.astype(BF16), S, pw)
    attn_outs = []
    for a, d in zip(qkv, ATT_DILATIONS):
        o, lse = _attention(a, S // d // ATT_BLOCK)
        shape = (N, ATT_GROUP_WIDTH) if d == 1 else (B * d, S // d, ATT_GROUP_WIDTH)
        attn_outs += [o.reshape(shape), lse.reshape(shape)]
    ng = pool_w.shape[0]
    pool_bd = jnp.einsum('gcd,gh->gchd', pool_w, jnp.eye(ng, dtype=pool_w.dtype)).reshape(pw, pw).astype(BF16)
    xacc, h2, eidx, gates, rank, counts = _mid(
        u, attn_outs, xf, pool_bd, pool_scale.reshape(1, pw), w_out.astype(BF16), gate1,
        g_ffn.reshape(1, D), shift2, scale2, gate2,
        ws_gate.astype(BF16), ws_up.astype(BF16), ws_down.astype(BF16),
        w_router.T.astype(BF16), router_bias.reshape(E, 1).astype(F32), S)
    bm = GMM_BLOCK
    nblk = N * TOP_K // bm + E
    nb_e = (counts[:, 0] + bm - 1) // bm
    bend = jnp.cumsum(nb_e)
    bstart = (bend - nb_e).astype(I32)
    nused = bend[-1:].astype(I32)
    dest_c = _dest(eidx, rank, (bstart * bm).astype(F32).reshape(E, 1))
    xs = _sc_dispatch(dest_c, h2, nblk * bm)
    ys = _gmm(bstart, nb_e.astype(I32), counts[:, 0], nused, xs, w_gate, w_up, w_down)
    return _sc_gather(dest_c, ys), gates.T, gate2, xacc


def kernel(x, c, w_ada, b_ada, g_mix, w_in, pool_w, pool_scale, w_out, g_ffn, w_router, router_bias,
           w_gate, w_up, w_down, ws_gate, ws_up, ws_down, g_final):
    B, S, D = x.shape
    depth = w_ada.shape[0]
    assert depth == 1, "the final residual is fused with the final norm, so exactly one layer is supported"
    assert S % (ATT_DILATIONS[-1] * ATT_BLOCK) == 0 and S % max(IN_TILE, MID_TILE, ATT_BLOCKS_PER_STEP * ATT_BLOCK) == 0
    xf = x.reshape(B * S, D)
    mod = _ada(c, w_ada[0], b_ada[0])
    yk, gates_t, gate2, xacc = _layer(
        xf, B, S, mod, g_mix[0], w_in[0], pool_w[0], pool_scale[0], w_out[0], g_ffn[0], w_router[0],
        router_bias[0], w_gate[0], w_up[0], w_down[0], ws_gate[0], ws_up[0], ws_down[0])
    out = _combine(yk, gates_t, gate2, xacc, g_final.reshape(1, D), S)
    return out.reshape(B, S, D)
```

```python
import functools

import jax
import jax.numpy as jnp
from jax import lax
from jax.experimental import pallas as pl
from jax.experimental.pallas import tpu as pltpu
from jax.experimental.pallas import tpu_sc as plsc

F32 = jnp.float32
BF16 = jnp.bfloat16
I32 = jnp.int32
U32 = jnp.uint32

LANES = 128
SINGLE_LOAD_STRIDE = 4
NORM_EPS = 1e-6
POOL_WINDOWS = (2, 4, 8, 16)
POOL_HALO = 16
ATT_DILATIONS = (1, 4, 16)
ATT_BLOCK = 128
ATT_HEADS_PER_GROUP = 4
ATT_HEAD_DIM = 64
ATT_GROUP_WIDTH = ATT_HEADS_PER_GROUP * ATT_HEAD_DIM
N_EXPERT_GROUPS = 8
TOPK_GROUPS = 4
TOP_K = 8
ROUTED_SCALE = 2.5

IN_TILE = 1024
ATT_BLOCKS_PER_STEP = 16
ATT_UNROLL = 8
MID_TILE = 512
ROUTE_TILE = 256
COMBINE_TILE = 512
DEST_TILE = 1024
GMM_BLOCK = 128
GMM_RING = 16
GMM_CHAIN = 4
GMM_EXPERTS_PER_STEP = 1
SC_CHUNK = 128

NEG_INF = float("-inf")


def _sigmoid(v):
    return 1.0 / (1.0 + jnp.exp(-v))


def _rms(v):
    return v * lax.rsqrt(jnp.mean(v * v, axis=-1, keepdims=True) + NORM_EPS)


def _pack_bf16_pairs(v):
    n = v.shape[1] // 2
    lo = lax.bitcast_convert_type(v[:, :n].astype(BF16).astype(F32), U32)
    hi = lax.bitcast_convert_type(v[:, n:].astype(BF16).astype(F32), U32)
    return (hi & jnp.uint32(0xFFFF0000)) | (lo >> 16)


def _unpack_bf16_pairs(p):
    lo = lax.bitcast_convert_type(p << 16, F32).astype(BF16)
    hi = lax.bitcast_convert_type(p & jnp.uint32(0xFFFF0000), F32).astype(BF16)
    return lo, hi


def _ada_kernel(c_ref, w_ref, b_ref, o_ref):
    c = c_ref[...]
    cs = c * _sigmoid(c)
    o_ref[...] = jnp.dot(cs, w_ref[...], preferred_element_type=F32,
                         precision=lax.Precision.HIGHEST) + b_ref[...]


def _ada(c, w_ada, b_ada):
    B, D = c.shape
    W = w_ada.shape[1]
    tn = 1024
    return pl.pallas_call(
        _ada_kernel,
        grid=(W // tn,),
        in_specs=[pl.BlockSpec((B, D), lambda j: (0, 0)),
                  pl.BlockSpec((D, tn), lambda j: (0, j)),
                  pl.BlockSpec((1, tn), lambda j: (0, j))],
        out_specs=pl.BlockSpec((B, tn), lambda j: (0, j)),
        out_shape=jax.ShapeDtypeStruct((B, W), F32),
        name="ada",
    )(c, w_ada, b_ada.reshape(1, W))


def _in_kernel(x_ref, g_ref, sh_ref, sc_ref, w_ref, pool_ref, q0_ref, q1_ref, q2_ref, scr_ref, tmp_ref):
    h = _rms(x_ref[...]) * g_ref[...]
    h = h * (1.0 + sc_ref[0]) + sh_ref[0]
    hb = h.astype(BF16)
    tm = x_ref.shape[0]
    pw = pool_ref.shape[1]
    gw = ATT_GROUP_WIDTH
    pool_ref[...] = jnp.dot(hb, w_ref[:, 0:pw], preferred_element_type=F32)
    for g, (out, d) in enumerate(zip((q0_ref, q1_ref, q2_ref), ATT_DILATIONS)):
        for sec in range(3):
            c0 = pw + sec * 3 * gw + g * gw
            res = jnp.dot(hb, w_ref[:, c0:c0 + gw], preferred_element_type=F32)
            if d == 1:
                out[:, sec * gw:(sec + 1) * gw] = res.astype(BF16)
            else:
                for c in range(gw // LANES):
                    scr_ref[c] = res[:, c * LANES:(c + 1) * LANES]
                    src, f1 = scr_ref.at[c], 1
                    if d > SINGLE_LOAD_STRIDE:
                        f1 = SINGLE_LOAD_STRIDE
                        for q in range(f1):
                            tmp_ref[c, q * (tm // f1):(q + 1) * (tm // f1), :] = scr_ref[c, pl.ds(q, tm // f1, stride=f1), :]
                        src = tmp_ref.at[c]
                    for r in range(d):
                        r_lo, r_hi = r % f1, r // f1
                        c1 = sec * gw + c * LANES
                        out[r, :, c1:c1 + LANES] = src[pl.ds(r_lo * (tm // f1) + r_hi, tm // d, stride=d // f1),
                                                       :].astype(BF16)


def _in_proj(xf, g_mix, shift1, scale1, w_in_b, S, pool_width):
    N, D = xf.shape
    tm = IN_TILE
    spt = S // tm
    vec = lambda i: (i // spt, 0, 0)
    row = lambda i: (i, 0)
    gw3 = 3 * ATT_GROUP_WIDTH
    B = N // S
    res_spec = lambda d: pl.BlockSpec((d, tm // d, gw3), lambda i: (i // spt, i % spt, 0))
    res_shape = lambda d: jax.ShapeDtypeStruct((B * d, S // d, gw3), BF16)
    outs = pl.pallas_call(
        _in_kernel,
        grid=(N // tm,),
        in_specs=[pl.BlockSpec((tm, D), row),
                  pl.BlockSpec((1, D), lambda i: (0, 0)),
                  pl.BlockSpec((1, 1, D), vec),
                  pl.BlockSpec((1, 1, D), vec),
                  pl.BlockSpec(w_in_b.shape, lambda i: (0, 0))],
        out_specs=[pl.BlockSpec((tm, pool_width), row), pl.BlockSpec((tm, gw3), row)]
                  + [res_spec(d) for d in ATT_DILATIONS[1:]],
        out_shape=[jax.ShapeDtypeStruct((N, pool_width), F32), jax.ShapeDtypeStruct((N, gw3), BF16)]
                  + [res_shape(d) for d in ATT_DILATIONS[1:]],
        scratch_shapes=[pltpu.VMEM((ATT_GROUP_WIDTH // LANES, tm, LANES), F32)] * 2,
        name="in_proj",
    )(xf, g_mix, shift1, scale1, w_in_b)
    return outs[0], [o.reshape(N, gw3) for o in outs[1:]]


def _attn_kernel(nbs, a_ref, halo_ref, o_ref, lse_ref, kv_ref, band_ref):
    i = pl.program_id(0)
    R = a_ref.shape[0] // ATT_BLOCK
    gw = ATT_GROUP_WIDTH
    blk = ATT_BLOCK
    nh = ATT_HEADS_PER_GROUP
    kv_ref[0:blk, :] = halo_ref[:, gw:3 * gw]
    kv_ref[blk:, :] = a_ref[:, gw:3 * gw]
    row = lax.broadcasted_iota(I32, (nh * blk, 2 * blk), 0) % blk
    col = lax.broadcasted_iota(I32, (nh * blk, 2 * blk), 1)
    in_band = (col >= row) & (col <= row + blk)
    band_ref[0] = jnp.where(in_band, 0.0, NEG_INF)
    band_ref[1] = jnp.where(in_band & (col >= blk), 0.0, NEG_INF)
    head_of_lane = lax.broadcasted_iota(I32, (blk, gw), 1) // ATT_HEAD_DIM
    nt = (((1,), (1,)), ((), ()))

    def one_block(jj, start):
        r0 = pl.multiple_of(jj * blk, blk)
        qf = a_ref[pl.ds(r0, blk), 0:gw].astype(F32) * (ATT_HEAD_DIM ** -0.5)
        q4 = jnp.concatenate([jnp.where(head_of_lane == h, qf, 0.0) for h in range(nh)], axis=0).astype(BF16)
        kc = kv_ref[pl.ds(r0, 2 * blk), 0:gw]
        vc = kv_ref[pl.ds(r0, 2 * blk), gw:2 * gw]
        s = lax.dot_general(q4, kc, nt, preferred_element_type=F32) + band_ref[1 if start is True else 0]
        if start is not None and start is not True:
            s = jnp.where(col >= jnp.where(start, blk, 0), s, NEG_INF)
        m = jnp.max(s, axis=1, keepdims=True)
        p = jnp.exp(s - m)
        l = jnp.sum(p, axis=1, keepdims=True)
        o4 = jnp.dot(p.astype(BF16), vc, preferred_element_type=F32) / l
        lse4 = m + jnp.log(l)
        o = jnp.zeros((blk, gw), F32)
        lse = jnp.zeros((blk, gw), F32)
        for h in range(nh):
            hm = head_of_lane == h
            o = jnp.where(hm, o4[h * blk:(h + 1) * blk, :], o)
            lse = jnp.where(hm, lse4[h * blk:(h + 1) * blk, :], lse)
        o_ref[pl.ds(r0, blk), :] = o
        lse_ref[pl.ds(r0, blk), :] = lse

    U = ATT_UNROLL
    assert U % nbs == 0 or nbs % U == 0

    def body(it, carry):
        for j in range(U):
            if U % nbs == 0:
                start = True if j % nbs == 0 else None
            else:
                start = (((i * R + it * U) % nbs) == 0) if j == 0 else None
            one_block(it * U + j, start)
        return carry

    lax.fori_loop(0, R // U, body, 0)


def _attention(a, nbs):
    N = a.shape[0]
    R = ATT_BLOCKS_PER_STEP
    gw = ATT_GROUP_WIDTH
    tm = R * ATT_BLOCK
    return pl.pallas_call(
        functools.partial(_attn_kernel, nbs),
        grid=(N // tm,),
        in_specs=[pl.BlockSpec((tm, 3 * gw), lambda i: (i, 0)),
                  pl.BlockSpec((ATT_BLOCK, 3 * gw), lambda i: (jnp.maximum(i * R - 1, 0), 0))],
        out_specs=[pl.BlockSpec((tm, gw), lambda i: (i, 0))] * 2,
        out_shape=[jax.ShapeDtypeStruct((N, gw), F32)] * 2,
        scratch_shapes=[pltpu.VMEM((tm + ATT_BLOCK, 2 * gw), BF16),
                        pltpu.VMEM((2, ATT_HEADS_PER_GROUP * ATT_BLOCK, 2 * ATT_BLOCK), F32)],
        name="attn",
    )(a, a)


def _mid_kernel(spt, u_ref, uh_ref, o0_ref, l0_ref, o1_ref, l1_ref, o2_ref, l2_ref, x_ref,
                pbd_ref, psc_ref, wout_ref, gate1_ref, gffn_ref, sh2_ref, sc2_ref, gate2_ref,
                wsg_ref, wsu_ref, wsd_ref, wr_ref, bias_ref, before_ref,
                xacc_ref, h2_ref, e_ref, g_ref, r_ref, cnt_ref, til_ref, base_ref):
    i = pl.program_id(0)
    tm, pw = u_ref.shape
    si = i % spt
    u = u_ref[...]
    keep = jnp.full((POOL_HALO, pw), si, I32) > 0
    ext = jnp.concatenate([jnp.where(keep, uh_ref[...], 0.0), u], axis=0)
    lane_grp = lax.broadcasted_iota(I32, (tm, pw), 1) // (pw // len(POOL_WINDOWS))
    pooled = jnp.zeros((tm, pw), F32)
    s, w = ext, 1
    while w < POOL_HALO:
        s = s + pltpu.roll(s, w, axis=0)
        w *= 2
        if w in POOL_WINDOWS:
            pooled = jnp.where(lane_grp == POOL_WINDOWS.index(w), s[POOL_HALO:, :], pooled)
    win = jnp.zeros((tm, pw), I32)
    for g, w in enumerate(POOL_WINDOWS):
        win = jnp.where(lane_grp == g, w, win)
    pos = si * tm + lax.broadcasted_iota(I32, (tm, pw), 0)
    cnt = jnp.minimum(pos + 1, win).astype(F32)
    pooled = pooled / cnt - u
    pool_out = jnp.dot(pooled.astype(BF16), pbd_ref[...], preferred_element_type=F32) * psc_ref[...]
    def token_order(slot, ref):
        d, n, w = ref.shape
        for r in range(d):
            for c in range(w // LANES):
                til_ref[slot, c, pl.ds(r, n, stride=d), :] = ref[r, :, c * LANES:(c + 1) * LANES]
        return jnp.concatenate([til_ref[slot, c] for c in range(w // LANES)], axis=1)

    l0 = l0_ref[...]
    l1 = token_order(0, l1_ref)
    l2 = token_order(1, l2_ref)
    m = jnp.maximum(jnp.maximum(l0, l1), l2)
    w0 = jnp.exp(l0 - m)
    w1 = jnp.exp(l1 - m)
    w2 = jnp.exp(l2 - m)
    attn = (w0 * o0_ref[...] + w1 * token_order(2, o1_ref) + w2 * token_order(3, o2_ref)) / (w0 + w1 + w2)
    mixed = (jnp.dot(pool_out.astype(BF16), wout_ref[0:pw, :], preferred_element_type=F32)
             + jnp.dot(attn.astype(BF16), wout_ref[pw:, :], preferred_element_type=F32))
    x1 = x_ref[...] + gate1_ref[0] * mixed
    h2 = _rms(x1) * gffn_ref[...]
    h2 = h2 * (1.0 + sc2_ref[0]) + sh2_ref[0]
    h2_ref[...] = _pack_bf16_pairs(h2)
    hb = h2.astype(BF16)
    a = jnp.dot(hb, wsg_ref[...], preferred_element_type=F32)
    b = jnp.dot(hb, wsu_ref[...], preferred_element_type=F32)
    act = (a * _sigmoid(a)) * b
    shared = jnp.dot(act.astype(BF16), wsd_ref[...], preferred_element_type=F32)
    xacc_ref[...] = x1 + gate2_ref[0] * shared
    @pl.when(i == 0)
    def _():
        base_ref[...] = jnp.zeros_like(base_ref)

    for t0 in range(0, tm, ROUTE_TILE):
        idxs, gates, ranks = _route_tile(hb[t0:t0 + ROUTE_TILE, :], wr_ref, bias_ref, before_ref, base_ref)
        for k in range(TOP_K):
            e_ref[k:k + 1, t0:t0 + ROUTE_TILE] = idxs[k]
            g_ref[k:k + 1, t0:t0 + ROUTE_TILE] = gates[k]
            r_ref[k:k + 1, t0:t0 + ROUTE_TILE] = ranks[k]
    cnt_ref[...] = base_ref[...].astype(I32)


def _mid(u, attn_outs, xf, pool_bd, pool_scale, w_out_b, gate1, g_ffn, shift2, scale2, gate2,
         wsg_b, wsu_b, wsd_b, wr_t, bias_col, S):
    N, D = xf.shape
    E = wr_t.shape[0]
    tok = jnp.arange(ROUTE_TILE, dtype=I32)
    before = (tok[:, None] < tok[None, :]).astype(BF16)
    pw = u.shape[1]
    tm = MID_TILE
    spt = S // tm
    row = lambda i: (i, 0)
    vec = lambda i: (i // spt, 0, 0)
    full = lambda a: pl.BlockSpec(a.shape, lambda i: (0,) * a.ndim)
    hpt = tm // POOL_HALO
    in_specs = [pl.BlockSpec((tm, pw), row),
                pl.BlockSpec((POOL_HALO, pw), lambda i: (jnp.maximum(i * hpt - 1, 0), 0))]
    gw = ATT_GROUP_WIDTH
    in_specs += [pl.BlockSpec((tm, gw), row)] * 2
    for d in ATT_DILATIONS[1:]:
        in_specs += [pl.BlockSpec((d, tm // d, gw), lambda i: (i // spt, i % spt, 0))] * 2
    in_specs += [pl.BlockSpec((tm, D), row), full(pool_bd), full(pool_scale), full(w_out_b),
                 pl.BlockSpec((1, 1, D), vec), full(g_ffn), pl.BlockSpec((1, 1, D), vec),
                 pl.BlockSpec((1, 1, D), vec), pl.BlockSpec((1, 1, D), vec),
                 full(wsg_b), full(wsu_b), full(wsd_b), full(wr_t), full(bias_col), full(before)]
    col = lambda i: (0, i)
    return pl.pallas_call(
        functools.partial(_mid_kernel, spt),
        grid=(N // tm,),
        in_specs=in_specs,
        out_specs=[pl.BlockSpec((tm, D), row), pl.BlockSpec((tm, D // 2), row),
                   pl.BlockSpec((TOP_K, tm), col), pl.BlockSpec((TOP_K, tm), col), pl.BlockSpec((TOP_K, tm), col),
                   pl.BlockSpec((E, 1), lambda i: (0, 0))],
        out_shape=[jax.ShapeDtypeStruct((N, D), F32), jax.ShapeDtypeStruct((N, D // 2), U32),
                   jax.ShapeDtypeStruct((TOP_K, N), I32), jax.ShapeDtypeStruct((TOP_K, N), F32),
                   jax.ShapeDtypeStruct((TOP_K, N), I32), jax.ShapeDtypeStruct((E, 1), I32)],
        scratch_shapes=[pltpu.VMEM((4, gw // LANES, tm, LANES), F32), pltpu.VMEM((E, 1), F32)],
        compiler_params=pltpu.CompilerParams(dimension_semantics=("arbitrary",)),
        name="mid",
    )(u, u, *attn_outs, xf, pool_bd, pool_scale, w_out_b, gate1, g_ffn, shift2, scale2, gate2,
      wsg_b, wsu_b, wsd_b, wr_t, bias_col, before)


def _route_tile(hb, wr_ref, bias_ref, before_ref, base_ref):
    T = hb.shape[0]
    E = wr_ref.shape[0]
    gsz = E // N_EXPERT_GROUPS
    logits = lax.dot_general(wr_ref[...], hb, (((1,), (1,)), ((), ())), preferred_element_type=F32)
    scores = _sigmoid(logits)
    biased = scores + bias_ref[...]
    giota = lax.broadcasted_iota(I32, (gsz, T), 0)
    gscore = []
    for g in range(N_EXPERT_GROUPS):
        blk = biased[g * gsz:(g + 1) * gsz, :]
        m1 = jnp.max(blk, axis=0, keepdims=True)
        i1 = jnp.min(jnp.where(blk == m1, giota, gsz), axis=0, keepdims=True)
        m2 = jnp.max(jnp.where(giota == i1, NEG_INF, blk), axis=0, keepdims=True)
        gscore.append(m1 + m2)
    parts = []
    for g in range(N_EXPERT_GROUPS):
        beaten = jnp.zeros((1, T), I32)
        for o in range(N_EXPERT_GROUPS):
            if o == g:
                continue
            wins = (gscore[o] >= gscore[g]) if o < g else (gscore[o] > gscore[g])
            beaten = beaten + wins.astype(I32)
        keep = jnp.broadcast_to(beaten, (gsz, T)) < TOPK_GROUPS
        parts.append(jnp.where(keep, biased[g * gsz:(g + 1) * gsz, :], NEG_INF))
    cur = jnp.concatenate(parts, axis=0)
    eiota = lax.broadcasted_iota(I32, (E, T), 0)
    live = cur > NEG_INF
    idxs, gates = [], []
    for k in range(TOP_K):
        m = jnp.max(cur, axis=0, keepdims=True)
        idx = jnp.min(jnp.where(cur == m, eiota, E), axis=0, keepdims=True)
        oh = eiota == idx
        gates.append(jnp.sum(jnp.where(oh, scores, 0.0), axis=0, keepdims=True))
        idxs.append(idx)
        cur = jnp.where(oh, NEG_INF, cur)
    selm = jnp.where(live & (cur == NEG_INF), 1.0, 0.0)
    gsum = gates[0]
    for k in range(1, TOP_K):
        gsum = gsum + gates[k]
    gates = [gk / gsum * ROUTED_SCALE for gk in gates]
    tot = jnp.dot(selm.astype(BF16), before_ref[...], preferred_element_type=F32) + base_ref[...]
    ranks = [jnp.sum(jnp.where(eiota == idxs[k], tot, 0.0), axis=0, keepdims=True).astype(I32)
             for k in range(TOP_K)]
    base_ref[...] = base_ref[...] + jnp.sum(selm, axis=1, keepdims=True)
    return idxs, gates, ranks


def _dest_kernel(e_ref, r_ref, off_ref, dc_ref):
    E = off_ref.shape[0]
    T = e_ref.shape[1]
    C = dc_ref.shape[2]
    eiota = lax.broadcasted_iota(I32, (E, T), 0)
    off = off_ref[...]
    for k in range(TOP_K):
        start = jnp.sum(jnp.where(eiota == e_ref[k:k + 1, :], off, 0.0), axis=0, keepdims=True)
        d = start.astype(I32) + r_ref[k:k + 1, :]
        for c in range(T // C):
            dc_ref[c, k:k + 1, :] = d[:, c * C:(c + 1) * C]


def _dest(eidx, rank, offs_col):
    N = eidx.shape[1]
    E = offs_col.shape[0]
    T = DEST_TILE
    C = SC_CHUNK
    col = lambda i: (0, i)
    return pl.pallas_call(
        _dest_kernel,
        grid=(N // T,),
        in_specs=[pl.BlockSpec((TOP_K, T), col), pl.BlockSpec((TOP_K, T), col),
                  pl.BlockSpec((E, 1), lambda i: (0, 0))],
        out_specs=pl.BlockSpec((T // C, TOP_K, C), lambda i: (i, 0, 0)),
        out_shape=jax.ShapeDtypeStruct((N // C, TOP_K, C), I32),
        name="dest",
    )(eidx, rank, offs_col)


def _sc_dispatch(dest_c, h2p, P):
    N, W = h2p.shape
    C = dest_c.shape[2]
    info = plsc.get_sparse_core_info()
    nw = info.num_cores * info.num_subcores
    per_w = N // C // nw
    mesh = plsc.VectorSubcoreMesh(core_axis_name="c", subcore_axis_name="s")

    @functools.partial(
        pl.kernel, mesh=mesh,
        out_type=jax.ShapeDtypeStruct((P, W), h2p.dtype),
        scratch_types=[pltpu.VMEM((TOP_K, C), I32), pltpu.VMEM((C, W), h2p.dtype), pltpu.SemaphoreType.DMA],
        name="sc_dispatch",
    )
    def k(dest_hbm, h_hbm, xs_hbm, idx_v, rows_v, sem):
        wid = lax.axis_index("s") * info.num_cores + lax.axis_index("c")

        @pl.loop(0, per_w)
        def _(j):
            ch = wid * per_w + j
            pltpu.sync_copy(dest_hbm.at[ch], idx_v)
            pltpu.sync_copy(h_hbm.at[pl.ds(ch * C, C)], rows_v)
            copies = [pltpu.async_copy(rows_v, xs_hbm.at[idx_v.at[kk]], sem) for kk in range(TOP_K)]
            for cp in copies:
                cp.wait()

    return k(dest_c, h2p)


def _gmm_kernel(bstart_ref, nbe_ref, cnt_ref, nu_ref, wg_ref, wu_ref, wd_ref, xs_hbm, ys_hbm,
                xbuf, ybuf, xsem, ysem):
    step = pl.program_id(0)
    last = pl.num_programs(0) - 1
    epg = wg_ref.shape[0]
    ring, bm = xbuf.shape[0], xbuf.shape[1]
    nblk = ys_hbm.shape[0] // bm
    nused = nu_ref[0]

    def x_copy(b, slot):
        return pltpu.make_async_copy(xs_hbm.at[pl.ds(pl.multiple_of(b * bm, bm), bm), :], xbuf.at[slot],
                                     xsem.at[slot])

    def y_copy(b, slot):
        return pltpu.make_async_copy(ybuf.at[slot], ys_hbm.at[pl.ds(pl.multiple_of(b * bm, bm), bm), :],
                                     ysem.at[slot])

    @pl.when(step == 0)
    def _():
        for j in range(ring - 1):
            @pl.when(j < nused)
            def _():
                x_copy(j, j).start()

    def run_expert(ee):
        e = step * epg + ee
        b0 = bstart_ref[e]
        nb = nbe_ref[e]

        @pl.when(nb > 0)
        def _():
            def prefetch(t):
                @pl.when(t < nused)
                def _():
                    x_copy(t, jnp.bitwise_and(t, ring - 1)).start()

            def process(b, n):
                slots = [jnp.bitwise_and(b + j, ring - 1) for j in range(n)]
                for j in range(n):
                    x_copy(b + j, slots[j]).wait()
                prefetch(b + ring - 1)
                for j in range(n):
                    @pl.when(b + j >= ring)
                    def _():
                        y_copy(b + j - ring, slots[j]).wait()
                rows = lax.broadcasted_iota(I32, (n * bm, 1), 0)
                valid = cnt_ref[e] - (b - b0) * bm
                xp = jnp.concatenate([xbuf[s] for s in slots], axis=0)
                xb = jnp.concatenate(_unpack_bf16_pairs(jnp.where(rows < valid, xp, jnp.uint32(0))), axis=1)
                a = jnp.dot(xb, wg_ref[ee].astype(BF16), preferred_element_type=F32)
                g = jnp.dot(xb, wu_ref[ee].astype(BF16), preferred_element_type=F32)
                act = (a * _sigmoid(a)) * g
                yp = _pack_bf16_pairs(jnp.dot(act.astype(BF16), wd_ref[ee].astype(BF16),
                                              preferred_element_type=F32))
                for j in range(n):
                    ybuf[slots[j]] = yp[j * bm:(j + 1) * bm, :]
                    y_copy(b + j, slots[j]).start()
                for j in range(1, n):
                    prefetch(b + ring - 1 + j)

            ch = GMM_CHAIN
            main = jnp.maximum(nb // ch - 1, 0)
            lax.fori_loop(0, main, lambda j, c: (process(b0 + ch * j, ch), c)[1], 0)
            tail = nb - ch * main
            for n in range(1, 2 * ch):
                @pl.when(tail == n)
                def _():
                    process(b0 + ch * main, n)

    for ee in range(epg):
        run_expert(ee)

    @pl.when(step == last)
    def _():
        for back in range(ring, 0, -1):
            @pl.when(nused >= back)
            def _():
                y_copy(nused - back, jnp.bitwise_and(nused - back, ring - 1)).wait()
        ybuf[0] = jnp.zeros(ybuf.shape[1:], ybuf.dtype)
        lax.fori_loop(nused, nblk, lambda b, c: (y_copy(b, 0).start(), c)[1], 0)
        lax.fori_loop(nused, nblk, lambda b, c: (y_copy(b, 0).wait(), c)[1], 0)


def _gmm(bstart, nb_e, counts, nused, xs, w_gate, w_up, w_down):
    P, W = xs.shape
    E, D, F = w_gate.shape
    bm = GMM_BLOCK
    epg = GMM_EXPERTS_PER_STEP
    wsel = lambda s, *_: (s, 0, 0)
    grid_spec = pltpu.PrefetchScalarGridSpec(
        num_scalar_prefetch=4,
        grid=(E // epg,),
        in_specs=[pl.BlockSpec((epg, D, F), wsel), pl.BlockSpec((epg, D, F), wsel), pl.BlockSpec((epg, F, D), wsel),
                  pl.BlockSpec(memory_space=pl.ANY)],
        out_specs=pl.BlockSpec(memory_space=pl.ANY),
        scratch_shapes=[pltpu.VMEM((GMM_RING, bm, W), xs.dtype), pltpu.VMEM((GMM_RING, bm, W), xs.dtype),
                        pltpu.SemaphoreType.DMA((GMM_RING,)), pltpu.SemaphoreType.DMA((GMM_RING,))],
    )
    return pl.pallas_call(
        _gmm_kernel,
        grid_spec=grid_spec,
        out_shape=jax.ShapeDtypeStruct((P, W), xs.dtype),
        compiler_params=pltpu.CompilerParams(dimension_semantics=("arbitrary",)),
        name="gmm",
    )(bstart, nb_e, counts, nused, w_gate, w_up, w_down, xs)


def _sc_gather(dest_c, ys):
    nch, K, C = dest_c.shape
    W = ys.shape[1]
    H = C // 2
    info = plsc.get_sparse_core_info()
    nw = info.num_cores * info.num_subcores
    per_w = nch // nw
    nbuf = 3
    mesh = plsc.VectorSubcoreMesh(core_axis_name="c", subcore_axis_name="s")
    items = [(kk, hh) for kk in range(K) for hh in range(2)]

    @functools.partial(
        pl.kernel, mesh=mesh,
        out_type=jax.ShapeDtypeStruct((K, nch * C, W), ys.dtype),
        scratch_types=([pltpu.VMEM((K, C), I32)] + [pltpu.VMEM((H, W), ys.dtype)] * nbuf
                       + [pltpu.SemaphoreType.DMA] * (2 * nbuf)),
        name="sc_gather",
    )
    def k(dest_hbm, ys_hbm, yk_hbm, idx_v, *rest):
        bufs, gsem, wsem = rest[:nbuf], rest[nbuf:2 * nbuf], rest[2 * nbuf:]
        wid = lax.axis_index("s") * info.num_cores + lax.axis_index("c")

        @pl.loop(0, per_w)
        def _(j):
            ch = wid * per_w + j
            pltpu.sync_copy(dest_hbm.at[ch], idx_v)

            def gather(i):
                kk, hh = items[i]
                return pltpu.async_copy(ys_hbm.at[idx_v.at[kk, pl.ds(hh * H, H)]], bufs[i % nbuf], gsem[i % nbuf])

            def write(i):
                kk, hh = items[i]
                return pltpu.async_copy(bufs[i % nbuf], yk_hbm.at[kk, pl.ds(ch * C + hh * H, H)], wsem[i % nbuf])

            n = len(items)
            g = {0: gather(0), 1: gather(1)}
            w = {}
            for i in range(n):
                g[i].wait()
                w[i] = write(i)
                if i + 2 < n:
                    if i >= 1:
                        w.pop(i - 1).wait()
                    g[i + 2] = gather(i + 2)
            for i in sorted(w):
                w[i].wait()

    return k(dest_c, ys)


def _combine_kernel(yk_ref, gt_ref, gate2_ref, xacc_ref, gfin_ref, o_ref):
    gt = gt_ref[...]
    hi_mask = jnp.uint32(0xFFFF0000)
    acc_lo = acc_hi = None
    for k in range(TOP_K):
        p = yk_ref[k]
        g = gt[:, k:k + 1]
        lo = lax.bitcast_convert_type(p << 16, F32) * g
        hi = lax.bitcast_convert_type(p & hi_mask, F32) * g
        acc_lo = lo if k == 0 else acc_lo + lo
        acc_hi = hi if k == 0 else acc_hi + hi
    routed = jnp.concatenate([acc_lo, acc_hi], axis=1)
    x2 = xacc_ref[...] + gate2_ref[0] * routed
    o_ref[...] = _rms(x2) * gfin_ref[...]


def _combine(yk, gates_t, gate2, xacc, g_final, S):
    N, D = xacc.shape
    W = yk.shape[2]
    T = COMBINE_TILE
    spt = S // T
    return pl.pallas_call(
        _combine_kernel,
        grid=(N // T,),
        in_specs=[pl.BlockSpec((TOP_K, T, W), lambda i: (0, i, 0)),
                  pl.BlockSpec((T, TOP_K), lambda i: (i, 0)),
                  pl.BlockSpec((1, 1, D), lambda i: (i // spt, 0, 0)),
                  pl.BlockSpec((T, D), lambda i: (i, 0)),
                  pl.BlockSpec((1, D), lambda i: (0, 0))],
        out_specs=pl.BlockSpec((T, D), lambda i: (i, 0)),
        out_shape=jax.ShapeDtypeStruct((N, D), F32),
        name="combine",
    )(yk, gates_t, gate2, xacc, g_final)


def _layer(xf, B, S, mod, g_mix, w_in, pool_w, pool_scale, w_out, g_ffn, w_router, router_bias,
           w_gate, w_up, w_down, ws_gate, ws_up, ws_down):
    N, D = xf.shape
    E = w_router.shape[1]
    pw = pool_scale.shape[0]
    shift1, scale1, gate1, shift2, scale2, gate2 = [m.reshape(B, 1, D) for m in jnp.split(mod, 6, axis=-1)]
    u, qkv = _in_proj(xf, g_mix.reshape(1, D), shift1, scale1, w_in.astype(BF16), S, pw)
    attn_outs = []
    for a, d in zip(qkv, ATT_DILATIONS):
        o, lse = _attention(a, S // d // ATT_BLOCK)
        shape = (N, ATT_GROUP_WIDTH) if d == 1 else (B * d, S // d, ATT_GROUP_WIDTH)
        attn_outs += [o.reshape(shape), lse.reshape(shape)]
    ng = pool_w.shape[0]
    pool_bd = jnp.einsum('gcd,gh->gchd', pool_w, jnp.eye(ng, dtype=pool_w.dtype)).reshape(pw, pw).astype(BF16)
    xacc, h2, eidx, gates, rank, counts = _mid(
        u, attn_outs, xf, pool_bd, pool_scale.reshape(1, pw), w_out.astype(BF16), gate1,
        g_ffn.reshape(1, D), shift2, scale2, gate2,
        ws_gate.astype(BF16), ws_up.astype(BF16), ws_down.astype(BF16),
        w_router.T.astype(BF16), router_bias.reshape(E, 1).astype(F32), S)
    bm = GMM_BLOCK
    nblk = N * TOP_K // bm + E
    nb_e = (counts[:, 0] + bm - 1) // bm
    bend = jnp.cumsum(nb_e)
    bstart = (bend - nb_e).astype(I32)
    nused = bend[-1:].astype(I32)
    dest_c = _dest(eidx, rank, (bstart * bm).astype(F32).reshape(E, 1))
    xs = _sc_dispatch(dest_c, h2, nblk * bm)
    ys = _gmm(bstart, nb_e.astype(I32), counts[:, 0], nused, xs, w_gate, w_up, w_down)
    return _sc_gather(dest_c, ys), gates.T, gate2, xacc


def kernel(x, c, w_ada, b_ada, g_mix, w_in, pool_w, pool_scale, w_out, g_ffn, w_router, router_bias,
           w_gate, w_up, w_down, ws_gate, ws_up, ws_down, g_final):
    B, S, D = x.shape
    depth = w_ada.shape[0]
    assert depth == 1, "the final residual is fused with the final norm, so exactly one layer is supported"
    assert S % (ATT_DILATIONS[-1] * ATT_BLOCK) == 0 and S % max(IN_TILE, MID_TILE, ATT_BLOCKS_PER_STEP * ATT_BLOCK) == 0
    xf = x.reshape(B * S, D)
    mod = _ada(c, w_ada[0], b_ada[0])
    yk, gates_t, gate2, xacc = _layer(
        xf, B, S, mod, g_mix[0], w_in[0], pool_w[0], pool_scale[0], w_out[0], g_ffn[0], w_router[0],
        router_bias[0], w_gate[0], w_up[0], w_down[0], ws_gate[0], ws_up[0], ws_down[0])
    out = _combine(yk, gates_t, gate2, xacc, g_final.reshape(1, D), S)
    return out.reshape(B, S, D)
```

```python
import functools

import jax
import jax.numpy as jnp
from jax import lax
from jax.experimental import pallas as pl
from jax.experimental.pallas import tpu as pltpu
from jax.experimental.pallas import tpu_sc as plsc

F32 = jnp.float32
BF16 = jnp.bfloat16
I32 = jnp.int32
U32 = jnp.uint32

LANES = 128
SINGLE_LOAD_STRIDE = 4
NORM_EPS = 1e-6
POOL_WINDOWS = (2, 4, 8, 16)
POOL_HALO = 16
ATT_DILATIONS = (1, 4, 16)
ATT_BLOCK = 128
ATT_HEADS_PER_GROUP = 4
ATT_HEAD_DIM = 64
ATT_GROUP_WIDTH = ATT_HEADS_PER_GROUP * ATT_HEAD_DIM
N_EXPERT_GROUPS = 8
TOPK_GROUPS = 4
TOP_K = 8
ROUTED_SCALE = 2.5

IN_TILE = 1024
ATT_BLOCKS_PER_STEP = 16
ATT_UNROLL = 8
MID_TILE = 512
ROUTE_TILE = 256
COMBINE_TILE = 512
GMM_BLOCK = 128
GMM_RING = 16
GMM_EXPERTS_PER_STEP = 1
SC_CHUNK = 128

NEG_INF = float("-inf")


def _sigmoid(v):
    return 1.0 / (1.0 + jnp.exp(-v))


def _rms(v):
    return v * lax.rsqrt(jnp.mean(v * v, axis=-1, keepdims=True) + NORM_EPS)


def _pack_bf16_pairs(v):
    n = v.shape[1] // 2
    lo = lax.bitcast_convert_type(v[:, :n].astype(BF16).astype(F32), U32)
    hi = lax.bitcast_convert_type(v[:, n:].astype(BF16).astype(F32), U32)
    return (hi & jnp.uint32(0xFFFF0000)) | (lo >> 16)


def _unpack_bf16_pairs(p):
    lo = lax.bitcast_convert_type(p << 16, F32).astype(BF16)
    hi = lax.bitcast_convert_type(p & jnp.uint32(0xFFFF0000), F32).astype(BF16)
    return lo, hi


def _ada_kernel(c_ref, w_ref, b_ref, o_ref):
    c = c_ref[...]
    cs = c * _sigmoid(c)
    o_ref[...] = jnp.dot(cs, w_ref[...], preferred_element_type=F32,
                         precision=lax.Precision.HIGHEST) + b_ref[...]


def _ada(c, w_ada, b_ada):
    B, D = c.shape
    W = w_ada.shape[1]
    tn = 1024
    return pl.pallas_call(
        _ada_kernel,
        grid=(W // tn,),
        in_specs=[pl.BlockSpec((B, D), lambda j: (0, 0)),
                  pl.BlockSpec((D, tn), lambda j: (0, j)),
                  pl.BlockSpec((1, tn), lambda j: (0, j))],
        out_specs=pl.BlockSpec((B, tn), lambda j: (0, j)),
        out_shape=jax.ShapeDtypeStruct((B, W), F32),
        name="ada",
    )(c, w_ada, b_ada.reshape(1, W))


def _in_kernel(x_ref, g_ref, sh_ref, sc_ref, w_ref, pool_ref, q0_ref, q1_ref, q2_ref, scr_ref, tmp_ref):
    h = _rms(x_ref[...]) * g_ref[...]
    h = h * (1.0 + sc_ref[0]) + sh_ref[0]
    hb = h.astype(BF16)
    tm = x_ref.shape[0]
    pw = pool_ref.shape[1]
    gw = ATT_GROUP_WIDTH
    pool_ref[...] = jnp.dot(hb, w_ref[:, 0:pw], preferred_element_type=F32)
    for g, (out, d) in enumerate(zip((q0_ref, q1_ref, q2_ref), ATT_DILATIONS)):
        for sec in range(3):
            c0 = pw + sec * 3 * gw + g * gw
            res = jnp.dot(hb, w_ref[:, c0:c0 + gw], preferred_element_type=F32)
            if d == 1:
                out[:, sec * gw:(sec + 1) * gw] = res.astype(BF16)
            else:
                for c in range(gw // LANES):
                    scr_ref[c] = res[:, c * LANES:(c + 1) * LANES]
                    src, f1 = scr_ref.at[c], 1
                    if d > SINGLE_LOAD_STRIDE:
                        f1 = SINGLE_LOAD_STRIDE
                        for q in range(f1):
                            tmp_ref[c, q * (tm // f1):(q + 1) * (tm // f1), :] = scr_ref[c, pl.ds(q, tm // f1, stride=f1), :]
                        src = tmp_ref.at[c]
                    for r in range(d):
                        r_lo, r_hi = r % f1, r // f1
                        c1 = sec * gw + c * LANES
                        out[r, :, c1:c1 + LANES] = src[pl.ds(r_lo * (tm // f1) + r_hi, tm // d, stride=d // f1),
                                                       :].astype(BF16)


def _in_proj(xf, g_mix, shift1, scale1, w_in_b, S, pool_width):
    N, D = xf.shape
    tm = IN_TILE
    spt = S // tm
    vec = lambda i: (i // spt, 0, 0)
    row = lambda i: (i, 0)
    gw3 = 3 * ATT_GROUP_WIDTH
    B = N // S
    res_spec = lambda d: pl.BlockSpec((d, tm // d, gw3), lambda i: (i // spt, i % spt, 0))
    res_shape = lambda d: jax.ShapeDtypeStruct((B * d, S // d, gw3), BF16)
    outs = pl.pallas_call(
        _in_kernel,
        grid=(N // tm,),
        in_specs=[pl.BlockSpec((tm, D), row),
                  pl.BlockSpec((1, D), lambda i: (0, 0)),
                  pl.BlockSpec((1, 1, D), vec),
                  pl.BlockSpec((1, 1, D), vec),
                  pl.BlockSpec(w_in_b.shape, lambda i: (0, 0))],
        out_specs=[pl.BlockSpec((tm, pool_width), row), pl.BlockSpec((tm, gw3), row)]
                  + [res_spec(d) for d in ATT_DILATIONS[1:]],
        out_shape=[jax.ShapeDtypeStruct((N, pool_width), F32), jax.ShapeDtypeStruct((N, gw3), BF16)]
                  + [res_shape(d) for d in ATT_DILATIONS[1:]],
        scratch_shapes=[pltpu.VMEM((ATT_GROUP_WIDTH // LANES, tm, LANES), F32)] * 2,
        name="in_proj",
    )(xf, g_mix, shift1, scale1, w_in_b)
    return outs[0], [o.reshape(N, gw3) for o in outs[1:]]


def _attn_kernel(nbs, a_ref, halo_ref, o_ref, lse_ref, kv_ref, band_ref):
    i = pl.program_id(0)
    R = a_ref.shape[0] // ATT_BLOCK
    gw = ATT_GROUP_WIDTH
    blk = ATT_BLOCK
    nh = ATT_HEADS_PER_GROUP
    kv_ref[0:blk, :] = halo_ref[:, gw:3 * gw]
    kv_ref[blk:, :] = a_ref[:, gw:3 * gw]
    row = lax.broadcasted_iota(I32, (nh * blk, 2 * blk), 0) % blk
    col = lax.broadcasted_iota(I32, (nh * blk, 2 * blk), 1)
    in_band = (col >= row) & (col <= row + blk)
    band_ref[0] = jnp.where(in_band, 0.0, NEG_INF)
    band_ref[1] = jnp.where(in_band & (col >= blk), 0.0, NEG_INF)
    head_of_lane = lax.broadcasted_iota(I32, (blk, gw), 1) // ATT_HEAD_DIM
    nt = (((1,), (1,)), ((), ()))

    def one_block(jj, start):
        r0 = pl.multiple_of(jj * blk, blk)
        qf = a_ref[pl.ds(r0, blk), 0:gw].astype(F32) * (ATT_HEAD_DIM ** -0.5)
        q4 = jnp.concatenate([jnp.where(head_of_lane == h, qf, 0.0) for h in range(nh)], axis=0).astype(BF16)
        kc = kv_ref[pl.ds(r0, 2 * blk), 0:gw]
        vc = kv_ref[pl.ds(r0, 2 * blk), gw:2 * gw]
        s = lax.dot_general(q4, kc, nt, preferred_element_type=F32) + band_ref[1 if start is True else 0]
        if start is not None and start is not True:
            s = jnp.where(col >= jnp.where(start, blk, 0), s, NEG_INF)
        m = jnp.max(s, axis=1, keepdims=True)
        p = jnp.exp(s - m)
        l = jnp.sum(p, axis=1, keepdims=True)
        o4 = jnp.dot(p.astype(BF16), vc, preferred_element_type=F32) / l
        lse4 = m + jnp.log(l)
        o = jnp.zeros((blk, gw), F32)
        lse = jnp.zeros((blk, gw), F32)
        for h in range(nh):
            hm = head_of_lane == h
            o = jnp.where(hm, o4[h * blk:(h + 1) * blk, :], o)
            lse = jnp.where(hm, lse4[h * blk:(h + 1) * blk, :], lse)
        o_ref[pl.ds(r0, blk), :] = o
        lse_ref[pl.ds(r0, blk), :] = lse

    U = ATT_UNROLL
    assert U % nbs == 0 or nbs % U == 0

    def body(it, carry):
        for j in range(U):
            if U % nbs == 0:
                start = True if j % nbs == 0 else None
            else:
                start = (((i * R + it * U) % nbs) == 0) if j == 0 else None
            one_block(it * U + j, start)
        return carry

    lax.fori_loop(0, R // U, body, 0)


def _attention(a, nbs):
    N = a.shape[0]
    R = ATT_BLOCKS_PER_STEP
    gw = ATT_GROUP_WIDTH
    tm = R * ATT_BLOCK
    return pl.pallas_call(
        functools.partial(_attn_kernel, nbs),
        grid=(N // tm,),
        in_specs=[pl.BlockSpec((tm, 3 * gw), lambda i: (i, 0)),
                  pl.BlockSpec((ATT_BLOCK, 3 * gw), lambda i: (jnp.maximum(i * R - 1, 0), 0))],
        out_specs=[pl.BlockSpec((tm, gw), lambda i: (i, 0))] * 2,
        out_shape=[jax.ShapeDtypeStruct((N, gw), F32)] * 2,
        scratch_shapes=[pltpu.VMEM((tm + ATT_BLOCK, 2 * gw), BF16),
                        pltpu.VMEM((2, ATT_HEADS_PER_GROUP * ATT_BLOCK, 2 * ATT_BLOCK), F32)],
        name="attn",
    )(a, a)


def _mid_kernel(spt, u_ref, uh_ref, o0_ref, l0_ref, o1_ref, l1_ref, o2_ref, l2_ref, x_ref,
                pbd_ref, psc_ref, wout_ref, gate1_ref, gffn_ref, sh2_ref, sc2_ref, gate2_ref,
                wsg_ref, wsu_ref, wsd_ref, wr_ref, bias_ref, before_ref,
                xacc_ref, h2_ref, e_ref, g_ref, r_ref, cnt_ref, til_ref, base_ref):
    i = pl.program_id(0)
    tm, pw = u_ref.shape
    si = i % spt
    u = u_ref[...]
    keep = jnp.full((POOL_HALO, pw), si, I32) > 0
    ext = jnp.concatenate([jnp.where(keep, uh_ref[...], 0.0), u], axis=0)
    lane_grp = lax.broadcasted_iota(I32, (tm, pw), 1) // (pw // len(POOL_WINDOWS))
    pooled = jnp.zeros((tm, pw), F32)
    s, w = ext, 1
    while w < POOL_HALO:
        s = s + pltpu.roll(s, w, axis=0)
        w *= 2
        if w in POOL_WINDOWS:
            pooled = jnp.where(lane_grp == POOL_WINDOWS.index(w), s[POOL_HALO:, :], pooled)
    win = jnp.zeros((tm, pw), I32)
    for g, w in enumerate(POOL_WINDOWS):
        win = jnp.where(lane_grp == g, w, win)
    pos = si * tm + lax.broadcasted_iota(I32, (tm, pw), 0)
    cnt = jnp.minimum(pos + 1, win).astype(F32)
    pooled = pooled / cnt - u
    pool_out = jnp.dot(pooled.astype(BF16), pbd_ref[...], preferred_element_type=F32) * psc_ref[...]
    def token_order(slot, ref):
        d, n, w = ref.shape
        for r in range(d):
            for c in range(w // LANES):
                til_ref[slot, c, pl.ds(r, n, stride=d), :] = ref[r, :, c * LANES:(c + 1) * LANES]
        return jnp.concatenate([til_ref[slot, c] for c in range(w // LANES)], axis=1)

    l0 = l0_ref[...]
    l1 = token_order(0, l1_ref)
    l2 = token_order(1, l2_ref)
    m = jnp.maximum(jnp.maximum(l0, l1), l2)
    w0 = jnp.exp(l0 - m)
    w1 = jnp.exp(l1 - m)
    w2 = jnp.exp(l2 - m)
    attn = (w0 * o0_ref[...] + w1 * token_order(2, o1_ref) + w2 * token_order(3, o2_ref)) / (w0 + w1 + w2)
    mixed = (jnp.dot(pool_out.astype(BF16), wout_ref[0:pw, :], preferred_element_type=F32)
             + jnp.dot(attn.astype(BF16), wout_ref[pw:, :], preferred_element_type=F32))
    x1 = x_ref[...] + gate1_ref[0] * mixed
    h2 = _rms(x1) * gffn_ref[...]
    h2 = h2 * (1.0 + sc2_ref[0]) + sh2_ref[0]
    h2_ref[...] = _pack_bf16_pairs(h2)
    hb = h2.astype(BF16)
    a = jnp.dot(hb, wsg_ref[...], preferred_element_type=F32)
    b = jnp.dot(hb, wsu_ref[...], preferred_element_type=F32)
    act = (a * _sigmoid(a)) * b
    shared = jnp.dot(act.astype(BF16), wsd_ref[...], preferred_element_type=F32)
    xacc_ref[...] = x1 + gate2_ref[0] * shared
    @pl.when(i == 0)
    def _():
        base_ref[...] = jnp.zeros_like(base_ref)

    C = e_ref.shape[2]
    for t0 in range(0, tm, ROUTE_TILE):
        idxs, gates, ranks = _route_tile(hb[t0:t0 + ROUTE_TILE, :], wr_ref, bias_ref, before_ref, base_ref)
        for k in range(TOP_K):
            g_ref[k:k + 1, t0:t0 + ROUTE_TILE] = gates[k]
            for c in range(ROUTE_TILE // C):
                e_ref[t0 // C + c, k:k + 1, :] = idxs[k][:, c * C:(c + 1) * C]
                r_ref[t0 // C + c, k:k + 1, :] = ranks[k][:, c * C:(c + 1) * C]
    cnt_ref[...] = base_ref[...].astype(I32)


def _mid(u, attn_outs, xf, pool_bd, pool_scale, w_out_b, gate1, g_ffn, shift2, scale2, gate2,
         wsg_b, wsu_b, wsd_b, wr_t, bias_col, S):
    N, D = xf.shape
    E = wr_t.shape[0]
    tok = jnp.arange(ROUTE_TILE, dtype=I32)
    before = (tok[:, None] < tok[None, :]).astype(BF16)
    pw = u.shape[1]
    tm = MID_TILE
    spt = S // tm
    row = lambda i: (i, 0)
    vec = lambda i: (i // spt, 0, 0)
    full = lambda a: pl.BlockSpec(a.shape, lambda i: (0,) * a.ndim)
    hpt = tm // POOL_HALO
    in_specs = [pl.BlockSpec((tm, pw), row),
                pl.BlockSpec((POOL_HALO, pw), lambda i: (jnp.maximum(i * hpt - 1, 0), 0))]
    gw = ATT_GROUP_WIDTH
    in_specs += [pl.BlockSpec((tm, gw), row)] * 2
    for d in ATT_DILATIONS[1:]:
        in_specs += [pl.BlockSpec((d, tm // d, gw), lambda i: (i // spt, i % spt, 0))] * 2
    in_specs += [pl.BlockSpec((tm, D), row), full(pool_bd), full(pool_scale), full(w_out_b),
                 pl.BlockSpec((1, 1, D), vec), full(g_ffn), pl.BlockSpec((1, 1, D), vec),
                 pl.BlockSpec((1, 1, D), vec), pl.BlockSpec((1, 1, D), vec),
                 full(wsg_b), full(wsu_b), full(wsd_b), full(wr_t), full(bias_col), full(before)]
    col = lambda i: (0, i)
    C = SC_CHUNK
    chunked = pl.BlockSpec((tm // C, TOP_K, C), lambda i: (i, 0, 0))
    return pl.pallas_call(
        functools.partial(_mid_kernel, spt),
        grid=(N // tm,),
        in_specs=in_specs,
        out_specs=[pl.BlockSpec((tm, D), row), pl.BlockSpec((tm, D // 2), row),
                   chunked, pl.BlockSpec((TOP_K, tm), col), chunked,
                   pl.BlockSpec((E, 1), lambda i: (0, 0))],
        out_shape=[jax.ShapeDtypeStruct((N, D), F32), jax.ShapeDtypeStruct((N, D // 2), U32),
                   jax.ShapeDtypeStruct((N // C, TOP_K, C), I32), jax.ShapeDtypeStruct((TOP_K, N), F32),
                   jax.ShapeDtypeStruct((N // C, TOP_K, C), I32), jax.ShapeDtypeStruct((E, 1), I32)],
        scratch_shapes=[pltpu.VMEM((4, gw // LANES, tm, LANES), F32), pltpu.VMEM((E, 1), F32)],
        compiler_params=pltpu.CompilerParams(dimension_semantics=("arbitrary",)),
        name="mid",
    )(u, u, *attn_outs, xf, pool_bd, pool_scale, w_out_b, gate1, g_ffn, shift2, scale2, gate2,
      wsg_b, wsu_b, wsd_b, wr_t, bias_col, before)


def _route_tile(hb, wr_ref, bias_ref, before_ref, base_ref):
    T = hb.shape[0]
    E = wr_ref.shape[0]
    gsz = E // N_EXPERT_GROUPS
    logits = lax.dot_general(wr_ref[...], hb, (((1,), (1,)), ((), ())), preferred_element_type=F32)
    scores = _sigmoid(logits)
    biased = scores + bias_ref[...]
    giota = lax.broadcasted_iota(I32, (gsz, T), 0)
    gscore = []
    for g in range(N_EXPERT_GROUPS):
        blk = biased[g * gsz:(g + 1) * gsz, :]
        m1 = jnp.max(blk, axis=0, keepdims=True)
        i1 = jnp.min(jnp.where(blk == m1, giota, gsz), axis=0, keepdims=True)
        m2 = jnp.max(jnp.where(giota == i1, NEG_INF, blk), axis=0, keepdims=True)
        gscore.append(m1 + m2)
    parts = []
    for g in range(N_EXPERT_GROUPS):
        beaten = jnp.zeros((1, T), I32)
        for o in range(N_EXPERT_GROUPS):
            if o == g:
                continue
            wins = (gscore[o] >= gscore[g]) if o < g else (gscore[o] > gscore[g])
            beaten = beaten + wins.astype(I32)
        keep = jnp.broadcast_to(beaten, (gsz, T)) < TOPK_GROUPS
        parts.append(jnp.where(keep, biased[g * gsz:(g + 1) * gsz, :], NEG_INF))
    cur = jnp.concatenate(parts, axis=0)
    eiota = lax.broadcasted_iota(I32, (E, T), 0)
    live = cur > NEG_INF
    idxs, gates = [], []
    for k in range(TOP_K):
        m = jnp.max(cur, axis=0, keepdims=True)
        idx = jnp.min(jnp.where(cur == m, eiota, E), axis=0, keepdims=True)
        oh = eiota == idx
        gates.append(jnp.sum(jnp.where(oh, scores, 0.0), axis=0, keepdims=True))
        idxs.append(idx)
        cur = jnp.where(oh, NEG_INF, cur)
    selm = jnp.where(live & (cur == NEG_INF), 1.0, 0.0)
    gsum = gates[0]
    for k in range(1, TOP_K):
        gsum = gsum + gates[k]
    gates = [gk / gsum * ROUTED_SCALE for gk in gates]
    tot = jnp.dot(selm.astype(BF16), before_ref[...], preferred_element_type=F32) + base_ref[...]
    ranks = [jnp.sum(jnp.where(eiota == idxs[k], tot, 0.0), axis=0, keepdims=True).astype(I32)
             for k in range(TOP_K)]
    base_ref[...] = base_ref[...] + jnp.sum(selm, axis=1, keepdims=True)
    return idxs, gates, ranks


def _sc_dispatch(eidx_c, rank_c, offs, h2p, P):
    N, W = h2p.shape
    nch, K, C = eidx_c.shape
    E = offs.shape[0]
    info = plsc.get_sparse_core_info()
    nw = info.num_cores * info.num_subcores
    L = info.num_lanes
    per_w = nch // nw
    mesh = plsc.VectorSubcoreMesh(core_axis_name="c", subcore_axis_name="s")

    @functools.partial(
        pl.kernel, mesh=mesh,
        out_type=[jax.ShapeDtypeStruct((P, W), h2p.dtype), jax.ShapeDtypeStruct((nch, K, C), I32)],
        scratch_types=[pltpu.VMEM((E,), I32), pltpu.VMEM((K, C), I32), pltpu.VMEM((K, C), I32),
                       pltpu.VMEM((K, C), I32), pltpu.VMEM((C, W), h2p.dtype), pltpu.SemaphoreType.DMA],
        compiler_params=pltpu.CompilerParams(needs_layout_passes=False),
        name="sc_dispatch",
    )
    def k(e_hbm, r_hbm, off_hbm, h_hbm, xs_hbm, dest_hbm, off_v, e_v, r_v, idx_v, rows_v, sem):
        wid = lax.axis_index("s") * info.num_cores + lax.axis_index("c")
        pltpu.sync_copy(off_hbm, off_v)

        @pl.loop(0, per_w)
        def _(j):
            ch = wid * per_w + j
            pltpu.sync_copy(e_hbm.at[ch], e_v)
            pltpu.sync_copy(r_hbm.at[ch], r_v)
            for kk in range(K):
                for q in range(C // L):
                    sl = pl.ds(q * L, L)
                    idx_v[kk, sl] = plsc.load_gather(off_v, [e_v[kk, sl]]) + r_v[kk, sl]
            pltpu.sync_copy(idx_v, dest_hbm.at[ch])
            pltpu.sync_copy(h_hbm.at[pl.ds(ch * C, C)], rows_v)
            copies = [pltpu.async_copy(rows_v, xs_hbm.at[idx_v.at[kk]], sem) for kk in range(K)]
            for cp in copies:
                cp.wait()

    return k(eidx_c, rank_c, offs, h2p)


def _gmm_kernel(bstart_ref, nbe_ref, cnt_ref, nu_ref, wg_ref, wu_ref, wd_ref, xs_hbm, ys_hbm,
                xbuf, ybuf, xsem, ysem):
    step = pl.program_id(0)
    last = pl.num_programs(0) - 1
    epg = wg_ref.shape[0]
    ring, bm = xbuf.shape[0], xbuf.shape[1]
    nblk = ys_hbm.shape[0] // bm
    nused = nu_ref[0]

    def x_copy(b, slot):
        return pltpu.make_async_copy(xs_hbm.at[pl.ds(pl.multiple_of(b * bm, bm), bm), :], xbuf.at[slot],
                                     xsem.at[slot])

    def y_copy(b, slot):
        return pltpu.make_async_copy(ybuf.at[slot], ys_hbm.at[pl.ds(pl.multiple_of(b * bm, bm), bm), :],
                                     ysem.at[slot])

    @pl.when(step == 0)
    def _():
        for j in range(ring - 1):
            @pl.when(j < nused)
            def _():
                x_copy(j, j).start()

    def run_expert(ee):
        e = step * epg + ee
        b0 = bstart_ref[e]
        nb = nbe_ref[e]

        @pl.when(nb > 0)
        def _():
            def prefetch(t):
                @pl.when(t < nused)
                def _():
                    x_copy(t, jnp.bitwise_and(t, ring - 1)).start()

            def process(b, n):
                slots = [jnp.bitwise_and(b + j, ring - 1) for j in range(n)]
                for j in range(n):
                    x_copy(b + j, slots[j]).wait()
                prefetch(b + ring - 1)
                for j in range(n):
                    @pl.when(b + j >= ring)
                    def _():
                        y_copy(b + j - ring, slots[j]).wait()
                rows = lax.broadcasted_iota(I32, (n * bm, 1), 0)
                valid = cnt_ref[e] - (b - b0) * bm
                xp = jnp.concatenate([xbuf[s] for s in slots], axis=0)
                xb = jnp.concatenate(_unpack_bf16_pairs(jnp.where(rows < valid, xp, jnp.uint32(0))), axis=1)
                a = jnp.dot(xb, wg_ref[ee].astype(BF16), preferred_element_type=F32)
                g = jnp.dot(xb, wu_ref[ee].astype(BF16), preferred_element_type=F32)
                act = (a * _sigmoid(a)) * g
                yp = _pack_bf16_pairs(jnp.dot(act.astype(BF16), wd_ref[ee].astype(BF16),
                                              preferred_element_type=F32))
                for j in range(n):
                    ybuf[slots[j]] = yp[j * bm:(j + 1) * bm, :]
                    y_copy(b + j, slots[j]).start()
                for j in range(1, n):
                    prefetch(b + ring - 1 + j)

            quads = lax.shift_right_logical(nb, 2)
            lax.fori_loop(0, quads, lambda j, c: (process(b0 + 4 * j, 4), c)[1], 0)
            rest2 = jnp.bitwise_and(nb, 2)

            @pl.when(rest2 != 0)
            def _():
                process(b0 + 4 * quads, 2)

            @pl.when(jnp.bitwise_and(nb, 1) != 0)
            def _():
                process(b0 + 4 * quads + rest2, 1)

    for ee in range(epg):
        run_expert(ee)

    @pl.when(step == last)
    def _():
        for back in range(ring, 0, -1):
            @pl.when(nused >= back)
            def _():
                y_copy(nused - back, jnp.bitwise_and(nused - back, ring - 1)).wait()
        ybuf[0] = jnp.zeros(ybuf.shape[1:], ybuf.dtype)
        lax.fori_loop(nused, nblk, lambda b, c: (y_copy(b, 0).start(), c)[1], 0)
        lax.fori_loop(nused, nblk, lambda b, c: (y_copy(b, 0).wait(), c)[1], 0)


def _gmm(bstart, nb_e, counts, nused, xs, w_gate, w_up, w_down):
    P, W = xs.shape
    E, D, F = w_gate.shape
    bm = GMM_BLOCK
    epg = GMM_EXPERTS_PER_STEP
    wsel = lambda s, *_: (s, 0, 0)
    grid_spec = pltpu.PrefetchScalarGridSpec(
        num_scalar_prefetch=4,
        grid=(E // epg,),
        in_specs=[pl.BlockSpec((epg, D, F), wsel), pl.BlockSpec((epg, D, F), wsel), pl.BlockSpec((epg, F, D), wsel),
                  pl.BlockSpec(memory_space=pl.ANY)],
        out_specs=pl.BlockSpec(memory_space=pl.ANY),
        scratch_shapes=[pltpu.VMEM((GMM_RING, bm, W), xs.dtype), pltpu.VMEM((GMM_RING, bm, W), xs.dtype),
                        pltpu.SemaphoreType.DMA((GMM_RING,)), pltpu.SemaphoreType.DMA((GMM_RING,))],
    )
    return pl.pallas_call(
        _gmm_kernel,
        grid_spec=grid_spec,
        out_shape=jax.ShapeDtypeStruct((P, W), xs.dtype),
        compiler_params=pltpu.CompilerParams(dimension_semantics=("arbitrary",)),
        name="gmm",
    )(bstart, nb_e, counts, nused, w_gate, w_up, w_down, xs)


def _sc_gather(dest_c, ys):
    nch, K, C = dest_c.shape
    W = ys.shape[1]
    H = C // 2
    info = plsc.get_sparse_core_info()
    nw = info.num_cores * info.num_subcores
    per_w = nch // nw
    nbuf = 3
    mesh = plsc.VectorSubcoreMesh(core_axis_name="c", subcore_axis_name="s")
    items = [(kk, hh) for kk in range(K) for hh in range(2)]

    @functools.partial(
        pl.kernel, mesh=mesh,
        out_type=jax.ShapeDtypeStruct((K, nch * C, W), ys.dtype),
        scratch_types=([pltpu.VMEM((K, C), I32)] + [pltpu.VMEM((H, W), ys.dtype)] * nbuf
                       + [pltpu.SemaphoreType.DMA] * (2 * nbuf)),
        name="sc_gather",
    )
    def k(dest_hbm, ys_hbm, yk_hbm, idx_v, *rest):
        bufs, gsem, wsem = rest[:nbuf], rest[nbuf:2 * nbuf], rest[2 * nbuf:]
        wid = lax.axis_index("s") * info.num_cores + lax.axis_index("c")

        @pl.loop(0, per_w)
        def _(j):
            ch = wid * per_w + j
            pltpu.sync_copy(dest_hbm.at[ch], idx_v)

            def gather(i):
                kk, hh = items[i]
                return pltpu.async_copy(ys_hbm.at[idx_v.at[kk, pl.ds(hh * H, H)]], bufs[i % nbuf], gsem[i % nbuf])

            def write(i):
                kk, hh = items[i]
                return pltpu.async_copy(bufs[i % nbuf], yk_hbm.at[kk, pl.ds(ch * C + hh * H, H)], wsem[i % nbuf])

            n = len(items)
            g = {0: gather(0), 1: gather(1)}
            w = {}
            for i in range(n):
                g[i].wait()
                w[i] = write(i)
                if i + 2 < n:
                    if i >= 1:
                        w.pop(i - 1).wait()
                    g[i + 2] = gather(i + 2)
            for i in sorted(w):
                w[i].wait()

    return k(dest_c, ys)


def _combine_kernel(yk_ref, gt_ref, gate2_ref, xacc_ref, gfin_ref, o_ref):
    gt = gt_ref[...]
    hi_mask = jnp.uint32(0xFFFF0000)
    acc_lo = acc_hi = None
    for k in range(TOP_K):
        p = yk_ref[k]
        g = gt[:, k:k + 1]
        lo = lax.bitcast_convert_type(p << 16, F32) * g
        hi = lax.bitcast_convert_type(p & hi_mask, F32) * g
        acc_lo = lo if k == 0 else acc_lo + lo
        acc_hi = hi if k == 0 else acc_hi + hi
    routed = jnp.concatenate([acc_lo, acc_hi], axis=1)
    x2 = xacc_ref[...] + gate2_ref[0] * routed
    o_ref[...] = _rms(x2) * gfin_ref[...]


def _combine(yk, gates_t, gate2, xacc, g_final, S):
    N, D = xacc.shape
    W = yk.shape[2]
    T = COMBINE_TILE
    spt = S // T
    return pl.pallas_call(
        _combine_kernel,
        grid=(N // T,),
        in_specs=[pl.BlockSpec((TOP_K, T, W), lambda i: (0, i, 0)),
                  pl.BlockSpec((T, TOP_K), lambda i: (i, 0)),
                  pl.BlockSpec((1, 1, D), lambda i: (i // spt, 0, 0)),
                  pl.BlockSpec((T, D), lambda i: (i, 0)),
                  pl.BlockSpec((1, D), lambda i: (0, 0))],
        out_specs=pl.BlockSpec((T, D), lambda i: (i, 0)),
        out_shape=jax.ShapeDtypeStruct((N, D), F32),
        name="combine",
    )(yk, gates_t, gate2, xacc, g_final)


def _layer(xf, B, S, mod, g_mix, w_in, pool_w, pool_scale, w_out, g_ffn, w_router, router_bias,
           w_gate, w_up, w_down, ws_gate, ws_up, ws_down):
    N, D = xf.shape
    E = w_router.shape[1]
    pw = pool_scale.shape[0]
    shift1, scale1, gate1, shift2, scale2, gate2 = [m.reshape(B, 1, D) for m in jnp.split(mod, 6, axis=-1)]
    u, qkv = _in_proj(xf, g_mix.reshape(1, D), shift1, scale1, w_in.astype(BF16), S, pw)
    attn_outs = []
    for a, d in zip(qkv, ATT_DILATIONS):
        o, lse = _attention(a, S // d // ATT_BLOCK)
        shape = (N, ATT_GROUP_WIDTH) if d == 1 else (B * d, S // d, ATT_GROUP_WIDTH)
        attn_outs += [o.reshape(shape), lse.reshape(shape)]
    ng = pool_w.shape[0]
    pool_bd = jnp.einsum('gcd,gh->gchd', pool_w, jnp.eye(ng, dtype=pool_w.dtype)).reshape(pw, pw).astype(BF16)
    xacc, h2, eidx, gates, rank, counts = _mid(
        u, attn_outs, xf, pool_bd, pool_scale.reshape(1, pw), w_out.astype(BF16), gate1,
        g_ffn.reshape(1, D), shift2, scale2, gate2,
        ws_gate.astype(BF16), ws_up.astype(BF16), ws_down.astype(BF16),
        w_router.T.astype(BF16), router_bias.reshape(E, 1).astype(F32), S)
    bm = GMM_BLOCK
    nblk = N * TOP_K // bm + E
    nb_e = (counts[:, 0] + bm - 1) // bm
    bend = jnp.cumsum(nb_e)
    bstart = (bend - nb_e).astype(I32)
    nused = bend[-1:].astype(I32)
    xs, dest_c = _sc_dispatch(eidx, rank, bstart * bm, h2, nblk * bm)
    ys = _gmm(bstart, nb_e.astype(I32), counts[:, 0], nused, xs, w_gate, w_up, w_down)
    return _sc_gather(dest_c, ys), gates.T, gate2, xacc


def kernel(x, c, w_ada, b_ada, g_mix, w_in, pool_w, pool_scale, w_out, g_ffn, w_router, router_bias,
           w_gate, w_up, w_down, ws_gate, ws_up, ws_down, g_final):
    B, S, D = x.shape
    depth = w_ada.shape[0]
    assert depth == 1, "the final residual is fused with the final norm, so exactly one layer is supported"
    assert S % (ATT_DILATIONS[-1] * ATT_BLOCK) == 0 and S % max(IN_TILE, MID_TILE, ATT_BLOCKS_PER_STEP * ATT_BLOCK) == 0
    xf = x.reshape(B * S, D)
    mod = _ada(c, w_ada[0], b_ada[0])
    yk, gates_t, gate2, xacc = _layer(
        xf, B, S, mod, g_mix[0], w_in[0], pool_w[0], pool_scale[0], w_out[0], g_ffn[0], w_router[0],
        router_bias[0], w_gate[0], w_up[0], w_down[0], ws_gate[0], ws_up[0], ws_down[0])
    out = _combine(yk, gates_t, gate2, xacc, g_final.reshape(1, D), S)
    return out.reshape(B, S, D)
```

```python
import functools

import jax
import jax.numpy as jnp
from jax import lax
from jax.experimental import pallas as pl
from jax.experimental.pallas import tpu as pltpu
from jax.experimental.pallas import tpu_sc as plsc

F32 = jnp.float32
BF16 = jnp.bfloat16
I32 = jnp.int32
U32 = jnp.uint32

LANES = 128
SINGLE_LOAD_STRIDE = 4
NORM_EPS = 1e-6
POOL_WINDOWS = (2, 4, 8, 16)
POOL_HALO = 16
ATT_DILATIONS = (1, 4, 16)
ATT_BLOCK = 128
ATT_HEADS_PER_GROUP = 4
ATT_HEAD_DIM = 64
ATT_GROUP_WIDTH = ATT_HEADS_PER_GROUP * ATT_HEAD_DIM
N_EXPERT_GROUPS = 8
TOPK_GROUPS = 4
TOP_K = 8
ROUTED_SCALE = 2.5

IN_TILE = 1024
ATT_BLOCKS_PER_STEP = 16
ATT_UNROLL = 8
MID_TILE = 512
ROUTE_TILE = 256
COMBINE_TILE = 512
GMM_BLOCK = 128
GMM_RING = 16
GMM_EXPERTS_PER_STEP = 1
SC_CHUNK = 128

NEG_INF = float("-inf")


def _sigmoid(v):
    return 1.0 / (1.0 + jnp.exp(-v))


def _rms(v):
    return v * lax.rsqrt(jnp.mean(v * v, axis=-1, keepdims=True) + NORM_EPS)


def _pack_bf16_pairs(v):
    n = v.shape[1] // 2
    lo = lax.bitcast_convert_type(v[:, :n].astype(BF16).astype(F32), U32)
    hi = lax.bitcast_convert_type(v[:, n:].astype(BF16).astype(F32), U32)
    return (hi & jnp.uint32(0xFFFF0000)) | (lo >> 16)


def _unpack_bf16_pairs(p):
    lo = lax.bitcast_convert_type(p << 16, F32).astype(BF16)
    hi = lax.bitcast_convert_type(p & jnp.uint32(0xFFFF0000), F32).astype(BF16)
    return lo, hi


def _ada_kernel(c_ref, w_ref, b_ref, o_ref):
    c = c_ref[...]
    cs = c * _sigmoid(c)
    o_ref[...] = jnp.dot(cs, w_ref[...], preferred_element_type=F32,
                         precision=lax.Precision.HIGHEST) + b_ref[...]


def _ada(c, w_ada, b_ada):
    B, D = c.shape
    W = w_ada.shape[1]
    tn = 1024
    return pl.pallas_call(
        _ada_kernel,
        grid=(W // tn,),
        in_specs=[pl.BlockSpec((B, D), lambda j: (0, 0)),
                  pl.BlockSpec((D, tn), lambda j: (0, j)),
                  pl.BlockSpec((1, tn), lambda j: (0, j))],
        out_specs=pl.BlockSpec((B, tn), lambda j: (0, j)),
        out_shape=jax.ShapeDtypeStruct((B, W), F32),
        name="ada",
    )(c, w_ada, b_ada.reshape(1, W))


def _in_kernel(x_ref, g_ref, sh_ref, sc_ref, w_ref, pool_ref, q0_ref, q1_ref, q2_ref, scr_ref, tmp_ref):
    h = _rms(x_ref[...]) * g_ref[...]
    h = h * (1.0 + sc_ref[0]) + sh_ref[0]
    hb = h.astype(BF16)
    tm = x_ref.shape[0]
    pw = pool_ref.shape[1]
    gw = ATT_GROUP_WIDTH
    pool_ref[...] = jnp.dot(hb, w_ref[:, 0:pw], preferred_element_type=F32)
    for g, (out, d) in enumerate(zip((q0_ref, q1_ref, q2_ref), ATT_DILATIONS)):
        for sec in range(3):
            c0 = pw + sec * 3 * gw + g * gw
            res = jnp.dot(hb, w_ref[:, c0:c0 + gw], preferred_element_type=F32)
            if d == 1:
                out[:, sec * gw:(sec + 1) * gw] = res.astype(BF16)
            else:
                for c in range(gw // LANES):
                    scr_ref[c] = res[:, c * LANES:(c + 1) * LANES]
                    src, f1 = scr_ref.at[c], 1
                    if d > SINGLE_LOAD_STRIDE:
                        f1 = SINGLE_LOAD_STRIDE
                        for q in range(f1):
                            tmp_ref[c, q * (tm // f1):(q + 1) * (tm // f1), :] = scr_ref[c, pl.ds(q, tm // f1, stride=f1), :]
                        src = tmp_ref.at[c]
                    for r in range(d):
                        r_lo, r_hi = r % f1, r // f1
                        c1 = sec * gw + c * LANES
                        out[r, :, c1:c1 + LANES] = src[pl.ds(r_lo * (tm // f1) + r_hi, tm // d, stride=d // f1),
                                                       :].astype(BF16)


def _in_proj(xf, g_mix, shift1, scale1, w_in_b, S, pool_width):
    N, D = xf.shape
    tm = IN_TILE
    spt = S // tm
    vec = lambda i: (i // spt, 0, 0)
    row = lambda i: (i, 0)
    gw3 = 3 * ATT_GROUP_WIDTH
    B = N // S
    res_spec = lambda d: pl.BlockSpec((d, tm // d, gw3), lambda i: (i // spt, i % spt, 0))
    res_shape = lambda d: jax.ShapeDtypeStruct((B * d, S // d, gw3), BF16)
    outs = pl.pallas_call(
        _in_kernel,
        grid=(N // tm,),
        in_specs=[pl.BlockSpec((tm, D), row),
                  pl.BlockSpec((1, D), lambda i: (0, 0)),
                  pl.BlockSpec((1, 1, D), vec),
                  pl.BlockSpec((1, 1, D), vec),
                  pl.BlockSpec(w_in_b.shape, lambda i: (0, 0))],
        out_specs=[pl.BlockSpec((tm, pool_width), row), pl.BlockSpec((tm, gw3), row)]
                  + [res_spec(d) for d in ATT_DILATIONS[1:]],
        out_shape=[jax.ShapeDtypeStruct((N, pool_width), F32), jax.ShapeDtypeStruct((N, gw3), BF16)]
                  + [res_shape(d) for d in ATT_DILATIONS[1:]],
        scratch_shapes=[pltpu.VMEM((ATT_GROUP_WIDTH // LANES, tm, LANES), F32)] * 2,
        name="in_proj",
    )(xf, g_mix, shift1, scale1, w_in_b)
    return outs[0], [o.reshape(N, gw3) for o in outs[1:]]


def _attn_kernel(nbs, a_ref, halo_ref, o_ref, lse_ref, kv_ref, band_ref):
    i = pl.program_id(0)
    R = a_ref.shape[0] // ATT_BLOCK
    gw = ATT_GROUP_WIDTH
    blk = ATT_BLOCK
    nh = ATT_HEADS_PER_GROUP
    kv_ref[0:blk, :] = halo_ref[:, gw:3 * gw]
    kv_ref[blk:, :] = a_ref[:, gw:3 * gw]
    row = lax.broadcasted_iota(I32, (nh * blk, 2 * blk), 0) % blk
    col = lax.broadcasted_iota(I32, (nh * blk, 2 * blk), 1)
    in_band = (col >= row) & (col <= row + blk)
    band_ref[0] = jnp.where(in_band, 0.0, NEG_INF)
    band_ref[1] = jnp.where(in_band & (col >= blk), 0.0, NEG_INF)
    head_of_lane = lax.broadcasted_iota(I32, (blk, gw), 1) // ATT_HEAD_DIM
    nt = (((1,), (1,)), ((), ()))

    def one_block(jj, start):
        r0 = pl.multiple_of(jj * blk, blk)
        qf = a_ref[pl.ds(r0, blk), 0:gw].astype(F32) * (ATT_HEAD_DIM ** -0.5)
        q4 = jnp.concatenate([jnp.where(head_of_lane == h, qf, 0.0) for h in range(nh)], axis=0).astype(BF16)
        kc = kv_ref[pl.ds(r0, 2 * blk), 0:gw]
        vc = kv_ref[pl.ds(r0, 2 * blk), gw:2 * gw]
        s = lax.dot_general(q4, kc, nt, preferred_element_type=F32) + band_ref[1 if start is True else 0]
        if start is not None and start is not True:
            s = jnp.where(col >= jnp.where(start, blk, 0), s, NEG_INF)
        m = jnp.max(s, axis=1, keepdims=True)
        p = jnp.exp(s - m)
        l = jnp.sum(p, axis=1, keepdims=True)
        o4 = jnp.dot(p.astype(BF16), vc, preferred_element_type=F32) / l
        lse4 = m + jnp.log(l)
        o = jnp.zeros((blk, gw), F32)
        lse = jnp.zeros((blk, gw), F32)
        for h in range(nh):
            hm = head_of_lane == h
            o = jnp.where(hm, o4[h * blk:(h + 1) * blk, :], o)
            lse = jnp.where(hm, lse4[h * blk:(h + 1) * blk, :], lse)
        o_ref[pl.ds(r0, blk), :] = o
        lse_ref[pl.ds(r0, blk), :] = lse

    U = ATT_UNROLL
    assert U % nbs == 0 or nbs % U == 0

    def body(it, carry):
        for j in range(U):
            if U % nbs == 0:
                start = True if j % nbs == 0 else None
            else:
                start = (((i * R + it * U) % nbs) == 0) if j == 0 else None
            one_block(it * U + j, start)
        return carry

    lax.fori_loop(0, R // U, body, 0)


def _attention(a, nbs):
    N = a.shape[0]
    R = ATT_BLOCKS_PER_STEP
    gw = ATT_GROUP_WIDTH
    tm = R * ATT_BLOCK
    return pl.pallas_call(
        functools.partial(_attn_kernel, nbs),
        grid=(N // tm,),
        in_specs=[pl.BlockSpec((tm, 3 * gw), lambda i: (i, 0)),
                  pl.BlockSpec((ATT_BLOCK, 3 * gw), lambda i: (jnp.maximum(i * R - 1, 0), 0))],
        out_specs=[pl.BlockSpec((tm, gw), lambda i: (i, 0))] * 2,
        out_shape=[jax.ShapeDtypeStruct((N, gw), F32)] * 2,
        scratch_shapes=[pltpu.VMEM((tm + ATT_BLOCK, 2 * gw), BF16),
                        pltpu.VMEM((2, ATT_HEADS_PER_GROUP * ATT_BLOCK, 2 * ATT_BLOCK), F32)],
        name="attn",
    )(a, a)


def _mid_kernel(spt, u_ref, uh_ref, o0_ref, l0_ref, o1_ref, l1_ref, o2_ref, l2_ref, x_ref,
                pbd_ref, psc_ref, wout_ref, gate1_ref, gffn_ref, sh2_ref, sc2_ref, gate2_ref,
                wsg_ref, wsu_ref, wsd_ref, wr_ref, bias_ref, before_ref,
                xacc_ref, h2_ref, e_ref, g_ref, r_ref, cnt_ref, til_ref, base_ref):
    i = pl.program_id(0)
    tm, pw = u_ref.shape
    si = i % spt
    u = u_ref[...]
    keep = jnp.full((POOL_HALO, pw), si, I32) > 0
    ext = jnp.concatenate([jnp.where(keep, uh_ref[...], 0.0), u], axis=0)
    lane_grp = lax.broadcasted_iota(I32, (tm, pw), 1) // (pw // len(POOL_WINDOWS))
    pooled = jnp.zeros((tm, pw), F32)
    s, w = ext, 1
    while w < POOL_HALO:
        s = s + pltpu.roll(s, w, axis=0)
        w *= 2
        if w in POOL_WINDOWS:
            pooled = jnp.where(lane_grp == POOL_WINDOWS.index(w), s[POOL_HALO:, :], pooled)
    win = jnp.zeros((tm, pw), I32)
    for g, w in enumerate(POOL_WINDOWS):
        win = jnp.where(lane_grp == g, w, win)
    pos = si * tm + lax.broadcasted_iota(I32, (tm, pw), 0)
    cnt = jnp.minimum(pos + 1, win).astype(F32)
    pooled = pooled / cnt - u
    pool_out = jnp.dot(pooled.astype(BF16), pbd_ref[...], preferred_element_type=F32) * psc_ref[...]
    def token_order(slot, ref):
        d, n, w = ref.shape
        for r in range(d):
            for c in range(w // LANES):
                til_ref[slot, c, pl.ds(r, n, stride=d), :] = ref[r, :, c * LANES:(c + 1) * LANES]
        return jnp.concatenate([til_ref[slot, c] for c in range(w // LANES)], axis=1)

    l0 = l0_ref[...]
    l1 = token_order(0, l1_ref)
    l2 = token_order(1, l2_ref)
    m = jnp.maximum(jnp.maximum(l0, l1), l2)
    w0 = jnp.exp(l0 - m)
    w1 = jnp.exp(l1 - m)
    w2 = jnp.exp(l2 - m)
    attn = (w0 * o0_ref[...] + w1 * token_order(2, o1_ref) + w2 * token_order(3, o2_ref)) / (w0 + w1 + w2)
    mixed = (jnp.dot(pool_out.astype(BF16), wout_ref[0:pw, :], preferred_element_type=F32)
             + jnp.dot(attn.astype(BF16), wout_ref[pw:, :], preferred_element_type=F32))
    x1 = x_ref[...] + gate1_ref[0] * mixed
    h2 = _rms(x1) * gffn_ref[...]
    h2 = h2 * (1.0 + sc2_ref[0]) + sh2_ref[0]
    h2_ref[...] = _pack_bf16_pairs(h2)
    hb = h2.astype(BF16)
    a = jnp.dot(hb, wsg_ref[...], preferred_element_type=F32)
    b = jnp.dot(hb, wsu_ref[...], preferred_element_type=F32)
    act = (a * _sigmoid(a)) * b
    shared = jnp.dot(act.astype(BF16), wsd_ref[...], preferred_element_type=F32)
    xacc_ref[...] = x1 + gate2_ref[0] * shared
    @pl.when(i == 0)
    def _():
        base_ref[...] = jnp.zeros_like(base_ref)

    C = e_ref.shape[2]
    for t0 in range(0, tm, ROUTE_TILE):
        idxs, gates, ranks = _route_tile(hb[t0:t0 + ROUTE_TILE, :], wr_ref, bias_ref, before_ref, base_ref)
        for k in range(TOP_K):
            g_ref[k:k + 1, t0:t0 + ROUTE_TILE] = gates[k]
            for c in range(ROUTE_TILE // C):
                e_ref[t0 // C + c, k:k + 1, :] = idxs[k][:, c * C:(c + 1) * C]
                r_ref[t0 // C + c, k:k + 1, :] = ranks[k][:, c * C:(c + 1) * C]
    cnt_ref[...] = base_ref[...].astype(I32)


def _mid(u, attn_outs, xf, pool_bd, pool_scale, w_out_b, gate1, g_ffn, shift2, scale2, gate2,
         wsg_b, wsu_b, wsd_b, wr_t, bias_col, S):
    N, D = xf.shape
    E = wr_t.shape[0]
    tok = jnp.arange(ROUTE_TILE, dtype=I32)
    before = (tok[:, None] < tok[None, :]).astype(BF16)
    pw = u.shape[1]
    tm = MID_TILE
    spt = S // tm
    row = lambda i: (i, 0)
    vec = lambda i: (i // spt, 0, 0)
    full = lambda a: pl.BlockSpec(a.shape, lambda i: (0,) * a.ndim)
    hpt = tm // POOL_HALO
    in_specs = [pl.BlockSpec((tm, pw), row),
                pl.BlockSpec((POOL_HALO, pw), lambda i: (jnp.maximum(i * hpt - 1, 0), 0))]
    gw = ATT_GROUP_WIDTH
    in_specs += [pl.BlockSpec((tm, gw), row)] * 2
    for d in ATT_DILATIONS[1:]:
        in_specs += [pl.BlockSpec((d, tm // d, gw), lambda i: (i // spt, i % spt, 0))] * 2
    in_specs += [pl.BlockSpec((tm, D), row), full(pool_bd), full(pool_scale), full(w_out_b),
                 pl.BlockSpec((1, 1, D), vec), full(g_ffn), pl.BlockSpec((1, 1, D), vec),
                 pl.BlockSpec((1, 1, D), vec), pl.BlockSpec((1, 1, D), vec),
                 full(wsg_b), full(wsu_b), full(wsd_b), full(wr_t), full(bias_col), full(before)]
    col = lambda i: (0, i)
    C = SC_CHUNK
    chunked = pl.BlockSpec((tm // C, TOP_K, C), lambda i: (i, 0, 0))
    return pl.pallas_call(
        functools.partial(_mid_kernel, spt),
        grid=(N // tm,),
        in_specs=in_specs,
        out_specs=[pl.BlockSpec((tm, D), row), pl.BlockSpec((tm, D // 2), row),
                   chunked, pl.BlockSpec((TOP_K, tm), col), chunked,
                   pl.BlockSpec((E, 1), lambda i: (0, 0))],
        out_shape=[jax.ShapeDtypeStruct((N, D), F32), jax.ShapeDtypeStruct((N, D // 2), U32),
                   jax.ShapeDtypeStruct((N // C, TOP_K, C), I32), jax.ShapeDtypeStruct((TOP_K, N), F32),
                   jax.ShapeDtypeStruct((N // C, TOP_K, C), I32), jax.ShapeDtypeStruct((E, 1), I32)],
        scratch_shapes=[pltpu.VMEM((4, gw // LANES, tm, LANES), F32), pltpu.VMEM((E, 1), F32)],
        compiler_params=pltpu.CompilerParams(dimension_semantics=("arbitrary",)),
        name="mid",
    )(u, u, *attn_outs, xf, pool_bd, pool_scale, w_out_b, gate1, g_ffn, shift2, scale2, gate2,
      wsg_b, wsu_b, wsd_b, wr_t, bias_col, before)


def _route_tile(hb, wr_ref, bias_ref, before_ref, base_ref):
    T = hb.shape[0]
    E = wr_ref.shape[0]
    gsz = E // N_EXPERT_GROUPS
    logits = lax.dot_general(wr_ref[...], hb, (((1,), (1,)), ((), ())), preferred_element_type=F32)
    scores = _sigmoid(logits)
    biased = scores + bias_ref[...]
    giota = lax.broadcasted_iota(I32, (gsz, T), 0)
    gscore = []
    for g in range(N_EXPERT_GROUPS):
        blk = biased[g * gsz:(g + 1) * gsz, :]
        m1 = jnp.max(blk, axis=0, keepdims=True)
        i1 = jnp.min(jnp.where(blk == m1, giota, gsz), axis=0, keepdims=True)
        m2 = jnp.max(jnp.where(giota == i1, NEG_INF, blk), axis=0, keepdims=True)
        gscore.append(m1 + m2)
    parts = []
    for g in range(N_EXPERT_GROUPS):
        beaten = jnp.zeros((1, T), I32)
        for o in range(N_EXPERT_GROUPS):
            if o == g:
                continue
            wins = (gscore[o] >= gscore[g]) if o < g else (gscore[o] > gscore[g])
            beaten = beaten + wins.astype(I32)
        keep = jnp.broadcast_to(beaten, (gsz, T)) < TOPK_GROUPS
        parts.append(jnp.where(keep, biased[g * gsz:(g + 1) * gsz, :], NEG_INF))
    cur = jnp.concatenate(parts, axis=0)
    eiota = lax.broadcasted_iota(I32, (E, T), 0)
    live = cur > NEG_INF
    idxs, gates = [], []
    for k in range(TOP_K):
        m = jnp.max(cur, axis=0, keepdims=True)
        idx = jnp.min(jnp.where(cur == m, eiota, E), axis=0, keepdims=True)
        oh = eiota == idx
        gates.append(jnp.sum(jnp.where(oh, scores, 0.0), axis=0, keepdims=True))
        idxs.append(idx)
        cur = jnp.where(oh, NEG_INF, cur)
    selm = jnp.where(live & (cur == NEG_INF), 1.0, 0.0)
    gsum = gates[0]
    for k in range(1, TOP_K):
        gsum = gsum + gates[k]
    gates = [gk / gsum * ROUTED_SCALE for gk in gates]
    tot = jnp.dot(selm.astype(BF16), before_ref[...], preferred_element_type=F32) + base_ref[...]
    ranks = [jnp.sum(jnp.where(eiota == idxs[k], tot, 0.0), axis=0, keepdims=True).astype(I32)
             for k in range(TOP_K)]
    base_ref[...] = base_ref[...] + jnp.sum(selm, axis=1, keepdims=True)
    return idxs, gates, ranks


def _sc_dispatch(eidx_c, rank_c, offs, h2p, P):
    N, W = h2p.shape
    nch, K, C = eidx_c.shape
    E = offs.shape[0]
    info = plsc.get_sparse_core_info()
    nw = info.num_cores * info.num_subcores
    L = info.num_lanes
    per_w = nch // nw
    mesh = plsc.VectorSubcoreMesh(core_axis_name="c", subcore_axis_name="s")

    @functools.partial(
        pl.kernel, mesh=mesh,
        out_type=[jax.ShapeDtypeStruct((P, W), h2p.dtype), jax.ShapeDtypeStruct((nch, K, C), I32)],
        scratch_types=[pltpu.VMEM((E,), I32), pltpu.VMEM((K, C), I32), pltpu.VMEM((K, C), I32),
                       pltpu.VMEM((K, C), I32), pltpu.VMEM((C, W), h2p.dtype), pltpu.SemaphoreType.DMA],
        compiler_params=pltpu.CompilerParams(needs_layout_passes=False),
        name="sc_dispatch",
    )
    def k(e_hbm, r_hbm, off_hbm, h_hbm, xs_hbm, dest_hbm, off_v, e_v, r_v, idx_v, rows_v, sem):
        wid = lax.axis_index("s") * info.num_cores + lax.axis_index("c")
        pltpu.sync_copy(off_hbm, off_v)

        @pl.loop(0, per_w)
        def _(j):
            ch = wid * per_w + j
            pltpu.sync_copy(e_hbm.at[ch], e_v)
            pltpu.sync_copy(r_hbm.at[ch], r_v)
            for kk in range(K):
                for q in range(C // L):
                    sl = pl.ds(q * L, L)
                    idx_v[kk, sl] = plsc.load_gather(off_v, [e_v[kk, sl]]) + r_v[kk, sl]
            pltpu.sync_copy(idx_v, dest_hbm.at[ch])
            pltpu.sync_copy(h_hbm.at[pl.ds(ch * C, C)], rows_v)
            copies = [pltpu.async_copy(rows_v, xs_hbm.at[idx_v.at[kk]], sem) for kk in range(K)]
            for cp in copies:
                cp.wait()

    return k(eidx_c, rank_c, offs, h2p)


def _gmm_kernel(bstart_ref, nbe_ref, cnt_ref, nu_ref, wg_ref, wu_ref, wd_ref, xs_hbm, ys_hbm,
                xbuf, ybuf, xsem, ysem):
    step = pl.program_id(0)
    last = pl.num_programs(0) - 1
    epg = wg_ref.shape[0]
    ring, bm = xbuf.shape[0], xbuf.shape[1]
    nblk = ys_hbm.shape[0] // bm
    nused = nu_ref[0]

    def x_copy(b, slot):
        return pltpu.make_async_copy(xs_hbm.at[pl.ds(pl.multiple_of(b * bm, bm), bm), :], xbuf.at[slot],
                                     xsem.at[slot])

    def y_copy(b, slot):
        return pltpu.make_async_copy(ybuf.at[slot], ys_hbm.at[pl.ds(pl.multiple_of(b * bm, bm), bm), :],
                                     ysem.at[slot])

    @pl.when(step == 0)
    def _():
        for j in range(ring - 1):
            @pl.when(j < nused)
            def _():
                x_copy(j, j).start()

    def run_expert(ee):
        e = step * epg + ee
        b0 = bstart_ref[e]
        nb = nbe_ref[e]

        @pl.when(nb > 0)
        def _():
            def prefetch(t):
                @pl.when(t < nused)
                def _():
                    x_copy(t, jnp.bitwise_and(t, ring - 1)).start()

            def process(b, n):
                slots = [jnp.bitwise_and(b + j, ring - 1) for j in range(n)]
                for j in range(n):
                    x_copy(b + j, slots[j]).wait()
                prefetch(b + ring - 1)
                for j in range(n):
                    @pl.when(b + j >= ring)
                    def _():
                        y_copy(b + j - ring, slots[j]).wait()
                rows = lax.broadcasted_iota(I32, (n * bm, 1), 0)
                valid = cnt_ref[e] - (b - b0) * bm
                xp = jnp.concatenate([xbuf[s] for s in slots], axis=0)
                xb = jnp.concatenate(_unpack_bf16_pairs(jnp.where(rows < valid, xp, jnp.uint32(0))), axis=1)
                a = jnp.dot(xb, wg_ref[ee].astype(BF16), preferred_element_type=F32)
                g = jnp.dot(xb, wu_ref[ee].astype(BF16), preferred_element_type=F32)
                act = (a * _sigmoid(a)) * g
                yp = _pack_bf16_pairs(jnp.dot(act.astype(BF16), wd_ref[ee].astype(BF16),
                                              preferred_element_type=F32))
                for j in range(n):
                    ybuf[slots[j]] = yp[j * bm:(j + 1) * bm, :]
                    y_copy(b + j, slots[j]).start()
                for j in range(1, n):
                    prefetch(b + ring - 1 + j)

            quads = lax.shift_right_logical(nb, 2)
            lax.fori_loop(0, quads, lambda j, c: (process(b0 + 4 * j, 4), c)[1], 0)
            rest2 = jnp.bitwise_and(nb, 2)

            @pl.when(rest2 != 0)
            def _():
                process(b0 + 4 * quads, 2)

            @pl.when(jnp.bitwise_and(nb, 1) != 0)
            def _():
                process(b0 + 4 * quads + rest2, 1)

    for ee in range(epg):
        run_expert(ee)

    @pl.when(step == last)
    def _():
        for back in range(ring, 0, -1):
            @pl.when(nused >= back)
            def _():
                y_copy(nused - back, jnp.bitwise_and(nused - back, ring - 1)).wait()
        ybuf[0] = jnp.zeros(ybuf.shape[1:], ybuf.dtype)
        lax.fori_loop(nused, nblk, lambda b, c: (y_copy(b, 0).start(), c)[1], 0)
        lax.fori_loop(nused, nblk, lambda b, c: (y_copy(b, 0).wait(), c)[1], 0)


def _gmm(bstart, nb_e, counts, nused, xs, w_gate, w_up, w_down):
    P, W = xs.shape
    E, D, F = w_gate.shape
    bm = GMM_BLOCK
    epg = GMM_EXPERTS_PER_STEP
    wsel = lambda s, *_: (s, 0, 0)
    grid_spec = pltpu.PrefetchScalarGridSpec(
        num_scalar_prefetch=4,
        grid=(E // epg,),
        in_specs=[pl.BlockSpec((epg, D, F), wsel), pl.BlockSpec((epg, D, F), wsel), pl.BlockSpec((epg, F, D), wsel),
                  pl.BlockSpec(memory_space=pl.ANY)],
        out_specs=pl.BlockSpec(memory_space=pl.ANY),
        scratch_shapes=[pltpu.VMEM((GMM_RING, bm, W), xs.dtype), pltpu.VMEM((GMM_RING, bm, W), xs.dtype),
                        pltpu.SemaphoreType.DMA((GMM_RING,)), pltpu.SemaphoreType.DMA((GMM_RING,))],
    )
    return pl.pallas_call(
        _gmm_kernel,
        grid_spec=grid_spec,
        out_shape=jax.ShapeDtypeStruct((P, W), xs.dtype),
        compiler_params=pltpu.CompilerParams(dimension_semantics=("arbitrary",)),
        name="gmm",
    )(bstart, nb_e, counts, nused, w_gate, w_up, w_down, xs)


def _sc_gather(dest_c, ys):
    nch, K, C = dest_c.shape
    W = ys.shape[1]
    H = C // 2
    info = plsc.get_sparse_core_info()
    nw = info.num_cores * info.num_subcores
    per_w = nch // nw
    nbuf = 3
    mesh = plsc.VectorSubcoreMesh(core_axis_name="c", subcore_axis_name="s")
    items = [(kk, hh) for kk in range(K) for hh in range(2)]

    @functools.partial(
        pl.kernel, mesh=mesh,
        out_type=jax.ShapeDtypeStruct((K, nch * C, W), ys.dtype),
        scratch_types=([pltpu.VMEM((K, C), I32)] + [pltpu.VMEM((H, W), ys.dtype)] * nbuf
                       + [pltpu.SemaphoreType.DMA] * (2 * nbuf)),
        name="sc_gather",
    )
    def k(dest_hbm, ys_hbm, yk_hbm, idx_v, *rest):
        bufs, gsem, wsem = rest[:nbuf], rest[nbuf:2 * nbuf], rest[2 * nbuf:]
        wid = lax.axis_index("s") * info.num_cores + lax.axis_index("c")

        @pl.loop(0, per_w)
        def _(j):
            ch = wid * per_w + j
            pltpu.sync_copy(dest_hbm.at[ch], idx_v)

            def gather(i):
                kk, hh = items[i]
                return pltpu.async_copy(ys_hbm.at[idx_v.at[kk, pl.ds(hh * H, H)]], bufs[i % nbuf], gsem[i % nbuf])

            def write(i):
                kk, hh = items[i]
                return pltpu.async_copy(bufs[i % nbuf], yk_hbm.at[kk, pl.ds(ch * C + hh * H, H)], wsem[i % nbuf])

            n = len(items)
            g = {0: gather(0), 1: gather(1)}
            w = {}
            for i in range(n):
                g[i].wait()
                w[i] = write(i)
                if i + 2 < n:
                    if i >= 1:
                        w.pop(i - 1).wait()
                    g[i + 2] = gather(i + 2)
            for i in sorted(w):
                w[i].wait()

    return k(dest_c, ys)


def _combine_kernel(yk_ref, g_ref, gate2_ref, xacc_ref, gfin_ref, o_ref, gpad_ref):
    @pl.when(pl.program_id(0) == 0)
    def _():
        gpad_ref[...] = jnp.zeros_like(gpad_ref)

    gpad_ref[0:TOP_K, :] = g_ref[...]
    gt = gpad_ref[...].T
    hi_mask = jnp.uint32(0xFFFF0000)
    acc_lo = acc_hi = None
    for k in range(TOP_K):
        p = yk_ref[k]
        g = gt[:, k:k + 1]
        lo = lax.bitcast_convert_type(p << 16, F32) * g
        hi = lax.bitcast_convert_type(p & hi_mask, F32) * g
        acc_lo = lo if k == 0 else acc_lo + lo
        acc_hi = hi if k == 0 else acc_hi + hi
    routed = jnp.concatenate([acc_lo, acc_hi], axis=1)
    x2 = xacc_ref[...] + gate2_ref[0] * routed
    o_ref[...] = _rms(x2) * gfin_ref[...]


def _combine(yk, gates, gate2, xacc, g_final, S):
    N, D = xacc.shape
    W = yk.shape[2]
    T = COMBINE_TILE
    spt = S // T
    return pl.pallas_call(
        _combine_kernel,
        grid=(N // T,),
        in_specs=[pl.BlockSpec((TOP_K, T, W), lambda i: (0, i, 0)),
                  pl.BlockSpec((TOP_K, T), lambda i: (0, i)),
                  pl.BlockSpec((1, 1, D), lambda i: (i // spt, 0, 0)),
                  pl.BlockSpec((T, D), lambda i: (i, 0)),
                  pl.BlockSpec((1, D), lambda i: (0, 0))],
        out_specs=pl.BlockSpec((T, D), lambda i: (i, 0)),
        out_shape=jax.ShapeDtypeStruct((N, D), F32),
        scratch_shapes=[pltpu.VMEM((LANES, T), F32)],
        compiler_params=pltpu.CompilerParams(dimension_semantics=("arbitrary",)),
        name="combine",
    )(yk, gates, gate2, xacc, g_final)


def _layer(xf, B, S, mod, g_mix, w_in, pool_w, pool_scale, w_out, g_ffn, w_router, router_bias,
           w_gate, w_up, w_down, ws_gate, ws_up, ws_down):
    N, D = xf.shape
    E = w_router.shape[1]
    pw = pool_scale.shape[0]
    shift1, scale1, gate1, shift2, scale2, gate2 = [m.reshape(B, 1, D) for m in jnp.split(mod, 6, axis=-1)]
    u, qkv = _in_proj(xf, g_mix.reshape(1, D), shift1, scale1, w_in.astype(BF16), S, pw)
    attn_outs = []
    for a, d in zip(qkv, ATT_DILATIONS):
        o, lse = _attention(a, S // d // ATT_BLOCK)
        shape = (N, ATT_GROUP_WIDTH) if d == 1 else (B * d, S // d, ATT_GROUP_WIDTH)
        attn_outs += [o.reshape(shape), lse.reshape(shape)]
    ng = pool_w.shape[0]
    pool_bd = jnp.einsum('gcd,gh->gchd', pool_w, jnp.eye(ng, dtype=pool_w.dtype)).reshape(pw, pw).astype(BF16)
    xacc, h2, eidx, gates, rank, counts = _mid(
        u, attn_outs, xf, pool_bd, pool_scale.reshape(1, pw), w_out.astype(BF16), gate1,
        g_ffn.reshape(1, D), shift2, scale2, gate2,
        ws_gate.astype(BF16), ws_up.astype(BF16), ws_down.astype(BF16),
        w_router.T.astype(BF16), router_bias.reshape(E, 1).astype(F32), S)
    bm = GMM_BLOCK
    nblk = N * TOP_K // bm + E
    nb_e = (counts[:, 0] + bm - 1) // bm
    bend = jnp.cumsum(nb_e)
    bstart = (bend - nb_e).astype(I32)
    nused = bend[-1:].astype(I32)
    xs, dest_c = _sc_dispatch(eidx, rank, bstart * bm, h2, nblk * bm)
    ys = _gmm(bstart, nb_e.astype(I32), counts[:, 0], nused, xs, w_gate, w_up, w_down)
    return _sc_gather(dest_c, ys), gates, gate2, xacc


def kernel(x, c, w_ada, b_ada, g_mix, w_in, pool_w, pool_scale, w_out, g_ffn, w_router, router_bias,
           w_gate, w_up, w_down, ws_gate, ws_up, ws_down, g_final):
    B, S, D = x.shape
    depth = w_ada.shape[0]
    assert depth == 1, "the final residual is fused with the final norm, so exactly one layer is supported"
    assert S % (ATT_DILATIONS[-1] * ATT_BLOCK) == 0 and S % max(IN_TILE, MID_TILE, ATT_BLOCKS_PER_STEP * ATT_BLOCK) == 0
    xf = x.reshape(B * S, D)
    mod = _ada(c, w_ada[0], b_ada[0])
    yk, gates, gate2, xacc = _layer(
        xf, B, S, mod, g_mix[0], w_in[0], pool_w[0], pool_scale[0], w_out[0], g_ffn[0], w_router[0],
        router_bias[0], w_gate[0], w_up[0], w_down[0], ws_gate[0], ws_up[0], ws_down[0])
    out = _combine(yk, gates, gate2, xacc, g_final.reshape(1, D), S)
    return out.reshape(B, S, D)
```

```python
import functools

import jax
import jax.numpy as jnp
from jax import lax
from jax.experimental import pallas as pl
from jax.experimental.pallas import tpu as pltpu
from jax.experimental.pallas import tpu_sc as plsc

F32 = jnp.float32
BF16 = jnp.bfloat16
I32 = jnp.int32
U32 = jnp.uint32

LANES = 128
SINGLE_LOAD_STRIDE = 4
NORM_EPS = 1e-6
POOL_WINDOWS = (2, 4, 8, 16)
POOL_HALO = 16
ATT_DILATIONS = (1, 4, 16)
ATT_BLOCK = 128
ATT_HEADS_PER_GROUP = 4
ATT_HEAD_DIM = 64
ATT_GROUP_WIDTH = ATT_HEADS_PER_GROUP * ATT_HEAD_DIM
N_EXPERT_GROUPS = 8
TOPK_GROUPS = 4
TOP_K = 8
ROUTED_SCALE = 2.5

IN_TILE = 1024
ATT_BLOCKS_PER_STEP = 16
ATT_UNROLL = 8
MID_TILE = 512
ROUTE_TILE = 512
COMBINE_TILE = 512
GMM_BLOCK = 128
GMM_RING = 16
GMM_EXPERTS_PER_STEP = 1
SC_CHUNK = 128
SC_GATHER_ROWS = 32
SC_GATHER_BUFS = 6

NEG_INF = float("-inf")


def _sigmoid(v):
    return 1.0 / (1.0 + jnp.exp(-v))


def _rms(v):
    return v * lax.rsqrt(jnp.mean(v * v, axis=-1, keepdims=True) + NORM_EPS)


def _pack_bf16_pairs(v):
    n = v.shape[1] // 2
    lo = lax.bitcast_convert_type(v[:, :n].astype(BF16).astype(F32), U32)
    hi = lax.bitcast_convert_type(v[:, n:].astype(BF16).astype(F32), U32)
    return (hi & jnp.uint32(0xFFFF0000)) | (lo >> 16)


def _unpack_bf16_pairs(p):
    lo = lax.bitcast_convert_type(p << 16, F32).astype(BF16)
    hi = lax.bitcast_convert_type(p & jnp.uint32(0xFFFF0000), F32).astype(BF16)
    return lo, hi


def _ada_kernel(c_ref, w_ref, b_ref, o_ref):
    c = c_ref[...]
    cs = c * _sigmoid(c)
    o_ref[...] = jnp.dot(cs, w_ref[...], preferred_element_type=F32,
                         precision=lax.Precision.HIGHEST) + b_ref[...]


def _ada(c, w_ada, b_ada):
    B, D = c.shape
    W = w_ada.shape[1]
    tn = 1024
    return pl.pallas_call(
        _ada_kernel,
        grid=(W // tn,),
        in_specs=[pl.BlockSpec((B, D), lambda j: (0, 0)),
                  pl.BlockSpec((D, tn), lambda j: (0, j)),
                  pl.BlockSpec((1, tn), lambda j: (0, j))],
        out_specs=pl.BlockSpec((B, tn), lambda j: (0, j)),
        out_shape=jax.ShapeDtypeStruct((B, W), F32),
        name="ada",
    )(c, w_ada, b_ada.reshape(1, W))


def _in_kernel(x_ref, g_ref, sh_ref, sc_ref, w_ref, pool_ref, q0_ref, q1_ref, q2_ref, scr_ref, tmp_ref):
    h = _rms(x_ref[...]) * g_ref[...]
    h = h * (1.0 + sc_ref[0]) + sh_ref[0]
    hb = h.astype(BF16)
    tm = x_ref.shape[0]
    pw = pool_ref.shape[1]
    gw = ATT_GROUP_WIDTH
    pool_ref[...] = jnp.dot(hb, w_ref[:, 0:pw], preferred_element_type=F32)
    for g, (out, d) in enumerate(zip((q0_ref, q1_ref, q2_ref), ATT_DILATIONS)):
        for sec in range(3):
            c0 = pw + sec * 3 * gw + g * gw
            res = jnp.dot(hb, w_ref[:, c0:c0 + gw], preferred_element_type=F32)
            if d == 1:
                out[:, sec * gw:(sec + 1) * gw] = res.astype(BF16)
            else:
                for c in range(gw // LANES):
                    scr_ref[c] = res[:, c * LANES:(c + 1) * LANES]
                    src, f1 = scr_ref.at[c], 1
                    if d > SINGLE_LOAD_STRIDE:
                        f1 = SINGLE_LOAD_STRIDE
                        for q in range(f1):
                            tmp_ref[c, q * (tm // f1):(q + 1) * (tm // f1), :] = scr_ref[c, pl.ds(q, tm // f1, stride=f1), :]
                        src = tmp_ref.at[c]
                    for r in range(d):
                        r_lo, r_hi = r % f1, r // f1
                        c1 = sec * gw + c * LANES
                        out[r, :, c1:c1 + LANES] = src[pl.ds(r_lo * (tm // f1) + r_hi, tm // d, stride=d // f1),
                                                       :].astype(BF16)


def _in_proj(xf, g_mix, shift1, scale1, w_in_b, S, pool_width):
    N, D = xf.shape
    tm = IN_TILE
    spt = S // tm
    vec = lambda i: (i // spt, 0, 0)
    row = lambda i: (i, 0)
    gw3 = 3 * ATT_GROUP_WIDTH
    B = N // S
    res_spec = lambda d: pl.BlockSpec((d, tm // d, gw3), lambda i: (i // spt, i % spt, 0))
    res_shape = lambda d: jax.ShapeDtypeStruct((B * d, S // d, gw3), BF16)
    outs = pl.pallas_call(
        _in_kernel,
        grid=(N // tm,),
        in_specs=[pl.BlockSpec((tm, D), row),
                  pl.BlockSpec((1, D), lambda i: (0, 0)),
                  pl.BlockSpec((1, 1, D), vec),
                  pl.BlockSpec((1, 1, D), vec),
                  pl.BlockSpec(w_in_b.shape, lambda i: (0, 0))],
        out_specs=[pl.BlockSpec((tm, pool_width), row), pl.BlockSpec((tm, gw3), row)]
                  + [res_spec(d) for d in ATT_DILATIONS[1:]],
        out_shape=[jax.ShapeDtypeStruct((N, pool_width), F32), jax.ShapeDtypeStruct((N, gw3), BF16)]
                  + [res_shape(d) for d in ATT_DILATIONS[1:]],
        scratch_shapes=[pltpu.VMEM((ATT_GROUP_WIDTH // LANES, tm, LANES), F32)] * 2,
        name="in_proj",
    )(xf, g_mix, shift1, scale1, w_in_b)
    return outs[0], [o.reshape(N, gw3) for o in outs[1:]]


def _attn_kernel(nbs, a_ref, halo_ref, o_ref, lse_ref, kv_ref, band_ref):
    i = pl.program_id(0)
    R = a_ref.shape[0] // ATT_BLOCK
    gw = ATT_GROUP_WIDTH
    blk = ATT_BLOCK
    nh = ATT_HEADS_PER_GROUP
    kv_ref[0:blk, :] = halo_ref[:, gw:3 * gw]
    kv_ref[blk:, :] = a_ref[:, gw:3 * gw]
    row = lax.broadcasted_iota(I32, (nh * blk, 2 * blk), 0) % blk
    col = lax.broadcasted_iota(I32, (nh * blk, 2 * blk), 1)
    in_band = (col >= row) & (col <= row + blk)
    band_ref[0] = jnp.where(in_band, 0.0, NEG_INF)
    band_ref[1] = jnp.where(in_band & (col >= blk), 0.0, NEG_INF)
    head_of_lane = lax.broadcasted_iota(I32, (blk, gw), 1) // ATT_HEAD_DIM
    nt = (((1,), (1,)), ((), ()))

    def one_block(jj, start):
        r0 = pl.multiple_of(jj * blk, blk)
        qf = a_ref[pl.ds(r0, blk), 0:gw].astype(F32) * (ATT_HEAD_DIM ** -0.5)
        q4 = jnp.concatenate([jnp.where(head_of_lane == h, qf, 0.0) for h in range(nh)], axis=0).astype(BF16)
        kc = kv_ref[pl.ds(r0, 2 * blk), 0:gw]
        vc = kv_ref[pl.ds(r0, 2 * blk), gw:2 * gw]
        s = lax.dot_general(q4, kc, nt, preferred_element_type=F32) + band_ref[1 if start is True else 0]
        if start is not None and start is not True:
            s = jnp.where(col >= jnp.where(start, blk, 0), s, NEG_INF)
        m = jnp.max(s, axis=1, keepdims=True)
        p = jnp.exp(s - m)
        l = jnp.sum(p, axis=1, keepdims=True)
        o4 = jnp.dot(p.astype(BF16), vc, preferred_element_type=F32) / l
        lse4 = m + jnp.log(l)
        o = jnp.zeros((blk, gw), F32)
        lse = jnp.zeros((blk, gw), F32)
        for h in range(nh):
            hm = head_of_lane == h
            o = jnp.where(hm, o4[h * blk:(h + 1) * blk, :], o)
            lse = jnp.where(hm, lse4[h * blk:(h + 1) * blk, :], lse)
        o_ref[pl.ds(r0, blk), :] = o
        lse_ref[pl.ds(r0, blk), :] = lse

    U = ATT_UNROLL
    assert U % nbs == 0 or nbs % U == 0

    def body(it, carry):
        for j in range(U):
            if U % nbs == 0:
                start = True if j % nbs == 0 else None
            else:
                start = (((i * R + it * U) % nbs) == 0) if j == 0 else None
            one_block(it * U + j, start)
        return carry

    lax.fori_loop(0, R // U, body, 0)


def _attention(a, nbs):
    N = a.shape[0]
    R = ATT_BLOCKS_PER_STEP
    gw = ATT_GROUP_WIDTH
    tm = R * ATT_BLOCK
    return pl.pallas_call(
        functools.partial(_attn_kernel, nbs),
        grid=(N // tm,),
        in_specs=[pl.BlockSpec((tm, 3 * gw), lambda i: (i, 0)),
                  pl.BlockSpec((ATT_BLOCK, 3 * gw), lambda i: (jnp.maximum(i * R - 1, 0), 0))],
        out_specs=[pl.BlockSpec((tm, gw), lambda i: (i, 0))] * 2,
        out_shape=[jax.ShapeDtypeStruct((N, gw), F32)] * 2,
        scratch_shapes=[pltpu.VMEM((tm + ATT_BLOCK, 2 * gw), BF16),
                        pltpu.VMEM((2, ATT_HEADS_PER_GROUP * ATT_BLOCK, 2 * ATT_BLOCK), F32)],
        name="attn",
    )(a, a)


def _mid_kernel(spt, u_ref, uh_ref, o0_ref, l0_ref, o1_ref, l1_ref, o2_ref, l2_ref, x_ref,
                pbd_ref, psc_ref, wout_ref, gate1_ref, gffn_ref, sh2_ref, sc2_ref, gate2_ref,
                wsg_ref, wsu_ref, wsd_ref, wr_ref, bias_ref, before_ref,
                xacc_ref, h2_ref, e_ref, g_ref, r_ref, cnt_ref, til_ref, base_ref):
    i = pl.program_id(0)
    tm, pw = u_ref.shape
    si = i % spt
    u = u_ref[...]
    keep = jnp.full((POOL_HALO, pw), si, I32) > 0
    ext = jnp.concatenate([jnp.where(keep, uh_ref[...], 0.0), u], axis=0)
    lane_grp = lax.broadcasted_iota(I32, (tm, pw), 1) // (pw // len(POOL_WINDOWS))
    pooled = jnp.zeros((tm, pw), F32)
    s, w = ext, 1
    while w < POOL_HALO:
        s = s + pltpu.roll(s, w, axis=0)
        w *= 2
        if w in POOL_WINDOWS:
            pooled = jnp.where(lane_grp == POOL_WINDOWS.index(w), s[POOL_HALO:, :], pooled)
    win = jnp.zeros((tm, pw), I32)
    for g, w in enumerate(POOL_WINDOWS):
        win = jnp.where(lane_grp == g, w, win)
    pos = si * tm + lax.broadcasted_iota(I32, (tm, pw), 0)
    cnt = jnp.minimum(pos + 1, win).astype(F32)
    pooled = pooled / cnt - u
    pool_out = jnp.dot(pooled.astype(BF16), pbd_ref[...], preferred_element_type=F32) * psc_ref[...]
    def token_order(slot, ref):
        d, n, w = ref.shape
        for r in range(d):
            for c in range(w // LANES):
                til_ref[slot, c, pl.ds(r, n, stride=d), :] = ref[r, :, c * LANES:(c + 1) * LANES]
        return jnp.concatenate([til_ref[slot, c] for c in range(w // LANES)], axis=1)

    l0 = l0_ref[...]
    l1 = token_order(0, l1_ref)
    l2 = token_order(1, l2_ref)
    m = jnp.maximum(jnp.maximum(l0, l1), l2)
    w0 = jnp.exp(l0 - m)
    w1 = jnp.exp(l1 - m)
    w2 = jnp.exp(l2 - m)
    attn = (w0 * o0_ref[...] + w1 * token_order(2, o1_ref) + w2 * token_order(3, o2_ref)) / (w0 + w1 + w2)
    mixed = (jnp.dot(pool_out.astype(BF16), wout_ref[0:pw, :], preferred_element_type=F32)
             + jnp.dot(attn.astype(BF16), wout_ref[pw:, :], preferred_element_type=F32))
    x1 = x_ref[...] + gate1_ref[0] * mixed
    h2 = _rms(x1) * gffn_ref[...]
    h2 = h2 * (1.0 + sc2_ref[0]) + sh2_ref[0]
    h2_ref[...] = _pack_bf16_pairs(h2)
    hb = h2.astype(BF16)
    a = jnp.dot(hb, wsg_ref[...], preferred_element_type=F32)
    b = jnp.dot(hb, wsu_ref[...], preferred_element_type=F32)
    act = (a * _sigmoid(a)) * b
    shared = jnp.dot(act.astype(BF16), wsd_ref[...], preferred_element_type=F32)
    xacc_ref[...] = x1 + gate2_ref[0] * shared
    @pl.when(i == 0)
    def _():
        base_ref[...] = jnp.zeros_like(base_ref)

    C = e_ref.shape[2]
    for t0 in range(0, tm, ROUTE_TILE):
        idxs, gates, ranks = _route_tile(hb[t0:t0 + ROUTE_TILE, :], wr_ref, bias_ref, before_ref, base_ref)
        for k in range(TOP_K):
            g_ref[k:k + 1, t0:t0 + ROUTE_TILE] = gates[k]
            for c in range(ROUTE_TILE // C):
                e_ref[t0 // C + c, k:k + 1, :] = idxs[k][:, c * C:(c + 1) * C]
                r_ref[t0 // C + c, k:k + 1, :] = ranks[k][:, c * C:(c + 1) * C]
    cnt_ref[...] = base_ref[...].astype(I32)


def _mid(u, attn_outs, xf, pool_bd, pool_scale, w_out_b, gate1, g_ffn, shift2, scale2, gate2,
         wsg_b, wsu_b, wsd_b, wr_t, bias_col, S):
    N, D = xf.shape
    E = wr_t.shape[0]
    tok = jnp.arange(ROUTE_TILE, dtype=I32)
    before = (tok[:, None] < tok[None, :]).astype(BF16)
    pw = u.shape[1]
    tm = MID_TILE
    spt = S // tm
    row = lambda i: (i, 0)
    vec = lambda i: (i // spt, 0, 0)
    full = lambda a: pl.BlockSpec(a.shape, lambda i: (0,) * a.ndim)
    hpt = tm // POOL_HALO
    in_specs = [pl.BlockSpec((tm, pw), row),
                pl.BlockSpec((POOL_HALO, pw), lambda i: (jnp.maximum(i * hpt - 1, 0), 0))]
    gw = ATT_GROUP_WIDTH
    in_specs += [pl.BlockSpec((tm, gw), row)] * 2
    for d in ATT_DILATIONS[1:]:
        in_specs += [pl.BlockSpec((d, tm // d, gw), lambda i: (i // spt, i % spt, 0))] * 2
    in_specs += [pl.BlockSpec((tm, D), row), full(pool_bd), full(pool_scale), full(w_out_b),
                 pl.BlockSpec((1, 1, D), vec), full(g_ffn), pl.BlockSpec((1, 1, D), vec),
                 pl.BlockSpec((1, 1, D), vec), pl.BlockSpec((1, 1, D), vec),
                 full(wsg_b), full(wsu_b), full(wsd_b), full(wr_t), full(bias_col), full(before)]
    col = lambda i: (0, i)
    C = SC_CHUNK
    chunked = pl.BlockSpec((tm // C, TOP_K, C), lambda i: (i, 0, 0))
    return pl.pallas_call(
        functools.partial(_mid_kernel, spt),
        grid=(N // tm,),
        in_specs=in_specs,
        out_specs=[pl.BlockSpec((tm, D), row), pl.BlockSpec((tm, D // 2), row),
                   chunked, pl.BlockSpec((TOP_K, tm), col), chunked,
                   pl.BlockSpec((E, 1), lambda i: (0, 0))],
        out_shape=[jax.ShapeDtypeStruct((N, D), F32), jax.ShapeDtypeStruct((N, D // 2), U32),
                   jax.ShapeDtypeStruct((N // C, TOP_K, C), I32), jax.ShapeDtypeStruct((TOP_K, N), F32),
                   jax.ShapeDtypeStruct((N // C, TOP_K, C), I32), jax.ShapeDtypeStruct((E, 1), I32)],
        scratch_shapes=[pltpu.VMEM((4, gw // LANES, tm, LANES), F32), pltpu.VMEM((E, 1), F32)],
        compiler_params=pltpu.CompilerParams(dimension_semantics=("arbitrary",)),
        name="mid",
    )(u, u, *attn_outs, xf, pool_bd, pool_scale, w_out_b, gate1, g_ffn, shift2, scale2, gate2,
      wsg_b, wsu_b, wsd_b, wr_t, bias_col, before)


def _route_tile(hb, wr_ref, bias_ref, before_ref, base_ref):
    T = hb.shape[0]
    E = wr_ref.shape[0]
    gsz = E // N_EXPERT_GROUPS
    logits = lax.dot_general(wr_ref[...], hb, (((1,), (1,)), ((), ())), preferred_element_type=F32)
    scores = _sigmoid(logits)
    biased = scores + bias_ref[...]
    giota = lax.broadcasted_iota(I32, (gsz, T), 0)
    gscore = []
    for g in range(N_EXPERT_GROUPS):
        blk = biased[g * gsz:(g + 1) * gsz, :]
        m1 = jnp.max(blk, axis=0, keepdims=True)
        i1 = jnp.min(jnp.where(blk == m1, giota, gsz), axis=0, keepdims=True)
        m2 = jnp.max(jnp.where(giota == i1, NEG_INF, blk), axis=0, keepdims=True)
        gscore.append(m1 + m2)
    parts = []
    for g in range(N_EXPERT_GROUPS):
        beaten = jnp.zeros((1, T), I32)
        for o in range(N_EXPERT_GROUPS):
            if o == g:
                continue
            wins = (gscore[o] >= gscore[g]) if o < g else (gscore[o] > gscore[g])
            beaten = beaten + wins.astype(I32)
        keep = jnp.broadcast_to(beaten, (gsz, T)) < TOPK_GROUPS
        parts.append(jnp.where(keep, biased[g * gsz:(g + 1) * gsz, :], NEG_INF))
    cur = jnp.concatenate(parts, axis=0)
    eiota = lax.broadcasted_iota(I32, (E, T), 0)
    live = cur > NEG_INF
    idxs, gates = [], []
    for k in range(TOP_K):
        m = jnp.max(cur, axis=0, keepdims=True)
        idx = jnp.min(jnp.where(cur == m, eiota, E), axis=0, keepdims=True)
        oh = eiota == idx
        gates.append(jnp.sum(jnp.where(oh, scores, 0.0), axis=0, keepdims=True))
        idxs.append(idx)
        cur = jnp.where(oh, NEG_INF, cur)
    selm = jnp.where(live & (cur == NEG_INF), 1.0, 0.0)
    gsum = gates[0]
    for k in range(1, TOP_K):
        gsum = gsum + gates[k]
    gates = [gk / gsum * ROUTED_SCALE for gk in gates]
    tot = jnp.dot(selm.astype(BF16), before_ref[...], preferred_element_type=F32) + base_ref[...]
    ranks = [jnp.sum(jnp.where(eiota == idxs[k], tot, 0.0), axis=0, keepdims=True).astype(I32)
             for k in range(TOP_K)]
    base_ref[...] = base_ref[...] + jnp.sum(selm, axis=1, keepdims=True)
    return idxs, gates, ranks


def _sc_dispatch(eidx_c, rank_c, offs, h2p, P):
    N, W = h2p.shape
    nch, K, C = eidx_c.shape
    E = offs.shape[0]
    info = plsc.get_sparse_core_info()
    nw = info.num_cores * info.num_subcores
    L = info.num_lanes
    per_w = nch // nw
    mesh = plsc.VectorSubcoreMesh(core_axis_name="c", subcore_axis_name="s")

    @functools.partial(
        pl.kernel, mesh=mesh,
        out_type=[jax.ShapeDtypeStruct((P, W), h2p.dtype), jax.ShapeDtypeStruct((nch, K, C), I32)],
        scratch_types=[pltpu.VMEM((E,), I32), pltpu.VMEM((K, C), I32), pltpu.VMEM((K, C), I32),
                       pltpu.VMEM((K, C), I32), pltpu.VMEM((C, W), h2p.dtype), pltpu.SemaphoreType.DMA],
        compiler_params=pltpu.CompilerParams(needs_layout_passes=False),
        name="sc_dispatch",
    )
    def k(e_hbm, r_hbm, off_hbm, h_hbm, xs_hbm, dest_hbm, off_v, e_v, r_v, idx_v, rows_v, sem):
        wid = lax.axis_index("s") * info.num_cores + lax.axis_index("c")
        pltpu.sync_copy(off_hbm, off_v)

        @pl.loop(0, per_w)
        def _(j):
            ch = wid * per_w + j
            pltpu.sync_copy(e_hbm.at[ch], e_v)
            pltpu.sync_copy(r_hbm.at[ch], r_v)
            for kk in range(K):
                for q in range(C // L):
                    sl = pl.ds(q * L, L)
                    idx_v[kk, sl] = plsc.load_gather(off_v, [e_v[kk, sl]]) + r_v[kk, sl]
            pltpu.sync_copy(idx_v, dest_hbm.at[ch])
            pltpu.sync_copy(h_hbm.at[pl.ds(ch * C, C)], rows_v)
            copies = [pltpu.async_copy(rows_v, xs_hbm.at[idx_v.at[kk]], sem) for kk in range(K)]
            for cp in copies:
                cp.wait()

    return k(eidx_c, rank_c, offs, h2p)


def _gmm_kernel(bstart_ref, nbe_ref, cnt_ref, nu_ref, wg_ref, wu_ref, wd_ref, xs_hbm, ys_hbm,
                xbuf, ybuf, xsem, ysem):
    step = pl.program_id(0)
    last = pl.num_programs(0) - 1
    epg = wg_ref.shape[0]
    ring, bm = xbuf.shape[0], xbuf.shape[1]
    nblk = ys_hbm.shape[0] // bm
    nused = nu_ref[0]

    def x_copy(b, slot):
        return pltpu.make_async_copy(xs_hbm.at[pl.ds(pl.multiple_of(b * bm, bm), bm), :], xbuf.at[slot],
                                     xsem.at[slot])

    def y_copy(b, slot):
        return pltpu.make_async_copy(ybuf.at[slot], ys_hbm.at[pl.ds(pl.multiple_of(b * bm, bm), bm), :],
                                     ysem.at[slot])

    @pl.when(step == 0)
    def _():
        for j in range(ring - 1):
            @pl.when(j < nused)
            def _():
                x_copy(j, j).start()

    def run_expert(ee):
        e = step * epg + ee
        b0 = bstart_ref[e]
        nb = nbe_ref[e]

        @pl.when(nb > 0)
        def _():
            def prefetch(t):
                @pl.when(t < nused)
                def _():
                    x_copy(t, jnp.bitwise_and(t, ring - 1)).start()

            def process(b, n):
                slots = [jnp.bitwise_and(b + j, ring - 1) for j in range(n)]
                for j in range(n):
                    x_copy(b + j, slots[j]).wait()
                prefetch(b + ring - 1)
                for j in range(n):
                    @pl.when(b + j >= ring)
                    def _():
                        y_copy(b + j - ring, slots[j]).wait()
                rows = lax.broadcasted_iota(I32, (n * bm, 1), 0)
                valid = cnt_ref[e] - (b - b0) * bm
                xp = jnp.concatenate([xbuf[s] for s in slots], axis=0)
                xb = jnp.concatenate(_unpack_bf16_pairs(jnp.where(rows < valid, xp, jnp.uint32(0))), axis=1)
                a = jnp.dot(xb, wg_ref[ee].astype(BF16), preferred_element_type=F32)
                g = jnp.dot(xb, wu_ref[ee].astype(BF16), preferred_element_type=F32)
                act = (a * _sigmoid(a)) * g
                yp = _pack_bf16_pairs(jnp.dot(act.astype(BF16), wd_ref[ee].astype(BF16),
                                              preferred_element_type=F32))
                for j in range(n):
                    ybuf[slots[j]] = yp[j * bm:(j + 1) * bm, :]
                    y_copy(b + j, slots[j]).start()
                for j in range(1, n):
                    prefetch(b + ring - 1 + j)

            quads = lax.shift_right_logical(nb, 2)
            lax.fori_loop(0, quads, lambda j, c: (process(b0 + 4 * j, 4), c)[1], 0)
            rest2 = jnp.bitwise_and(nb, 2)

            @pl.when(rest2 != 0)
            def _():
                process(b0 + 4 * quads, 2)

            @pl.when(jnp.bitwise_and(nb, 1) != 0)
            def _():
                process(b0 + 4 * quads + rest2, 1)

    for ee in range(epg):
        run_expert(ee)

    @pl.when(step == last)
    def _():
        for back in range(ring, 0, -1):
            @pl.when(nused >= back)
            def _():
                y_copy(nused - back, jnp.bitwise_and(nused - back, ring - 1)).wait()
        ybuf[0] = jnp.zeros(ybuf.shape[1:], ybuf.dtype)
        lax.fori_loop(nused, nblk, lambda b, c: (y_copy(b, 0).start(), c)[1], 0)
        lax.fori_loop(nused, nblk, lambda b, c: (y_copy(b, 0).wait(), c)[1], 0)


def _gmm(bstart, nb_e, counts, nused, xs, w_gate, w_up, w_down):
    P, W = xs.shape
    E, D, F = w_gate.shape
    bm = GMM_BLOCK
    epg = GMM_EXPERTS_PER_STEP
    wsel = lambda s, *_: (s, 0, 0)
    grid_spec = pltpu.PrefetchScalarGridSpec(
        num_scalar_prefetch=4,
        grid=(E // epg,),
        in_specs=[pl.BlockSpec((epg, D, F), wsel), pl.BlockSpec((epg, D, F), wsel), pl.BlockSpec((epg, F, D), wsel),
                  pl.BlockSpec(memory_space=pl.ANY)],
        out_specs=pl.BlockSpec(memory_space=pl.ANY),
        scratch_shapes=[pltpu.VMEM((GMM_RING, bm, W), xs.dtype), pltpu.VMEM((GMM_RING, bm, W), xs.dtype),
                        pltpu.SemaphoreType.DMA((GMM_RING,)), pltpu.SemaphoreType.DMA((GMM_RING,))],
    )
    return pl.pallas_call(
        _gmm_kernel,
        grid_spec=grid_spec,
        out_shape=jax.ShapeDtypeStruct((P, W), xs.dtype),
        compiler_params=pltpu.CompilerParams(dimension_semantics=("arbitrary",)),
        name="gmm",
    )(bstart, nb_e, counts, nused, w_gate, w_up, w_down, xs)


def _sc_gather(dest_c, ys):
    nch, K, C = dest_c.shape
    W = ys.shape[1]
    H = SC_GATHER_ROWS
    info = plsc.get_sparse_core_info()
    nw = info.num_cores * info.num_subcores
    per_w = nch // nw
    nbuf = SC_GATHER_BUFS
    ahead = nbuf - 1
    mesh = plsc.VectorSubcoreMesh(core_axis_name="c", subcore_axis_name="s")
    items = [(kk, hh) for kk in range(K) for hh in range(C // H)]

    @functools.partial(
        pl.kernel, mesh=mesh,
        out_type=jax.ShapeDtypeStruct((K, nch * C, W), ys.dtype),
        scratch_types=([pltpu.VMEM((K, C), I32)] + [pltpu.VMEM((H, W), ys.dtype)] * nbuf
                       + [pltpu.SemaphoreType.DMA] * (2 * nbuf)),
        name="sc_gather",
    )
    def k(dest_hbm, ys_hbm, yk_hbm, idx_v, *rest):
        bufs, gsem, wsem = rest[:nbuf], rest[nbuf:2 * nbuf], rest[2 * nbuf:]
        wid = lax.axis_index("s") * info.num_cores + lax.axis_index("c")

        @pl.loop(0, per_w)
        def _(j):
            ch = wid * per_w + j
            pltpu.sync_copy(dest_hbm.at[ch], idx_v)

            def gather(i):
                kk, hh = items[i]
                return pltpu.async_copy(ys_hbm.at[idx_v.at[kk, pl.ds(hh * H, H)]], bufs[i % nbuf], gsem[i % nbuf])

            def write(i):
                kk, hh = items[i]
                return pltpu.async_copy(bufs[i % nbuf], yk_hbm.at[kk, pl.ds(ch * C + hh * H, H)], wsem[i % nbuf])

            n = len(items)
            g = {i: gather(i) for i in range(ahead)}
            w = {}
            for i in range(n):
                g[i].wait()
                w[i] = write(i)
                if i + ahead < n:
                    if i >= 1:
                        w.pop(i - 1).wait()
                    g[i + ahead] = gather(i + ahead)
            for i in sorted(w):
                w[i].wait()

    return k(dest_c, ys)


def _combine_kernel(yk_ref, g_ref, gate2_ref, xacc_ref, gfin_ref, o_ref, gpad_ref):
    @pl.when(pl.program_id(0) == 0)
    def _():
        gpad_ref[...] = jnp.zeros_like(gpad_ref)

    gpad_ref[0:TOP_K, :] = g_ref[...]
    gt = gpad_ref[...].T
    hi_mask = jnp.uint32(0xFFFF0000)
    acc_lo = acc_hi = None
    for k in range(TOP_K):
        p = yk_ref[k]
        g = gt[:, k:k + 1]
        lo = lax.bitcast_convert_type(p << 16, F32) * g
        hi = lax.bitcast_convert_type(p & hi_mask, F32) * g
        acc_lo = lo if k == 0 else acc_lo + lo
        acc_hi = hi if k == 0 else acc_hi + hi
    routed = jnp.concatenate([acc_lo, acc_hi], axis=1)
    x2 = xacc_ref[...] + gate2_ref[0] * routed
    o_ref[...] = _rms(x2) * gfin_ref[...]


def _combine(yk, gates, gate2, xacc, g_final, S):
    N, D = xacc.shape
    W = yk.shape[2]
    T = COMBINE_TILE
    spt = S // T
    return pl.pallas_call(
        _combine_kernel,
        grid=(N // T,),
        in_specs=[pl.BlockSpec((TOP_K, T, W), lambda i: (0, i, 0)),
                  pl.BlockSpec((TOP_K, T), lambda i: (0, i)),
                  pl.BlockSpec((1, 1, D), lambda i: (i // spt, 0, 0)),
                  pl.BlockSpec((T, D), lambda i: (i, 0)),
                  pl.BlockSpec((1, D), lambda i: (0, 0))],
        out_specs=pl.BlockSpec((T, D), lambda i: (i, 0)),
        out_shape=jax.ShapeDtypeStruct((N, D), F32),
        scratch_shapes=[pltpu.VMEM((LANES, T), F32)],
        compiler_params=pltpu.CompilerParams(dimension_semantics=("arbitrary",)),
        name="combine",
    )(yk, gates, gate2, xacc, g_final)


def _layer(xf, B, S, mod, g_mix, w_in, pool_w, pool_scale, w_out, g_ffn, w_router, router_bias,
           w_gate, w_up, w_down, ws_gate, ws_up, ws_down):
    N, D = xf.shape
    E = w_router.shape[1]
    pw = pool_scale.shape[0]
    shift1, scale1, gate1, shift2, scale2, gate2 = [m.reshape(B, 1, D) for m in jnp.split(mod, 6, axis=-1)]
    u, qkv = _in_proj(xf, g_mix.reshape(1, D), shift1, scale1, w_in.astype(BF16), S, pw)
    attn_outs = []
    for a, d in zip(qkv, ATT_DILATIONS):
        o, lse = _attention(a, S // d // ATT_BLOCK)
        shape = (N, ATT_GROUP_WIDTH) if d == 1 else (B * d, S // d, ATT_GROUP_WIDTH)
        attn_outs += [o.reshape(shape), lse.reshape(shape)]
    ng = pool_w.shape[0]
    pool_bd = jnp.einsum('gcd,gh->gchd', pool_w, jnp.eye(ng, dtype=pool_w.dtype)).reshape(pw, pw).astype(BF16)
    xacc, h2, eidx, gates, rank, counts = _mid(
        u, attn_outs, xf, pool_bd, pool_scale.reshape(1, pw), w_out.astype(BF16), gate1,
        g_ffn.reshape(1, D), shift2, scale2, gate2,
        ws_gate.astype(BF16), ws_up.astype(BF16), ws_down.astype(BF16),
        w_router.T.astype(BF16), router_bias.reshape(E, 1).astype(F32), S)
    bm = GMM_BLOCK
    nblk = N * TOP_K // bm + E
    nb_e = (counts[:, 0] + bm - 1) // bm
    bend = jnp.cumsum(nb_e)
    bstart = (bend - nb_e).astype(I32)
    nused = bend[-1:].astype(I32)
    xs, dest_c = _sc_dispatch(eidx, rank, bstart * bm, h2, nblk * bm)
    ys = _gmm(bstart, nb_e.astype(I32), counts[:, 0], nused, xs, w_gate, w_up, w_down)
    return _sc_gather(dest_c, ys), gates, gate2, xacc


def kernel(x, c, w_ada, b_ada, g_mix, w_in, pool_w, pool_scale, w_out, g_ffn, w_router, router_bias,
           w_gate, w_up, w_down, ws_gate, ws_up, ws_down, g_final):
    B, S, D = x.shape
    depth = w_ada.shape[0]
    assert depth == 1, "the final residual is fused with the final norm, so exactly one layer is supported"
    assert S % (ATT_DILATIONS[-1] * ATT_BLOCK) == 0 and S % max(IN_TILE, MID_TILE, ATT_BLOCKS_PER_STEP * ATT_BLOCK) == 0
    xf = x.reshape(B * S, D)
    mod = _ada(c, w_ada[0], b_ada[0])
    yk, gates, gate2, xacc = _layer(
        xf, B, S, mod, g_mix[0], w_in[0], pool_w[0], pool_scale[0], w_out[0], g_ffn[0], w_router[0],
        router_bias[0], w_gate[0], w_up[0], w_down[0], ws_gate[0], ws_up[0], ws_down[0])
    out = _combine(yk, gates, gate2, xacc, g_final.reshape(1, D), S)
    return out.reshape(B, S, D)
```

```python
import functools

import jax
import jax.numpy as jnp
from jax import lax
from jax.experimental import pallas as pl
from jax.experimental.pallas import tpu as pltpu
from jax.experimental.pallas import tpu_sc as plsc

F32 = jnp.float32
BF16 = jnp.bfloat16
I32 = jnp.int32
U32 = jnp.uint32

LANES = 128
SINGLE_LOAD_STRIDE = 4
NORM_EPS = 1e-6
POOL_WINDOWS = (2, 4, 8, 16)
POOL_HALO = 16
ATT_DILATIONS = (1, 4, 16)
ATT_BLOCK = 128
ATT_HEADS_PER_GROUP = 4
ATT_HEAD_DIM = 64
ATT_GROUP_WIDTH = ATT_HEADS_PER_GROUP * ATT_HEAD_DIM
N_EXPERT_GROUPS = 8
TOPK_GROUPS = 4
TOP_K = 8
ROUTED_SCALE = 2.5

IN_TILE = 1024
ADA_TILE = 3072
ATT_BLOCKS_PER_STEP = 32
ATT_UNROLL = 8
MID_TILE = 512
ROUTE_TILE = 512
COMBINE_TILE = 512
GMM_BLOCK = 128
GMM_RING = 16
GMM_EXPERTS_PER_STEP = 1
SC_CHUNK = 128
SC_GATHER_ROWS = 32
SC_GATHER_BUFS = 6

NEG_INF = float("-inf")


def _sigmoid(v):
    return 1.0 / (1.0 + jnp.exp(-v))


def _rms(v):
    return v * lax.rsqrt(jnp.mean(v * v, axis=-1, keepdims=True) + NORM_EPS)


def _pack_bf16_pairs(v):
    n = v.shape[1] // 2
    lo = lax.bitcast_convert_type(v[:, :n].astype(BF16).astype(F32), U32)
    hi = lax.bitcast_convert_type(v[:, n:].astype(BF16).astype(F32), U32)
    return (hi & jnp.uint32(0xFFFF0000)) | (lo >> 16)


def _unpack_bf16_pairs(p):
    lo = lax.bitcast_convert_type(p << 16, F32).astype(BF16)
    hi = lax.bitcast_convert_type(p & jnp.uint32(0xFFFF0000), F32).astype(BF16)
    return lo, hi


def _ada_kernel(c_ref, w_ref, b_ref, o_ref):
    c = c_ref[...]
    cs = c * _sigmoid(c)
    o_ref[...] = jnp.dot(cs, w_ref[...], preferred_element_type=F32,
                         precision=lax.Precision.HIGHEST) + b_ref[...]


def _ada(c, w_ada, b_ada):
    B, D = c.shape
    W = w_ada.shape[1]
    tn = ADA_TILE
    return pl.pallas_call(
        _ada_kernel,
        grid=(W // tn,),
        in_specs=[pl.BlockSpec((B, D), lambda j: (0, 0)),
                  pl.BlockSpec((D, tn), lambda j: (0, j)),
                  pl.BlockSpec((1, tn), lambda j: (0, j))],
        out_specs=pl.BlockSpec((B, tn), lambda j: (0, j)),
        out_shape=jax.ShapeDtypeStruct((B, W), F32),
        name="ada",
    )(c, w_ada, b_ada.reshape(1, W))


def _in_kernel(x_ref, g_ref, sh_ref, sc_ref, w_ref, pool_ref, q0_ref, q1_ref, q2_ref, scr_ref, tmp_ref):
    h = _rms(x_ref[...]) * g_ref[...]
    h = h * (1.0 + sc_ref[0]) + sh_ref[0]
    hb = h.astype(BF16)
    tm = x_ref.shape[0]
    pw = pool_ref.shape[1]
    gw = ATT_GROUP_WIDTH
    pool_ref[...] = jnp.dot(hb, w_ref[:, 0:pw], preferred_element_type=F32)
    for g, (out, d) in enumerate(zip((q0_ref, q1_ref, q2_ref), ATT_DILATIONS)):
        for sec in range(3):
            c0 = pw + sec * 3 * gw + g * gw
            res = jnp.dot(hb, w_ref[:, c0:c0 + gw], preferred_element_type=F32)
            if d == 1:
                out[:, sec * gw:(sec + 1) * gw] = res.astype(BF16)
            else:
                for c in range(gw // LANES):
                    scr_ref[c] = res[:, c * LANES:(c + 1) * LANES]
                    src, f1 = scr_ref.at[c], 1
                    if d > SINGLE_LOAD_STRIDE:
                        f1 = SINGLE_LOAD_STRIDE
                        for q in range(f1):
                            tmp_ref[c, q * (tm // f1):(q + 1) * (tm // f1), :] = scr_ref[c, pl.ds(q, tm // f1, stride=f1), :]
                        src = tmp_ref.at[c]
                    for r in range(d):
                        r_lo, r_hi = r % f1, r // f1
                        c1 = sec * gw + c * LANES
                        out[r, :, c1:c1 + LANES] = src[pl.ds(r_lo * (tm // f1) + r_hi, tm // d, stride=d // f1),
                                                       :].astype(BF16)


def _in_proj(xf, g_mix, shift1, scale1, w_in_b, S, pool_width):
    N, D = xf.shape
    tm = IN_TILE
    spt = S // tm
    vec = lambda i: (i // spt, 0, 0)
    row = lambda i: (i, 0)
    gw3 = 3 * ATT_GROUP_WIDTH
    B = N // S
    res_spec = lambda d: pl.BlockSpec((d, tm // d, gw3), lambda i: (i // spt, i % spt, 0))
    res_shape = lambda d: jax.ShapeDtypeStruct((B * d, S // d, gw3), BF16)
    outs = pl.pallas_call(
        _in_kernel,
        grid=(N // tm,),
        in_specs=[pl.BlockSpec((tm, D), row),
                  pl.BlockSpec((1, D), lambda i: (0, 0)),
                  pl.BlockSpec((1, 1, D), vec),
                  pl.BlockSpec((1, 1, D), vec),
                  pl.BlockSpec(w_in_b.shape, lambda i: (0, 0))],
        out_specs=[pl.BlockSpec((tm, pool_width), row), pl.BlockSpec((tm, gw3), row)]
                  + [res_spec(d) for d in ATT_DILATIONS[1:]],
        out_shape=[jax.ShapeDtypeStruct((N, pool_width), F32), jax.ShapeDtypeStruct((N, gw3), BF16)]
                  + [res_shape(d) for d in ATT_DILATIONS[1:]],
        scratch_shapes=[pltpu.VMEM((ATT_GROUP_WIDTH // LANES, tm, LANES), F32)] * 2,
        name="in_proj",
    )(xf, g_mix, shift1, scale1, w_in_b)
    return outs[0], [o.reshape(N, gw3) for o in outs[1:]]


def _attn_kernel(nbs, a_ref, halo_ref, o_ref, lse_ref, kv_ref, band_ref):
    i = pl.program_id(0)
    R = a_ref.shape[0] // ATT_BLOCK
    gw = ATT_GROUP_WIDTH
    blk = ATT_BLOCK
    nh = ATT_HEADS_PER_GROUP
    kv_ref[0:blk, :] = halo_ref[:, gw:3 * gw]
    kv_ref[blk:, :] = a_ref[:, gw:3 * gw]
    row = lax.broadcasted_iota(I32, (nh * blk, 2 * blk), 0) % blk
    col = lax.broadcasted_iota(I32, (nh * blk, 2 * blk), 1)
    in_band = (col >= row) & (col <= row + blk)
    band_ref[0] = jnp.where(in_band, 0.0, NEG_INF)
    band_ref[1] = jnp.where(in_band & (col >= blk), 0.0, NEG_INF)
    head_of_lane = lax.broadcasted_iota(I32, (blk, gw), 1) // ATT_HEAD_DIM
    nt = (((1,), (1,)), ((), ()))

    def one_block(jj, start):
        r0 = pl.multiple_of(jj * blk, blk)
        qf = a_ref[pl.ds(r0, blk), 0:gw].astype(F32) * (ATT_HEAD_DIM ** -0.5)
        q4 = jnp.concatenate([jnp.where(head_of_lane == h, qf, 0.0) for h in range(nh)], axis=0).astype(BF16)
        kc = kv_ref[pl.ds(r0, 2 * blk), 0:gw]
        vc = kv_ref[pl.ds(r0, 2 * blk), gw:2 * gw]
        s = lax.dot_general(q4, kc, nt, preferred_element_type=F32) + band_ref[1 if start is True else 0]
        if start is not None and start is not True:
            s = jnp.where(col >= jnp.where(start, blk, 0), s, NEG_INF)
        m = jnp.max(s, axis=1, keepdims=True)
        p = jnp.exp(s - m)
        l = jnp.sum(p, axis=1, keepdims=True)
        o4 = jnp.dot(p.astype(BF16), vc, preferred_element_type=F32) / l
        lse4 = m + jnp.log(l)
        o = jnp.zeros((blk, gw), F32)
        lse = jnp.zeros((blk, gw), F32)
        for h in range(nh):
            hm = head_of_lane == h
            o = jnp.where(hm, o4[h * blk:(h + 1) * blk, :], o)
            lse = jnp.where(hm, lse4[h * blk:(h + 1) * blk, :], lse)
        o_ref[pl.ds(r0, blk), :] = o
        lse_ref[pl.ds(r0, blk), :] = lse

    U = ATT_UNROLL
    assert U % nbs == 0 or nbs % U == 0

    def body(it, carry):
        for j in range(U):
            if U % nbs == 0:
                start = True if j % nbs == 0 else None
            else:
                start = (((i * R + it * U) % nbs) == 0) if j == 0 else None
            one_block(it * U + j, start)
        return carry

    lax.fori_loop(0, R // U, body, 0)


def _attention(a, nbs):
    N = a.shape[0]
    R = ATT_BLOCKS_PER_STEP
    gw = ATT_GROUP_WIDTH
    tm = R * ATT_BLOCK
    return pl.pallas_call(
        functools.partial(_attn_kernel, nbs),
        grid=(N // tm,),
        in_specs=[pl.BlockSpec((tm, 3 * gw), lambda i: (i, 0)),
                  pl.BlockSpec((ATT_BLOCK, 3 * gw), lambda i: (jnp.maximum(i * R - 1, 0), 0))],
        out_specs=[pl.BlockSpec((tm, gw), lambda i: (i, 0))] * 2,
        out_shape=[jax.ShapeDtypeStruct((N, gw), F32)] * 2,
        scratch_shapes=[pltpu.VMEM((tm + ATT_BLOCK, 2 * gw), BF16),
                        pltpu.VMEM((2, ATT_HEADS_PER_GROUP * ATT_BLOCK, 2 * ATT_BLOCK), F32)],
        name="attn",
    )(a, a)


def _mid_kernel(spt, u_ref, uh_ref, o0_ref, l0_ref, o1_ref, l1_ref, o2_ref, l2_ref, x_ref,
                pbd_ref, psc_ref, wout_ref, gate1_ref, gffn_ref, sh2_ref, sc2_ref, gate2_ref,
                wsg_ref, wsu_ref, wsd_ref, wr_ref, bias_ref, before_ref,
                xacc_ref, h2_ref, e_ref, g_ref, r_ref, cnt_ref, til_ref, base_ref):
    i = pl.program_id(0)
    tm, pw = u_ref.shape
    si = i % spt
    u = u_ref[...]
    keep = jnp.full((POOL_HALO, pw), si, I32) > 0
    ext = jnp.concatenate([jnp.where(keep, uh_ref[...], 0.0), u], axis=0)
    lane_grp = lax.broadcasted_iota(I32, (tm, pw), 1) // (pw // len(POOL_WINDOWS))
    pooled = jnp.zeros((tm, pw), F32)
    s, w = ext, 1
    while w < POOL_HALO:
        s = s + pltpu.roll(s, w, axis=0)
        w *= 2
        if w in POOL_WINDOWS:
            pooled = jnp.where(lane_grp == POOL_WINDOWS.index(w), s[POOL_HALO:, :], pooled)
    win = jnp.zeros((tm, pw), I32)
    for g, w in enumerate(POOL_WINDOWS):
        win = jnp.where(lane_grp == g, w, win)
    pos = si * tm + lax.broadcasted_iota(I32, (tm, pw), 0)
    cnt = jnp.minimum(pos + 1, win).astype(F32)
    pooled = pooled / cnt - u
    pool_out = jnp.dot(pooled.astype(BF16), pbd_ref[...], preferred_element_type=F32) * psc_ref[...]
    def token_order(slot, ref):
        d, n, w = ref.shape
        for r in range(d):
            for c in range(w // LANES):
                til_ref[slot, c, pl.ds(r, n, stride=d), :] = ref[r, :, c * LANES:(c + 1) * LANES]
        return jnp.concatenate([til_ref[slot, c] for c in range(w // LANES)], axis=1)

    l0 = l0_ref[...]
    l1 = token_order(0, l1_ref)
    l2 = token_order(1, l2_ref)
    m = jnp.maximum(jnp.maximum(l0, l1), l2)
    w0 = jnp.exp(l0 - m)
    w1 = jnp.exp(l1 - m)
    w2 = jnp.exp(l2 - m)
    attn = (w0 * o0_ref[...] + w1 * token_order(2, o1_ref) + w2 * token_order(3, o2_ref)) / (w0 + w1 + w2)
    mixed = (jnp.dot(pool_out.astype(BF16), wout_ref[0:pw, :], preferred_element_type=F32)
             + jnp.dot(attn.astype(BF16), wout_ref[pw:, :], preferred_element_type=F32))
    x1 = x_ref[...] + gate1_ref[0] * mixed
    h2 = _rms(x1) * gffn_ref[...]
    h2 = h2 * (1.0 + sc2_ref[0]) + sh2_ref[0]
    h2_ref[...] = _pack_bf16_pairs(h2)
    hb = h2.astype(BF16)
    a = jnp.dot(hb, wsg_ref[...], preferred_element_type=F32)
    b = jnp.dot(hb, wsu_ref[...], preferred_element_type=F32)
    act = (a * _sigmoid(a)) * b
    shared = jnp.dot(act.astype(BF16), wsd_ref[...], preferred_element_type=F32)
    xacc_ref[...] = x1 + gate2_ref[0] * shared
    @pl.when(i == 0)
    def _():
        base_ref[...] = jnp.zeros_like(base_ref)

    C = e_ref.shape[2]
    for t0 in range(0, tm, ROUTE_TILE):
        idxs, gates, ranks = _route_tile(hb[t0:t0 + ROUTE_TILE, :], wr_ref, bias_ref, before_ref, base_ref)
        for k in range(TOP_K):
            g_ref[k:k + 1, t0:t0 + ROUTE_TILE] = gates[k]
            for c in range(ROUTE_TILE // C):
                e_ref[t0 // C + c, k:k + 1, :] = idxs[k][:, c * C:(c + 1) * C]
                r_ref[t0 // C + c, k:k + 1, :] = ranks[k][:, c * C:(c + 1) * C]
    cnt_ref[...] = base_ref[...].astype(I32)


def _mid(u, attn_outs, xf, pool_bd, pool_scale, w_out_b, gate1, g_ffn, shift2, scale2, gate2,
         wsg_b, wsu_b, wsd_b, wr_t, bias_col, S):
    N, D = xf.shape
    E = wr_t.shape[0]
    tok = jnp.arange(ROUTE_TILE, dtype=I32)
    before = (tok[:, None] < tok[None, :]).astype(BF16)
    pw = u.shape[1]
    tm = MID_TILE
    spt = S // tm
    row = lambda i: (i, 0)
    vec = lambda i: (i // spt, 0, 0)
    full = lambda a: pl.BlockSpec(a.shape, lambda i: (0,) * a.ndim)
    hpt = tm // POOL_HALO
    in_specs = [pl.BlockSpec((tm, pw), row),
                pl.BlockSpec((POOL_HALO, pw), lambda i: (jnp.maximum(i * hpt - 1, 0), 0))]
    gw = ATT_GROUP_WIDTH
    in_specs += [pl.BlockSpec((tm, gw), row)] * 2
    for d in ATT_DILATIONS[1:]:
        in_specs += [pl.BlockSpec((d, tm // d, gw), lambda i: (i // spt, i % spt, 0))] * 2
    in_specs += [pl.BlockSpec((tm, D), row), full(pool_bd), full(pool_scale), full(w_out_b),
                 pl.BlockSpec((1, 1, D), vec), full(g_ffn), pl.BlockSpec((1, 1, D), vec),
                 pl.BlockSpec((1, 1, D), vec), pl.BlockSpec((1, 1, D), vec),
                 full(wsg_b), full(wsu_b), full(wsd_b), full(wr_t), full(bias_col), full(before)]
    col = lambda i: (0, i)
    C = SC_CHUNK
    chunked = pl.BlockSpec((tm // C, TOP_K, C), lambda i: (i, 0, 0))
    return pl.pallas_call(
        functools.partial(_mid_kernel, spt),
        grid=(N // tm,),
        in_specs=in_specs,
        out_specs=[pl.BlockSpec((tm, D), row), pl.BlockSpec((tm, D // 2), row),
                   chunked, pl.BlockSpec((TOP_K, tm), col), chunked,
                   pl.BlockSpec((E, 1), lambda i: (0, 0))],
        out_shape=[jax.ShapeDtypeStruct((N, D), F32), jax.ShapeDtypeStruct((N, D // 2), U32),
                   jax.ShapeDtypeStruct((N // C, TOP_K, C), I32), jax.ShapeDtypeStruct((TOP_K, N), F32),
                   jax.ShapeDtypeStruct((N // C, TOP_K, C), I32), jax.ShapeDtypeStruct((E, 1), I32)],
        scratch_shapes=[pltpu.VMEM((4, gw // LANES, tm, LANES), F32), pltpu.VMEM((E, 1), F32)],
        compiler_params=pltpu.CompilerParams(dimension_semantics=("arbitrary",)),
        name="mid",
    )(u, u, *attn_outs, xf, pool_bd, pool_scale, w_out_b, gate1, g_ffn, shift2, scale2, gate2,
      wsg_b, wsu_b, wsd_b, wr_t, bias_col, before)


def _route_tile(hb, wr_ref, bias_ref, before_ref, base_ref):
    T = hb.shape[0]
    E = wr_ref.shape[0]
    gsz = E // N_EXPERT_GROUPS
    logits = lax.dot_general(wr_ref[...], hb, (((1,), (1,)), ((), ())), preferred_element_type=F32)
    scores = _sigmoid(logits)
    biased = scores + bias_ref[...]
    giota = lax.broadcasted_iota(I32, (gsz, T), 0)
    gscore = []
    for g in range(N_EXPERT_GROUPS):
        blk = biased[g * gsz:(g + 1) * gsz, :]
        m1 = jnp.max(blk, axis=0, keepdims=True)
        i1 = jnp.min(jnp.where(blk == m1, giota, gsz), axis=0, keepdims=True)
        m2 = jnp.max(jnp.where(giota == i1, NEG_INF, blk), axis=0, keepdims=True)
        gscore.append(m1 + m2)
    parts = []
    for g in range(N_EXPERT_GROUPS):
        beaten = jnp.zeros((1, T), I32)
        for o in range(N_EXPERT_GROUPS):
            if o == g:
                continue
            wins = (gscore[o] >= gscore[g]) if o < g else (gscore[o] > gscore[g])
            beaten = beaten + wins.astype(I32)
        keep = jnp.broadcast_to(beaten, (gsz, T)) < TOPK_GROUPS
        parts.append(jnp.where(keep, biased[g * gsz:(g + 1) * gsz, :], NEG_INF))
    cur = jnp.concatenate(parts, axis=0)
    eiota = lax.broadcasted_iota(I32, (E, T), 0)
    live = cur > NEG_INF
    idxs, gates = [], []
    for k in range(TOP_K):
        m = jnp.max(cur, axis=0, keepdims=True)
        idx = jnp.min(jnp.where(cur == m, eiota, E), axis=0, keepdims=True)
        oh = eiota == idx
        gates.append(jnp.sum(jnp.where(oh, scores, 0.0), axis=0, keepdims=True))
        idxs.append(idx)
        cur = jnp.where(oh, NEG_INF, cur)
    selm = jnp.where(live & (cur == NEG_INF), 1.0, 0.0)
    gsum = gates[0]
    for k in range(1, TOP_K):
        gsum = gsum + gates[k]
    gates = [gk / gsum * ROUTED_SCALE for gk in gates]
    tot = jnp.dot(selm.astype(BF16), before_ref[...], preferred_element_type=F32) + base_ref[...]
    ranks = [jnp.sum(jnp.where(eiota == idxs[k], tot, 0.0), axis=0, keepdims=True).astype(I32)
             for k in range(TOP_K)]
    base_ref[...] = base_ref[...] + jnp.sum(selm, axis=1, keepdims=True)
    return idxs, gates, ranks


def _sc_dispatch(eidx_c, rank_c, offs, h2p, P):
    N, W = h2p.shape
    nch, K, C = eidx_c.shape
    E = offs.shape[0]
    info = plsc.get_sparse_core_info()
    nw = info.num_cores * info.num_subcores
    L = info.num_lanes
    assert nch % nw == 0, "token chunks must split evenly over the vector subcores"
    per_w = nch // nw
    mesh = plsc.VectorSubcoreMesh(core_axis_name="c", subcore_axis_name="s")

    @functools.partial(
        pl.kernel, mesh=mesh,
        out_type=[jax.ShapeDtypeStruct((P, W), h2p.dtype), jax.ShapeDtypeStruct((nch, K, C), I32)],
        scratch_types=[pltpu.VMEM((E,), I32), pltpu.VMEM((K, C), I32), pltpu.VMEM((K, C), I32),
                       pltpu.VMEM((K, C), I32), pltpu.VMEM((C, W), h2p.dtype), pltpu.SemaphoreType.DMA],
        compiler_params=pltpu.CompilerParams(needs_layout_passes=False),
        name="sc_dispatch",
    )
    def k(e_hbm, r_hbm, off_hbm, h_hbm, xs_hbm, dest_hbm, off_v, e_v, r_v, idx_v, rows_v, sem):
        wid = lax.axis_index("s") * info.num_cores + lax.axis_index("c")
        pltpu.sync_copy(off_hbm, off_v)

        @pl.loop(0, per_w)
        def _(j):
            ch = wid * per_w + j
            pltpu.sync_copy(e_hbm.at[ch], e_v)
            pltpu.sync_copy(r_hbm.at[ch], r_v)
            for kk in range(K):
                for q in range(C // L):
                    sl = pl.ds(q * L, L)
                    idx_v[kk, sl] = plsc.load_gather(off_v, [e_v[kk, sl]]) + r_v[kk, sl]
            pltpu.sync_copy(idx_v, dest_hbm.at[ch])
            pltpu.sync_copy(h_hbm.at[pl.ds(ch * C, C)], rows_v)
            copies = [pltpu.async_copy(rows_v, xs_hbm.at[idx_v.at[kk]], sem) for kk in range(K)]
            for cp in copies:
                cp.wait()

    return k(eidx_c, rank_c, offs, h2p)


def _gmm_kernel(bstart_ref, nbe_ref, cnt_ref, nu_ref, wg_ref, wu_ref, wd_ref, xs_hbm, ys_hbm,
                wgb, wub, wdb, xbuf, ybuf, xsem, ysem):
    step = pl.program_id(0)
    last = pl.num_programs(0) - 1
    epg = wg_ref.shape[0]
    ring, bm = xbuf.shape[0], xbuf.shape[1]
    nblk = ys_hbm.shape[0] // bm
    nused = nu_ref[0]

    def x_copy(b, slot):
        return pltpu.make_async_copy(xs_hbm.at[pl.ds(pl.multiple_of(b * bm, bm), bm), :], xbuf.at[slot],
                                     xsem.at[slot])

    def y_copy(b, slot):
        return pltpu.make_async_copy(ybuf.at[slot], ys_hbm.at[pl.ds(pl.multiple_of(b * bm, bm), bm), :],
                                     ysem.at[slot])

    @pl.when(step == 0)
    def _():
        for j in range(ring - 1):
            @pl.when(j < nused)
            def _():
                x_copy(j, j).start()

    def run_expert(ee):
        e = step * epg + ee
        b0 = bstart_ref[e]
        nb = nbe_ref[e]

        @pl.when(nb > 0)
        def _():
            def prefetch(t):
                @pl.when(t < nused)
                def _():
                    x_copy(t, jnp.bitwise_and(t, ring - 1)).start()

            def process(b, n, fresh):
                slots = [jnp.bitwise_and(b + j, ring - 1) for j in range(n)]
                for j in range(n):
                    x_copy(b + j, slots[j]).wait()
                prefetch(b + ring - 1)
                for j in range(n):
                    @pl.when(b + j >= ring)
                    def _():
                        y_copy(b + j - ring, slots[j]).wait()
                rows = lax.broadcasted_iota(I32, (n * bm, 1), 0)
                valid = cnt_ref[e] - (b - b0) * bm
                xp = jnp.concatenate([xbuf[s] for s in slots], axis=0)
                xb = jnp.concatenate(_unpack_bf16_pairs(jnp.where(rows < valid, xp, jnp.uint32(0))), axis=1)
                if fresh:
                    wg, wu, wd = (w[ee].astype(BF16) for w in (wg_ref, wu_ref, wd_ref))
                    wgb[...], wub[...], wdb[...] = wg, wu, wd
                else:
                    wg, wu, wd = wgb[...], wub[...], wdb[...]
                a = jnp.dot(xb, wg, preferred_element_type=F32)
                g = jnp.dot(xb, wu, preferred_element_type=F32)
                act = (a * _sigmoid(a)) * g
                yp = _pack_bf16_pairs(jnp.dot(act.astype(BF16), wd, preferred_element_type=F32))
                for j in range(n):
                    ybuf[slots[j]] = yp[j * bm:(j + 1) * bm, :]
                    y_copy(b + j, slots[j]).start()
                for j in range(1, n):
                    prefetch(b + ring - 1 + j)

            quads = lax.shift_right_logical(nb, 2)
            lax.fori_loop(0, quads, lambda j, c: (process(b0 + 4 * j, 4, True), c)[1], 0)
            rest2 = jnp.bitwise_and(nb, 2)
            rest1 = jnp.bitwise_and(nb, 1)
            for n, rest, first, have_copy in ((2, rest2, b0 + 4 * quads, quads > 0),
                                              (1, rest1, b0 + 4 * quads + rest2, nb > 1)):
                @pl.when((rest != 0) & have_copy)
                def _():
                    process(first, n, False)

                @pl.when((rest != 0) & jnp.logical_not(have_copy))
                def _():
                    process(first, n, True)

    for ee in range(epg):
        run_expert(ee)

    @pl.when(step == last)
    def _():
        for back in range(ring, 0, -1):
            @pl.when(nused >= back)
            def _():
                y_copy(nused - back, jnp.bitwise_and(nused - back, ring - 1)).wait()
        ybuf[0] = jnp.zeros(ybuf.shape[1:], ybuf.dtype)
        lax.fori_loop(nused, nblk, lambda b, c: (y_copy(b, 0).start(), c)[1], 0)
        lax.fori_loop(nused, nblk, lambda b, c: (y_copy(b, 0).wait(), c)[1], 0)


def _gmm(bstart, nb_e, counts, nused, xs, w_gate, w_up, w_down):
    P, W = xs.shape
    E, D, F = w_gate.shape
    bm = GMM_BLOCK
    epg = GMM_EXPERTS_PER_STEP
    wsel = lambda s, *_: (s, 0, 0)
    grid_spec = pltpu.PrefetchScalarGridSpec(
        num_scalar_prefetch=4,
        grid=(E // epg,),
        in_specs=[pl.BlockSpec((epg, D, F), wsel), pl.BlockSpec((epg, D, F), wsel), pl.BlockSpec((epg, F, D), wsel),
                  pl.BlockSpec(memory_space=pl.ANY)],
        out_specs=pl.BlockSpec(memory_space=pl.ANY),
        scratch_shapes=[pltpu.VMEM((D, F), BF16), pltpu.VMEM((D, F), BF16), pltpu.VMEM((F, D), BF16),
                        pltpu.VMEM((GMM_RING, bm, W), xs.dtype), pltpu.VMEM((GMM_RING, bm, W), xs.dtype),
                        pltpu.SemaphoreType.DMA((GMM_RING,)), pltpu.SemaphoreType.DMA((GMM_RING,))],
    )
    return pl.pallas_call(
        _gmm_kernel,
        grid_spec=grid_spec,
        out_shape=jax.ShapeDtypeStruct((P, W), xs.dtype),
        compiler_params=pltpu.CompilerParams(dimension_semantics=("arbitrary",)),
        name="gmm",
    )(bstart, nb_e, counts, nused, w_gate, w_up, w_down, xs)


def _sc_gather(dest_c, ys):
    nch, K, C = dest_c.shape
    W = ys.shape[1]
    H = SC_GATHER_ROWS
    info = plsc.get_sparse_core_info()
    nw = info.num_cores * info.num_subcores
    assert nch % nw == 0, "token chunks must split evenly over the vector subcores"
    per_w = nch // nw
    nbuf = SC_GATHER_BUFS
    ahead = nbuf - 1
    mesh = plsc.VectorSubcoreMesh(core_axis_name="c", subcore_axis_name="s")
    items = [(kk, hh) for kk in range(K) for hh in range(C // H)]

    @functools.partial(
        pl.kernel, mesh=mesh,
        out_type=jax.ShapeDtypeStruct((K, nch * C, W), ys.dtype),
        scratch_types=([pltpu.VMEM((K, C), I32)] + [pltpu.VMEM((H, W), ys.dtype)] * nbuf
                       + [pltpu.SemaphoreType.DMA] * (2 * nbuf)),
        name="sc_gather",
    )
    def k(dest_hbm, ys_hbm, yk_hbm, idx_v, *rest):
        bufs, gsem, wsem = rest[:nbuf], rest[nbuf:2 * nbuf], rest[2 * nbuf:]
        wid = lax.axis_index("s") * info.num_cores + lax.axis_index("c")

        @pl.loop(0, per_w)
        def _(j):
            ch = wid * per_w + j
            pltpu.sync_copy(dest_hbm.at[ch], idx_v)

            def gather(i):
                kk, hh = items[i]
                return pltpu.async_copy(ys_hbm.at[idx_v.at[kk, pl.ds(hh * H, H)]], bufs[i % nbuf], gsem[i % nbuf])

            def write(i):
                kk, hh = items[i]
                return pltpu.async_copy(bufs[i % nbuf], yk_hbm.at[kk, pl.ds(ch * C + hh * H, H)], wsem[i % nbuf])

            n = len(items)
            g = {i: gather(i) for i in range(ahead)}
            w = {}
            for i in range(n):
                g[i].wait()
                w[i] = write(i)
                if i + ahead < n:
                    if i >= 1:
                        w.pop(i - 1).wait()
                    g[i + ahead] = gather(i + ahead)
            for i in sorted(w):
                w[i].wait()

    return k(dest_c, ys)


def _combine_kernel(yk_ref, g_ref, gate2_ref, xacc_ref, gfin_ref, o_ref, gpad_ref):
    @pl.when(pl.program_id(0) == 0)
    def _():
        gpad_ref[...] = jnp.zeros_like(gpad_ref)

    gpad_ref[0:TOP_K, :] = g_ref[...]
    gt = gpad_ref[...].T
    hi_mask = jnp.uint32(0xFFFF0000)
    acc_lo = acc_hi = None
    for k in range(TOP_K):
        p = yk_ref[k]
        g = gt[:, k:k + 1]
        lo = lax.bitcast_convert_type(p << 16, F32) * g
        hi = lax.bitcast_convert_type(p & hi_mask, F32) * g
        acc_lo = lo if k == 0 else acc_lo + lo
        acc_hi = hi if k == 0 else acc_hi + hi
    routed = jnp.concatenate([acc_lo, acc_hi], axis=1)
    x2 = xacc_ref[...] + gate2_ref[0] * routed
    o_ref[...] = _rms(x2) * gfin_ref[...]


def _combine(yk, gates, gate2, xacc, g_final, S):
    N, D = xacc.shape
    W = yk.shape[2]
    T = COMBINE_TILE
    spt = S // T
    return pl.pallas_call(
        _combine_kernel,
        grid=(N // T,),
        in_specs=[pl.BlockSpec((TOP_K, T, W), lambda i: (0, i, 0)),
                  pl.BlockSpec((TOP_K, T), lambda i: (0, i)),
                  pl.BlockSpec((1, 1, D), lambda i: (i // spt, 0, 0)),
                  pl.BlockSpec((T, D), lambda i: (i, 0)),
                  pl.BlockSpec((1, D), lambda i: (0, 0))],
        out_specs=pl.BlockSpec((T, D), lambda i: (i, 0)),
        out_shape=jax.ShapeDtypeStruct((N, D), F32),
        scratch_shapes=[pltpu.VMEM((LANES, T), F32)],
        compiler_params=pltpu.CompilerParams(dimension_semantics=("arbitrary",)),
        name="combine",
    )(yk, gates, gate2, xacc, g_final)


def _layer(xf, B, S, mod, g_mix, w_in, pool_w, pool_scale, w_out, g_ffn, w_router, router_bias,
           w_gate, w_up, w_down, ws_gate, ws_up, ws_down):
    N, D = xf.shape
    E = w_router.shape[1]
    pw = pool_scale.shape[0]
    shift1, scale1, gate1, shift2, scale2, gate2 = [m.reshape(B, 1, D) for m in jnp.split(mod, 6, axis=-1)]
    u, qkv = _in_proj(xf, g_mix.reshape(1, D), shift1, scale1, w_in.astype(BF16), S, pw)
    attn_outs = []
    for a, d in zip(qkv, ATT_DILATIONS):
        o, lse = _attention(a, S // d // ATT_BLOCK)
        shape = (N, ATT_GROUP_WIDTH) if d == 1 else (B * d, S // d, ATT_GROUP_WIDTH)
        attn_outs += [o.reshape(shape), lse.reshape(shape)]
    ng = pool_w.shape[0]
    pool_bd = jnp.einsum('gcd,gh->gchd', pool_w, jnp.eye(ng, dtype=pool_w.dtype)).reshape(pw, pw).astype(BF16)
    xacc, h2, eidx, gates, rank, counts = _mid(
        u, attn_outs, xf, pool_bd, pool_scale.reshape(1, pw), w_out.astype(BF16), gate1,
        g_ffn.reshape(1, D), shift2, scale2, gate2,
        ws_gate.astype(BF16), ws_up.astype(BF16), ws_down.astype(BF16),
        w_router.T.astype(BF16), router_bias.reshape(E, 1).astype(F32), S)
    bm = GMM_BLOCK
    nblk = N * TOP_K // bm + E
    nb_e = (counts[:, 0] + bm - 1) // bm
    bend = jnp.cumsum(nb_e)
    bstart = (bend - nb_e).astype(I32)
    nused = bend[-1:].astype(I32)
    xs, dest_c = _sc_dispatch(eidx, rank, bstart * bm, h2, nblk * bm)
    ys = _gmm(bstart, nb_e.astype(I32), counts[:, 0], nused, xs, w_gate, w_up, w_down)
    return _sc_gather(dest_c, ys), gates, gate2, xacc


def kernel(x, c, w_ada, b_ada, g_mix, w_in, pool_w, pool_scale, w_out, g_ffn, w_router, router_bias,
           w_gate, w_up, w_down, ws_gate, ws_up, ws_down, g_final):
    B, S, D = x.shape
    depth = w_ada.shape[0]
    assert depth == 1, "the final residual is fused with the final norm, so exactly one layer is supported"
    assert S % (ATT_DILATIONS[-1] * ATT_BLOCK) == 0
    assert all(S % t == 0 for t in (IN_TILE, MID_TILE, COMBINE_TILE, ATT_BLOCKS_PER_STEP * ATT_BLOCK))
    assert MID_TILE % ROUTE_TILE == 0 and ROUTE_TILE % SC_CHUNK == 0 and SC_CHUNK % SC_GATHER_ROWS == 0
    assert w_gate.shape[1] % GMM_EXPERTS_PER_STEP == 0 and max(POOL_WINDOWS) <= POOL_HALO
    xf = x.reshape(B * S, D)
    mod = _ada(c, w_ada[0], b_ada[0])
    yk, gates, gate2, xacc = _layer(
        xf, B, S, mod, g_mix[0], w_in[0], pool_w[0], pool_scale[0], w_out[0], g_ffn[0], w_router[0],
        router_bias[0], w_gate[0], w_up[0], w_down[0], ws_gate[0], ws_up[0], ws_down[0])
    out = _combine(yk, gates, gate2, xacc, g_final.reshape(1, D), S)
    return out.reshape(B, S, D)
```

```python
import functools

import jax
import jax.numpy as jnp
from jax import lax
from jax.experimental import pallas as pl
from jax.experimental.pallas import tpu as pltpu
from jax.experimental.pallas import tpu_sc as plsc

F32 = jnp.float32
BF16 = jnp.bfloat16
I32 = jnp.int32
U32 = jnp.uint32

LANES = 128
SINGLE_LOAD_STRIDE = 4
NORM_EPS = 1e-6
POOL_WINDOWS = (2, 4, 8, 16)
POOL_HALO = 16
ATT_DILATIONS = (1, 4, 16)
ATT_BLOCK = 128
ATT_HEADS_PER_GROUP = 4
ATT_HEAD_DIM = 64
ATT_GROUP_WIDTH = ATT_HEADS_PER_GROUP * ATT_HEAD_DIM
N_EXPERT_GROUPS = 8
TOPK_GROUPS = 4
TOP_K = 8
ROUTED_SCALE = 2.5

IN_TILE = 1024
ATT_BLOCKS_PER_STEP = 16
ATT_UNROLL = 8
MID_TILE = 512
ROUTE_TILE = 512
COMBINE_TILE = 512
GMM_BLOCK = 128
GMM_RING = 32
GMM_EXPERTS_PER_STEP = 1
SC_CHUNK = 128
SC_GATHER_ROWS = 32
SC_GATHER_BUFS = 6

NEG_INF = float("-inf")


def _sigmoid(v):
    return 1.0 / (1.0 + jnp.exp(-v))


def _rms(v):
    return v * lax.rsqrt(jnp.mean(v * v, axis=-1, keepdims=True) + NORM_EPS)


def _pack_bf16_pairs(v):
    n = v.shape[1] // 2
    lo = lax.bitcast_convert_type(v[:, :n].astype(BF16).astype(F32), U32)
    hi = lax.bitcast_convert_type(v[:, n:].astype(BF16).astype(F32), U32)
    return (hi & jnp.uint32(0xFFFF0000)) | (lo >> 16)


def _unpack_bf16_pairs(p):
    lo = lax.bitcast_convert_type(p << 16, F32).astype(BF16)
    hi = lax.bitcast_convert_type(p & jnp.uint32(0xFFFF0000), F32).astype(BF16)
    return lo, hi


def _ada_kernel(c_ref, w_ref, b_ref, o_ref):
    c = c_ref[...]
    cs = c * _sigmoid(c)
    o_ref[...] = jnp.dot(cs, w_ref[...], preferred_element_type=F32,
                         precision=lax.Precision.HIGHEST) + b_ref[...]


def _ada(c, w_ada, b_ada):
    B, D = c.shape
    W = w_ada.shape[1]
    tn = 1024
    return pl.pallas_call(
        _ada_kernel,
        grid=(W // tn,),
        in_specs=[pl.BlockSpec((B, D), lambda j: (0, 0)),
                  pl.BlockSpec((D, tn), lambda j: (0, j)),
                  pl.BlockSpec((1, tn), lambda j: (0, j))],
        out_specs=pl.BlockSpec((B, tn), lambda j: (0, j)),
        out_shape=jax.ShapeDtypeStruct((B, W), F32),
        name="ada",
    )(c, w_ada, b_ada.reshape(1, W))


def _in_kernel(x_ref, g_ref, sh_ref, sc_ref, w_ref, pool_ref, q0_ref, q1_ref, q2_ref, scr_ref, tmp_ref):
    h = _rms(x_ref[...]) * g_ref[...]
    h = h * (1.0 + sc_ref[0]) + sh_ref[0]
    hb = h.astype(BF16)
    tm = x_ref.shape[0]
    pw = pool_ref.shape[1]
    gw = ATT_GROUP_WIDTH
    pool_ref[...] = jnp.dot(hb, w_ref[:, 0:pw], preferred_element_type=F32)
    for g, (out, d) in enumerate(zip((q0_ref, q1_ref, q2_ref), ATT_DILATIONS)):
        for sec in range(3):
            c0 = pw + sec * 3 * gw + g * gw
            res = jnp.dot(hb, w_ref[:, c0:c0 + gw], preferred_element_type=F32)
            if d == 1:
                out[:, sec * gw:(sec + 1) * gw] = res.astype(BF16)
            else:
                for c in range(gw // LANES):
                    scr_ref[c] = res[:, c * LANES:(c + 1) * LANES]
                    src, f1 = scr_ref.at[c], 1
                    if d > SINGLE_LOAD_STRIDE:
                        f1 = SINGLE_LOAD_STRIDE
                        for q in range(f1):
                            tmp_ref[c, q * (tm // f1):(q + 1) * (tm // f1), :] = scr_ref[c, pl.ds(q, tm // f1, stride=f1), :]
                        src = tmp_ref.at[c]
                    for r in range(d):
                        r_lo, r_hi = r % f1, r // f1
                        c1 = sec * gw + c * LANES
                        out[r, :, c1:c1 + LANES] = src[pl.ds(r_lo * (tm // f1) + r_hi, tm // d, stride=d // f1),
                                                       :].astype(BF16)


def _in_proj(xf, g_mix, shift1, scale1, w_in_b, S, pool_width):
    N, D = xf.shape
    tm = IN_TILE
    spt = S // tm
    vec = lambda i: (i // spt, 0, 0)
    row = lambda i: (i, 0)
    gw3 = 3 * ATT_GROUP_WIDTH
    B = N // S
    res_spec = lambda d: pl.BlockSpec((d, tm // d, gw3), lambda i: (i // spt, i % spt, 0))
    res_shape = lambda d: jax.ShapeDtypeStruct((B * d, S // d, gw3), BF16)
    outs = pl.pallas_call(
        _in_kernel,
        grid=(N // tm,),
        in_specs=[pl.BlockSpec((tm, D), row),
                  pl.BlockSpec((1, D), lambda i: (0, 0)),
                  pl.BlockSpec((1, 1, D), vec),
                  pl.BlockSpec((1, 1, D), vec),
                  pl.BlockSpec(w_in_b.shape, lambda i: (0, 0))],
        out_specs=[pl.BlockSpec((tm, pool_width), row), pl.BlockSpec((tm, gw3), row)]
                  + [res_spec(d) for d in ATT_DILATIONS[1:]],
        out_shape=[jax.ShapeDtypeStruct((N, pool_width), F32), jax.ShapeDtypeStruct((N, gw3), BF16)]
                  + [res_shape(d) for d in ATT_DILATIONS[1:]],
        scratch_shapes=[pltpu.VMEM((ATT_GROUP_WIDTH // LANES, tm, LANES), F32)] * 2,
        name="in_proj",
    )(xf, g_mix, shift1, scale1, w_in_b)
    return outs[0], [o.reshape(N, gw3) for o in outs[1:]]


def _attn_kernel(nbs, a_ref, halo_ref, o_ref, lse_ref, kv_ref, band_ref):
    i = pl.program_id(0)
    R = a_ref.shape[0] // ATT_BLOCK
    gw = ATT_GROUP_WIDTH
    blk = ATT_BLOCK
    nh = ATT_HEADS_PER_GROUP
    kv_ref[0:blk, :] = halo_ref[:, gw:3 * gw]
    kv_ref[blk:, :] = a_ref[:, gw:3 * gw]
    row = lax.broadcasted_iota(I32, (nh * blk, 2 * blk), 0) % blk
    col = lax.broadcasted_iota(I32, (nh * blk, 2 * blk), 1)
    in_band = (col >= row) & (col <= row + blk)
    band_ref[0] = jnp.where(in_band, 0.0, NEG_INF)
    band_ref[1] = jnp.where(in_band & (col >= blk), 0.0, NEG_INF)
    head_of_lane = lax.broadcasted_iota(I32, (blk, gw), 1) // ATT_HEAD_DIM
    nt = (((1,), (1,)), ((), ()))

    def one_block(jj, start):
        r0 = pl.multiple_of(jj * blk, blk)
        qf = a_ref[pl.ds(r0, blk), 0:gw].astype(F32) * (ATT_HEAD_DIM ** -0.5)
        q4 = jnp.concatenate([jnp.where(head_of_lane == h, qf, 0.0) for h in range(nh)], axis=0).astype(BF16)
        kc = kv_ref[pl.ds(r0, 2 * blk), 0:gw]
        vc = kv_ref[pl.ds(r0, 2 * blk), gw:2 * gw]
        s = lax.dot_general(q4, kc, nt, preferred_element_type=F32) + band_ref[1 if start is True else 0]
        if start is not None and start is not True:
            s = jnp.where(col >= jnp.where(start, blk, 0), s, NEG_INF)
        m = jnp.max(s, axis=1, keepdims=True)
        p = jnp.exp(s - m)
        l = jnp.sum(p, axis=1, keepdims=True)
        o4 = jnp.dot(p.astype(BF16), vc, preferred_element_type=F32) / l
        lse4 = m + jnp.log(l)
        o = jnp.zeros((blk, gw), F32)
        lse = jnp.zeros((blk, gw), F32)
        for h in range(nh):
            hm = head_of_lane == h
            o = jnp.where(hm, o4[h * blk:(h + 1) * blk, :], o)
            lse = jnp.where(hm, lse4[h * blk:(h + 1) * blk, :], lse)
        o_ref[pl.ds(r0, blk), :] = o
        lse_ref[pl.ds(r0, blk), :] = lse

    U = ATT_UNROLL
    assert U % nbs == 0 or nbs % U == 0

    def body(it, carry):
        for j in range(U):
            if U % nbs == 0:
                start = True if j % nbs == 0 else None
            else:
                start = (((i * R + it * U) % nbs) == 0) if j == 0 else None
            one_block(it * U + j, start)
        return carry

    lax.fori_loop(0, R // U, body, 0)


def _attention(a, nbs):
    N = a.shape[0]
    R = ATT_BLOCKS_PER_STEP
    gw = ATT_GROUP_WIDTH
    tm = R * ATT_BLOCK
    return pl.pallas_call(
        functools.partial(_attn_kernel, nbs),
        grid=(N // tm,),
        in_specs=[pl.BlockSpec((tm, 3 * gw), lambda i: (i, 0)),
                  pl.BlockSpec((ATT_BLOCK, 3 * gw), lambda i: (jnp.maximum(i * R - 1, 0), 0))],
        out_specs=[pl.BlockSpec((tm, gw), lambda i: (i, 0))] * 2,
        out_shape=[jax.ShapeDtypeStruct((N, gw), F32)] * 2,
        scratch_shapes=[pltpu.VMEM((tm + ATT_BLOCK, 2 * gw), BF16),
                        pltpu.VMEM((2, ATT_HEADS_PER_GROUP * ATT_BLOCK, 2 * ATT_BLOCK), F32)],
        name="attn",
    )(a, a)


def _mid_kernel(spt, u_ref, uh_ref, o0_ref, l0_ref, o1_ref, l1_ref, o2_ref, l2_ref, x_ref,
                pbd_ref, psc_ref, wout_ref, gate1_ref, gffn_ref, sh2_ref, sc2_ref, gate2_ref,
                wsg_ref, wsu_ref, wsd_ref, wr_ref, bias_ref, before_ref,
                xacc_ref, h2_ref, e_ref, g_ref, r_ref, cnt_ref, til_ref, base_ref):
    i = pl.program_id(0)
    tm, pw = u_ref.shape
    si = i % spt
    u = u_ref[...]
    keep = jnp.full((POOL_HALO, pw), si, I32) > 0
    ext = jnp.concatenate([jnp.where(keep, uh_ref[...], 0.0), u], axis=0)
    lane_grp = lax.broadcasted_iota(I32, (tm, pw), 1) // (pw // len(POOL_WINDOWS))
    pooled = jnp.zeros((tm, pw), F32)
    s, w = ext, 1
    while w < POOL_HALO:
        s = s + pltpu.roll(s, w, axis=0)
        w *= 2
        if w in POOL_WINDOWS:
            pooled = jnp.where(lane_grp == POOL_WINDOWS.index(w), s[POOL_HALO:, :], pooled)
    win = jnp.zeros((tm, pw), I32)
    for g, w in enumerate(POOL_WINDOWS):
        win = jnp.where(lane_grp == g, w, win)
    pos = si * tm + lax.broadcasted_iota(I32, (tm, pw), 0)
    cnt = jnp.minimum(pos + 1, win).astype(F32)
    pooled = pooled / cnt - u
    pool_out = jnp.dot(pooled.astype(BF16), pbd_ref[...], preferred_element_type=F32) * psc_ref[...]
    def token_order(slot, ref):
        d, n, w = ref.shape
        for r in range(d):
            for c in range(w // LANES):
                til_ref[slot, c, pl.ds(r, n, stride=d), :] = ref[r, :, c * LANES:(c + 1) * LANES]
        return jnp.concatenate([til_ref[slot, c] for c in range(w // LANES)], axis=1)

    l0 = l0_ref[...]
    l1 = token_order(0, l1_ref)
    l2 = token_order(1, l2_ref)
    m = jnp.maximum(jnp.maximum(l0, l1), l2)
    w0 = jnp.exp(l0 - m)
    w1 = jnp.exp(l1 - m)
    w2 = jnp.exp(l2 - m)
    attn = (w0 * o0_ref[...] + w1 * token_order(2, o1_ref) + w2 * token_order(3, o2_ref)) / (w0 + w1 + w2)
    mixed = (jnp.dot(pool_out.astype(BF16), wout_ref[0:pw, :], preferred_element_type=F32)
             + jnp.dot(attn.astype(BF16), wout_ref[pw:, :], preferred_element_type=F32))
    x1 = x_ref[...] + gate1_ref[0] * mixed
    h2 = _rms(x1) * gffn_ref[...]
    h2 = h2 * (1.0 + sc2_ref[0]) + sh2_ref[0]
    h2_ref[...] = _pack_bf16_pairs(h2)
    hb = h2.astype(BF16)
    a = jnp.dot(hb, wsg_ref[...], preferred_element_type=F32)
    b = jnp.dot(hb, wsu_ref[...], preferred_element_type=F32)
    act = (a * _sigmoid(a)) * b
    shared = jnp.dot(act.astype(BF16), wsd_ref[...], preferred_element_type=F32)
    xacc_ref[...] = x1 + gate2_ref[0] * shared
    @pl.when(i == 0)
    def _():
        base_ref[...] = jnp.zeros_like(base_ref)

    C = e_ref.shape[2]
    for t0 in range(0, tm, ROUTE_TILE):
        idxs, gates, ranks = _route_tile(hb[t0:t0 + ROUTE_TILE, :], wr_ref, bias_ref, before_ref, base_ref)
        for k in range(TOP_K):
            g_ref[k:k + 1, t0:t0 + ROUTE_TILE] = gates[k]
            for c in range(ROUTE_TILE // C):
                e_ref[t0 // C + c, k:k + 1, :] = idxs[k][:, c * C:(c + 1) * C]
                r_ref[t0 // C + c, k:k + 1, :] = ranks[k][:, c * C:(c + 1) * C]
    cnt_ref[...] = base_ref[...].astype(I32)


def _mid(u, attn_outs, xf, pool_bd, pool_scale, w_out_b, gate1, g_ffn, shift2, scale2, gate2,
         wsg_b, wsu_b, wsd_b, wr_t, bias_col, S):
    N, D = xf.shape
    E = wr_t.shape[0]
    tok = jnp.arange(ROUTE_TILE, dtype=I32)
    before = (tok[:, None] < tok[None, :]).astype(BF16)
    pw = u.shape[1]
    tm = MID_TILE
    spt = S // tm
    row = lambda i: (i, 0)
    vec = lambda i: (i // spt, 0, 0)
    full = lambda a: pl.BlockSpec(a.shape, lambda i: (0,) * a.ndim)
    hpt = tm // POOL_HALO
    in_specs = [pl.BlockSpec((tm, pw), row),
                pl.BlockSpec((POOL_HALO, pw), lambda i: (jnp.maximum(i * hpt - 1, 0), 0))]
    gw = ATT_GROUP_WIDTH
    in_specs += [pl.BlockSpec((tm, gw), row)] * 2
    for d in ATT_DILATIONS[1:]:
        in_specs += [pl.BlockSpec((d, tm // d, gw), lambda i: (i // spt, i % spt, 0))] * 2
    in_specs += [pl.BlockSpec((tm, D), row), full(pool_bd), full(pool_scale), full(w_out_b),
                 pl.BlockSpec((1, 1, D), vec), full(g_ffn), pl.BlockSpec((1, 1, D), vec),
                 pl.BlockSpec((1, 1, D), vec), pl.BlockSpec((1, 1, D), vec),
                 full(wsg_b), full(wsu_b), full(wsd_b), full(wr_t), full(bias_col), full(before)]
    col = lambda i: (0, i)
    C = SC_CHUNK
    chunked = pl.BlockSpec((tm // C, TOP_K, C), lambda i: (i, 0, 0))
    return pl.pallas_call(
        functools.partial(_mid_kernel, spt),
        grid=(N // tm,),
        in_specs=in_specs,
        out_specs=[pl.BlockSpec((tm, D), row), pl.BlockSpec((tm, D // 2), row),
                   chunked, pl.BlockSpec((TOP_K, tm), col), chunked,
                   pl.BlockSpec((E, 1), lambda i: (0, 0))],
        out_shape=[jax.ShapeDtypeStruct((N, D), F32), jax.ShapeDtypeStruct((N, D // 2), U32),
                   jax.ShapeDtypeStruct((N // C, TOP_K, C), I32), jax.ShapeDtypeStruct((TOP_K, N), F32),
                   jax.ShapeDtypeStruct((N // C, TOP_K, C), I32), jax.ShapeDtypeStruct((E, 1), I32)],
        scratch_shapes=[pltpu.VMEM((4, gw // LANES, tm, LANES), F32), pltpu.VMEM((E, 1), F32)],
        compiler_params=pltpu.CompilerParams(dimension_semantics=("arbitrary",)),
        name="mid",
    )(u, u, *attn_outs, xf, pool_bd, pool_scale, w_out_b, gate1, g_ffn, shift2, scale2, gate2,
      wsg_b, wsu_b, wsd_b, wr_t, bias_col, before)


def _route_tile(hb, wr_ref, bias_ref, before_ref, base_ref):
    T = hb.shape[0]
    E = wr_ref.shape[0]
    gsz = E // N_EXPERT_GROUPS
    logits = lax.dot_general(wr_ref[...], hb, (((1,), (1,)), ((), ())), preferred_element_type=F32)
    scores = _sigmoid(logits)
    biased = scores + bias_ref[...]
    giota = lax.broadcasted_iota(I32, (gsz, T), 0)
    gscore = []
    for g in range(N_EXPERT_GROUPS):
        blk = biased[g * gsz:(g + 1) * gsz, :]
        m1 = jnp.max(blk, axis=0, keepdims=True)
        i1 = jnp.min(jnp.where(blk == m1, giota, gsz), axis=0, keepdims=True)
        m2 = jnp.max(jnp.where(giota == i1, NEG_INF, blk), axis=0, keepdims=True)
        gscore.append(m1 + m2)
    parts = []
    for g in range(N_EXPERT_GROUPS):
        beaten = jnp.zeros((1, T), I32)
        for o in range(N_EXPERT_GROUPS):
            if o == g:
                continue
            wins = (gscore[o] >= gscore[g]) if o < g else (gscore[o] > gscore[g])
            beaten = beaten + wins.astype(I32)
        keep = jnp.broadcast_to(beaten, (gsz, T)) < TOPK_GROUPS
        parts.append(jnp.where(keep, biased[g * gsz:(g + 1) * gsz, :], NEG_INF))
    cur = jnp.concatenate(parts, axis=0)
    eiota = lax.broadcasted_iota(I32, (E, T), 0)
    live = cur > NEG_INF
    idxs, gates = [], []
    for k in range(TOP_K):
        m = jnp.max(cur, axis=0, keepdims=True)
        idx = jnp.min(jnp.where(cur == m, eiota, E), axis=0, keepdims=True)
        oh = eiota == idx
        gates.append(jnp.sum(jnp.where(oh, scores, 0.0), axis=0, keepdims=True))
        idxs.append(idx)
        cur = jnp.where(oh, NEG_INF, cur)
    selm = jnp.where(live & (cur == NEG_INF), 1.0, 0.0)
    gsum = gates[0]
    for k in range(1, TOP_K):
        gsum = gsum + gates[k]
    gates = [gk / gsum * ROUTED_SCALE for gk in gates]
    tot = jnp.dot(selm.astype(BF16), before_ref[...], preferred_element_type=F32) + base_ref[...]
    ranks = [jnp.sum(jnp.where(eiota == idxs[k], tot, 0.0), axis=0, keepdims=True).astype(I32)
             for k in range(TOP_K)]
    base_ref[...] = base_ref[...] + jnp.sum(selm, axis=1, keepdims=True)
    return idxs, gates, ranks


def _sc_dispatch(eidx_c, rank_c, offs, h2p, P):
    N, W = h2p.shape
    nch, K, C = eidx_c.shape
    E = offs.shape[0]
    info = plsc.get_sparse_core_info()
    nw = info.num_cores * info.num_subcores
    L = info.num_lanes
    assert nch % nw == 0, "token chunks must split evenly over the vector subcores"
    per_w = nch // nw
    mesh = plsc.VectorSubcoreMesh(core_axis_name="c", subcore_axis_name="s")

    @functools.partial(
        pl.kernel, mesh=mesh,
        out_type=[jax.ShapeDtypeStruct((P, W), h2p.dtype), jax.ShapeDtypeStruct((nch, K, C), I32)],
        scratch_types=[pltpu.VMEM((E,), I32), pltpu.VMEM((K, C), I32), pltpu.VMEM((K, C), I32),
                       pltpu.VMEM((K, C), I32), pltpu.VMEM((C, W), h2p.dtype), pltpu.SemaphoreType.DMA],
        compiler_params=pltpu.CompilerParams(needs_layout_passes=False),
        name="sc_dispatch",
    )
    def k(e_hbm, r_hbm, off_hbm, h_hbm, xs_hbm, dest_hbm, off_v, e_v, r_v, idx_v, rows_v, sem):
        wid = lax.axis_index("s") * info.num_cores + lax.axis_index("c")
        pltpu.sync_copy(off_hbm, off_v)

        @pl.loop(0, per_w)
        def _(j):
            ch = wid * per_w + j
            pltpu.sync_copy(e_hbm.at[ch], e_v)
            pltpu.sync_copy(r_hbm.at[ch], r_v)
            for kk in range(K):
                for q in range(C // L):
                    sl = pl.ds(q * L, L)
                    idx_v[kk, sl] = plsc.load_gather(off_v, [e_v[kk, sl]]) + r_v[kk, sl]
            pltpu.sync_copy(idx_v, dest_hbm.at[ch])
            pltpu.sync_copy(h_hbm.at[pl.ds(ch * C, C)], rows_v)
            copies = [pltpu.async_copy(rows_v, xs_hbm.at[idx_v.at[kk]], sem) for kk in range(K)]
            for cp in copies:
                cp.wait()

    return k(eidx_c, rank_c, offs, h2p)


def _gmm_kernel(bstart_ref, nbe_ref, cnt_ref, nu_ref, wg_ref, wu_ref, wd_ref, xs_hbm, ys_hbm,
                wgb, wub, wdb, xbuf, ybuf, xsem, ysem):
    step = pl.program_id(0)
    last = pl.num_programs(0) - 1
    epg = wg_ref.shape[0]
    ring, bm = xbuf.shape[0], xbuf.shape[1]
    nblk = ys_hbm.shape[0] // bm
    nused = nu_ref[0]

    def x_copy(b, slot):
        return pltpu.make_async_copy(xs_hbm.at[pl.ds(pl.multiple_of(b * bm, bm), bm), :], xbuf.at[slot],
                                     xsem.at[slot])

    def y_copy(b, slot):
        return pltpu.make_async_copy(ybuf.at[slot], ys_hbm.at[pl.ds(pl.multiple_of(b * bm, bm), bm), :],
                                     ysem.at[slot])

    @pl.when(step == 0)
    def _():
        for j in range(ring - 1):
            @pl.when(j < nused)
            def _():
                x_copy(j, j).start()

    def run_expert(ee):
        e = step * epg + ee
        b0 = bstart_ref[e]
        nb = nbe_ref[e]

        @pl.when(nb > 0)
        def _():
            def prefetch(t):
                @pl.when(t < nused)
                def _():
                    x_copy(t, jnp.bitwise_and(t, ring - 1)).start()

            def process(b, n, fresh):
                slots = [jnp.bitwise_and(b + j, ring - 1) for j in range(n)]
                for j in range(n):
                    x_copy(b + j, slots[j]).wait()
                prefetch(b + ring - 1)
                for j in range(n):
                    @pl.when(b + j >= ring)
                    def _():
                        y_copy(b + j - ring, slots[j]).wait()
                rows = lax.broadcasted_iota(I32, (n * bm, 1), 0)
                valid = cnt_ref[e] - (b - b0) * bm
                xp = jnp.concatenate([xbuf[s] for s in slots], axis=0)
                xb = jnp.concatenate(_unpack_bf16_pairs(jnp.where(rows < valid, xp, jnp.uint32(0))), axis=1)
                if fresh:
                    wg, wu, wd = (w[ee].astype(BF16) for w in (wg_ref, wu_ref, wd_ref))
                    wgb[...], wub[...], wdb[...] = wg, wu, wd
                else:
                    wg, wu, wd = wgb[...], wub[...], wdb[...]
                a = jnp.dot(xb, wg, preferred_element_type=F32)
                g = jnp.dot(xb, wu, preferred_element_type=F32)
                act = (a * _sigmoid(a)) * g
                yp = _pack_bf16_pairs(jnp.dot(act.astype(BF16), wd, preferred_element_type=F32))
                for j in range(n):
                    ybuf[slots[j]] = yp[j * bm:(j + 1) * bm, :]
                    y_copy(b + j, slots[j]).start()
                for j in range(1, n):
                    prefetch(b + ring - 1 + j)

            quads = lax.shift_right_logical(nb, 2)
            lax.fori_loop(0, quads, lambda j, c: (process(b0 + 4 * j, 4, True), c)[1], 0)
            rest2 = jnp.bitwise_and(nb, 2)
            rest1 = jnp.bitwise_and(nb, 1)
            for n, rest, first, have_copy in ((2, rest2, b0 + 4 * quads, quads > 0),
                                              (1, rest1, b0 + 4 * quads + rest2, nb > 1)):
                @pl.when((rest != 0) & have_copy)
                def _():
                    process(first, n, False)

                @pl.when((rest != 0) & jnp.logical_not(have_copy))
                def _():
                    process(first, n, True)

    for ee in range(epg):
        run_expert(ee)

    @pl.when(step == last)
    def _():
        for back in range(ring, 0, -1):
            @pl.when(nused >= back)
            def _():
                y_copy(nused - back, jnp.bitwise_and(nused - back, ring - 1)).wait()
        ybuf[0] = jnp.zeros(ybuf.shape[1:], ybuf.dtype)
        lax.fori_loop(nused, nblk, lambda b, c: (y_copy(b, 0).start(), c)[1], 0)
        lax.fori_loop(nused, nblk, lambda b, c: (y_copy(b, 0).wait(), c)[1], 0)


def _gmm(bstart, nb_e, counts, nused, xs, w_gate, w_up, w_down):
    P, W = xs.shape
    E, D, F = w_gate.shape
    bm = GMM_BLOCK
    epg = GMM_EXPERTS_PER_STEP
    wsel = lambda s, *_: (s, 0, 0)
    grid_spec = pltpu.PrefetchScalarGridSpec(
        num_scalar_prefetch=4,
        grid=(E // epg,),
        in_specs=[pl.BlockSpec((epg, D, F), wsel), pl.BlockSpec((epg, D, F), wsel), pl.BlockSpec((epg, F, D), wsel),
                  pl.BlockSpec(memory_space=pl.ANY)],
        out_specs=pl.BlockSpec(memory_space=pl.ANY),
        scratch_shapes=[pltpu.VMEM((D, F), BF16), pltpu.VMEM((D, F), BF16), pltpu.VMEM((F, D), BF16),
                        pltpu.VMEM((GMM_RING, bm, W), xs.dtype), pltpu.VMEM((GMM_RING, bm, W), xs.dtype),
                        pltpu.SemaphoreType.DMA((GMM_RING,)), pltpu.SemaphoreType.DMA((GMM_RING,))],
    )
    return pl.pallas_call(
        _gmm_kernel,
        grid_spec=grid_spec,
        out_shape=jax.ShapeDtypeStruct((P, W), xs.dtype),
        compiler_params=pltpu.CompilerParams(dimension_semantics=("arbitrary",)),
        name="gmm",
    )(bstart, nb_e, counts, nused, w_gate, w_up, w_down, xs)


def _sc_gather(dest_c, ys):
    nch, K, C = dest_c.shape
    W = ys.shape[1]
    H = SC_GATHER_ROWS
    info = plsc.get_sparse_core_info()
    nw = info.num_cores * info.num_subcores
    assert nch % nw == 0, "token chunks must split evenly over the vector subcores"
    per_w = nch // nw
    nbuf = SC_GATHER_BUFS
    ahead = nbuf - 1
    mesh = plsc.VectorSubcoreMesh(core_axis_name="c", subcore_axis_name="s")
    items = [(kk, hh) for kk in range(K) for hh in range(C // H)]

    @functools.partial(
        pl.kernel, mesh=mesh,
        out_type=jax.ShapeDtypeStruct((K, nch * C, W), ys.dtype),
        scratch_types=([pltpu.VMEM((K, C), I32)] + [pltpu.VMEM((H, W), ys.dtype)] * nbuf
                       + [pltpu.SemaphoreType.DMA] * (2 * nbuf)),
        name="sc_gather",
    )
    def k(dest_hbm, ys_hbm, yk_hbm, idx_v, *rest):
        bufs, gsem, wsem = rest[:nbuf], rest[nbuf:2 * nbuf], rest[2 * nbuf:]
        wid = lax.axis_index("s") * info.num_cores + lax.axis_index("c")

        @pl.loop(0, per_w)
        def _(j):
            ch = wid * per_w + j
            pltpu.sync_copy(dest_hbm.at[ch], idx_v)

            def gather(i):
                kk, hh = items[i]
                return pltpu.async_copy(ys_hbm.at[idx_v.at[kk, pl.ds(hh * H, H)]], bufs[i % nbuf], gsem[i % nbuf])

            def write(i):
                kk, hh = items[i]
                return pltpu.async_copy(bufs[i % nbuf], yk_hbm.at[kk, pl.ds(ch * C + hh * H, H)], wsem[i % nbuf])

            n = len(items)
            g = {i: gather(i) for i in range(ahead)}
            w = {}
            for i in range(n):
                g[i].wait()
                w[i] = write(i)
                if i + ahead < n:
                    if i >= 1:
                        w.pop(i - 1).wait()
                    g[i + ahead] = gather(i + ahead)
            for i in sorted(w):
                w[i].wait()

    return k(dest_c, ys)


def _combine_kernel(yk_ref, g_ref, gate2_ref, xacc_ref, gfin_ref, o_ref, gpad_ref):
    @pl.when(pl.program_id(0) == 0)
    def _():
        gpad_ref[...] = jnp.zeros_like(gpad_ref)

    gpad_ref[0:TOP_K, :] = g_ref[...]
    gt = gpad_ref[...].T
    hi_mask = jnp.uint32(0xFFFF0000)
    acc_lo = acc_hi = None
    for k in range(TOP_K):
        p = yk_ref[k]
        g = gt[:, k:k + 1]
        lo = lax.bitcast_convert_type(p << 16, F32) * g
        hi = lax.bitcast_convert_type(p & hi_mask, F32) * g
        acc_lo = lo if k == 0 else acc_lo + lo
        acc_hi = hi if k == 0 else acc_hi + hi
    routed = jnp.concatenate([acc_lo, acc_hi], axis=1)
    x2 = xacc_ref[...] + gate2_ref[0] * routed
    o_ref[...] = _rms(x2) * gfin_ref[...]


def _combine(yk, gates, gate2, xacc, g_final, S):
    N, D = xacc.shape
    W = yk.shape[2]
    T = COMBINE_TILE
    spt = S // T
    return pl.pallas_call(
        _combine_kernel,
        grid=(N // T,),
        in_specs=[pl.BlockSpec((TOP_K, T, W), lambda i: (0, i, 0)),
                  pl.BlockSpec((TOP_K, T), lambda i: (0, i)),
                  pl.BlockSpec((1, 1, D), lambda i: (i // spt, 0, 0)),
                  pl.BlockSpec((T, D), lambda i: (i, 0)),
                  pl.BlockSpec((1, D), lambda i: (0, 0))],
        out_specs=pl.BlockSpec((T, D), lambda i: (i, 0)),
        out_shape=jax.ShapeDtypeStruct((N, D), F32),
        scratch_shapes=[pltpu.VMEM((LANES, T), F32)],
        compiler_params=pltpu.CompilerParams(dimension_semantics=("arbitrary",)),
        name="combine",
    )(yk, gates, gate2, xacc, g_final)


def _layer(xf, B, S, mod, g_mix, w_in, pool_w, pool_scale, w_out, g_ffn, w_router, router_bias,
           w_gate, w_up, w_down, ws_gate, ws_up, ws_down):
    N, D = xf.shape
    E = w_router.shape[1]
    pw = pool_scale.shape[0]
    shift1, scale1, gate1, shift2, scale2, gate2 = [m.reshape(B, 1, D) for m in jnp.split(mod, 6, axis=-1)]
    u, qkv = _in_proj(xf, g_mix.reshape(1, D), shift1, scale1, w_in.astype(BF16), S, pw)
    attn_outs = []
    for a, d in zip(qkv, ATT_DILATIONS):
        o, lse = _attention(a, S // d // ATT_BLOCK)
        shape = (N, ATT_GROUP_WIDTH) if d == 1 else (B * d, S // d, ATT_GROUP_WIDTH)
        attn_outs += [o.reshape(shape), lse.reshape(shape)]
    ng = pool_w.shape[0]
    pool_bd = jnp.einsum('gcd,gh->gchd', pool_w, jnp.eye(ng, dtype=pool_w.dtype)).reshape(pw, pw).astype(BF16)
    xacc, h2, eidx, gates, rank, counts = _mid(
        u, attn_outs, xf, pool_bd, pool_scale.reshape(1, pw), w_out.astype(BF16), gate1,
        g_ffn.reshape(1, D), shift2, scale2, gate2,
        ws_gate.astype(BF16), ws_up.astype(BF16), ws_down.astype(BF16),
        w_router.T.astype(BF16), router_bias.reshape(E, 1).astype(F32), S)
    bm = GMM_BLOCK
    nblk = N * TOP_K // bm + E
    nb_e = (counts[:, 0] + bm - 1) // bm
    bend = jnp.cumsum(nb_e)
    bstart = (bend - nb_e).astype(I32)
    nused = bend[-1:].astype(I32)
    xs, dest_c = _sc_dispatch(eidx, rank, bstart * bm, h2, nblk * bm)
    ys = _gmm(bstart, nb_e.astype(I32), counts[:, 0], nused, xs, w_gate, w_up, w_down)
    return _sc_gather(dest_c, ys), gates, gate2, xacc


def kernel(x, c, w_ada, b_ada, g_mix, w_in, pool_w, pool_scale, w_out, g_ffn, w_router, router_bias,
           w_gate, w_up, w_down, ws_gate, ws_up, ws_down, g_final):
    B, S, D = x.shape
    depth = w_ada.shape[0]
    assert depth == 1, "the final residual is fused with the final norm, so exactly one layer is supported"
    assert S % (ATT_DILATIONS[-1] * ATT_BLOCK) == 0
    assert all(S % t == 0 for t in (IN_TILE, MID_TILE, COMBINE_TILE, ATT_BLOCKS_PER_STEP * ATT_BLOCK))
    assert MID_TILE % ROUTE_TILE == 0 and ROUTE_TILE % SC_CHUNK == 0 and SC_CHUNK % SC_GATHER_ROWS == 0
    assert w_gate.shape[1] % GMM_EXPERTS_PER_STEP == 0 and max(POOL_WINDOWS) <= POOL_HALO
    xf = x.reshape(B * S, D)
    mod = _ada(c, w_ada[0], b_ada[0])
    yk, gates, gate2, xacc = _layer(
        xf, B, S, mod, g_mix[0], w_in[0], pool_w[0], pool_scale[0], w_out[0], g_ffn[0], w_router[0],
        router_bias[0], w_gate[0], w_up[0], w_down[0], ws_gate[0], ws_up[0], ws_down[0])
    out = _combine(yk, gates, gate2, xacc, g_final.reshape(1, D), S)
    return out.reshape(B, S, D)
```

```python
import functools

import jax
import jax.numpy as jnp
from jax import lax
from jax.experimental import pallas as pl
from jax.experimental.pallas import tpu as pltpu
from jax.experimental.pallas import tpu_sc as plsc

F32 = jnp.float32
BF16 = jnp.bfloat16
I32 = jnp.int32
U32 = jnp.uint32

LANES = 128
SINGLE_LOAD_STRIDE = 4
NORM_EPS = 1e-6
POOL_WINDOWS = (2, 4, 8, 16)
POOL_HALO = 16
ATT_DILATIONS = (1, 4, 16)
ATT_BLOCK = 128
ATT_HEADS_PER_GROUP = 4
ATT_HEAD_DIM = 64
ATT_GROUP_WIDTH = ATT_HEADS_PER_GROUP * ATT_HEAD_DIM
N_EXPERT_GROUPS = 8
TOPK_GROUPS = 4
TOP_K = 8
ROUTED_SCALE = 2.5

IN_TILE = 1024
ATT_BLOCKS_PER_STEP = 16
ATT_UNROLL = 8
MID_TILE = 512
ROUTE_TILE = 512
COMBINE_TILE = 512
GMM_BLOCK = 128
GMM_RING = 16
GMM_EXPERTS_PER_STEP = 1
SC_CHUNK = 128
SC_GATHER_ROWS = 32
SC_ROUND_CHUNK_WORDS = 16384
SC_GATHER_BUFS = 6

NEG_INF = float("-inf")


def _sigmoid(v):
    return 1.0 / (1.0 + jnp.exp(-v))


def _rms(v):
    return v * lax.rsqrt(jnp.mean(v * v, axis=-1, keepdims=True) + NORM_EPS)


def _pack_bf16_pairs(v):
    n = v.shape[1] // 2
    lo = lax.bitcast_convert_type(v[:, :n].astype(BF16).astype(F32), U32)
    hi = lax.bitcast_convert_type(v[:, n:].astype(BF16).astype(F32), U32)
    return (hi & jnp.uint32(0xFFFF0000)) | (lo >> 16)


def _unpack_bf16_pairs(p):
    lo = lax.bitcast_convert_type(p << 16, F32).astype(BF16)
    hi = lax.bitcast_convert_type(p & jnp.uint32(0xFFFF0000), F32).astype(BF16)
    return lo, hi


def _ada_kernel(c_ref, w_ref, b_ref, o_ref):
    c = c_ref[...]
    cs = c * _sigmoid(c)
    o_ref[...] = jnp.dot(cs, w_ref[...], preferred_element_type=F32,
                         precision=lax.Precision.HIGHEST) + b_ref[...]


def _ada(c, w_ada, b_ada):
    B, D = c.shape
    W = w_ada.shape[1]
    tn = 1024
    return pl.pallas_call(
        _ada_kernel,
        grid=(W // tn,),
        in_specs=[pl.BlockSpec((B, D), lambda j: (0, 0)),
                  pl.BlockSpec((D, tn), lambda j: (0, j)),
                  pl.BlockSpec((1, tn), lambda j: (0, j))],
        out_specs=pl.BlockSpec((B, tn), lambda j: (0, j)),
        out_shape=jax.ShapeDtypeStruct((B, W), F32),
        name="ada",
    )(c, w_ada, b_ada.reshape(1, W))


def _in_kernel(x_ref, g_ref, sh_ref, sc_ref, w_ref, pool_ref, q0_ref, q1_ref, q2_ref, scr_ref, tmp_ref):
    h = _rms(x_ref[...]) * g_ref[...]
    h = h * (1.0 + sc_ref[0]) + sh_ref[0]
    hb = h.astype(BF16)
    tm = x_ref.shape[0]
    pw = pool_ref.shape[1]
    gw = ATT_GROUP_WIDTH
    pool_ref[...] = jnp.dot(hb, w_ref[:, 0:pw], preferred_element_type=F32)
    for g, (out, d) in enumerate(zip((q0_ref, q1_ref, q2_ref), ATT_DILATIONS)):
        for sec in range(3):
            c0 = pw + sec * 3 * gw + g * gw
            res = jnp.dot(hb, w_ref[:, c0:c0 + gw], preferred_element_type=F32)
            if d == 1:
                out[:, sec * gw:(sec + 1) * gw] = res.astype(BF16)
            else:
                for c in range(gw // LANES):
                    scr_ref[c] = res[:, c * LANES:(c + 1) * LANES]
                    src, f1 = scr_ref.at[c], 1
                    if d > SINGLE_LOAD_STRIDE:
                        f1 = SINGLE_LOAD_STRIDE
                        for q in range(f1):
                            tmp_ref[c, q * (tm // f1):(q + 1) * (tm // f1), :] = scr_ref[c, pl.ds(q, tm // f1, stride=f1), :]
                        src = tmp_ref.at[c]
                    for r in range(d):
                        r_lo, r_hi = r % f1, r // f1
                        c1 = sec * gw + c * LANES
                        out[r, :, c1:c1 + LANES] = src[pl.ds(r_lo * (tm // f1) + r_hi, tm // d, stride=d // f1),
                                                       :].astype(BF16)


def _in_proj(xf, g_mix, shift1, scale1, w_in_b, S, pool_width):
    N, D = xf.shape
    tm = IN_TILE
    spt = S // tm
    vec = lambda i: (i // spt, 0, 0)
    row = lambda i: (i, 0)
    gw3 = 3 * ATT_GROUP_WIDTH
    B = N // S
    res_spec = lambda d: pl.BlockSpec((d, tm // d, gw3), lambda i: (i // spt, i % spt, 0))
    res_shape = lambda d: jax.ShapeDtypeStruct((B * d, S // d, gw3), BF16)
    outs = pl.pallas_call(
        _in_kernel,
        grid=(N // tm,),
        in_specs=[pl.BlockSpec((tm, D), row),
                  pl.BlockSpec((1, D), lambda i: (0, 0)),
                  pl.BlockSpec((1, 1, D), vec),
                  pl.BlockSpec((1, 1, D), vec),
                  pl.BlockSpec(w_in_b.shape, lambda i: (0, 0))],
        out_specs=[pl.BlockSpec((tm, pool_width), row), pl.BlockSpec((tm, gw3), row)]
                  + [res_spec(d) for d in ATT_DILATIONS[1:]],
        out_shape=[jax.ShapeDtypeStruct((N, pool_width), F32), jax.ShapeDtypeStruct((N, gw3), BF16)]
                  + [res_shape(d) for d in ATT_DILATIONS[1:]],
        scratch_shapes=[pltpu.VMEM((ATT_GROUP_WIDTH // LANES, tm, LANES), F32)] * 2,
        name="in_proj",
    )(xf, g_mix, shift1, scale1, w_in_b)
    return outs[0], [o.reshape(N, gw3) for o in outs[1:]]


def _attn_kernel(nbs, a_ref, halo_ref, o_ref, lse_ref, kv_ref, band_ref):
    i = pl.program_id(0)
    R = a_ref.shape[0] // ATT_BLOCK
    gw = ATT_GROUP_WIDTH
    blk = ATT_BLOCK
    nh = ATT_HEADS_PER_GROUP
    kv_ref[0:blk, :] = halo_ref[:, gw:3 * gw]
    kv_ref[blk:, :] = a_ref[:, gw:3 * gw]
    row = lax.broadcasted_iota(I32, (nh * blk, 2 * blk), 0) % blk
    col = lax.broadcasted_iota(I32, (nh * blk, 2 * blk), 1)
    in_band = (col >= row) & (col <= row + blk)
    band_ref[0] = jnp.where(in_band, 0.0, NEG_INF)
    band_ref[1] = jnp.where(in_band & (col >= blk), 0.0, NEG_INF)
    head_of_lane = lax.broadcasted_iota(I32, (blk, gw), 1) // ATT_HEAD_DIM
    nt = (((1,), (1,)), ((), ()))

    def one_block(jj, start):
        r0 = pl.multiple_of(jj * blk, blk)
        qf = a_ref[pl.ds(r0, blk), 0:gw].astype(F32) * (ATT_HEAD_DIM ** -0.5)
        q4 = jnp.concatenate([jnp.where(head_of_lane == h, qf, 0.0) for h in range(nh)], axis=0).astype(BF16)
        kc = kv_ref[pl.ds(r0, 2 * blk), 0:gw]
        vc = kv_ref[pl.ds(r0, 2 * blk), gw:2 * gw]
        s = lax.dot_general(q4, kc, nt, preferred_element_type=F32) + band_ref[1 if start is True else 0]
        if start is not None and start is not True:
            s = jnp.where(col >= jnp.where(start, blk, 0), s, NEG_INF)
        m = jnp.max(s, axis=1, keepdims=True)
        p = jnp.exp(s - m)
        l = jnp.sum(p, axis=1, keepdims=True)
        o4 = jnp.dot(p.astype(BF16), vc, preferred_element_type=F32) / l
        lse4 = m + jnp.log(l)
        o = jnp.zeros((blk, gw), F32)
        lse = jnp.zeros((blk, gw), F32)
        for h in range(nh):
            hm = head_of_lane == h
            o = jnp.where(hm, o4[h * blk:(h + 1) * blk, :], o)
            lse = jnp.where(hm, lse4[h * blk:(h + 1) * blk, :], lse)
        o_ref[pl.ds(r0, blk), :] = o
        lse_ref[pl.ds(r0, blk), :] = lse

    U = ATT_UNROLL
    assert U % nbs == 0 or nbs % U == 0

    def body(it, carry):
        for j in range(U):
            if U % nbs == 0:
                start = True if j % nbs == 0 else None
            else:
                start = (((i * R + it * U) % nbs) == 0) if j == 0 else None
            one_block(it * U + j, start)
        return carry

    lax.fori_loop(0, R // U, body, 0)


def _attention(a, nbs):
    N = a.shape[0]
    R = ATT_BLOCKS_PER_STEP
    gw = ATT_GROUP_WIDTH
    tm = R * ATT_BLOCK
    return pl.pallas_call(
        functools.partial(_attn_kernel, nbs),
        grid=(N // tm,),
        in_specs=[pl.BlockSpec((tm, 3 * gw), lambda i: (i, 0)),
                  pl.BlockSpec((ATT_BLOCK, 3 * gw), lambda i: (jnp.maximum(i * R - 1, 0), 0))],
        out_specs=[pl.BlockSpec((tm, gw), lambda i: (i, 0))] * 2,
        out_shape=[jax.ShapeDtypeStruct((N, gw), F32)] * 2,
        scratch_shapes=[pltpu.VMEM((tm + ATT_BLOCK, 2 * gw), BF16),
                        pltpu.VMEM((2, ATT_HEADS_PER_GROUP * ATT_BLOCK, 2 * ATT_BLOCK), F32)],
        name="attn",
    )(a, a)


def _mid_kernel(spt, u_ref, uh_ref, o0_ref, l0_ref, o1_ref, l1_ref, o2_ref, l2_ref, x_ref,
                pbd_ref, psc_ref, wout_ref, gate1_ref, gffn_ref, sh2_ref, sc2_ref, gate2_ref,
                wsg_ref, wsu_ref, wsd_ref, wr_ref, bias_ref, before_ref,
                xacc_ref, h2_ref, e_ref, g_ref, r_ref, cnt_ref, til_ref, base_ref):
    i = pl.program_id(0)
    tm, pw = u_ref.shape
    si = i % spt
    u = u_ref[...]
    keep = jnp.full((POOL_HALO, pw), si, I32) > 0
    ext = jnp.concatenate([jnp.where(keep, uh_ref[...], 0.0), u], axis=0)
    lane_grp = lax.broadcasted_iota(I32, (tm, pw), 1) // (pw // len(POOL_WINDOWS))
    pooled = jnp.zeros((tm, pw), F32)
    s, w = ext, 1
    while w < POOL_HALO:
        s = s + pltpu.roll(s, w, axis=0)
        w *= 2
        if w in POOL_WINDOWS:
            pooled = jnp.where(lane_grp == POOL_WINDOWS.index(w), s[POOL_HALO:, :], pooled)
    win = jnp.zeros((tm, pw), I32)
    for g, w in enumerate(POOL_WINDOWS):
        win = jnp.where(lane_grp == g, w, win)
    pos = si * tm + lax.broadcasted_iota(I32, (tm, pw), 0)
    cnt = jnp.minimum(pos + 1, win).astype(F32)
    pooled = pooled / cnt - u
    pool_out = jnp.dot(pooled.astype(BF16), pbd_ref[...], preferred_element_type=F32) * psc_ref[...]
    def token_order(slot, ref):
        d, n, w = ref.shape
        for r in range(d):
            for c in range(w // LANES):
                til_ref[slot, c, pl.ds(r, n, stride=d), :] = ref[r, :, c * LANES:(c + 1) * LANES]
        return jnp.concatenate([til_ref[slot, c] for c in range(w // LANES)], axis=1)

    l0 = l0_ref[...]
    l1 = token_order(0, l1_ref)
    l2 = token_order(1, l2_ref)
    m = jnp.maximum(jnp.maximum(l0, l1), l2)
    w0 = jnp.exp(l0 - m)
    w1 = jnp.exp(l1 - m)
    w2 = jnp.exp(l2 - m)
    attn = (w0 * o0_ref[...] + w1 * token_order(2, o1_ref) + w2 * token_order(3, o2_ref)) / (w0 + w1 + w2)
    mixed = (jnp.dot(pool_out.astype(BF16), wout_ref[0:pw, :], preferred_element_type=F32)
             + jnp.dot(attn.astype(BF16), wout_ref[pw:, :], preferred_element_type=F32))
    x1 = x_ref[...] + gate1_ref[0] * mixed
    h2 = _rms(x1) * gffn_ref[...]
    h2 = h2 * (1.0 + sc2_ref[0]) + sh2_ref[0]
    h2_ref[...] = _pack_bf16_pairs(h2)
    hb = h2.astype(BF16)
    a = jnp.dot(hb, wsg_ref[...], preferred_element_type=F32)
    b = jnp.dot(hb, wsu_ref[...], preferred_element_type=F32)
    act = (a * _sigmoid(a)) * b
    shared = jnp.dot(act.astype(BF16), wsd_ref[...], preferred_element_type=F32)
    xacc_ref[...] = x1 + gate2_ref[0] * shared
    @pl.when(i == 0)
    def _():
        base_ref[...] = jnp.zeros_like(base_ref)

    C = e_ref.shape[2]
    for t0 in range(0, tm, ROUTE_TILE):
        idxs, gates, ranks = _route_tile(hb[t0:t0 + ROUTE_TILE, :], wr_ref, bias_ref, before_ref, base_ref)
        for k in range(TOP_K):
            g_ref[k:k + 1, t0:t0 + ROUTE_TILE] = gates[k]
            for c in range(ROUTE_TILE // C):
                e_ref[t0 // C + c, k:k + 1, :] = idxs[k][:, c * C:(c + 1) * C]
                r_ref[t0 // C + c, k:k + 1, :] = ranks[k][:, c * C:(c + 1) * C]
    cnt_ref[...] = base_ref[...].astype(I32)


def _mid(u, attn_outs, xf, pool_bd, pool_scale, w_out_b, gate1, g_ffn, shift2, scale2, gate2,
         wsg_b, wsu_b, wsd_b, wr_t, bias_col, S):
    N, D = xf.shape
    E = wr_t.shape[0]
    tok = jnp.arange(ROUTE_TILE, dtype=I32)
    before = (tok[:, None] < tok[None, :]).astype(BF16)
    pw = u.shape[1]
    tm = MID_TILE
    spt = S // tm
    row = lambda i: (i, 0)
    vec = lambda i: (i // spt, 0, 0)
    full = lambda a: pl.BlockSpec(a.shape, lambda i: (0,) * a.ndim)
    hpt = tm // POOL_HALO
    in_specs = [pl.BlockSpec((tm, pw), row),
                pl.BlockSpec((POOL_HALO, pw), lambda i: (jnp.maximum(i * hpt - 1, 0), 0))]
    gw = ATT_GROUP_WIDTH
    in_specs += [pl.BlockSpec((tm, gw), row)] * 2
    for d in ATT_DILATIONS[1:]:
        in_specs += [pl.BlockSpec((d, tm // d, gw), lambda i: (i // spt, i % spt, 0))] * 2
    in_specs += [pl.BlockSpec((tm, D), row), full(pool_bd), full(pool_scale), full(w_out_b),
                 pl.BlockSpec((1, 1, D), vec), full(g_ffn), pl.BlockSpec((1, 1, D), vec),
                 pl.BlockSpec((1, 1, D), vec), pl.BlockSpec((1, 1, D), vec),
                 full(wsg_b), full(wsu_b), full(wsd_b), full(wr_t), full(bias_col), full(before)]
    col = lambda i: (0, i)
    C = SC_CHUNK
    chunked = pl.BlockSpec((tm // C, TOP_K, C), lambda i: (i, 0, 0))
    return pl.pallas_call(
        functools.partial(_mid_kernel, spt),
        grid=(N // tm,),
        in_specs=in_specs,
        out_specs=[pl.BlockSpec((tm, D), row), pl.BlockSpec((tm, D // 2), row),
                   chunked, pl.BlockSpec((TOP_K, tm), col), chunked,
                   pl.BlockSpec((E, 1), lambda i: (0, 0))],
        out_shape=[jax.ShapeDtypeStruct((N, D), F32), jax.ShapeDtypeStruct((N, D // 2), U32),
                   jax.ShapeDtypeStruct((N // C, TOP_K, C), I32), jax.ShapeDtypeStruct((TOP_K, N), F32),
                   jax.ShapeDtypeStruct((N // C, TOP_K, C), I32), jax.ShapeDtypeStruct((E, 1), I32)],
        scratch_shapes=[pltpu.VMEM((4, gw // LANES, tm, LANES), F32), pltpu.VMEM((E, 1), F32)],
        compiler_params=pltpu.CompilerParams(dimension_semantics=("arbitrary",)),
        name="mid",
    )(u, u, *attn_outs, xf, pool_bd, pool_scale, w_out_b, gate1, g_ffn, shift2, scale2, gate2,
      wsg_b, wsu_b, wsd_b, wr_t, bias_col, before)


def _route_tile(hb, wr_ref, bias_ref, before_ref, base_ref):
    T = hb.shape[0]
    E = wr_ref.shape[0]
    gsz = E // N_EXPERT_GROUPS
    logits = lax.dot_general(wr_ref[...], hb, (((1,), (1,)), ((), ())), preferred_element_type=F32)
    scores = _sigmoid(logits)
    biased = scores + bias_ref[...]
    giota = lax.broadcasted_iota(I32, (gsz, T), 0)
    gscore = []
    for g in range(N_EXPERT_GROUPS):
        blk = biased[g * gsz:(g + 1) * gsz, :]
        m1 = jnp.max(blk, axis=0, keepdims=True)
        i1 = jnp.min(jnp.where(blk == m1, giota, gsz), axis=0, keepdims=True)
        m2 = jnp.max(jnp.where(giota == i1, NEG_INF, blk), axis=0, keepdims=True)
        gscore.append(m1 + m2)
    parts = []
    for g in range(N_EXPERT_GROUPS):
        beaten = jnp.zeros((1, T), I32)
        for o in range(N_EXPERT_GROUPS):
            if o == g:
                continue
            wins = (gscore[o] >= gscore[g]) if o < g else (gscore[o] > gscore[g])
            beaten = beaten + wins.astype(I32)
        keep = jnp.broadcast_to(beaten, (gsz, T)) < TOPK_GROUPS
        parts.append(jnp.where(keep, biased[g * gsz:(g + 1) * gsz, :], NEG_INF))
    cur = jnp.concatenate(parts, axis=0)
    eiota = lax.broadcasted_iota(I32, (E, T), 0)
    live = cur > NEG_INF
    idxs, gates = [], []
    for k in range(TOP_K):
        m = jnp.max(cur, axis=0, keepdims=True)
        idx = jnp.min(jnp.where(cur == m, eiota, E), axis=0, keepdims=True)
        oh = eiota == idx
        gates.append(jnp.sum(jnp.where(oh, scores, 0.0), axis=0, keepdims=True))
        idxs.append(idx)
        cur = jnp.where(oh, NEG_INF, cur)
    selm = jnp.where(live & (cur == NEG_INF), 1.0, 0.0)
    gsum = gates[0]
    for k in range(1, TOP_K):
        gsum = gsum + gates[k]
    gates = [gk / gsum * ROUTED_SCALE for gk in gates]
    tot = jnp.dot(selm.astype(BF16), before_ref[...], preferred_element_type=F32) + base_ref[...]
    ranks = [jnp.sum(jnp.where(eiota == idxs[k], tot, 0.0), axis=0, keepdims=True).astype(I32)
             for k in range(TOP_K)]
    base_ref[...] = base_ref[...] + jnp.sum(selm, axis=1, keepdims=True)
    return idxs, gates, ranks


def _sc_dispatch(eidx_c, rank_c, offs, h2p, P):
    N, W = h2p.shape
    nch, K, C = eidx_c.shape
    E = offs.shape[0]
    info = plsc.get_sparse_core_info()
    nw = info.num_cores * info.num_subcores
    L = info.num_lanes
    assert nch % nw == 0, "token chunks must split evenly over the vector subcores"
    per_w = nch // nw
    mesh = plsc.VectorSubcoreMesh(core_axis_name="c", subcore_axis_name="s")

    @functools.partial(
        pl.kernel, mesh=mesh,
        out_type=[jax.ShapeDtypeStruct((P, W), h2p.dtype), jax.ShapeDtypeStruct((nch, K, C), I32)],
        scratch_types=[pltpu.VMEM((E,), I32), pltpu.VMEM((K, C), I32), pltpu.VMEM((K, C), I32),
                       pltpu.VMEM((K, C), I32), pltpu.VMEM((C, W), h2p.dtype), pltpu.SemaphoreType.DMA],
        compiler_params=pltpu.CompilerParams(needs_layout_passes=False),
        name="sc_dispatch",
    )
    def k(e_hbm, r_hbm, off_hbm, h_hbm, xs_hbm, dest_hbm, off_v, e_v, r_v, idx_v, rows_v, sem):
        wid = lax.axis_index("s") * info.num_cores + lax.axis_index("c")
        pltpu.sync_copy(off_hbm, off_v)

        @pl.loop(0, per_w)
        def _(j):
            ch = wid * per_w + j
            pltpu.sync_copy(e_hbm.at[ch], e_v)
            pltpu.sync_copy(r_hbm.at[ch], r_v)
            for kk in range(K):
                for q in range(C // L):
                    sl = pl.ds(q * L, L)
                    idx_v[kk, sl] = plsc.load_gather(off_v, [e_v[kk, sl]]) + r_v[kk, sl]
            pltpu.sync_copy(idx_v, dest_hbm.at[ch])
            pltpu.sync_copy(h_hbm.at[pl.ds(ch * C, C)], rows_v)
            copies = [pltpu.async_copy(rows_v, xs_hbm.at[idx_v.at[kk]], sem) for kk in range(K)]
            for cp in copies:
                cp.wait()

    return k(eidx_c, rank_c, offs, h2p)


def _sc_round_weights(w):
    R, Wd = w.shape
    half = Wd // 2
    info = plsc.get_sparse_core_info()
    nw = info.num_cores * info.num_subcores
    L = info.num_lanes
    ch = SC_ROUND_CHUNK_WORDS // Wd
    assert R % (ch * nw * 2) == 0 and half % L == 0
    per_w = R // ch // nw
    mesh = plsc.VectorSubcoreMesh(core_axis_name="c", subcore_axis_name="s")

    @functools.partial(
        pl.kernel, mesh=mesh,
        out_type=jax.ShapeDtypeStruct((R, half), U32),
        scratch_types=([pltpu.VMEM((ch, Wd), U32)] * 2 + [pltpu.VMEM((ch, half), U32)] * 2
                       + [pltpu.SemaphoreType.DMA] * 4),
        name="sc_round_weights",
    )
    def k(w_hbm, o_hbm, in0, in1, out0, out1, is0, is1, os0, os1):
        inb, outb, isem, osem = (in0, in1), (out0, out1), (is0, is1), (os0, os1)
        wid = lax.axis_index("s") * info.num_cores + lax.axis_index("c")

        def in_copy(j, s):
            return pltpu.make_async_copy(w_hbm.at[pl.ds((wid * per_w + j) * ch, ch)], inb[s], isem[s])

        def out_copy(j, s):
            return pltpu.make_async_copy(outb[s], o_hbm.at[pl.ds((wid * per_w + j) * ch, ch)], osem[s])

        def rne16(x):
            return (x + jnp.uint32(0x7FFF) + ((x >> 16) & jnp.uint32(1))) >> 16

        in_copy(0, 0).start()

        @pl.loop(0, per_w, step=2)
        def _(jj):
            for s in range(2):
                j = jj + s
                in_copy(j, s).wait()

                @pl.when(j + 1 < per_w)
                def _():
                    in_copy(j + 1, 1 - s).start()

                @pl.when(j >= 2)
                def _():
                    out_copy(j - 2, s).wait()

                @plsc.parallel_loop(0, ch, unroll=2)
                def _(r):
                    for c in range(half // L):
                        a = inb[s][r, pl.ds(c * L, L)]
                        b = inb[s][r, pl.ds(half + c * L, L)]
                        outb[s][r, pl.ds(c * L, L)] = (rne16(b) << 16) | rne16(a)

                out_copy(j, s).start()

        out_copy(per_w - 2, 0).wait()
        out_copy(per_w - 1, 1).wait()

    return k(lax.bitcast_convert_type(w, U32))


def _gmm_kernel(bstart_ref, nbe_ref, cnt_ref, nu_ref, wg_ref, wu_ref, wd_ref, xs_hbm, ys_hbm,
                wgb, wub, wdb, xbuf, ybuf, xsem, ysem):
    step = pl.program_id(0)
    last = pl.num_programs(0) - 1
    epg = wg_ref.shape[0]
    ring, bm = xbuf.shape[0], xbuf.shape[1]
    nblk = ys_hbm.shape[0] // bm
    nused = nu_ref[0]

    def x_copy(b, slot):
        return pltpu.make_async_copy(xs_hbm.at[pl.ds(pl.multiple_of(b * bm, bm), bm), :], xbuf.at[slot],
                                     xsem.at[slot])

    def y_copy(b, slot):
        return pltpu.make_async_copy(ybuf.at[slot], ys_hbm.at[pl.ds(pl.multiple_of(b * bm, bm), bm), :],
                                     ysem.at[slot])

    @pl.when(step == 0)
    def _():
        for j in range(ring - 1):
            @pl.when(j < nused)
            def _():
                x_copy(j, j).start()

    def run_expert(ee):
        e = step * epg + ee
        b0 = bstart_ref[e]
        nb = nbe_ref[e]

        @pl.when(nb > 0)
        def _():
            def prefetch(t):
                @pl.when(t < nused)
                def _():
                    x_copy(t, jnp.bitwise_and(t, ring - 1)).start()

            def process(b, n, fresh):
                slots = [jnp.bitwise_and(b + j, ring - 1) for j in range(n)]
                for j in range(n):
                    x_copy(b + j, slots[j]).wait()
                prefetch(b + ring - 1)
                for j in range(n):
                    @pl.when(b + j >= ring)
                    def _():
                        y_copy(b + j - ring, slots[j]).wait()
                rows = lax.broadcasted_iota(I32, (n * bm, 1), 0)
                valid = cnt_ref[e] - (b - b0) * bm
                xp = jnp.concatenate([xbuf[s] for s in slots], axis=0)
                xb = jnp.concatenate(_unpack_bf16_pairs(jnp.where(rows < valid, xp, jnp.uint32(0))), axis=1)
                if fresh:
                    wg, wu, wd = (jnp.concatenate(_unpack_bf16_pairs(w[ee]), axis=1) for w in (wg_ref, wu_ref, wd_ref))
                    wgb[...], wub[...], wdb[...] = wg, wu, wd
                else:
                    wg, wu, wd = wgb[...], wub[...], wdb[...]
                a = jnp.dot(xb, wg, preferred_element_type=F32)
                g = jnp.dot(xb, wu, preferred_element_type=F32)
                act = (a * _sigmoid(a)) * g
                yp = _pack_bf16_pairs(jnp.dot(act.astype(BF16), wd, preferred_element_type=F32))
                for j in range(n):
                    ybuf[slots[j]] = yp[j * bm:(j + 1) * bm, :]
                    y_copy(b + j, slots[j]).start()
                for j in range(1, n):
                    prefetch(b + ring - 1 + j)

            quads = lax.shift_right_logical(nb, 2)
            lax.fori_loop(0, quads, lambda j, c: (process(b0 + 4 * j, 4, True), c)[1], 0)
            rest2 = jnp.bitwise_and(nb, 2)
            rest1 = jnp.bitwise_and(nb, 1)
            for n, rest, first, have_copy in ((2, rest2, b0 + 4 * quads, quads > 0),
                                              (1, rest1, b0 + 4 * quads + rest2, nb > 1)):
                @pl.when((rest != 0) & have_copy)
                def _():
                    process(first, n, False)

                @pl.when((rest != 0) & jnp.logical_not(have_copy))
                def _():
                    process(first, n, True)

    for ee in range(epg):
        run_expert(ee)

    @pl.when(step == last)
    def _():
        for back in range(ring, 0, -1):
            @pl.when(nused >= back)
            def _():
                y_copy(nused - back, jnp.bitwise_and(nused - back, ring - 1)).wait()
        ybuf[0] = jnp.zeros(ybuf.shape[1:], ybuf.dtype)
        lax.fori_loop(nused, nblk, lambda b, c: (y_copy(b, 0).start(), c)[1], 0)
        lax.fori_loop(nused, nblk, lambda b, c: (y_copy(b, 0).wait(), c)[1], 0)


def _gmm(bstart, nb_e, counts, nused, xs, w_gate, w_up, w_down):
    P, W = xs.shape
    E, D, F = w_gate.shape[0], w_gate.shape[1], w_down.shape[1]
    bm = GMM_BLOCK
    epg = GMM_EXPERTS_PER_STEP
    wsel = lambda s, *_: (s, 0, 0)
    grid_spec = pltpu.PrefetchScalarGridSpec(
        num_scalar_prefetch=4,
        grid=(E // epg,),
        in_specs=[pl.BlockSpec((epg, D, F // 2), wsel), pl.BlockSpec((epg, D, F // 2), wsel),
                  pl.BlockSpec((epg, F, D // 2), wsel),
                  pl.BlockSpec(memory_space=pl.ANY)],
        out_specs=pl.BlockSpec(memory_space=pl.ANY),
        scratch_shapes=[pltpu.VMEM((D, F), BF16), pltpu.VMEM((D, F), BF16), pltpu.VMEM((F, D), BF16),
                        pltpu.VMEM((GMM_RING, bm, W), xs.dtype), pltpu.VMEM((GMM_RING, bm, W), xs.dtype),
                        pltpu.SemaphoreType.DMA((GMM_RING,)), pltpu.SemaphoreType.DMA((GMM_RING,))],
    )
    return pl.pallas_call(
        _gmm_kernel,
        grid_spec=grid_spec,
        out_shape=jax.ShapeDtypeStruct((P, W), xs.dtype),
        compiler_params=pltpu.CompilerParams(dimension_semantics=("arbitrary",)),
        name="gmm",
    )(bstart, nb_e, counts, nused, w_gate, w_up, w_down, xs)


def _sc_gather(dest_c, ys):
    nch, K, C = dest_c.shape
    W = ys.shape[1]
    H = SC_GATHER_ROWS
    info = plsc.get_sparse_core_info()
    nw = info.num_cores * info.num_subcores
    assert nch % nw == 0, "token chunks must split evenly over the vector subcores"
    per_w = nch // nw
    nbuf = SC_GATHER_BUFS
    ahead = nbuf - 1
    mesh = plsc.VectorSubcoreMesh(core_axis_name="c", subcore_axis_name="s")
    items = [(kk, hh) for kk in range(K) for hh in range(C // H)]

    @functools.partial(
        pl.kernel, mesh=mesh,
        out_type=jax.ShapeDtypeStruct((K, nch * C, W), ys.dtype),
        scratch_types=([pltpu.VMEM((K, C), I32)] + [pltpu.VMEM((H, W), ys.dtype)] * nbuf
                       + [pltpu.SemaphoreType.DMA] * (2 * nbuf)),
        name="sc_gather",
    )
    def k(dest_hbm, ys_hbm, yk_hbm, idx_v, *rest):
        bufs, gsem, wsem = rest[:nbuf], rest[nbuf:2 * nbuf], rest[2 * nbuf:]
        wid = lax.axis_index("s") * info.num_cores + lax.axis_index("c")

        @pl.loop(0, per_w)
        def _(j):
            ch = wid * per_w + j
            pltpu.sync_copy(dest_hbm.at[ch], idx_v)

            def gather(i):
                kk, hh = items[i]
                return pltpu.async_copy(ys_hbm.at[idx_v.at[kk, pl.ds(hh * H, H)]], bufs[i % nbuf], gsem[i % nbuf])

            def write(i):
                kk, hh = items[i]
                return pltpu.async_copy(bufs[i % nbuf], yk_hbm.at[kk, pl.ds(ch * C + hh * H, H)], wsem[i % nbuf])

            n = len(items)
            g = {i: gather(i) for i in range(ahead)}
            w = {}
            for i in range(n):
                g[i].wait()
                w[i] = write(i)
                if i + ahead < n:
                    if i >= 1:
                        w.pop(i - 1).wait()
                    g[i + ahead] = gather(i + ahead)
            for i in sorted(w):
                w[i].wait()

    return k(dest_c, ys)


def _combine_kernel(yk_ref, g_ref, gate2_ref, xacc_ref, gfin_ref, o_ref, gpad_ref):
    @pl.when(pl.program_id(0) == 0)
    def _():
        gpad_ref[...] = jnp.zeros_like(gpad_ref)

    gpad_ref[0:TOP_K, :] = g_ref[...]
    gt = gpad_ref[...].T
    hi_mask = jnp.uint32(0xFFFF0000)
    acc_lo = acc_hi = None
    for k in range(TOP_K):
        p = yk_ref[k]
        g = gt[:, k:k + 1]
        lo = lax.bitcast_convert_type(p << 16, F32) * g
        hi = lax.bitcast_convert_type(p & hi_mask, F32) * g
        acc_lo = lo if k == 0 else acc_lo + lo
        acc_hi = hi if k == 0 else acc_hi + hi
    routed = jnp.concatenate([acc_lo, acc_hi], axis=1)
    x2 = xacc_ref[...] + gate2_ref[0] * routed
    o_ref[...] = _rms(x2) * gfin_ref[...]


def _combine(yk, gates, gate2, xacc, g_final, S):
    N, D = xacc.shape
    W = yk.shape[2]
    T = COMBINE_TILE
    spt = S // T
    return pl.pallas_call(
        _combine_kernel,
        grid=(N // T,),
        in_specs=[pl.BlockSpec((TOP_K, T, W), lambda i: (0, i, 0)),
                  pl.BlockSpec((TOP_K, T), lambda i: (0, i)),
                  pl.BlockSpec((1, 1, D), lambda i: (i // spt, 0, 0)),
                  pl.BlockSpec((T, D), lambda i: (i, 0)),
                  pl.BlockSpec((1, D), lambda i: (0, 0))],
        out_specs=pl.BlockSpec((T, D), lambda i: (i, 0)),
        out_shape=jax.ShapeDtypeStruct((N, D), F32),
        scratch_shapes=[pltpu.VMEM((LANES, T), F32)],
        compiler_params=pltpu.CompilerParams(dimension_semantics=("arbitrary",)),
        name="combine",
    )(yk, gates, gate2, xacc, g_final)


def _layer(xf, B, S, mod, g_mix, w_in, pool_w, pool_scale, w_out, g_ffn, w_router, router_bias,
           w_gate, w_up, w_down, ws_gate, ws_up, ws_down):
    N, D = xf.shape
    E = w_router.shape[1]
    w_experts = [_sc_round_weights(w.reshape(-1, w.shape[-1])).reshape(w.shape[0], w.shape[1], -1)
                 for w in (w_gate, w_up, w_down)]
    pw = pool_scale.shape[0]
    shift1, scale1, gate1, shift2, scale2, gate2 = [m.reshape(B, 1, D) for m in jnp.split(mod, 6, axis=-1)]
    u, qkv = _in_proj(xf, g_mix.reshape(1, D), shift1, scale1, w_in.astype(BF16), S, pw)
    attn_outs = []
    for a, d in zip(qkv, ATT_DILATIONS):
        o, lse = _attention(a, S // d // ATT_BLOCK)
        shape = (N, ATT_GROUP_WIDTH) if d == 1 else (B * d, S // d, ATT_GROUP_WIDTH)
        attn_outs += [o.reshape(shape), lse.reshape(shape)]
    ng = pool_w.shape[0]
    pool_bd = jnp.einsum('gcd,gh->gchd', pool_w, jnp.eye(ng, dtype=pool_w.dtype)).reshape(pw, pw).astype(BF16)
    xacc, h2, eidx, gates, rank, counts = _mid(
        u, attn_outs, xf, pool_bd, pool_scale.reshape(1, pw), w_out.astype(BF16), gate1,
        g_ffn.reshape(1, D), shift2, scale2, gate2,
        ws_gate.astype(BF16), ws_up.astype(BF16), ws_down.astype(BF16),
        w_router.T.astype(BF16), router_bias.reshape(E, 1).astype(F32), S)
    bm = GMM_BLOCK
    nblk = N * TOP_K // bm + E
    nb_e = (counts[:, 0] + bm - 1) // bm
    bend = jnp.cumsum(nb_e)
    bstart = (bend - nb_e).astype(I32)
    nused = bend[-1:].astype(I32)
    xs, dest_c = _sc_dispatch(eidx, rank, bstart * bm, h2, nblk * bm)
    ys = _gmm(bstart, nb_e.astype(I32), counts[:, 0], nused, xs, *w_experts)
    return _sc_gather(dest_c, ys), gates, gate2, xacc


def kernel(x, c, w_ada, b_ada, g_mix, w_in, pool_w, pool_scale, w_out, g_ffn, w_router, router_bias,
           w_gate, w_up, w_down, ws_gate, ws_up, ws_down, g_final):
    B, S, D = x.shape
    depth = w_ada.shape[0]
    assert depth == 1, "the final residual is fused with the final norm, so exactly one layer is supported"
    assert S % (ATT_DILATIONS[-1] * ATT_BLOCK) == 0
    assert all(S % t == 0 for t in (IN_TILE, MID_TILE, COMBINE_TILE, ATT_BLOCKS_PER_STEP * ATT_BLOCK))
    assert MID_TILE % ROUTE_TILE == 0 and ROUTE_TILE % SC_CHUNK == 0 and SC_CHUNK % SC_GATHER_ROWS == 0
    assert w_gate.shape[1] % GMM_EXPERTS_PER_STEP == 0 and max(POOL_WINDOWS) <= POOL_HALO
    xf = x.reshape(B * S, D)
    mod = _ada(c, w_ada[0], b_ada[0])
    yk, gates, gate2, xacc = _layer(
        xf, B, S, mod, g_mix[0], w_in[0], pool_w[0], pool_scale[0], w_out[0], g_ffn[0], w_router[0],
        router_bias[0], w_gate[0], w_up[0], w_down[0], ws_gate[0], ws_up[0], ws_down[0])
    out = _combine(yk, gates, gate2, xacc, g_final.reshape(1, D), S)
    return out.reshape(B, S, D)
```

```python
import functools

import jax
import jax.numpy as jnp
from jax import lax
from jax.experimental import pallas as pl
from jax.experimental.pallas import tpu as pltpu
from jax.experimental.pallas import tpu_sc as plsc

F32 = jnp.float32
BF16 = jnp.bfloat16
I32 = jnp.int32
U32 = jnp.uint32

LANES = 128
SINGLE_LOAD_STRIDE = 4
NORM_EPS = 1e-6
POOL_WINDOWS = (2, 4, 8, 16)
POOL_HALO = 16
ATT_DILATIONS = (1, 4, 16)
ATT_BLOCK = 128
ATT_HEADS_PER_GROUP = 4
ATT_HEAD_DIM = 64
ATT_GROUP_WIDTH = ATT_HEADS_PER_GROUP * ATT_HEAD_DIM
N_EXPERT_GROUPS = 8
TOPK_GROUPS = 4
TOP_K = 8
ROUTED_SCALE = 2.5

IN_TILE = 1024
ATT_BLOCKS_PER_STEP = 16
ATT_UNROLL = 8
MID_TILE = 512
ROUTE_TILE = 512
COMBINE_TILE = 512
GMM_BLOCK = 128
GMM_RING = 16
GMM_EXPERTS_PER_STEP = 1
SC_CHUNK = 128
SC_GATHER_ROWS = 32
SC_GATHER_BUFS = 6

NEG_INF = float("-inf")


def _sigmoid(v):
    return 1.0 / (1.0 + jnp.exp(-v))


def _rms(v):
    return v * lax.rsqrt(jnp.mean(v * v, axis=-1, keepdims=True) + NORM_EPS)


def _pack_bf16_pairs(v):
    n = v.shape[1] // 2
    lo = lax.bitcast_convert_type(v[:, :n].astype(BF16).astype(F32), U32)
    hi = lax.bitcast_convert_type(v[:, n:].astype(BF16).astype(F32), U32)
    return (hi & jnp.uint32(0xFFFF0000)) | (lo >> 16)


def _unpack_bf16_pairs(p):
    lo = lax.bitcast_convert_type(p << 16, F32).astype(BF16)
    hi = lax.bitcast_convert_type(p & jnp.uint32(0xFFFF0000), F32).astype(BF16)
    return lo, hi


def _ada_kernel(c_ref, w_ref, b_ref, o_ref):
    c = c_ref[...]
    cs = c * _sigmoid(c)
    o_ref[...] = jnp.dot(cs, w_ref[...], preferred_element_type=F32,
                         precision=lax.Precision.HIGHEST) + b_ref[...]


def _ada(c, w_ada, b_ada):
    B, D = c.shape
    W = w_ada.shape[1]
    tn = 1024
    return pl.pallas_call(
        _ada_kernel,
        grid=(W // tn,),
        in_specs=[pl.BlockSpec((B, D), lambda j: (0, 0)),
                  pl.BlockSpec((D, tn), lambda j: (0, j)),
                  pl.BlockSpec((1, tn), lambda j: (0, j))],
        out_specs=pl.BlockSpec((B, tn), lambda j: (0, j)),
        out_shape=jax.ShapeDtypeStruct((B, W), F32),
        name="ada",
    )(c, w_ada, b_ada.reshape(1, W))


def _in_kernel(x_ref, g_ref, sh_ref, sc_ref, w_ref, pool_ref, q0_ref, q1_ref, q2_ref, scr_ref, tmp_ref):
    h = _rms(x_ref[...]) * g_ref[...]
    h = h * (1.0 + sc_ref[0]) + sh_ref[0]
    hb = h.astype(BF16)
    tm = x_ref.shape[0]
    pw = pool_ref.shape[1]
    gw = ATT_GROUP_WIDTH
    pool_ref[...] = jnp.dot(hb, w_ref[:, 0:pw], preferred_element_type=F32)
    for g, (out, d) in enumerate(zip((q0_ref, q1_ref, q2_ref), ATT_DILATIONS)):
        for sec in range(3):
            c0 = pw + sec * 3 * gw + g * gw
            res = jnp.dot(hb, w_ref[:, c0:c0 + gw], preferred_element_type=F32)
            if d == 1:
                out[:, sec * gw:(sec + 1) * gw] = res.astype(BF16)
            else:
                for c in range(gw // LANES):
                    scr_ref[c] = res[:, c * LANES:(c + 1) * LANES]
                    src, f1 = scr_ref.at[c], 1
                    if d > SINGLE_LOAD_STRIDE:
                        f1 = SINGLE_LOAD_STRIDE
                        for q in range(f1):
                            tmp_ref[c, q * (tm // f1):(q + 1) * (tm // f1), :] = scr_ref[c, pl.ds(q, tm // f1, stride=f1), :]
                        src = tmp_ref.at[c]
                    for r in range(d):
                        r_lo, r_hi = r % f1, r // f1
                        c1 = sec * gw + c * LANES
                        out[r, :, c1:c1 + LANES] = src[pl.ds(r_lo * (tm // f1) + r_hi, tm // d, stride=d // f1),
                                                       :].astype(BF16)


def _in_proj(xf, g_mix, shift1, scale1, w_in_b, S, pool_width):
    N, D = xf.shape
    tm = IN_TILE
    spt = S // tm
    vec = lambda i: (i // spt, 0, 0)
    row = lambda i: (i, 0)
    gw3 = 3 * ATT_GROUP_WIDTH
    B = N // S
    res_spec = lambda d: pl.BlockSpec((d, tm // d, gw3), lambda i: (i // spt, i % spt, 0))
    res_shape = lambda d: jax.ShapeDtypeStruct((B * d, S // d, gw3), BF16)
    outs = pl.pallas_call(
        _in_kernel,
        grid=(N // tm,),
        in_specs=[pl.BlockSpec((tm, D), row),
                  pl.BlockSpec((1, D), lambda i: (0, 0)),
                  pl.BlockSpec((1, 1, D), vec),
                  pl.BlockSpec((1, 1, D), vec),
                  pl.BlockSpec(w_in_b.shape, lambda i: (0, 0))],
        out_specs=[pl.BlockSpec((tm, pool_width), row), pl.BlockSpec((tm, gw3), row)]
                  + [res_spec(d) for d in ATT_DILATIONS[1:]],
        out_shape=[jax.ShapeDtypeStruct((N, pool_width), F32), jax.ShapeDtypeStruct((N, gw3), BF16)]
                  + [res_shape(d) for d in ATT_DILATIONS[1:]],
        scratch_shapes=[pltpu.VMEM((ATT_GROUP_WIDTH // LANES, tm, LANES), F32)] * 2,
        name="in_proj",
    )(xf, g_mix, shift1, scale1, w_in_b)
    return outs[0], [o.reshape(N, gw3) for o in outs[1:]]


def _attn_kernel(nbs, a_ref, halo_ref, o_ref, lse_ref, kv_ref, band_ref):
    i = pl.program_id(0)
    R = a_ref.shape[0] // ATT_BLOCK
    gw = ATT_GROUP_WIDTH
    blk = ATT_BLOCK
    nh = ATT_HEADS_PER_GROUP
    kv_ref[0:blk, :] = halo_ref[:, gw:3 * gw]
    kv_ref[blk:, :] = a_ref[:, gw:3 * gw]
    row = lax.broadcasted_iota(I32, (nh * blk, 2 * blk), 0) % blk
    col = lax.broadcasted_iota(I32, (nh * blk, 2 * blk), 1)
    in_band = (col >= row) & (col <= row + blk)
    band_ref[0] = jnp.where(in_band, 0.0, NEG_INF)
    band_ref[1] = jnp.where(in_band & (col >= blk), 0.0, NEG_INF)
    head_of_lane = lax.broadcasted_iota(I32, (blk, gw), 1) // ATT_HEAD_DIM
    nt = (((1,), (1,)), ((), ()))

    def one_block(jj, start):
        r0 = pl.multiple_of(jj * blk, blk)
        qf = a_ref[pl.ds(r0, blk), 0:gw].astype(F32) * (ATT_HEAD_DIM ** -0.5)
        q4 = jnp.concatenate([jnp.where(head_of_lane == h, qf, 0.0) for h in range(nh)], axis=0).astype(BF16)
        kc = kv_ref[pl.ds(r0, 2 * blk), 0:gw]
        vc = kv_ref[pl.ds(r0, 2 * blk), gw:2 * gw]
        s = lax.dot_general(q4, kc, nt, preferred_element_type=F32) + band_ref[1 if start is True else 0]
        if start is not None and start is not True:
            s = jnp.where(col >= jnp.where(start, blk, 0), s, NEG_INF)
        m = jnp.max(s, axis=1, keepdims=True)
        p = jnp.exp(s - m)
        l = jnp.sum(p, axis=1, keepdims=True)
        o4 = jnp.dot(p.astype(BF16), vc, preferred_element_type=F32) / l
        lse4 = m + jnp.log(l)
        o = jnp.zeros((blk, gw), F32)
        lse = jnp.zeros((blk, gw), F32)
        for h in range(nh):
            hm = head_of_lane == h
            o = jnp.where(hm, o4[h * blk:(h + 1) * blk, :], o)
            lse = jnp.where(hm, lse4[h * blk:(h + 1) * blk, :], lse)
        o_ref[pl.ds(r0, blk), :] = o
        lse_ref[pl.ds(r0, blk), :] = lse

    U = ATT_UNROLL
    assert U % nbs == 0 or nbs % U == 0

    def body(it, carry):
        for j in range(U):
            if U % nbs == 0:
                start = True if j % nbs == 0 else None
            else:
                start = (((i * R + it * U) % nbs) == 0) if j == 0 else None
            one_block(it * U + j, start)
        return carry

    lax.fori_loop(0, R // U, body, 0)


def _attention(a, nbs):
    N = a.shape[0]
    R = ATT_BLOCKS_PER_STEP
    gw = ATT_GROUP_WIDTH
    tm = R * ATT_BLOCK
    return pl.pallas_call(
        functools.partial(_attn_kernel, nbs),
        grid=(N // tm,),
        in_specs=[pl.BlockSpec((tm, 3 * gw), lambda i: (i, 0)),
                  pl.BlockSpec((ATT_BLOCK, 3 * gw), lambda i: (jnp.maximum(i * R - 1, 0), 0))],
        out_specs=[pl.BlockSpec((tm, gw), lambda i: (i, 0))] * 2,
        out_shape=[jax.ShapeDtypeStruct((N, gw), F32)] * 2,
        scratch_shapes=[pltpu.VMEM((tm + ATT_BLOCK, 2 * gw), BF16),
                        pltpu.VMEM((2, ATT_HEADS_PER_GROUP * ATT_BLOCK, 2 * ATT_BLOCK), F32)],
        name="attn",
    )(a, a)


def _mid_kernel(spt, u_ref, uh_ref, o0_ref, l0_ref, o1_ref, l1_ref, o2_ref, l2_ref, x_ref,
                pbd_ref, psc_ref, wout_ref, gate1_ref, gffn_ref, sh2_ref, sc2_ref, gate2_ref,
                wsg_ref, wsu_ref, wsd_ref, wr_ref, bias_ref, before_ref,
                xacc_ref, h2_ref, e_ref, g_ref, r_ref, cnt_ref, til_ref, tmp_ref, base_ref):
    i = pl.program_id(0)
    tm, pw = u_ref.shape
    si = i % spt
    u = u_ref[...]
    keep = jnp.full((POOL_HALO, pw), si, I32) > 0
    ext = jnp.concatenate([jnp.where(keep, uh_ref[...], 0.0), u], axis=0)
    lane_grp = lax.broadcasted_iota(I32, (tm, pw), 1) // (pw // len(POOL_WINDOWS))
    pooled = jnp.zeros((tm, pw), F32)
    s, w = ext, 1
    while w < POOL_HALO:
        s = s + pltpu.roll(s, w, axis=0)
        w *= 2
        if w in POOL_WINDOWS:
            pooled = jnp.where(lane_grp == POOL_WINDOWS.index(w), s[POOL_HALO:, :], pooled)
    win = jnp.zeros((tm, pw), I32)
    for g, w in enumerate(POOL_WINDOWS):
        win = jnp.where(lane_grp == g, w, win)
    pos = si * tm + lax.broadcasted_iota(I32, (tm, pw), 0)
    cnt = jnp.minimum(pos + 1, win).astype(F32)
    pooled = pooled / cnt - u
    pool_out = jnp.dot(pooled.astype(BF16), pbd_ref[...], preferred_element_type=F32) * psc_ref[...]
    def token_order(slot, ref):
        d, n, w = ref.shape
        f1 = SINGLE_LOAD_STRIDE if d > SINGLE_LOAD_STRIDE else 1
        for c in range(w // LANES):
            cols = slice(c * LANES, (c + 1) * LANES)
            if f1 == 1:
                for r in range(d):
                    til_ref[slot, c, pl.ds(r, n, stride=d), :] = ref[r, :, cols]
            else:
                f2, slab = d // f1, tm // f1
                for r in range(d):
                    r_lo, r_hi = r % f1, r // f1
                    tmp_ref[c, pl.ds(r_lo * slab + r_hi, n, stride=f2), :] = ref[r, :, cols]
                for r_lo in range(f1):
                    til_ref[slot, c, pl.ds(r_lo, slab, stride=f1), :] = tmp_ref[c, r_lo * slab:(r_lo + 1) * slab, :]
        return jnp.concatenate([til_ref[slot, c] for c in range(w // LANES)], axis=1)

    l0 = l0_ref[...]
    l1 = token_order(0, l1_ref)
    l2 = token_order(1, l2_ref)
    m = jnp.maximum(jnp.maximum(l0, l1), l2)
    w0 = jnp.exp(l0 - m)
    w1 = jnp.exp(l1 - m)
    w2 = jnp.exp(l2 - m)
    attn = (w0 * o0_ref[...] + w1 * token_order(2, o1_ref) + w2 * token_order(3, o2_ref)) / (w0 + w1 + w2)
    mixed = (jnp.dot(pool_out.astype(BF16), wout_ref[0:pw, :], preferred_element_type=F32)
             + jnp.dot(attn.astype(BF16), wout_ref[pw:, :], preferred_element_type=F32))
    x1 = x_ref[...] + gate1_ref[0] * mixed
    h2 = _rms(x1) * gffn_ref[...]
    h2 = h2 * (1.0 + sc2_ref[0]) + sh2_ref[0]
    h2_ref[...] = _pack_bf16_pairs(h2)
    hb = h2.astype(BF16)
    a = jnp.dot(hb, wsg_ref[...], preferred_element_type=F32)
    b = jnp.dot(hb, wsu_ref[...], preferred_element_type=F32)
    act = (a * _sigmoid(a)) * b
    shared = jnp.dot(act.astype(BF16), wsd_ref[...], preferred_element_type=F32)
    xacc_ref[...] = x1 + gate2_ref[0] * shared
    @pl.when(i == 0)
    def _():
        base_ref[...] = jnp.zeros_like(base_ref)

    C = e_ref.shape[2]
    for t0 in range(0, tm, ROUTE_TILE):
        idxs, gates, ranks = _route_tile(hb[t0:t0 + ROUTE_TILE, :], wr_ref, bias_ref, before_ref, base_ref)
        for k in range(TOP_K):
            g_ref[k:k + 1, t0:t0 + ROUTE_TILE] = gates[k]
            for c in range(ROUTE_TILE // C):
                e_ref[t0 // C + c, k:k + 1, :] = idxs[k][:, c * C:(c + 1) * C]
                r_ref[t0 // C + c, k:k + 1, :] = ranks[k][:, c * C:(c + 1) * C]
    cnt_ref[...] = base_ref[...].astype(I32)


def _mid(u, attn_outs, xf, pool_bd, pool_scale, w_out_b, gate1, g_ffn, shift2, scale2, gate2,
         wsg_b, wsu_b, wsd_b, wr_t, bias_col, S):
    N, D = xf.shape
    E = wr_t.shape[0]
    tok = jnp.arange(ROUTE_TILE, dtype=I32)
    before = (tok[:, None] < tok[None, :]).astype(BF16)
    pw = u.shape[1]
    tm = MID_TILE
    spt = S // tm
    row = lambda i: (i, 0)
    vec = lambda i: (i // spt, 0, 0)
    full = lambda a: pl.BlockSpec(a.shape, lambda i: (0,) * a.ndim)
    hpt = tm // POOL_HALO
    in_specs = [pl.BlockSpec((tm, pw), row),
                pl.BlockSpec((POOL_HALO, pw), lambda i: (jnp.maximum(i * hpt - 1, 0), 0))]
    gw = ATT_GROUP_WIDTH
    in_specs += [pl.BlockSpec((tm, gw), row)] * 2
    for d in ATT_DILATIONS[1:]:
        in_specs += [pl.BlockSpec((d, tm // d, gw), lambda i: (i // spt, i % spt, 0))] * 2
    in_specs += [pl.BlockSpec((tm, D), row), full(pool_bd), full(pool_scale), full(w_out_b),
                 pl.BlockSpec((1, 1, D), vec), full(g_ffn), pl.BlockSpec((1, 1, D), vec),
                 pl.BlockSpec((1, 1, D), vec), pl.BlockSpec((1, 1, D), vec),
                 full(wsg_b), full(wsu_b), full(wsd_b), full(wr_t), full(bias_col), full(before)]
    col = lambda i: (0, i)
    C = SC_CHUNK
    chunked = pl.BlockSpec((tm // C, TOP_K, C), lambda i: (i, 0, 0))
    return pl.pallas_call(
        functools.partial(_mid_kernel, spt),
        grid=(N // tm,),
        in_specs=in_specs,
        out_specs=[pl.BlockSpec((tm, D), row), pl.BlockSpec((tm, D // 2), row),
                   chunked, pl.BlockSpec((TOP_K, tm), col), chunked,
                   pl.BlockSpec((E, 1), lambda i: (0, 0))],
        out_shape=[jax.ShapeDtypeStruct((N, D), F32), jax.ShapeDtypeStruct((N, D // 2), U32),
                   jax.ShapeDtypeStruct((N // C, TOP_K, C), I32), jax.ShapeDtypeStruct((TOP_K, N), F32),
                   jax.ShapeDtypeStruct((N // C, TOP_K, C), I32), jax.ShapeDtypeStruct((E, 1), I32)],
        scratch_shapes=[pltpu.VMEM((4, gw // LANES, tm, LANES), F32), pltpu.VMEM((gw // LANES, tm, LANES), F32),
                        pltpu.VMEM((E, 1), F32)],
        compiler_params=pltpu.CompilerParams(dimension_semantics=("arbitrary",)),
        name="mid",
    )(u, u, *attn_outs, xf, pool_bd, pool_scale, w_out_b, gate1, g_ffn, shift2, scale2, gate2,
      wsg_b, wsu_b, wsd_b, wr_t, bias_col, before)


def _route_tile(hb, wr_ref, bias_ref, before_ref, base_ref):
    T = hb.shape[0]
    E = wr_ref.shape[0]
    gsz = E // N_EXPERT_GROUPS
    logits = lax.dot_general(wr_ref[...], hb, (((1,), (1,)), ((), ())), preferred_element_type=F32)
    scores = _sigmoid(logits)
    biased = scores + bias_ref[...]
    giota = lax.broadcasted_iota(I32, (gsz, T), 0)
    gscore = []
    for g in range(N_EXPERT_GROUPS):
        blk = biased[g * gsz:(g + 1) * gsz, :]
        m1 = jnp.max(blk, axis=0, keepdims=True)
        i1 = jnp.min(jnp.where(blk == m1, giota, gsz), axis=0, keepdims=True)
        m2 = jnp.max(jnp.where(giota == i1, NEG_INF, blk), axis=0, keepdims=True)
        gscore.append(m1 + m2)
    parts = []
    for g in range(N_EXPERT_GROUPS):
        beaten = jnp.zeros((1, T), I32)
        for o in range(N_EXPERT_GROUPS):
            if o == g:
                continue
            wins = (gscore[o] >= gscore[g]) if o < g else (gscore[o] > gscore[g])
            beaten = beaten + wins.astype(I32)
        keep = jnp.broadcast_to(beaten, (gsz, T)) < TOPK_GROUPS
        parts.append(jnp.where(keep, biased[g * gsz:(g + 1) * gsz, :], NEG_INF))
    cur = jnp.concatenate(parts, axis=0)
    eiota = lax.broadcasted_iota(I32, (E, T), 0)
    live = cur > NEG_INF
    idxs, gates = [], []
    sub = lax.broadcasted_iota(I32, (8, T), 0)
    for k in range(TOP_K):
        nodes = [(cur[8 * j:8 * (j + 1), :], j, scores[8 * j:8 * (j + 1), :]) for j in range(E // 8)]
        while len(nodes) > 1:
            merged = []
            for p in range(0, len(nodes), 2):
                (va, ta, sa), (vb, tb, sb) = nodes[p], nodes[p + 1]
                later = vb > va
                merged.append((jnp.where(later, vb, va), jnp.where(later, tb, ta), jnp.where(later, sb, sa)))
            nodes = merged
        v8, t8, s8 = nodes[0]
        e8 = t8 * 8 + sub
        m = jnp.max(v8, axis=0, keepdims=True)
        idx = jnp.min(jnp.where(v8 == m, e8, E), axis=0, keepdims=True)
        oh = eiota == idx
        gates.append(jnp.sum(jnp.where(e8 == idx, s8, 0.0), axis=0, keepdims=True))
        idxs.append(idx)
        cur = jnp.where(oh, NEG_INF, cur)
    selm = jnp.where(live & (cur == NEG_INF), 1.0, 0.0)
    gsum = gates[0]
    for k in range(1, TOP_K):
        gsum = gsum + gates[k]
    gates = [gk / gsum * ROUTED_SCALE for gk in gates]
    tot = jnp.dot(selm.astype(BF16), before_ref[...], preferred_element_type=F32) + base_ref[...]
    ranks = [jnp.sum(jnp.where(eiota == idxs[k], tot, 0.0), axis=0, keepdims=True).astype(I32)
             for k in range(TOP_K)]
    base_ref[...] = base_ref[...] + jnp.sum(selm, axis=1, keepdims=True)
    return idxs, gates, ranks


def _sc_dispatch(eidx_c, rank_c, offs, h2p, P):
    N, W = h2p.shape
    nch, K, C = eidx_c.shape
    E = offs.shape[0]
    info = plsc.get_sparse_core_info()
    nw = info.num_cores * info.num_subcores
    L = info.num_lanes
    assert nch % nw == 0, "token chunks must split evenly over the vector subcores"
    per_w = nch // nw
    mesh = plsc.VectorSubcoreMesh(core_axis_name="c", subcore_axis_name="s")

    @functools.partial(
        pl.kernel, mesh=mesh,
        out_type=[jax.ShapeDtypeStruct((P, W), h2p.dtype), jax.ShapeDtypeStruct((nch, K, C), I32)],
        scratch_types=[pltpu.VMEM((E,), I32), pltpu.VMEM((K, C), I32), pltpu.VMEM((K, C), I32),
                       pltpu.VMEM((K, C), I32), pltpu.VMEM((C, W), h2p.dtype), pltpu.SemaphoreType.DMA],
        compiler_params=pltpu.CompilerParams(needs_layout_passes=False),
        name="sc_dispatch",
    )
    def k(e_hbm, r_hbm, off_hbm, h_hbm, xs_hbm, dest_hbm, off_v, e_v, r_v, idx_v, rows_v, sem):
        wid = lax.axis_index("s") * info.num_cores + lax.axis_index("c")
        pltpu.sync_copy(off_hbm, off_v)

        @pl.loop(0, per_w)
        def _(j):
            ch = wid * per_w + j
            pltpu.sync_copy(e_hbm.at[ch], e_v)
            pltpu.sync_copy(r_hbm.at[ch], r_v)
            for kk in range(K):
                for q in range(C // L):
                    sl = pl.ds(q * L, L)
                    idx_v[kk, sl] = plsc.load_gather(off_v, [e_v[kk, sl]]) + r_v[kk, sl]
            pltpu.sync_copy(idx_v, dest_hbm.at[ch])
            pltpu.sync_copy(h_hbm.at[pl.ds(ch * C, C)], rows_v)
            copies = [pltpu.async_copy(rows_v, xs_hbm.at[idx_v.at[kk]], sem) for kk in range(K)]
            for cp in copies:
                cp.wait()

    return k(eidx_c, rank_c, offs, h2p)


def _gmm_kernel(bstart_ref, nbe_ref, cnt_ref, nu_ref, wg_ref, wu_ref, wd_ref, xs_hbm, ys_hbm,
                wgb, wub, wdb, xbuf, ybuf, xsem, ysem):
    step = pl.program_id(0)
    last = pl.num_programs(0) - 1
    epg = wg_ref.shape[0]
    ring, bm = xbuf.shape[0], xbuf.shape[1]
    nblk = ys_hbm.shape[0] // bm
    nused = nu_ref[0]

    def x_copy(b, slot):
        return pltpu.make_async_copy(xs_hbm.at[pl.ds(pl.multiple_of(b * bm, bm), bm), :], xbuf.at[slot],
                                     xsem.at[slot])

    def y_copy(b, slot):
        return pltpu.make_async_copy(ybuf.at[slot], ys_hbm.at[pl.ds(pl.multiple_of(b * bm, bm), bm), :],
                                     ysem.at[slot])

    @pl.when(step == 0)
    def _():
        for j in range(ring - 1):
            @pl.when(j < nused)
            def _():
                x_copy(j, j).start()

    def run_expert(ee):
        e = step * epg + ee
        b0 = bstart_ref[e]
        nb = nbe_ref[e]

        @pl.when(nb > 0)
        def _():
            def prefetch(t):
                @pl.when(t < nused)
                def _():
                    x_copy(t, jnp.bitwise_and(t, ring - 1)).start()

            def process(b, n, fresh):
                slots = [jnp.bitwise_and(b + j, ring - 1) for j in range(n)]
                for j in range(n):
                    x_copy(b + j, slots[j]).wait()
                prefetch(b + ring - 1)
                for j in range(n):
                    @pl.when(b + j >= ring)
                    def _():
                        y_copy(b + j - ring, slots[j]).wait()
                rows = lax.broadcasted_iota(I32, (n * bm, 1), 0)
                valid = cnt_ref[e] - (b - b0) * bm
                xp = jnp.concatenate([xbuf[s] for s in slots], axis=0)
                xb = jnp.concatenate(_unpack_bf16_pairs(jnp.where(rows < valid, xp, jnp.uint32(0))), axis=1)
                if fresh:
                    wg, wu, wd = (w[ee].astype(BF16) for w in (wg_ref, wu_ref, wd_ref))
                    wgb[...], wub[...], wdb[...] = wg, wu, wd
                else:
                    wg, wu, wd = wgb[...], wub[...], wdb[...]
                a = jnp.dot(xb, wg, preferred_element_type=F32)
                g = jnp.dot(xb, wu, preferred_element_type=F32)
                act = (a * _sigmoid(a)) * g
                yp = _pack_bf16_pairs(jnp.dot(act.astype(BF16), wd, preferred_element_type=F32))
                for j in range(n):
                    ybuf[slots[j]] = yp[j * bm:(j + 1) * bm, :]
                    y_copy(b + j, slots[j]).start()
                for j in range(1, n):
                    prefetch(b + ring - 1 + j)

            quads = lax.shift_right_logical(nb, 2)
            lax.fori_loop(0, quads, lambda j, c: (process(b0 + 4 * j, 4, True), c)[1], 0)
            rest2 = jnp.bitwise_and(nb, 2)
            rest1 = jnp.bitwise_and(nb, 1)
            for n, rest, first, have_copy in ((2, rest2, b0 + 4 * quads, quads > 0),
                                              (1, rest1, b0 + 4 * quads + rest2, nb > 1)):
                @pl.when((rest != 0) & have_copy)
                def _():
                    process(first, n, False)

                @pl.when((rest != 0) & jnp.logical_not(have_copy))
                def _():
                    process(first, n, True)

    for ee in range(epg):
        run_expert(ee)

    @pl.when(step == last)
    def _():
        for back in range(ring, 0, -1):
            @pl.when(nused >= back)
            def _():
                y_copy(nused - back, jnp.bitwise_and(nused - back, ring - 1)).wait()
        ybuf[0] = jnp.zeros(ybuf.shape[1:], ybuf.dtype)
        lax.fori_loop(nused, nblk, lambda b, c: (y_copy(b, 0).start(), c)[1], 0)
        lax.fori_loop(nused, nblk, lambda b, c: (y_copy(b, 0).wait(), c)[1], 0)


def _gmm(bstart, nb_e, counts, nused, xs, w_gate, w_up, w_down):
    P, W = xs.shape
    E, D, F = w_gate.shape
    bm = GMM_BLOCK
    epg = GMM_EXPERTS_PER_STEP
    wsel = lambda s, *_: (s, 0, 0)
    grid_spec = pltpu.PrefetchScalarGridSpec(
        num_scalar_prefetch=4,
        grid=(E // epg,),
        in_specs=[pl.BlockSpec((epg, D, F), wsel), pl.BlockSpec((epg, D, F), wsel), pl.BlockSpec((epg, F, D), wsel),
                  pl.BlockSpec(memory_space=pl.ANY)],
        out_specs=pl.BlockSpec(memory_space=pl.ANY),
        scratch_shapes=[pltpu.VMEM((D, F), BF16), pltpu.VMEM((D, F), BF16), pltpu.VMEM((F, D), BF16),
                        pltpu.VMEM((GMM_RING, bm, W), xs.dtype), pltpu.VMEM((GMM_RING, bm, W), xs.dtype),
                        pltpu.SemaphoreType.DMA((GMM_RING,)), pltpu.SemaphoreType.DMA((GMM_RING,))],
    )
    return pl.pallas_call(
        _gmm_kernel,
        grid_spec=grid_spec,
        out_shape=jax.ShapeDtypeStruct((P, W), xs.dtype),
        compiler_params=pltpu.CompilerParams(dimension_semantics=("arbitrary",)),
        name="gmm",
    )(bstart, nb_e, counts, nused, w_gate, w_up, w_down, xs)


def _sc_gather(dest_c, ys):
    nch, K, C = dest_c.shape
    W = ys.shape[1]
    H = SC_GATHER_ROWS
    info = plsc.get_sparse_core_info()
    nw = info.num_cores * info.num_subcores
    assert nch % nw == 0, "token chunks must split evenly over the vector subcores"
    per_w = nch // nw
    nbuf = SC_GATHER_BUFS
    ahead = nbuf - 1
    mesh = plsc.VectorSubcoreMesh(core_axis_name="c", subcore_axis_name="s")
    items = [(kk, hh) for kk in range(K) for hh in range(C // H)]

    @functools.partial(
        pl.kernel, mesh=mesh,
        out_type=jax.ShapeDtypeStruct((K, nch * C, W), ys.dtype),
        scratch_types=([pltpu.VMEM((K, C), I32)] + [pltpu.VMEM((H, W), ys.dtype)] * nbuf
                       + [pltpu.SemaphoreType.DMA] * (2 * nbuf)),
        name="sc_gather",
    )
    def k(dest_hbm, ys_hbm, yk_hbm, idx_v, *rest):
        bufs, gsem, wsem = rest[:nbuf], rest[nbuf:2 * nbuf], rest[2 * nbuf:]
        wid = lax.axis_index("s") * info.num_cores + lax.axis_index("c")

        @pl.loop(0, per_w)
        def _(j):
            ch = wid * per_w + j
            pltpu.sync_copy(dest_hbm.at[ch], idx_v)

            def gather(i):
                kk, hh = items[i]
                return pltpu.async_copy(ys_hbm.at[idx_v.at[kk, pl.ds(hh * H, H)]], bufs[i % nbuf], gsem[i % nbuf])

            def write(i):
                kk, hh = items[i]
                return pltpu.async_copy(bufs[i % nbuf], yk_hbm.at[kk, pl.ds(ch * C + hh * H, H)], wsem[i % nbuf])

            n = len(items)
            g = {i: gather(i) for i in range(ahead)}
            w = {}
            for i in range(n):
                g[i].wait()
                w[i] = write(i)
                if i + ahead < n:
                    if i >= 1:
                        w.pop(i - 1).wait()
                    g[i + ahead] = gather(i + ahead)
            for i in sorted(w):
                w[i].wait()

    return k(dest_c, ys)


def _combine_kernel(yk_ref, g_ref, gate2_ref, xacc_ref, gfin_ref, o_ref, gpad_ref):
    @pl.when(pl.program_id(0) == 0)
    def _():
        gpad_ref[...] = jnp.zeros_like(gpad_ref)

    gpad_ref[0:TOP_K, :] = g_ref[...]
    gt = gpad_ref[...].T
    hi_mask = jnp.uint32(0xFFFF0000)
    acc_lo = acc_hi = None
    for k in range(TOP_K):
        p = yk_ref[k]
        g = gt[:, k:k + 1]
        lo = lax.bitcast_convert_type(p << 16, F32) * g
        hi = lax.bitcast_convert_type(p & hi_mask, F32) * g
        acc_lo = lo if k == 0 else acc_lo + lo
        acc_hi = hi if k == 0 else acc_hi + hi
    routed = jnp.concatenate([acc_lo, acc_hi], axis=1)
    x2 = xacc_ref[...] + gate2_ref[0] * routed
    o_ref[...] = _rms(x2) * gfin_ref[...]


def _combine(yk, gates, gate2, xacc, g_final, S):
    N, D = xacc.shape
    W = yk.shape[2]
    T = COMBINE_TILE
    spt = S // T
    return pl.pallas_call(
        _combine_kernel,
        grid=(N // T,),
        in_specs=[pl.BlockSpec((TOP_K, T, W), lambda i: (0, i, 0)),
                  pl.BlockSpec((TOP_K, T), lambda i: (0, i)),
                  pl.BlockSpec((1, 1, D), lambda i: (i // spt, 0, 0)),
                  pl.BlockSpec((T, D), lambda i: (i, 0)),
                  pl.BlockSpec((1, D), lambda i: (0, 0))],
        out_specs=pl.BlockSpec((T, D), lambda i: (i, 0)),
        out_shape=jax.ShapeDtypeStruct((N, D), F32),
        scratch_shapes=[pltpu.VMEM((LANES, T), F32)],
        compiler_params=pltpu.CompilerParams(dimension_semantics=("arbitrary",)),
        name="combine",
    )(yk, gates, gate2, xacc, g_final)


def _layer(xf, B, S, mod, g_mix, w_in, pool_w, pool_scale, w_out, g_ffn, w_router, router_bias,
           w_gate, w_up, w_down, ws_gate, ws_up, ws_down):
    N, D = xf.shape
    E = w_router.shape[1]
    pw = pool_scale.shape[0]
    shift1, scale1, gate1, shift2, scale2, gate2 = [m.reshape(B, 1, D) for m in jnp.split(mod, 6, axis=-1)]
    u, qkv = _in_proj(xf, g_mix.reshape(1, D), shift1, scale1, w_in.astype(BF16), S, pw)
    attn_outs = []
    for a, d in zip(qkv, ATT_DILATIONS):
        o, lse = _attention(a, S // d // ATT_BLOCK)
        shape = (N, ATT_GROUP_WIDTH) if d == 1 else (B * d, S // d, ATT_GROUP_WIDTH)
        attn_outs += [o.reshape(shape), lse.reshape(shape)]
    ng = pool_w.shape[0]
    pool_bd = jnp.einsum('gcd,gh->gchd', pool_w, jnp.eye(ng, dtype=pool_w.dtype)).reshape(pw, pw).astype(BF16)
    xacc, h2, eidx, gates, rank, counts = _mid(
        u, attn_outs, xf, pool_bd, pool_scale.reshape(1, pw), w_out.astype(BF16), gate1,
        g_ffn.reshape(1, D), shift2, scale2, gate2,
        ws_gate.astype(BF16), ws_up.astype(BF16), ws_down.astype(BF16),
        w_router.T.astype(BF16), router_bias.reshape(E, 1).astype(F32), S)
    bm = GMM_BLOCK
    nblk = N * TOP_K // bm + E
    nb_e = (counts[:, 0] + bm - 1) // bm
    bend = jnp.cumsum(nb_e)
    bstart = (bend - nb_e).astype(I32)
    nused = bend[-1:].astype(I32)
    xs, dest_c = _sc_dispatch(eidx, rank, bstart * bm, h2, nblk * bm)
    ys = _gmm(bstart, nb_e.astype(I32), counts[:, 0], nused, xs, w_gate, w_up, w_down)
    return _sc_gather(dest_c, ys), gates, gate2, xacc


def kernel(x, c, w_ada, b_ada, g_mix, w_in, pool_w, pool_scale, w_out, g_ffn, w_router, router_bias,
           w_gate, w_up, w_down, ws_gate, ws_up, ws_down, g_final):
    B, S, D = x.shape
    depth = w_ada.shape[0]
    assert depth == 1, "the final residual is fused with the final norm, so exactly one layer is supported"
    assert S % (ATT_DILATIONS[-1] * ATT_BLOCK) == 0
    assert all(S % t == 0 for t in (IN_TILE, MID_TILE, COMBINE_TILE, ATT_BLOCKS_PER_STEP * ATT_BLOCK))
    assert MID_TILE % ROUTE_TILE == 0 and ROUTE_TILE % SC_CHUNK == 0 and SC_CHUNK % SC_GATHER_ROWS == 0
    assert w_gate.shape[1] % GMM_EXPERTS_PER_STEP == 0 and max(POOL_WINDOWS) <= POOL_HALO
    xf = x.reshape(B * S, D)
    mod = _ada(c, w_ada[0], b_ada[0])
    yk, gates, gate2, xacc = _layer(
        xf, B, S, mod, g_mix[0], w_in[0], pool_w[0], pool_scale[0], w_out[0], g_ffn[0], w_router[0],
        router_bias[0], w_gate[0], w_up[0], w_down[0], ws_gate[0], ws_up[0], ws_down[0])
    out = _combine(yk, gates, gate2, xacc, g_final.reshape(1, D), S)
    return out.reshape(B, S, D)
```

```python
import functools

import jax
import jax.numpy as jnp
from jax import lax
from jax.experimental import pallas as pl
from jax.experimental.pallas import tpu as pltpu
from jax.experimental.pallas import tpu_sc as plsc

F32 = jnp.float32
BF16 = jnp.bfloat16
I32 = jnp.int32
U32 = jnp.uint32

LANES = 128
SINGLE_LOAD_STRIDE = 4
NORM_EPS = 1e-6
POOL_WINDOWS = (2, 4, 8, 16)
POOL_HALO = 16
ATT_DILATIONS = (1, 4, 16)
ATT_BLOCK = 128
ATT_HEADS_PER_GROUP = 4
ATT_HEAD_DIM = 64
ATT_GROUP_WIDTH = ATT_HEADS_PER_GROUP * ATT_HEAD_DIM
N_EXPERT_GROUPS = 8
TOPK_GROUPS = 4
TOP_K = 8
ROUTED_SCALE = 2.5

IN_TILE = 1024
ATT_BLOCKS_PER_STEP = 16
ATT_UNROLL = 8
MID_TILE = 512
ROUTE_TILE = 512
COMBINE_TILE = 512
GMM_BLOCK = 128
GMM_RING = 16
GMM_EXPERTS_PER_STEP = 1
SC_CHUNK = 128
SC_GATHER_ROWS = 32
SC_GATHER_BUFS = 6

NEG_INF = float("-inf")


def _sigmoid(v):
    return 1.0 / (1.0 + jnp.exp(-v))


def _rms(v):
    return v * lax.rsqrt(jnp.mean(v * v, axis=-1, keepdims=True) + NORM_EPS)


def _pack_bf16_pairs(v):
    n = v.shape[1] // 2
    lo = lax.bitcast_convert_type(v[:, :n].astype(BF16).astype(F32), U32)
    hi = lax.bitcast_convert_type(v[:, n:].astype(BF16).astype(F32), U32)
    return (hi & jnp.uint32(0xFFFF0000)) | (lo >> 16)


def _unpack_bf16_pairs(p):
    lo = lax.bitcast_convert_type(p << 16, F32).astype(BF16)
    hi = lax.bitcast_convert_type(p & jnp.uint32(0xFFFF0000), F32).astype(BF16)
    return lo, hi


def _ada_kernel(c_ref, w_ref, b_ref, o_ref):
    c = c_ref[...]
    cs = c * _sigmoid(c)
    o_ref[...] = jnp.dot(cs, w_ref[...], preferred_element_type=F32,
                         precision=lax.Precision.HIGHEST) + b_ref[...]


def _ada(c, w_ada, b_ada):
    B, D = c.shape
    W = w_ada.shape[1]
    tn = 1024
    return pl.pallas_call(
        _ada_kernel,
        grid=(W // tn,),
        in_specs=[pl.BlockSpec((B, D), lambda j: (0, 0)),
                  pl.BlockSpec((D, tn), lambda j: (0, j)),
                  pl.BlockSpec((1, tn), lambda j: (0, j))],
        out_specs=pl.BlockSpec((B, tn), lambda j: (0, j)),
        out_shape=jax.ShapeDtypeStruct((B, W), F32),
        name="ada",
    )(c, w_ada, b_ada.reshape(1, W))


def _in_kernel(x_ref, g_ref, sh_ref, sc_ref, w_ref, pool_ref, q0_ref, q1_ref, q2_ref, scr_ref, tmp_ref):
    h = _rms(x_ref[...]) * g_ref[...]
    h = h * (1.0 + sc_ref[0]) + sh_ref[0]
    hb = h.astype(BF16)
    tm = x_ref.shape[0]
    pw = pool_ref.shape[1]
    gw = ATT_GROUP_WIDTH
    pool_ref[...] = jnp.dot(hb, w_ref[:, 0:pw], preferred_element_type=F32)
    for g, (out, d) in enumerate(zip((q0_ref, q1_ref, q2_ref), ATT_DILATIONS)):
        for sec in range(3):
            c0 = pw + sec * 3 * gw + g * gw
            res = jnp.dot(hb, w_ref[:, c0:c0 + gw], preferred_element_type=F32)
            if d == 1:
                out[:, sec * gw:(sec + 1) * gw] = res.astype(BF16)
            else:
                for c in range(gw // LANES):
                    scr_ref[c] = res[:, c * LANES:(c + 1) * LANES]
                    src, f1 = scr_ref.at[c], 1
                    if d > SINGLE_LOAD_STRIDE:
                        f1 = SINGLE_LOAD_STRIDE
                        for q in range(f1):
                            tmp_ref[c, q * (tm // f1):(q + 1) * (tm // f1), :] = scr_ref[c, pl.ds(q, tm // f1, stride=f1), :]
                        src = tmp_ref.at[c]
                    for r in range(d):
                        r_lo, r_hi = r % f1, r // f1
                        c1 = sec * gw + c * LANES
                        out[r, :, c1:c1 + LANES] = src[pl.ds(r_lo * (tm // f1) + r_hi, tm // d, stride=d // f1),
                                                       :].astype(BF16)


def _in_proj(xf, g_mix, shift1, scale1, w_in_b, S, pool_width):
    N, D = xf.shape
    tm = IN_TILE
    spt = S // tm
    vec = lambda i: (i // spt, 0, 0)
    row = lambda i: (i, 0)
    gw3 = 3 * ATT_GROUP_WIDTH
    B = N // S
    res_spec = lambda d: pl.BlockSpec((d, tm // d, gw3), lambda i: (i // spt, i % spt, 0))
    res_shape = lambda d: jax.ShapeDtypeStruct((B * d, S // d, gw3), BF16)
    outs = pl.pallas_call(
        _in_kernel,
        grid=(N // tm,),
        in_specs=[pl.BlockSpec((tm, D), row),
                  pl.BlockSpec((1, D), lambda i: (0, 0)),
                  pl.BlockSpec((1, 1, D), vec),
                  pl.BlockSpec((1, 1, D), vec),
                  pl.BlockSpec(w_in_b.shape, lambda i: (0, 0))],
        out_specs=[pl.BlockSpec((tm, pool_width), row), pl.BlockSpec((tm, gw3), row)]
                  + [res_spec(d) for d in ATT_DILATIONS[1:]],
        out_shape=[jax.ShapeDtypeStruct((N, pool_width), F32), jax.ShapeDtypeStruct((N, gw3), BF16)]
                  + [res_shape(d) for d in ATT_DILATIONS[1:]],
        scratch_shapes=[pltpu.VMEM((ATT_GROUP_WIDTH // LANES, tm, LANES), F32)] * 2,
        name="in_proj",
    )(xf, g_mix, shift1, scale1, w_in_b)
    return outs[0], [o.reshape(N, gw3) for o in outs[1:]]


def _attn_kernel(nbs, a_ref, halo_ref, o_ref, lse_ref, kv_ref, band_ref):
    i = pl.program_id(0)
    R = a_ref.shape[0] // ATT_BLOCK
    gw = ATT_GROUP_WIDTH
    blk = ATT_BLOCK
    nh = ATT_HEADS_PER_GROUP
    kv_ref[0:blk, :] = halo_ref[:, gw:3 * gw]
    kv_ref[blk:, :] = a_ref[:, gw:3 * gw]
    row = lax.broadcasted_iota(I32, (nh * blk, 2 * blk), 0) % blk
    col = lax.broadcasted_iota(I32, (nh * blk, 2 * blk), 1)
    in_band = (col >= row) & (col <= row + blk)
    band_ref[0] = jnp.where(in_band, 0.0, NEG_INF)
    band_ref[1] = jnp.where(in_band & (col >= blk), 0.0, NEG_INF)
    head_of_lane = lax.broadcasted_iota(I32, (blk, gw), 1) // ATT_HEAD_DIM
    nt = (((1,), (1,)), ((), ()))

    def one_block(jj, start):
        r0 = pl.multiple_of(jj * blk, blk)
        qf = a_ref[pl.ds(r0, blk), 0:gw].astype(F32) * (ATT_HEAD_DIM ** -0.5)
        q4 = jnp.concatenate([jnp.where(head_of_lane == h, qf, 0.0) for h in range(nh)], axis=0).astype(BF16)
        kc = kv_ref[pl.ds(r0, 2 * blk), 0:gw]
        vc = kv_ref[pl.ds(r0, 2 * blk), gw:2 * gw]
        s = lax.dot_general(q4, kc, nt, preferred_element_type=F32) + band_ref[1 if start is True else 0]
        if start is not None and start is not True:
            s = jnp.where(col >= jnp.where(start, blk, 0), s, NEG_INF)
        m = jnp.max(s, axis=1, keepdims=True)
        p = jnp.exp(s - m)
        l = jnp.sum(p, axis=1, keepdims=True)
        o4 = jnp.dot(p.astype(BF16), vc, preferred_element_type=F32) / l
        lse4 = m + jnp.log(l)
        o = jnp.zeros((blk, gw), F32)
        lse = jnp.zeros((blk, gw), F32)
        for h in range(nh):
            hm = head_of_lane == h
            o = jnp.where(hm, o4[h * blk:(h + 1) * blk, :], o)
            lse = jnp.where(hm, lse4[h * blk:(h + 1) * blk, :], lse)
        o_ref[pl.ds(r0, blk), :] = o
        lse_ref[pl.ds(r0, blk), :] = lse

    U = ATT_UNROLL
    assert U % nbs == 0 or nbs % U == 0

    def body(it, carry):
        for j in range(U):
            if U % nbs == 0:
                start = True if j % nbs == 0 else None
            else:
                start = (((i * R + it * U) % nbs) == 0) if j == 0 else None
            one_block(it * U + j, start)
        return carry

    lax.fori_loop(0, R // U, body, 0)


def _attention(a, nbs):
    N = a.shape[0]
    R = ATT_BLOCKS_PER_STEP
    gw = ATT_GROUP_WIDTH
    tm = R * ATT_BLOCK
    return pl.pallas_call(
        functools.partial(_attn_kernel, nbs),
        grid=(N // tm,),
        in_specs=[pl.BlockSpec((tm, 3 * gw), lambda i: (i, 0)),
                  pl.BlockSpec((ATT_BLOCK, 3 * gw), lambda i: (jnp.maximum(i * R - 1, 0), 0))],
        out_specs=[pl.BlockSpec((tm, gw), lambda i: (i, 0))] * 2,
        out_shape=[jax.ShapeDtypeStruct((N, gw), F32)] * 2,
        scratch_shapes=[pltpu.VMEM((tm + ATT_BLOCK, 2 * gw), BF16),
                        pltpu.VMEM((2, ATT_HEADS_PER_GROUP * ATT_BLOCK, 2 * ATT_BLOCK), F32)],
        name="attn",
    )(a, a)


def _mid_kernel(spt, u_ref, uh_ref, o0_ref, l0_ref, o1_ref, l1_ref, o2_ref, l2_ref, x_ref,
                pbd_ref, psc_ref, wout_ref, gate1_ref, gffn_ref, sh2_ref, sc2_ref, gate2_ref,
                wsg_ref, wsu_ref, wsd_ref, wr_ref, bias_ref, before_ref,
                xacc_ref, h2_ref, e_ref, g_ref, r_ref, cnt_ref, til_ref, tmp_ref, base_ref):
    i = pl.program_id(0)
    tm, pw = u_ref.shape
    si = i % spt
    u = u_ref[...]
    keep = jnp.full((POOL_HALO, pw), si, I32) > 0
    ext = jnp.concatenate([jnp.where(keep, uh_ref[...], 0.0), u], axis=0)
    lane_grp = lax.broadcasted_iota(I32, (tm, pw), 1) // (pw // len(POOL_WINDOWS))
    pooled = jnp.zeros((tm, pw), F32)
    s, w = ext, 1
    while w < POOL_HALO:
        s = s + pltpu.roll(s, w, axis=0)
        w *= 2
        if w in POOL_WINDOWS:
            pooled = jnp.where(lane_grp == POOL_WINDOWS.index(w), s[POOL_HALO:, :], pooled)
    win = jnp.zeros((tm, pw), I32)
    for g, w in enumerate(POOL_WINDOWS):
        win = jnp.where(lane_grp == g, w, win)
    pos = si * tm + lax.broadcasted_iota(I32, (tm, pw), 0)
    cnt = jnp.minimum(pos + 1, win).astype(F32)
    pooled = pooled / cnt - u
    pool_out = jnp.dot(pooled.astype(BF16), pbd_ref[...], preferred_element_type=F32) * psc_ref[...]
    def token_order(slot, ref):
        d, n, w = ref.shape
        f1 = SINGLE_LOAD_STRIDE if d > SINGLE_LOAD_STRIDE else 1
        for c in range(w // LANES):
            cols = slice(c * LANES, (c + 1) * LANES)
            if f1 == 1:
                for r in range(d):
                    til_ref[slot, c, pl.ds(r, n, stride=d), :] = ref[r, :, cols]
            else:
                f2, slab = d // f1, tm // f1
                for r in range(d):
                    r_lo, r_hi = r % f1, r // f1
                    tmp_ref[c, pl.ds(r_lo * slab + r_hi, n, stride=f2), :] = ref[r, :, cols]
                for r_lo in range(f1):
                    til_ref[slot, c, pl.ds(r_lo, slab, stride=f1), :] = tmp_ref[c, r_lo * slab:(r_lo + 1) * slab, :]
        return jnp.concatenate([til_ref[slot, c] for c in range(w // LANES)], axis=1)

    l0 = l0_ref[...]
    l1 = token_order(0, l1_ref)
    l2 = token_order(1, l2_ref)
    m = jnp.maximum(jnp.maximum(l0, l1), l2)
    w0 = jnp.exp(l0 - m)
    w1 = jnp.exp(l1 - m)
    w2 = jnp.exp(l2 - m)
    attn = (w0 * o0_ref[...] + w1 * token_order(2, o1_ref) + w2 * token_order(3, o2_ref)) / (w0 + w1 + w2)
    mixed = (jnp.dot(pool_out.astype(BF16), wout_ref[0:pw, :], preferred_element_type=F32)
             + jnp.dot(attn.astype(BF16), wout_ref[pw:, :], preferred_element_type=F32))
    x1 = x_ref[...] + gate1_ref[0] * mixed
    h2 = _rms(x1) * gffn_ref[...]
    h2 = h2 * (1.0 + sc2_ref[0]) + sh2_ref[0]
    h2_ref[...] = _pack_bf16_pairs(h2)
    hb = h2.astype(BF16)
    a = jnp.dot(hb, wsg_ref[...], preferred_element_type=F32)
    b = jnp.dot(hb, wsu_ref[...], preferred_element_type=F32)
    act = (a * _sigmoid(a)) * b
    shared = jnp.dot(act.astype(BF16), wsd_ref[...], preferred_element_type=F32)
    xacc_ref[...] = x1 + gate2_ref[0] * shared
    @pl.when(i == 0)
    def _():
        base_ref[...] = jnp.zeros_like(base_ref)

    C = e_ref.shape[2]
    for t0 in range(0, tm, ROUTE_TILE):
        idxs, gates, ranks = _route_tile(hb[t0:t0 + ROUTE_TILE, :], wr_ref, bias_ref, before_ref, base_ref)
        for k in range(TOP_K):
            g_ref[k:k + 1, t0:t0 + ROUTE_TILE] = gates[k]
            for c in range(ROUTE_TILE // C):
                e_ref[t0 // C + c, k:k + 1, :] = idxs[k][:, c * C:(c + 1) * C]
                r_ref[t0 // C + c, k:k + 1, :] = ranks[k][:, c * C:(c + 1) * C]
    cnt_ref[...] = base_ref[...].astype(I32)


def _mid(u, attn_outs, xf, pool_bd, pool_scale, w_out_b, gate1, g_ffn, shift2, scale2, gate2,
         wsg_b, wsu_b, wsd_b, wr_t, bias_col, S):
    N, D = xf.shape
    E = wr_t.shape[0]
    tok = jnp.arange(ROUTE_TILE, dtype=I32)
    before = (tok[:, None] < tok[None, :]).astype(BF16)
    pw = u.shape[1]
    tm = MID_TILE
    spt = S // tm
    row = lambda i: (i, 0)
    vec = lambda i: (i // spt, 0, 0)
    full = lambda a: pl.BlockSpec(a.shape, lambda i: (0,) * a.ndim)
    hpt = tm // POOL_HALO
    in_specs = [pl.BlockSpec((tm, pw), row),
                pl.BlockSpec((POOL_HALO, pw), lambda i: (jnp.maximum(i * hpt - 1, 0), 0))]
    gw = ATT_GROUP_WIDTH
    in_specs += [pl.BlockSpec((tm, gw), row)] * 2
    for d in ATT_DILATIONS[1:]:
        in_specs += [pl.BlockSpec((d, tm // d, gw), lambda i: (i // spt, i % spt, 0))] * 2
    in_specs += [pl.BlockSpec((tm, D), row), full(pool_bd), full(pool_scale), full(w_out_b),
                 pl.BlockSpec((1, 1, D), vec), full(g_ffn), pl.BlockSpec((1, 1, D), vec),
                 pl.BlockSpec((1, 1, D), vec), pl.BlockSpec((1, 1, D), vec),
                 full(wsg_b), full(wsu_b), full(wsd_b), full(wr_t), full(bias_col), full(before)]
    col = lambda i: (0, i)
    C = SC_CHUNK
    chunked = pl.BlockSpec((tm // C, TOP_K, C), lambda i: (i, 0, 0))
    return pl.pallas_call(
        functools.partial(_mid_kernel, spt),
        grid=(N // tm,),
        in_specs=in_specs,
        out_specs=[pl.BlockSpec((tm, D), row), pl.BlockSpec((tm, D // 2), row),
                   chunked, pl.BlockSpec((TOP_K, tm), col), chunked,
                   pl.BlockSpec((E, 1), lambda i: (0, 0))],
        out_shape=[jax.ShapeDtypeStruct((N, D), F32), jax.ShapeDtypeStruct((N, D // 2), U32),
                   jax.ShapeDtypeStruct((N // C, TOP_K, C), I32), jax.ShapeDtypeStruct((TOP_K, N), F32),
                   jax.ShapeDtypeStruct((N // C, TOP_K, C), I32), jax.ShapeDtypeStruct((E, 1), I32)],
        scratch_shapes=[pltpu.VMEM((4, gw // LANES, tm, LANES), F32), pltpu.VMEM((gw // LANES, tm, LANES), F32),
                        pltpu.VMEM((E, 1), F32)],
        compiler_params=pltpu.CompilerParams(dimension_semantics=("arbitrary",)),
        name="mid",
    )(u, u, *attn_outs, xf, pool_bd, pool_scale, w_out_b, gate1, g_ffn, shift2, scale2, gate2,
      wsg_b, wsu_b, wsd_b, wr_t, bias_col, before)


def _route_tile(hb, wr_ref, bias_ref, before_ref, base_ref):
    T = hb.shape[0]
    E = wr_ref.shape[0]
    gsz = E // N_EXPERT_GROUPS
    logits = lax.dot_general(wr_ref[...], hb, (((1,), (1,)), ((), ())), preferred_element_type=F32)
    scores = _sigmoid(logits)
    biased = scores + bias_ref[...]
    giota = lax.broadcasted_iota(I32, (gsz, T), 0)
    gscore = []
    for g in range(N_EXPERT_GROUPS):
        blk = biased[g * gsz:(g + 1) * gsz, :]
        m1 = jnp.max(blk, axis=0, keepdims=True)
        i1 = jnp.min(jnp.where(blk == m1, giota, gsz), axis=0, keepdims=True)
        m2 = jnp.max(jnp.where(giota == i1, NEG_INF, blk), axis=0, keepdims=True)
        gscore.append(m1 + m2)
    parts = []
    for g in range(N_EXPERT_GROUPS):
        beaten = jnp.zeros((1, T), I32)
        for o in range(N_EXPERT_GROUPS):
            if o == g:
                continue
            wins = (gscore[o] >= gscore[g]) if o < g else (gscore[o] > gscore[g])
            beaten = beaten + wins.astype(I32)
        keep = jnp.broadcast_to(beaten, (gsz, T)) < TOPK_GROUPS
        parts.append(jnp.where(keep, biased[g * gsz:(g + 1) * gsz, :], NEG_INF))
    cur = jnp.concatenate(parts, axis=0)
    eiota = lax.broadcasted_iota(I32, (E, T), 0)
    live = cur > NEG_INF
    idxs, gates, picks = [], [], []
    sub = lax.broadcasted_iota(I32, (8, T), 0)
    for k in range(TOP_K):
        nodes = [(cur[8 * j:8 * (j + 1), :], j, scores[8 * j:8 * (j + 1), :]) for j in range(E // 8)]
        while len(nodes) > 1:
            merged = []
            for p in range(0, len(nodes), 2):
                (va, ta, sa), (vb, tb, sb) = nodes[p], nodes[p + 1]
                later = vb > va
                merged.append((jnp.where(later, vb, va), jnp.where(later, tb, ta), jnp.where(later, sb, sa)))
            nodes = merged
        v8, t8, s8 = nodes[0]
        e8 = t8 * 8 + sub
        m = jnp.max(v8, axis=0, keepdims=True)
        idx = jnp.min(jnp.where(v8 == m, e8, E), axis=0, keepdims=True)
        oh = eiota == idx
        won = e8 == idx
        gates.append(jnp.sum(jnp.where(won, s8, 0.0), axis=0, keepdims=True))
        idxs.append(idx)
        picks.append((t8, won))
        cur = jnp.where(oh, NEG_INF, cur)
    selm = jnp.where(live & (cur == NEG_INF), 1.0, 0.0)
    gsum = gates[0]
    for k in range(1, TOP_K):
        gsum = gsum + gates[k]
    gates = [gk / gsum * ROUTED_SCALE for gk in gates]
    tot = jnp.dot(selm.astype(BF16), before_ref[...], preferred_element_type=F32) + base_ref[...]
    tiles = [tot[8 * j:8 * (j + 1), :] for j in range(E // 8)]
    ranks = []
    for t8, won in picks:
        level, bit = tiles, 0
        while len(level) > 1:
            odd = ((t8 >> bit) & 1) == 1
            level = [jnp.where(odd, level[p + 1], level[p]) for p in range(0, len(level), 2)]
            bit += 1
        ranks.append(jnp.sum(jnp.where(won, level[0], 0.0), axis=0, keepdims=True).astype(I32))
    base_ref[...] = base_ref[...] + jnp.sum(selm, axis=1, keepdims=True)
    return idxs, gates, ranks


def _sc_dispatch(eidx_c, rank_c, offs, h2p, P):
    N, W = h2p.shape
    nch, K, C = eidx_c.shape
    E = offs.shape[0]
    info = plsc.get_sparse_core_info()
    nw = info.num_cores * info.num_subcores
    L = info.num_lanes
    assert nch % nw == 0, "token chunks must split evenly over the vector subcores"
    per_w = nch // nw
    mesh = plsc.VectorSubcoreMesh(core_axis_name="c", subcore_axis_name="s")

    @functools.partial(
        pl.kernel, mesh=mesh,
        out_type=[jax.ShapeDtypeStruct((P, W), h2p.dtype), jax.ShapeDtypeStruct((nch, K, C), I32)],
        scratch_types=[pltpu.VMEM((E,), I32), pltpu.VMEM((K, C), I32), pltpu.VMEM((K, C), I32),
                       pltpu.VMEM((K, C), I32), pltpu.VMEM((C, W), h2p.dtype), pltpu.SemaphoreType.DMA],
        compiler_params=pltpu.CompilerParams(needs_layout_passes=False),
        name="sc_dispatch",
    )
    def k(e_hbm, r_hbm, off_hbm, h_hbm, xs_hbm, dest_hbm, off_v, e_v, r_v, idx_v, rows_v, sem):
        wid = lax.axis_index("s") * info.num_cores + lax.axis_index("c")
        pltpu.sync_copy(off_hbm, off_v)

        @pl.loop(0, per_w)
        def _(j):
            ch = wid * per_w + j
            pltpu.sync_copy(e_hbm.at[ch], e_v)
            pltpu.sync_copy(r_hbm.at[ch], r_v)
            for kk in range(K):
                for q in range(C // L):
                    sl = pl.ds(q * L, L)
                    idx_v[kk, sl] = plsc.load_gather(off_v, [e_v[kk, sl]]) + r_v[kk, sl]
            pltpu.sync_copy(idx_v, dest_hbm.at[ch])
            pltpu.sync_copy(h_hbm.at[pl.ds(ch * C, C)], rows_v)
            copies = [pltpu.async_copy(rows_v, xs_hbm.at[idx_v.at[kk]], sem) for kk in range(K)]
            for cp in copies:
                cp.wait()

    return k(eidx_c, rank_c, offs, h2p)


def _gmm_kernel(bstart_ref, nbe_ref, cnt_ref, nu_ref, wg_ref, wu_ref, wd_ref, xs_hbm, ys_hbm,
                wgb, wub, wdb, xbuf, ybuf, xsem, ysem):
    step = pl.program_id(0)
    last = pl.num_programs(0) - 1
    epg = wg_ref.shape[0]
    ring, bm = xbuf.shape[0], xbuf.shape[1]
    nblk = ys_hbm.shape[0] // bm
    nused = nu_ref[0]

    def x_copy(b, slot):
        return pltpu.make_async_copy(xs_hbm.at[pl.ds(pl.multiple_of(b * bm, bm), bm), :], xbuf.at[slot],
                                     xsem.at[slot])

    def y_copy(b, slot):
        return pltpu.make_async_copy(ybuf.at[slot], ys_hbm.at[pl.ds(pl.multiple_of(b * bm, bm), bm), :],
                                     ysem.at[slot])

    @pl.when(step == 0)
    def _():
        for j in range(ring - 1):
            @pl.when(j < nused)
            def _():
                x_copy(j, j).start()

    def run_expert(ee):
        e = step * epg + ee
        b0 = bstart_ref[e]
        nb = nbe_ref[e]

        @pl.when(nb > 0)
        def _():
            def prefetch(t):
                @pl.when(t < nused)
                def _():
                    x_copy(t, jnp.bitwise_and(t, ring - 1)).start()

            def process(b, n, fresh):
                slots = [jnp.bitwise_and(b + j, ring - 1) for j in range(n)]
                for j in range(n):
                    x_copy(b + j, slots[j]).wait()
                prefetch(b + ring - 1)
                for j in range(n):
                    @pl.when(b + j >= ring)
                    def _():
                        y_copy(b + j - ring, slots[j]).wait()
                rows = lax.broadcasted_iota(I32, (n * bm, 1), 0)
                valid = cnt_ref[e] - (b - b0) * bm
                xp = jnp.concatenate([xbuf[s] for s in slots], axis=0)
                xb = jnp.concatenate(_unpack_bf16_pairs(jnp.where(rows < valid, xp, jnp.uint32(0))), axis=1)
                if fresh:
                    wg, wu, wd = (w[ee].astype(BF16) for w in (wg_ref, wu_ref, wd_ref))
                    wgb[...], wub[...], wdb[...] = wg, wu, wd
                else:
                    wg, wu, wd = wgb[...], wub[...], wdb[...]
                a = jnp.dot(xb, wg, preferred_element_type=F32)
                g = jnp.dot(xb, wu, preferred_element_type=F32)
                act = (a * _sigmoid(a)) * g
                yp = _pack_bf16_pairs(jnp.dot(act.astype(BF16), wd, preferred_element_type=F32))
                for j in range(n):
                    ybuf[slots[j]] = yp[j * bm:(j + 1) * bm, :]
                    y_copy(b + j, slots[j]).start()
                for j in range(1, n):
                    prefetch(b + ring - 1 + j)

            quads = lax.shift_right_logical(nb, 2)
            lax.fori_loop(0, quads, lambda j, c: (process(b0 + 4 * j, 4, True), c)[1], 0)
            rest2 = jnp.bitwise_and(nb, 2)
            rest1 = jnp.bitwise_and(nb, 1)
            for n, rest, first, have_copy in ((2, rest2, b0 + 4 * quads, quads > 0),
                                              (1, rest1, b0 + 4 * quads + rest2, nb > 1)):
                @pl.when((rest != 0) & have_copy)
                def _():
                    process(first, n, False)

                @pl.when((rest != 0) & jnp.logical_not(have_copy))
                def _():
                    process(first, n, True)

    for ee in range(epg):
        run_expert(ee)

    @pl.when(step == last)
    def _():
        for back in range(ring, 0, -1):
            @pl.when(nused >= back)
            def _():
                y_copy(nused - back, jnp.bitwise_and(nused - back, ring - 1)).wait()
        ybuf[0] = jnp.zeros(ybuf.shape[1:], ybuf.dtype)
        lax.fori_loop(nused, nblk, lambda b, c: (y_copy(b, 0).start(), c)[1], 0)
        lax.fori_loop(nused, nblk, lambda b, c: (y_copy(b, 0).wait(), c)[1], 0)


def _gmm(bstart, nb_e, counts, nused, xs, w_gate, w_up, w_down):
    P, W = xs.shape
    E, D, F = w_gate.shape
    bm = GMM_BLOCK
    epg = GMM_EXPERTS_PER_STEP
    wsel = lambda s, *_: (s, 0, 0)
    grid_spec = pltpu.PrefetchScalarGridSpec(
        num_scalar_prefetch=4,
        grid=(E // epg,),
        in_specs=[pl.BlockSpec((epg, D, F), wsel), pl.BlockSpec((epg, D, F), wsel), pl.BlockSpec((epg, F, D), wsel),
                  pl.BlockSpec(memory_space=pl.ANY)],
        out_specs=pl.BlockSpec(memory_space=pl.ANY),
        scratch_shapes=[pltpu.VMEM((D, F), BF16), pltpu.VMEM((D, F), BF16), pltpu.VMEM((F, D), BF16),
                        pltpu.VMEM((GMM_RING, bm, W), xs.dtype), pltpu.VMEM((GMM_RING, bm, W), xs.dtype),
                        pltpu.SemaphoreType.DMA((GMM_RING,)), pltpu.SemaphoreType.DMA((GMM_RING,))],
    )
    return pl.pallas_call(
        _gmm_kernel,
        grid_spec=grid_spec,
        out_shape=jax.ShapeDtypeStruct((P, W), xs.dtype),
        compiler_params=pltpu.CompilerParams(dimension_semantics=("arbitrary",)),
        name="gmm",
    )(bstart, nb_e, counts, nused, w_gate, w_up, w_down, xs)


def _sc_gather(dest_c, ys):
    nch, K, C = dest_c.shape
    W = ys.shape[1]
    H = SC_GATHER_ROWS
    info = plsc.get_sparse_core_info()
    nw = info.num_cores * info.num_subcores
    assert nch % nw == 0, "token chunks must split evenly over the vector subcores"
    per_w = nch // nw
    nbuf = SC_GATHER_BUFS
    ahead = nbuf - 1
    mesh = plsc.VectorSubcoreMesh(core_axis_name="c", subcore_axis_name="s")
    items = [(kk, hh) for kk in range(K) for hh in range(C // H)]

    @functools.partial(
        pl.kernel, mesh=mesh,
        out_type=jax.ShapeDtypeStruct((K, nch * C, W), ys.dtype),
        scratch_types=([pltpu.VMEM((K, C), I32)] + [pltpu.VMEM((H, W), ys.dtype)] * nbuf
                       + [pltpu.SemaphoreType.DMA] * (2 * nbuf)),
        name="sc_gather",
    )
    def k(dest_hbm, ys_hbm, yk_hbm, idx_v, *rest):
        bufs, gsem, wsem = rest[:nbuf], rest[nbuf:2 * nbuf], rest[2 * nbuf:]
        wid = lax.axis_index("s") * info.num_cores + lax.axis_index("c")

        @pl.loop(0, per_w)
        def _(j):
            ch = wid * per_w + j
            pltpu.sync_copy(dest_hbm.at[ch], idx_v)

            def gather(i):
                kk, hh = items[i]
                return pltpu.async_copy(ys_hbm.at[idx_v.at[kk, pl.ds(hh * H, H)]], bufs[i % nbuf], gsem[i % nbuf])

            def write(i):
                kk, hh = items[i]
                return pltpu.async_copy(bufs[i % nbuf], yk_hbm.at[kk, pl.ds(ch * C + hh * H, H)], wsem[i % nbuf])

            n = len(items)
            g = {i: gather(i) for i in range(ahead)}
            w = {}
            for i in range(n):
                g[i].wait()
                w[i] = write(i)
                if i + ahead < n:
                    if i >= 1:
                        w.pop(i - 1).wait()
                    g[i + ahead] = gather(i + ahead)
            for i in sorted(w):
                w[i].wait()

    return k(dest_c, ys)


def _combine_kernel(yk_ref, g_ref, gate2_ref, xacc_ref, gfin_ref, o_ref, gpad_ref):
    @pl.when(pl.program_id(0) == 0)
    def _():
        gpad_ref[...] = jnp.zeros_like(gpad_ref)

    gpad_ref[0:TOP_K, :] = g_ref[...]
    gt = gpad_ref[...].T
    hi_mask = jnp.uint32(0xFFFF0000)
    acc_lo = acc_hi = None
    for k in range(TOP_K):
        p = yk_ref[k]
        g = gt[:, k:k + 1]
        lo = lax.bitcast_convert_type(p << 16, F32) * g
        hi = lax.bitcast_convert_type(p & hi_mask, F32) * g
        acc_lo = lo if k == 0 else acc_lo + lo
        acc_hi = hi if k == 0 else acc_hi + hi
    routed = jnp.concatenate([acc_lo, acc_hi], axis=1)
    x2 = xacc_ref[...] + gate2_ref[0] * routed
    o_ref[...] = _rms(x2) * gfin_ref[...]


def _combine(yk, gates, gate2, xacc, g_final, S):
    N, D = xacc.shape
    W = yk.shape[2]
    T = COMBINE_TILE
    spt = S // T
    return pl.pallas_call(
        _combine_kernel,
        grid=(N // T,),
        in_specs=[pl.BlockSpec((TOP_K, T, W), lambda i: (0, i, 0)),
                  pl.BlockSpec((TOP_K, T), lambda i: (0, i)),
                  pl.BlockSpec((1, 1, D), lambda i: (i // spt, 0, 0)),
                  pl.BlockSpec((T, D), lambda i: (i, 0)),
                  pl.BlockSpec((1, D), lambda i: (0, 0))],
        out_specs=pl.BlockSpec((T, D), lambda i: (i, 0)),
        out_shape=jax.ShapeDtypeStruct((N, D), F32),
        scratch_shapes=[pltpu.VMEM((LANES, T), F32)],
        compiler_params=pltpu.CompilerParams(dimension_semantics=("arbitrary",)),
        name="combine",
    )(yk, gates, gate2, xacc, g_final)


def _layer(xf, B, S, mod, g_mix, w_in, pool_w, pool_scale, w_out, g_ffn, w_router, router_bias,
           w_gate, w_up, w_down, ws_gate, ws_up, ws_down):
    N, D = xf.shape
    E = w_router.shape[1]
    pw = pool_scale.shape[0]
    shift1, scale1, gate1, shift2, scale2, gate2 = [m.reshape(B, 1, D) for m in jnp.split(mod, 6, axis=-1)]
    u, qkv = _in_proj(xf, g_mix.reshape(1, D), shift1, scale1, w_in.astype(BF16), S, pw)
    attn_outs = []
    for a, d in zip(qkv, ATT_DILATIONS):
        o, lse = _attention(a, S // d // ATT_BLOCK)
        shape = (N, ATT_GROUP_WIDTH) if d == 1 else (B * d, S // d, ATT_GROUP_WIDTH)
        attn_outs += [o.reshape(shape), lse.reshape(shape)]
    ng = pool_w.shape[0]
    pool_bd = jnp.einsum('gcd,gh->gchd', pool_w, jnp.eye(ng, dtype=pool_w.dtype)).reshape(pw, pw).astype(BF16)
    xacc, h2, eidx, gates, rank, counts = _mid(
        u, attn_outs, xf, pool_bd, pool_scale.reshape(1, pw), w_out.astype(BF16), gate1,
        g_ffn.reshape(1, D), shift2, scale2, gate2,
        ws_gate.astype(BF16), ws_up.astype(BF16), ws_down.astype(BF16),
        w_router.T.astype(BF16), router_bias.reshape(E, 1).astype(F32), S)
    bm = GMM_BLOCK
    nblk = N * TOP_K // bm + E
    nb_e = (counts[:, 0] + bm - 1) // bm
    bend = jnp.cumsum(nb_e)
    bstart = (bend - nb_e).astype(I32)
    nused = bend[-1:].astype(I32)
    xs, dest_c = _sc_dispatch(eidx, rank, bstart * bm, h2, nblk * bm)
    ys = _gmm(bstart, nb_e.astype(I32), counts[:, 0], nused, xs, w_gate, w_up, w_down)
    return _sc_gather(dest_c, ys), gates, gate2, xacc


def kernel(x, c, w_ada, b_ada, g_mix, w_in, pool_w, pool_scale, w_out, g_ffn, w_router, router_bias,
           w_gate, w_up, w_down, ws_gate, ws_up, ws_down, g_final):
    B, S, D = x.shape
    depth = w_ada.shape[0]
    assert depth == 1, "the final residual is fused with the final norm, so exactly one layer is supported"
    assert S % (ATT_DILATIONS[-1] * ATT_BLOCK) == 0
    assert all(S % t == 0 for t in (IN_TILE, MID_TILE, COMBINE_TILE, ATT_BLOCKS_PER_STEP * ATT_BLOCK))
    assert MID_TILE % ROUTE_TILE == 0 and ROUTE_TILE % SC_CHUNK == 0 and SC_CHUNK % SC_GATHER_ROWS == 0
    assert w_gate.shape[1] % GMM_EXPERTS_PER_STEP == 0 and max(POOL_WINDOWS) <= POOL_HALO
    xf = x.reshape(B * S, D)
    mod = _ada(c, w_ada[0], b_ada[0])
    yk, gates, gate2, xacc = _layer(
        xf, B, S, mod, g_mix[0], w_in[0], pool_w[0], pool_scale[0], w_out[0], g_ffn[0], w_router[0],
        router_bias[0], w_gate[0], w_up[0], w_down[0], ws_gate[0], ws_up[0], ws_down[0])
    out = _combine(yk, gates, gate2, xacc, g_final.reshape(1, D), S)
    return out.reshape(B, S, D)
```

```python
import functools

import jax
import jax.numpy as jnp
from jax import lax
from jax.experimental import pallas as pl
from jax.experimental.pallas import tpu as pltpu
from jax.experimental.pallas import tpu_sc as plsc

F32 = jnp.float32
BF16 = jnp.bfloat16
I32 = jnp.int32
U32 = jnp.uint32

LANES = 128
SINGLE_LOAD_STRIDE = 4
NORM_EPS = 1e-6
POOL_WINDOWS = (2, 4, 8, 16)
POOL_HALO = 16
ATT_DILATIONS = (1, 4, 16)
ATT_BLOCK = 128
ATT_HEADS_PER_GROUP = 4
ATT_HEAD_DIM = 64
ATT_GROUP_WIDTH = ATT_HEADS_PER_GROUP * ATT_HEAD_DIM
N_EXPERT_GROUPS = 8
TOPK_GROUPS = 4
TOP_K = 8
ROUTED_SCALE = 2.5

IN_TILE = 1024
ATT_BLOCKS_PER_STEP = 16
ATT_UNROLL = 8
MID_TILE = 512
ROUTE_TILE = 512
COMBINE_TILE = 512
GMM_BLOCK = 128
GMM_RING = 16
GMM_EXPERTS_PER_STEP = 1
SC_CHUNK = 128
SC_GATHER_ROWS = 32
SC_GATHER_BUFS = 6

NEG_INF = float("-inf")


def _sigmoid(v):
    return 1.0 / (1.0 + jnp.exp(-v))


def _rms(v):
    return v * lax.rsqrt(jnp.mean(v * v, axis=-1, keepdims=True) + NORM_EPS)


def _pack_bf16_pairs(v):
    n = v.shape[1] // 2
    lo = lax.bitcast_convert_type(v[:, :n].astype(BF16).astype(F32), U32)
    hi = lax.bitcast_convert_type(v[:, n:].astype(BF16).astype(F32), U32)
    return (hi & jnp.uint32(0xFFFF0000)) | (lo >> 16)


def _unpack_bf16_pairs(p):
    lo = lax.bitcast_convert_type(p << 16, F32).astype(BF16)
    hi = lax.bitcast_convert_type(p & jnp.uint32(0xFFFF0000), F32).astype(BF16)
    return lo, hi


def _ada_kernel(c_ref, w_ref, b_ref, o_ref):
    c = c_ref[...]
    cs = c * _sigmoid(c)
    o_ref[...] = jnp.dot(cs, w_ref[...], preferred_element_type=F32,
                         precision=lax.Precision.HIGHEST) + b_ref[...]


def _ada(c, w_ada, b_ada):
    B, D = c.shape
    W = w_ada.shape[1]
    tn = 1024
    return pl.pallas_call(
        _ada_kernel,
        grid=(W // tn,),
        in_specs=[pl.BlockSpec((B, D), lambda j: (0, 0)),
                  pl.BlockSpec((D, tn), lambda j: (0, j)),
                  pl.BlockSpec((1, tn), lambda j: (0, j))],
        out_specs=pl.BlockSpec((B, tn), lambda j: (0, j)),
        out_shape=jax.ShapeDtypeStruct((B, W), F32),
        name="ada",
    )(c, w_ada, b_ada.reshape(1, W))


def _in_kernel(x_ref, g_ref, sh_ref, sc_ref, w_ref, pool_ref, q0_ref, q1_ref, q2_ref, scr_ref, tmp_ref):
    h = _rms(x_ref[...]) * g_ref[...]
    h = h * (1.0 + sc_ref[0]) + sh_ref[0]
    hb = h.astype(BF16)
    tm = x_ref.shape[0]
    pw = pool_ref.shape[1]
    gw = ATT_GROUP_WIDTH
    pool_ref[...] = jnp.dot(hb, w_ref[:, 0:pw], preferred_element_type=F32)
    for g, (out, d) in enumerate(zip((q0_ref, q1_ref, q2_ref), ATT_DILATIONS)):
        for sec in range(3):
            c0 = pw + sec * 3 * gw + g * gw
            res = jnp.dot(hb, w_ref[:, c0:c0 + gw], preferred_element_type=F32)
            if d == 1:
                out[:, sec * gw:(sec + 1) * gw] = res.astype(BF16)
            else:
                for c in range(gw // LANES):
                    scr_ref[c] = res[:, c * LANES:(c + 1) * LANES]
                    src, f1 = scr_ref.at[c], 1
                    if d > SINGLE_LOAD_STRIDE:
                        f1 = SINGLE_LOAD_STRIDE
                        for q in range(f1):
                            tmp_ref[c, q * (tm // f1):(q + 1) * (tm // f1), :] = scr_ref[c, pl.ds(q, tm // f1, stride=f1), :]
                        src = tmp_ref.at[c]
                    for r in range(d):
                        r_lo, r_hi = r % f1, r // f1
                        c1 = sec * gw + c * LANES
                        out[r, :, c1:c1 + LANES] = src[pl.ds(r_lo * (tm // f1) + r_hi, tm // d, stride=d // f1),
                                                       :].astype(BF16)


def _in_proj(xf, g_mix, shift1, scale1, w_in_b, S, pool_width):
    N, D = xf.shape
    tm = IN_TILE
    spt = S // tm
    vec = lambda i: (i // spt, 0, 0)
    row = lambda i: (i, 0)
    gw3 = 3 * ATT_GROUP_WIDTH
    B = N // S
    res_spec = lambda d: pl.BlockSpec((d, tm // d, gw3), lambda i: (i // spt, i % spt, 0))
    res_shape = lambda d: jax.ShapeDtypeStruct((B * d, S // d, gw3), BF16)
    outs = pl.pallas_call(
        _in_kernel,
        grid=(N // tm,),
        in_specs=[pl.BlockSpec((tm, D), row),
                  pl.BlockSpec((1, D), lambda i: (0, 0)),
                  pl.BlockSpec((1, 1, D), vec),
                  pl.BlockSpec((1, 1, D), vec),
                  pl.BlockSpec(w_in_b.shape, lambda i: (0, 0))],
        out_specs=[pl.BlockSpec((tm, pool_width), row), pl.BlockSpec((tm, gw3), row)]
                  + [res_spec(d) for d in ATT_DILATIONS[1:]],
        out_shape=[jax.ShapeDtypeStruct((N, pool_width), F32), jax.ShapeDtypeStruct((N, gw3), BF16)]
                  + [res_shape(d) for d in ATT_DILATIONS[1:]],
        scratch_shapes=[pltpu.VMEM((ATT_GROUP_WIDTH // LANES, tm, LANES), F32)] * 2,
        name="in_proj",
    )(xf, g_mix, shift1, scale1, w_in_b)
    return outs[0], [o.reshape(N, gw3) for o in outs[1:]]


def _attn_kernel(nbs, a_ref, halo_ref, o_ref, lse_ref, kv_ref, band_ref):
    i = pl.program_id(0)
    R = a_ref.shape[0] // ATT_BLOCK
    gw = ATT_GROUP_WIDTH
    blk = ATT_BLOCK
    nh = ATT_HEADS_PER_GROUP
    kv_ref[0:blk, :] = halo_ref[:, gw:3 * gw]
    kv_ref[blk:, :] = a_ref[:, gw:3 * gw]
    row = lax.broadcasted_iota(I32, (nh * blk, 2 * blk), 0) % blk
    col = lax.broadcasted_iota(I32, (nh * blk, 2 * blk), 1)
    in_band = (col >= row) & (col <= row + blk)
    band_ref[0] = jnp.where(in_band, 0.0, NEG_INF)
    band_ref[1] = jnp.where(in_band & (col >= blk), 0.0, NEG_INF)
    head_of_lane = lax.broadcasted_iota(I32, (blk, gw), 1) // ATT_HEAD_DIM
    nt = (((1,), (1,)), ((), ()))

    def one_block(jj, start):
        r0 = pl.multiple_of(jj * blk, blk)
        qf = a_ref[pl.ds(r0, blk), 0:gw].astype(F32) * (ATT_HEAD_DIM ** -0.5)
        q4 = jnp.concatenate([jnp.where(head_of_lane == h, qf, 0.0) for h in range(nh)], axis=0).astype(BF16)
        kc = kv_ref[pl.ds(r0, 2 * blk), 0:gw]
        vc = kv_ref[pl.ds(r0, 2 * blk), gw:2 * gw]
        s = lax.dot_general(q4, kc, nt, preferred_element_type=F32) + band_ref[1 if start is True else 0]
        if start is not None and start is not True:
            s = jnp.where(col >= jnp.where(start, blk, 0), s, NEG_INF)
        m = jnp.max(s, axis=1, keepdims=True)
        p = jnp.exp(s - m)
        l = jnp.sum(p, axis=1, keepdims=True)
        o4 = jnp.dot(p.astype(BF16), vc, preferred_element_type=F32) / l
        lse4 = m + jnp.log(l)
        o = jnp.zeros((blk, gw), F32)
        lse = jnp.zeros((blk, gw), F32)
        for h in range(nh):
            hm = head_of_lane == h
            o = jnp.where(hm, o4[h * blk:(h + 1) * blk, :], o)
            lse = jnp.where(hm, lse4[h * blk:(h + 1) * blk, :], lse)
        o_ref[pl.ds(r0, blk), :] = o
        lse_ref[pl.ds(r0, blk), :] = lse

    U = ATT_UNROLL
    assert U % nbs == 0 or nbs % U == 0

    def body(it, carry):
        for j in range(U):
            if U % nbs == 0:
                start = True if j % nbs == 0 else None
            else:
                start = (((i * R + it * U) % nbs) == 0) if j == 0 else None
            one_block(it * U + j, start)
        return carry

    lax.fori_loop(0, R // U, body, 0)


def _attention(a, nbs):
    N = a.shape[0]
    R = ATT_BLOCKS_PER_STEP
    gw = ATT_GROUP_WIDTH
    tm = R * ATT_BLOCK
    return pl.pallas_call(
        functools.partial(_attn_kernel, nbs),
        grid=(N // tm,),
        in_specs=[pl.BlockSpec((tm, 3 * gw), lambda i: (i, 0)),
                  pl.BlockSpec((ATT_BLOCK, 3 * gw), lambda i: (jnp.maximum(i * R - 1, 0), 0))],
        out_specs=[pl.BlockSpec((tm, gw), lambda i: (i, 0))] * 2,
        out_shape=[jax.ShapeDtypeStruct((N, gw), F32)] * 2,
        scratch_shapes=[pltpu.VMEM((tm + ATT_BLOCK, 2 * gw), BF16),
                        pltpu.VMEM((2, ATT_HEADS_PER_GROUP * ATT_BLOCK, 2 * ATT_BLOCK), F32)],
        name="attn",
    )(a, a)


def _mid_kernel(spt, u_ref, uh_ref, o0_ref, l0_ref, o1_ref, l1_ref, o2_ref, l2_ref, x_ref,
                pbd_ref, psc_ref, wout_ref, gate1_ref, gffn_ref, sh2_ref, sc2_ref, gate2_ref,
                wsg_ref, wsu_ref, wsd_ref, wr_ref, bias_ref, before_ref,
                xacc_ref, h2_ref, e_ref, g_ref, r_ref, cnt_ref, til_ref, tmp_ref, base_ref):
    i = pl.program_id(0)
    tm, pw = u_ref.shape
    si = i % spt
    u = u_ref[...]
    keep = jnp.full((POOL_HALO, pw), si, I32) > 0
    ext = jnp.concatenate([jnp.where(keep, uh_ref[...], 0.0), u], axis=0)
    lane_grp = lax.broadcasted_iota(I32, (tm, pw), 1) // (pw // len(POOL_WINDOWS))
    pooled = jnp.zeros((tm, pw), F32)
    s, w = ext, 1
    while w < POOL_HALO:
        s = s + pltpu.roll(s, w, axis=0)
        w *= 2
        if w in POOL_WINDOWS:
            pooled = jnp.where(lane_grp == POOL_WINDOWS.index(w), s[POOL_HALO:, :], pooled)
    win = jnp.zeros((tm, pw), I32)
    for g, w in enumerate(POOL_WINDOWS):
        win = jnp.where(lane_grp == g, w, win)
    pos = si * tm + lax.broadcasted_iota(I32, (tm, pw), 0)
    cnt = jnp.minimum(pos + 1, win).astype(F32)
    pooled = pooled / cnt - u
    pool_out = jnp.dot(pooled.astype(BF16), pbd_ref[...], preferred_element_type=F32) * psc_ref[...]
    def token_order(slot, ref):
        d, n, w = ref.shape
        f1 = SINGLE_LOAD_STRIDE if d > SINGLE_LOAD_STRIDE else 1
        for c in range(w // LANES):
            cols = slice(c * LANES, (c + 1) * LANES)
            if f1 == 1:
                for r in range(d):
                    til_ref[slot, c, pl.ds(r, n, stride=d), :] = ref[r, :, cols]
            else:
                f2, slab = d // f1, tm // f1
                for r in range(d):
                    r_lo, r_hi = r % f1, r // f1
                    tmp_ref[c, pl.ds(r_lo * slab + r_hi, n, stride=f2), :] = ref[r, :, cols]
                for r_lo in range(f1):
                    til_ref[slot, c, pl.ds(r_lo, slab, stride=f1), :] = tmp_ref[c, r_lo * slab:(r_lo + 1) * slab, :]
        return jnp.concatenate([til_ref[slot, c] for c in range(w // LANES)], axis=1)

    l0 = l0_ref[...]
    l1 = token_order(0, l1_ref)
    l2 = token_order(1, l2_ref)
    m = jnp.maximum(jnp.maximum(l0, l1), l2)
    w0 = jnp.exp(l0 - m)
    w1 = jnp.exp(l1 - m)
    w2 = jnp.exp(l2 - m)
    attn = (w0 * o0_ref[...] + w1 * token_order(2, o1_ref) + w2 * token_order(3, o2_ref)) / (w0 + w1 + w2)
    mixed = jnp.dot(jnp.concatenate([pool_out.astype(BF16), attn.astype(BF16)], axis=1), wout_ref[...],
                    preferred_element_type=F32)
    x1 = x_ref[...] + gate1_ref[0] * mixed
    h2 = _rms(x1) * gffn_ref[...]
    h2 = h2 * (1.0 + sc2_ref[0]) + sh2_ref[0]
    h2_ref[...] = _pack_bf16_pairs(h2)
    hb = h2.astype(BF16)
    a = jnp.dot(hb, wsg_ref[...], preferred_element_type=F32)
    b = jnp.dot(hb, wsu_ref[...], preferred_element_type=F32)
    act = (a * _sigmoid(a)) * b
    shared = jnp.dot(act.astype(BF16), wsd_ref[...], preferred_element_type=F32)
    xacc_ref[...] = x1 + gate2_ref[0] * shared
    @pl.when(i == 0)
    def _():
        base_ref[...] = jnp.zeros_like(base_ref)

    C = e_ref.shape[2]
    for t0 in range(0, tm, ROUTE_TILE):
        idxs, gates, ranks = _route_tile(hb[t0:t0 + ROUTE_TILE, :], wr_ref, bias_ref, before_ref, base_ref)
        for k in range(TOP_K):
            g_ref[k:k + 1, t0:t0 + ROUTE_TILE] = gates[k]
            for c in range(ROUTE_TILE // C):
                e_ref[t0 // C + c, k:k + 1, :] = idxs[k][:, c * C:(c + 1) * C]
                r_ref[t0 // C + c, k:k + 1, :] = ranks[k][:, c * C:(c + 1) * C]
    cnt_ref[...] = base_ref[...].astype(I32)


def _mid(u, attn_outs, xf, pool_bd, pool_scale, w_out_b, gate1, g_ffn, shift2, scale2, gate2,
         wsg_b, wsu_b, wsd_b, wr_t, bias_col, S):
    N, D = xf.shape
    E = wr_t.shape[0]
    tok = jnp.arange(ROUTE_TILE, dtype=I32)
    before = (tok[:, None] < tok[None, :]).astype(BF16)
    pw = u.shape[1]
    tm = MID_TILE
    spt = S // tm
    row = lambda i: (i, 0)
    vec = lambda i: (i // spt, 0, 0)
    full = lambda a: pl.BlockSpec(a.shape, lambda i: (0,) * a.ndim)
    hpt = tm // POOL_HALO
    in_specs = [pl.BlockSpec((tm, pw), row),
                pl.BlockSpec((POOL_HALO, pw), lambda i: (jnp.maximum(i * hpt - 1, 0), 0))]
    gw = ATT_GROUP_WIDTH
    in_specs += [pl.BlockSpec((tm, gw), row)] * 2
    for d in ATT_DILATIONS[1:]:
        in_specs += [pl.BlockSpec((d, tm // d, gw), lambda i: (i // spt, i % spt, 0))] * 2
    in_specs += [pl.BlockSpec((tm, D), row), full(pool_bd), full(pool_scale), full(w_out_b),
                 pl.BlockSpec((1, 1, D), vec), full(g_ffn), pl.BlockSpec((1, 1, D), vec),
                 pl.BlockSpec((1, 1, D), vec), pl.BlockSpec((1, 1, D), vec),
                 full(wsg_b), full(wsu_b), full(wsd_b), full(wr_t), full(bias_col), full(before)]
    col = lambda i: (0, i)
    C = SC_CHUNK
    chunked = pl.BlockSpec((tm // C, TOP_K, C), lambda i: (i, 0, 0))
    return pl.pallas_call(
        functools.partial(_mid_kernel, spt),
        grid=(N // tm,),
        in_specs=in_specs,
        out_specs=[pl.BlockSpec((tm, D), row), pl.BlockSpec((tm, D // 2), row),
                   chunked, pl.BlockSpec((TOP_K, tm), col), chunked,
                   pl.BlockSpec((E, 1), lambda i: (0, 0))],
        out_shape=[jax.ShapeDtypeStruct((N, D), F32), jax.ShapeDtypeStruct((N, D // 2), U32),
                   jax.ShapeDtypeStruct((N // C, TOP_K, C), I32), jax.ShapeDtypeStruct((TOP_K, N), F32),
                   jax.ShapeDtypeStruct((N // C, TOP_K, C), I32), jax.ShapeDtypeStruct((E, 1), I32)],
        scratch_shapes=[pltpu.VMEM((4, gw // LANES, tm, LANES), F32), pltpu.VMEM((gw // LANES, tm, LANES), F32),
                        pltpu.VMEM((E, 1), F32)],
        compiler_params=pltpu.CompilerParams(dimension_semantics=("arbitrary",)),
        name="mid",
    )(u, u, *attn_outs, xf, pool_bd, pool_scale, w_out_b, gate1, g_ffn, shift2, scale2, gate2,
      wsg_b, wsu_b, wsd_b, wr_t, bias_col, before)


def _route_tile(hb, wr_ref, bias_ref, before_ref, base_ref):
    T = hb.shape[0]
    E = wr_ref.shape[0]
    gsz = E // N_EXPERT_GROUPS
    logits = lax.dot_general(wr_ref[...], hb, (((1,), (1,)), ((), ())), preferred_element_type=F32)
    scores = _sigmoid(logits)
    biased = scores + bias_ref[...]
    giota = lax.broadcasted_iota(I32, (gsz, T), 0)
    gscore = []
    for g in range(N_EXPERT_GROUPS):
        blk = biased[g * gsz:(g + 1) * gsz, :]
        m1 = jnp.max(blk, axis=0, keepdims=True)
        i1 = jnp.min(jnp.where(blk == m1, giota, gsz), axis=0, keepdims=True)
        m2 = jnp.max(jnp.where(giota == i1, NEG_INF, blk), axis=0, keepdims=True)
        gscore.append(m1 + m2)
    parts = []
    for g in range(N_EXPERT_GROUPS):
        beaten = jnp.zeros((1, T), I32)
        for o in range(N_EXPERT_GROUPS):
            if o == g:
                continue
            wins = (gscore[o] >= gscore[g]) if o < g else (gscore[o] > gscore[g])
            beaten = beaten + wins.astype(I32)
        keep = jnp.broadcast_to(beaten, (gsz, T)) < TOPK_GROUPS
        parts.append(jnp.where(keep, biased[g * gsz:(g + 1) * gsz, :], NEG_INF))
    cur = jnp.concatenate(parts, axis=0)
    eiota = lax.broadcasted_iota(I32, (E, T), 0)
    live = cur > NEG_INF
    idxs, gates, picks = [], [], []
    sub = lax.broadcasted_iota(I32, (8, T), 0)
    for k in range(TOP_K):
        nodes = [(cur[8 * j:8 * (j + 1), :], j, scores[8 * j:8 * (j + 1), :]) for j in range(E // 8)]
        while len(nodes) > 1:
            merged = []
            for p in range(0, len(nodes), 2):
                (va, ta, sa), (vb, tb, sb) = nodes[p], nodes[p + 1]
                later = vb > va
                merged.append((jnp.where(later, vb, va), jnp.where(later, tb, ta), jnp.where(later, sb, sa)))
            nodes = merged
        v8, t8, s8 = nodes[0]
        e8 = t8 * 8 + sub
        m = jnp.max(v8, axis=0, keepdims=True)
        idx = jnp.min(jnp.where(v8 == m, e8, E), axis=0, keepdims=True)
        oh = eiota == idx
        won = e8 == idx
        gates.append(jnp.sum(jnp.where(won, s8, 0.0), axis=0, keepdims=True))
        idxs.append(idx)
        picks.append((t8, won))
        cur = jnp.where(oh, NEG_INF, cur)
    selm = jnp.where(live & (cur == NEG_INF), 1.0, 0.0)
    gsum = gates[0]
    for k in range(1, TOP_K):
        gsum = gsum + gates[k]
    gates = [gk / gsum * ROUTED_SCALE for gk in gates]
    tot = jnp.dot(selm.astype(BF16), before_ref[...], preferred_element_type=F32) + base_ref[...]
    tiles = [tot[8 * j:8 * (j + 1), :] for j in range(E // 8)]
    ranks = []
    for t8, won in picks:
        level, bit = tiles, 0
        while len(level) > 1:
            odd = ((t8 >> bit) & 1) == 1
            level = [jnp.where(odd, level[p + 1], level[p]) for p in range(0, len(level), 2)]
            bit += 1
        ranks.append(jnp.sum(jnp.where(won, level[0], 0.0), axis=0, keepdims=True).astype(I32))
    base_ref[...] = base_ref[...] + jnp.sum(selm, axis=1, keepdims=True)
    return idxs, gates, ranks


def _sc_dispatch(eidx_c, rank_c, offs, h2p, P):
    N, W = h2p.shape
    nch, K, C = eidx_c.shape
    E = offs.shape[0]
    info = plsc.get_sparse_core_info()
    nw = info.num_cores * info.num_subcores
    L = info.num_lanes
    assert nch % nw == 0, "token chunks must split evenly over the vector subcores"
    per_w = nch // nw
    mesh = plsc.VectorSubcoreMesh(core_axis_name="c", subcore_axis_name="s")

    @functools.partial(
        pl.kernel, mesh=mesh,
        out_type=[jax.ShapeDtypeStruct((P, W), h2p.dtype), jax.ShapeDtypeStruct((nch, K, C), I32)],
        scratch_types=[pltpu.VMEM((E,), I32), pltpu.VMEM((K, C), I32), pltpu.VMEM((K, C), I32),
                       pltpu.VMEM((K, C), I32), pltpu.VMEM((C, W), h2p.dtype), pltpu.SemaphoreType.DMA],
        compiler_params=pltpu.CompilerParams(needs_layout_passes=False),
        name="sc_dispatch",
    )
    def k(e_hbm, r_hbm, off_hbm, h_hbm, xs_hbm, dest_hbm, off_v, e_v, r_v, idx_v, rows_v, sem):
        wid = lax.axis_index("s") * info.num_cores + lax.axis_index("c")
        pltpu.sync_copy(off_hbm, off_v)

        @pl.loop(0, per_w)
        def _(j):
            ch = wid * per_w + j
            pltpu.sync_copy(e_hbm.at[ch], e_v)
            pltpu.sync_copy(r_hbm.at[ch], r_v)
            for kk in range(K):
                for q in range(C // L):
                    sl = pl.ds(q * L, L)
                    idx_v[kk, sl] = plsc.load_gather(off_v, [e_v[kk, sl]]) + r_v[kk, sl]
            pltpu.sync_copy(idx_v, dest_hbm.at[ch])
            pltpu.sync_copy(h_hbm.at[pl.ds(ch * C, C)], rows_v)
            copies = [pltpu.async_copy(rows_v, xs_hbm.at[idx_v.at[kk]], sem) for kk in range(K)]
            for cp in copies:
                cp.wait()

    return k(eidx_c, rank_c, offs, h2p)


def _gmm_kernel(bstart_ref, nbe_ref, cnt_ref, nu_ref, wg_ref, wu_ref, wd_ref, xs_hbm, ys_hbm,
                wgb, wub, wdb, xbuf, ybuf, xsem, ysem):
    step = pl.program_id(0)
    last = pl.num_programs(0) - 1
    epg = wg_ref.shape[0]
    ring, bm = xbuf.shape[0], xbuf.shape[1]
    nblk = ys_hbm.shape[0] // bm
    nused = nu_ref[0]

    def x_copy(b, slot):
        return pltpu.make_async_copy(xs_hbm.at[pl.ds(pl.multiple_of(b * bm, bm), bm), :], xbuf.at[slot],
                                     xsem.at[slot])

    def y_copy(b, slot):
        return pltpu.make_async_copy(ybuf.at[slot], ys_hbm.at[pl.ds(pl.multiple_of(b * bm, bm), bm), :],
                                     ysem.at[slot])

    @pl.when(step == 0)
    def _():
        for j in range(ring - 1):
            @pl.when(j < nused)
            def _():
                x_copy(j, j).start()

    def run_expert(ee):
        e = step * epg + ee
        b0 = bstart_ref[e]
        nb = nbe_ref[e]

        @pl.when(nb > 0)
        def _():
            def prefetch(t):
                @pl.when(t < nused)
                def _():
                    x_copy(t, jnp.bitwise_and(t, ring - 1)).start()

            def process(b, n, fresh):
                slots = [jnp.bitwise_and(b + j, ring - 1) for j in range(n)]
                for j in range(n):
                    x_copy(b + j, slots[j]).wait()
                prefetch(b + ring - 1)
                for j in range(n):
                    @pl.when(b + j >= ring)
                    def _():
                        y_copy(b + j - ring, slots[j]).wait()
                rows = lax.broadcasted_iota(I32, (n * bm, 1), 0)
                valid = cnt_ref[e] - (b - b0) * bm
                xp = jnp.concatenate([xbuf[s] for s in slots], axis=0)
                xb = jnp.concatenate(_unpack_bf16_pairs(jnp.where(rows < valid, xp, jnp.uint32(0))), axis=1)
                if fresh:
                    wg, wu, wd = (w[ee].astype(BF16) for w in (wg_ref, wu_ref, wd_ref))
                    wgb[...], wub[...], wdb[...] = wg, wu, wd
                else:
                    wg, wu, wd = wgb[...], wub[...], wdb[...]
                a = jnp.dot(xb, wg, preferred_element_type=F32)
                g = jnp.dot(xb, wu, preferred_element_type=F32)
                act = (a * _sigmoid(a)) * g
                yp = _pack_bf16_pairs(jnp.dot(act.astype(BF16), wd, preferred_element_type=F32))
                for j in range(n):
                    ybuf[slots[j]] = yp[j * bm:(j + 1) * bm, :]
                    y_copy(b + j, slots[j]).start()
                for j in range(1, n):
                    prefetch(b + ring - 1 + j)

            quads = lax.shift_right_logical(nb, 2)
            lax.fori_loop(0, quads, lambda j, c: (process(b0 + 4 * j, 4, True), c)[1], 0)
            rest2 = jnp.bitwise_and(nb, 2)
            rest1 = jnp.bitwise_and(nb, 1)
            for n, rest, first, have_copy in ((2, rest2, b0 + 4 * quads, quads > 0),
                                              (1, rest1, b0 + 4 * quads + rest2, nb > 1)):
                @pl.when((rest != 0) & have_copy)
                def _():
                    process(first, n, False)

                @pl.when((rest != 0) & jnp.logical_not(have_copy))
                def _():
                    process(first, n, True)

    for ee in range(epg):
        run_expert(ee)

    @pl.when(step == last)
    def _():
        for back in range(ring, 0, -1):
            @pl.when(nused >= back)
            def _():
                y_copy(nused - back, jnp.bitwise_and(nused - back, ring - 1)).wait()
        ybuf[0] = jnp.zeros(ybuf.shape[1:], ybuf.dtype)
        lax.fori_loop(nused, nblk, lambda b, c: (y_copy(b, 0).start(), c)[1], 0)
        lax.fori_loop(nused, nblk, lambda b, c: (y_copy(b, 0).wait(), c)[1], 0)


def _gmm(bstart, nb_e, counts, nused, xs, w_gate, w_up, w_down):
    P, W = xs.shape
    E, D, F = w_gate.shape
    bm = GMM_BLOCK
    epg = GMM_EXPERTS_PER_STEP
    wsel = lambda s, *_: (s, 0, 0)
    grid_spec = pltpu.PrefetchScalarGridSpec(
        num_scalar_prefetch=4,
        grid=(E // epg,),
        in_specs=[pl.BlockSpec((epg, D, F), wsel), pl.BlockSpec((epg, D, F), wsel), pl.BlockSpec((epg, F, D), wsel),
                  pl.BlockSpec(memory_space=pl.ANY)],
        out_specs=pl.BlockSpec(memory_space=pl.ANY),
        scratch_shapes=[pltpu.VMEM((D, F), BF16), pltpu.VMEM((D, F), BF16), pltpu.VMEM((F, D), BF16),
                        pltpu.VMEM((GMM_RING, bm, W), xs.dtype), pltpu.VMEM((GMM_RING, bm, W), xs.dtype),
                        pltpu.SemaphoreType.DMA((GMM_RING,)), pltpu.SemaphoreType.DMA((GMM_RING,))],
    )
    return pl.pallas_call(
        _gmm_kernel,
        grid_spec=grid_spec,
        out_shape=jax.ShapeDtypeStruct((P, W), xs.dtype),
        compiler_params=pltpu.CompilerParams(dimension_semantics=("arbitrary",)),
        name="gmm",
    )(bstart, nb_e, counts, nused, w_gate, w_up, w_down, xs)


def _sc_gather(dest_c, ys):
    nch, K, C = dest_c.shape
    W = ys.shape[1]
    H = SC_GATHER_ROWS
    info = plsc.get_sparse_core_info()
    nw = info.num_cores * info.num_subcores
    assert nch % nw == 0, "token chunks must split evenly over the vector subcores"
    per_w = nch // nw
    nbuf = SC_GATHER_BUFS
    ahead = nbuf - 1
    mesh = plsc.VectorSubcoreMesh(core_axis_name="c", subcore_axis_name="s")
    items = [(kk, hh) for kk in range(K) for hh in range(C // H)]

    @functools.partial(
        pl.kernel, mesh=mesh,
        out_type=jax.ShapeDtypeStruct((K, nch * C, W), ys.dtype),
        scratch_types=([pltpu.VMEM((K, C), I32)] + [pltpu.VMEM((H, W), ys.dtype)] * nbuf
                       + [pltpu.SemaphoreType.DMA] * (2 * nbuf)),
        name="sc_gather",
    )
    def k(dest_hbm, ys_hbm, yk_hbm, idx_v, *rest):
        bufs, gsem, wsem = rest[:nbuf], rest[nbuf:2 * nbuf], rest[2 * nbuf:]
        wid = lax.axis_index("s") * info.num_cores + lax.axis_index("c")

        @pl.loop(0, per_w)
        def _(j):
            ch = wid * per_w + j
            pltpu.sync_copy(dest_hbm.at[ch], idx_v)

            def gather(i):
                kk, hh = items[i]
                return pltpu.async_copy(ys_hbm.at[idx_v.at[kk, pl.ds(hh * H, H)]], bufs[i % nbuf], gsem[i % nbuf])

            def write(i):
                kk, hh = items[i]
                return pltpu.async_copy(bufs[i % nbuf], yk_hbm.at[kk, pl.ds(ch * C + hh * H, H)], wsem[i % nbuf])

            n = len(items)
            g = {i: gather(i) for i in range(ahead)}
            w = {}
            for i in range(n):
                g[i].wait()
                w[i] = write(i)
                if i + ahead < n:
                    if i >= 1:
                        w.pop(i - 1).wait()
                    g[i + ahead] = gather(i + ahead)
            for i in sorted(w):
                w[i].wait()

    return k(dest_c, ys)


def _combine_kernel(yk_ref, g_ref, gate2_ref, xacc_ref, gfin_ref, o_ref, gpad_ref):
    @pl.when(pl.program_id(0) == 0)
    def _():
        gpad_ref[...] = jnp.zeros_like(gpad_ref)

    gpad_ref[0:TOP_K, :] = g_ref[...]
    gt = gpad_ref[...].T
    hi_mask = jnp.uint32(0xFFFF0000)
    acc_lo = acc_hi = None
    for k in range(TOP_K):
        p = yk_ref[k]
        g = gt[:, k:k + 1]
        lo = lax.bitcast_convert_type(p << 16, F32) * g
        hi = lax.bitcast_convert_type(p & hi_mask, F32) * g
        acc_lo = lo if k == 0 else acc_lo + lo
        acc_hi = hi if k == 0 else acc_hi + hi
    routed = jnp.concatenate([acc_lo, acc_hi], axis=1)
    x2 = xacc_ref[...] + gate2_ref[0] * routed
    o_ref[...] = _rms(x2) * gfin_ref[...]


def _combine(yk, gates, gate2, xacc, g_final, S):
    N, D = xacc.shape
    W = yk.shape[2]
    T = COMBINE_TILE
    spt = S // T
    return pl.pallas_call(
        _combine_kernel,
        grid=(N // T,),
        in_specs=[pl.BlockSpec((TOP_K, T, W), lambda i: (0, i, 0)),
                  pl.BlockSpec((TOP_K, T), lambda i: (0, i)),
                  pl.BlockSpec((1, 1, D), lambda i: (i // spt, 0, 0)),
                  pl.BlockSpec((T, D), lambda i: (i, 0)),
                  pl.BlockSpec((1, D), lambda i: (0, 0))],
        out_specs=pl.BlockSpec((T, D), lambda i: (i, 0)),
        out_shape=jax.ShapeDtypeStruct((N, D), F32),
        scratch_shapes=[pltpu.VMEM((LANES, T), F32)],
        compiler_params=pltpu.CompilerParams(dimension_semantics=("arbitrary",)),
        name="combine",
    )(yk, gates, gate2, xacc, g_final)


def _layer(xf, B, S, mod, g_mix, w_in, pool_w, pool_scale, w_out, g_ffn, w_router, router_bias,
           w_gate, w_up, w_down, ws_gate, ws_up, ws_down):
    N, D = xf.shape
    E = w_router.shape[1]
    pw = pool_scale.shape[0]
    shift1, scale1, gate1, shift2, scale2, gate2 = [m.reshape(B, 1, D) for m in jnp.split(mod, 6, axis=-1)]
    u, qkv = _in_proj(xf, g_mix.reshape(1, D), shift1, scale1, w_in.astype(BF16), S, pw)
    attn_outs = []
    for a, d in zip(qkv, ATT_DILATIONS):
        o, lse = _attention(a, S // d // ATT_BLOCK)
        shape = (N, ATT_GROUP_WIDTH) if d == 1 else (B * d, S // d, ATT_GROUP_WIDTH)
        attn_outs += [o.reshape(shape), lse.reshape(shape)]
    ng = pool_w.shape[0]
    pool_bd = jnp.einsum('gcd,gh->gchd', pool_w, jnp.eye(ng, dtype=pool_w.dtype)).reshape(pw, pw).astype(BF16)
    xacc, h2, eidx, gates, rank, counts = _mid(
        u, attn_outs, xf, pool_bd, pool_scale.reshape(1, pw), w_out.astype(BF16), gate1,
        g_ffn.reshape(1, D), shift2, scale2, gate2,
        ws_gate.astype(BF16), ws_up.astype(BF16), ws_down.astype(BF16),
        w_router.T.astype(BF16), router_bias.reshape(E, 1).astype(F32), S)
    bm = GMM_BLOCK
    nblk = N * TOP_K // bm + E
    nb_e = (counts[:, 0] + bm - 1) // bm
    bend = jnp.cumsum(nb_e)
    bstart = (bend - nb_e).astype(I32)
    nused = bend[-1:].astype(I32)
    xs, dest_c = _sc_dispatch(eidx, rank, bstart * bm, h2, nblk * bm)
    ys = _gmm(bstart, nb_e.astype(I32), counts[:, 0], nused, xs, w_gate, w_up, w_down)
    return _sc_gather(dest_c, ys), gates, gate2, xacc


def kernel(x, c, w_ada, b_ada, g_mix, w_in, pool_w, pool_scale, w_out, g_ffn, w_router, router_bias,
           w_gate, w_up, w_down, ws_gate, ws_up, ws_down, g_final):
    B, S, D = x.shape
    depth = w_ada.shape[0]
    assert depth == 1, "the final residual is fused with the final norm, so exactly one layer is supported"
    assert S % (ATT_DILATIONS[-1] * ATT_BLOCK) == 0
    assert all(S % t == 0 for t in (IN_TILE, MID_TILE, COMBINE_TILE, ATT_BLOCKS_PER_STEP * ATT_BLOCK))
    assert MID_TILE % ROUTE_TILE == 0 and ROUTE_TILE % SC_CHUNK == 0 and SC_CHUNK % SC_GATHER_ROWS == 0
    assert w_gate.shape[1] % GMM_EXPERTS_PER_STEP == 0 and max(POOL_WINDOWS) <= POOL_HALO
    xf = x.reshape(B * S, D)
    mod = _ada(c, w_ada[0], b_ada[0])
    yk, gates, gate2, xacc = _layer(
        xf, B, S, mod, g_mix[0], w_in[0], pool_w[0], pool_scale[0], w_out[0], g_ffn[0], w_router[0],
        router_bias[0], w_gate[0], w_up[0], w_down[0], ws_gate[0], ws_up[0], ws_down[0])
    out = _combine(yk, gates, gate2, xacc, g_final.reshape(1, D), S)
    return out.reshape(B, S, D)
```

```python
import functools

import jax
import jax.numpy as jnp
from jax import lax
from jax.experimental import pallas as pl
from jax.experimental.pallas import tpu as pltpu
from jax.experimental.pallas import tpu_sc as plsc

F32 = jnp.float32
BF16 = jnp.bfloat16
I32 = jnp.int32
U32 = jnp.uint32

LANES = 128
SINGLE_LOAD_STRIDE = 4
NORM_EPS = 1e-6
POOL_WINDOWS = (2, 4, 8, 16)
POOL_HALO = 16
ATT_DILATIONS = (1, 4, 16)
ATT_BLOCK = 128
ATT_HEADS_PER_GROUP = 4
ATT_HEAD_DIM = 64
ATT_GROUP_WIDTH = ATT_HEADS_PER_GROUP * ATT_HEAD_DIM
N_EXPERT_GROUPS = 8
TOPK_GROUPS = 4
TOP_K = 8
ROUTED_SCALE = 2.5

IN_TILE = 1024
ATT_BLOCKS_PER_STEP = 16
ATT_UNROLL = 8
MID_TILE = 512
ROUTE_TILE = 512
COMBINE_TILE = 512
GMM_BLOCK = 128
GMM_RING = 16
GMM_EXPERTS_PER_STEP = 1
SC_CHUNK = 128
SC_GATHER_ROWS = 32
SC_GATHER_BUFS = 6

NEG_INF = float("-inf")


def _sigmoid(v):
    return 1.0 / (1.0 + jnp.exp(-v))


def _rms(v):
    return v * lax.rsqrt(jnp.mean(v * v, axis=-1, keepdims=True) + NORM_EPS)


def _pack_bf16_pairs(v):
    n = v.shape[1] // 2
    lo = lax.bitcast_convert_type(v[:, :n].astype(BF16).astype(F32), U32)
    hi = lax.bitcast_convert_type(v[:, n:].astype(BF16).astype(F32), U32)
    return (hi & jnp.uint32(0xFFFF0000)) | (lo >> 16)


def _unpack_bf16_pairs(p):
    lo = lax.bitcast_convert_type(p << 16, F32).astype(BF16)
    hi = lax.bitcast_convert_type(p & jnp.uint32(0xFFFF0000), F32).astype(BF16)
    return lo, hi


def _ada_kernel(c_ref, w_ref, b_ref, o_ref):
    c = c_ref[...]
    cs = c * _sigmoid(c)
    o_ref[...] = jnp.dot(cs, w_ref[...], preferred_element_type=F32,
                         precision=lax.Precision.HIGHEST) + b_ref[...]


def _ada(c, w_ada, b_ada):
    B, D = c.shape
    W = w_ada.shape[1]
    tn = 1024
    return pl.pallas_call(
        _ada_kernel,
        grid=(W // tn,),
        in_specs=[pl.BlockSpec((B, D), lambda j: (0, 0)),
                  pl.BlockSpec((D, tn), lambda j: (0, j)),
                  pl.BlockSpec((1, tn), lambda j: (0, j))],
        out_specs=pl.BlockSpec((B, tn), lambda j: (0, j)),
        out_shape=jax.ShapeDtypeStruct((B, W), F32),
        name="ada",
    )(c, w_ada, b_ada.reshape(1, W))


def _in_kernel(x_ref, g_ref, sh_ref, sc_ref, w_ref, pool_ref, q0_ref, q1_ref, q2_ref, scr_ref, tmp_ref):
    h = _rms(x_ref[...]) * g_ref[...]
    h = h * (1.0 + sc_ref[0]) + sh_ref[0]
    hb = h.astype(BF16)
    tm = x_ref.shape[0]
    pw = pool_ref.shape[1]
    gw = ATT_GROUP_WIDTH
    pool_ref[...] = jnp.dot(hb, w_ref[:, 0:pw], preferred_element_type=F32)
    for g, (out, d) in enumerate(zip((q0_ref, q1_ref, q2_ref), ATT_DILATIONS)):
        for sec in range(3):
            c0 = pw + sec * 3 * gw + g * gw
            res = jnp.dot(hb, w_ref[:, c0:c0 + gw], preferred_element_type=F32)
            if d == 1:
                out[:, sec * gw:(sec + 1) * gw] = res.astype(BF16)
            else:
                for c in range(gw // LANES):
                    scr_ref[c] = res[:, c * LANES:(c + 1) * LANES]
                    src, f1 = scr_ref.at[c], 1
                    if d > SINGLE_LOAD_STRIDE:
                        f1 = SINGLE_LOAD_STRIDE
                        for q in range(f1):
                            tmp_ref[c, q * (tm // f1):(q + 1) * (tm // f1), :] = scr_ref[c, pl.ds(q, tm // f1, stride=f1), :]
                        src = tmp_ref.at[c]
                    for r in range(d):
                        r_lo, r_hi = r % f1, r // f1
                        c1 = sec * gw + c * LANES
                        out[r, :, c1:c1 + LANES] = src[pl.ds(r_lo * (tm // f1) + r_hi, tm // d, stride=d // f1),
                                                       :].astype(BF16)


def _in_proj(xf, g_mix, shift1, scale1, w_in_b, S, pool_width):
    N, D = xf.shape
    tm = IN_TILE
    spt = S // tm
    vec = lambda i: (i // spt, 0, 0)
    row = lambda i: (i, 0)
    gw3 = 3 * ATT_GROUP_WIDTH
    B = N // S
    res_spec = lambda d: pl.BlockSpec((d, tm // d, gw3), lambda i: (i // spt, i % spt, 0))
    res_shape = lambda d: jax.ShapeDtypeStruct((B * d, S // d, gw3), BF16)
    outs = pl.pallas_call(
        _in_kernel,
        grid=(N // tm,),
        in_specs=[pl.BlockSpec((tm, D), row),
                  pl.BlockSpec((1, D), lambda i: (0, 0)),
                  pl.BlockSpec((1, 1, D), vec),
                  pl.BlockSpec((1, 1, D), vec),
                  pl.BlockSpec(w_in_b.shape, lambda i: (0, 0))],
        out_specs=[pl.BlockSpec((tm, pool_width), row), pl.BlockSpec((tm, gw3), row)]
                  + [res_spec(d) for d in ATT_DILATIONS[1:]],
        out_shape=[jax.ShapeDtypeStruct((N, pool_width), F32), jax.ShapeDtypeStruct((N, gw3), BF16)]
                  + [res_shape(d) for d in ATT_DILATIONS[1:]],
        scratch_shapes=[pltpu.VMEM((ATT_GROUP_WIDTH // LANES, tm, LANES), F32)] * 2,
        name="in_proj",
    )(xf, g_mix, shift1, scale1, w_in_b)
    return outs[0], [o.reshape(N, gw3) for o in outs[1:]]


def _attn_kernel(nbs, a_ref, halo_ref, o_ref, lse_ref, kv_ref, band_ref):
    i = pl.program_id(0)
    R = a_ref.shape[0] // ATT_BLOCK
    gw = ATT_GROUP_WIDTH
    blk = ATT_BLOCK
    nh = ATT_HEADS_PER_GROUP
    kv_ref[0:blk, :] = halo_ref[:, gw:3 * gw]
    kv_ref[blk:, :] = a_ref[:, gw:3 * gw]
    row = lax.broadcasted_iota(I32, (nh * blk, 2 * blk), 0) % blk
    col = lax.broadcasted_iota(I32, (nh * blk, 2 * blk), 1)
    in_band = (col >= row) & (col <= row + blk)
    band_ref[0] = jnp.where(in_band, 0.0, NEG_INF)
    band_ref[1] = jnp.where(in_band & (col >= blk), 0.0, NEG_INF)
    head_of_lane = lax.broadcasted_iota(I32, (blk, gw), 1) // ATT_HEAD_DIM
    nt = (((1,), (1,)), ((), ()))

    def one_block(jj, start):
        r0 = pl.multiple_of(jj * blk, blk)
        qf = a_ref[pl.ds(r0, blk), 0:gw].astype(F32) * (ATT_HEAD_DIM ** -0.5)
        q4 = jnp.concatenate([jnp.where(head_of_lane == h, qf, 0.0) for h in range(nh)], axis=0).astype(BF16)
        kc = kv_ref[pl.ds(r0, 2 * blk), 0:gw]
        vc = kv_ref[pl.ds(r0, 2 * blk), gw:2 * gw]
        s = lax.dot_general(q4, kc, nt, preferred_element_type=F32) + band_ref[1 if start is True else 0]
        if start is not None and start is not True:
            s = jnp.where(col >= jnp.where(start, blk, 0), s, NEG_INF)
        m = jnp.max(s, axis=1, keepdims=True)
        p = jnp.exp(s - m)
        l = jnp.sum(p, axis=1, keepdims=True)
        o4 = jnp.dot(p.astype(BF16), vc, preferred_element_type=F32) / l
        lse4 = m + jnp.log(l)
        o = jnp.zeros((blk, gw), F32)
        lse = jnp.zeros((blk, gw), F32)
        for h in range(nh):
            hm = head_of_lane == h
            o = jnp.where(hm, o4[h * blk:(h + 1) * blk, :], o)
            lse = jnp.where(hm, lse4[h * blk:(h + 1) * blk, :], lse)
        o_ref[pl.ds(r0, blk), :] = o
        lse_ref[pl.ds(r0, blk), :] = lse

    U = ATT_UNROLL
    assert U % nbs == 0 or nbs % U == 0

    def body(it, carry):
        for j in range(U):
            if U % nbs == 0:
                start = True if j % nbs == 0 else None
            else:
                start = (((i * R + it * U) % nbs) == 0) if j == 0 else None
            one_block(it * U + j, start)
        return carry

    lax.fori_loop(0, R // U, body, 0)


def _attention(a, nbs):
    N = a.shape[0]
    R = ATT_BLOCKS_PER_STEP
    gw = ATT_GROUP_WIDTH
    tm = R * ATT_BLOCK
    return pl.pallas_call(
        functools.partial(_attn_kernel, nbs),
        grid=(N // tm,),
        in_specs=[pl.BlockSpec((tm, 3 * gw), lambda i: (i, 0)),
                  pl.BlockSpec((ATT_BLOCK, 3 * gw), lambda i: (jnp.maximum(i * R - 1, 0), 0))],
        out_specs=[pl.BlockSpec((tm, gw), lambda i: (i, 0))] * 2,
        out_shape=[jax.ShapeDtypeStruct((N, gw), F32)] * 2,
        scratch_shapes=[pltpu.VMEM((tm + ATT_BLOCK, 2 * gw), BF16),
                        pltpu.VMEM((2, ATT_HEADS_PER_GROUP * ATT_BLOCK, 2 * ATT_BLOCK), F32)],
        name="attn",
    )(a, a)


def _mid_kernel(spt, u_ref, uh_ref, o0_ref, l0_ref, o1_ref, l1_ref, o2_ref, l2_ref, x_ref,
                pbd_ref, psc_ref, wout_ref, gate1_ref, gffn_ref, sh2_ref, sc2_ref, gate2_ref,
                wsg_ref, wsu_ref, wsd_ref, wr_ref, bias_ref, before_ref,
                xacc_ref, h2_ref, e_ref, g_ref, r_ref, cnt_ref, til_ref, tmp_ref, base_ref):
    i = pl.program_id(0)
    tm, pw = u_ref.shape
    si = i % spt
    u = u_ref[...]
    keep = jnp.full((POOL_HALO, pw), si, I32) > 0
    ext = jnp.concatenate([jnp.where(keep, uh_ref[...], 0.0), u], axis=0)
    lane_grp = lax.broadcasted_iota(I32, (tm, pw), 1) // (pw // len(POOL_WINDOWS))
    pooled = jnp.zeros((tm, pw), F32)
    s, w = ext, 1
    while w < POOL_HALO:
        s = s + pltpu.roll(s, w, axis=0)
        w *= 2
        if w in POOL_WINDOWS:
            pooled = jnp.where(lane_grp == POOL_WINDOWS.index(w), s[POOL_HALO:, :], pooled)
    win = jnp.zeros((tm, pw), I32)
    for g, w in enumerate(POOL_WINDOWS):
        win = jnp.where(lane_grp == g, w, win)
    pos = si * tm + lax.broadcasted_iota(I32, (tm, pw), 0)
    cnt = jnp.minimum(pos + 1, win).astype(F32)
    pooled = pooled / cnt - u
    pool_out = jnp.dot(pooled.astype(BF16), pbd_ref[...], preferred_element_type=F32) * psc_ref[...]
    def token_order(slot, ref):
        d, n, w = ref.shape
        f1 = SINGLE_LOAD_STRIDE if d > SINGLE_LOAD_STRIDE else 1
        for c in range(w // LANES):
            cols = slice(c * LANES, (c + 1) * LANES)
            if f1 == 1:
                for r in range(d):
                    til_ref[slot, c, pl.ds(r, n, stride=d), :] = ref[r, :, cols]
            else:
                f2, slab = d // f1, tm // f1
                for r in range(d):
                    r_lo, r_hi = r % f1, r // f1
                    tmp_ref[c, pl.ds(r_lo * slab + r_hi, n, stride=f2), :] = ref[r, :, cols]
                for r_lo in range(f1):
                    til_ref[slot, c, pl.ds(r_lo, slab, stride=f1), :] = tmp_ref[c, r_lo * slab:(r_lo + 1) * slab, :]
        return jnp.concatenate([til_ref[slot, c] for c in range(w // LANES)], axis=1)

    l0 = l0_ref[...]
    l1 = token_order(0, l1_ref)
    l2 = token_order(1, l2_ref)
    m = jnp.maximum(jnp.maximum(l0, l1), l2)
    w0 = jnp.exp(l0 - m)
    w1 = jnp.exp(l1 - m)
    w2 = jnp.exp(l2 - m)
    attn = (w0 * o0_ref[...] + w1 * token_order(2, o1_ref) + w2 * token_order(3, o2_ref)) / (w0 + w1 + w2)
    mixed = jnp.dot(jnp.concatenate([pool_out.astype(BF16), attn.astype(BF16)], axis=1), wout_ref[...],
                    preferred_element_type=F32)
    x1 = x_ref[...] + gate1_ref[0] * mixed
    h2 = _rms(x1) * gffn_ref[...]
    h2 = h2 * (1.0 + sc2_ref[0]) + sh2_ref[0]
    h2_ref[...] = _pack_bf16_pairs(h2)
    hb = h2.astype(BF16)
    a = jnp.dot(hb, wsg_ref[...], preferred_element_type=F32)
    b = jnp.dot(hb, wsu_ref[...], preferred_element_type=F32)
    act = (a * _sigmoid(a)) * b
    shared = jnp.dot(act.astype(BF16), wsd_ref[...], preferred_element_type=F32)
    xacc_ref[...] = x1 + gate2_ref[0] * shared
    @pl.when(i == 0)
    def _():
        base_ref[...] = jnp.zeros_like(base_ref)

    C = e_ref.shape[2]
    for t0 in range(0, tm, ROUTE_TILE):
        idxs, gates, ranks = _route_tile(hb[t0:t0 + ROUTE_TILE, :], wr_ref, bias_ref, before_ref, base_ref)
        for k in range(TOP_K):
            g_ref[k:k + 1, t0:t0 + ROUTE_TILE] = gates[k]
            for c in range(ROUTE_TILE // C):
                e_ref[t0 // C + c, k:k + 1, :] = idxs[k][:, c * C:(c + 1) * C]
                r_ref[t0 // C + c, k:k + 1, :] = ranks[k][:, c * C:(c + 1) * C]
    cnt_ref[...] = base_ref[...].astype(I32)


def _mid(u, attn_outs, xf, pool_bd, pool_scale, w_out_b, gate1, g_ffn, shift2, scale2, gate2,
         wsg_b, wsu_b, wsd_b, wr_t, bias_col, S):
    N, D = xf.shape
    E = wr_t.shape[0]
    tok = jnp.arange(ROUTE_TILE, dtype=I32)
    before = (tok[:, None] < tok[None, :]).astype(BF16)
    pw = u.shape[1]
    tm = MID_TILE
    spt = S // tm
    row = lambda i: (i, 0)
    vec = lambda i: (i // spt, 0, 0)
    full = lambda a: pl.BlockSpec(a.shape, lambda i: (0,) * a.ndim)
    hpt = tm // POOL_HALO
    in_specs = [pl.BlockSpec((tm, pw), row),
                pl.BlockSpec((POOL_HALO, pw), lambda i: (jnp.maximum(i * hpt - 1, 0), 0))]
    gw = ATT_GROUP_WIDTH
    in_specs += [pl.BlockSpec((tm, gw), row)] * 2
    for d in ATT_DILATIONS[1:]:
        in_specs += [pl.BlockSpec((d, tm // d, gw), lambda i: (i // spt, i % spt, 0))] * 2
    in_specs += [pl.BlockSpec((tm, D), row), full(pool_bd), full(pool_scale), full(w_out_b),
                 pl.BlockSpec((1, 1, D), vec), full(g_ffn), pl.BlockSpec((1, 1, D), vec),
                 pl.BlockSpec((1, 1, D), vec), pl.BlockSpec((1, 1, D), vec),
                 full(wsg_b), full(wsu_b), full(wsd_b), full(wr_t), full(bias_col), full(before)]
    col = lambda i: (0, i)
    C = SC_CHUNK
    chunked = pl.BlockSpec((tm // C, TOP_K, C), lambda i: (i, 0, 0))
    return pl.pallas_call(
        functools.partial(_mid_kernel, spt),
        grid=(N // tm,),
        in_specs=in_specs,
        out_specs=[pl.BlockSpec((tm, D), row), pl.BlockSpec((tm, D // 2), row),
                   chunked, pl.BlockSpec((TOP_K, tm), col), chunked,
                   pl.BlockSpec((E, 1), lambda i: (0, 0))],
        out_shape=[jax.ShapeDtypeStruct((N, D), F32), jax.ShapeDtypeStruct((N, D // 2), U32),
                   jax.ShapeDtypeStruct((N // C, TOP_K, C), I32), jax.ShapeDtypeStruct((TOP_K, N), F32),
                   jax.ShapeDtypeStruct((N // C, TOP_K, C), I32), jax.ShapeDtypeStruct((E, 1), I32)],
        scratch_shapes=[pltpu.VMEM((4, gw // LANES, tm, LANES), F32), pltpu.VMEM((gw // LANES, tm, LANES), F32),
                        pltpu.VMEM((E, 1), F32)],
        compiler_params=pltpu.CompilerParams(dimension_semantics=("arbitrary",)),
        name="mid",
    )(u, u, *attn_outs, xf, pool_bd, pool_scale, w_out_b, gate1, g_ffn, shift2, scale2, gate2,
      wsg_b, wsu_b, wsd_b, wr_t, bias_col, before)


def _route_tile(hb, wr_ref, bias_ref, before_ref, base_ref):
    T = hb.shape[0]
    E = wr_ref.shape[0]
    gsz = E // N_EXPERT_GROUPS
    logits = lax.dot_general(wr_ref[...], hb, (((1,), (1,)), ((), ())), preferred_element_type=F32)
    scores = _sigmoid(logits)
    biased = scores + bias_ref[...]
    giota = lax.broadcasted_iota(I32, (gsz, T), 0)
    gscore = []
    for g in range(N_EXPERT_GROUPS):
        blk = biased[g * gsz:(g + 1) * gsz, :]
        m1 = jnp.max(blk, axis=0, keepdims=True)
        i1 = jnp.min(jnp.where(blk == m1, giota, gsz), axis=0, keepdims=True)
        m2 = jnp.max(jnp.where(giota == i1, NEG_INF, blk), axis=0, keepdims=True)
        gscore.append(m1 + m2)
    parts = []
    for g in range(N_EXPERT_GROUPS):
        beaten = jnp.zeros((1, T), I32)
        for o in range(N_EXPERT_GROUPS):
            if o == g:
                continue
            wins = (gscore[o] >= gscore[g]) if o < g else (gscore[o] > gscore[g])
            beaten = beaten + wins.astype(I32)
        keep = jnp.broadcast_to(beaten, (gsz, T)) < TOPK_GROUPS
        parts.append(jnp.where(keep, biased[g * gsz:(g + 1) * gsz, :], NEG_INF))
    cur = jnp.concatenate(parts, axis=0)
    eiota = lax.broadcasted_iota(I32, (E, T), 0)
    live = cur > NEG_INF
    idxs, gates, picks = [], [], []
    sub = lax.broadcasted_iota(I32, (8, T), 0)
    for k in range(TOP_K):
        nodes = [(cur[8 * j:8 * (j + 1), :], j, scores[8 * j:8 * (j + 1), :]) for j in range(E // 8)]
        while len(nodes) > 1:
            merged = []
            for p in range(0, len(nodes), 2):
                (va, ta, sa), (vb, tb, sb) = nodes[p], nodes[p + 1]
                later = vb > va
                merged.append((jnp.where(later, vb, va), jnp.where(later, tb, ta), jnp.where(later, sb, sa)))
            nodes = merged
        v8, t8, s8 = nodes[0]
        e8 = t8 * 8 + sub
        m = jnp.max(v8, axis=0, keepdims=True)
        idx = jnp.min(jnp.where(v8 == m, e8, E), axis=0, keepdims=True)
        oh = eiota == idx
        won = e8 == idx
        gates.append(jnp.sum(jnp.where(won, s8, 0.0), axis=0, keepdims=True))
        idxs.append(idx)
        picks.append((t8, won))
        cur = jnp.where(oh, NEG_INF, cur)
    selm = jnp.where(live & (cur == NEG_INF), 1.0, 0.0)
    gsum = gates[0]
    for k in range(1, TOP_K):
        gsum = gsum + gates[k]
    gates = [gk / gsum * ROUTED_SCALE for gk in gates]
    tot = jnp.dot(selm.astype(BF16), before_ref[...], preferred_element_type=F32) + base_ref[...]
    tiles = [tot[8 * j:8 * (j + 1), :] for j in range(E // 8)]
    ranks = []
    for t8, won in picks:
        level, bit = tiles, 0
        while len(level) > 1:
            odd = ((t8 >> bit) & 1) == 1
            level = [jnp.where(odd, level[p + 1], level[p]) for p in range(0, len(level), 2)]
            bit += 1
        ranks.append(jnp.sum(jnp.where(won, level[0], 0.0), axis=0, keepdims=True).astype(I32))
    base_ref[...] = base_ref[...] + jnp.sum(selm, axis=1, keepdims=True)
    return idxs, gates, ranks


def _sc_dispatch(eidx_c, rank_c, offs, h2p, P):
    N, W = h2p.shape
    nch, K, C = eidx_c.shape
    E = offs.shape[0]
    info = plsc.get_sparse_core_info()
    nw = info.num_cores * info.num_subcores
    L = info.num_lanes
    assert nch % nw == 0, "token chunks must split evenly over the vector subcores"
    per_w = nch // nw
    mesh = plsc.VectorSubcoreMesh(core_axis_name="c", subcore_axis_name="s")

    @functools.partial(
        pl.kernel, mesh=mesh,
        out_type=[jax.ShapeDtypeStruct((P, W), h2p.dtype), jax.ShapeDtypeStruct((nch, K, C), I32)],
        scratch_types=[pltpu.VMEM((E,), I32), pltpu.VMEM((K, C), I32), pltpu.VMEM((K, C), I32),
                       pltpu.VMEM((K, C), I32), pltpu.VMEM((C, W), h2p.dtype), pltpu.SemaphoreType.DMA],
        compiler_params=pltpu.CompilerParams(needs_layout_passes=False),
        name="sc_dispatch",
    )
    def k(e_hbm, r_hbm, off_hbm, h_hbm, xs_hbm, dest_hbm, off_v, e_v, r_v, idx_v, rows_v, sem):
        wid = lax.axis_index("s") * info.num_cores + lax.axis_index("c")
        pltpu.sync_copy(off_hbm, off_v)

        @pl.loop(0, per_w)
        def _(j):
            ch = wid * per_w + j
            pltpu.sync_copy(e_hbm.at[ch], e_v)
            pltpu.sync_copy(r_hbm.at[ch], r_v)
            for kk in range(K):
                for q in range(C // L):
                    sl = pl.ds(q * L, L)
                    idx_v[kk, sl] = plsc.load_gather(off_v, [e_v[kk, sl]]) + r_v[kk, sl]
            pltpu.sync_copy(idx_v, dest_hbm.at[ch])
            pltpu.sync_copy(h_hbm.at[pl.ds(ch * C, C)], rows_v)
            copies = [pltpu.async_copy(rows_v, xs_hbm.at[idx_v.at[kk]], sem) for kk in range(K)]
            for cp in copies:
                cp.wait()

    return k(eidx_c, rank_c, offs, h2p)


def _gmm_kernel(bstart_ref, nbe_ref, cnt_ref, nu_ref, wg_ref, wu_ref, wd_ref, xs_hbm, ys_hbm,
                wgb, wub, wdb, xbuf, ybuf, xsem, ysem):
    step = pl.program_id(0)
    last = pl.num_programs(0) - 1
    epg = wg_ref.shape[0]
    ring, bm = xbuf.shape[0], xbuf.shape[1]
    nblk = ys_hbm.shape[0] // bm
    nused = nu_ref[0]

    def x_copy(b, slot):
        return pltpu.make_async_copy(xs_hbm.at[pl.ds(pl.multiple_of(b * bm, bm), bm), :], xbuf.at[slot],
                                     xsem.at[slot])

    def y_copy(b, slot):
        return pltpu.make_async_copy(ybuf.at[slot], ys_hbm.at[pl.ds(pl.multiple_of(b * bm, bm), bm), :],
                                     ysem.at[slot])

    @pl.when(step == 0)
    def _():
        for j in range(ring - 1):
            @pl.when(j < nused)
            def _():
                x_copy(j, j).start()

    def run_expert(ee):
        e = step * epg + ee
        b0 = bstart_ref[e]
        nb = nbe_ref[e]

        @pl.when(nb > 0)
        def _():
            def prefetch(t):
                @pl.when(t < nused)
                def _():
                    x_copy(t, jnp.bitwise_and(t, ring - 1)).start()

            def process(b, n, fresh):
                slots = [jnp.bitwise_and(b + j, ring - 1) for j in range(n)]
                for j in range(n):
                    x_copy(b + j, slots[j]).wait()
                prefetch(b + ring - 1)
                for j in range(n):
                    @pl.when(b + j >= ring)
                    def _():
                        y_copy(b + j - ring, slots[j]).wait()
                rows = lax.broadcasted_iota(I32, (n * bm, 1), 0)
                valid = cnt_ref[e] - (b - b0) * bm
                xp = jnp.concatenate([xbuf[s] for s in slots], axis=0)
                xb = jnp.concatenate(_unpack_bf16_pairs(jnp.where(rows < valid, xp, jnp.uint32(0))), axis=1)
                if fresh:
                    wg, wu, wd = (w[ee].astype(BF16) for w in (wg_ref, wu_ref, wd_ref))
                    wgb[...], wub[...], wdb[...] = wg, wu, wd
                else:
                    wg, wu, wd = wgb[...], wub[...], wdb[...]
                a = jnp.dot(xb, wg, preferred_element_type=F32)
                g = jnp.dot(xb, wu, preferred_element_type=F32)
                act = (a * _sigmoid(a)) * g
                yp = _pack_bf16_pairs(jnp.dot(act.astype(BF16), wd, preferred_element_type=F32))
                for j in range(n):
                    ybuf[slots[j]] = yp[j * bm:(j + 1) * bm, :]
                    y_copy(b + j, slots[j]).start()
                for j in range(1, n):
                    prefetch(b + ring - 1 + j)

            quads = lax.shift_right_logical(nb, 2)
            lax.fori_loop(0, quads, lambda j, c: (process(b0 + 4 * j, 4, True), c)[1], 0)
            rest2 = jnp.bitwise_and(nb, 2)
            rest1 = jnp.bitwise_and(nb, 1)
            for n, rest, first, have_copy in ((2, rest2, b0 + 4 * quads, quads > 0),
                                              (1, rest1, b0 + 4 * quads + rest2, nb > 1)):
                @pl.when((rest != 0) & have_copy)
                def _():
                    process(first, n, False)

                @pl.when((rest != 0) & jnp.logical_not(have_copy))
                def _():
                    process(first, n, True)

    for ee in range(epg):
        run_expert(ee)

    @pl.when(step == last)
    def _():
        for back in range(ring, 0, -1):
            @pl.when(nused >= back)
            def _():
                y_copy(nused - back, jnp.bitwise_and(nused - back, ring - 1)).wait()
        ybuf[0] = jnp.zeros(ybuf.shape[1:], ybuf.dtype)
        lax.fori_loop(nused, nblk, lambda b, c: (y_copy(b, 0).start(), c)[1], 0)
        lax.fori_loop(nused, nblk, lambda b, c: (y_copy(b, 0).wait(), c)[1], 0)


def _gmm(bstart, nb_e, counts, nused, xs, w_gate, w_up, w_down):
    P, W = xs.shape
    E, D, F = w_gate.shape
    bm = GMM_BLOCK
    epg = GMM_EXPERTS_PER_STEP
    wsel = lambda s, *_: (s, 0, 0)
    grid_spec = pltpu.PrefetchScalarGridSpec(
        num_scalar_prefetch=4,
        grid=(E // epg,),
        in_specs=[pl.BlockSpec((epg, D, F), wsel), pl.BlockSpec((epg, D, F), wsel), pl.BlockSpec((epg, F, D), wsel),
                  pl.BlockSpec(memory_space=pl.ANY)],
        out_specs=pl.BlockSpec(memory_space=pl.ANY),
        scratch_shapes=[pltpu.VMEM((D, F), BF16), pltpu.VMEM((D, F), BF16), pltpu.VMEM((F, D), BF16),
                        pltpu.VMEM((GMM_RING, bm, W), xs.dtype), pltpu.VMEM((GMM_RING, bm, W), xs.dtype),
                        pltpu.SemaphoreType.DMA((GMM_RING,)), pltpu.SemaphoreType.DMA((GMM_RING,))],
    )
    return pl.pallas_call(
        _gmm_kernel,
        grid_spec=grid_spec,
        out_shape=jax.ShapeDtypeStruct((P, W), xs.dtype),
        compiler_params=pltpu.CompilerParams(dimension_semantics=("arbitrary",)),
        name="gmm",
    )(bstart, nb_e, counts, nused, w_gate, w_up, w_down, xs)


def _sc_gather(dest_c, ys):
    nch, K, C = dest_c.shape
    W = ys.shape[1]
    H = SC_GATHER_ROWS
    info = plsc.get_sparse_core_info()
    nw = info.num_cores * info.num_subcores
    assert nch % nw == 0, "token chunks must split evenly over the vector subcores"
    per_w = nch // nw
    nbuf = SC_GATHER_BUFS
    ahead = nbuf - 1
    mesh = plsc.VectorSubcoreMesh(core_axis_name="c", subcore_axis_name="s")
    items = [(kk, hh) for kk in range(K) for hh in range(C // H)]

    @functools.partial(
        pl.kernel, mesh=mesh,
        out_type=jax.ShapeDtypeStruct((K, nch * C, W), ys.dtype),
        scratch_types=([pltpu.VMEM((per_w, K, C), I32)] + [pltpu.VMEM((H, W), ys.dtype)] * nbuf
                       + [pltpu.SemaphoreType.DMA] * (2 * nbuf)),
        name="sc_gather",
    )
    def k(dest_hbm, ys_hbm, yk_hbm, idx_v, *rest):
        bufs, gsem, wsem = rest[:nbuf], rest[nbuf:2 * nbuf], rest[2 * nbuf:]
        wid = lax.axis_index("s") * info.num_cores + lax.axis_index("c")
        pltpu.sync_copy(dest_hbm.at[pl.ds(wid * per_w, per_w)], idx_v)

        @pl.loop(0, per_w)
        def _(j):
            ch = wid * per_w + j

            def gather(i):
                kk, hh = items[i]
                return pltpu.async_copy(ys_hbm.at[idx_v.at[j, kk, pl.ds(hh * H, H)]], bufs[i % nbuf],
                                        gsem[i % nbuf])

            def write(i):
                kk, hh = items[i]
                return pltpu.async_copy(bufs[i % nbuf], yk_hbm.at[kk, pl.ds(ch * C + hh * H, H)], wsem[i % nbuf])

            n = len(items)
            g = {i: gather(i) for i in range(ahead)}
            w = {}
            for i in range(n):
                g[i].wait()
                w[i] = write(i)
                if i + ahead < n:
                    if i >= 1:
                        w.pop(i - 1).wait()
                    g[i + ahead] = gather(i + ahead)
            for i in sorted(w):
                w[i].wait()

    return k(dest_c, ys)


def _combine_kernel(yk_ref, g_ref, gate2_ref, xacc_ref, gfin_ref, o_ref, gpad_ref):
    @pl.when(pl.program_id(0) == 0)
    def _():
        gpad_ref[...] = jnp.zeros_like(gpad_ref)

    gpad_ref[0:TOP_K, :] = g_ref[...]
    gt = gpad_ref[...].T
    hi_mask = jnp.uint32(0xFFFF0000)
    acc_lo = acc_hi = None
    for k in range(TOP_K):
        p = yk_ref[k]
        g = gt[:, k:k + 1]
        lo = lax.bitcast_convert_type(p << 16, F32) * g
        hi = lax.bitcast_convert_type(p & hi_mask, F32) * g
        acc_lo = lo if k == 0 else acc_lo + lo
        acc_hi = hi if k == 0 else acc_hi + hi
    routed = jnp.concatenate([acc_lo, acc_hi], axis=1)
    x2 = xacc_ref[...] + gate2_ref[0] * routed
    o_ref[...] = _rms(x2) * gfin_ref[...]


def _combine(yk, gates, gate2, xacc, g_final, S):
    N, D = xacc.shape
    W = yk.shape[2]
    T = COMBINE_TILE
    spt = S // T
    return pl.pallas_call(
        _combine_kernel,
        grid=(N // T,),
        in_specs=[pl.BlockSpec((TOP_K, T, W), lambda i: (0, i, 0)),
                  pl.BlockSpec((TOP_K, T), lambda i: (0, i)),
                  pl.BlockSpec((1, 1, D), lambda i: (i // spt, 0, 0)),
                  pl.BlockSpec((T, D), lambda i: (i, 0)),
                  pl.BlockSpec((1, D), lambda i: (0, 0))],
        out_specs=pl.BlockSpec((T, D), lambda i: (i, 0)),
        out_shape=jax.ShapeDtypeStruct((N, D), F32),
        scratch_shapes=[pltpu.VMEM((LANES, T), F32)],
        compiler_params=pltpu.CompilerParams(dimension_semantics=("arbitrary",)),
        name="combine",
    )(yk, gates, gate2, xacc, g_final)


def _layer(xf, B, S, mod, g_mix, w_in, pool_w, pool_scale, w_out, g_ffn, w_router, router_bias,
           w_gate, w_up, w_down, ws_gate, ws_up, ws_down):
    N, D = xf.shape
    E = w_router.shape[1]
    pw = pool_scale.shape[0]
    shift1, scale1, gate1, shift2, scale2, gate2 = [m.reshape(B, 1, D) for m in jnp.split(mod, 6, axis=-1)]
    u, qkv = _in_proj(xf, g_mix.reshape(1, D), shift1, scale1, w_in.astype(BF16), S, pw)
    attn_outs = []
    for a, d in zip(qkv, ATT_DILATIONS):
        o, lse = _attention(a, S // d // ATT_BLOCK)
        shape = (N, ATT_GROUP_WIDTH) if d == 1 else (B * d, S // d, ATT_GROUP_WIDTH)
        attn_outs += [o.reshape(shape), lse.reshape(shape)]
    ng = pool_w.shape[0]
    pool_bd = jnp.einsum('gcd,gh->gchd', pool_w, jnp.eye(ng, dtype=pool_w.dtype)).reshape(pw, pw).astype(BF16)
    xacc, h2, eidx, gates, rank, counts = _mid(
        u, attn_outs, xf, pool_bd, pool_scale.reshape(1, pw), w_out.astype(BF16), gate1,
        g_ffn.reshape(1, D), shift2, scale2, gate2,
        ws_gate.astype(BF16), ws_up.astype(BF16), ws_down.astype(BF16),
        w_router.T.astype(BF16), router_bias.reshape(E, 1).astype(F32), S)
    bm = GMM_BLOCK
    nblk = N * TOP_K // bm + E
    nb_e = (counts[:, 0] + bm - 1) // bm
    bend = jnp.cumsum(nb_e)
    bstart = (bend - nb_e).astype(I32)
    nused = bend[-1:].astype(I32)
    xs, dest_c = _sc_dispatch(eidx, rank, bstart * bm, h2, nblk * bm)
    ys = _gmm(bstart, nb_e.astype(I32), counts[:, 0], nused, xs, w_gate, w_up, w_down)
    return _sc_gather(dest_c, ys), gates, gate2, xacc


def kernel(x, c, w_ada, b_ada, g_mix, w_in, pool_w, pool_scale, w_out, g_ffn, w_router, router_bias,
           w_gate, w_up, w_down, ws_gate, ws_up, ws_down, g_final):
    B, S, D = x.shape
    depth = w_ada.shape[0]
    assert depth == 1, "the final residual is fused with the final norm, so exactly one layer is supported"
    assert S % (ATT_DILATIONS[-1] * ATT_BLOCK) == 0
    assert all(S % t == 0 for t in (IN_TILE, MID_TILE, COMBINE_TILE, ATT_BLOCKS_PER_STEP * ATT_BLOCK))
    assert MID_TILE % ROUTE_TILE == 0 and ROUTE_TILE % SC_CHUNK == 0 and SC_CHUNK % SC_GATHER_ROWS == 0
    assert w_gate.shape[1] % GMM_EXPERTS_PER_STEP == 0 and max(POOL_WINDOWS) <= POOL_HALO
    xf = x.reshape(B * S, D)
    mod = _ada(c, w_ada[0], b_ada[0])
    yk, gates, gate2, xacc = _layer(
        xf, B, S, mod, g_mix[0], w_in[0], pool_w[0], pool_scale[0], w_out[0], g_ffn[0], w_router[0],
        router_bias[0], w_gate[0], w_up[0], w_down[0], ws_gate[0], ws_up[0], ws_down[0])
    out = _combine(yk, gates, gate2, xacc, g_final.reshape(1, D), S)
    return out.reshape(B, S, D)
```

```python
import functools

import jax
import jax.numpy as jnp
from jax import lax
from jax.experimental import pallas as pl
from jax.experimental.pallas import tpu as pltpu
from jax.experimental.pallas import tpu_sc as plsc

F32 = jnp.float32
BF16 = jnp.bfloat16
I32 = jnp.int32
U32 = jnp.uint32

LANES = 128
SINGLE_LOAD_STRIDE = 4
NORM_EPS = 1e-6
POOL_WINDOWS = (2, 4, 8, 16)
POOL_HALO = 16
ATT_DILATIONS = (1, 4, 16)
ATT_BLOCK = 128
ATT_HEADS_PER_GROUP = 4
ATT_HEAD_DIM = 64
ATT_GROUP_WIDTH = ATT_HEADS_PER_GROUP * ATT_HEAD_DIM
N_EXPERT_GROUPS = 8
TOPK_GROUPS = 4
TOP_K = 8
ROUTED_SCALE = 2.5

IN_TILE = 1024
ATT_BLOCKS_PER_STEP = 16
ATT_UNROLL = 16
MID_TILE = 512
ROUTE_TILE = 512
COMBINE_TILE = 512
GMM_BLOCK = 128
GMM_RING = 16
GMM_EXPERTS_PER_STEP = 1
SC_CHUNK = 128
SC_GATHER_ROWS = 32
SC_GATHER_BUFS = 6

NEG_INF = float("-inf")


def _sigmoid(v):
    return 1.0 / (1.0 + jnp.exp(-v))


def _rms(v):
    return v * lax.rsqrt(jnp.mean(v * v, axis=-1, keepdims=True) + NORM_EPS)


def _pack_bf16_pairs(v):
    n = v.shape[1] // 2
    lo = lax.bitcast_convert_type(v[:, :n].astype(BF16).astype(F32), U32)
    hi = lax.bitcast_convert_type(v[:, n:].astype(BF16).astype(F32), U32)
    return (hi & jnp.uint32(0xFFFF0000)) | (lo >> 16)


def _unpack_bf16_pairs(p):
    lo = lax.bitcast_convert_type(p << 16, F32).astype(BF16)
    hi = lax.bitcast_convert_type(p & jnp.uint32(0xFFFF0000), F32).astype(BF16)
    return lo, hi


def _ada_kernel(c_ref, w_ref, b_ref, o_ref):
    c = c_ref[...]
    cs = c * _sigmoid(c)
    o_ref[...] = jnp.dot(cs, w_ref[...], preferred_element_type=F32,
                         precision=lax.Precision.HIGHEST) + b_ref[...]


def _ada(c, w_ada, b_ada):
    B, D = c.shape
    W = w_ada.shape[1]
    tn = 1024
    return pl.pallas_call(
        _ada_kernel,
        grid=(W // tn,),
        in_specs=[pl.BlockSpec((B, D), lambda j: (0, 0)),
                  pl.BlockSpec((D, tn), lambda j: (0, j)),
                  pl.BlockSpec((1, tn), lambda j: (0, j))],
        out_specs=pl.BlockSpec((B, tn), lambda j: (0, j)),
        out_shape=jax.ShapeDtypeStruct((B, W), F32),
        name="ada",
    )(c, w_ada, b_ada.reshape(1, W))


def _in_kernel(x_ref, g_ref, sh_ref, sc_ref, w_ref, pool_ref, q0_ref, q1_ref, q2_ref, scr_ref, tmp_ref):
    h = _rms(x_ref[...]) * g_ref[...]
    h = h * (1.0 + sc_ref[0]) + sh_ref[0]
    hb = h.astype(BF16)
    tm = x_ref.shape[0]
    pw = pool_ref.shape[1]
    gw = ATT_GROUP_WIDTH
    pool_ref[...] = jnp.dot(hb, w_ref[:, 0:pw], preferred_element_type=F32)
    for g, (out, d) in enumerate(zip((q0_ref, q1_ref, q2_ref), ATT_DILATIONS)):
        for sec in range(3):
            c0 = pw + sec * 3 * gw + g * gw
            res = jnp.dot(hb, w_ref[:, c0:c0 + gw], preferred_element_type=F32)
            if d == 1:
                out[:, sec * gw:(sec + 1) * gw] = res.astype(BF16)
            else:
                for c in range(gw // LANES):
                    scr_ref[c] = res[:, c * LANES:(c + 1) * LANES]
                    src, f1 = scr_ref.at[c], 1
                    if d > SINGLE_LOAD_STRIDE:
                        f1 = SINGLE_LOAD_STRIDE
                        for q in range(f1):
                            tmp_ref[c, q * (tm // f1):(q + 1) * (tm // f1), :] = scr_ref[c, pl.ds(q, tm // f1, stride=f1), :]
                        src = tmp_ref.at[c]
                    for r in range(d):
                        r_lo, r_hi = r % f1, r // f1
                        c1 = sec * gw + c * LANES
                        out[r, :, c1:c1 + LANES] = src[pl.ds(r_lo * (tm // f1) + r_hi, tm // d, stride=d // f1),
                                                       :].astype(BF16)


def _in_proj(xf, g_mix, shift1, scale1, w_in_b, S, pool_width):
    N, D = xf.shape
    tm = IN_TILE
    spt = S // tm
    vec = lambda i: (i // spt, 0, 0)
    row = lambda i: (i, 0)
    gw3 = 3 * ATT_GROUP_WIDTH
    B = N // S
    res_spec = lambda d: pl.BlockSpec((d, tm // d, gw3), lambda i: (i // spt, i % spt, 0))
    res_shape = lambda d: jax.ShapeDtypeStruct((B * d, S // d, gw3), BF16)
    outs = pl.pallas_call(
        _in_kernel,
        grid=(N // tm,),
        in_specs=[pl.BlockSpec((tm, D), row),
                  pl.BlockSpec((1, D), lambda i: (0, 0)),
                  pl.BlockSpec((1, 1, D), vec),
                  pl.BlockSpec((1, 1, D), vec),
                  pl.BlockSpec(w_in_b.shape, lambda i: (0, 0))],
        out_specs=[pl.BlockSpec((tm, pool_width), row), pl.BlockSpec((tm, gw3), row)]
                  + [res_spec(d) for d in ATT_DILATIONS[1:]],
        out_shape=[jax.ShapeDtypeStruct((N, pool_width), F32), jax.ShapeDtypeStruct((N, gw3), BF16)]
                  + [res_shape(d) for d in ATT_DILATIONS[1:]],
        scratch_shapes=[pltpu.VMEM((ATT_GROUP_WIDTH // LANES, tm, LANES), F32)] * 2,
        name="in_proj",
    )(xf, g_mix, shift1, scale1, w_in_b)
    return outs[0], [o.reshape(N, gw3) for o in outs[1:]]


def _attn_kernel(nbs, a_ref, halo_ref, o_ref, lse_ref, kv_ref, band_ref):
    i = pl.program_id(0)
    R = a_ref.shape[0] // ATT_BLOCK
    gw = ATT_GROUP_WIDTH
    blk = ATT_BLOCK
    nh = ATT_HEADS_PER_GROUP
    kv_ref[0:blk, :] = halo_ref[:, gw:3 * gw]
    kv_ref[blk:, :] = a_ref[:, gw:3 * gw]
    row = lax.broadcasted_iota(I32, (nh * blk, 2 * blk), 0) % blk
    col = lax.broadcasted_iota(I32, (nh * blk, 2 * blk), 1)
    in_band = (col >= row) & (col <= row + blk)
    band_ref[0] = jnp.where(in_band, 0.0, NEG_INF)
    band_ref[1] = jnp.where(in_band & (col >= blk), 0.0, NEG_INF)
    head_of_lane = lax.broadcasted_iota(I32, (blk, gw), 1) // ATT_HEAD_DIM
    nt = (((1,), (1,)), ((), ()))

    def one_block(jj, start):
        r0 = pl.multiple_of(jj * blk, blk)
        qf = a_ref[pl.ds(r0, blk), 0:gw].astype(F32) * (ATT_HEAD_DIM ** -0.5)
        q4 = jnp.concatenate([jnp.where(head_of_lane == h, qf, 0.0) for h in range(nh)], axis=0).astype(BF16)
        kc = kv_ref[pl.ds(r0, 2 * blk), 0:gw]
        vc = kv_ref[pl.ds(r0, 2 * blk), gw:2 * gw]
        s = lax.dot_general(q4, kc, nt, preferred_element_type=F32) + band_ref[1 if start is True else 0]
        if start is not None and start is not True:
            s = jnp.where(col >= jnp.where(start, blk, 0), s, NEG_INF)
        m = jnp.max(s, axis=1, keepdims=True)
        p = jnp.exp(s - m)
        l = jnp.sum(p, axis=1, keepdims=True)
        o4 = jnp.dot(p.astype(BF16), vc, preferred_element_type=F32) / l
        lse4 = m + jnp.log(l)
        o = jnp.zeros((blk, gw), F32)
        lse = jnp.zeros((blk, gw), F32)
        for h in range(nh):
            hm = head_of_lane == h
            o = jnp.where(hm, o4[h * blk:(h + 1) * blk, :], o)
            lse = jnp.where(hm, lse4[h * blk:(h + 1) * blk, :], lse)
        o_ref[pl.ds(r0, blk), :] = o
        lse_ref[pl.ds(r0, blk), :] = lse

    U = ATT_UNROLL
    assert U % nbs == 0 or nbs % U == 0

    def body(it, carry):
        for j in range(U):
            if U % nbs == 0:
                start = True if j % nbs == 0 else None
            else:
                start = (((i * R + it * U) % nbs) == 0) if j == 0 else None
            one_block(it * U + j, start)
        return carry

    lax.fori_loop(0, R // U, body, 0)


def _attention(a, nbs):
    N = a.shape[0]
    R = ATT_BLOCKS_PER_STEP
    gw = ATT_GROUP_WIDTH
    tm = R * ATT_BLOCK
    return pl.pallas_call(
        functools.partial(_attn_kernel, nbs),
        grid=(N // tm,),
        in_specs=[pl.BlockSpec((tm, 3 * gw), lambda i: (i, 0)),
                  pl.BlockSpec((ATT_BLOCK, 3 * gw), lambda i: (jnp.maximum(i * R - 1, 0), 0))],
        out_specs=[pl.BlockSpec((tm, gw), lambda i: (i, 0))] * 2,
        out_shape=[jax.ShapeDtypeStruct((N, gw), F32)] * 2,
        scratch_shapes=[pltpu.VMEM((tm + ATT_BLOCK, 2 * gw), BF16),
                        pltpu.VMEM((2, ATT_HEADS_PER_GROUP * ATT_BLOCK, 2 * ATT_BLOCK), F32)],
        name="attn",
    )(a, a)


def _mid_kernel(spt, u_ref, uh_ref, o0_ref, l0_ref, o1_ref, l1_ref, o2_ref, l2_ref, x_ref,
                pbd_ref, psc_ref, wout_ref, gate1_ref, gffn_ref, sh2_ref, sc2_ref, gate2_ref,
                wsg_ref, wsu_ref, wsd_ref, wr_ref, bias_ref, before_ref,
                xacc_ref, h2_ref, e_ref, g_ref, r_ref, cnt_ref, til_ref, tmp_ref, base_ref):
    i = pl.program_id(0)
    tm, pw = u_ref.shape
    si = i % spt
    u = u_ref[...]
    keep = jnp.full((POOL_HALO, pw), si, I32) > 0
    ext = jnp.concatenate([jnp.where(keep, uh_ref[...], 0.0), u], axis=0)
    lane_grp = lax.broadcasted_iota(I32, (tm, pw), 1) // (pw // len(POOL_WINDOWS))
    pooled = jnp.zeros((tm, pw), F32)
    s, w = ext, 1
    while w < POOL_HALO:
        s = s + pltpu.roll(s, w, axis=0)
        w *= 2
        if w in POOL_WINDOWS:
            pooled = jnp.where(lane_grp == POOL_WINDOWS.index(w), s[POOL_HALO:, :], pooled)
    win = jnp.zeros((tm, pw), I32)
    for g, w in enumerate(POOL_WINDOWS):
        win = jnp.where(lane_grp == g, w, win)
    pos = si * tm + lax.broadcasted_iota(I32, (tm, pw), 0)
    cnt = jnp.minimum(pos + 1, win).astype(F32)
    pooled = pooled / cnt - u
    pool_out = jnp.dot(pooled.astype(BF16), pbd_ref[...], preferred_element_type=F32) * psc_ref[...]
    def token_order(slot, ref):
        d, n, w = ref.shape
        f1 = SINGLE_LOAD_STRIDE if d > SINGLE_LOAD_STRIDE else 1
        for c in range(w // LANES):
            cols = slice(c * LANES, (c + 1) * LANES)
            if f1 == 1:
                for r in range(d):
                    til_ref[slot, c, pl.ds(r, n, stride=d), :] = ref[r, :, cols]
            else:
                f2, slab = d // f1, tm // f1
                for r in range(d):
                    r_lo, r_hi = r % f1, r // f1
                    tmp_ref[c, pl.ds(r_lo * slab + r_hi, n, stride=f2), :] = ref[r, :, cols]
                for r_lo in range(f1):
                    til_ref[slot, c, pl.ds(r_lo, slab, stride=f1), :] = tmp_ref[c, r_lo * slab:(r_lo + 1) * slab, :]
        return jnp.concatenate([til_ref[slot, c] for c in range(w // LANES)], axis=1)

    l0 = l0_ref[...]
    l1 = token_order(0, l1_ref)
    l2 = token_order(1, l2_ref)
    m = jnp.maximum(jnp.maximum(l0, l1), l2)
    w0 = jnp.exp(l0 - m)
    w1 = jnp.exp(l1 - m)
    w2 = jnp.exp(l2 - m)
    attn = (w0 * o0_ref[...] + w1 * token_order(2, o1_ref) + w2 * token_order(3, o2_ref)) / (w0 + w1 + w2)
    mixed = jnp.dot(jnp.concatenate([pool_out.astype(BF16), attn.astype(BF16)], axis=1), wout_ref[...],
                    preferred_element_type=F32)
    x1 = x_ref[...] + gate1_ref[0] * mixed
    h2 = _rms(x1) * gffn_ref[...]
    h2 = h2 * (1.0 + sc2_ref[0]) + sh2_ref[0]
    h2_ref[...] = _pack_bf16_pairs(h2)
    hb = h2.astype(BF16)
    a = jnp.dot(hb, wsg_ref[...], preferred_element_type=F32)
    b = jnp.dot(hb, wsu_ref[...], preferred_element_type=F32)
    act = (a * _sigmoid(a)) * b
    shared = jnp.dot(act.astype(BF16), wsd_ref[...], preferred_element_type=F32)
    xacc_ref[...] = x1 + gate2_ref[0] * shared
    @pl.when(i == 0)
    def _():
        base_ref[...] = jnp.zeros_like(base_ref)

    C = e_ref.shape[2]
    for t0 in range(0, tm, ROUTE_TILE):
        idxs, gates, ranks = _route_tile(hb[t0:t0 + ROUTE_TILE, :], wr_ref, bias_ref, before_ref, base_ref)
        for k in range(TOP_K):
            g_ref[k:k + 1, t0:t0 + ROUTE_TILE] = gates[k]
            for c in range(ROUTE_TILE // C):
                e_ref[t0 // C + c, k:k + 1, :] = idxs[k][:, c * C:(c + 1) * C]
                r_ref[t0 // C + c, k:k + 1, :] = ranks[k][:, c * C:(c + 1) * C]
    cnt_ref[...] = base_ref[...].astype(I32)


def _mid(u, attn_outs, xf, pool_bd, pool_scale, w_out_b, gate1, g_ffn, shift2, scale2, gate2,
         wsg_b, wsu_b, wsd_b, wr_t, bias_col, S):
    N, D = xf.shape
    E = wr_t.shape[0]
    tok = jnp.arange(ROUTE_TILE, dtype=I32)
    before = (tok[:, None] < tok[None, :]).astype(BF16)
    pw = u.shape[1]
    tm = MID_TILE
    spt = S // tm
    row = lambda i: (i, 0)
    vec = lambda i: (i // spt, 0, 0)
    full = lambda a: pl.BlockSpec(a.shape, lambda i: (0,) * a.ndim)
    hpt = tm // POOL_HALO
    in_specs = [pl.BlockSpec((tm, pw), row),
                pl.BlockSpec((POOL_HALO, pw), lambda i: (jnp.maximum(i * hpt - 1, 0), 0))]
    gw = ATT_GROUP_WIDTH
    in_specs += [pl.BlockSpec((tm, gw), row)] * 2
    for d in ATT_DILATIONS[1:]:
        in_specs += [pl.BlockSpec((d, tm // d, gw), lambda i: (i // spt, i % spt, 0))] * 2
    in_specs += [pl.BlockSpec((tm, D), row), full(pool_bd), full(pool_scale), full(w_out_b),
                 pl.BlockSpec((1, 1, D), vec), full(g_ffn), pl.BlockSpec((1, 1, D), vec),
                 pl.BlockSpec((1, 1, D), vec), pl.BlockSpec((1, 1, D), vec),
                 full(wsg_b), full(wsu_b), full(wsd_b), full(wr_t), full(bias_col), full(before)]
    col = lambda i: (0, i)
    C = SC_CHUNK
    chunked = pl.BlockSpec((tm // C, TOP_K, C), lambda i: (i, 0, 0))
    return pl.pallas_call(
        functools.partial(_mid_kernel, spt),
        grid=(N // tm,),
        in_specs=in_specs,
        out_specs=[pl.BlockSpec((tm, D), row), pl.BlockSpec((tm, D // 2), row),
                   chunked, pl.BlockSpec((TOP_K, tm), col), chunked,
                   pl.BlockSpec((E, 1), lambda i: (0, 0))],
        out_shape=[jax.ShapeDtypeStruct((N, D), F32), jax.ShapeDtypeStruct((N, D // 2), U32),
                   jax.ShapeDtypeStruct((N // C, TOP_K, C), I32), jax.ShapeDtypeStruct((TOP_K, N), F32),
                   jax.ShapeDtypeStruct((N // C, TOP_K, C), I32), jax.ShapeDtypeStruct((E, 1), I32)],
        scratch_shapes=[pltpu.VMEM((4, gw // LANES, tm, LANES), F32), pltpu.VMEM((gw // LANES, tm, LANES), F32),
                        pltpu.VMEM((E, 1), F32)],
        compiler_params=pltpu.CompilerParams(dimension_semantics=("arbitrary",)),
        name="mid",
    )(u, u, *attn_outs, xf, pool_bd, pool_scale, w_out_b, gate1, g_ffn, shift2, scale2, gate2,
      wsg_b, wsu_b, wsd_b, wr_t, bias_col, before)


def _route_tile(hb, wr_ref, bias_ref, before_ref, base_ref):
    T = hb.shape[0]
    E = wr_ref.shape[0]
    gsz = E // N_EXPERT_GROUPS
    logits = lax.dot_general(wr_ref[...], hb, (((1,), (1,)), ((), ())), preferred_element_type=F32)
    scores = _sigmoid(logits)
    biased = scores + bias_ref[...]
    giota = lax.broadcasted_iota(I32, (gsz, T), 0)
    gscore = []
    for g in range(N_EXPERT_GROUPS):
        blk = biased[g * gsz:(g + 1) * gsz, :]
        m1 = jnp.max(blk, axis=0, keepdims=True)
        i1 = jnp.min(jnp.where(blk == m1, giota, gsz), axis=0, keepdims=True)
        m2 = jnp.max(jnp.where(giota == i1, NEG_INF, blk), axis=0, keepdims=True)
        gscore.append(m1 + m2)
    parts = []
    for g in range(N_EXPERT_GROUPS):
        beaten = jnp.zeros((1, T), I32)
        for o in range(N_EXPERT_GROUPS):
            if o == g:
                continue
            wins = (gscore[o] >= gscore[g]) if o < g else (gscore[o] > gscore[g])
            beaten = beaten + wins.astype(I32)
        keep = jnp.broadcast_to(beaten, (gsz, T)) < TOPK_GROUPS
        parts.append(jnp.where(keep, biased[g * gsz:(g + 1) * gsz, :], NEG_INF))
    cur = jnp.concatenate(parts, axis=0)
    eiota = lax.broadcasted_iota(I32, (E, T), 0)
    live = cur > NEG_INF
    idxs, gates, picks = [], [], []
    sub = lax.broadcasted_iota(I32, (8, T), 0)
    for k in range(TOP_K):
        nodes = [(cur[8 * j:8 * (j + 1), :], j, scores[8 * j:8 * (j + 1), :]) for j in range(E // 8)]
        while len(nodes) > 1:
            merged = []
            for p in range(0, len(nodes), 2):
                (va, ta, sa), (vb, tb, sb) = nodes[p], nodes[p + 1]
                later = vb > va
                merged.append((jnp.where(later, vb, va), jnp.where(later, tb, ta), jnp.where(later, sb, sa)))
            nodes = merged
        v8, t8, s8 = nodes[0]
        e8 = t8 * 8 + sub
        m = jnp.max(v8, axis=0, keepdims=True)
        idx = jnp.min(jnp.where(v8 == m, e8, E), axis=0, keepdims=True)
        oh = eiota == idx
        won = e8 == idx
        gates.append(jnp.sum(jnp.where(won, s8, 0.0), axis=0, keepdims=True))
        idxs.append(idx)
        picks.append((t8, won))
        cur = jnp.where(oh, NEG_INF, cur)
    selm = jnp.where(live & (cur == NEG_INF), 1.0, 0.0)
    gsum = gates[0]
    for k in range(1, TOP_K):
        gsum = gsum + gates[k]
    gates = [gk / gsum * ROUTED_SCALE for gk in gates]
    tot = jnp.dot(selm.astype(BF16), before_ref[...], preferred_element_type=F32) + base_ref[...]
    tiles = [tot[8 * j:8 * (j + 1), :] for j in range(E // 8)]
    ranks = []
    for t8, won in picks:
        level, bit = tiles, 0
        while len(level) > 1:
            odd = ((t8 >> bit) & 1) == 1
            level = [jnp.where(odd, level[p + 1], level[p]) for p in range(0, len(level), 2)]
            bit += 1
        ranks.append(jnp.sum(jnp.where(won, level[0], 0.0), axis=0, keepdims=True).astype(I32))
    base_ref[...] = base_ref[...] + jnp.sum(selm, axis=1, keepdims=True)
    return idxs, gates, ranks


def _sc_dispatch(eidx_c, rank_c, offs, h2p, P):
    N, W = h2p.shape
    nch, K, C = eidx_c.shape
    E = offs.shape[0]
    info = plsc.get_sparse_core_info()
    nw = info.num_cores * info.num_subcores
    L = info.num_lanes
    assert nch % nw == 0, "token chunks must split evenly over the vector subcores"
    per_w = nch // nw
    mesh = plsc.VectorSubcoreMesh(core_axis_name="c", subcore_axis_name="s")

    @functools.partial(
        pl.kernel, mesh=mesh,
        out_type=[jax.ShapeDtypeStruct((P, W), h2p.dtype), jax.ShapeDtypeStruct((nch, K, C), I32)],
        scratch_types=[pltpu.VMEM((E,), I32), pltpu.VMEM((K, C), I32), pltpu.VMEM((K, C), I32),
                       pltpu.VMEM((K, C), I32), pltpu.VMEM((C, W), h2p.dtype), pltpu.SemaphoreType.DMA],
        compiler_params=pltpu.CompilerParams(needs_layout_passes=False),
        name="sc_dispatch",
    )
    def k(e_hbm, r_hbm, off_hbm, h_hbm, xs_hbm, dest_hbm, off_v, e_v, r_v, idx_v, rows_v, sem):
        wid = lax.axis_index("s") * info.num_cores + lax.axis_index("c")
        pltpu.sync_copy(off_hbm, off_v)

        @pl.loop(0, per_w)
        def _(j):
            ch = wid * per_w + j
            pltpu.sync_copy(e_hbm.at[ch], e_v)
            pltpu.sync_copy(r_hbm.at[ch], r_v)
            for kk in range(K):
                for q in range(C // L):
                    sl = pl.ds(q * L, L)
                    idx_v[kk, sl] = plsc.load_gather(off_v, [e_v[kk, sl]]) + r_v[kk, sl]
            pltpu.sync_copy(idx_v, dest_hbm.at[ch])
            pltpu.sync_copy(h_hbm.at[pl.ds(ch * C, C)], rows_v)
            copies = [pltpu.async_copy(rows_v, xs_hbm.at[idx_v.at[kk]], sem) for kk in range(K)]
            for cp in copies:
                cp.wait()

    return k(eidx_c, rank_c, offs, h2p)


def _gmm_kernel(bstart_ref, nbe_ref, cnt_ref, nu_ref, wg_ref, wu_ref, wd_ref, xs_hbm, ys_hbm,
                wgb, wub, wdb, xbuf, ybuf, xsem, ysem):
    step = pl.program_id(0)
    last = pl.num_programs(0) - 1
    epg = wg_ref.shape[0]
    ring, bm = xbuf.shape[0], xbuf.shape[1]
    nblk = ys_hbm.shape[0] // bm
    nused = nu_ref[0]

    def x_copy(b, slot):
        return pltpu.make_async_copy(xs_hbm.at[pl.ds(pl.multiple_of(b * bm, bm), bm), :], xbuf.at[slot],
                                     xsem.at[slot])

    def y_copy(b, slot):
        return pltpu.make_async_copy(ybuf.at[slot], ys_hbm.at[pl.ds(pl.multiple_of(b * bm, bm), bm), :],
                                     ysem.at[slot])

    @pl.when(step == 0)
    def _():
        for j in range(ring - 1):
            @pl.when(j < nused)
            def _():
                x_copy(j, j).start()

    def run_expert(ee):
        e = step * epg + ee
        b0 = bstart_ref[e]
        nb = nbe_ref[e]

        @pl.when(nb > 0)
        def _():
            def prefetch(t):
                @pl.when(t < nused)
                def _():
                    x_copy(t, jnp.bitwise_and(t, ring - 1)).start()

            def process(b, n, fresh):
                slots = [jnp.bitwise_and(b + j, ring - 1) for j in range(n)]
                for j in range(n):
                    x_copy(b + j, slots[j]).wait()
                prefetch(b + ring - 1)
                for j in range(n):
                    @pl.when(b + j >= ring)
                    def _():
                        y_copy(b + j - ring, slots[j]).wait()
                rows = lax.broadcasted_iota(I32, (n * bm, 1), 0)
                valid = cnt_ref[e] - (b - b0) * bm
                xp = jnp.concatenate([xbuf[s] for s in slots], axis=0)
                xb = jnp.concatenate(_unpack_bf16_pairs(jnp.where(rows < valid, xp, jnp.uint32(0))), axis=1)
                if fresh:
                    wg, wu, wd = (w[ee].astype(BF16) for w in (wg_ref, wu_ref, wd_ref))
                    wgb[...], wub[...], wdb[...] = wg, wu, wd
                else:
                    wg, wu, wd = wgb[...], wub[...], wdb[...]
                a = jnp.dot(xb, wg, preferred_element_type=F32)
                g = jnp.dot(xb, wu, preferred_element_type=F32)
                act = (a * _sigmoid(a)) * g
                yp = _pack_bf16_pairs(jnp.dot(act.astype(BF16), wd, preferred_element_type=F32))
                for j in range(n):
                    ybuf[slots[j]] = yp[j * bm:(j + 1) * bm, :]
                    y_copy(b + j, slots[j]).start()
                for j in range(1, n):
                    prefetch(b + ring - 1 + j)

            quads = lax.shift_right_logical(nb, 2)
            lax.fori_loop(0, quads, lambda j, c: (process(b0 + 4 * j, 4, True), c)[1], 0)
            rest2 = jnp.bitwise_and(nb, 2)
            rest1 = jnp.bitwise_and(nb, 1)
            for n, rest, first, have_copy in ((2, rest2, b0 + 4 * quads, quads > 0),
                                              (1, rest1, b0 + 4 * quads + rest2, nb > 1)):
                @pl.when((rest != 0) & have_copy)
                def _():
                    process(first, n, False)

                @pl.when((rest != 0) & jnp.logical_not(have_copy))
                def _():
                    process(first, n, True)

    for ee in range(epg):
        run_expert(ee)

    @pl.when(step == last)
    def _():
        for back in range(ring, 0, -1):
            @pl.when(nused >= back)
            def _():
                y_copy(nused - back, jnp.bitwise_and(nused - back, ring - 1)).wait()
        ybuf[0] = jnp.zeros(ybuf.shape[1:], ybuf.dtype)
        lax.fori_loop(nused, nblk, lambda b, c: (y_copy(b, 0).start(), c)[1], 0)
        lax.fori_loop(nused, nblk, lambda b, c: (y_copy(b, 0).wait(), c)[1], 0)


def _gmm(bstart, nb_e, counts, nused, xs, w_gate, w_up, w_down):
    P, W = xs.shape
    E, D, F = w_gate.shape
    bm = GMM_BLOCK
    epg = GMM_EXPERTS_PER_STEP
    wsel = lambda s, *_: (s, 0, 0)
    grid_spec = pltpu.PrefetchScalarGridSpec(
        num_scalar_prefetch=4,
        grid=(E // epg,),
        in_specs=[pl.BlockSpec((epg, D, F), wsel), pl.BlockSpec((epg, D, F), wsel), pl.BlockSpec((epg, F, D), wsel),
                  pl.BlockSpec(memory_space=pl.ANY)],
        out_specs=pl.BlockSpec(memory_space=pl.ANY),
        scratch_shapes=[pltpu.VMEM((D, F), BF16), pltpu.VMEM((D, F), BF16), pltpu.VMEM((F, D), BF16),
                        pltpu.VMEM((GMM_RING, bm, W), xs.dtype), pltpu.VMEM((GMM_RING, bm, W), xs.dtype),
                        pltpu.SemaphoreType.DMA((GMM_RING,)), pltpu.SemaphoreType.DMA((GMM_RING,))],
    )
    return pl.pallas_call(
        _gmm_kernel,
        grid_spec=grid_spec,
        out_shape=jax.ShapeDtypeStruct((P, W), xs.dtype),
        compiler_params=pltpu.CompilerParams(dimension_semantics=("arbitrary",)),
        name="gmm",
    )(bstart, nb_e, counts, nused, w_gate, w_up, w_down, xs)


def _sc_gather(dest_c, ys):
    nch, K, C = dest_c.shape
    W = ys.shape[1]
    H = SC_GATHER_ROWS
    info = plsc.get_sparse_core_info()
    nw = info.num_cores * info.num_subcores
    assert nch % nw == 0, "token chunks must split evenly over the vector subcores"
    per_w = nch // nw
    nbuf = SC_GATHER_BUFS
    ahead = nbuf - 1
    mesh = plsc.VectorSubcoreMesh(core_axis_name="c", subcore_axis_name="s")
    items = [(kk, hh) for kk in range(K) for hh in range(C // H)]

    @functools.partial(
        pl.kernel, mesh=mesh,
        out_type=jax.ShapeDtypeStruct((K, nch * C, W), ys.dtype),
        scratch_types=([pltpu.VMEM((K, C), I32)] + [pltpu.VMEM((H, W), ys.dtype)] * nbuf
                       + [pltpu.SemaphoreType.DMA] * (2 * nbuf)),
        name="sc_gather",
    )
    def k(dest_hbm, ys_hbm, yk_hbm, idx_v, *rest):
        bufs, gsem, wsem = rest[:nbuf], rest[nbuf:2 * nbuf], rest[2 * nbuf:]
        wid = lax.axis_index("s") * info.num_cores + lax.axis_index("c")

        @pl.loop(0, per_w)
        def _(j):
            ch = wid * per_w + j
            pltpu.sync_copy(dest_hbm.at[ch], idx_v)

            def gather(i):
                kk, hh = items[i]
                return pltpu.async_copy(ys_hbm.at[idx_v.at[kk, pl.ds(hh * H, H)]], bufs[i % nbuf], gsem[i % nbuf])

            def write(i):
                kk, hh = items[i]
                return pltpu.async_copy(bufs[i % nbuf], yk_hbm.at[kk, pl.ds(ch * C + hh * H, H)], wsem[i % nbuf])

            n = len(items)
            g = {i: gather(i) for i in range(ahead)}
            w = {}
            for i in range(n):
                g[i].wait()
                w[i] = write(i)
                if i + ahead < n:
                    if i >= 1:
                        w.pop(i - 1).wait()
                    g[i + ahead] = gather(i + ahead)
            for i in sorted(w):
                w[i].wait()

    return k(dest_c, ys)


def _combine_kernel(yk_ref, g_ref, gate2_ref, xacc_ref, gfin_ref, o_ref, gpad_ref):
    @pl.when(pl.program_id(0) == 0)
    def _():
        gpad_ref[...] = jnp.zeros_like(gpad_ref)

    gpad_ref[0:TOP_K, :] = g_ref[...]
    gt = gpad_ref[...].T
    hi_mask = jnp.uint32(0xFFFF0000)
    acc_lo = acc_hi = None
    for k in range(TOP_K):
        p = yk_ref[k]
        g = gt[:, k:k + 1]
        lo = lax.bitcast_convert_type(p << 16, F32) * g
        hi = lax.bitcast_convert_type(p & hi_mask, F32) * g
        acc_lo = lo if k == 0 else acc_lo + lo
        acc_hi = hi if k == 0 else acc_hi + hi
    routed = jnp.concatenate([acc_lo, acc_hi], axis=1)
    x2 = xacc_ref[...] + gate2_ref[0] * routed
    o_ref[...] = _rms(x2) * gfin_ref[...]


def _combine(yk, gates, gate2, xacc, g_final, S):
    N, D = xacc.shape
    W = yk.shape[2]
    T = COMBINE_TILE
    spt = S // T
    return pl.pallas_call(
        _combine_kernel,
        grid=(N // T,),
        in_specs=[pl.BlockSpec((TOP_K, T, W), lambda i: (0, i, 0)),
                  pl.BlockSpec((TOP_K, T), lambda i: (0, i)),
                  pl.BlockSpec((1, 1, D), lambda i: (i // spt, 0, 0)),
                  pl.BlockSpec((T, D), lambda i: (i, 0)),
                  pl.BlockSpec((1, D), lambda i: (0, 0))],
        out_specs=pl.BlockSpec((T, D), lambda i: (i, 0)),
        out_shape=jax.ShapeDtypeStruct((N, D), F32),
        scratch_shapes=[pltpu.VMEM((LANES, T), F32)],
        compiler_params=pltpu.CompilerParams(dimension_semantics=("arbitrary",)),
        name="combine",
    )(yk, gates, gate2, xacc, g_final)


def _layer(xf, B, S, mod, g_mix, w_in, pool_w, pool_scale, w_out, g_ffn, w_router, router_bias,
           w_gate, w_up, w_down, ws_gate, ws_up, ws_down):
    N, D = xf.shape
    E = w_router.shape[1]
    pw = pool_scale.shape[0]
    shift1, scale1, gate1, shift2, scale2, gate2 = [m.reshape(B, 1, D) for m in jnp.split(mod, 6, axis=-1)]
    u, qkv = _in_proj(xf, g_mix.reshape(1, D), shift1, scale1, w_in.astype(BF16), S, pw)
    attn_outs = []
    for a, d in zip(qkv, ATT_DILATIONS):
        o, lse = _attention(a, S // d // ATT_BLOCK)
        shape = (N, ATT_GROUP_WIDTH) if d == 1 else (B * d, S // d, ATT_GROUP_WIDTH)
        attn_outs += [o.reshape(shape), lse.reshape(shape)]
    ng = pool_w.shape[0]
    pool_bd = jnp.einsum('gcd,gh->gchd', pool_w, jnp.eye(ng, dtype=pool_w.dtype)).reshape(pw, pw).astype(BF16)
    xacc, h2, eidx, gates, rank, counts = _mid(
        u, attn_outs, xf, pool_bd, pool_scale.reshape(1, pw), w_out.astype(BF16), gate1,
        g_ffn.reshape(1, D), shift2, scale2, gate2,
        ws_gate.astype(BF16), ws_up.astype(BF16), ws_down.astype(BF16),
        w_router.T.astype(BF16), router_bias.reshape(E, 1).astype(F32), S)
    bm = GMM_BLOCK
    nblk = N * TOP_K // bm + E
    nb_e = (counts[:, 0] + bm - 1) // bm
    bend = jnp.cumsum(nb_e)
    bstart = (bend - nb_e).astype(I32)
    nused = bend[-1:].astype(I32)
    xs, dest_c = _sc_dispatch(eidx, rank, bstart * bm, h2, nblk * bm)
    ys = _gmm(bstart, nb_e.astype(I32), counts[:, 0], nused, xs, w_gate, w_up, w_down)
    return _sc_gather(dest_c, ys), gates, gate2, xacc


def kernel(x, c, w_ada, b_ada, g_mix, w_in, pool_w, pool_scale, w_out, g_ffn, w_router, router_bias,
           w_gate, w_up, w_down, ws_gate, ws_up, ws_down, g_final):
    B, S, D = x.shape
    depth = w_ada.shape[0]
    assert depth == 1, "the final residual is fused with the final norm, so exactly one layer is supported"
    assert S % (ATT_DILATIONS[-1] * ATT_BLOCK) == 0
    assert all(S % t == 0 for t in (IN_TILE, MID_TILE, COMBINE_TILE, ATT_BLOCKS_PER_STEP * ATT_BLOCK))
    assert MID_TILE % ROUTE_TILE == 0 and ROUTE_TILE % SC_CHUNK == 0 and SC_CHUNK % SC_GATHER_ROWS == 0
    assert w_gate.shape[1] % GMM_EXPERTS_PER_STEP == 0 and max(POOL_WINDOWS) <= POOL_HALO
    xf = x.reshape(B * S, D)
    mod = _ada(c, w_ada[0], b_ada[0])
    yk, gates, gate2, xacc = _layer(
        xf, B, S, mod, g_mix[0], w_in[0], pool_w[0], pool_scale[0], w_out[0], g_ffn[0], w_router[0],
        router_bias[0], w_gate[0], w_up[0], w_down[0], ws_gate[0], ws_up[0], ws_down[0])
    out = _combine(yk, gates, gate2, xacc, g_final.reshape(1, D), S)
    return out.reshape(B, S, D)
```

```python
import functools

import jax
import jax.numpy as jnp
from jax import lax
from jax.experimental import pallas as pl
from jax.experimental.pallas import tpu as pltpu
from jax.experimental.pallas import tpu_sc as plsc

F32 = jnp.float32
BF16 = jnp.bfloat16
I32 = jnp.int32
U32 = jnp.uint32

LANES = 128
SINGLE_LOAD_STRIDE = 4
NORM_EPS = 1e-6
POOL_WINDOWS = (2, 4, 8, 16)
POOL_HALO = 16
ATT_DILATIONS = (1, 4, 16)
ATT_BLOCK = 128
ATT_HEADS_PER_GROUP = 4
ATT_HEAD_DIM = 64
ATT_GROUP_WIDTH = ATT_HEADS_PER_GROUP * ATT_HEAD_DIM
N_EXPERT_GROUPS = 8
TOPK_GROUPS = 4
TOP_K = 8
ROUTED_SCALE = 2.5

IN_TILE = 1024
ATT_BLOCKS_PER_STEP = 16
ATT_UNROLL = 16
MID_TILE = 512
ROUTE_TILE = 512
COMBINE_TILE = 512
GMM_BLOCK = 128
GMM_RING = 16
GMM_EXPERTS_PER_STEP = 1
SC_CHUNK = 128
SC_GATHER_ROWS = 32
SC_GATHER_BUFS = 6

NEG_INF = float("-inf")


def _sigmoid(v):
    return 1.0 / (1.0 + jnp.exp(-v))


def _rms(v):
    return v * lax.rsqrt(jnp.mean(v * v, axis=-1, keepdims=True) + NORM_EPS)


def _pack_bf16_pairs(v):
    n = v.shape[1] // 2
    lo = lax.bitcast_convert_type(v[:, :n].astype(BF16).astype(F32), U32)
    hi = lax.bitcast_convert_type(v[:, n:].astype(BF16).astype(F32), U32)
    return (hi & jnp.uint32(0xFFFF0000)) | (lo >> 16)


def _unpack_bf16_pairs(p):
    lo = lax.bitcast_convert_type(p << 16, F32).astype(BF16)
    hi = lax.bitcast_convert_type(p & jnp.uint32(0xFFFF0000), F32).astype(BF16)
    return lo, hi


def _ada_kernel(c_ref, w_ref, b_ref, o_ref):
    c = c_ref[...]
    cs = c * _sigmoid(c)
    o_ref[...] = jnp.dot(cs, w_ref[...], preferred_element_type=F32,
                         precision=lax.Precision.HIGHEST) + b_ref[...]


def _ada(c, w_ada, b_ada):
    B, D = c.shape
    W = w_ada.shape[1]
    tn = 1024
    return pl.pallas_call(
        _ada_kernel,
        grid=(W // tn,),
        in_specs=[pl.BlockSpec((B, D), lambda j: (0, 0)),
                  pl.BlockSpec((D, tn), lambda j: (0, j)),
                  pl.BlockSpec((1, tn), lambda j: (0, j))],
        out_specs=pl.BlockSpec((B, tn), lambda j: (0, j)),
        out_shape=jax.ShapeDtypeStruct((B, W), F32),
        name="ada",
    )(c, w_ada, b_ada.reshape(1, W))


def _in_kernel(x_ref, g_ref, sh_ref, sc_ref, w_ref, pool_ref, q0_ref, q1_ref, q2_ref, scr_ref, tmp_ref):
    h = _rms(x_ref[...]) * g_ref[...]
    h = h * (1.0 + sc_ref[0]) + sh_ref[0]
    hb = h.astype(BF16)
    tm = x_ref.shape[0]
    pw = pool_ref.shape[1]
    gw = ATT_GROUP_WIDTH
    pool_ref[...] = jnp.dot(hb, w_ref[:, 0:pw], preferred_element_type=F32)
    for g, (out, d) in enumerate(zip((q0_ref, q1_ref, q2_ref), ATT_DILATIONS)):
        for sec in range(3):
            c0 = pw + sec * 3 * gw + g * gw
            res = jnp.dot(hb, w_ref[:, c0:c0 + gw], preferred_element_type=F32)
            if d == 1:
                out[:, sec * gw:(sec + 1) * gw] = res.astype(BF16)
            else:
                for c in range(gw // LANES):
                    scr_ref[c] = res[:, c * LANES:(c + 1) * LANES]
                    src, f1 = scr_ref.at[c], 1
                    if d > SINGLE_LOAD_STRIDE:
                        f1 = SINGLE_LOAD_STRIDE
                        for q in range(f1):
                            tmp_ref[c, q * (tm // f1):(q + 1) * (tm // f1), :] = scr_ref[c, pl.ds(q, tm // f1, stride=f1), :]
                        src = tmp_ref.at[c]
                    for r in range(d):
                        r_lo, r_hi = r % f1, r // f1
                        c1 = sec * gw + c * LANES
                        out[r, :, c1:c1 + LANES] = src[pl.ds(r_lo * (tm // f1) + r_hi, tm // d, stride=d // f1),
                                                       :].astype(BF16)


def _in_proj(xf, g_mix, shift1, scale1, w_in_b, S, pool_width):
    N, D = xf.shape
    tm = IN_TILE
    spt = S // tm
    vec = lambda i: (i // spt, 0, 0)
    row = lambda i: (i, 0)
    gw3 = 3 * ATT_GROUP_WIDTH
    B = N // S
    res_spec = lambda d: pl.BlockSpec((d, tm // d, gw3), lambda i: (i // spt, i % spt, 0))
    res_shape = lambda d: jax.ShapeDtypeStruct((B * d, S // d, gw3), BF16)
    outs = pl.pallas_call(
        _in_kernel,
        grid=(N // tm,),
        in_specs=[pl.BlockSpec((tm, D), row),
                  pl.BlockSpec((1, D), lambda i: (0, 0)),
                  pl.BlockSpec((1, 1, D), vec),
                  pl.BlockSpec((1, 1, D), vec),
                  pl.BlockSpec(w_in_b.shape, lambda i: (0, 0))],
        out_specs=[pl.BlockSpec((tm, pool_width), row), pl.BlockSpec((tm, gw3), row)]
                  + [res_spec(d) for d in ATT_DILATIONS[1:]],
        out_shape=[jax.ShapeDtypeStruct((N, pool_width), F32), jax.ShapeDtypeStruct((N, gw3), BF16)]
                  + [res_shape(d) for d in ATT_DILATIONS[1:]],
        scratch_shapes=[pltpu.VMEM((ATT_GROUP_WIDTH // LANES, tm, LANES), F32)] * 2,
        name="in_proj",
    )(xf, g_mix, shift1, scale1, w_in_b)
    return outs[0], [o.reshape(N, gw3) for o in outs[1:]]


def _attn_kernel(nbs, a_ref, halo_ref, o_ref, lse_ref, kv_ref, band_ref):
    i = pl.program_id(0)
    R = a_ref.shape[0] // ATT_BLOCK
    gw = ATT_GROUP_WIDTH
    blk = ATT_BLOCK
    nh = ATT_HEADS_PER_GROUP
    kv_ref[0:blk, :] = halo_ref[:, gw:3 * gw]
    kv_ref[blk:, :] = a_ref[:, gw:3 * gw]
    row = lax.broadcasted_iota(I32, (nh * blk, 2 * blk), 0) % blk
    col = lax.broadcasted_iota(I32, (nh * blk, 2 * blk), 1)
    in_band = (col >= row) & (col <= row + blk)
    band_ref[0] = jnp.where(in_band, 0.0, NEG_INF)
    band_ref[1] = jnp.where(in_band & (col >= blk), 0.0, NEG_INF)
    head_of_lane = lax.broadcasted_iota(I32, (blk, gw), 1) // ATT_HEAD_DIM
    nt = (((1,), (1,)), ((), ()))

    def one_block(jj, start):
        r0 = pl.multiple_of(jj * blk, blk)
        qf = a_ref[pl.ds(r0, blk), 0:gw].astype(F32) * (ATT_HEAD_DIM ** -0.5)
        q4 = jnp.concatenate([jnp.where(head_of_lane == h, qf, 0.0) for h in range(nh)], axis=0).astype(BF16)
        kc = kv_ref[pl.ds(r0, 2 * blk), 0:gw]
        vc = kv_ref[pl.ds(r0, 2 * blk), gw:2 * gw]
        s = lax.dot_general(q4, kc, nt, preferred_element_type=F32) + band_ref[1 if start is True else 0]
        if start is not None and start is not True:
            s = jnp.where(col >= jnp.where(start, blk, 0), s, NEG_INF)
        m = jnp.max(s, axis=1, keepdims=True)
        p = jnp.exp(s - m)
        l = jnp.sum(p, axis=1, keepdims=True)
        o4 = jnp.dot(p.astype(BF16), vc, preferred_element_type=F32) / l
        lse4 = m + jnp.log(l)
        o = jnp.zeros((blk, gw), F32)
        lse = jnp.zeros((blk, gw), F32)
        for h in range(nh):
            hm = head_of_lane == h
            o = jnp.where(hm, o4[h * blk:(h + 1) * blk, :], o)
            lse = jnp.where(hm, lse4[h * blk:(h + 1) * blk, :], lse)
        o_ref[pl.ds(r0, blk), :] = o
        lse_ref[pl.ds(r0, blk), :] = lse

    U = ATT_UNROLL
    assert U % nbs == 0 or nbs % U == 0

    def body(it, carry):
        for j in range(U):
            if U % nbs == 0:
                start = True if j % nbs == 0 else None
            else:
                start = (((i * R + it * U) % nbs) == 0) if j == 0 else None
            one_block(it * U + j, start)
        return carry

    lax.fori_loop(0, R // U, body, 0)


def _attention(a, nbs):
    N = a.shape[0]
    R = ATT_BLOCKS_PER_STEP
    gw = ATT_GROUP_WIDTH
    tm = R * ATT_BLOCK
    return pl.pallas_call(
        functools.partial(_attn_kernel, nbs),
        grid=(N // tm,),
        in_specs=[pl.BlockSpec((tm, 3 * gw), lambda i: (i, 0)),
                  pl.BlockSpec((ATT_BLOCK, 3 * gw), lambda i: (jnp.maximum(i * R - 1, 0), 0))],
        out_specs=[pl.BlockSpec((tm, gw), lambda i: (i, 0))] * 2,
        out_shape=[jax.ShapeDtypeStruct((N, gw), F32)] * 2,
        scratch_shapes=[pltpu.VMEM((tm + ATT_BLOCK, 2 * gw), BF16),
                        pltpu.VMEM((2, ATT_HEADS_PER_GROUP * ATT_BLOCK, 2 * ATT_BLOCK), F32)],
        name="attn",
    )(a, a)


def _mid_kernel(spt, u_ref, uh_ref, o0_ref, l0_ref, o1_ref, l1_ref, o2_ref, l2_ref, x_ref,
                pbd_ref, psc_ref, wout_ref, gate1_ref, gffn_ref, sh2_ref, sc2_ref, gate2_ref,
                wsg_ref, wsu_ref, wsd_ref, wr_ref, bias_ref, before_ref,
                xacc_ref, h2_ref, e_ref, g_ref, r_ref, cnt_ref, til_ref, tmp_ref, base_ref):
    i = pl.program_id(0)
    tm, pw = u_ref.shape
    si = i % spt
    u = u_ref[...]
    keep = jnp.full((POOL_HALO, pw), si, I32) > 0
    ext = jnp.concatenate([jnp.where(keep, uh_ref[...], 0.0), u], axis=0)
    lane_grp = lax.broadcasted_iota(I32, (tm, pw), 1) // (pw // len(POOL_WINDOWS))
    pooled = jnp.zeros((tm, pw), F32)
    s, w = ext, 1
    while w < POOL_HALO:
        s = s + pltpu.roll(s, w, axis=0)
        w *= 2
        if w in POOL_WINDOWS:
            pooled = jnp.where(lane_grp == POOL_WINDOWS.index(w), s[POOL_HALO:, :], pooled)
    win = jnp.zeros((tm, pw), I32)
    for g, w in enumerate(POOL_WINDOWS):
        win = jnp.where(lane_grp == g, w, win)
    pos = si * tm + lax.broadcasted_iota(I32, (tm, pw), 0)
    cnt = jnp.minimum(pos + 1, win).astype(F32)
    pooled = pooled / cnt - u
    pool_out = jnp.dot(pooled.astype(BF16), pbd_ref[...], preferred_element_type=F32) * psc_ref[...]
    def token_order(slot, ref):
        d, n, w = ref.shape
        f1 = SINGLE_LOAD_STRIDE if d > SINGLE_LOAD_STRIDE else 1
        for c in range(w // LANES):
            cols = slice(c * LANES, (c + 1) * LANES)
            if f1 == 1:
                for r in range(d):
                    til_ref[slot, c, pl.ds(r, n, stride=d), :] = ref[r, :, cols]
            else:
                f2, slab = d // f1, tm // f1
                for r in range(d):
                    r_lo, r_hi = r % f1, r // f1
                    tmp_ref[c, pl.ds(r_lo * slab + r_hi, n, stride=f2), :] = ref[r, :, cols]
                for r_lo in range(f1):
                    til_ref[slot, c, pl.ds(r_lo, slab, stride=f1), :] = tmp_ref[c, r_lo * slab:(r_lo + 1) * slab, :]
        return jnp.concatenate([til_ref[slot, c] for c in range(w // LANES)], axis=1)

    l0 = l0_ref[...]
    l1 = token_order(0, l1_ref)
    l2 = token_order(1, l2_ref)
    m = jnp.maximum(jnp.maximum(l0, l1), l2)
    w0 = jnp.exp(l0 - m)
    w1 = jnp.exp(l1 - m)
    w2 = jnp.exp(l2 - m)
    attn = (w0 * o0_ref[...] + w1 * token_order(2, o1_ref) + w2 * token_order(3, o2_ref)) / (w0 + w1 + w2)
    mixed = jnp.dot(jnp.concatenate([pool_out.astype(BF16), attn.astype(BF16)], axis=1), wout_ref[...],
                    preferred_element_type=F32)
    x1 = x_ref[...] + gate1_ref[0] * mixed
    h2 = _rms(x1) * gffn_ref[...]
    h2 = h2 * (1.0 + sc2_ref[0]) + sh2_ref[0]
    h2_ref[...] = _pack_bf16_pairs(h2)
    hb = h2.astype(BF16)
    a = jnp.dot(hb, wsg_ref[...], preferred_element_type=F32)
    b = jnp.dot(hb, wsu_ref[...], preferred_element_type=F32)
    act = (a * _sigmoid(a)) * b
    shared = jnp.dot(act.astype(BF16), wsd_ref[...], preferred_element_type=F32)
    xacc_ref[...] = x1 + gate2_ref[0] * shared
    @pl.when(i == 0)
    def _():
        base_ref[...] = jnp.zeros_like(base_ref)

    C = e_ref.shape[2]
    for t0 in range(0, tm, ROUTE_TILE):
        idxs, gates, ranks = _route_tile(hb[t0:t0 + ROUTE_TILE, :], wr_ref, bias_ref, before_ref, base_ref)
        for k in range(TOP_K):
            g_ref[k:k + 1, t0:t0 + ROUTE_TILE] = gates[k]
            for c in range(ROUTE_TILE // C):
                e_ref[t0 // C + c, k:k + 1, :] = idxs[k][:, c * C:(c + 1) * C]
                r_ref[t0 // C + c, k:k + 1, :] = ranks[k][:, c * C:(c + 1) * C]
    cnt_ref[...] = base_ref[...].astype(I32)


def _mid(u, attn_outs, xf, pool_bd, pool_scale, w_out_b, gate1, g_ffn, shift2, scale2, gate2,
         wsg_b, wsu_b, wsd_b, wr_t, bias_col, S):
    N, D = xf.shape
    E = wr_t.shape[0]
    tok = jnp.arange(ROUTE_TILE, dtype=I32)
    before = (tok[:, None] < tok[None, :]).astype(BF16)
    pw = u.shape[1]
    tm = MID_TILE
    spt = S // tm
    row = lambda i: (i, 0)
    vec = lambda i: (i // spt, 0, 0)
    full = lambda a: pl.BlockSpec(a.shape, lambda i: (0,) * a.ndim)
    hpt = tm // POOL_HALO
    in_specs = [pl.BlockSpec((tm, pw), row),
                pl.BlockSpec((POOL_HALO, pw), lambda i: (jnp.maximum(i * hpt - 1, 0), 0))]
    gw = ATT_GROUP_WIDTH
    in_specs += [pl.BlockSpec((tm, gw), row)] * 2
    for d in ATT_DILATIONS[1:]:
        in_specs += [pl.BlockSpec((d, tm // d, gw), lambda i: (i // spt, i % spt, 0))] * 2
    in_specs += [pl.BlockSpec((tm, D), row), full(pool_bd), full(pool_scale), full(w_out_b),
                 pl.BlockSpec((1, 1, D), vec), full(g_ffn), pl.BlockSpec((1, 1, D), vec),
                 pl.BlockSpec((1, 1, D), vec), pl.BlockSpec((1, 1, D), vec),
                 full(wsg_b), full(wsu_b), full(wsd_b), full(wr_t), full(bias_col), full(before)]
    col = lambda i: (0, i)
    C = SC_CHUNK
    chunked = pl.BlockSpec((tm // C, TOP_K, C), lambda i: (i, 0, 0))
    return pl.pallas_call(
        functools.partial(_mid_kernel, spt),
        grid=(N // tm,),
        in_specs=in_specs,
        out_specs=[pl.BlockSpec((tm, D), row), pl.BlockSpec((tm, D // 2), row),
                   chunked, pl.BlockSpec((TOP_K, tm), col), chunked,
                   pl.BlockSpec((E, 1), lambda i: (0, 0))],
        out_shape=[jax.ShapeDtypeStruct((N, D), F32), jax.ShapeDtypeStruct((N, D // 2), U32),
                   jax.ShapeDtypeStruct((N // C, TOP_K, C), I32), jax.ShapeDtypeStruct((TOP_K, N), F32),
                   jax.ShapeDtypeStruct((N // C, TOP_K, C), I32), jax.ShapeDtypeStruct((E, 1), I32)],
        scratch_shapes=[pltpu.VMEM((4, gw // LANES, tm, LANES), F32), pltpu.VMEM((gw // LANES, tm, LANES), F32),
                        pltpu.VMEM((E, 1), F32)],
        compiler_params=pltpu.CompilerParams(dimension_semantics=("arbitrary",)),
        name="mid",
    )(u, u, *attn_outs, xf, pool_bd, pool_scale, w_out_b, gate1, g_ffn, shift2, scale2, gate2,
      wsg_b, wsu_b, wsd_b, wr_t, bias_col, before)


def _route_tile(hb, wr_ref, bias_ref, before_ref, base_ref):
    T = hb.shape[0]
    E = wr_ref.shape[0]
    gsz = E // N_EXPERT_GROUPS
    logits = lax.dot_general(wr_ref[...], hb, (((1,), (1,)), ((), ())), preferred_element_type=F32)
    scores = _sigmoid(logits)
    biased = scores + bias_ref[...]
    giota = lax.broadcasted_iota(I32, (gsz, T), 0)
    gscore = []
    for g in range(N_EXPERT_GROUPS):
        blk = biased[g * gsz:(g + 1) * gsz, :]
        m1 = jnp.max(blk, axis=0, keepdims=True)
        i1 = jnp.min(jnp.where(blk == m1, giota, gsz), axis=0, keepdims=True)
        m2 = jnp.max(jnp.where(giota == i1, NEG_INF, blk), axis=0, keepdims=True)
        gscore.append(m1 + m2)
    parts = []
    for g in range(N_EXPERT_GROUPS):
        beaten = jnp.zeros((1, T), I32)
        for o in range(N_EXPERT_GROUPS):
            if o == g:
                continue
            wins = (gscore[o] >= gscore[g]) if o < g else (gscore[o] > gscore[g])
            beaten = beaten + wins.astype(I32)
        keep = jnp.broadcast_to(beaten, (gsz, T)) < TOPK_GROUPS
        parts.append(jnp.where(keep, biased[g * gsz:(g + 1) * gsz, :], NEG_INF))
    cur = jnp.concatenate(parts, axis=0)
    eiota = lax.broadcasted_iota(I32, (E, T), 0)
    live = cur > NEG_INF
    idxs, gates, picks = [], [], []
    sub = lax.broadcasted_iota(I32, (8, T), 0)
    for k in range(TOP_K):
        nodes = [(cur[8 * j:8 * (j + 1), :], j, scores[8 * j:8 * (j + 1), :]) for j in range(E // 8)]
        while len(nodes) > 1:
            merged = []
            for p in range(0, len(nodes), 2):
                (va, ta, sa), (vb, tb, sb) = nodes[p], nodes[p + 1]
                later = vb > va
                merged.append((jnp.where(later, vb, va), jnp.where(later, tb, ta), jnp.where(later, sb, sa)))
            nodes = merged
        v8, t8, s8 = nodes[0]
        e8 = t8 * 8 + sub
        m = jnp.max(v8, axis=0, keepdims=True)
        idx = jnp.min(jnp.where(v8 == m, e8, E), axis=0, keepdims=True)
        oh = eiota == idx
        won = e8 == idx
        gates.append(jnp.sum(jnp.where(won, s8, 0.0), axis=0, keepdims=True))
        idxs.append(idx)
        picks.append((t8, won))
        cur = jnp.where(oh, NEG_INF, cur)
    selm = jnp.where(live & (cur == NEG_INF), 1.0, 0.0)
    gsum = gates[0]
    for k in range(1, TOP_K):
        gsum = gsum + gates[k]
    gates = [gk / gsum * ROUTED_SCALE for gk in gates]
    tot = jnp.dot(selm.astype(BF16), before_ref[...], preferred_element_type=F32) + base_ref[...]
    tiles = [tot[8 * j:8 * (j + 1), :] for j in range(E // 8)]
    ranks = []
    for t8, won in picks:
        level, bit = tiles, 0
        while len(level) > 1:
            odd = ((t8 >> bit) & 1) == 1
            level = [jnp.where(odd, level[p + 1], level[p]) for p in range(0, len(level), 2)]
            bit += 1
        ranks.append(jnp.sum(jnp.where(won, level[0], 0.0), axis=0, keepdims=True).astype(I32))
    base_ref[...] = base_ref[...] + jnp.sum(selm, axis=1, keepdims=True)
    return idxs, gates, ranks


def _sc_dispatch(eidx_c, rank_c, offs, h2p, P):
    N, W = h2p.shape
    nch, K, C = eidx_c.shape
    E = offs.shape[0]
    info = plsc.get_sparse_core_info()
    nw = info.num_cores * info.num_subcores
    L = info.num_lanes
    assert nch % nw == 0, "token chunks must split evenly over the vector subcores"
    per_w = nch // nw
    mesh = plsc.VectorSubcoreMesh(core_axis_name="c", subcore_axis_name="s")

    @functools.partial(
        pl.kernel, mesh=mesh,
        out_type=[jax.ShapeDtypeStruct((P, W), h2p.dtype), jax.ShapeDtypeStruct((nch, K, C), I32)],
        scratch_types=[pltpu.VMEM((E,), I32), pltpu.VMEM((K, C), I32), pltpu.VMEM((K, C), I32),
                       pltpu.VMEM((K, C), I32), pltpu.VMEM((C, W), h2p.dtype), pltpu.SemaphoreType.DMA,
                       pltpu.SemaphoreType.DMA],
        compiler_params=pltpu.CompilerParams(needs_layout_passes=False),
        name="sc_dispatch",
    )
    def k(e_hbm, r_hbm, off_hbm, h_hbm, xs_hbm, dest_hbm, off_v, e_v, r_v, idx_v, rows_v, sem, rsem):
        wid = lax.axis_index("s") * info.num_cores + lax.axis_index("c")
        pltpu.sync_copy(off_hbm, off_v)

        @pl.loop(0, per_w)
        def _(j):
            ch = wid * per_w + j
            rows_in = pltpu.async_copy(h_hbm.at[pl.ds(ch * C, C)], rows_v, rsem)
            pltpu.sync_copy(e_hbm.at[ch], e_v)
            pltpu.sync_copy(r_hbm.at[ch], r_v)
            for kk in range(K):
                for q in range(C // L):
                    sl = pl.ds(q * L, L)
                    idx_v[kk, sl] = plsc.load_gather(off_v, [e_v[kk, sl]]) + r_v[kk, sl]
            pltpu.sync_copy(idx_v, dest_hbm.at[ch])
            rows_in.wait()
            copies = [pltpu.async_copy(rows_v, xs_hbm.at[idx_v.at[kk]], sem) for kk in range(K)]
            for cp in copies:
                cp.wait()

    return k(eidx_c, rank_c, offs, h2p)


def _gmm_kernel(bstart_ref, nbe_ref, cnt_ref, nu_ref, wg_ref, wu_ref, wd_ref, xs_hbm, ys_hbm,
                wgb, wub, wdb, xbuf, ybuf, xsem, ysem):
    step = pl.program_id(0)
    last = pl.num_programs(0) - 1
    epg = wg_ref.shape[0]
    ring, bm = xbuf.shape[0], xbuf.shape[1]
    nblk = ys_hbm.shape[0] // bm
    nused = nu_ref[0]

    def x_copy(b, slot):
        return pltpu.make_async_copy(xs_hbm.at[pl.ds(pl.multiple_of(b * bm, bm), bm), :], xbuf.at[slot],
                                     xsem.at[slot])

    def y_copy(b, slot):
        return pltpu.make_async_copy(ybuf.at[slot], ys_hbm.at[pl.ds(pl.multiple_of(b * bm, bm), bm), :],
                                     ysem.at[slot])

    @pl.when(step == 0)
    def _():
        for j in range(ring - 1):
            @pl.when(j < nused)
            def _():
                x_copy(j, j).start()

    def run_expert(ee):
        e = step * epg + ee
        b0 = bstart_ref[e]
        nb = nbe_ref[e]

        @pl.when(nb > 0)
        def _():
            def prefetch(t):
                @pl.when(t < nused)
                def _():
                    x_copy(t, jnp.bitwise_and(t, ring - 1)).start()

            def process(b, n, fresh):
                slots = [jnp.bitwise_and(b + j, ring - 1) for j in range(n)]
                for j in range(n):
                    x_copy(b + j, slots[j]).wait()
                prefetch(b + ring - 1)
                for j in range(n):
                    @pl.when(b + j >= ring)
                    def _():
                        y_copy(b + j - ring, slots[j]).wait()
                rows = lax.broadcasted_iota(I32, (n * bm, 1), 0)
                valid = cnt_ref[e] - (b - b0) * bm
                xp = jnp.concatenate([xbuf[s] for s in slots], axis=0)
                xb = jnp.concatenate(_unpack_bf16_pairs(jnp.where(rows < valid, xp, jnp.uint32(0))), axis=1)
                if fresh:
                    wg, wu, wd = (w[ee].astype(BF16) for w in (wg_ref, wu_ref, wd_ref))
                    wgb[...], wub[...], wdb[...] = wg, wu, wd
                else:
                    wg, wu, wd = wgb[...], wub[...], wdb[...]
                a = jnp.dot(xb, wg, preferred_element_type=F32)
                g = jnp.dot(xb, wu, preferred_element_type=F32)
                act = (a * _sigmoid(a)) * g
                yp = _pack_bf16_pairs(jnp.dot(act.astype(BF16), wd, preferred_element_type=F32))
                for j in range(n):
                    ybuf[slots[j]] = yp[j * bm:(j + 1) * bm, :]
                    y_copy(b + j, slots[j]).start()
                for j in range(1, n):
                    prefetch(b + ring - 1 + j)

            quads = lax.shift_right_logical(nb, 2)
            lax.fori_loop(0, quads, lambda j, c: (process(b0 + 4 * j, 4, True), c)[1], 0)
            rest2 = jnp.bitwise_and(nb, 2)
            rest1 = jnp.bitwise_and(nb, 1)
            for n, rest, first, have_copy in ((2, rest2, b0 + 4 * quads, quads > 0),
                                              (1, rest1, b0 + 4 * quads + rest2, nb > 1)):
                @pl.when((rest != 0) & have_copy)
                def _():
                    process(first, n, False)

                @pl.when((rest != 0) & jnp.logical_not(have_copy))
                def _():
                    process(first, n, True)

    for ee in range(epg):
        run_expert(ee)

    @pl.when(step == last)
    def _():
        for back in range(ring, 0, -1):
            @pl.when(nused >= back)
            def _():
                y_copy(nused - back, jnp.bitwise_and(nused - back, ring - 1)).wait()
        ybuf[0] = jnp.zeros(ybuf.shape[1:], ybuf.dtype)
        lax.fori_loop(nused, nblk, lambda b, c: (y_copy(b, 0).start(), c)[1], 0)
        lax.fori_loop(nused, nblk, lambda b, c: (y_copy(b, 0).wait(), c)[1], 0)


def _gmm(bstart, nb_e, counts, nused, xs, w_gate, w_up, w_down):
    P, W = xs.shape
    E, D, F = w_gate.shape
    bm = GMM_BLOCK
    epg = GMM_EXPERTS_PER_STEP
    wsel = lambda s, *_: (s, 0, 0)
    grid_spec = pltpu.PrefetchScalarGridSpec(
        num_scalar_prefetch=4,
        grid=(E // epg,),
        in_specs=[pl.BlockSpec((epg, D, F), wsel), pl.BlockSpec((epg, D, F), wsel), pl.BlockSpec((epg, F, D), wsel),
                  pl.BlockSpec(memory_space=pl.ANY)],
        out_specs=pl.BlockSpec(memory_space=pl.ANY),
        scratch_shapes=[pltpu.VMEM((D, F), BF16), pltpu.VMEM((D, F), BF16), pltpu.VMEM((F, D), BF16),
                        pltpu.VMEM((GMM_RING, bm, W), xs.dtype), pltpu.VMEM((GMM_RING, bm, W), xs.dtype),
                        pltpu.SemaphoreType.DMA((GMM_RING,)), pltpu.SemaphoreType.DMA((GMM_RING,))],
    )
    return pl.pallas_call(
        _gmm_kernel,
        grid_spec=grid_spec,
        out_shape=jax.ShapeDtypeStruct((P, W), xs.dtype),
        compiler_params=pltpu.CompilerParams(dimension_semantics=("arbitrary",)),
        name="gmm",
    )(bstart, nb_e, counts, nused, w_gate, w_up, w_down, xs)


def _sc_gather(dest_c, ys):
    nch, K, C = dest_c.shape
    W = ys.shape[1]
    H = SC_GATHER_ROWS
    info = plsc.get_sparse_core_info()
    nw = info.num_cores * info.num_subcores
    assert nch % nw == 0, "token chunks must split evenly over the vector subcores"
    per_w = nch // nw
    nbuf = SC_GATHER_BUFS
    ahead = nbuf - 1
    mesh = plsc.VectorSubcoreMesh(core_axis_name="c", subcore_axis_name="s")
    items = [(kk, hh) for kk in range(K) for hh in range(C // H)]

    @functools.partial(
        pl.kernel, mesh=mesh,
        out_type=jax.ShapeDtypeStruct((K, nch * C, W), ys.dtype),
        scratch_types=([pltpu.VMEM((K, C), I32)] + [pltpu.VMEM((H, W), ys.dtype)] * nbuf
                       + [pltpu.SemaphoreType.DMA] * (2 * nbuf)),
        name="sc_gather",
    )
    def k(dest_hbm, ys_hbm, yk_hbm, idx_v, *rest):
        bufs, gsem, wsem = rest[:nbuf], rest[nbuf:2 * nbuf], rest[2 * nbuf:]
        wid = lax.axis_index("s") * info.num_cores + lax.axis_index("c")

        @pl.loop(0, per_w)
        def _(j):
            ch = wid * per_w + j
            pltpu.sync_copy(dest_hbm.at[ch], idx_v)

            def gather(i):
                kk, hh = items[i]
                return pltpu.async_copy(ys_hbm.at[idx_v.at[kk, pl.ds(hh * H, H)]], bufs[i % nbuf], gsem[i % nbuf])

            def write(i):
                kk, hh = items[i]
                return pltpu.async_copy(bufs[i % nbuf], yk_hbm.at[kk, pl.ds(ch * C + hh * H, H)], wsem[i % nbuf])

            n = len(items)
            g = {i: gather(i) for i in range(ahead)}
            w = {}
            for i in range(n):
                g[i].wait()
                w[i] = write(i)
                if i + ahead < n:
                    if i >= 1:
                        w.pop(i - 1).wait()
                    g[i + ahead] = gather(i + ahead)
            for i in sorted(w):
                w[i].wait()

    return k(dest_c, ys)


def _combine_kernel(yk_ref, g_ref, gate2_ref, xacc_ref, gfin_ref, o_ref, gpad_ref):
    @pl.when(pl.program_id(0) == 0)
    def _():
        gpad_ref[...] = jnp.zeros_like(gpad_ref)

    gpad_ref[0:TOP_K, :] = g_ref[...]
    gt = gpad_ref[...].T
    hi_mask = jnp.uint32(0xFFFF0000)
    acc_lo = acc_hi = None
    for k in range(TOP_K):
        p = yk_ref[k]
        g = gt[:, k:k + 1]
        lo = lax.bitcast_convert_type(p << 16, F32) * g
        hi = lax.bitcast_convert_type(p & hi_mask, F32) * g
        acc_lo = lo if k == 0 else acc_lo + lo
        acc_hi = hi if k == 0 else acc_hi + hi
    routed = jnp.concatenate([acc_lo, acc_hi], axis=1)
    x2 = xacc_ref[...] + gate2_ref[0] * routed
    o_ref[...] = _rms(x2) * gfin_ref[...]


def _combine(yk, gates, gate2, xacc, g_final, S):
    N, D = xacc.shape
    W = yk.shape[2]
    T = COMBINE_TILE
    spt = S // T
    return pl.pallas_call(
        _combine_kernel,
        grid=(N // T,),
        in_specs=[pl.BlockSpec((TOP_K, T, W), lambda i: (0, i, 0)),
                  pl.BlockSpec((TOP_K, T), lambda i: (0, i)),
                  pl.BlockSpec((1, 1, D), lambda i: (i // spt, 0, 0)),
                  pl.BlockSpec((T, D), lambda i: (i, 0)),
                  pl.BlockSpec((1, D), lambda i: (0, 0))],
        out_specs=pl.BlockSpec((T, D), lambda i: (i, 0)),
        out_shape=jax.ShapeDtypeStruct((N, D), F32),
        scratch_shapes=[pltpu.VMEM((LANES, T), F32)],
        compiler_params=pltpu.CompilerParams(dimension_semantics=("arbitrary",)),
        name="combine",
    )(yk, gates, gate2, xacc, g_final)


def _layer(xf, B, S, mod, g_mix, w_in, pool_w, pool_scale, w_out, g_ffn, w_router, router_bias,
           w_gate, w_up, w_down, ws_gate, ws_up, ws_down):
    N, D = xf.shape
    E = w_router.shape[1]
    pw = pool_scale.shape[0]
    shift1, scale1, gate1, shift2, scale2, gate2 = [m.reshape(B, 1, D) for m in jnp.split(mod, 6, axis=-1)]
    u, qkv = _in_proj(xf, g_mix.reshape(1, D), shift1, scale1, w_in.astype(BF16), S, pw)
    attn_outs = []
    for a, d in zip(qkv, ATT_DILATIONS):
        o, lse = _attention(a, S // d // ATT_BLOCK)
        shape = (N, ATT_GROUP_WIDTH) if d == 1 else (B * d, S // d, ATT_GROUP_WIDTH)
        attn_outs += [o.reshape(shape), lse.reshape(shape)]
    ng = pool_w.shape[0]
    pool_bd = jnp.einsum('gcd,gh->gchd', pool_w, jnp.eye(ng, dtype=pool_w.dtype)).reshape(pw, pw).astype(BF16)
    xacc, h2, eidx, gates, rank, counts = _mid(
        u, attn_outs, xf, pool_bd, pool_scale.reshape(1, pw), w_out.astype(BF16), gate1,
        g_ffn.reshape(1, D), shift2, scale2, gate2,
        ws_gate.astype(BF16), ws_up.astype(BF16), ws_down.astype(BF16),
        w_router.T.astype(BF16), router_bias.reshape(E, 1).astype(F32), S)
    bm = GMM_BLOCK
    nblk = N * TOP_K // bm + E
    nb_e = (counts[:, 0] + bm - 1) // bm
    bend = jnp.cumsum(nb_e)
    bstart = (bend - nb_e).astype(I32)
    nused = bend[-1:].astype(I32)
    xs, dest_c = _sc_dispatch(eidx, rank, bstart * bm, h2, nblk * bm)
    ys = _gmm(bstart, nb_e.astype(I32), counts[:, 0], nused, xs, w_gate, w_up, w_down)
    return _sc_gather(dest_c, ys), gates, gate2, xacc


def kernel(x, c, w_ada, b_ada, g_mix, w_in, pool_w, pool_scale, w_out, g_ffn, w_router, router_bias,
           w_gate, w_up, w_down, ws_gate, ws_up, ws_down, g_final):
    B, S, D = x.shape
    depth = w_ada.shape[0]
    assert depth == 1, "the final residual is fused with the final norm, so exactly one layer is supported"
    assert S % (ATT_DILATIONS[-1] * ATT_BLOCK) == 0
    assert all(S % t == 0 for t in (IN_TILE, MID_TILE, COMBINE_TILE, ATT_BLOCKS_PER_STEP * ATT_BLOCK))
    assert MID_TILE % ROUTE_TILE == 0 and ROUTE_TILE % SC_CHUNK == 0 and SC_CHUNK % SC_GATHER_ROWS == 0
    assert w_gate.shape[1] % GMM_EXPERTS_PER_STEP == 0 and max(POOL_WINDOWS) <= POOL_HALO
    xf = x.reshape(B * S, D)
    mod = _ada(c, w_ada[0], b_ada[0])
    yk, gates, gate2, xacc = _layer(
        xf, B, S, mod, g_mix[0], w_in[0], pool_w[0], pool_scale[0], w_out[0], g_ffn[0], w_router[0],
        router_bias[0], w_gate[0], w_up[0], w_down[0], ws_gate[0], ws_up[0], ws_down[0])
    out = _combine(yk, gates, gate2, xacc, g_final.reshape(1, D), S)
    return out.reshape(B, S, D)
```

```python
import functools

import jax
import jax.numpy as jnp
from jax import lax
from jax.experimental import pallas as pl
from jax.experimental.pallas import tpu as pltpu
from jax.experimental.pallas import tpu_sc as plsc

F32 = jnp.float32
BF16 = jnp.bfloat16
I32 = jnp.int32
U32 = jnp.uint32

LANES = 128
SINGLE_LOAD_STRIDE = 4
NORM_EPS = 1e-6
POOL_WINDOWS = (2, 4, 8, 16)
POOL_HALO = 16
ATT_DILATIONS = (1, 4, 16)
ATT_BLOCK = 128
ATT_HEADS_PER_GROUP = 4
ATT_HEAD_DIM = 64
ATT_GROUP_WIDTH = ATT_HEADS_PER_GROUP * ATT_HEAD_DIM
N_EXPERT_GROUPS = 8
TOPK_GROUPS = 4
TOP_K = 8
ROUTED_SCALE = 2.5

IN_TILE = 1024
ATT_BLOCKS_PER_STEP = 32
ATT_UNROLL = 16
MID_TILE = 512
ROUTE_TILE = 512
COMBINE_TILE = 512
GMM_BLOCK = 128
GMM_RING = 16
GMM_EXPERTS_PER_STEP = 1
SC_CHUNK = 128
SC_GATHER_ROWS = 32
SC_GATHER_BUFS = 6

NEG_INF = float("-inf")


def _sigmoid(v):
    return 1.0 / (1.0 + jnp.exp(-v))


def _rms(v):
    return v * lax.rsqrt(jnp.mean(v * v, axis=-1, keepdims=True) + NORM_EPS)


def _pack_bf16_pairs(v):
    n = v.shape[1] // 2
    lo = lax.bitcast_convert_type(v[:, :n].astype(BF16).astype(F32), U32)
    hi = lax.bitcast_convert_type(v[:, n:].astype(BF16).astype(F32), U32)
    return (hi & jnp.uint32(0xFFFF0000)) | (lo >> 16)


def _unpack_bf16_pairs(p):
    lo = lax.bitcast_convert_type(p << 16, F32).astype(BF16)
    hi = lax.bitcast_convert_type(p & jnp.uint32(0xFFFF0000), F32).astype(BF16)
    return lo, hi


def _ada_kernel(c_ref, w_ref, b_ref, o_ref):
    c = c_ref[...]
    cs = c * _sigmoid(c)
    o_ref[...] = jnp.dot(cs, w_ref[...], preferred_element_type=F32,
                         precision=lax.Precision.HIGHEST) + b_ref[...]


def _ada(c, w_ada, b_ada):
    B, D = c.shape
    W = w_ada.shape[1]
    tn = 1024
    return pl.pallas_call(
        _ada_kernel,
        grid=(W // tn,),
        in_specs=[pl.BlockSpec((B, D), lambda j: (0, 0)),
                  pl.BlockSpec((D, tn), lambda j: (0, j)),
                  pl.BlockSpec((1, tn), lambda j: (0, j))],
        out_specs=pl.BlockSpec((B, tn), lambda j: (0, j)),
        out_shape=jax.ShapeDtypeStruct((B, W), F32),
        name="ada",
    )(c, w_ada, b_ada.reshape(1, W))


def _in_kernel(x_ref, g_ref, sh_ref, sc_ref, w_ref, pool_ref, q0_ref, q1_ref, q2_ref, scr_ref, tmp_ref):
    h = _rms(x_ref[...]) * g_ref[...]
    h = h * (1.0 + sc_ref[0]) + sh_ref[0]
    hb = h.astype(BF16)
    tm = x_ref.shape[0]
    pw = pool_ref.shape[1]
    gw = ATT_GROUP_WIDTH
    pool_ref[...] = jnp.dot(hb, w_ref[:, 0:pw], preferred_element_type=F32)
    for g, (out, d) in enumerate(zip((q0_ref, q1_ref, q2_ref), ATT_DILATIONS)):
        for sec in range(3):
            c0 = pw + sec * 3 * gw + g * gw
            res = jnp.dot(hb, w_ref[:, c0:c0 + gw], preferred_element_type=F32)
            if d == 1:
                out[:, sec * gw:(sec + 1) * gw] = res.astype(BF16)
            else:
                for c in range(gw // LANES):
                    scr_ref[c] = res[:, c * LANES:(c + 1) * LANES]
                    src, f1 = scr_ref.at[c], 1
                    if d > SINGLE_LOAD_STRIDE:
                        f1 = SINGLE_LOAD_STRIDE
                        for q in range(f1):
                            tmp_ref[c, q * (tm // f1):(q + 1) * (tm // f1), :] = scr_ref[c, pl.ds(q, tm // f1, stride=f1), :]
                        src = tmp_ref.at[c]
                    for r in range(d):
                        r_lo, r_hi = r % f1, r // f1
                        c1 = sec * gw + c * LANES
                        out[r, :, c1:c1 + LANES] = src[pl.ds(r_lo * (tm // f1) + r_hi, tm // d, stride=d // f1),
                                                       :].astype(BF16)


def _in_proj(xf, g_mix, shift1, scale1, w_in_b, S, pool_width):
    N, D = xf.shape
    tm = IN_TILE
    spt = S // tm
    vec = lambda i: (i // spt, 0, 0)
    row = lambda i: (i, 0)
    gw3 = 3 * ATT_GROUP_WIDTH
    B = N // S
    res_spec = lambda d: pl.BlockSpec((d, tm // d, gw3), lambda i: (i // spt, i % spt, 0))
    res_shape = lambda d: jax.ShapeDtypeStruct((B * d, S // d, gw3), BF16)
    outs = pl.pallas_call(
        _in_kernel,
        grid=(N // tm,),
        in_specs=[pl.BlockSpec((tm, D), row),
                  pl.BlockSpec((1, D), lambda i: (0, 0)),
                  pl.BlockSpec((1, 1, D), vec),
                  pl.BlockSpec((1, 1, D), vec),
                  pl.BlockSpec(w_in_b.shape, lambda i: (0, 0))],
        out_specs=[pl.BlockSpec((tm, pool_width), row), pl.BlockSpec((tm, gw3), row)]
                  + [res_spec(d) for d in ATT_DILATIONS[1:]],
        out_shape=[jax.ShapeDtypeStruct((N, pool_width), F32), jax.ShapeDtypeStruct((N, gw3), BF16)]
                  + [res_shape(d) for d in ATT_DILATIONS[1:]],
        scratch_shapes=[pltpu.VMEM((ATT_GROUP_WIDTH // LANES, tm, LANES), F32)] * 2,
        name="in_proj",
    )(xf, g_mix, shift1, scale1, w_in_b)
    return outs[0], [o.reshape(N, gw3) for o in outs[1:]]


def _attn_kernel(nbs, a_ref, halo_ref, o_ref, lse_ref, kv_ref, band_ref):
    i = pl.program_id(0)
    R = a_ref.shape[0] // ATT_BLOCK
    gw = ATT_GROUP_WIDTH
    blk = ATT_BLOCK
    nh = ATT_HEADS_PER_GROUP
    kv_ref[0:blk, :] = halo_ref[:, gw:3 * gw]
    kv_ref[blk:, :] = a_ref[:, gw:3 * gw]
    row = lax.broadcasted_iota(I32, (nh * blk, 2 * blk), 0) % blk
    col = lax.broadcasted_iota(I32, (nh * blk, 2 * blk), 1)
    in_band = (col >= row) & (col <= row + blk)
    band_ref[0] = jnp.where(in_band, 0.0, NEG_INF)
    band_ref[1] = jnp.where(in_band & (col >= blk), 0.0, NEG_INF)
    head_of_lane = lax.broadcasted_iota(I32, (blk, gw), 1) // ATT_HEAD_DIM
    nt = (((1,), (1,)), ((), ()))

    def one_block(jj, start):
        r0 = pl.multiple_of(jj * blk, blk)
        qf = a_ref[pl.ds(r0, blk), 0:gw].astype(F32) * (ATT_HEAD_DIM ** -0.5)
        q4 = jnp.concatenate([jnp.where(head_of_lane == h, qf, 0.0) for h in range(nh)], axis=0).astype(BF16)
        kc = kv_ref[pl.ds(r0, 2 * blk), 0:gw]
        vc = kv_ref[pl.ds(r0, 2 * blk), gw:2 * gw]
        s = lax.dot_general(q4, kc, nt, preferred_element_type=F32) + band_ref[1 if start is True else 0]
        if start is not None and start is not True:
            s = jnp.where(col >= jnp.where(start, blk, 0), s, NEG_INF)
        m = jnp.max(s, axis=1, keepdims=True)
        p = jnp.exp(s - m)
        l = jnp.sum(p, axis=1, keepdims=True)
        o4 = jnp.dot(p.astype(BF16), vc, preferred_element_type=F32) / l
        lse4 = m + jnp.log(l)
        o = jnp.zeros((blk, gw), F32)
        lse = jnp.zeros((blk, gw), F32)
        for h in range(nh):
            hm = head_of_lane == h
            o = jnp.where(hm, o4[h * blk:(h + 1) * blk, :], o)
            lse = jnp.where(hm, lse4[h * blk:(h + 1) * blk, :], lse)
        o_ref[pl.ds(r0, blk), :] = o
        lse_ref[pl.ds(r0, blk), :] = lse

    U = ATT_UNROLL
    assert U % nbs == 0 or nbs % U == 0

    def body(it, carry):
        for j in range(U):
            if U % nbs == 0:
                start = True if j % nbs == 0 else None
            else:
                start = (((i * R + it * U) % nbs) == 0) if j == 0 else None
            one_block(it * U + j, start)
        return carry

    lax.fori_loop(0, R // U, body, 0)


def _attention(a, nbs):
    N = a.shape[0]
    R = ATT_BLOCKS_PER_STEP
    gw = ATT_GROUP_WIDTH
    tm = R * ATT_BLOCK
    return pl.pallas_call(
        functools.partial(_attn_kernel, nbs),
        grid=(N // tm,),
        in_specs=[pl.BlockSpec((tm, 3 * gw), lambda i: (i, 0)),
                  pl.BlockSpec((ATT_BLOCK, 3 * gw), lambda i: (jnp.maximum(i * R - 1, 0), 0))],
        out_specs=[pl.BlockSpec((tm, gw), lambda i: (i, 0))] * 2,
        out_shape=[jax.ShapeDtypeStruct((N, gw), F32)] * 2,
        scratch_shapes=[pltpu.VMEM((tm + ATT_BLOCK, 2 * gw), BF16),
                        pltpu.VMEM((2, ATT_HEADS_PER_GROUP * ATT_BLOCK, 2 * ATT_BLOCK), F32)],
        name="attn",
    )(a, a)


def _mid_kernel(spt, u_ref, uh_ref, o0_ref, l0_ref, o1_ref, l1_ref, o2_ref, l2_ref, x_ref,
                pbd_ref, psc_ref, wout_ref, gate1_ref, gffn_ref, sh2_ref, sc2_ref, gate2_ref,
                wsg_ref, wsu_ref, wsd_ref, wr_ref, bias_ref, before_ref,
                xacc_ref, h2_ref, e_ref, g_ref, r_ref, cnt_ref, til_ref, tmp_ref, base_ref):
    i = pl.program_id(0)
    tm, pw = u_ref.shape
    si = i % spt
    u = u_ref[...]
    keep = jnp.full((POOL_HALO, pw), si, I32) > 0
    ext = jnp.concatenate([jnp.where(keep, uh_ref[...], 0.0), u], axis=0)
    lane_grp = lax.broadcasted_iota(I32, (tm, pw), 1) // (pw // len(POOL_WINDOWS))
    pooled = jnp.zeros((tm, pw), F32)
    s, w = ext, 1
    while w < POOL_HALO:
        s = s + pltpu.roll(s, w, axis=0)
        w *= 2
        if w in POOL_WINDOWS:
            pooled = jnp.where(lane_grp == POOL_WINDOWS.index(w), s[POOL_HALO:, :], pooled)
    win = jnp.zeros((tm, pw), I32)
    for g, w in enumerate(POOL_WINDOWS):
        win = jnp.where(lane_grp == g, w, win)
    pos = si * tm + lax.broadcasted_iota(I32, (tm, pw), 0)
    cnt = jnp.minimum(pos + 1, win).astype(F32)
    pooled = pooled / cnt - u
    pool_out = jnp.dot(pooled.astype(BF16), pbd_ref[...], preferred_element_type=F32) * psc_ref[...]
    def token_order(slot, ref):
        d, n, w = ref.shape
        f1 = SINGLE_LOAD_STRIDE if d > SINGLE_LOAD_STRIDE else 1
        for c in range(w // LANES):
            cols = slice(c * LANES, (c + 1) * LANES)
            if f1 == 1:
                for r in range(d):
                    til_ref[slot, c, pl.ds(r, n, stride=d), :] = ref[r, :, cols]
            else:
                f2, slab = d // f1, tm // f1
                for r in range(d):
                    r_lo, r_hi = r % f1, r // f1
                    tmp_ref[c, pl.ds(r_lo * slab + r_hi, n, stride=f2), :] = ref[r, :, cols]
                for r_lo in range(f1):
                    til_ref[slot, c, pl.ds(r_lo, slab, stride=f1), :] = tmp_ref[c, r_lo * slab:(r_lo + 1) * slab, :]
        return jnp.concatenate([til_ref[slot, c] for c in range(w // LANES)], axis=1)

    l0 = l0_ref[...]
    l1 = token_order(0, l1_ref)
    l2 = token_order(1, l2_ref)
    m = jnp.maximum(jnp.maximum(l0, l1), l2)
    w0 = jnp.exp(l0 - m)
    w1 = jnp.exp(l1 - m)
    w2 = jnp.exp(l2 - m)
    attn = (w0 * o0_ref[...] + w1 * token_order(2, o1_ref) + w2 * token_order(3, o2_ref)) / (w0 + w1 + w2)
    mixed = jnp.dot(jnp.concatenate([pool_out.astype(BF16), attn.astype(BF16)], axis=1), wout_ref[...],
                    preferred_element_type=F32)
    x1 = x_ref[...] + gate1_ref[0] * mixed
    h2 = _rms(x1) * gffn_ref[...]
    h2 = h2 * (1.0 + sc2_ref[0]) + sh2_ref[0]
    h2_ref[...] = _pack_bf16_pairs(h2)
    hb = h2.astype(BF16)
    a = jnp.dot(hb, wsg_ref[...], preferred_element_type=F32)
    b = jnp.dot(hb, wsu_ref[...], preferred_element_type=F32)
    act = (a * _sigmoid(a)) * b
    shared = jnp.dot(act.astype(BF16), wsd_ref[...], preferred_element_type=F32)
    xacc_ref[...] = x1 + gate2_ref[0] * shared
    @pl.when(i == 0)
    def _():
        base_ref[...] = jnp.zeros_like(base_ref)

    C = e_ref.shape[2]
    for t0 in range(0, tm, ROUTE_TILE):
        idxs, gates, ranks = _route_tile(hb[t0:t0 + ROUTE_TILE, :], wr_ref, bias_ref, before_ref, base_ref)
        for k in range(TOP_K):
            g_ref[k:k + 1, t0:t0 + ROUTE_TILE] = gates[k]
            for c in range(ROUTE_TILE // C):
                e_ref[t0 // C + c, k:k + 1, :] = idxs[k][:, c * C:(c + 1) * C]
                r_ref[t0 // C + c, k:k + 1, :] = ranks[k][:, c * C:(c + 1) * C]
    cnt_ref[...] = base_ref[...].astype(I32)


def _mid(u, attn_outs, xf, pool_bd, pool_scale, w_out_b, gate1, g_ffn, shift2, scale2, gate2,
         wsg_b, wsu_b, wsd_b, wr_t, bias_col, S):
    N, D = xf.shape
    E = wr_t.shape[0]
    tok = jnp.arange(ROUTE_TILE, dtype=I32)
    before = (tok[:, None] < tok[None, :]).astype(BF16)
    pw = u.shape[1]
    tm = MID_TILE
    spt = S // tm
    row = lambda i: (i, 0)
    vec = lambda i: (i // spt, 0, 0)
    full = lambda a: pl.BlockSpec(a.shape, lambda i: (0,) * a.ndim)
    hpt = tm // POOL_HALO
    in_specs = [pl.BlockSpec((tm, pw), row),
                pl.BlockSpec((POOL_HALO, pw), lambda i: (jnp.maximum(i * hpt - 1, 0), 0))]
    gw = ATT_GROUP_WIDTH
    in_specs += [pl.BlockSpec((tm, gw), row)] * 2
    for d in ATT_DILATIONS[1:]:
        in_specs += [pl.BlockSpec((d, tm // d, gw), lambda i: (i // spt, i % spt, 0))] * 2
    in_specs += [pl.BlockSpec((tm, D), row), full(pool_bd), full(pool_scale), full(w_out_b),
                 pl.BlockSpec((1, 1, D), vec), full(g_ffn), pl.BlockSpec((1, 1, D), vec),
                 pl.BlockSpec((1, 1, D), vec), pl.BlockSpec((1, 1, D), vec),
                 full(wsg_b), full(wsu_b), full(wsd_b), full(wr_t), full(bias_col), full(before)]
    col = lambda i: (0, i)
    C = SC_CHUNK
    chunked = pl.BlockSpec((tm // C, TOP_K, C), lambda i: (i, 0, 0))
    return pl.pallas_call(
        functools.partial(_mid_kernel, spt),
        grid=(N // tm,),
        in_specs=in_specs,
        out_specs=[pl.BlockSpec((tm, D), row), pl.BlockSpec((tm, D // 2), row),
                   chunked, pl.BlockSpec((TOP_K, tm), col), chunked,
                   pl.BlockSpec((E, 1), lambda i: (0, 0))],
        out_shape=[jax.ShapeDtypeStruct((N, D), F32), jax.ShapeDtypeStruct((N, D // 2), U32),
                   jax.ShapeDtypeStruct((N // C, TOP_K, C), I32), jax.ShapeDtypeStruct((TOP_K, N), F32),
                   jax.ShapeDtypeStruct((N // C, TOP_K, C), I32), jax.ShapeDtypeStruct((E, 1), I32)],
        scratch_shapes=[pltpu.VMEM((4, gw // LANES, tm, LANES), F32), pltpu.VMEM((gw // LANES, tm, LANES), F32),
                        pltpu.VMEM((E, 1), F32)],
        compiler_params=pltpu.CompilerParams(dimension_semantics=("arbitrary",)),
        name="mid",
    )(u, u, *attn_outs, xf, pool_bd, pool_scale, w_out_b, gate1, g_ffn, shift2, scale2, gate2,
      wsg_b, wsu_b, wsd_b, wr_t, bias_col, before)


def _route_tile(hb, wr_ref, bias_ref, before_ref, base_ref):
    T = hb.shape[0]
    E = wr_ref.shape[0]
    gsz = E // N_EXPERT_GROUPS
    logits = lax.dot_general(wr_ref[...], hb, (((1,), (1,)), ((), ())), preferred_element_type=F32)
    scores = _sigmoid(logits)
    biased = scores + bias_ref[...]
    giota = lax.broadcasted_iota(I32, (gsz, T), 0)
    gscore = []
    for g in range(N_EXPERT_GROUPS):
        blk = biased[g * gsz:(g + 1) * gsz, :]
        m1 = jnp.max(blk, axis=0, keepdims=True)
        i1 = jnp.min(jnp.where(blk == m1, giota, gsz), axis=0, keepdims=True)
        m2 = jnp.max(jnp.where(giota == i1, NEG_INF, blk), axis=0, keepdims=True)
        gscore.append(m1 + m2)
    parts = []
    for g in range(N_EXPERT_GROUPS):
        beaten = jnp.zeros((1, T), I32)
        for o in range(N_EXPERT_GROUPS):
            if o == g:
                continue
            wins = (gscore[o] >= gscore[g]) if o < g else (gscore[o] > gscore[g])
            beaten = beaten + wins.astype(I32)
        keep = jnp.broadcast_to(beaten, (gsz, T)) < TOPK_GROUPS
        parts.append(jnp.where(keep, biased[g * gsz:(g + 1) * gsz, :], NEG_INF))
    cur = jnp.concatenate(parts, axis=0)
    eiota = lax.broadcasted_iota(I32, (E, T), 0)
    live = cur > NEG_INF
    idxs, gates, picks = [], [], []
    sub = lax.broadcasted_iota(I32, (8, T), 0)
    for k in range(TOP_K):
        nodes = [(cur[8 * j:8 * (j + 1), :], j, scores[8 * j:8 * (j + 1), :]) for j in range(E // 8)]
        while len(nodes) > 1:
            merged = []
            for p in range(0, len(nodes), 2):
                (va, ta, sa), (vb, tb, sb) = nodes[p], nodes[p + 1]
                later = vb > va
                merged.append((jnp.where(later, vb, va), jnp.where(later, tb, ta), jnp.where(later, sb, sa)))
            nodes = merged
        v8, t8, s8 = nodes[0]
        e8 = t8 * 8 + sub
        m = jnp.max(v8, axis=0, keepdims=True)
        idx = jnp.min(jnp.where(v8 == m, e8, E), axis=0, keepdims=True)
        oh = eiota == idx
        won = e8 == idx
        gates.append(jnp.sum(jnp.where(won, s8, 0.0), axis=0, keepdims=True))
        idxs.append(idx)
        picks.append((t8, won))
        cur = jnp.where(oh, NEG_INF, cur)
    selm = jnp.where(live & (cur == NEG_INF), 1.0, 0.0)
    gsum = gates[0]
    for k in range(1, TOP_K):
        gsum = gsum + gates[k]
    gates = [gk / gsum * ROUTED_SCALE for gk in gates]
    tot = jnp.dot(selm.astype(BF16), before_ref[...], preferred_element_type=F32) + base_ref[...]
    tiles = [tot[8 * j:8 * (j + 1), :] for j in range(E // 8)]
    ranks = []
    for t8, won in picks:
        level, bit = tiles, 0
        while len(level) > 1:
            odd = ((t8 >> bit) & 1) == 1
            level = [jnp.where(odd, level[p + 1], level[p]) for p in range(0, len(level), 2)]
            bit += 1
        ranks.append(jnp.sum(jnp.where(won, level[0], 0.0), axis=0, keepdims=True).astype(I32))
    base_ref[...] = base_ref[...] + jnp.sum(selm, axis=1, keepdims=True)
    return idxs, gates, ranks


def _sc_dispatch(eidx_c, rank_c, offs, h2p, P):
    N, W = h2p.shape
    nch, K, C = eidx_c.shape
    E = offs.shape[0]
    info = plsc.get_sparse_core_info()
    nw = info.num_cores * info.num_subcores
    L = info.num_lanes
    assert nch % nw == 0, "token chunks must split evenly over the vector subcores"
    per_w = nch // nw
    mesh = plsc.VectorSubcoreMesh(core_axis_name="c", subcore_axis_name="s")

    @functools.partial(
        pl.kernel, mesh=mesh,
        out_type=[jax.ShapeDtypeStruct((P, W), h2p.dtype), jax.ShapeDtypeStruct((nch, K, C), I32)],
        scratch_types=[pltpu.VMEM((E,), I32), pltpu.VMEM((K, C), I32), pltpu.VMEM((K, C), I32),
                       pltpu.VMEM((K, C), I32), pltpu.VMEM((C, W), h2p.dtype), pltpu.SemaphoreType.DMA],
        compiler_params=pltpu.CompilerParams(needs_layout_passes=False),
        name="sc_dispatch",
    )
    def k(e_hbm, r_hbm, off_hbm, h_hbm, xs_hbm, dest_hbm, off_v, e_v, r_v, idx_v, rows_v, sem):
        wid = lax.axis_index("s") * info.num_cores + lax.axis_index("c")
        pltpu.sync_copy(off_hbm, off_v)

        @pl.loop(0, per_w)
        def _(j):
            ch = wid * per_w + j
            pltpu.sync_copy(e_hbm.at[ch], e_v)
            pltpu.sync_copy(r_hbm.at[ch], r_v)
            for kk in range(K):
                for q in range(C // L):
                    sl = pl.ds(q * L, L)
                    idx_v[kk, sl] = plsc.load_gather(off_v, [e_v[kk, sl]]) + r_v[kk, sl]
            pltpu.sync_copy(idx_v, dest_hbm.at[ch])
            pltpu.sync_copy(h_hbm.at[pl.ds(ch * C, C)], rows_v)
            copies = [pltpu.async_copy(rows_v, xs_hbm.at[idx_v.at[kk]], sem) for kk in range(K)]
            for cp in copies:
                cp.wait()

    return k(eidx_c, rank_c, offs, h2p)


def _gmm_kernel(bstart_ref, nbe_ref, cnt_ref, nu_ref, wg_ref, wu_ref, wd_ref, xs_hbm, ys_hbm,
                wgb, wub, wdb, xbuf, ybuf, xsem, ysem):
    step = pl.program_id(0)
    last = pl.num_programs(0) - 1
    epg = wg_ref.shape[0]
    ring, bm = xbuf.shape[0], xbuf.shape[1]
    nblk = ys_hbm.shape[0] // bm
    nused = nu_ref[0]

    def x_copy(b, slot):
        return pltpu.make_async_copy(xs_hbm.at[pl.ds(pl.multiple_of(b * bm, bm), bm), :], xbuf.at[slot],
                                     xsem.at[slot])

    def y_copy(b, slot):
        return pltpu.make_async_copy(ybuf.at[slot], ys_hbm.at[pl.ds(pl.multiple_of(b * bm, bm), bm), :],
                                     ysem.at[slot])

    @pl.when(step == 0)
    def _():
        for j in range(ring - 1):
            @pl.when(j < nused)
            def _():
                x_copy(j, j).start()

    def run_expert(ee):
        e = step * epg + ee
        b0 = bstart_ref[e]
        nb = nbe_ref[e]

        @pl.when(nb > 0)
        def _():
            def prefetch(t):
                @pl.when(t < nused)
                def _():
                    x_copy(t, jnp.bitwise_and(t, ring - 1)).start()

            def process(b, n, fresh):
                slots = [jnp.bitwise_and(b + j, ring - 1) for j in range(n)]
                for j in range(n):
                    x_copy(b + j, slots[j]).wait()
                prefetch(b + ring - 1)
                for j in range(n):
                    @pl.when(b + j >= ring)
                    def _():
                        y_copy(b + j - ring, slots[j]).wait()
                rows = lax.broadcasted_iota(I32, (n * bm, 1), 0)
                valid = cnt_ref[e] - (b - b0) * bm
                xp = jnp.concatenate([xbuf[s] for s in slots], axis=0)
                xb = jnp.concatenate(_unpack_bf16_pairs(jnp.where(rows < valid, xp, jnp.uint32(0))), axis=1)
                if fresh:
                    wg, wu, wd = (w[ee].astype(BF16) for w in (wg_ref, wu_ref, wd_ref))
                    wgb[...], wub[...], wdb[...] = wg, wu, wd
                else:
                    wg, wu, wd = wgb[...], wub[...], wdb[...]
                a = jnp.dot(xb, wg, preferred_element_type=F32)
                g = jnp.dot(xb, wu, preferred_element_type=F32)
                act = (a * _sigmoid(a)) * g
                yp = _pack_bf16_pairs(jnp.dot(act.astype(BF16), wd, preferred_element_type=F32))
                for j in range(n):
                    ybuf[slots[j]] = yp[j * bm:(j + 1) * bm, :]
                    y_copy(b + j, slots[j]).start()
                for j in range(1, n):
                    prefetch(b + ring - 1 + j)

            quads = lax.shift_right_logical(nb, 2)
            lax.fori_loop(0, quads, lambda j, c: (process(b0 + 4 * j, 4, True), c)[1], 0)
            rest2 = jnp.bitwise_and(nb, 2)
            rest1 = jnp.bitwise_and(nb, 1)
            for n, rest, first, have_copy in ((2, rest2, b0 + 4 * quads, quads > 0),
                                              (1, rest1, b0 + 4 * quads + rest2, nb > 1)):
                @pl.when((rest != 0) & have_copy)
                def _():
                    process(first, n, False)

                @pl.when((rest != 0) & jnp.logical_not(have_copy))
                def _():
                    process(first, n, True)

    for ee in range(epg):
        run_expert(ee)

    @pl.when(step == last)
    def _():
        for back in range(ring, 0, -1):
            @pl.when(nused >= back)
            def _():
                y_copy(nused - back, jnp.bitwise_and(nused - back, ring - 1)).wait()
        ybuf[0] = jnp.zeros(ybuf.shape[1:], ybuf.dtype)
        lax.fori_loop(nused, nblk, lambda b, c: (y_copy(b, 0).start(), c)[1], 0)
        lax.fori_loop(nused, nblk, lambda b, c: (y_copy(b, 0).wait(), c)[1], 0)


def _gmm(bstart, nb_e, counts, nused, xs, w_gate, w_up, w_down):
    P, W = xs.shape
    E, D, F = w_gate.shape
    bm = GMM_BLOCK
    epg = GMM_EXPERTS_PER_STEP
    wsel = lambda s, *_: (s, 0, 0)
    grid_spec = pltpu.PrefetchScalarGridSpec(
        num_scalar_prefetch=4,
        grid=(E // epg,),
        in_specs=[pl.BlockSpec((epg, D, F), wsel), pl.BlockSpec((epg, D, F), wsel), pl.BlockSpec((epg, F, D), wsel),
                  pl.BlockSpec(memory_space=pl.ANY)],
        out_specs=pl.BlockSpec(memory_space=pl.ANY),
        scratch_shapes=[pltpu.VMEM((D, F), BF16), pltpu.VMEM((D, F), BF16), pltpu.VMEM((F, D), BF16),
                        pltpu.VMEM((GMM_RING, bm, W), xs.dtype), pltpu.VMEM((GMM_RING, bm, W), xs.dtype),
                        pltpu.SemaphoreType.DMA((GMM_RING,)), pltpu.SemaphoreType.DMA((GMM_RING,))],
    )
    return pl.pallas_call(
        _gmm_kernel,
        grid_spec=grid_spec,
        out_shape=jax.ShapeDtypeStruct((P, W), xs.dtype),
        compiler_params=pltpu.CompilerParams(dimension_semantics=("arbitrary",)),
        name="gmm",
    )(bstart, nb_e, counts, nused, w_gate, w_up, w_down, xs)


def _sc_gather(dest_c, ys):
    nch, K, C = dest_c.shape
    W = ys.shape[1]
    H = SC_GATHER_ROWS
    info = plsc.get_sparse_core_info()
    nw = info.num_cores * info.num_subcores
    assert nch % nw == 0, "token chunks must split evenly over the vector subcores"
    per_w = nch // nw
    nbuf = SC_GATHER_BUFS
    ahead = nbuf - 1
    mesh = plsc.VectorSubcoreMesh(core_axis_name="c", subcore_axis_name="s")
    items = [(kk, hh) for kk in range(K) for hh in range(C // H)]

    @functools.partial(
        pl.kernel, mesh=mesh,
        out_type=jax.ShapeDtypeStruct((K, nch * C, W), ys.dtype),
        scratch_types=([pltpu.VMEM((K, C), I32)] + [pltpu.VMEM((H, W), ys.dtype)] * nbuf
                       + [pltpu.SemaphoreType.DMA] * (2 * nbuf)),
        name="sc_gather",
    )
    def k(dest_hbm, ys_hbm, yk_hbm, idx_v, *rest):
        bufs, gsem, wsem = rest[:nbuf], rest[nbuf:2 * nbuf], rest[2 * nbuf:]
        wid = lax.axis_index("s") * info.num_cores + lax.axis_index("c")

        @pl.loop(0, per_w)
        def _(j):
            ch = wid * per_w + j
            pltpu.sync_copy(dest_hbm.at[ch], idx_v)

            def gather(i):
                kk, hh = items[i]
                return pltpu.async_copy(ys_hbm.at[idx_v.at[kk, pl.ds(hh * H, H)]], bufs[i % nbuf], gsem[i % nbuf])

            def write(i):
                kk, hh = items[i]
                return pltpu.async_copy(bufs[i % nbuf], yk_hbm.at[kk, pl.ds(ch * C + hh * H, H)], wsem[i % nbuf])

            n = len(items)
            g = {i: gather(i) for i in range(ahead)}
            w = {}
            for i in range(n):
                g[i].wait()
                w[i] = write(i)
                if i + ahead < n:
                    if i >= 1:
                        w.pop(i - 1).wait()
                    g[i + ahead] = gather(i + ahead)
            for i in sorted(w):
                w[i].wait()

    return k(dest_c, ys)


def _combine_kernel(yk_ref, g_ref, gate2_ref, xacc_ref, gfin_ref, o_ref, gpad_ref):
    @pl.when(pl.program_id(0) == 0)
    def _():
        gpad_ref[...] = jnp.zeros_like(gpad_ref)

    gpad_ref[0:TOP_K, :] = g_ref[...]
    gt = gpad_ref[...].T
    hi_mask = jnp.uint32(0xFFFF0000)
    acc_lo = acc_hi = None
    for k in range(TOP_K):
        p = yk_ref[k]
        g = gt[:, k:k + 1]
        lo = lax.bitcast_convert_type(p << 16, F32) * g
        hi = lax.bitcast_convert_type(p & hi_mask, F32) * g
        acc_lo = lo if k == 0 else acc_lo + lo
        acc_hi = hi if k == 0 else acc_hi + hi
    routed = jnp.concatenate([acc_lo, acc_hi], axis=1)
    x2 = xacc_ref[...] + gate2_ref[0] * routed
    o_ref[...] = _rms(x2) * gfin_ref[...]


def _combine(yk, gates, gate2, xacc, g_final, S):
    N, D = xacc.shape
    W = yk.shape[2]
    T = COMBINE_TILE
    spt = S // T
    return pl.pallas_call(
        _combine_kernel,
        grid=(N // T,),
        in_specs=[pl.BlockSpec((TOP_K, T, W), lambda i: (0, i, 0)),
                  pl.BlockSpec((TOP_K, T), lambda i: (0, i)),
                  pl.BlockSpec((1, 1, D), lambda i: (i // spt, 0, 0)),
                  pl.BlockSpec((T, D), lambda i: (i, 0)),
                  pl.BlockSpec((1, D), lambda i: (0, 0))],
        out_specs=pl.BlockSpec((T, D), lambda i: (i, 0)),
        out_shape=jax.ShapeDtypeStruct((N, D), F32),
        scratch_shapes=[pltpu.VMEM((LANES, T), F32)],
        compiler_params=pltpu.CompilerParams(dimension_semantics=("arbitrary",)),
        name="combine",
    )(yk, gates, gate2, xacc, g_final)


def _layer(xf, B, S, mod, g_mix, w_in, pool_w, pool_scale, w_out, g_ffn, w_router, router_bias,
           w_gate, w_up, w_down, ws_gate, ws_up, ws_down):
    N, D = xf.shape
    E = w_router.shape[1]
    pw = pool_scale.shape[0]
    shift1, scale1, gate1, shift2, scale2, gate2 = [m.reshape(B, 1, D) for m in jnp.split(mod, 6, axis=-1)]
    u, qkv = _in_proj(xf, g_mix.reshape(1, D), shift1, scale1, w_in.astype(BF16), S, pw)
    attn_outs = []
    for a, d in zip(qkv, ATT_DILATIONS):
        o, lse = _attention(a, S // d // ATT_BLOCK)
        shape = (N, ATT_GROUP_WIDTH) if d == 1 else (B * d, S // d, ATT_GROUP_WIDTH)
        attn_outs += [o.reshape(shape), lse.reshape(shape)]
    ng = pool_w.shape[0]
    pool_bd = jnp.einsum('gcd,gh->gchd', pool_w, jnp.eye(ng, dtype=pool_w.dtype)).reshape(pw, pw).astype(BF16)
    xacc, h2, eidx, gates, rank, counts = _mid(
        u, attn_outs, xf, pool_bd, pool_scale.reshape(1, pw), w_out.astype(BF16), gate1,
        g_ffn.reshape(1, D), shift2, scale2, gate2,
        ws_gate.astype(BF16), ws_up.astype(BF16), ws_down.astype(BF16),
        w_router.T.astype(BF16), router_bias.reshape(E, 1).astype(F32), S)
    bm = GMM_BLOCK
    nblk = N * TOP_K // bm + E
    nb_e = (counts[:, 0] + bm - 1) // bm
    bend = jnp.cumsum(nb_e)
    bstart = (bend - nb_e).astype(I32)
    nused = bend[-1:].astype(I32)
    xs, dest_c = _sc_dispatch(eidx, rank, bstart * bm, h2, nblk * bm)
    ys = _gmm(bstart, nb_e.astype(I32), counts[:, 0], nused, xs, w_gate, w_up, w_down)
    return _sc_gather(dest_c, ys), gates, gate2, xacc


def kernel(x, c, w_ada, b_ada, g_mix, w_in, pool_w, pool_scale, w_out, g_ffn, w_router, router_bias,
           w_gate, w_up, w_down, ws_gate, ws_up, ws_down, g_final):
    B, S, D = x.shape
    depth = w_ada.shape[0]
    assert depth == 1, "the final residual is fused with the final norm, so exactly one layer is supported"
    assert S % (ATT_DILATIONS[-1] * ATT_BLOCK) == 0
    assert all(S % t == 0 for t in (IN_TILE, MID_TILE, COMBINE_TILE, ATT_BLOCKS_PER_STEP * ATT_BLOCK))
    assert MID_TILE % ROUTE_TILE == 0 and ROUTE_TILE % SC_CHUNK == 0 and SC_CHUNK % SC_GATHER_ROWS == 0
    assert w_gate.shape[1] % GMM_EXPERTS_PER_STEP == 0 and max(POOL_WINDOWS) <= POOL_HALO
    xf = x.reshape(B * S, D)
    mod = _ada(c, w_ada[0], b_ada[0])
    yk, gates, gate2, xacc = _layer(
        xf, B, S, mod, g_mix[0], w_in[0], pool_w[0], pool_scale[0], w_out[0], g_ffn[0], w_router[0],
        router_bias[0], w_gate[0], w_up[0], w_down[0], ws_gate[0], ws_up[0], ws_down[0])
    out = _combine(yk, gates, gate2, xacc, g_final.reshape(1, D), S)
    return out.reshape(B, S, D)
```

```python
import functools

import jax
import jax.numpy as jnp
from jax import lax
from jax.experimental import pallas as pl
from jax.experimental.pallas import tpu as pltpu
from jax.experimental.pallas import tpu_sc as plsc

F32 = jnp.float32
BF16 = jnp.bfloat16
I32 = jnp.int32
U32 = jnp.uint32

LANES = 128
SINGLE_LOAD_STRIDE = 4
NORM_EPS = 1e-6
POOL_WINDOWS = (2, 4, 8, 16)
POOL_HALO = 16
ATT_DILATIONS = (1, 4, 16)
ATT_BLOCK = 128
ATT_HEADS_PER_GROUP = 4
ATT_HEAD_DIM = 64
ATT_GROUP_WIDTH = ATT_HEADS_PER_GROUP * ATT_HEAD_DIM
N_EXPERT_GROUPS = 8
TOPK_GROUPS = 4
TOP_K = 8
ROUTED_SCALE = 2.5

IN_TILE = 1024
ATT_BLOCKS_PER_STEP = 16
ATT_UNROLL = 16
MID_TILE = 512
ROUTE_TILE = 512
COMBINE_TILE = 512
GMM_BLOCK = 128
GMM_RING = 16
GMM_EXPERTS_PER_STEP = 1
SC_CHUNK = 128
SC_GATHER_ROWS = 32
SC_GATHER_BUFS = 6

NEG_INF = float("-inf")


def _sigmoid(v):
    return 1.0 / (1.0 + jnp.exp(-v))


def _rms(v):
    return v * lax.rsqrt(jnp.mean(v * v, axis=-1, keepdims=True) + NORM_EPS)


def _pack_bf16_pairs(v):
    n = v.shape[1] // 2
    lo = lax.bitcast_convert_type(v[:, :n].astype(BF16).astype(F32), U32)
    hi = lax.bitcast_convert_type(v[:, n:].astype(BF16).astype(F32), U32)
    return (hi & jnp.uint32(0xFFFF0000)) | (lo >> 16)


def _unpack_bf16_pairs(p):
    lo = lax.bitcast_convert_type(p << 16, F32).astype(BF16)
    hi = lax.bitcast_convert_type(p & jnp.uint32(0xFFFF0000), F32).astype(BF16)
    return lo, hi


def _ada_kernel(c_ref, w_ref, b_ref, o_ref):
    c = c_ref[...]
    cs = c * _sigmoid(c)
    o_ref[...] = jnp.dot(cs, w_ref[...], preferred_element_type=F32,
                         precision=lax.Precision.HIGHEST) + b_ref[...]


def _ada(c, w_ada, b_ada):
    B, D = c.shape
    W = w_ada.shape[1]
    tn = 1024
    return pl.pallas_call(
        _ada_kernel,
        grid=(W // tn,),
        in_specs=[pl.BlockSpec((B, D), lambda j: (0, 0)),
                  pl.BlockSpec((D, tn), lambda j: (0, j)),
                  pl.BlockSpec((1, tn), lambda j: (0, j))],
        out_specs=pl.BlockSpec((B, tn), lambda j: (0, j)),
        out_shape=jax.ShapeDtypeStruct((B, W), F32),
        name="ada",
    )(c, w_ada, b_ada.reshape(1, W))


def _in_kernel(x_ref, g_ref, sh_ref, sc_ref, w_ref, pool_ref, q0_ref, q1_ref, q2_ref, scr_ref, tmp_ref):
    h = _rms(x_ref[...]) * g_ref[...]
    h = h * (1.0 + sc_ref[0]) + sh_ref[0]
    hb = h.astype(BF16)
    tm = x_ref.shape[0]
    pw = pool_ref.shape[1]
    gw = ATT_GROUP_WIDTH
    pool_ref[...] = jnp.dot(hb, w_ref[:, 0:pw], preferred_element_type=F32)
    for g, (out, d) in enumerate(zip((q0_ref, q1_ref, q2_ref), ATT_DILATIONS)):
        for sec in range(3):
            c0 = pw + sec * 3 * gw + g * gw
            res = jnp.dot(hb, w_ref[:, c0:c0 + gw], preferred_element_type=F32)
            if d == 1:
                out[:, sec * gw:(sec + 1) * gw] = res.astype(BF16)
            else:
                for c in range(gw // LANES):
                    scr_ref[c] = res[:, c * LANES:(c + 1) * LANES]
                    src, f1 = scr_ref.at[c], 1
                    if d > SINGLE_LOAD_STRIDE:
                        f1 = SINGLE_LOAD_STRIDE
                        for q in range(f1):
                            tmp_ref[c, q * (tm // f1):(q + 1) * (tm // f1), :] = scr_ref[c, pl.ds(q, tm // f1, stride=f1), :]
                        src = tmp_ref.at[c]
                    for r in range(d):
                        r_lo, r_hi = r % f1, r // f1
                        c1 = sec * gw + c * LANES
                        out[r, :, c1:c1 + LANES] = src[pl.ds(r_lo * (tm // f1) + r_hi, tm // d, stride=d // f1),
                                                       :].astype(BF16)


def _in_proj(xf, g_mix, shift1, scale1, w_in_b, S, pool_width):
    N, D = xf.shape
    tm = IN_TILE
    spt = S // tm
    vec = lambda i: (i // spt, 0, 0)
    row = lambda i: (i, 0)
    gw3 = 3 * ATT_GROUP_WIDTH
    B = N // S
    res_spec = lambda d: pl.BlockSpec((d, tm // d, gw3), lambda i: (i // spt, i % spt, 0))
    res_shape = lambda d: jax.ShapeDtypeStruct((B * d, S // d, gw3), BF16)
    outs = pl.pallas_call(
        _in_kernel,
        grid=(N // tm,),
        in_specs=[pl.BlockSpec((tm, D), row),
                  pl.BlockSpec((1, D), lambda i: (0, 0)),
                  pl.BlockSpec((1, 1, D), vec),
                  pl.BlockSpec((1, 1, D), vec),
                  pl.BlockSpec(w_in_b.shape, lambda i: (0, 0))],
        out_specs=[pl.BlockSpec((tm, pool_width), row), pl.BlockSpec((tm, gw3), row)]
                  + [res_spec(d) for d in ATT_DILATIONS[1:]],
        out_shape=[jax.ShapeDtypeStruct((N, pool_width), F32), jax.ShapeDtypeStruct((N, gw3), BF16)]
                  + [res_shape(d) for d in ATT_DILATIONS[1:]],
        scratch_shapes=[pltpu.VMEM((ATT_GROUP_WIDTH // LANES, tm, LANES), F32)] * 2,
        name="in_proj",
    )(xf, g_mix, shift1, scale1, w_in_b)
    return outs[0], [o.reshape(N, gw3) for o in outs[1:]]


def _attn_kernel(nbs, a_ref, halo_ref, o_ref, lse_ref, kv_ref, band_ref):
    i = pl.program_id(0)
    R = a_ref.shape[0] // ATT_BLOCK
    gw = ATT_GROUP_WIDTH
    blk = ATT_BLOCK
    nh = ATT_HEADS_PER_GROUP
    kv_ref[0:blk, :] = halo_ref[:, gw:3 * gw]
    kv_ref[blk:, :] = a_ref[:, gw:3 * gw]
    row = lax.broadcasted_iota(I32, (nh * blk, 2 * blk), 0) % blk
    col = lax.broadcasted_iota(I32, (nh * blk, 2 * blk), 1)
    in_band = (col >= row) & (col <= row + blk)
    band_ref[0] = jnp.where(in_band, 0.0, NEG_INF)
    band_ref[1] = jnp.where(in_band & (col >= blk), 0.0, NEG_INF)
    head_of_lane = lax.broadcasted_iota(I32, (blk, gw), 1) // ATT_HEAD_DIM
    nt = (((1,), (1,)), ((), ()))

    def one_block(jj, start):
        r0 = pl.multiple_of(jj * blk, blk)
        qf = a_ref[pl.ds(r0, blk), 0:gw].astype(F32) * (ATT_HEAD_DIM ** -0.5)
        q4 = jnp.concatenate([jnp.where(head_of_lane == h, qf, 0.0) for h in range(nh)], axis=0).astype(BF16)
        kc = kv_ref[pl.ds(r0, 2 * blk), 0:gw]
        vc = kv_ref[pl.ds(r0, 2 * blk), gw:2 * gw]
        s = lax.dot_general(q4, kc, nt, preferred_element_type=F32) + band_ref[1 if start is True else 0]
        if start is not None and start is not True:
            s = jnp.where(col >= jnp.where(start, blk, 0), s, NEG_INF)
        m = jnp.max(s, axis=1, keepdims=True)
        p = jnp.exp(s - m)
        l = jnp.sum(p, axis=1, keepdims=True)
        o4 = jnp.dot(p.astype(BF16), vc, preferred_element_type=F32) / l
        lse4 = m + jnp.log(l)
        o = jnp.zeros((blk, gw), F32)
        lse = jnp.zeros((blk, gw), F32)
        for h in range(nh):
            hm = head_of_lane == h
            o = jnp.where(hm, o4[h * blk:(h + 1) * blk, :], o)
            lse = jnp.where(hm, lse4[h * blk:(h + 1) * blk, :], lse)
        o_ref[pl.ds(r0, blk), :] = o
        lse_ref[pl.ds(r0, blk), :] = lse

    U = ATT_UNROLL
    assert U % nbs == 0 or nbs % U == 0

    def body(it, carry):
        for j in range(U):
            if U % nbs == 0:
                start = True if j % nbs == 0 else None
            else:
                start = (((i * R + it * U) % nbs) == 0) if j == 0 else None
            one_block(it * U + j, start)
        return carry

    lax.fori_loop(0, R // U, body, 0)


def _attention(a, nbs):
    N = a.shape[0]
    R = ATT_BLOCKS_PER_STEP
    gw = ATT_GROUP_WIDTH
    tm = R * ATT_BLOCK
    return pl.pallas_call(
        functools.partial(_attn_kernel, nbs),
        grid=(N // tm,),
        in_specs=[pl.BlockSpec((tm, 3 * gw), lambda i: (i, 0)),
                  pl.BlockSpec((ATT_BLOCK, 3 * gw), lambda i: (jnp.maximum(i * R - 1, 0), 0))],
        out_specs=[pl.BlockSpec((tm, gw), lambda i: (i, 0))] * 2,
        out_shape=[jax.ShapeDtypeStruct((N, gw), F32)] * 2,
        scratch_shapes=[pltpu.VMEM((tm + ATT_BLOCK, 2 * gw), BF16),
                        pltpu.VMEM((2, ATT_HEADS_PER_GROUP * ATT_BLOCK, 2 * ATT_BLOCK), F32)],
        name="attn",
    )(a, a)


def _mid_kernel(spt, u_ref, uh_ref, o0_ref, l0_ref, o1_ref, l1_ref, o2_ref, l2_ref, x_ref,
                pbd_ref, psc_ref, wout_ref, gate1_ref, gffn_ref, sh2_ref, sc2_ref,
                wr_ref, bias_ref, before_ref,
                xacc_ref, h2_ref, e_ref, g_ref, r_ref, cnt_ref, til_ref, tmp_ref, base_ref):
    i = pl.program_id(0)
    tm, pw = u_ref.shape
    si = i % spt
    u = u_ref[...]
    keep = jnp.full((POOL_HALO, pw), si, I32) > 0
    ext = jnp.concatenate([jnp.where(keep, uh_ref[...], 0.0), u], axis=0)
    lane_grp = lax.broadcasted_iota(I32, (tm, pw), 1) // (pw // len(POOL_WINDOWS))
    pooled = jnp.zeros((tm, pw), F32)
    s, w = ext, 1
    while w < POOL_HALO:
        s = s + pltpu.roll(s, w, axis=0)
        w *= 2
        if w in POOL_WINDOWS:
            pooled = jnp.where(lane_grp == POOL_WINDOWS.index(w), s[POOL_HALO:, :], pooled)
    win = jnp.zeros((tm, pw), I32)
    for g, w in enumerate(POOL_WINDOWS):
        win = jnp.where(lane_grp == g, w, win)
    pos = si * tm + lax.broadcasted_iota(I32, (tm, pw), 0)
    cnt = jnp.minimum(pos + 1, win).astype(F32)
    pooled = pooled / cnt - u
    pool_out = jnp.dot(pooled.astype(BF16), pbd_ref[...], preferred_element_type=F32) * psc_ref[...]
    def token_order(slot, ref):
        d, n, w = ref.shape
        f1 = SINGLE_LOAD_STRIDE if d > SINGLE_LOAD_STRIDE else 1
        for c in range(w // LANES):
            cols = slice(c * LANES, (c + 1) * LANES)
            if f1 == 1:
                for r in range(d):
                    til_ref[slot, c, pl.ds(r, n, stride=d), :] = ref[r, :, cols]
            else:
                f2, slab = d // f1, tm // f1
                for r in range(d):
                    r_lo, r_hi = r % f1, r // f1
                    tmp_ref[c, pl.ds(r_lo * slab + r_hi, n, stride=f2), :] = ref[r, :, cols]
                for r_lo in range(f1):
                    til_ref[slot, c, pl.ds(r_lo, slab, stride=f1), :] = tmp_ref[c, r_lo * slab:(r_lo + 1) * slab, :]
        return jnp.concatenate([til_ref[slot, c] for c in range(w // LANES)], axis=1)

    l0 = l0_ref[...]
    l1 = token_order(0, l1_ref)
    l2 = token_order(1, l2_ref)
    m = jnp.maximum(jnp.maximum(l0, l1), l2)
    w0 = jnp.exp(l0 - m)
    w1 = jnp.exp(l1 - m)
    w2 = jnp.exp(l2 - m)
    attn = (w0 * o0_ref[...] + w1 * token_order(2, o1_ref) + w2 * token_order(3, o2_ref)) / (w0 + w1 + w2)
    mixed = jnp.dot(jnp.concatenate([pool_out.astype(BF16), attn.astype(BF16)], axis=1), wout_ref[...],
                    preferred_element_type=F32)
    x1 = x_ref[...] + gate1_ref[0] * mixed
    h2 = _rms(x1) * gffn_ref[...]
    h2 = h2 * (1.0 + sc2_ref[0]) + sh2_ref[0]
    h2_ref[...] = _pack_bf16_pairs(h2)
    hb = h2.astype(BF16)
    xacc_ref[...] = x1
    @pl.when(i == 0)
    def _():
        base_ref[...] = jnp.zeros_like(base_ref)

    C = e_ref.shape[2]
    for t0 in range(0, tm, ROUTE_TILE):
        idxs, gates, ranks = _route_tile(hb[t0:t0 + ROUTE_TILE, :], wr_ref, bias_ref, before_ref, base_ref)
        for k in range(TOP_K):
            g_ref[k:k + 1, t0:t0 + ROUTE_TILE] = gates[k]
            for c in range(ROUTE_TILE // C):
                e_ref[t0 // C + c, k:k + 1, :] = idxs[k][:, c * C:(c + 1) * C]
                r_ref[t0 // C + c, k:k + 1, :] = ranks[k][:, c * C:(c + 1) * C]
    cnt_ref[...] = base_ref[...].astype(I32)


def _mid(u, attn_outs, xf, pool_bd, pool_scale, w_out_b, gate1, g_ffn, shift2, scale2,
         wr_t, bias_col, S):
    N, D = xf.shape
    E = wr_t.shape[0]
    tok = jnp.arange(ROUTE_TILE, dtype=I32)
    before = (tok[:, None] < tok[None, :]).astype(BF16)
    pw = u.shape[1]
    tm = MID_TILE
    spt = S // tm
    row = lambda i: (i, 0)
    vec = lambda i: (i // spt, 0, 0)
    full = lambda a: pl.BlockSpec(a.shape, lambda i: (0,) * a.ndim)
    hpt = tm // POOL_HALO
    in_specs = [pl.BlockSpec((tm, pw), row),
                pl.BlockSpec((POOL_HALO, pw), lambda i: (jnp.maximum(i * hpt - 1, 0), 0))]
    gw = ATT_GROUP_WIDTH
    in_specs += [pl.BlockSpec((tm, gw), row)] * 2
    for d in ATT_DILATIONS[1:]:
        in_specs += [pl.BlockSpec((d, tm // d, gw), lambda i: (i // spt, i % spt, 0))] * 2
    in_specs += [pl.BlockSpec((tm, D), row), full(pool_bd), full(pool_scale), full(w_out_b),
                 pl.BlockSpec((1, 1, D), vec), full(g_ffn), pl.BlockSpec((1, 1, D), vec),
                 pl.BlockSpec((1, 1, D), vec),
                 full(wr_t), full(bias_col), full(before)]
    col = lambda i: (0, i)
    C = SC_CHUNK
    chunked = pl.BlockSpec((tm // C, TOP_K, C), lambda i: (i, 0, 0))
    return pl.pallas_call(
        functools.partial(_mid_kernel, spt),
        grid=(N // tm,),
        in_specs=in_specs,
        out_specs=[pl.BlockSpec((tm, D), row), pl.BlockSpec((tm, D // 2), row),
                   chunked, pl.BlockSpec((TOP_K, tm), col), chunked,
                   pl.BlockSpec((E, 1), lambda i: (0, 0))],
        out_shape=[jax.ShapeDtypeStruct((N, D), F32), jax.ShapeDtypeStruct((N, D // 2), U32),
                   jax.ShapeDtypeStruct((N // C, TOP_K, C), I32), jax.ShapeDtypeStruct((TOP_K, N), F32),
                   jax.ShapeDtypeStruct((N // C, TOP_K, C), I32), jax.ShapeDtypeStruct((E, 1), I32)],
        scratch_shapes=[pltpu.VMEM((4, gw // LANES, tm, LANES), F32), pltpu.VMEM((gw // LANES, tm, LANES), F32),
                        pltpu.VMEM((E, 1), F32)],
        compiler_params=pltpu.CompilerParams(dimension_semantics=("arbitrary",)),
        name="mid",
    )(u, u, *attn_outs, xf, pool_bd, pool_scale, w_out_b, gate1, g_ffn, shift2, scale2,
      wr_t, bias_col, before)


def _route_tile(hb, wr_ref, bias_ref, before_ref, base_ref):
    T = hb.shape[0]
    E = wr_ref.shape[0]
    gsz = E // N_EXPERT_GROUPS
    logits = lax.dot_general(wr_ref[...], hb, (((1,), (1,)), ((), ())), preferred_element_type=F32)
    scores = _sigmoid(logits)
    biased = scores + bias_ref[...]
    giota = lax.broadcasted_iota(I32, (gsz, T), 0)
    gscore = []
    for g in range(N_EXPERT_GROUPS):
        blk = biased[g * gsz:(g + 1) * gsz, :]
        m1 = jnp.max(blk, axis=0, keepdims=True)
        i1 = jnp.min(jnp.where(blk == m1, giota, gsz), axis=0, keepdims=True)
        m2 = jnp.max(jnp.where(giota == i1, NEG_INF, blk), axis=0, keepdims=True)
        gscore.append(m1 + m2)
    parts = []
    for g in range(N_EXPERT_GROUPS):
        beaten = jnp.zeros((1, T), I32)
        for o in range(N_EXPERT_GROUPS):
            if o == g:
                continue
            wins = (gscore[o] >= gscore[g]) if o < g else (gscore[o] > gscore[g])
            beaten = beaten + wins.astype(I32)
        keep = jnp.broadcast_to(beaten, (gsz, T)) < TOPK_GROUPS
        parts.append(jnp.where(keep, biased[g * gsz:(g + 1) * gsz, :], NEG_INF))
    cur = jnp.concatenate(parts, axis=0)
    eiota = lax.broadcasted_iota(I32, (E, T), 0)
    live = cur > NEG_INF
    idxs, gates, picks = [], [], []
    sub = lax.broadcasted_iota(I32, (8, T), 0)
    for k in range(TOP_K):
        nodes = [(cur[8 * j:8 * (j + 1), :], j, scores[8 * j:8 * (j + 1), :]) for j in range(E // 8)]
        while len(nodes) > 1:
            merged = []
            for p in range(0, len(nodes), 2):
                (va, ta, sa), (vb, tb, sb) = nodes[p], nodes[p + 1]
                later = vb > va
                merged.append((jnp.where(later, vb, va), jnp.where(later, tb, ta), jnp.where(later, sb, sa)))
            nodes = merged
        v8, t8, s8 = nodes[0]
        e8 = t8 * 8 + sub
        m = jnp.max(v8, axis=0, keepdims=True)
        idx = jnp.min(jnp.where(v8 == m, e8, E), axis=0, keepdims=True)
        oh = eiota == idx
        won = e8 == idx
        gates.append(jnp.sum(jnp.where(won, s8, 0.0), axis=0, keepdims=True))
        idxs.append(idx)
        picks.append((t8, won))
        cur = jnp.where(oh, NEG_INF, cur)
    selm = jnp.where(live & (cur == NEG_INF), 1.0, 0.0)
    gsum = gates[0]
    for k in range(1, TOP_K):
        gsum = gsum + gates[k]
    gates = [gk / gsum * ROUTED_SCALE for gk in gates]
    tot = jnp.dot(selm.astype(BF16), before_ref[...], preferred_element_type=F32) + base_ref[...]
    tiles = [tot[8 * j:8 * (j + 1), :] for j in range(E // 8)]
    ranks = []
    for t8, won in picks:
        level, bit = tiles, 0
        while len(level) > 1:
            odd = ((t8 >> bit) & 1) == 1
            level = [jnp.where(odd, level[p + 1], level[p]) for p in range(0, len(level), 2)]
            bit += 1
        ranks.append(jnp.sum(jnp.where(won, level[0], 0.0), axis=0, keepdims=True).astype(I32))
    base_ref[...] = base_ref[...] + jnp.sum(selm, axis=1, keepdims=True)
    return idxs, gates, ranks


def _sc_dispatch(eidx_c, rank_c, offs, h2p, P):
    N, W = h2p.shape
    nch, K, C = eidx_c.shape
    E = offs.shape[0]
    info = plsc.get_sparse_core_info()
    nw = info.num_cores * info.num_subcores
    L = info.num_lanes
    assert nch % nw == 0, "token chunks must split evenly over the vector subcores"
    per_w = nch // nw
    mesh = plsc.VectorSubcoreMesh(core_axis_name="c", subcore_axis_name="s")

    @functools.partial(
        pl.kernel, mesh=mesh,
        out_type=[jax.ShapeDtypeStruct((P, W), h2p.dtype), jax.ShapeDtypeStruct((nch, K, C), I32)],
        scratch_types=[pltpu.VMEM((E,), I32), pltpu.VMEM((K, C), I32), pltpu.VMEM((K, C), I32),
                       pltpu.VMEM((K, C), I32), pltpu.VMEM((C, W), h2p.dtype), pltpu.SemaphoreType.DMA],
        compiler_params=pltpu.CompilerParams(needs_layout_passes=False),
        name="sc_dispatch",
    )
    def k(e_hbm, r_hbm, off_hbm, h_hbm, xs_hbm, dest_hbm, off_v, e_v, r_v, idx_v, rows_v, sem):
        wid = lax.axis_index("s") * info.num_cores + lax.axis_index("c")
        pltpu.sync_copy(off_hbm, off_v)

        @pl.loop(0, per_w)
        def _(j):
            ch = wid * per_w + j
            pltpu.sync_copy(e_hbm.at[ch], e_v)
            pltpu.sync_copy(r_hbm.at[ch], r_v)
            for kk in range(K):
                for q in range(C // L):
                    sl = pl.ds(q * L, L)
                    idx_v[kk, sl] = plsc.load_gather(off_v, [e_v[kk, sl]]) + r_v[kk, sl]
            pltpu.sync_copy(idx_v, dest_hbm.at[ch])
            pltpu.sync_copy(h_hbm.at[pl.ds(ch * C, C)], rows_v)
            copies = [pltpu.async_copy(rows_v, xs_hbm.at[idx_v.at[kk]], sem) for kk in range(K)]
            for cp in copies:
                cp.wait()

    return k(eidx_c, rank_c, offs, h2p)


def _gmm_kernel(bstart_ref, nbe_ref, cnt_ref, nu_ref, wg_ref, wu_ref, wd_ref, xs_hbm, ys_hbm,
                wgb, wub, wdb, xbuf, ybuf, xsem, ysem):
    step = pl.program_id(0)
    last = pl.num_programs(0) - 1
    epg = wg_ref.shape[0]
    ring, bm = xbuf.shape[0], xbuf.shape[1]
    nblk = ys_hbm.shape[0] // bm
    nused = nu_ref[0]

    def x_copy(b, slot):
        return pltpu.make_async_copy(xs_hbm.at[pl.ds(pl.multiple_of(b * bm, bm), bm), :], xbuf.at[slot],
                                     xsem.at[slot])

    def y_copy(b, slot):
        return pltpu.make_async_copy(ybuf.at[slot], ys_hbm.at[pl.ds(pl.multiple_of(b * bm, bm), bm), :],
                                     ysem.at[slot])

    @pl.when(step == 0)
    def _():
        for j in range(ring - 1):
            @pl.when(j < nused)
            def _():
                x_copy(j, j).start()

    def run_expert(ee):
        e = step * epg + ee
        b0 = bstart_ref[e]
        nb = nbe_ref[e]

        @pl.when(nb > 0)
        def _():
            def prefetch(t):
                @pl.when(t < nused)
                def _():
                    x_copy(t, jnp.bitwise_and(t, ring - 1)).start()

            def process(b, n, fresh):
                slots = [jnp.bitwise_and(b + j, ring - 1) for j in range(n)]
                for j in range(n):
                    x_copy(b + j, slots[j]).wait()
                prefetch(b + ring - 1)
                for j in range(n):
                    @pl.when(b + j >= ring)
                    def _():
                        y_copy(b + j - ring, slots[j]).wait()
                rows = lax.broadcasted_iota(I32, (n * bm, 1), 0)
                valid = cnt_ref[e] - (b - b0) * bm
                xp = jnp.concatenate([xbuf[s] for s in slots], axis=0)
                xb = jnp.concatenate(_unpack_bf16_pairs(jnp.where(rows < valid, xp, jnp.uint32(0))), axis=1)
                if fresh:
                    wg, wu, wd = (w[ee].astype(BF16) for w in (wg_ref, wu_ref, wd_ref))
                    wgb[...], wub[...], wdb[...] = wg, wu, wd
                else:
                    wg, wu, wd = wgb[...], wub[...], wdb[...]
                a = jnp.dot(xb, wg, preferred_element_type=F32)
                g = jnp.dot(xb, wu, preferred_element_type=F32)
                act = (a * _sigmoid(a)) * g
                yp = _pack_bf16_pairs(jnp.dot(act.astype(BF16), wd, preferred_element_type=F32))
                for j in range(n):
                    ybuf[slots[j]] = yp[j * bm:(j + 1) * bm, :]
                    y_copy(b + j, slots[j]).start()
                for j in range(1, n):
                    prefetch(b + ring - 1 + j)

            quads = lax.shift_right_logical(nb, 2)
            lax.fori_loop(0, quads, lambda j, c: (process(b0 + 4 * j, 4, True), c)[1], 0)
            rest2 = jnp.bitwise_and(nb, 2)
            rest1 = jnp.bitwise_and(nb, 1)
            for n, rest, first, have_copy in ((2, rest2, b0 + 4 * quads, quads > 0),
                                              (1, rest1, b0 + 4 * quads + rest2, nb > 1)):
                @pl.when((rest != 0) & have_copy)
                def _():
                    process(first, n, False)

                @pl.when((rest != 0) & jnp.logical_not(have_copy))
                def _():
                    process(first, n, True)

    for ee in range(epg):
        run_expert(ee)

    @pl.when(step == last)
    def _():
        for back in range(ring, 0, -1):
            @pl.when(nused >= back)
            def _():
                y_copy(nused - back, jnp.bitwise_and(nused - back, ring - 1)).wait()
        ybuf[0] = jnp.zeros(ybuf.shape[1:], ybuf.dtype)
        lax.fori_loop(nused, nblk, lambda b, c: (y_copy(b, 0).start(), c)[1], 0)
        lax.fori_loop(nused, nblk, lambda b, c: (y_copy(b, 0).wait(), c)[1], 0)


def _gmm(bstart, nb_e, counts, nused, xs, w_gate, w_up, w_down):
    P, W = xs.shape
    E, D, F = w_gate.shape
    bm = GMM_BLOCK
    epg = GMM_EXPERTS_PER_STEP
    wsel = lambda s, *_: (s, 0, 0)
    grid_spec = pltpu.PrefetchScalarGridSpec(
        num_scalar_prefetch=4,
        grid=(E // epg,),
        in_specs=[pl.BlockSpec((epg, D, F), wsel), pl.BlockSpec((epg, D, F), wsel), pl.BlockSpec((epg, F, D), wsel),
                  pl.BlockSpec(memory_space=pl.ANY)],
        out_specs=pl.BlockSpec(memory_space=pl.ANY),
        scratch_shapes=[pltpu.VMEM((D, F), BF16), pltpu.VMEM((D, F), BF16), pltpu.VMEM((F, D), BF16),
                        pltpu.VMEM((GMM_RING, bm, W), xs.dtype), pltpu.VMEM((GMM_RING, bm, W), xs.dtype),
                        pltpu.SemaphoreType.DMA((GMM_RING,)), pltpu.SemaphoreType.DMA((GMM_RING,))],
    )
    return pl.pallas_call(
        _gmm_kernel,
        grid_spec=grid_spec,
        out_shape=jax.ShapeDtypeStruct((P, W), xs.dtype),
        compiler_params=pltpu.CompilerParams(dimension_semantics=("arbitrary",)),
        name="gmm",
    )(bstart, nb_e, counts, nused, w_gate, w_up, w_down, xs)


def _sc_gather(dest_c, ys):
    nch, K, C = dest_c.shape
    W = ys.shape[1]
    H = SC_GATHER_ROWS
    info = plsc.get_sparse_core_info()
    nw = info.num_cores * info.num_subcores
    assert nch % nw == 0, "token chunks must split evenly over the vector subcores"
    per_w = nch // nw
    nbuf = SC_GATHER_BUFS
    ahead = nbuf - 1
    mesh = plsc.VectorSubcoreMesh(core_axis_name="c", subcore_axis_name="s")
    items = [(kk, hh) for kk in range(K) for hh in range(C // H)]

    @functools.partial(
        pl.kernel, mesh=mesh,
        out_type=jax.ShapeDtypeStruct((K, nch * C, W), ys.dtype),
        scratch_types=([pltpu.VMEM((K, C), I32)] + [pltpu.VMEM((H, W), ys.dtype)] * nbuf
                       + [pltpu.SemaphoreType.DMA] * (2 * nbuf)),
        name="sc_gather",
    )
    def k(dest_hbm, ys_hbm, yk_hbm, idx_v, *rest):
        bufs, gsem, wsem = rest[:nbuf], rest[nbuf:2 * nbuf], rest[2 * nbuf:]
        wid = lax.axis_index("s") * info.num_cores + lax.axis_index("c")

        @pl.loop(0, per_w)
        def _(j):
            ch = wid * per_w + j
            pltpu.sync_copy(dest_hbm.at[ch], idx_v)

            def gather(i):
                kk, hh = items[i]
                return pltpu.async_copy(ys_hbm.at[idx_v.at[kk, pl.ds(hh * H, H)]], bufs[i % nbuf], gsem[i % nbuf])

            def write(i):
                kk, hh = items[i]
                return pltpu.async_copy(bufs[i % nbuf], yk_hbm.at[kk, pl.ds(ch * C + hh * H, H)], wsem[i % nbuf])

            n = len(items)
            g = {i: gather(i) for i in range(ahead)}
            w = {}
            for i in range(n):
                g[i].wait()
                w[i] = write(i)
                if i + ahead < n:
                    if i >= 1:
                        w.pop(i - 1).wait()
                    g[i + ahead] = gather(i + ahead)
            for i in sorted(w):
                w[i].wait()

    return k(dest_c, ys)


def _combine_kernel(yk_ref, g_ref, gate2_ref, xacc_ref, gfin_ref, h2_ref, wsg_ref, wsu_ref, wsd_ref,
                    o_ref, gpad_ref):
    @pl.when(pl.program_id(0) == 0)
    def _():
        gpad_ref[...] = jnp.zeros_like(gpad_ref)

    hb = jnp.concatenate(_unpack_bf16_pairs(h2_ref[...]), axis=1)
    a = jnp.dot(hb, wsg_ref[...], preferred_element_type=F32)
    b = jnp.dot(hb, wsu_ref[...], preferred_element_type=F32)
    act = (a * _sigmoid(a)) * b
    shared = jnp.dot(act.astype(BF16), wsd_ref[...], preferred_element_type=F32)

    gpad_ref[0:TOP_K, :] = g_ref[...]
    gt = gpad_ref[...].T
    hi_mask = jnp.uint32(0xFFFF0000)
    acc_lo = acc_hi = None
    for k in range(TOP_K):
        p = yk_ref[k]
        g = gt[:, k:k + 1]
        lo = lax.bitcast_convert_type(p << 16, F32) * g
        hi = lax.bitcast_convert_type(p & hi_mask, F32) * g
        acc_lo = lo if k == 0 else acc_lo + lo
        acc_hi = hi if k == 0 else acc_hi + hi
    routed = jnp.concatenate([acc_lo, acc_hi], axis=1)
    x2 = (xacc_ref[...] + gate2_ref[0] * shared) + gate2_ref[0] * routed
    o_ref[...] = _rms(x2) * gfin_ref[...]


def _combine(yk, gates, gate2, xacc, g_final, h2, wsg_b, wsu_b, wsd_b, S):
    N, D = xacc.shape
    W = yk.shape[2]
    T = COMBINE_TILE
    spt = S // T
    full = lambda a: pl.BlockSpec(a.shape, lambda i: (0,) * a.ndim)
    return pl.pallas_call(
        _combine_kernel,
        grid=(N // T,),
        in_specs=[pl.BlockSpec((TOP_K, T, W), lambda i: (0, i, 0)),
                  pl.BlockSpec((TOP_K, T), lambda i: (0, i)),
                  pl.BlockSpec((1, 1, D), lambda i: (i // spt, 0, 0)),
                  pl.BlockSpec((T, D), lambda i: (i, 0)),
                  pl.BlockSpec((1, D), lambda i: (0, 0)),
                  pl.BlockSpec((T, D // 2), lambda i: (i, 0)), full(wsg_b), full(wsu_b), full(wsd_b)],
        out_specs=pl.BlockSpec((T, D), lambda i: (i, 0)),
        out_shape=jax.ShapeDtypeStruct((N, D), F32),
        scratch_shapes=[pltpu.VMEM((LANES, T), F32)],
        compiler_params=pltpu.CompilerParams(dimension_semantics=("arbitrary",),
                                             vmem_limit_bytes=48 * 1024 * 1024),
        name="combine",
    )(yk, gates, gate2, xacc, g_final, h2, wsg_b, wsu_b, wsd_b)


def _layer(xf, B, S, mod, g_mix, w_in, pool_w, pool_scale, w_out, g_ffn, w_router, router_bias,
           w_gate, w_up, w_down, ws_gate, ws_up, ws_down):
    N, D = xf.shape
    E = w_router.shape[1]
    pw = pool_scale.shape[0]
    shift1, scale1, gate1, shift2, scale2, gate2 = [m.reshape(B, 1, D) for m in jnp.split(mod, 6, axis=-1)]
    u, qkv = _in_proj(xf, g_mix.reshape(1, D), shift1, scale1, w_in.astype(BF16), S, pw)
    attn_outs = []
    for a, d in zip(qkv, ATT_DILATIONS):
        o, lse = _attention(a, S // d // ATT_BLOCK)
        shape = (N, ATT_GROUP_WIDTH) if d == 1 else (B * d, S // d, ATT_GROUP_WIDTH)
        attn_outs += [o.reshape(shape), lse.reshape(shape)]
    ng = pool_w.shape[0]
    pool_bd = jnp.einsum('gcd,gh->gchd', pool_w, jnp.eye(ng, dtype=pool_w.dtype)).reshape(pw, pw).astype(BF16)
    xacc, h2, eidx, gates, rank, counts = _mid(
        u, attn_outs, xf, pool_bd, pool_scale.reshape(1, pw), w_out.astype(BF16), gate1,
        g_ffn.reshape(1, D), shift2, scale2,
        w_router.T.astype(BF16), router_bias.reshape(E, 1).astype(F32), S)
    bm = GMM_BLOCK
    nblk = N * TOP_K // bm + E
    nb_e = (counts[:, 0] + bm - 1) // bm
    bend = jnp.cumsum(nb_e)
    bstart = (bend - nb_e).astype(I32)
    nused = bend[-1:].astype(I32)
    xs, dest_c = _sc_dispatch(eidx, rank, bstart * bm, h2, nblk * bm)
    ys = _gmm(bstart, nb_e.astype(I32), counts[:, 0], nused, xs, w_gate, w_up, w_down)
    shared_w = (ws_gate.astype(BF16), ws_up.astype(BF16), ws_down.astype(BF16))
    return _sc_gather(dest_c, ys), gates, gate2, xacc, h2, shared_w


def kernel(x, c, w_ada, b_ada, g_mix, w_in, pool_w, pool_scale, w_out, g_ffn, w_router, router_bias,
           w_gate, w_up, w_down, ws_gate, ws_up, ws_down, g_final):
    B, S, D = x.shape
    depth = w_ada.shape[0]
    assert depth == 1, "the final residual is fused with the final norm, so exactly one layer is supported"
    assert S % (ATT_DILATIONS[-1] * ATT_BLOCK) == 0
    assert all(S % t == 0 for t in (IN_TILE, MID_TILE, COMBINE_TILE, ATT_BLOCKS_PER_STEP * ATT_BLOCK))
    assert MID_TILE % ROUTE_TILE == 0 and ROUTE_TILE % SC_CHUNK == 0 and SC_CHUNK % SC_GATHER_ROWS == 0
    assert w_gate.shape[1] % GMM_EXPERTS_PER_STEP == 0 and max(POOL_WINDOWS) <= POOL_HALO
    xf = x.reshape(B * S, D)
    mod = _ada(c, w_ada[0], b_ada[0])
    yk, gates, gate2, xacc, h2, shared_w = _layer(
        xf, B, S, mod, g_mix[0], w_in[0], pool_w[0], pool_scale[0], w_out[0], g_ffn[0], w_router[0],
        router_bias[0], w_gate[0], w_up[0], w_down[0], ws_gate[0], ws_up[0], ws_down[0])
    out = _combine(yk, gates, gate2, xacc, g_final.reshape(1, D), h2, *shared_w, S)
    return out.reshape(B, S, D)
```
